```python
import math, functools
import jax, jax.numpy as jnp
from jax import lax
import numpy as np

D_MODEL = 1024
BATCH = 8
SEQ = 8192
DEPTH = 1

N_META = 16
CHUNK = 64
GDN_HEADS = 8
GDN_DK = 128
GDN_DV = 128
RET_HEADS = 8
RET_DK = 128
RET_DV = 128
CONV_K = 4
D_FF = 2816
ROPE_BASE = 10000.0
EPS = 1e-6

GDN_QK = GDN_HEADS * GDN_DK
GDN_V = GDN_HEADS * GDN_DV
GDN_CONV = 2 * GDN_QK + GDN_V
RET_QK = RET_HEADS * RET_DK
RET_V = RET_HEADS * RET_DV
PROJ_SIZES = (GDN_CONV, GDN_V, GDN_HEADS, GDN_HEADS, RET_QK, RET_QK, RET_V, RET_V, D_MODEL, D_MODEL)
D_PROJ = sum(PROJ_SIZES)

kernel_name = "hybrid_gdn_retention_macaron_layer"


def rms_norm(x, w):
    xf = x.astype(jnp.float32)
    y = xf * lax.rsqrt(jnp.mean(xf * xf, axis=-1, keepdims=True) + EPS)
    return (y * w.astype(jnp.float32)).astype(x.dtype)


def swiglu(x, w_in, w_out):
    gate, up = jnp.split(x @ w_in, 2, axis=-1)
    return (jax.nn.silu(gate) * up) @ w_out


def causal_depthwise_conv(x, w):
    c = x.shape[-1]
    return lax.conv_general_dilated(
        x, w[:, None, :].astype(x.dtype), window_strides=(1,), padding=[(CONV_K - 1, 0)],
        dimension_numbers=("NWC", "WIO", "NWC"), feature_group_count=c)


def to_heads(t, n_heads):
    b, l, _ = t.shape
    return t.reshape(b, l, n_heads, -1).transpose(0, 2, 1, 3).astype(jnp.float32)


def l2norm(t):
    return t * lax.rsqrt(jnp.sum(t * t, axis=-1, keepdims=True) + EPS)


def rotary(t, pos):
    d = t.shape[-1]
    inv = 1.0 / (ROPE_BASE ** jnp.linspace(0.0, 1.0, d // 2, dtype=jnp.float32))
    ang = pos[:, None] * inv[None, :]
    cos, sin = jnp.cos(ang), jnp.sin(ang)
    tp = t.reshape(*t.shape[:-1], d // 2, 2)
    t0, t1 = tp[..., 0], tp[..., 1]
    return jnp.stack([t0 * cos - t1 * sin, t1 * cos + t0 * sin], axis=-1).reshape(t.shape)


def gdn_chunk_scan(q, k, v, g, beta, state, chunk):
    b, h, l, dk = q.shape
    n = l // chunk
    split = lambda t: t.reshape(b, h, n, chunk, *t.shape[3:])
    q, k, v, g, beta = split(q), split(k), split(v), split(g), split(beta)
    g = jnp.cumsum(g, axis=-1)
    causal = jnp.tril(jnp.ones((chunk, chunk), dtype=bool))
    strict = jnp.tril(jnp.ones((chunk, chunk), dtype=bool), -1)
    diff = g[..., :, None] - g[..., None, :]
    decay = jnp.where(causal, jnp.exp(jnp.where(causal, diff, 0.0)), 0.0)
    k_beta = k * beta[..., None]
    a = jnp.where(strict, jnp.einsum("bhncd,bhnmd->bhncm", k_beta, k) * decay, 0.0) + jnp.eye(chunk, dtype=q.dtype)
    solve = functools.partial(lax.linalg.triangular_solve, left_side=True, lower=True)
    u = solve(a, v * beta[..., None])
    w = solve(a, k_beta * jnp.exp(g)[..., None])
    qk = jnp.einsum("bhncd,bhnmd->bhncm", q, k) * decay
    g_last = g[..., -1]
    q_dec = q * jnp.exp(g)[..., None]
    k_dec = k * jnp.exp(g_last[..., None] - g)[..., None]

    def step(s, xs):
        qk_c, u_c, w_c, qd_c, kd_c, gl_c = xs
        v_new = u_c - jnp.einsum("bhck,bhkv->bhcv", w_c, s)
        o = jnp.einsum("bhck,bhkv->bhcv", qd_c, s) + jnp.einsum("bhcm,bhmv->bhcv", qk_c, v_new)
        s = s * jnp.exp(gl_c)[..., None, None] + jnp.einsum("bhck,bhcv->bhkv", kd_c, v_new)
        return s, o

    xs = tuple(jnp.moveaxis(t, 2, 0) for t in (qk, u, w, q_dec, k_dec, g_last))
    state, o = lax.scan(step, state, xs)
    return jnp.moveaxis(o, 0, 2).reshape(b, h, l, -1), state


def retention_chunk_scan(q, k, v, log_gamma, state, chunk):
    b, h, l, dk = q.shape
    n = l // chunk
    split = lambda t: t.reshape(b, h, n, chunk, t.shape[-1])
    q, k, v = split(q), split(k), split(v)
    pos = jnp.arange(chunk, dtype=jnp.float32)
    lg = log_gamma[:, None]
    causal = jnp.tril(jnp.ones((chunk, chunk), dtype=bool))
    diff = pos[:, None] - pos[None, :]
    decay = jnp.where(causal, jnp.exp(jnp.where(causal, diff, 0.0) * log_gamma[:, None, None]), 0.0)
    scores = jnp.einsum("bhncd,bhnmd->bhncm", q, k) * decay[None, :, None]
    intra = jnp.einsum("bhncm,bhnmv->bhncv", scores, v)
    q_dec = q * jnp.exp((pos + 1.0) * lg)[None, :, None, :, None]
    k_dec = k * jnp.exp((chunk - 1.0 - pos) * lg)[None, :, None, :, None]
    chunk_decay = jnp.exp(chunk * log_gamma)[None, :, None, None]

    def step(s, xs):
        qd, kd, vc = xs
        o = jnp.einsum("bhck,bhkv->bhcv", qd, s)
        s = s * chunk_decay + jnp.einsum("bhck,bhcv->bhkv", kd, vc)
        return s, o

    xs = tuple(jnp.moveaxis(t, 2, 0) for t in (q_dec, k_dec, v))
    state, inter = lax.scan(step, state, xs)
    o = intra + jnp.moveaxis(inter, 0, 2)
    return o.reshape(b, h, l, -1), state


def hybrid_mixer(n, w_in, conv_w, a_log, dt_bias, gdn_norm, ret_norm, w_br_gdn, w_br_ret, w_out):
    b, l, _ = n.shape
    f32 = jnp.float32
    offs = [int(i) for i in np.cumsum(PROJ_SIZES)[:-1]]
    qkv, z, b_raw, a_raw, rq, rk, rv, rg, ga, gb = jnp.split(n @ w_in, offs, axis=-1)

    qkv = jax.nn.silu(causal_depthwise_conv(qkv, conv_w))
    q, k, v = jnp.split(qkv, [GDN_QK, 2 * GDN_QK], axis=-1)
    q = l2norm(to_heads(q, GDN_HEADS)) * (GDN_DK ** -0.5)
    k = l2norm(to_heads(k, GDN_HEADS))
    v = to_heads(v, GDN_HEADS)
    g = (-jnp.exp(a_log.astype(f32)) * jax.nn.softplus(a_raw.astype(f32) + dt_bias.astype(f32))).transpose(0, 2, 1)
    beta = jax.nn.sigmoid(b_raw.astype(f32)).transpose(0, 2, 1)
    s0 = jnp.zeros((b, GDN_HEADS, GDN_DK, GDN_DV), f32)
    o_m, s_m = gdn_chunk_scan(q[:, :, :N_META], k[:, :, :N_META], v[:, :, :N_META],
                              g[:, :, :N_META], beta[:, :, :N_META], s0, N_META)
    o_r, _ = gdn_chunk_scan(q[:, :, N_META:], k[:, :, N_META:], v[:, :, N_META:],
                            g[:, :, N_META:], beta[:, :, N_META:], s_m, CHUNK)
    o_a = jnp.concatenate([o_m, o_r], axis=2).transpose(0, 2, 1, 3)
    o_a = o_a * lax.rsqrt(jnp.mean(o_a * o_a, axis=-1, keepdims=True) + EPS) * gdn_norm.astype(f32)
    y_a = (o_a * jax.nn.silu(z.reshape(b, l, GDN_HEADS, GDN_DV).astype(f32))).reshape(b, l, GDN_V).astype(n.dtype)

    pos = jnp.arange(l, dtype=f32)
    rq = rotary(to_heads(rq, RET_HEADS), pos)
    rk = rotary(to_heads(rk, RET_HEADS), pos) * (RET_DK ** -0.5)
    rv = to_heads(rv, RET_HEADS)
    log_gamma = jnp.log1p(-jnp.exp2(-5.0 - jnp.arange(RET_HEADS, dtype=f32)))
    r0 = jnp.zeros((b, RET_HEADS, RET_DK, RET_DV), f32)
    p_m, r_m = retention_chunk_scan(rq[:, :, :N_META], rk[:, :, :N_META], rv[:, :, :N_META], log_gamma, r0, N_META)
    p_r, _ = retention_chunk_scan(rq[:, :, N_META:], rk[:, :, N_META:], rv[:, :, N_META:], log_gamma, r_m, CHUNK)
    o_b = jnp.concatenate([p_m, p_r], axis=2).transpose(0, 2, 1, 3)
    mu = jnp.mean(o_b, axis=-1, keepdims=True)
    var = jnp.mean(jnp.square(o_b - mu), axis=-1, keepdims=True)
    o_b = ((o_b - mu) * lax.rsqrt(var + EPS)).reshape(b, l, RET_V) * ret_norm.astype(f32)
    y_b = (jax.nn.silu(rg.astype(f32)) * o_b).astype(n.dtype)

    merged = jax.nn.sigmoid(ga) * (y_a @ w_br_gdn) + jax.nn.sigmoid(gb) * (y_b @ w_br_ret)
    return merged @ w_out


def _fwd_setup_inputs(seed: int = 0) -> dict:
    key = jax.random.key(seed)
    ks = jax.random.split(key, 20)
    f32 = jnp.float32
    nrm = lambda k, shape, scale: jax.random.normal(k, shape, f32) * scale
    gain = lambda k, shape: 1.0 + 0.02 * jax.random.normal(k, shape, f32)
    dt = jnp.exp(jax.random.uniform(ks[8], (DEPTH, GDN_HEADS), f32, math.log(1e-3), math.log(1e-1)))
    return {
        "x": nrm(ks[0], (BATCH, SEQ, D_MODEL), 1.0),
        "meta_tokens": nrm(ks[1], (N_META, D_MODEL), 1.0),
        "ffn1_norm": gain(ks[2], (DEPTH, D_MODEL)),
        "ffn1_w_in": nrm(ks[3], (DEPTH, D_MODEL, 2 * D_FF), D_MODEL ** -0.5),
        "ffn1_w_out": nrm(ks[4], (DEPTH, D_FF, D_MODEL), D_FF ** -0.5),
        "mix_norm": gain(ks[5], (DEPTH, D_MODEL)),
        "w_in": nrm(ks[6], (DEPTH, D_MODEL, D_PROJ), D_MODEL ** -0.5),
        "gdn_conv_w": nrm(ks[7], (DEPTH, CONV_K, GDN_CONV), CONV_K ** -0.5),
        "gdn_a_log": jnp.log(jax.random.uniform(ks[9], (DEPTH, GDN_HEADS), f32, 1.0, 16.0)),
        "gdn_dt_bias": dt + jnp.log(-jnp.expm1(-dt)),
        "gdn_out_norm": gain(ks[10], (DEPTH, GDN_DV)),
        "ret_out_norm": gain(ks[11], (DEPTH, RET_V)),
        "w_branch_gdn": nrm(ks[12], (DEPTH, GDN_V, D_MODEL), GDN_V ** -0.5),
        "w_branch_ret": nrm(ks[13], (DEPTH, RET_V, D_MODEL), RET_V ** -0.5),
        "w_out": nrm(ks[14], (DEPTH, D_MODEL, D_MODEL), D_MODEL ** -0.5),
        "ffn2_norm": gain(ks[15], (DEPTH, D_MODEL)),
        "ffn2_w_in": nrm(ks[16], (DEPTH, D_MODEL, 2 * D_FF), D_MODEL ** -0.5),
        "ffn2_w_out": nrm(ks[17], (DEPTH, D_FF, D_MODEL), D_FF ** -0.5),
        "final_norm": gain(ks[18], (D_MODEL,)),
    }


def _fwd_reference(x, meta_tokens, ffn1_norm, ffn1_w_in, ffn1_w_out, mix_norm, w_in, gdn_conv_w,
              gdn_a_log, gdn_dt_bias, gdn_out_norm, ret_out_norm, w_branch_gdn, w_branch_ret,
              w_out, ffn2_norm, ffn2_w_in, ffn2_w_out, final_norm):
    b = x.shape[0]
    meta = jnp.broadcast_to(meta_tokens[None].astype(x.dtype), (b, N_META, D_MODEL))
    h = jnp.concatenate([meta, x], axis=1)
    for i in range(DEPTH):
        h = h + 0.5 * swiglu(rms_norm(h, ffn1_norm[i]), ffn1_w_in[i], ffn1_w_out[i])
        h = h + hybrid_mixer(rms_norm(h, mix_norm[i]), w_in[i], gdn_conv_w[i], gdn_a_log[i],
                             gdn_dt_bias[i], gdn_out_norm[i], ret_out_norm[i],
                             w_branch_gdn[i], w_branch_ret[i], w_out[i])
        h = h + 0.5 * swiglu(rms_norm(h, ffn2_norm[i]), ffn2_w_in[i], ffn2_w_out[i])
    return rms_norm(h, final_norm)[:, N_META:]


import jax as _jax
import jax.numpy as _jnp

TWIN_FORMAT = 'train_step'
FWD_PARAMS = ['x', 'meta_tokens', 'ffn1_norm', 'ffn1_w_in', 'ffn1_w_out', 'mix_norm', 'w_in', 'gdn_conv_w', 'gdn_a_log', 'gdn_dt_bias', 'gdn_out_norm', 'ret_out_norm', 'w_branch_gdn', 'w_branch_ret', 'w_out', 'ffn2_norm', 'ffn2_w_in', 'ffn2_w_out', 'final_norm']
TWIN_WEIGHTS = ['meta_tokens', 'ffn1_norm', 'ffn1_w_in', 'ffn1_w_out', 'mix_norm', 'w_in', 'gdn_conv_w', 'gdn_a_log', 'gdn_dt_bias', 'gdn_out_norm', 'ret_out_norm', 'w_branch_gdn', 'w_branch_ret', 'w_out', 'ffn2_norm', 'ffn2_w_in', 'ffn2_w_out', 'final_norm']
TWIN_DIFF_INPUT = 'x'
TWIN_INPUTS = ['x', 'meta_tokens', 'ffn1_norm', 'ffn1_w_in', 'ffn1_w_out', 'mix_norm', 'w_in', 'gdn_conv_w', 'gdn_a_log', 'gdn_dt_bias', 'gdn_out_norm', 'ret_out_norm', 'w_branch_gdn', 'w_branch_ret', 'w_out', 'ffn2_norm', 'ffn2_w_in', 'ffn2_w_out', 'final_norm', 'loss_target', 'm_meta_tokens', 'm_ffn1_norm', 'm_ffn1_w_in', 'm_ffn1_w_out', 'm_mix_norm', 'm_w_in', 'm_gdn_conv_w', 'm_gdn_a_log', 'm_gdn_dt_bias', 'm_gdn_out_norm', 'm_ret_out_norm', 'm_w_branch_gdn', 'm_w_branch_ret', 'm_w_out', 'm_ffn2_norm', 'm_ffn2_w_in', 'm_ffn2_w_out', 'm_final_norm', 'v_meta_tokens', 'v_ffn1_norm', 'v_ffn1_w_in', 'v_ffn1_w_out', 'v_mix_norm', 'v_w_in', 'v_gdn_conv_w', 'v_gdn_a_log', 'v_gdn_dt_bias', 'v_gdn_out_norm', 'v_ret_out_norm', 'v_w_branch_gdn', 'v_w_branch_ret', 'v_w_out', 'v_ffn2_norm', 'v_ffn2_w_in', 'v_ffn2_w_out', 'v_final_norm']
TWIN_OUTPUTS = ['loss', 'grad_x', 'grad_meta_tokens', 'grad_ffn1_norm', 'grad_ffn1_w_in', 'grad_ffn1_w_out', 'grad_mix_norm', 'grad_w_in', 'grad_gdn_conv_w', 'grad_gdn_a_log', 'grad_gdn_dt_bias', 'grad_gdn_out_norm', 'grad_ret_out_norm', 'grad_w_branch_gdn', 'grad_w_branch_ret', 'grad_w_out', 'grad_ffn2_norm', 'grad_ffn2_w_in', 'grad_ffn2_w_out', 'grad_final_norm', 'delta_meta_tokens', 'delta_ffn1_norm', 'delta_ffn1_w_in', 'delta_ffn1_w_out', 'delta_mix_norm', 'delta_w_in', 'delta_gdn_conv_w', 'delta_gdn_a_log', 'delta_gdn_dt_bias', 'delta_gdn_out_norm', 'delta_ret_out_norm', 'delta_w_branch_gdn', 'delta_w_branch_ret', 'delta_w_out', 'delta_ffn2_norm', 'delta_ffn2_w_in', 'delta_ffn2_w_out', 'delta_final_norm', 'new_m_meta_tokens', 'new_m_ffn1_norm', 'new_m_ffn1_w_in', 'new_m_ffn1_w_out', 'new_m_mix_norm', 'new_m_w_in', 'new_m_gdn_conv_w', 'new_m_gdn_a_log', 'new_m_gdn_dt_bias', 'new_m_gdn_out_norm', 'new_m_ret_out_norm', 'new_m_w_branch_gdn', 'new_m_w_branch_ret', 'new_m_w_out', 'new_m_ffn2_norm', 'new_m_ffn2_w_in', 'new_m_ffn2_w_out', 'new_m_final_norm', 'new_v_meta_tokens', 'new_v_ffn1_norm', 'new_v_ffn1_w_in', 'new_v_ffn1_w_out', 'new_v_mix_norm', 'new_v_w_in', 'new_v_gdn_conv_w', 'new_v_gdn_a_log', 'new_v_gdn_dt_bias', 'new_v_gdn_out_norm', 'new_v_ret_out_norm', 'new_v_w_branch_gdn', 'new_v_w_branch_ret', 'new_v_w_out', 'new_v_ffn2_norm', 'new_v_ffn2_w_in', 'new_v_ffn2_w_out', 'new_v_final_norm']
TWIN_LEAF_KINDS = {'loss': 'loss', 'grad_x': 'grad_x', 'grad_meta_tokens': 'grad_w', 'grad_ffn1_norm': 'grad_w', 'grad_ffn1_w_in': 'grad_w', 'grad_ffn1_w_out': 'grad_w', 'grad_mix_norm': 'grad_w', 'grad_w_in': 'grad_w', 'grad_gdn_conv_w': 'grad_w', 'grad_gdn_a_log': 'grad_w', 'grad_gdn_dt_bias': 'grad_w', 'grad_gdn_out_norm': 'grad_w', 'grad_ret_out_norm': 'grad_w', 'grad_w_branch_gdn': 'grad_w', 'grad_w_branch_ret': 'grad_w', 'grad_w_out': 'grad_w', 'grad_ffn2_norm': 'grad_w', 'grad_ffn2_w_in': 'grad_w', 'grad_ffn2_w_out': 'grad_w', 'grad_final_norm': 'grad_w', 'delta_meta_tokens': 'delta_w', 'delta_ffn1_norm': 'delta_w', 'delta_ffn1_w_in': 'delta_w', 'delta_ffn1_w_out': 'delta_w', 'delta_mix_norm': 'delta_w', 'delta_w_in': 'delta_w', 'delta_gdn_conv_w': 'delta_w', 'delta_gdn_a_log': 'delta_w', 'delta_gdn_dt_bias': 'delta_w', 'delta_gdn_out_norm': 'delta_w', 'delta_ret_out_norm': 'delta_w', 'delta_w_branch_gdn': 'delta_w', 'delta_w_branch_ret': 'delta_w', 'delta_w_out': 'delta_w', 'delta_ffn2_norm': 'delta_w', 'delta_ffn2_w_in': 'delta_w', 'delta_ffn2_w_out': 'delta_w', 'delta_final_norm': 'delta_w', 'new_m_meta_tokens': 'new_m', 'new_m_ffn1_norm': 'new_m', 'new_m_ffn1_w_in': 'new_m', 'new_m_ffn1_w_out': 'new_m', 'new_m_mix_norm': 'new_m', 'new_m_w_in': 'new_m', 'new_m_gdn_conv_w': 'new_m', 'new_m_gdn_a_log': 'new_m', 'new_m_gdn_dt_bias': 'new_m', 'new_m_gdn_out_norm': 'new_m', 'new_m_ret_out_norm': 'new_m', 'new_m_w_branch_gdn': 'new_m', 'new_m_w_branch_ret': 'new_m', 'new_m_w_out': 'new_m', 'new_m_ffn2_norm': 'new_m', 'new_m_ffn2_w_in': 'new_m', 'new_m_ffn2_w_out': 'new_m', 'new_m_final_norm': 'new_m', 'new_v_meta_tokens': 'new_v', 'new_v_ffn1_norm': 'new_v', 'new_v_ffn1_w_in': 'new_v', 'new_v_ffn1_w_out': 'new_v', 'new_v_mix_norm': 'new_v', 'new_v_w_in': 'new_v', 'new_v_gdn_conv_w': 'new_v', 'new_v_gdn_a_log': 'new_v', 'new_v_gdn_dt_bias': 'new_v', 'new_v_gdn_out_norm': 'new_v', 'new_v_ret_out_norm': 'new_v', 'new_v_w_branch_gdn': 'new_v', 'new_v_w_branch_ret': 'new_v', 'new_v_w_out': 'new_v', 'new_v_ffn2_norm': 'new_v', 'new_v_ffn2_w_in': 'new_v', 'new_v_ffn2_w_out': 'new_v', 'new_v_final_norm': 'new_v'}


def _forward(args):
    return _fwd_reference(*[args[k] for k in FWD_PARAMS])


def _output_shape():
    out = _jax.eval_shape(lambda: _forward(_fwd_setup_inputs(0)))
    return out.shape, out.dtype

N_MICROBATCH = 1
ADAM_LR = 0.001
ADAM_B1 = 0.9
ADAM_B2 = 0.999
ADAM_EPS = 1e-08
ADAM_WD = 0.01
ADAM_STEP = 10
PER_EXAMPLE_BATCH_AXIS = {'x': 0, 'loss_target': 0}
SHARED_INPUTS = []
_WEIGHT_DTYPES = {'meta_tokens': _jnp.float32, 'ffn1_norm': _jnp.float32, 'ffn1_w_in': _jnp.float32, 'ffn1_w_out': _jnp.float32, 'mix_norm': _jnp.float32, 'w_in': _jnp.float32, 'gdn_conv_w': _jnp.float32, 'gdn_a_log': _jnp.float32, 'gdn_dt_bias': _jnp.float32, 'gdn_out_norm': _jnp.float32, 'ret_out_norm': _jnp.float32, 'w_branch_gdn': _jnp.float32, 'w_branch_ret': _jnp.float32, 'w_out': _jnp.float32, 'ffn2_norm': _jnp.float32, 'ffn2_w_in': _jnp.float32, 'ffn2_w_out': _jnp.float32, 'final_norm': _jnp.float32}
MOMENT_SCALE = {'meta_tokens': 1.053190e-02, 'ffn1_norm': 1.244797e-01, 'ffn1_w_in': 5.385159e-02, 'ffn1_w_out': 8.787777e-02, 'mix_norm': 1.982230e-01, 'w_in': 6.231006e-02, 'gdn_conv_w': 5.587816e-02, 'gdn_a_log': 2.942478e-01, 'gdn_dt_bias': 2.901509e-01, 'gdn_out_norm': 2.482255e-01, 'ret_out_norm': 7.921121e-02, 'w_branch_gdn': 7.238670e-02, 'w_branch_ret': 7.377744e-02, 'w_out': 1.032417e-01, 'ffn2_norm': 9.969741e-02, 'ffn2_w_in': 3.899277e-02, 'ffn2_w_out': 6.352174e-02, 'final_norm': 6.395090e+01}


def _to_microbatches(a, axis):
    t = _jnp.moveaxis(a, axis, 0)
    t = t.reshape((N_MICROBATCH, t.shape[0] // N_MICROBATCH) + t.shape[1:])
    return _jnp.moveaxis(t, 1, axis + 1)


def setup_inputs(seed: int = 0) -> dict:
    inp = _fwd_setup_inputs(seed)
    key = _jax.random.fold_in(_jax.random.key(seed), 7919)
    shape, _ = _output_shape()
    out = dict(inp)
    out["loss_target"] = _jax.random.normal(_jax.random.fold_in(key, 0), shape, _jnp.float32)
    for i, name in enumerate(TWIN_WEIGHTS):
        w = inp[name].astype(_jnp.float32)
        if MOMENT_SCALE is None:
            s = _jnp.sqrt(_jnp.mean(_jnp.square(w)) + 1e-30)
        else:
            s = MOMENT_SCALE[name]
        km, kv = _jax.random.split(_jax.random.fold_in(key, i + 1))
        out[name] = w
        out["m_" + name] = s * _jax.random.normal(km, w.shape, _jnp.float32)
        out["v_" + name] = (s * s) * _jax.random.uniform(kv, w.shape, _jnp.float32, 0.5, 1.5)
    if N_MICROBATCH > 1:
        for name, axis in PER_EXAMPLE_BATCH_AXIS.items():
            out[name] = _to_microbatches(out[name], axis)
    return {'x': out['x'], 'meta_tokens': out['meta_tokens'], 'ffn1_norm': out['ffn1_norm'], 'ffn1_w_in': out['ffn1_w_in'], 'ffn1_w_out': out['ffn1_w_out'], 'mix_norm': out['mix_norm'], 'w_in': out['w_in'], 'gdn_conv_w': out['gdn_conv_w'], 'gdn_a_log': out['gdn_a_log'], 'gdn_dt_bias': out['gdn_dt_bias'], 'gdn_out_norm': out['gdn_out_norm'], 'ret_out_norm': out['ret_out_norm'], 'w_branch_gdn': out['w_branch_gdn'], 'w_branch_ret': out['w_branch_ret'], 'w_out': out['w_out'], 'ffn2_norm': out['ffn2_norm'], 'ffn2_w_in': out['ffn2_w_in'], 'ffn2_w_out': out['ffn2_w_out'], 'final_norm': out['final_norm'], 'loss_target': out['loss_target'], 'm_meta_tokens': out['m_meta_tokens'], 'm_ffn1_norm': out['m_ffn1_norm'], 'm_ffn1_w_in': out['m_ffn1_w_in'], 'm_ffn1_w_out': out['m_ffn1_w_out'], 'm_mix_norm': out['m_mix_norm'], 'm_w_in': out['m_w_in'], 'm_gdn_conv_w': out['m_gdn_conv_w'], 'm_gdn_a_log': out['m_gdn_a_log'], 'm_gdn_dt_bias': out['m_gdn_dt_bias'], 'm_gdn_out_norm': out['m_gdn_out_norm'], 'm_ret_out_norm': out['m_ret_out_norm'], 'm_w_branch_gdn': out['m_w_branch_gdn'], 'm_w_branch_ret': out['m_w_branch_ret'], 'm_w_out': out['m_w_out'], 'm_ffn2_norm': out['m_ffn2_norm'], 'm_ffn2_w_in': out['m_ffn2_w_in'], 'm_ffn2_w_out': out['m_ffn2_w_out'], 'm_final_norm': out['m_final_norm'], 'v_meta_tokens': out['v_meta_tokens'], 'v_ffn1_norm': out['v_ffn1_norm'], 'v_ffn1_w_in': out['v_ffn1_w_in'], 'v_ffn1_w_out': out['v_ffn1_w_out'], 'v_mix_norm': out['v_mix_norm'], 'v_w_in': out['v_w_in'], 'v_gdn_conv_w': out['v_gdn_conv_w'], 'v_gdn_a_log': out['v_gdn_a_log'], 'v_gdn_dt_bias': out['v_gdn_dt_bias'], 'v_gdn_out_norm': out['v_gdn_out_norm'], 'v_ret_out_norm': out['v_ret_out_norm'], 'v_w_branch_gdn': out['v_w_branch_gdn'], 'v_w_branch_ret': out['v_w_branch_ret'], 'v_w_out': out['v_w_out'], 'v_ffn2_norm': out['v_ffn2_norm'], 'v_ffn2_w_in': out['v_ffn2_w_in'], 'v_ffn2_w_out': out['v_ffn2_w_out'], 'v_final_norm': out['v_final_norm']}


def _loss(weights, diff, rest, loss_target):
    with _jax.named_scope("forward"):
        args = {**rest, TWIN_DIFF_INPUT: diff, **{k: w.astype(_WEIGHT_DTYPES[k]) for k, w in weights.items()}}
        y = _forward(args)
    with _jax.named_scope("loss_head"):
        err = _jnp.square(y.astype(_jnp.float32) - loss_target)
        return 0.5 * _jnp.sum(_jnp.mean(err, axis=-1)) if err.ndim else 0.5 * err


def _adamw(w, g, m, v):
    m = ADAM_B1 * m + (1.0 - ADAM_B1) * g
    v = ADAM_B2 * v + (1.0 - ADAM_B2) * _jnp.square(g)
    m_hat = m / (1.0 - ADAM_B1 ** ADAM_STEP)
    v_hat = v / (1.0 - ADAM_B2 ** ADAM_STEP)
    delta = -ADAM_LR * (m_hat / (_jnp.sqrt(v_hat) + ADAM_EPS) + ADAM_WD * w)
    return delta, m, v


def reference(x, meta_tokens, ffn1_norm, ffn1_w_in, ffn1_w_out, mix_norm, w_in, gdn_conv_w, gdn_a_log, gdn_dt_bias, gdn_out_norm, ret_out_norm, w_branch_gdn, w_branch_ret, w_out, ffn2_norm, ffn2_w_in, ffn2_w_out, final_norm, loss_target, m_meta_tokens, m_ffn1_norm, m_ffn1_w_in, m_ffn1_w_out, m_mix_norm, m_w_in, m_gdn_conv_w, m_gdn_a_log, m_gdn_dt_bias, m_gdn_out_norm, m_ret_out_norm, m_w_branch_gdn, m_w_branch_ret, m_w_out, m_ffn2_norm, m_ffn2_w_in, m_ffn2_w_out, m_final_norm, v_meta_tokens, v_ffn1_norm, v_ffn1_w_in, v_ffn1_w_out, v_mix_norm, v_w_in, v_gdn_conv_w, v_gdn_a_log, v_gdn_dt_bias, v_gdn_out_norm, v_ret_out_norm, v_w_branch_gdn, v_w_branch_ret, v_w_out, v_ffn2_norm, v_ffn2_w_in, v_ffn2_w_out, v_final_norm):
    given = dict(x=x, meta_tokens=meta_tokens, ffn1_norm=ffn1_norm, ffn1_w_in=ffn1_w_in, ffn1_w_out=ffn1_w_out, mix_norm=mix_norm, w_in=w_in, gdn_conv_w=gdn_conv_w, gdn_a_log=gdn_a_log, gdn_dt_bias=gdn_dt_bias, gdn_out_norm=gdn_out_norm, ret_out_norm=ret_out_norm, w_branch_gdn=w_branch_gdn, w_branch_ret=w_branch_ret, w_out=w_out, ffn2_norm=ffn2_norm, ffn2_w_in=ffn2_w_in, ffn2_w_out=ffn2_w_out, final_norm=final_norm, loss_target=loss_target, m_meta_tokens=m_meta_tokens, m_ffn1_norm=m_ffn1_norm, m_ffn1_w_in=m_ffn1_w_in, m_ffn1_w_out=m_ffn1_w_out, m_mix_norm=m_mix_norm, m_w_in=m_w_in, m_gdn_conv_w=m_gdn_conv_w, m_gdn_a_log=m_gdn_a_log, m_gdn_dt_bias=m_gdn_dt_bias, m_gdn_out_norm=m_gdn_out_norm, m_ret_out_norm=m_ret_out_norm, m_w_branch_gdn=m_w_branch_gdn, m_w_branch_ret=m_w_branch_ret, m_w_out=m_w_out, m_ffn2_norm=m_ffn2_norm, m_ffn2_w_in=m_ffn2_w_in, m_ffn2_w_out=m_ffn2_w_out, m_final_norm=m_final_norm, v_meta_tokens=v_meta_tokens, v_ffn1_norm=v_ffn1_norm, v_ffn1_w_in=v_ffn1_w_in, v_ffn1_w_out=v_ffn1_w_out, v_mix_norm=v_mix_norm, v_w_in=v_w_in, v_gdn_conv_w=v_gdn_conv_w, v_gdn_a_log=v_gdn_a_log, v_gdn_dt_bias=v_gdn_dt_bias, v_gdn_out_norm=v_gdn_out_norm, v_ret_out_norm=v_ret_out_norm, v_w_branch_gdn=v_w_branch_gdn, v_w_branch_ret=v_w_branch_ret, v_w_out=v_w_out, v_ffn2_norm=v_ffn2_norm, v_ffn2_w_in=v_ffn2_w_in, v_ffn2_w_out=v_ffn2_w_out, v_final_norm=v_final_norm)
    weights = {n: given[n] for n in TWIN_WEIGHTS}
    shared = {n: given[n] for n in SHARED_INPUTS}
    per_example = {n: given[n] for n in ['x']}
    grad_fn = _jax.value_and_grad(_loss, argnums=(0, 1))

    def one_microbatch(ex, loss_target):
        ex = dict(ex)
        diff = ex.pop(TWIN_DIFF_INPUT)
        return grad_fn(weights, diff, {**shared, **ex}, loss_target)

    if N_MICROBATCH == 1:
        loss, (grad_w, grad_x) = one_microbatch(per_example, given["loss_target"])
    else:
        def body(carry, xs):
            loss_sum, grad_sum = carry
            l_k, (gw_k, gx_k) = one_microbatch(xs[0], xs[1])
            with _jax.named_scope("update"):
                return (loss_sum + l_k, _jax.tree.map(_jnp.add, grad_sum, gw_k)), gx_k

        init = (_jnp.zeros((), _jnp.float32), _jax.tree.map(_jnp.zeros_like, weights))
        (loss, grad_w), grad_x = _jax.lax.scan(body, init, (per_example, given["loss_target"]))
    with _jax.named_scope("update"):
        delta_w, new_m, new_v = {}, {}, {}
        for n in TWIN_WEIGHTS:
            delta_w[n], new_m[n], new_v[n] = _adamw(weights[n], grad_w[n], given["m_" + n], given["v_" + n])
    return (loss, grad_x, *[grad_w[n] for n in TWIN_WEIGHTS], *[delta_w[n] for n in TWIN_WEIGHTS],
            *[new_m[n] for n in TWIN_WEIGHTS], *[new_v[n] for n in TWIN_WEIGHTS])
```

```python
import functools
import math

import numpy as np
import jax
import jax.numpy as jnp
from jax import lax
from jax.experimental import pallas as pl
from jax.experimental.pallas import tpu as pltpu

F32 = jnp.float32
BF16 = jnp.bfloat16

N_DEV = 8
N_META = 16
CHUNK = 64
HEAD_DIM = 128
CONV_K = 4
ROPE_BASE = 10000.0
EPS = 1e-6
PAD_FRONT = 240
HEAD_ROWS = PAD_FRONT + N_META
LANES = 128
PACK_COLS = 1024
PACK_ROW_MULT = 16
VMEM_LIMIT_BYTES = 56 * 1024 * 1024

ADAM_LR = 0.001
ADAM_B1 = 0.9
ADAM_B2 = 0.999
ADAM_EPS = 1e-08
ADAM_WD = 0.01
ADAM_STEP = 10

NN = (((1,), (0,)), ((), ()))
NT = (((1,), (1,)), ((), ()))
TN = (((0,), (0,)), ((), ()))


def _tile(n, target, mult):
    best = 0
    for t in range(mult, min(n, target) + 1, mult):
        if n % t == 0:
            best = t
    return best if best else n


def _params(*semantics):
    return pltpu.CompilerParams(dimension_semantics=semantics, vmem_limit_bytes=VMEM_LIMIT_BYTES)


def _raw_dot(a, b, dims, hi):
    if hi:
        return lax.dot_general(a, b, dims, precision=lax.Precision.HIGHEST, preferred_element_type=F32)
    return lax.dot_general(a.astype(BF16), b.astype(BF16), dims, preferred_element_type=F32)


def _make_mm(hi):
    @jax.custom_vjp
    def nn(a, b):
        return _raw_dot(a, b, NN, hi)

    @jax.custom_vjp
    def nt(a, b):
        return _raw_dot(a, b, NT, hi)

    @jax.custom_vjp
    def tn(a, b):
        return _raw_dot(a, b, TN, hi)

    nn.defvjp(lambda a, b: (_raw_dot(a, b, NN, hi), (a, b)),
              lambda r, g: (_raw_dot(g, r[1], NT, hi), _raw_dot(r[0], g, TN, hi)))
    nt.defvjp(lambda a, b: (_raw_dot(a, b, NT, hi), (a, b)),
              lambda r, g: (_raw_dot(g, r[1], NN, hi), _raw_dot(g, r[0], TN, hi)))
    tn.defvjp(lambda a, b: (_raw_dot(a, b, TN, hi), (a, b)),
              lambda r, g: (_raw_dot(r[1], g, NT, hi), _raw_dot(r[0], g, NN, hi)))
    return nn, nt, tn


def _silu(x):
    return x * jax.nn.sigmoid(x)


def _rms_parts(x):
    r = lax.rsqrt(jnp.mean(x * x, axis=-1, keepdims=True) + EPS)
    return x * r, r


def _rms_bwd(dy, xh, r, gain):
    dxh = dy * gain
    dx = r * (dxh - xh * jnp.mean(dxh * xh, axis=-1, keepdims=True))
    return dx, jnp.sum(dy * xh, axis=0, keepdims=True)


def _ffn_fwd(h, gain, wg, wu, wo, name):
    tp, d = h.shape
    ff = wg.shape[1]
    tm = _tile(tp, 384, 8)
    tf = _tile(ff, 1408, LANES)
    nj = ff // tf

    def body(h_ref, g_ref, wg_ref, wu_ref, wo_ref, o_ref, n_sc, acc_sc):
        j = pl.program_id(1)

        @pl.when(j == 0)
        def _():
            xh, _ = _rms_parts(h_ref[...])
            n_sc[...] = (xh * g_ref[...]).astype(BF16)
            acc_sc[...] = jnp.zeros_like(acc_sc)

        n = n_sc[...]
        a_g = jnp.dot(n, wg_ref[...], preferred_element_type=F32)
        a_u = jnp.dot(n, wu_ref[...], preferred_element_type=F32)
        hid = (_silu(a_g) * a_u).astype(BF16)
        acc_sc[...] += jnp.dot(hid, wo_ref[...], preferred_element_type=F32)

        @pl.when(j == nj - 1)
        def _():
            o_ref[...] = h_ref[...] + 0.5 * acc_sc[...]

    return pl.pallas_call(
        body, name=name, grid=(tp // tm, nj),
        in_specs=[pl.BlockSpec((tm, d), lambda i, j: (i, 0)), pl.BlockSpec((1, d), lambda i, j: (0, 0)),
                  pl.BlockSpec((d, tf), lambda i, j: (0, j)), pl.BlockSpec((d, tf), lambda i, j: (0, j)),
                  pl.BlockSpec((tf, d), lambda i, j: (j, 0))],
        out_specs=pl.BlockSpec((tm, d), lambda i, j: (i, 0)),
        out_shape=jax.ShapeDtypeStruct((tp, d), F32),
        scratch_shapes=[pltpu.VMEM((tm, d), BF16), pltpu.VMEM((tm, d), F32)],
        compiler_params=_params("parallel", "arbitrary"))(h, gain, wg, wu, wo)


def _ffn_bwd(h, dho, gain, wg, wu, wo, name):
    tp, d = h.shape
    ff = wg.shape[1]
    tm = _tile(tp, 384, 8)
    tf = _tile(ff, 1408, LANES)
    ni, nj = tp // tm, ff // tf

    def body(h_ref, dho_ref, g_ref, wg_ref, wu_ref, wo_ref,
             dh_ref, dgain_ref, n_ref, hid_ref, dag_ref, dau_ref, dn_sc, dhb_sc):
        i, j = pl.program_id(0), pl.program_id(1)

        @pl.when(j == 0)
        def _():
            xh, _ = _rms_parts(h_ref[...])
            n_ref[...] = (xh * g_ref[...]).astype(BF16)
            dn_sc[...] = jnp.zeros_like(dn_sc)
            dhb_sc[...] = (0.5 * dho_ref[...]).astype(BF16)

        @pl.when((i == 0) & (j == 0))
        def _():
            dgain_ref[...] = jnp.zeros_like(dgain_ref)

        n = n_ref[...]
        a_g = jnp.dot(n, wg_ref[...], preferred_element_type=F32)
        a_u = jnp.dot(n, wu_ref[...], preferred_element_type=F32)
        sg = jax.nn.sigmoid(a_g)
        s = a_g * sg
        hid_ref[...] = (s * a_u).astype(BF16)
        d_hid = lax.dot_general(dhb_sc[...], wo_ref[...], NT, preferred_element_type=F32)
        d_au = (d_hid * s).astype(BF16)
        d_ag = (d_hid * a_u * (sg * (1.0 + a_g * (1.0 - sg)))).astype(BF16)
        dau_ref[...] = d_au
        dag_ref[...] = d_ag
        dn_sc[...] += (lax.dot_general(d_ag, wg_ref[...], NT, preferred_element_type=F32)
                       + lax.dot_general(d_au, wu_ref[...], NT, preferred_element_type=F32))

        @pl.when(j == nj - 1)
        def _():
            xh, r = _rms_parts(h_ref[...])
            dx, dg = _rms_bwd(dn_sc[...], xh, r, g_ref[...])
            dh_ref[...] = dho_ref[...] + dx
            dgain_ref[...] += dg

    return pl.pallas_call(
        body, name=name, grid=(ni, nj),
        in_specs=[pl.BlockSpec((tm, d), lambda i, j: (i, 0)), pl.BlockSpec((tm, d), lambda i, j: (i, 0)),
                  pl.BlockSpec((1, d), lambda i, j: (0, 0)),
                  pl.BlockSpec((d, tf), lambda i, j: (0, j)), pl.BlockSpec((d, tf), lambda i, j: (0, j)),
                  pl.BlockSpec((tf, d), lambda i, j: (j, 0))],
        out_specs=[pl.BlockSpec((tm, d), lambda i, j: (i, 0)), pl.BlockSpec((1, d), lambda i, j: (0, 0)),
                   pl.BlockSpec((tm, d), lambda i, j: (i, 0)),
                   pl.BlockSpec((tm, tf), lambda i, j: (i, j)), pl.BlockSpec((tm, tf), lambda i, j: (i, j)),
                   pl.BlockSpec((tm, tf), lambda i, j: (i, j))],
        out_shape=[jax.ShapeDtypeStruct((tp, d), F32), jax.ShapeDtypeStruct((1, d), F32),
                   jax.ShapeDtypeStruct((tp, d), BF16), jax.ShapeDtypeStruct((tp, ff), BF16),
                   jax.ShapeDtypeStruct((tp, ff), BF16), jax.ShapeDtypeStruct((tp, ff), BF16)],
        scratch_shapes=[pltpu.VMEM((tm, d), F32), pltpu.VMEM((tm, d), BF16)],
        compiler_params=_params("arbitrary", "arbitrary"))(h, dho, gain, wg, wu, wo)


def _matmul_tn(a, b, name, scale=1.0):
    t, m = a.shape
    n = b.shape[1]
    bm = _tile(m, 1024, LANES)
    bn = _tile(n, 1536, LANES)
    tk = _tile(t, 768, 16)
    nk = t // tk

    def body(a_ref, b_ref, o_ref):
        k = pl.program_id(2)

        @pl.when(k == 0)
        def _():
            o_ref[...] = jnp.zeros_like(o_ref)

        o_ref[...] += lax.dot_general(a_ref[...].astype(BF16), b_ref[...].astype(BF16), TN,
                                      preferred_element_type=F32)

        if scale != 1.0:
            @pl.when(k == nk - 1)
            def _():
                o_ref[...] = o_ref[...] * scale

    return pl.pallas_call(
        body, name=name, grid=(m // bm, n // bn, nk),
        in_specs=[pl.BlockSpec((tk, bm), lambda i, j, k: (k, i)), pl.BlockSpec((tk, bn), lambda i, j, k: (k, j))],
        out_specs=pl.BlockSpec((bm, bn), lambda i, j, k: (i, j)),
        out_shape=jax.ShapeDtypeStruct((m, n), F32),
        compiler_params=_params("parallel", "parallel", "arbitrary"))(a, b)


def _matmul_nt_parts(parts, w, acc, name):
    t = parts[0].shape[0]
    d = w.shape[0]
    widths = [p.shape[1] for p in parts]
    tk = _tile(math.gcd(*widths), 1024, LANES)
    counts = [wd // tk for wd in widths]
    starts = [sum(counts[:g]) for g in range(len(parts))]
    nk = sum(counts)
    tm = _tile(t, 768, 8)
    n_parts = len(parts)

    def body(*refs):
        a_refs, w_ref, o_ref = refs[:n_parts], refs[n_parts], refs[-1]
        k = pl.program_id(1)

        @pl.when(k == 0)
        def _():
            o_ref[...] = jnp.zeros_like(o_ref) if acc is None else refs[n_parts + 1][...]

        for g in range(n_parts):
            @pl.when((k >= starts[g]) & (k < starts[g] + counts[g]))
            def _(g=g):
                o_ref[...] += lax.dot_general(a_refs[g][...].astype(BF16), w_ref[...], NT,
                                              preferred_element_type=F32)

    in_specs = [pl.BlockSpec((tm, tk), lambda i, k, lo=starts[g], nb=counts[g]: (i, jnp.clip(k - lo, 0, nb - 1)))
                for g in range(n_parts)]
    in_specs.append(pl.BlockSpec((d, tk), lambda i, k: (0, k)))
    args = list(parts) + [w]
    if acc is not None:
        in_specs.append(pl.BlockSpec((tm, d), lambda i, k: (i, 0)))
        args.append(acc)
    return pl.pallas_call(
        body, name=name, grid=(t // tm, nk), in_specs=in_specs,
        out_specs=pl.BlockSpec((tm, d), lambda i, k: (i, 0)),
        out_shape=jax.ShapeDtypeStruct((t, d), F32),
        compiler_params=_params("parallel", "arbitrary"))(*args)


def _proj_fwd(h, gain, wp, name):
    tp, d = h.shape
    npad = wp.shape[1]
    tm = _tile(tp, 768, 8)
    tn = _tile(npad, 1152, LANES)

    def body(h_ref, g_ref, w_ref, o_ref, n_ref):
        @pl.when(pl.program_id(1) == 0)
        def _():
            xh, _ = _rms_parts(h_ref[...])
            n_ref[...] = (xh * g_ref[...]).astype(BF16)

        o_ref[...] = jnp.dot(n_ref[...], w_ref[...], preferred_element_type=F32)

    return pl.pallas_call(
        body, name=name, grid=(tp // tm, npad // tn),
        in_specs=[pl.BlockSpec((tm, d), lambda i, j: (i, 0)), pl.BlockSpec((1, d), lambda i, j: (0, 0)),
                  pl.BlockSpec((d, tn), lambda i, j: (0, j))],
        out_specs=[pl.BlockSpec((tm, tn), lambda i, j: (i, j)), pl.BlockSpec((tm, d), lambda i, j: (i, 0))],
        out_shape=[jax.ShapeDtypeStruct((tp, npad), F32), jax.ShapeDtypeStruct((tp, d), BF16)],
        compiler_params=_params("parallel", "arbitrary"))(h, gain, wp)


def _norm_bwd(h, gain, dn, dres, name):
    tp, d = h.shape
    tm = _tile(tp, 256, 8)

    def body(h_ref, g_ref, dn_ref, dres_ref, dh_ref, dgain_ref):
        @pl.when(pl.program_id(0) == 0)
        def _():
            dgain_ref[...] = jnp.zeros_like(dgain_ref)

        xh, r = _rms_parts(h_ref[...])
        dx, dg = _rms_bwd(dn_ref[...], xh, r, g_ref[...])
        dh_ref[...] = dres_ref[...] + dx
        dgain_ref[...] += dg

    row = pl.BlockSpec((tm, d), lambda i: (i, 0))
    vec = pl.BlockSpec((1, d), lambda i: (0, 0))
    return pl.pallas_call(
        body, name=name, grid=(tp // tm,), in_specs=[row, vec, row, row], out_specs=[row, vec],
        out_shape=[jax.ShapeDtypeStruct((tp, d), F32), jax.ShapeDtypeStruct((1, d), F32)],
        compiler_params=_params("arbitrary"))(h, gain, dn, dres)


def _head_post(a, grp):
    a = _silu(a)
    r = lax.rsqrt(jnp.sum(a * a, axis=-1, keepdims=True) + EPS)
    if isinstance(grp, int):
        return a if grp == 2 else a * r * (HEAD_DIM ** -0.5 if grp == 0 else 1.0)
    scale = jnp.where(grp == 0, HEAD_DIM ** -0.5, 1.0).astype(F32)
    return jnp.where(grp == 2, a, a * r * scale)


def _conv_taps(ext_sc, w_ref, tm):
    c = None
    for i in range(CONV_K):
        s = CONV_K - 1 - i
        term = w_ref[i:i + 1, :] * ext_sc[8 - s:8 - s + tm, :]
        c = term if c is None else c + term
    return c


def _conv_fwd(proj, conv_w, hv, name):
    tp = proj.shape[0]
    tm = _tile(tp, 256, 8)
    nh = hv // HEAD_DIM

    def body(x_ref, halo_ref, w_ref, o_ref, ext_sc):
        i, grp = pl.program_id(0), pl.program_id(1)
        ext_sc[0:8, :] = jnp.where(i == 0, 0.0, halo_ref[...])
        ext_sc[8:, :] = x_ref[...]
        c = _conv_taps(ext_sc, w_ref, tm)
        for h in range(nh):
            sl = slice(h * HEAD_DIM, (h + 1) * HEAD_DIM)
            o_ref[:, sl] = _head_post(c[:, sl], grp)

    return pl.pallas_call(
        body, name=name, grid=(tp // tm, 3),
        in_specs=[pl.BlockSpec((tm, hv), lambda i, g: (i, g)),
                  pl.BlockSpec((8, hv), lambda i, g: (jnp.maximum(i * (tm // 8) - 1, 0), g)),
                  pl.BlockSpec((CONV_K, hv), lambda i, g: (0, g))],
        out_specs=pl.BlockSpec((tm, hv), lambda i, g: (i, g)),
        out_shape=jax.ShapeDtypeStruct((tp, 3 * hv), F32),
        scratch_shapes=[pltpu.VMEM((tm + 8, hv), F32)],
        compiler_params=_params("parallel", "arbitrary"))(proj, proj, conv_w)


def _conv_bwd_pre(proj, conv_w, dy, grp, hv, name):
    tp = proj.shape[0]
    tm = _tile(tp, 256, 8)
    nh = hv // HEAD_DIM

    def body(x_ref, halo_ref, w_ref, dy_ref, dc_ref, dw_ref, ext_sc):
        i = pl.program_id(0)
        ext_sc[0:8, :] = jnp.where(i == 0, 0.0, halo_ref[...])
        ext_sc[8:, :] = x_ref[...]
        c = _conv_taps(ext_sc, w_ref, tm)
        for h in range(nh):
            sl = slice(h * HEAD_DIM, (h + 1) * HEAD_DIM)
            _, vjp = jax.vjp(lambda a: _head_post(a, grp), c[:, sl])
            dc_ref[:, sl] = vjp(dy_ref[:, sl])[0]

        @pl.when(i == 0)
        def _():
            dw_ref[...] = jnp.zeros_like(dw_ref)

        dc = dc_ref[...]
        for k in range(CONV_K):
            s = CONV_K - 1 - k
            dw_ref[k:k + 1, :] += jnp.sum(dc * ext_sc[8 - s:8 - s + tm, :], axis=0, keepdims=True)

    return pl.pallas_call(
        body, name=name, grid=(tp // tm,),
        in_specs=[pl.BlockSpec((tm, hv), lambda i: (i, grp)),
                  pl.BlockSpec((8, hv), lambda i: (jnp.maximum(i * (tm // 8) - 1, 0), grp)),
                  pl.BlockSpec((CONV_K, hv), lambda i: (0, grp)),
                  pl.BlockSpec((tm, hv), lambda i: (i, 0))],
        out_specs=[pl.BlockSpec((tm, hv), lambda i: (i, 0)), pl.BlockSpec((CONV_K, hv), lambda i: (0, 0))],
        out_shape=[jax.ShapeDtypeStruct((tp, hv), F32), jax.ShapeDtypeStruct((CONV_K, hv), F32)],
        scratch_shapes=[pltpu.VMEM((tm + 8, hv), F32)],
        compiler_params=_params("arbitrary"))(proj, proj, conv_w, dy)


def _conv_bwd_in(dc, conv_w, grp, name):
    tp, hv = dc.shape
    tm = _tile(tp, 256, 8)
    ni = tp // tm

    def body(dc_ref, halo_ref, w_ref, dx_ref, ext_sc):
        i = pl.program_id(0)
        ext_sc[0:tm, :] = dc_ref[...]
        ext_sc[tm:, :] = jnp.where(i == ni - 1, 0.0, halo_ref[...])
        dx = None
        for k in range(CONV_K):
            s = CONV_K - 1 - k
            term = w_ref[k:k + 1, :] * ext_sc[s:s + tm, :]
            dx = term if dx is None else dx + term
        dx_ref[...] = dx.astype(BF16)

    return pl.pallas_call(
        body, name=name, grid=(ni,),
        in_specs=[pl.BlockSpec((tm, hv), lambda i: (i, 0)),
                  pl.BlockSpec((8, hv), lambda i: (jnp.minimum((i + 1) * (tm // 8), tp // 8 - 1), 0)),
                  pl.BlockSpec((CONV_K, hv), lambda i: (0, grp))],
        out_specs=pl.BlockSpec((tm, hv), lambda i: (i, 0)),
        out_shape=jax.ShapeDtypeStruct((tp, hv), BF16),
        scratch_shapes=[pltpu.VMEM((tm + 8, hv), F32)],
        compiler_params=_params("parallel"))(dc, dc, conv_w)


def _gdn_chunk(state, q, k, v, ba, alog, dtb, h, nh):
    mm_nn, mm_nt, mm_tn = _make_mm(False)
    hi_nn, hi_nt, _ = _make_mm(True)
    c = q.shape[0]
    lane = lax.broadcasted_iota(jnp.int32, (c, LANES), 1)
    row = lax.broadcasted_iota(jnp.int32, (c, 1), 0)
    ri = lax.broadcasted_iota(jnp.int32, (c, c), 0)
    ci = lax.broadcasted_iota(jnp.int32, (c, c), 1)
    causal = ri >= ci
    strict = ri > ci
    tril = causal.astype(F32)
    eye = (ri == ci).astype(F32)
    sel_a = (lane == nh + h).astype(F32)
    sel_b = (lane == h).astype(F32)

    x = ba + dtb
    softplus = jnp.maximum(x, 0.0) + jnp.log1p(jnp.exp(-jnp.abs(x)))
    g_all = -jnp.exp(alog) * softplus
    gc_all = hi_nn(tril, g_all)
    gcol = jnp.sum(gc_all * sel_a, axis=1, keepdims=True)
    grow = hi_nt(sel_a, gc_all)
    beta = jnp.sum(jax.nn.sigmoid(ba) * sel_b, axis=1, keepdims=True)
    decay = jnp.where(causal, jnp.exp(jnp.where(causal, gcol - grow, 0.0)), 0.0)
    kb = k * beta
    x_neg = -jnp.where(strict, mm_nt(kb, k) * decay, 0.0)
    t_inv = eye + x_neg
    xp = x_neg
    for _ in range(int(math.log2(c)) - 1):
        xp = hi_nn(xp, xp)
        t_inv = t_inv + hi_nn(t_inv, xp)
    eg = jnp.exp(gcol)
    u = hi_nn(t_inv, v * beta)
    w = hi_nn(t_inv, kb * eg)
    qk = mm_nt(q, k) * decay
    glast = jnp.sum(gcol * (row == c - 1).astype(F32), axis=0, keepdims=True)
    v_new = u - mm_nn(w, state)
    o = mm_nn(q * eg, state) + mm_nn(qk, v_new)
    new_state = state * jnp.exp(glast) + mm_tn(k * jnp.exp(glast - gcol), v_new)
    return o, new_state


def _gdn_specs(nh, nc, rev):
    cidx = (lambda c: nc - 1 - c) if rev else (lambda c: c)
    q = pl.BlockSpec((CHUNK, HEAD_DIM), lambda c, h: (cidx(c), h))
    k = pl.BlockSpec((CHUNK, HEAD_DIM), lambda c, h: (cidx(c), nh + h))
    v = pl.BlockSpec((CHUNK, HEAD_DIM), lambda c, h: (cidx(c), 2 * nh + h))
    ba = pl.BlockSpec((CHUNK, LANES), lambda c, h: (cidx(c), 10 * nh))
    vec = pl.BlockSpec((1, LANES), lambda c, h: (0, 0))
    st = pl.BlockSpec((1, 1, HEAD_DIM, HEAD_DIM), lambda c, h: (cidx(c), h, 0, 0))
    return q, k, v, ba, vec, st


def _gdn_fwd(qkv, proj, alog, dtb, nh, name):
    tp = qkv.shape[0]
    nc = tp // CHUNK

    def body(q_ref, k_ref, v_ref, ba_ref, al_ref, dt_ref, o_ref, st_ref, s_sc):
        c, h = pl.program_id(0), pl.program_id(1)

        @pl.when(c == 0)
        def _():
            s_sc[h] = jnp.zeros((HEAD_DIM, HEAD_DIM), F32)

        s = s_sc[h]
        st_ref[0, 0] = s
        o, s_new = _gdn_chunk(s, q_ref[...], k_ref[...], v_ref[...], ba_ref[...], al_ref[...], dt_ref[...], h, nh)
        o_ref[...] = o
        s_sc[h] = s_new

    q, k, v, ba, vec, st = _gdn_specs(nh, nc, False)
    return pl.pallas_call(
        body, name=name, grid=(nc, nh), in_specs=[q, k, v, ba, vec, vec],
        out_specs=[pl.BlockSpec((CHUNK, HEAD_DIM), lambda c, h: (c, h)), st],
        out_shape=[jax.ShapeDtypeStruct((tp, nh * HEAD_DIM), F32),
                   jax.ShapeDtypeStruct((nc, nh, HEAD_DIM, HEAD_DIM), F32)],
        scratch_shapes=[pltpu.VMEM((nh, HEAD_DIM, HEAD_DIM), F32)],
        compiler_params=_params("arbitrary", "arbitrary"))(qkv, qkv, qkv, proj, alog, dtb)


def _gdn_bwd(qkv, proj, alog, dtb, states, do, nh, name):
    tp = qkv.shape[0]
    nc = tp // CHUNK

    def body(q_ref, k_ref, v_ref, ba_ref, al_ref, dt_ref, st_ref, do_ref,
             dq_ref, dk_ref, dv_ref, dba_ref, dal_ref, ddt_ref, ds_sc):
        c, h = pl.program_id(0), pl.program_id(1)

        @pl.when(c == 0)
        def _():
            ds_sc[h] = jnp.zeros((HEAD_DIM, HEAD_DIM), F32)

        @pl.when((c == 0) & (h == 0))
        def _():
            dal_ref[...] = jnp.zeros_like(dal_ref)
            ddt_ref[...] = jnp.zeros_like(ddt_ref)

        @pl.when(h == 0)
        def _():
            dba_ref[...] = jnp.zeros_like(dba_ref)

        fn = lambda s, q, k, v, ba, al, dt: _gdn_chunk(s, q, k, v, ba, al, dt, h, nh)
        _, vjp = jax.vjp(fn, st_ref[0, 0], q_ref[...], k_ref[...], v_ref[...], ba_ref[...], al_ref[...], dt_ref[...])
        ds, dq, dk, dv, dba, dal, ddt = vjp((do_ref[...], ds_sc[h]))
        ds_sc[h] = ds
        dq_ref[...] = dq
        dk_ref[...] = dk
        dv_ref[...] = dv
        dba_ref[...] += dba
        dal_ref[...] += dal
        ddt_ref[...] += ddt

    q, k, v, ba, vec, st = _gdn_specs(nh, nc, True)
    do_spec = pl.BlockSpec((CHUNK, HEAD_DIM), lambda c, h: (nc - 1 - c, h))
    outs = pl.pallas_call(
        body, name=name, grid=(nc, nh), in_specs=[q, k, v, ba, vec, vec, st, do_spec],
        out_specs=[do_spec, do_spec, do_spec, pl.BlockSpec((CHUNK, LANES), lambda c, h: (nc - 1 - c, 0)), vec, vec],
        out_shape=[jax.ShapeDtypeStruct((tp, nh * HEAD_DIM), F32)] * 3
                  + [jax.ShapeDtypeStruct((tp, LANES), F32), jax.ShapeDtypeStruct((1, LANES), F32),
                     jax.ShapeDtypeStruct((1, LANES), F32)],
        scratch_shapes=[pltpu.VMEM((nh, HEAD_DIM, HEAD_DIM), F32)],
        compiler_params=_params("arbitrary", "arbitrary"))(qkv, qkv, qkv, proj, alog, dtb, states, do)
    return outs


def _swap_pairs(t):
    lane = lax.broadcasted_iota(jnp.int32, t.shape, 1)
    n = t.shape[1]
    return jnp.where(lane % 2 == 0, pltpu.roll(t, n - 1, 1), pltpu.roll(t, 1, 1))


def _rot(t, cos, sin_signed):
    return t * cos + _swap_pairs(t) * sin_signed


def _rot_t(dt, cos, sin_signed):
    return dt * cos + _swap_pairs(dt * sin_signed)


def _ret_chunk(state, q, k, v, dec, xi, zeta, cd):
    mm_nn, mm_nt, mm_tn = _make_mm(False)
    scores = mm_nt(q, k) * dec
    o = mm_nn(scores, v) + mm_nn(q * xi, state)
    return o, state * cd + mm_tn(k * zeta, v)


def _ret_specs(nh, nc, rev):
    cidx = (lambda c: nc - 1 - c) if rev else (lambda c: c)
    q = pl.BlockSpec((CHUNK, HEAD_DIM), lambda c, h: (cidx(c), 3 * nh + h))
    k = pl.BlockSpec((CHUNK, HEAD_DIM), lambda c, h: (cidx(c), 4 * nh + h))
    v = pl.BlockSpec((CHUNK, HEAD_DIM), lambda c, h: (cidx(c), 5 * nh + h))
    rope = pl.BlockSpec((CHUNK, HEAD_DIM), lambda c, h: (cidx(c), 0))
    dec = pl.BlockSpec((1, CHUNK, CHUNK), lambda c, h: (h, 0, 0))
    tab = pl.BlockSpec((1, CHUNK, HEAD_DIM), lambda c, h: (h, 0, 0))
    st = pl.BlockSpec((1, 1, HEAD_DIM, HEAD_DIM), lambda c, h: (cidx(c), h, 0, 0))
    return q, k, v, rope, dec, tab, st


def _ret_fwd(proj, cos, sin, dec, xi, zeta, cd, nh, name):
    tp = proj.shape[0]
    nc = tp // CHUNK
    kscale = HEAD_DIM ** -0.5

    def body(q_ref, k_ref, v_ref, cos_ref, sin_ref, dec_ref, xi_ref, zeta_ref, cd_ref, o_ref, st_ref, s_sc):
        c, h = pl.program_id(0), pl.program_id(1)

        @pl.when(c == 0)
        def _():
            s_sc[h] = jnp.zeros((HEAD_DIM, HEAD_DIM), F32)

        s = s_sc[h]
        st_ref[0, 0] = s
        q = _rot(q_ref[...], cos_ref[...], sin_ref[...])
        k = _rot(k_ref[...], cos_ref[...], sin_ref[...]) * kscale
        o, s_new = _ret_chunk(s, q, k, v_ref[...], dec_ref[0], xi_ref[0], zeta_ref[0], cd_ref[0][0:1, :])
        o_ref[...] = o
        s_sc[h] = s_new

    q, k, v, rope, decs, tab, st = _ret_specs(nh, nc, False)
    return pl.pallas_call(
        body, name=name, grid=(nc, nh), in_specs=[q, k, v, rope, rope, decs, tab, tab, tab],
        out_specs=[pl.BlockSpec((CHUNK, HEAD_DIM), lambda c, h: (c, h)), st],
        out_shape=[jax.ShapeDtypeStruct((tp, nh * HEAD_DIM), F32),
                   jax.ShapeDtypeStruct((nc, nh, HEAD_DIM, HEAD_DIM), F32)],
        scratch_shapes=[pltpu.VMEM((nh, HEAD_DIM, HEAD_DIM), F32)],
        compiler_params=_params("arbitrary", "arbitrary"))(proj, proj, proj, cos, sin, dec, xi, zeta, cd)


def _ret_bwd(proj, cos, sin, dec, xi, zeta, cd, states, do, nh, name):
    tp = proj.shape[0]
    nc = tp // CHUNK
    kscale = HEAD_DIM ** -0.5

    def body(q_ref, k_ref, v_ref, cos_ref, sin_ref, dec_ref, xi_ref, zeta_ref, cd_ref, st_ref, do_ref,
             dq_ref, dk_ref, dv_ref, ds_sc):
        c, h = pl.program_id(0), pl.program_id(1)

        @pl.when(c == 0)
        def _():
            ds_sc[h] = jnp.zeros((HEAD_DIM, HEAD_DIM), F32)

        cos_t, sin_t = cos_ref[...], sin_ref[...]
        q = _rot(q_ref[...], cos_t, sin_t)
        k = _rot(k_ref[...], cos_t, sin_t) * kscale
        dec_t, xi_t, zeta_t, cd_t = dec_ref[0], xi_ref[0], zeta_ref[0], cd_ref[0][0:1, :]
        fn = lambda s, q, k, v: _ret_chunk(s, q, k, v, dec_t, xi_t, zeta_t, cd_t)
        _, vjp = jax.vjp(fn, st_ref[0, 0], q, k, v_ref[...])
        ds, dq, dk, dv = vjp((do_ref[...], ds_sc[h]))
        ds_sc[h] = ds
        dq_ref[...] = _rot_t(dq, cos_t, sin_t).astype(BF16)
        dk_ref[...] = _rot_t(dk * kscale, cos_t, sin_t).astype(BF16)
        dv_ref[...] = dv.astype(BF16)

    q, k, v, rope, decs, tab, st = _ret_specs(nh, nc, True)
    do_spec = pl.BlockSpec((CHUNK, HEAD_DIM), lambda c, h: (nc - 1 - c, h))
    return pl.pallas_call(
        body, name=name, grid=(nc, nh), in_specs=[q, k, v, rope, rope, decs, tab, tab, tab, st, do_spec],
        out_specs=[do_spec, do_spec, do_spec],
        out_shape=[jax.ShapeDtypeStruct((tp, nh * HEAD_DIM), BF16)] * 3,
        scratch_shapes=[pltpu.VMEM((nh, HEAD_DIM, HEAD_DIM), F32)],
        compiler_params=_params("arbitrary", "arbitrary"))(proj, proj, proj, cos, sin, dec, xi, zeta, cd, states, do)


def _gdn_out(o, z, gnorm):
    return o * lax.rsqrt(jnp.mean(o * o, axis=-1, keepdims=True) + EPS) * gnorm * _silu(z)


def _ret_out(o, rg, rnorm):
    mu = jnp.mean(o, axis=-1, keepdims=True)
    var = jnp.mean(jnp.square(o - mu), axis=-1, keepdims=True)
    return _silu(rg) * ((o - mu) * lax.rsqrt(var + EPS) * rnorm)


def _post_specs(tm, hv, d):
    row = lambda col: pl.BlockSpec((tm, hv), lambda i: (i, col))
    return dict(
        oa=row(0), ob=row(0), z=row(6), rg=row(7), ga=row(8), gb=row(9),
        gnorm=pl.BlockSpec((1, HEAD_DIM), lambda i: (0, 0)), rnorm=pl.BlockSpec((1, hv), lambda i: (0, 0)),
        w=pl.BlockSpec((hv, d), lambda i: (0, 0)), res=pl.BlockSpec((tm, d), lambda i: (i, 0)))


def _post_fwd(oa, ob, proj, gnorm, rnorm, wbg, wbr, wo, h1, name):
    tp, d = h1.shape
    hv = oa.shape[1]
    nh = hv // HEAD_DIM
    tm = _tile(tp, 256, 8)

    def body(oa_ref, ob_ref, z_ref, rg_ref, ga_ref, gb_ref, gn_ref, rn_ref, wbg_ref, wbr_ref, wo_ref, h_ref,
             o_ref, ya_sc, yb_sc):
        for h in range(nh):
            sl = slice(h * HEAD_DIM, (h + 1) * HEAD_DIM)
            ya_sc[:, sl] = _gdn_out(oa_ref[:, sl], z_ref[:, sl], gn_ref[...]).astype(BF16)
            yb_sc[:, sl] = _ret_out(ob_ref[:, sl], rg_ref[:, sl], rn_ref[:, sl]).astype(BF16)
        pa = jnp.dot(ya_sc[...], wbg_ref[...], preferred_element_type=F32)
        pb = jnp.dot(yb_sc[...], wbr_ref[...], preferred_element_type=F32)
        merged = jax.nn.sigmoid(ga_ref[...]) * pa + jax.nn.sigmoid(gb_ref[...]) * pb
        o_ref[...] = h_ref[...] + jnp.dot(merged.astype(BF16), wo_ref[...], preferred_element_type=F32)

    sp = _post_specs(tm, hv, d)
    return pl.pallas_call(
        body, name=name, grid=(tp // tm,),
        in_specs=[sp["oa"], sp["ob"], sp["z"], sp["rg"], sp["ga"], sp["gb"], sp["gnorm"], sp["rnorm"],
                  sp["w"], sp["w"], sp["w"], sp["res"]],
        out_specs=sp["res"], out_shape=jax.ShapeDtypeStruct((tp, d), F32),
        scratch_shapes=[pltpu.VMEM((tm, hv), BF16), pltpu.VMEM((tm, hv), BF16)],
        compiler_params=_params("parallel"))(oa, ob, proj, proj, proj, proj, gnorm, rnorm, wbg, wbr, wo, h1)


def _post_bwd(oa, ob, proj, gnorm, rnorm, wbg, wbr, wo, dh2, name):
    tp, d = dh2.shape
    hv = oa.shape[1]
    nh = hv // HEAD_DIM
    tm = _tile(tp, 256, 8)

    def body(oa_ref, ob_ref, z_ref, rg_ref, ga_ref, gb_ref, gn_ref, rn_ref, wbg_ref, wbr_ref, wo_ref, dh_ref,
             doa_ref, dob_ref, dg_ref, ya_ref, yb_ref, mg_ref, dpa_ref, dpb_ref, dgn_ref, drn_ref,
             dya_sc, dyb_sc):
        @pl.when(pl.program_id(0) == 0)
        def _():
            dgn_ref[...] = jnp.zeros_like(dgn_ref)
            drn_ref[...] = jnp.zeros_like(drn_ref)

        for h in range(nh):
            sl = slice(h * HEAD_DIM, (h + 1) * HEAD_DIM)
            ya_ref[:, sl] = _gdn_out(oa_ref[:, sl], z_ref[:, sl], gn_ref[...]).astype(BF16)
            yb_ref[:, sl] = _ret_out(ob_ref[:, sl], rg_ref[:, sl], rn_ref[:, sl]).astype(BF16)
        pa = jnp.dot(ya_ref[...], wbg_ref[...], preferred_element_type=F32)
        pb = jnp.dot(yb_ref[...], wbr_ref[...], preferred_element_type=F32)
        sa = jax.nn.sigmoid(ga_ref[...])
        sb = jax.nn.sigmoid(gb_ref[...])
        mg_ref[...] = (sa * pa + sb * pb).astype(BF16)
        dm = lax.dot_general(dh_ref[...].astype(BF16), wo_ref[...], NT, preferred_element_type=F32)
        dpa = (dm * sa).astype(BF16)
        dpb = (dm * sb).astype(BF16)
        dpa_ref[...] = dpa
        dpb_ref[...] = dpb
        dg_ref[:, 2 * hv:3 * hv] = (dm * pa * sa * (1.0 - sa)).astype(BF16)
        dg_ref[:, 3 * hv:4 * hv] = (dm * pb * sb * (1.0 - sb)).astype(BF16)
        dya_sc[...] = lax.dot_general(dpa, wbg_ref[...], NT, preferred_element_type=F32)
        dyb_sc[...] = lax.dot_general(dpb, wbr_ref[...], NT, preferred_element_type=F32)
        for h in range(nh):
            sl = slice(h * HEAD_DIM, (h + 1) * HEAD_DIM)
            _, vjp_a = jax.vjp(_gdn_out, oa_ref[:, sl], z_ref[:, sl], gn_ref[...])
            doa, dz, dgn = vjp_a(dya_sc[:, sl])
            doa_ref[:, sl] = doa
            dg_ref[:, sl] = dz.astype(BF16)
            dgn_ref[...] += dgn
            _, vjp_b = jax.vjp(_ret_out, ob_ref[:, sl], rg_ref[:, sl], rn_ref[:, sl])
            dob, drg, drn = vjp_b(dyb_sc[:, sl])
            dob_ref[:, sl] = dob
            dg_ref[:, hv + h * HEAD_DIM:hv + (h + 1) * HEAD_DIM] = drg.astype(BF16)
            drn_ref[:, sl] += drn

    sp = _post_specs(tm, hv, d)
    act = pl.BlockSpec((tm, hv), lambda i: (i, 0))
    return pl.pallas_call(
        body, name=name, grid=(tp // tm,),
        in_specs=[sp["oa"], sp["ob"], sp["z"], sp["rg"], sp["ga"], sp["gb"], sp["gnorm"], sp["rnorm"],
                  sp["w"], sp["w"], sp["w"], sp["res"]],
        out_specs=[act, act, pl.BlockSpec((tm, 4 * hv), lambda i: (i, 0)), act, act, sp["res"], sp["res"],
                   sp["res"], sp["gnorm"], sp["rnorm"]],
        out_shape=[jax.ShapeDtypeStruct((tp, hv), F32), jax.ShapeDtypeStruct((tp, hv), F32),
                   jax.ShapeDtypeStruct((tp, 4 * hv), BF16), jax.ShapeDtypeStruct((tp, hv), BF16),
                   jax.ShapeDtypeStruct((tp, hv), BF16), jax.ShapeDtypeStruct((tp, d), BF16),
                   jax.ShapeDtypeStruct((tp, d), BF16), jax.ShapeDtypeStruct((tp, d), BF16),
                   jax.ShapeDtypeStruct((1, HEAD_DIM), F32), jax.ShapeDtypeStruct((1, hv), F32)],
        scratch_shapes=[pltpu.VMEM((tm, hv), F32), pltpu.VMEM((tm, hv), F32)],
        compiler_params=_params("arbitrary"))(oa, ob, proj, proj, proj, proj, gnorm, rnorm, wbg, wbr, wo, dh2)


def _final(h3, gain, target, name):
    tp, d = h3.shape
    tm = HEAD_ROWS

    def body(h_ref, g_ref, t_ref, loss_ref, dh_ref, dgain_ref):
        i = pl.program_id(0)

        @pl.when(i == 0)
        def _():
            loss_ref[...] = jnp.zeros_like(loss_ref)
            dgain_ref[...] = jnp.zeros_like(dgain_ref)

        xh, r = _rms_parts(h_ref[...])
        err = jnp.where(i == 0, 0.0, xh * g_ref[...] - t_ref[...])
        dx, dg = _rms_bwd(err * (1.0 / d), xh, r, g_ref[...])
        dh_ref[...] = dx
        dgain_ref[...] += dg
        loss_ref[...] += 0.5 * jnp.sum(jnp.mean(err * err, axis=-1, keepdims=True), axis=0, keepdims=True)

    row = pl.BlockSpec((tm, d), lambda i: (i, 0))
    vec = pl.BlockSpec((1, d), lambda i: (0, 0))
    return pl.pallas_call(
        body, name=name, grid=(tp // tm,),
        in_specs=[row, vec, pl.BlockSpec((tm, d), lambda i: (jnp.maximum(i - 1, 0), 0))],
        out_specs=[pl.BlockSpec((1, LANES), lambda i: (0, 0)), row, vec],
        out_shape=[jax.ShapeDtypeStruct((1, LANES), F32), jax.ShapeDtypeStruct((tp, d), F32),
                   jax.ShapeDtypeStruct((1, d), F32)],
        compiler_params=_params("arbitrary"))(h3, gain, target)


def _peer(k):
    x, y, c = lax.axis_index("x"), lax.axis_index("y"), lax.axis_index("c")
    return (1 - x if k & 4 else x, 1 - y if k & 2 else y, 1 - c if k & 1 else c)


def _my_index():
    return 4 * lax.axis_index("x") + 2 * lax.axis_index("y") + lax.axis_index("c")


def _exchange(buf, scatter, name):
    def body(x_ref, out_ref, send_sems, recv_sems, local_sem):
        me = _my_index()
        mine = pltpu.make_async_copy(x_ref.at[me] if scatter else x_ref, out_ref.at[me], local_sem)
        mine.start()
        sends = []
        for k in range(1, N_DEV):
            x, y, c = _peer(k)
            cp = pltpu.make_async_remote_copy(
                src_ref=x_ref.at[4 * x + 2 * y + c] if scatter else x_ref, dst_ref=out_ref.at[me],
                send_sem=send_sems.at[k - 1], recv_sem=recv_sems.at[k - 1],
                device_id=(x, y, c), device_id_type=pl.DeviceIdType.MESH)
            cp.start()
            sends.append(cp)
        for k in range(1, N_DEV):
            x, y, c = _peer(k)
            landed = out_ref.at[4 * x + 2 * y + c]
            pltpu.make_async_remote_copy(
                src_ref=landed, dst_ref=landed, send_sem=send_sems.at[k - 1], recv_sem=recv_sems.at[k - 1],
                device_id=(x, y, c), device_id_type=pl.DeviceIdType.MESH).wait_recv()
        for cp in sends:
            cp.wait_send()
        mine.wait()

    return pl.pallas_call(
        body, name=name,
        in_specs=[pl.BlockSpec(memory_space=pl.ANY)], out_specs=pl.BlockSpec(memory_space=pl.ANY),
        out_shape=jax.ShapeDtypeStruct((N_DEV,) + buf.shape[-2:], buf.dtype),
        scratch_shapes=[pltpu.SemaphoreType.DMA((N_DEV - 1,)), pltpu.SemaphoreType.DMA((N_DEV - 1,)),
                        pltpu.SemaphoreType.DMA],
        compiler_params=pltpu.CompilerParams(has_side_effects=True))(buf)


def _sum_rows(parts, name):
    _, r, c = parts.shape
    tr = _tile(r, 256, PACK_ROW_MULT)

    def body(p_ref, o_ref):
        acc = p_ref[0].astype(F32)
        for q in range(1, N_DEV):
            acc = acc + p_ref[q].astype(F32)
        o_ref[...] = acc

    return pl.pallas_call(
        body, name=name, grid=(r // tr,),
        in_specs=[pl.BlockSpec((N_DEV, tr, c), lambda i: (0, i, 0))],
        out_specs=pl.BlockSpec((tr, c), lambda i: (i, 0)),
        out_shape=jax.ShapeDtypeStruct((r, c), F32),
        compiler_params=_params("parallel"))(parts)


def _adamw(w, g, m, v, name):
    r, c = w.shape
    tr = _tile(r, 256, 8)
    c1 = 1.0 - ADAM_B1 ** ADAM_STEP
    c2 = 1.0 - ADAM_B2 ** ADAM_STEP

    def body(w_ref, g_ref, m_ref, v_ref, d_ref, mo_ref, vo_ref):
        g = g_ref[...]
        m = ADAM_B1 * m_ref[...] + (1.0 - ADAM_B1) * g
        v = ADAM_B2 * v_ref[...] + (1.0 - ADAM_B2) * (g * g)
        d_ref[...] = -ADAM_LR * ((m / c1) / (jnp.sqrt(v / c2) + ADAM_EPS) + ADAM_WD * w_ref[...])
        mo_ref[...] = m
        vo_ref[...] = v

    blk = pl.BlockSpec((tr, c), lambda i: (i, 0))
    return pl.pallas_call(
        body, name=name, grid=(r // tr,), in_specs=[blk] * 4, out_specs=[blk] * 3,
        out_shape=[jax.ShapeDtypeStruct((r, c), F32)] * 3,
        compiler_params=_params("parallel"))(w, g, m, v)


def _pack_rows(flat, mult=PACK_ROW_MULT):
    n = flat.shape[-1]
    per = PACK_COLS * mult
    total = -(-n // per) * per
    flat = jnp.pad(flat, [(0, 0)] * (flat.ndim - 1) + [(0, total - n)])
    return flat.reshape(flat.shape[:-1] + (total // PACK_COLS, PACK_COLS))


def _f32_as_bf16_pairs(a):
    return lax.bitcast_convert_type(a, BF16).reshape(a.shape[:-1] + (2 * a.shape[-1],))


def _bf16_pairs_as_f32(a):
    return lax.bitcast_convert_type(a.reshape(a.shape[:-1] + (a.shape[-1] // 2, 2)), F32)


def _rope_tables(tp):
    pos = jnp.arange(tp, dtype=F32) - float(PAD_FRONT)
    inv = 1.0 / (ROPE_BASE ** jnp.linspace(0.0, 1.0, HEAD_DIM // 2, dtype=F32))
    ang = pos[:, None] * inv[None, :]
    cos = jnp.repeat(jnp.cos(ang), 2, axis=1)
    sin = jnp.repeat(jnp.sin(ang), 2, axis=1) * jnp.tile(jnp.array([-1.0, 1.0], F32), HEAD_DIM // 2)[None, :]
    return cos, sin


def _retention_tables(nh):
    log_gamma = jnp.log1p(-jnp.exp2(-5.0 - jnp.arange(nh, dtype=F32)))
    pos = jnp.arange(CHUNK, dtype=F32)
    causal = pos[:, None] >= pos[None, :]
    diff = pos[:, None] - pos[None, :]
    dec = jnp.where(causal, jnp.exp(jnp.where(causal, diff, 0.0) * log_gamma[:, None, None]), 0.0)
    ones = jnp.ones((1, 1, HEAD_DIM), F32)
    xi = jnp.exp((pos + 1.0)[None, :] * log_gamma[:, None])[:, :, None] * ones
    zeta = jnp.exp((CHUNK - 1.0 - pos)[None, :] * log_gamma[:, None])[:, :, None] * ones
    cd = jnp.exp(CHUNK * log_gamma)[:, None, None] * jnp.ones((1, 8, HEAD_DIM), F32)
    return dec, xi, zeta, cd


SHARDED = ("meta_tokens", "ffn1_w_in", "ffn1_w_out", "w_in", "gdn_conv_w", "w_branch_gdn", "w_branch_ret",
           "w_out", "ffn2_w_in", "ffn2_w_out")
COLUMN_SHARDED = ("meta_tokens", "ffn1_w_in", "w_in", "gdn_conv_w", "ffn2_w_in")
EXACT_F32 = ("meta_tokens", "gdn_conv_w")
REPLICATED = ("ffn1_norm", "mix_norm", "gdn_a_log", "gdn_dt_bias", "gdn_out_norm", "ret_out_norm", "ffn2_norm",
              "final_norm")
WEIGHTS = ("meta_tokens", "ffn1_norm", "ffn1_w_in", "ffn1_w_out", "mix_norm", "w_in", "gdn_conv_w", "gdn_a_log",
           "gdn_dt_bias", "gdn_out_norm", "ret_out_norm", "w_branch_gdn", "w_branch_ret", "w_out", "ffn2_norm",
           "ffn2_w_in", "ffn2_w_out", "final_norm")


def _as2d(a):
    if a.ndim == 3:
        return a[0]
    if a.ndim == 1:
        return a[None, :]
    return a


def _gather_weights(local):
    flats = []
    for name in SHARDED:
        a = local[name]
        flats.append(_f32_as_bf16_pairs(a).reshape(-1) if name in EXACT_F32 else a.astype(BF16).reshape(-1))
    sizes = [f.shape[0] for f in flats]
    packed = _pack_rows(jnp.concatenate(flats))
    gathered = _exchange(packed, False, "gather_weights").reshape(N_DEV, -1)
    full = {}
    off = 0
    for name, n in zip(SHARDED, sizes):
        piece = gathered[:, off:off + n]
        off += n
        r, c = local[name].shape
        if name in EXACT_F32:
            piece = _bf16_pairs_as_f32(piece)
        piece = piece.reshape(N_DEV, r, c)
        if name in COLUMN_SHARDED:
            full[name] = piece.transpose(1, 0, 2).reshape(r, N_DEV * c)
        else:
            full[name] = piece.reshape(N_DEV * r, c)
    return full


def _scatter_grads(grads, local):
    flats = []
    for name in SHARDED:
        g = grads[name]
        r, c = local[name].shape
        if name in COLUMN_SHARDED:
            g = g.reshape(r, N_DEV, c).transpose(1, 0, 2)
        flats.append(g.reshape(N_DEV, r * c).astype(BF16))
    payload = _pack_rows(jnp.concatenate(flats, axis=1))
    received = _exchange(payload, True, "scatter_grads")
    summed = _sum_rows(received, "sum_grads").reshape(-1)
    out = {}
    off = 0
    for name in SHARDED:
        r, c = local[name].shape
        out[name] = summed[off:off + r * c].reshape(r, c)
        off += r * c
    return out


def _allreduce_small(grads):
    names = [n for n in REPLICATED]
    flat = jnp.concatenate([grads[n].reshape(-1) for n in names])
    packed = _pack_rows(flat, 8)
    gathered = _exchange(packed, False, "gather_small_grads")
    summed = _sum_rows(gathered, "sum_small_grads").reshape(-1)
    out = {}
    off = 0
    for n in names:
        size = grads[n].size
        out[n] = summed[off:off + size].reshape(grads[n].shape)
        off += size
    return out


def _local_step(x, target, meta, w, rep):
    seq, d = x.shape
    tp = HEAD_ROWS + seq
    ff = w["ffn1_w_out"].shape[0]
    hv = w["w_branch_gdn"].shape[0]
    nh = hv // HEAD_DIM
    assert hv == d and seq % CHUNK == 0 and tp % HEAD_ROWS == 0

    win = w["w_in"]
    o_z = 3 * hv
    o_b = o_z + hv
    o_r = o_b + 2 * nh
    ref_cols = lambda a, b: win[:, a:b]
    wp = jnp.concatenate(
        [ref_cols(0, 3 * hv), ref_cols(o_r, o_r + 3 * hv), ref_cols(o_z, o_z + hv), ref_cols(o_r + 3 * hv, o_r + 6 * hv),
         ref_cols(o_b, o_b + 2 * nh), jnp.zeros((d, LANES - 2 * nh), BF16)], axis=1)

    pad_lanes = lambda row: jnp.pad(row, ((0, 0), (nh, LANES - 2 * nh)))
    alog = pad_lanes(rep["gdn_a_log"])
    dtb = pad_lanes(rep["gdn_dt_bias"])
    cos, sin = _rope_tables(tp)
    dec, xi, zeta, cd = _retention_tables(nh)

    h0 = jnp.concatenate([jnp.zeros((PAD_FRONT, d), F32), meta, x], axis=0)
    f1g, f1u, f1o = w["ffn1_w_in"][:, :ff], w["ffn1_w_in"][:, ff:], w["ffn1_w_out"]
    f2g, f2u, f2o = w["ffn2_w_in"][:, :ff], w["ffn2_w_in"][:, ff:], w["ffn2_w_out"]

    h1 = _ffn_fwd(h0, rep["ffn1_norm"], f1g, f1u, f1o, "ffn1_fwd")
    proj, n2 = _proj_fwd(h1, rep["mix_norm"], wp, "proj_fwd")
    qkv = _conv_fwd(proj, w["gdn_conv_w"], hv, "conv_fwd")
    oa, s_gdn = _gdn_fwd(qkv, proj, alog, dtb, nh, "gdn_fwd")
    ob, s_ret = _ret_fwd(proj, cos, sin, dec, xi, zeta, cd, nh, "ret_fwd")
    h2 = _post_fwd(oa, ob, proj, rep["gdn_out_norm"], rep["ret_out_norm"], w["w_branch_gdn"], w["w_branch_ret"],
                   w["w_out"], h1, "post_fwd")
    h3 = _ffn_fwd(h2, rep["ffn2_norm"], f2g, f2u, f2o, "ffn2_fwd")
    loss_row, dh3, d_final = _final(h3, rep["final_norm"], target, "final")

    dh2, d_f2n, n3, hid2, dag2, dau2 = _ffn_bwd(h2, dh3, rep["ffn2_norm"], f2g, f2u, f2o, "ffn2_bwd")
    g_f2in = jnp.concatenate([_matmul_tn(n3, dag2, "ffn2_dwg"), _matmul_tn(n3, dau2, "ffn2_dwu")], axis=1)
    g_f2out = _matmul_tn(hid2, dh3, "ffn2_dwo", 0.5)

    doa, dob, dgate, ya, yb, merged, dpa, dpb, d_gn, d_rn = _post_bwd(
        oa, ob, proj, rep["gdn_out_norm"], rep["ret_out_norm"], w["w_branch_gdn"], w["w_branch_ret"], w["w_out"],
        dh2, "post_bwd")
    g_wbg = _matmul_tn(ya, dpa, "dw_branch_gdn")
    g_wbr = _matmul_tn(yb, dpb, "dw_branch_ret")
    g_wo = _matmul_tn(merged, dh2, "dw_out")

    d_ret = _ret_bwd(proj, cos, sin, dec, xi, zeta, cd, s_ret, dob, nh, "ret_bwd")
    gdn_grads = _gdn_bwd(qkv, proj, alog, dtb, s_gdn, doa, nh, "gdn_bwd")
    dba, d_alog, d_dtb = gdn_grads[3:]
    dpre, g_conv = [], []
    for grp, tag in enumerate("qkv"):
        dc, dw = _conv_bwd_pre(proj, w["gdn_conv_w"], gdn_grads[grp], grp, hv, "conv_bwd_pre_" + tag)
        dpre.append(_conv_bwd_in(dc, w["gdn_conv_w"], grp, "conv_bwd_in_" + tag))
        g_conv.append(dw)
    g_conv = jnp.concatenate(g_conv, axis=1)

    wide = dpre + list(d_ret) + [dgate]
    dn2 = _matmul_nt_parts(wide, wp[:, :10 * hv], None, "dn2_wide")
    dn2 = _matmul_nt_parts([dba], wp[:, 10 * hv:], dn2, "dn2_beta_alpha")
    g_wp = [_matmul_tn(n2, dg, "dw_in_%d" % idx) for idx, dg in enumerate(wide + [dba])]
    dh1, d_mixn = _norm_bwd(h1, rep["mix_norm"], dn2, dh2, "mix_norm_bwd")

    dh0, d_f1n, n1, hid1, dag1, dau1 = _ffn_bwd(h0, dh1, rep["ffn1_norm"], f1g, f1u, f1o, "ffn1_bwd")
    g_f1in = jnp.concatenate([_matmul_tn(n1, dag1, "ffn1_dwg"), _matmul_tn(n1, dau1, "ffn1_dwu")], axis=1)
    g_f1out = _matmul_tn(hid1, dh1, "ffn1_dwo", 0.5)

    g_gate, g_ba = g_wp[6], g_wp[7]
    g_win = jnp.concatenate(g_wp[0:3] + [g_gate[:, :hv], g_ba[:, :2 * nh]] + g_wp[3:6] + [g_gate[:, hv:]], axis=1)
    grads = {"meta_tokens": dh0[PAD_FRONT:HEAD_ROWS], "ffn1_w_in": g_f1in, "ffn1_w_out": g_f1out, "w_in": g_win,
             "gdn_conv_w": g_conv, "w_branch_gdn": g_wbg, "w_branch_ret": g_wbr, "w_out": g_wo,
             "ffn2_w_in": g_f2in, "ffn2_w_out": g_f2out}
    small = {"ffn1_norm": d_f1n, "mix_norm": d_mixn, "gdn_a_log": d_alog[:, nh:2 * nh],
             "gdn_dt_bias": d_dtb[:, nh:2 * nh], "gdn_out_norm": d_gn, "ret_out_norm": d_rn, "ffn2_norm": d_f2n,
             "final_norm": d_final}
    return loss_row[0, 0], dh0[HEAD_ROWS:], grads, small


def kernel(x, meta_tokens, ffn1_norm, ffn1_w_in, ffn1_w_out, mix_norm, w_in, gdn_conv_w, gdn_a_log, gdn_dt_bias, gdn_out_norm, ret_out_norm, w_branch_gdn, w_branch_ret, w_out, ffn2_norm, ffn2_w_in, ffn2_w_out, final_norm, loss_target, m_meta_tokens, m_ffn1_norm, m_ffn1_w_in, m_ffn1_w_out, m_mix_norm, m_w_in, m_gdn_conv_w, m_gdn_a_log, m_gdn_dt_bias, m_gdn_out_norm, m_ret_out_norm, m_w_branch_gdn, m_w_branch_ret, m_w_out, m_ffn2_norm, m_ffn2_w_in, m_ffn2_w_out, m_final_norm, v_meta_tokens, v_ffn1_norm, v_ffn1_w_in, v_ffn1_w_out, v_mix_norm, v_w_in, v_gdn_conv_w, v_gdn_a_log, v_gdn_dt_bias, v_gdn_out_norm, v_ret_out_norm, v_w_branch_gdn, v_w_branch_ret, v_w_out, v_ffn2_norm, v_ffn2_w_in, v_ffn2_w_out, v_final_norm):
    given = dict(locals())
    params = {n: _as2d(given[n]) for n in WEIGHTS}
    local = {n: params[n] for n in SHARDED}
    rep = {n: params[n] for n in REPLICATED}

    full = _gather_weights(local)
    loss_sum, grad_x, grads, small = _local_step(x[0], loss_target[0], full.pop("meta_tokens"), full, rep)
    g = _scatter_grads(grads, local)
    g.update(_allreduce_small(small))
    loss = lax.psum(loss_sum, ("x", "y", "c"))

    outs = {}
    for n in WEIGHTS:
        delta, new_m, new_v = _adamw(params[n], g[n], _as2d(given["m_" + n]), _as2d(given["v_" + n]), "adamw_" + n)
        shape = given[n].shape
        outs[n] = (g[n].reshape(shape), delta.reshape(shape), new_m.reshape(shape), new_v.reshape(shape))
    return (loss, grad_x[None], *[outs[n][0] for n in WEIGHTS], *[outs[n][1] for n in WEIGHTS],
            *[outs[n][2] for n in WEIGHTS], *[outs[n][3] for n in WEIGHTS])
```

```python
import functools
import math

import numpy as np
import jax
import jax.numpy as jnp
from jax import lax
from jax.experimental import pallas as pl
from jax.experimental.pallas import tpu as pltpu

F32 = jnp.float32
BF16 = jnp.bfloat16

N_DEV = 8
N_META = 16
CHUNK = 64
HEAD_DIM = 128
CONV_K = 4
ROPE_BASE = 10000.0
EPS = 1e-6
PAD_FRONT = 240
HEAD_ROWS = PAD_FRONT + N_META
LANES = 128
VMEM_LIMIT_BYTES = 56 * 1024 * 1024

ADAM_LR = 0.001
ADAM_B1 = 0.9
ADAM_B2 = 0.999
ADAM_EPS = 1e-08
ADAM_WD = 0.01
ADAM_STEP = 10

NN = (((1,), (0,)), ((), ()))
NT = (((1,), (1,)), ((), ()))
TN = (((0,), (0,)), ((), ()))


def _tile(n, target, mult):
    best = 0
    for t in range(mult, min(n, target) + 1, mult):
        if n % t == 0:
            best = t
    return best if best else n


def _params(*semantics):
    return pltpu.CompilerParams(dimension_semantics=semantics, vmem_limit_bytes=VMEM_LIMIT_BYTES)


def _raw_dot(a, b, dims, hi):
    if hi:
        return lax.dot_general(a, b, dims, precision=lax.Precision.HIGHEST, preferred_element_type=F32)
    return lax.dot_general(a.astype(BF16), b.astype(BF16), dims, preferred_element_type=F32)


def _make_mm(hi):
    @jax.custom_vjp
    def nn(a, b):
        return _raw_dot(a, b, NN, hi)

    @jax.custom_vjp
    def nt(a, b):
        return _raw_dot(a, b, NT, hi)

    @jax.custom_vjp
    def tn(a, b):
        return _raw_dot(a, b, TN, hi)

    nn.defvjp(lambda a, b: (_raw_dot(a, b, NN, hi), (a, b)),
              lambda r, g: (_raw_dot(g, r[1], NT, hi), _raw_dot(r[0], g, TN, hi)))
    nt.defvjp(lambda a, b: (_raw_dot(a, b, NT, hi), (a, b)),
              lambda r, g: (_raw_dot(g, r[1], NN, hi), _raw_dot(g, r[0], TN, hi)))
    tn.defvjp(lambda a, b: (_raw_dot(a, b, TN, hi), (a, b)),
              lambda r, g: (_raw_dot(r[1], g, NT, hi), _raw_dot(r[0], g, NN, hi)))
    return nn, nt, tn


def _silu(x):
    return x * jax.nn.sigmoid(x)


def _rms_parts(x):
    r = lax.rsqrt(jnp.mean(x * x, axis=-1, keepdims=True) + EPS)
    return x * r, r


def _rms_bwd(dy, xh, r, gain):
    dxh = dy * gain
    dx = r * (dxh - xh * jnp.mean(dxh * xh, axis=-1, keepdims=True))
    return dx, jnp.sum(dy * xh, axis=0, keepdims=True)


def _ffn_specs(tm, d, tf, nj):
    return [pl.BlockSpec((tm, d), lambda i, j: (i, 0)), pl.BlockSpec((1, d), lambda i, j: (0, 0)),
            pl.BlockSpec((1, d, tf), lambda i, j: (j, 0, 0)), pl.BlockSpec((1, d, tf), lambda i, j: (nj + j, 0, 0)),
            pl.BlockSpec((tf, d), lambda i, j: (j, 0))]


def _ffn_fwd(h, gain, w_in, wo, name):
    tp, d = h.shape
    tf = w_in.shape[2]
    nj = w_in.shape[0] // 2
    tm = _tile(tp, 768, 8)
    row, vec, wg_spec, wu_spec, wo_spec = _ffn_specs(tm, d, tf, nj)

    def body(h_ref, g_ref, wg3_ref, wu3_ref, wo_ref, o_ref, n_sc, acc_sc):
        wg_ref, wu_ref = wg3_ref.at[0], wu3_ref.at[0]
        j = pl.program_id(1)

        @pl.when(j == 0)
        def _():
            xh, _ = _rms_parts(h_ref[...])
            n_sc[...] = (xh * g_ref[...]).astype(BF16)
            acc_sc[...] = jnp.zeros_like(acc_sc)

        n = n_sc[...]
        a_g = jnp.dot(n, wg_ref[...], preferred_element_type=F32)
        a_u = jnp.dot(n, wu_ref[...], preferred_element_type=F32)
        hid = (_silu(a_g) * a_u).astype(BF16)
        acc_sc[...] += jnp.dot(hid, wo_ref[...], preferred_element_type=F32)

        @pl.when(j == nj - 1)
        def _():
            o_ref[...] = h_ref[...] + 0.5 * acc_sc[...]

    return pl.pallas_call(
        body, name=name, grid=(tp // tm, nj),
        in_specs=[row, vec, wg_spec, wu_spec, wo_spec], out_specs=row,
        out_shape=jax.ShapeDtypeStruct((tp, d), F32),
        scratch_shapes=[pltpu.VMEM((tm, d), BF16), pltpu.VMEM((tm, d), F32)],
        compiler_params=_params("parallel", "arbitrary"))(h, gain, w_in, w_in, wo)


def _ffn_bwd(h, dho, gain, w_in, wo, name):
    tp, d = h.shape
    tf = w_in.shape[2]
    nj = w_in.shape[0] // 2
    tm = _tile(tp, 384, 8)
    ni = tp // tm
    row, vec, wg_spec, wu_spec, wo_spec = _ffn_specs(tm, d, tf, nj)

    def body(h_ref, dho_ref, g_ref, wg3_ref, wu3_ref, wo_ref,
             dh_ref, dgain_ref, n_ref, hid3_ref, dag3_ref, dau3_ref, dn_sc, dhb_sc):
        wg_ref, wu_ref = wg3_ref.at[0], wu3_ref.at[0]
        hid_ref, dag_ref, dau_ref = hid3_ref.at[0], dag3_ref.at[0], dau3_ref.at[0]
        i, j = pl.program_id(0), pl.program_id(1)

        @pl.when(j == 0)
        def _():
            xh, _ = _rms_parts(h_ref[...])
            n_ref[...] = (xh * g_ref[...]).astype(BF16)
            dn_sc[...] = jnp.zeros_like(dn_sc)
            dhb_sc[...] = (0.5 * dho_ref[...]).astype(BF16)

        @pl.when((i == 0) & (j == 0))
        def _():
            dgain_ref[...] = jnp.zeros_like(dgain_ref)

        n = n_ref[...]
        a_g = jnp.dot(n, wg_ref[...], preferred_element_type=F32)
        a_u = jnp.dot(n, wu_ref[...], preferred_element_type=F32)
        sg = jax.nn.sigmoid(a_g)
        s = a_g * sg
        hid_ref[...] = (s * a_u).astype(BF16)
        d_hid = lax.dot_general(dhb_sc[...], wo_ref[...], NT, preferred_element_type=F32)
        d_au = (d_hid * s).astype(BF16)
        d_ag = (d_hid * a_u * (sg * (1.0 + a_g * (1.0 - sg)))).astype(BF16)
        dau_ref[...] = d_au
        dag_ref[...] = d_ag
        dn_sc[...] += (lax.dot_general(d_ag, wg_ref[...], NT, preferred_element_type=F32)
                       + lax.dot_general(d_au, wu_ref[...], NT, preferred_element_type=F32))

        @pl.when(j == nj - 1)
        def _():
            xh, r = _rms_parts(h_ref[...])
            dx, dg = _rms_bwd(dn_sc[...], xh, r, g_ref[...])
            dh_ref[...] = dho_ref[...] + dx
            dgain_ref[...] += dg

    act = pl.BlockSpec((1, tm, tf), lambda i, j: (j, i, 0))
    return pl.pallas_call(
        body, name=name, grid=(ni, nj),
        in_specs=[row, row, vec, wg_spec, wu_spec, wo_spec],
        out_specs=[row, vec, row, act, act, act],
        out_shape=[jax.ShapeDtypeStruct((tp, d), F32), jax.ShapeDtypeStruct((1, d), F32),
                   jax.ShapeDtypeStruct((tp, d), BF16)] + [jax.ShapeDtypeStruct((nj, tp, tf), BF16)] * 3,
        scratch_shapes=[pltpu.VMEM((tm, d), F32), pltpu.VMEM((tm, d), BF16)],
        compiler_params=_params("arbitrary", "arbitrary"))(h, dho, gain, w_in, w_in, wo)


def _matmul_tn(a, b, name, scale=1.0):
    t, m = a.shape
    n = b.shape[1]
    bm = _tile(m, 1024, LANES)
    bn = _tile(n, 1536, LANES)
    tk = _tile(t, 768, 16)
    nk = t // tk

    def body(a_ref, b_ref, o_ref):
        k = pl.program_id(2)

        @pl.when(k == 0)
        def _():
            o_ref[...] = jnp.zeros_like(o_ref)

        o_ref[...] += lax.dot_general(a_ref[...].astype(BF16), b_ref[...].astype(BF16), TN,
                                      preferred_element_type=F32)

        if scale != 1.0:
            @pl.when(k == nk - 1)
            def _():
                o_ref[...] = o_ref[...] * scale

    return pl.pallas_call(
        body, name=name, grid=(m // bm, n // bn, nk),
        in_specs=[pl.BlockSpec((tk, bm), lambda i, j, k: (k, i)), pl.BlockSpec((tk, bn), lambda i, j, k: (k, j))],
        out_specs=pl.BlockSpec((bm, bn), lambda i, j, k: (i, j)),
        out_shape=jax.ShapeDtypeStruct((m, n), F32),
        compiler_params=_params("parallel", "parallel", "arbitrary"))(a, b)


def _matmul_tn_blocks(a, b, name, scale=1.0):
    a_blocked = a.ndim == 3
    nb, t = (a.shape[0], a.shape[1]) if a_blocked else (b.shape[0], b.shape[1])
    m, n = a.shape[-1], b.shape[-1]
    tk = _tile(t, 768, 16)
    nk = t // tk
    if a_blocked:
        bo = _tile(n, 1024, LANES)
        a_spec = pl.BlockSpec((1, tk, m), lambda p, o, k: (p, k, 0))
        b_spec = pl.BlockSpec((tk, bo), lambda p, o, k: (k, o))
        o_spec = pl.BlockSpec((m, bo), lambda p, o, k: (p, o))
        out_shape = jax.ShapeDtypeStruct((nb * m, n), F32)
        grid = (nb, n // bo, nk)
    else:
        bo = _tile(m, 1024, LANES)
        a_spec = pl.BlockSpec((tk, bo), lambda p, o, k: (k, o))
        b_spec = pl.BlockSpec((1, tk, n), lambda p, o, k: (p, k, 0))
        o_spec = pl.BlockSpec((1, bo, n), lambda p, o, k: (p, o, 0))
        out_shape = jax.ShapeDtypeStruct((nb, m, n), F32)
        grid = (nb, m // bo, nk)

    def body(a_ref, b_ref, o_ref):
        k = pl.program_id(2)
        a_blk = a_ref[0] if a_blocked else a_ref[...]
        b_blk = b_ref[...] if a_blocked else b_ref[0]
        part = lax.dot_general(a_blk.astype(BF16), b_blk.astype(BF16), TN, preferred_element_type=F32)
        out = o_ref if a_blocked else o_ref.at[0]

        @pl.when(k == 0)
        def _():
            out[...] = part

        @pl.when(k > 0)
        def _():
            out[...] += part

        if scale != 1.0:
            @pl.when(k == nk - 1)
            def _():
                out[...] = out[...] * scale

    return pl.pallas_call(
        body, name=name, grid=grid, in_specs=[a_spec, b_spec], out_specs=o_spec, out_shape=out_shape,
        compiler_params=_params("parallel", "parallel", "arbitrary"))(a, b)


def _matmul_nt_parts(parts, w, acc, name):
    t = parts[0].shape[0]
    d = w.shape[0]
    widths = [p.shape[1] for p in parts]
    tk = _tile(math.gcd(*widths), 1024, LANES)
    counts = [wd // tk for wd in widths]
    starts = [sum(counts[:g]) for g in range(len(parts))]
    nk = sum(counts)
    tm = _tile(t, 768, 8)
    n_parts = len(parts)

    def body(*refs):
        a_refs, w_ref, o_ref = refs[:n_parts], refs[n_parts], refs[-1]
        k = pl.program_id(1)

        @pl.when(k == 0)
        def _():
            o_ref[...] = jnp.zeros_like(o_ref) if acc is None else refs[n_parts + 1][...]

        for g in range(n_parts):
            @pl.when((k >= starts[g]) & (k < starts[g] + counts[g]))
            def _(g=g):
                o_ref[...] += lax.dot_general(a_refs[g][...].astype(BF16), w_ref[...], NT,
                                              preferred_element_type=F32)

    in_specs = [pl.BlockSpec((tm, tk), lambda i, k, lo=starts[g], nb=counts[g]: (i, jnp.clip(k - lo, 0, nb - 1)))
                for g in range(n_parts)]
    in_specs.append(pl.BlockSpec((d, tk), lambda i, k: (0, k)))
    args = list(parts) + [w]
    if acc is not None:
        in_specs.append(pl.BlockSpec((tm, d), lambda i, k: (i, 0)))
        args.append(acc)
    return pl.pallas_call(
        body, name=name, grid=(t // tm, nk), in_specs=in_specs,
        out_specs=pl.BlockSpec((tm, d), lambda i, k: (i, 0)),
        out_shape=jax.ShapeDtypeStruct((t, d), F32),
        compiler_params=_params("parallel", "arbitrary"))(*args)


def _proj_fwd(h, gain, wp, name):
    tp, d = h.shape
    npad = wp.shape[1]
    tm = _tile(tp, 768, 8)
    tn = _tile(npad, 1152, LANES)

    def body(h_ref, g_ref, w_ref, o_ref, n_ref):
        @pl.when(pl.program_id(1) == 0)
        def _():
            xh, _ = _rms_parts(h_ref[...])
            n_ref[...] = (xh * g_ref[...]).astype(BF16)

        o_ref[...] = jnp.dot(n_ref[...], w_ref[...], preferred_element_type=F32)

    return pl.pallas_call(
        body, name=name, grid=(tp // tm, npad // tn),
        in_specs=[pl.BlockSpec((tm, d), lambda i, j: (i, 0)), pl.BlockSpec((1, d), lambda i, j: (0, 0)),
                  pl.BlockSpec((d, tn), lambda i, j: (0, j))],
        out_specs=[pl.BlockSpec((tm, tn), lambda i, j: (i, j)), pl.BlockSpec((tm, d), lambda i, j: (i, 0))],
        out_shape=[jax.ShapeDtypeStruct((tp, npad), F32), jax.ShapeDtypeStruct((tp, d), BF16)],
        compiler_params=_params("parallel", "arbitrary"))(h, gain, wp)


def _norm_bwd(h, gain, dn, dres, name):
    tp, d = h.shape
    tm = _tile(tp, 256, 8)

    def body(h_ref, g_ref, dn_ref, dres_ref, dh_ref, dgain_ref):
        @pl.when(pl.program_id(0) == 0)
        def _():
            dgain_ref[...] = jnp.zeros_like(dgain_ref)

        xh, r = _rms_parts(h_ref[...])
        dx, dg = _rms_bwd(dn_ref[...], xh, r, g_ref[...])
        dh_ref[...] = dres_ref[...] + dx
        dgain_ref[...] += dg

    row = pl.BlockSpec((tm, d), lambda i: (i, 0))
    vec = pl.BlockSpec((1, d), lambda i: (0, 0))
    return pl.pallas_call(
        body, name=name, grid=(tp // tm,), in_specs=[row, vec, row, row], out_specs=[row, vec],
        out_shape=[jax.ShapeDtypeStruct((tp, d), F32), jax.ShapeDtypeStruct((1, d), F32)],
        compiler_params=_params("arbitrary"))(h, gain, dn, dres)


def _head_post(a, grp):
    a = _silu(a)
    r = lax.rsqrt(jnp.sum(a * a, axis=-1, keepdims=True) + EPS)
    if isinstance(grp, int):
        return a if grp == 2 else a * r * (HEAD_DIM ** -0.5 if grp == 0 else 1.0)
    scale = jnp.where(grp == 0, HEAD_DIM ** -0.5, 1.0).astype(F32)
    return jnp.where(grp == 2, a, a * r * scale)


def _conv_taps(ext_sc, w_ref, tm):
    c = None
    for i in range(CONV_K):
        s = CONV_K - 1 - i
        term = w_ref[i:i + 1, :] * ext_sc[8 - s:8 - s + tm, :]
        c = term if c is None else c + term
    return c


def _conv_fwd(proj, conv_w, hv, name):
    tp = proj.shape[0]
    tm = _tile(tp, 256, 8)
    nh = hv // HEAD_DIM

    def body(x_ref, halo_ref, w_ref, o_ref, ext_sc):
        i, grp = pl.program_id(0), pl.program_id(1)
        ext_sc[0:8, :] = jnp.where(i == 0, 0.0, halo_ref[...])
        ext_sc[8:, :] = x_ref[...]
        c = _conv_taps(ext_sc, w_ref, tm)
        for h in range(nh):
            sl = slice(h * HEAD_DIM, (h + 1) * HEAD_DIM)
            o_ref[:, sl] = _head_post(c[:, sl], grp)

    return pl.pallas_call(
        body, name=name, grid=(tp // tm, 3),
        in_specs=[pl.BlockSpec((tm, hv), lambda i, g: (i, g)),
                  pl.BlockSpec((8, hv), lambda i, g: (jnp.maximum(i * (tm // 8) - 1, 0), g)),
                  pl.BlockSpec((CONV_K, hv), lambda i, g: (0, g))],
        out_specs=pl.BlockSpec((tm, hv), lambda i, g: (i, g)),
        out_shape=jax.ShapeDtypeStruct((tp, 3 * hv), F32),
        scratch_shapes=[pltpu.VMEM((tm + 8, hv), F32)],
        compiler_params=_params("parallel", "arbitrary"))(proj, proj, conv_w)


def _conv_bwd_pre(proj, conv_w, dy, grp, hv, name):
    tp = proj.shape[0]
    tm = _tile(tp, 256, 8)
    nh = hv // HEAD_DIM

    def body(x_ref, halo_ref, w_ref, dy_ref, dc_ref, dw_ref, ext_sc):
        i = pl.program_id(0)
        ext_sc[0:8, :] = jnp.where(i == 0, 0.0, halo_ref[...])
        ext_sc[8:, :] = x_ref[...]
        c = _conv_taps(ext_sc, w_ref, tm)
        for h in range(nh):
            sl = slice(h * HEAD_DIM, (h + 1) * HEAD_DIM)
            _, vjp = jax.vjp(lambda a: _head_post(a, grp), c[:, sl])
            dc_ref[:, sl] = vjp(dy_ref[:, sl])[0]

        @pl.when(i == 0)
        def _():
            dw_ref[...] = jnp.zeros_like(dw_ref)

        dc = dc_ref[...]
        for k in range(CONV_K):
            s = CONV_K - 1 - k
            dw_ref[k:k + 1, :] += jnp.sum(dc * ext_sc[8 - s:8 - s + tm, :], axis=0, keepdims=True)

    return pl.pallas_call(
        body, name=name, grid=(tp // tm,),
        in_specs=[pl.BlockSpec((tm, hv), lambda i: (i, grp)),
                  pl.BlockSpec((8, hv), lambda i: (jnp.maximum(i * (tm // 8) - 1, 0), grp)),
                  pl.BlockSpec((CONV_K, hv), lambda i: (0, grp)),
                  pl.BlockSpec((tm, hv), lambda i: (i, 0))],
        out_specs=[pl.BlockSpec((tm, hv), lambda i: (i, 0)), pl.BlockSpec((CONV_K, hv), lambda i: (0, 0))],
        out_shape=[jax.ShapeDtypeStruct((tp, hv), F32), jax.ShapeDtypeStruct((CONV_K, hv), F32)],
        scratch_shapes=[pltpu.VMEM((tm + 8, hv), F32)],
        compiler_params=_params("arbitrary"))(proj, proj, conv_w, dy)


def _conv_bwd_in(dc, conv_w, grp, name):
    tp, hv = dc.shape
    tm = _tile(tp, 256, 8)
    ni = tp // tm

    def body(dc_ref, halo_ref, w_ref, dx_ref, ext_sc):
        i = pl.program_id(0)
        ext_sc[0:tm, :] = dc_ref[...]
        ext_sc[tm:, :] = jnp.where(i == ni - 1, 0.0, halo_ref[...])
        dx = None
        for k in range(CONV_K):
            s = CONV_K - 1 - k
            term = w_ref[k:k + 1, :] * ext_sc[s:s + tm, :]
            dx = term if dx is None else dx + term
        dx_ref[...] = dx.astype(BF16)

    return pl.pallas_call(
        body, name=name, grid=(ni,),
        in_specs=[pl.BlockSpec((tm, hv), lambda i: (i, 0)),
                  pl.BlockSpec((8, hv), lambda i: (jnp.minimum((i + 1) * (tm // 8), tp // 8 - 1), 0)),
                  pl.BlockSpec((CONV_K, hv), lambda i: (0, grp))],
        out_specs=pl.BlockSpec((tm, hv), lambda i: (i, 0)),
        out_shape=jax.ShapeDtypeStruct((tp, hv), BF16),
        scratch_shapes=[pltpu.VMEM((tm + 8, hv), F32)],
        compiler_params=_params("parallel"))(dc, dc, conv_w)


def _gdn_gates(ba, alog, dtb):
    hi_nn, _, _ = _make_mm(True)
    c = ba.shape[0]
    tril = (lax.broadcasted_iota(jnp.int32, (c, c), 0) >= lax.broadcasted_iota(jnp.int32, (c, c), 1)).astype(F32)
    x = ba + dtb
    softplus = jnp.maximum(x, 0.0) + jnp.log1p(jnp.exp(-jnp.abs(x)))
    return hi_nn(tril, -jnp.exp(alog) * softplus), jax.nn.sigmoid(ba)


def _gdn_heads(states, qs, ks, vs, gc_all, beta_all):
    mm_nn, mm_nt, mm_tn = _make_mm(False)
    hi_nn, hi_nt, _ = _make_mm(True)
    nh = len(qs)
    heads = range(nh)
    c = qs[0].shape[0]
    lane = lax.broadcasted_iota(jnp.int32, (c, LANES), 1)
    last_row = (lax.broadcasted_iota(jnp.int32, (c, 1), 0) == c - 1).astype(F32)
    ri = lax.broadcasted_iota(jnp.int32, (c, c), 0)
    ci = lax.broadcasted_iota(jnp.int32, (c, c), 1)
    causal = ri >= ci
    strict = ri > ci
    eye = (ri == ci).astype(F32)
    sel_a = [(lane == nh + h).astype(F32) for h in heads]
    sel_b = [(lane == h).astype(F32) for h in heads]

    gcol = [jnp.sum(gc_all * sel_a[h], axis=1, keepdims=True) for h in heads]
    grow = [hi_nt(sel_a[h], gc_all) for h in heads]
    beta = [jnp.sum(beta_all * sel_b[h], axis=1, keepdims=True) for h in heads]
    decay = [jnp.where(causal, jnp.exp(jnp.where(causal, gcol[h] - grow[h], 0.0)), 0.0) for h in heads]
    kb = [ks[h] * beta[h] for h in heads]
    kk = [mm_nt(kb[h], ks[h]) for h in heads]
    qk = [mm_nt(qs[h], ks[h]) for h in heads]
    xp = [-jnp.where(strict, kk[h] * decay[h], 0.0) for h in heads]
    t_inv = [eye + xp[h] for h in heads]
    for _ in range(int(math.log2(c)) - 1):
        xp = [hi_nn(xp[h], xp[h]) for h in heads]
        t_inv = [t_inv[h] + hi_nn(t_inv[h], xp[h]) for h in heads]
    eg = [jnp.exp(gcol[h]) for h in heads]
    u = [hi_nn(t_inv[h], vs[h] * beta[h]) for h in heads]
    w = [hi_nn(t_inv[h], kb[h] * eg[h]) for h in heads]
    qk = [qk[h] * decay[h] for h in heads]
    glast = [jnp.sum(gcol[h] * last_row, axis=0, keepdims=True) for h in heads]
    ws = [mm_nn(w[h], states[h]) for h in heads]
    qs_state = [mm_nn(qs[h] * eg[h], states[h]) for h in heads]
    v_new = [u[h] - ws[h] for h in heads]
    intra = [mm_nn(qk[h], v_new[h]) for h in heads]
    kv = [mm_tn(ks[h] * jnp.exp(glast[h] - gcol[h]), v_new[h]) for h in heads]
    outs = [qs_state[h] + intra[h] for h in heads]
    new_states = [states[h] * jnp.exp(glast[h]) + kv[h] for h in heads]
    return outs, new_states


def _scan_specs(nh, nc, rev, first_col):
    cidx = (lambda c: nc - 1 - c) if rev else (lambda c: c)
    hv = nh * HEAD_DIM
    cols = [pl.BlockSpec((CHUNK, hv), lambda c, g=g: (cidx(c), first_col + g)) for g in range(3)]
    st = pl.BlockSpec((1, nh, HEAD_DIM, HEAD_DIM), lambda c: (cidx(c), 0, 0, 0))
    act = pl.BlockSpec((CHUNK, hv), lambda c: (cidx(c), 0))
    return cols, st, act


def _gdn_fwd(qkv, proj, alog, dtb, nh, name):
    tp = qkv.shape[0]
    nc = tp // CHUNK

    def body(q_ref, k_ref, v_ref, ba_ref, al_ref, dt_ref, o_ref, st_ref, s_sc):
        @pl.when(pl.program_id(0) == 0)
        def _():
            s_sc[...] = jnp.zeros_like(s_sc)

        gc_all, beta_all = _gdn_gates(ba_ref[...], al_ref[...], dt_ref[...])
        sls = [slice(h * HEAD_DIM, (h + 1) * HEAD_DIM) for h in range(nh)]
        states = [s_sc[h] for h in range(nh)]
        for h in range(nh):
            st_ref[0, h] = states[h]
        outs, new_states = _gdn_heads(states, [q_ref[:, sl] for sl in sls], [k_ref[:, sl] for sl in sls],
                                      [v_ref[:, sl] for sl in sls], gc_all, beta_all)
        for h in range(nh):
            o_ref[:, sls[h]] = outs[h]
            s_sc[h] = new_states[h]

    cols, st, act = _scan_specs(nh, nc, False, 0)
    ba = pl.BlockSpec((CHUNK, LANES), lambda c: (c, 10 * nh))
    vec = pl.BlockSpec((1, LANES), lambda c: (0, 0))
    return pl.pallas_call(
        body, name=name, grid=(nc,), in_specs=cols + [ba, vec, vec], out_specs=[act, st],
        out_shape=[jax.ShapeDtypeStruct((tp, nh * HEAD_DIM), F32),
                   jax.ShapeDtypeStruct((nc, nh, HEAD_DIM, HEAD_DIM), F32)],
        scratch_shapes=[pltpu.VMEM((nh, HEAD_DIM, HEAD_DIM), F32)],
        compiler_params=_params("arbitrary"))(qkv, qkv, qkv, proj, alog, dtb)


def _gdn_bwd(qkv, proj, alog, dtb, states, do, nh, name):
    tp = qkv.shape[0]
    nc = tp // CHUNK

    def body(q_ref, k_ref, v_ref, ba_ref, al_ref, dt_ref, st_ref, do_ref,
             dq_ref, dk_ref, dv_ref, dba_ref, dal_ref, ddt_ref, ds_sc):
        @pl.when(pl.program_id(0) == 0)
        def _():
            ds_sc[...] = jnp.zeros_like(ds_sc)
            dal_ref[...] = jnp.zeros_like(dal_ref)
            ddt_ref[...] = jnp.zeros_like(ddt_ref)

        (gc_all, beta_all), gates_vjp = jax.vjp(_gdn_gates, ba_ref[...], al_ref[...], dt_ref[...])
        sls = [slice(h * HEAD_DIM, (h + 1) * HEAD_DIM) for h in range(nh)]
        _, vjp = jax.vjp(_gdn_heads, [st_ref[0, h] for h in range(nh)], [q_ref[:, sl] for sl in sls],
                         [k_ref[:, sl] for sl in sls], [v_ref[:, sl] for sl in sls], gc_all, beta_all)
        ds, dq, dk, dv, dgc, dbeta = vjp(([do_ref[:, sl] for sl in sls], [ds_sc[h] for h in range(nh)]))
        for h in range(nh):
            ds_sc[h] = ds[h]
            dq_ref[:, sls[h]] = dq[h]
            dk_ref[:, sls[h]] = dk[h]
            dv_ref[:, sls[h]] = dv[h]
        dba, dal, ddt = gates_vjp((dgc, dbeta))
        dba_ref[...] = dba
        dal_ref[...] += dal
        ddt_ref[...] += ddt

    cols, st, act = _scan_specs(nh, nc, True, 0)
    ba = pl.BlockSpec((CHUNK, LANES), lambda c: (nc - 1 - c, 10 * nh))
    vec = pl.BlockSpec((1, LANES), lambda c: (0, 0))
    return pl.pallas_call(
        body, name=name, grid=(nc,), in_specs=cols + [ba, vec, vec, st, act],
        out_specs=[act, act, act, pl.BlockSpec((CHUNK, LANES), lambda c: (nc - 1 - c, 0)), vec, vec],
        out_shape=[jax.ShapeDtypeStruct((tp, nh * HEAD_DIM), F32)] * 3
                  + [jax.ShapeDtypeStruct((tp, LANES), F32), jax.ShapeDtypeStruct((1, LANES), F32),
                     jax.ShapeDtypeStruct((1, LANES), F32)],
        scratch_shapes=[pltpu.VMEM((nh, HEAD_DIM, HEAD_DIM), F32)],
        compiler_params=_params("arbitrary"))(qkv, qkv, qkv, proj, alog, dtb, states, do)


def _swap_pairs(t):
    lane = lax.broadcasted_iota(jnp.int32, t.shape, 1)
    n = t.shape[1]
    return jnp.where(lane % 2 == 0, pltpu.roll(t, n - 1, 1), pltpu.roll(t, 1, 1))


def _rot(t, cos, sin_signed):
    return t * cos + _swap_pairs(t) * sin_signed


def _rot_t(dt, cos, sin_signed):
    return dt * cos + _swap_pairs(dt * sin_signed)


def _ret_heads(states, qs, ks, vs, dec, xi, zeta, cd):
    mm_nn, mm_nt, mm_tn = _make_mm(False)
    heads = range(len(qs))
    scores = [mm_nt(qs[h], ks[h]) for h in heads]
    inter = [mm_nn(qs[h] * xi[h], states[h]) for h in heads]
    kv = [mm_tn(ks[h] * zeta[h], vs[h]) for h in heads]
    intra = [mm_nn(scores[h] * dec[h], vs[h]) for h in heads]
    return [intra[h] + inter[h] for h in heads], [states[h] * cd[h] + kv[h] for h in heads]


def _ret_table_specs(nh, nc, rev):
    cidx = (lambda c: nc - 1 - c) if rev else (lambda c: c)
    rope = pl.BlockSpec((CHUNK, HEAD_DIM), lambda c: (cidx(c), 0))
    dec = pl.BlockSpec((nh, CHUNK, CHUNK), lambda c: (0, 0, 0))
    tab = pl.BlockSpec((nh, CHUNK, HEAD_DIM), lambda c: (0, 0, 0))
    cd = pl.BlockSpec((nh, 8, HEAD_DIM), lambda c: (0, 0, 0))
    return [rope, rope, dec, tab, tab, cd]


def _ret_fwd(proj, cos, sin, dec, xi, zeta, cd, nh, name):
    tp = proj.shape[0]
    nc = tp // CHUNK
    kscale = HEAD_DIM ** -0.5

    def body(q_ref, k_ref, v_ref, cos_ref, sin_ref, dec_ref, xi_ref, zeta_ref, cd_ref, o_ref, st_ref, s_sc):
        @pl.when(pl.program_id(0) == 0)
        def _():
            s_sc[...] = jnp.zeros_like(s_sc)

        cos_t, sin_t = cos_ref[...], sin_ref[...]
        heads = range(nh)
        sls = [slice(h * HEAD_DIM, (h + 1) * HEAD_DIM) for h in heads]
        states = [s_sc[h] for h in heads]
        for h in heads:
            st_ref[0, h] = states[h]
        qs = [_rot(q_ref[:, sl], cos_t, sin_t) for sl in sls]
        ks = [_rot(k_ref[:, sl], cos_t, sin_t) * kscale for sl in sls]
        outs, new_states = _ret_heads(states, qs, ks, [v_ref[:, sl] for sl in sls], [dec_ref[h] for h in heads],
                                      [xi_ref[h] for h in heads], [zeta_ref[h] for h in heads],
                                      [cd_ref[h][0:1, :] for h in heads])
        for h in heads:
            o_ref[:, sls[h]] = outs[h]
            s_sc[h] = new_states[h]

    cols, st, act = _scan_specs(nh, nc, False, 3)
    return pl.pallas_call(
        body, name=name, grid=(nc,), in_specs=cols + _ret_table_specs(nh, nc, False), out_specs=[act, st],
        out_shape=[jax.ShapeDtypeStruct((tp, nh * HEAD_DIM), F32),
                   jax.ShapeDtypeStruct((nc, nh, HEAD_DIM, HEAD_DIM), F32)],
        scratch_shapes=[pltpu.VMEM((nh, HEAD_DIM, HEAD_DIM), F32)],
        compiler_params=_params("arbitrary"))(proj, proj, proj, cos, sin, dec, xi, zeta, cd)


def _ret_bwd(proj, cos, sin, dec, xi, zeta, cd, states, do, nh, name):
    tp = proj.shape[0]
    nc = tp // CHUNK
    kscale = HEAD_DIM ** -0.5

    def body(q_ref, k_ref, v_ref, cos_ref, sin_ref, dec_ref, xi_ref, zeta_ref, cd_ref, st_ref, do_ref,
             dq_ref, dk_ref, dv_ref, ds_sc):
        @pl.when(pl.program_id(0) == 0)
        def _():
            ds_sc[...] = jnp.zeros_like(ds_sc)

        cos_t, sin_t = cos_ref[...], sin_ref[...]
        heads = range(nh)
        sls = [slice(h * HEAD_DIM, (h + 1) * HEAD_DIM) for h in heads]
        qs = [_rot(q_ref[:, sl], cos_t, sin_t) for sl in sls]
        ks = [_rot(k_ref[:, sl], cos_t, sin_t) * kscale for sl in sls]
        fn = functools.partial(_ret_heads, dec=[dec_ref[h] for h in heads], xi=[xi_ref[h] for h in heads],
                               zeta=[zeta_ref[h] for h in heads], cd=[cd_ref[h][0:1, :] for h in heads])
        _, vjp = jax.vjp(fn, [st_ref[0, h] for h in heads], qs, ks, [v_ref[:, sl] for sl in sls])
        ds, dq, dk, dv = vjp(([do_ref[:, sl] for sl in sls], [ds_sc[h] for h in heads]))
        for h in heads:
            ds_sc[h] = ds[h]
            dq_ref[:, sls[h]] = _rot_t(dq[h], cos_t, sin_t).astype(BF16)
            dk_ref[:, sls[h]] = _rot_t(dk[h] * kscale, cos_t, sin_t).astype(BF16)
            dv_ref[:, sls[h]] = dv[h].astype(BF16)

    cols, st, act = _scan_specs(nh, nc, True, 3)
    return pl.pallas_call(
        body, name=name, grid=(nc,), in_specs=cols + _ret_table_specs(nh, nc, True) + [st, act],
        out_specs=[act, act, act],
        out_shape=[jax.ShapeDtypeStruct((tp, nh * HEAD_DIM), BF16)] * 3,
        scratch_shapes=[pltpu.VMEM((nh, HEAD_DIM, HEAD_DIM), F32)],
        compiler_params=_params("arbitrary"))(proj, proj, proj, cos, sin, dec, xi, zeta, cd, states, do)


def _gdn_out(o, z, gnorm):
    return o * lax.rsqrt(jnp.mean(o * o, axis=-1, keepdims=True) + EPS) * gnorm * _silu(z)


def _ret_out(o, rg, rnorm):
    mu = jnp.mean(o, axis=-1, keepdims=True)
    var = jnp.mean(jnp.square(o - mu), axis=-1, keepdims=True)
    return _silu(rg) * ((o - mu) * lax.rsqrt(var + EPS) * rnorm)


def _post_specs(tm, hv, d):
    row = lambda col: pl.BlockSpec((tm, hv), lambda i: (i, col))
    return dict(
        oa=row(0), ob=row(0), z=row(6), rg=row(7), ga=row(8), gb=row(9),
        gnorm=pl.BlockSpec((1, HEAD_DIM), lambda i: (0, 0)), rnorm=pl.BlockSpec((1, hv), lambda i: (0, 0)),
        w=pl.BlockSpec((hv, d), lambda i: (0, 0)), res=pl.BlockSpec((tm, d), lambda i: (i, 0)))


def _post_fwd(oa, ob, proj, gnorm, rnorm, wbg, wbr, wo, h1, name):
    tp, d = h1.shape
    hv = oa.shape[1]
    nh = hv // HEAD_DIM
    tm = _tile(tp, 256, 8)

    def body(oa_ref, ob_ref, z_ref, rg_ref, ga_ref, gb_ref, gn_ref, rn_ref, wbg_ref, wbr_ref, wo_ref, h_ref,
             o_ref, ya_sc, yb_sc):
        for h in range(nh):
            sl = slice(h * HEAD_DIM, (h + 1) * HEAD_DIM)
            ya_sc[:, sl] = _gdn_out(oa_ref[:, sl], z_ref[:, sl], gn_ref[...]).astype(BF16)
            yb_sc[:, sl] = _ret_out(ob_ref[:, sl], rg_ref[:, sl], rn_ref[:, sl]).astype(BF16)
        pa = jnp.dot(ya_sc[...], wbg_ref[...], preferred_element_type=F32)
        pb = jnp.dot(yb_sc[...], wbr_ref[...], preferred_element_type=F32)
        merged = jax.nn.sigmoid(ga_ref[...]) * pa + jax.nn.sigmoid(gb_ref[...]) * pb
        o_ref[...] = h_ref[...] + jnp.dot(merged.astype(BF16), wo_ref[...], preferred_element_type=F32)

    sp = _post_specs(tm, hv, d)
    return pl.pallas_call(
        body, name=name, grid=(tp // tm,),
        in_specs=[sp["oa"], sp["ob"], sp["z"], sp["rg"], sp["ga"], sp["gb"], sp["gnorm"], sp["rnorm"],
                  sp["w"], sp["w"], sp["w"], sp["res"]],
        out_specs=sp["res"], out_shape=jax.ShapeDtypeStruct((tp, d), F32),
        scratch_shapes=[pltpu.VMEM((tm, hv), BF16), pltpu.VMEM((tm, hv), BF16)],
        compiler_params=_params("parallel"))(oa, ob, proj, proj, proj, proj, gnorm, rnorm, wbg, wbr, wo, h1)


def _post_bwd(oa, ob, proj, gnorm, rnorm, wbg, wbr, wo, dh2, name):
    tp, d = dh2.shape
    hv = oa.shape[1]
    nh = hv // HEAD_DIM
    tm = _tile(tp, 256, 8)

    def body(oa_ref, ob_ref, z_ref, rg_ref, ga_ref, gb_ref, gn_ref, rn_ref, wbg_ref, wbr_ref, wo_ref, dh_ref,
             doa_ref, dob_ref, dg_ref, ya_ref, yb_ref, mg_ref, dpa_ref, dpb_ref, dgn_ref, drn_ref,
             dya_sc, dyb_sc):
        @pl.when(pl.program_id(0) == 0)
        def _():
            dgn_ref[...] = jnp.zeros_like(dgn_ref)
            drn_ref[...] = jnp.zeros_like(drn_ref)

        for h in range(nh):
            sl = slice(h * HEAD_DIM, (h + 1) * HEAD_DIM)
            ya_ref[:, sl] = _gdn_out(oa_ref[:, sl], z_ref[:, sl], gn_ref[...]).astype(BF16)
            yb_ref[:, sl] = _ret_out(ob_ref[:, sl], rg_ref[:, sl], rn_ref[:, sl]).astype(BF16)
        pa = jnp.dot(ya_ref[...], wbg_ref[...], preferred_element_type=F32)
        pb = jnp.dot(yb_ref[...], wbr_ref[...], preferred_element_type=F32)
        sa = jax.nn.sigmoid(ga_ref[...])
        sb = jax.nn.sigmoid(gb_ref[...])
        mg_ref[...] = (sa * pa + sb * pb).astype(BF16)
        dm = lax.dot_general(dh_ref[...].astype(BF16), wo_ref[...], NT, preferred_element_type=F32)
        dpa = (dm * sa).astype(BF16)
        dpb = (dm * sb).astype(BF16)
        dpa_ref[...] = dpa
        dpb_ref[...] = dpb
        dg_ref[:, 2 * hv:3 * hv] = (dm * pa * sa * (1.0 - sa)).astype(BF16)
        dg_ref[:, 3 * hv:4 * hv] = (dm * pb * sb * (1.0 - sb)).astype(BF16)
        dya_sc[...] = lax.dot_general(dpa, wbg_ref[...], NT, preferred_element_type=F32)
        dyb_sc[...] = lax.dot_general(dpb, wbr_ref[...], NT, preferred_element_type=F32)
        for h in range(nh):
            sl = slice(h * HEAD_DIM, (h + 1) * HEAD_DIM)
            _, vjp_a = jax.vjp(_gdn_out, oa_ref[:, sl], z_ref[:, sl], gn_ref[...])
            doa, dz, dgn = vjp_a(dya_sc[:, sl])
            doa_ref[:, sl] = doa
            dg_ref[:, sl] = dz.astype(BF16)
            dgn_ref[...] += dgn
            _, vjp_b = jax.vjp(_ret_out, ob_ref[:, sl], rg_ref[:, sl], rn_ref[:, sl])
            dob, drg, drn = vjp_b(dyb_sc[:, sl])
            dob_ref[:, sl] = dob
            dg_ref[:, hv + h * HEAD_DIM:hv + (h + 1) * HEAD_DIM] = drg.astype(BF16)
            drn_ref[:, sl] += drn

    sp = _post_specs(tm, hv, d)
    act = pl.BlockSpec((tm, hv), lambda i: (i, 0))
    return pl.pallas_call(
        body, name=name, grid=(tp // tm,),
        in_specs=[sp["oa"], sp["ob"], sp["z"], sp["rg"], sp["ga"], sp["gb"], sp["gnorm"], sp["rnorm"],
                  sp["w"], sp["w"], sp["w"], sp["res"]],
        out_specs=[act, act, pl.BlockSpec((tm, 4 * hv), lambda i: (i, 0)), act, act, sp["res"], sp["res"],
                   sp["res"], sp["gnorm"], sp["rnorm"]],
        out_shape=[jax.ShapeDtypeStruct((tp, hv), F32), jax.ShapeDtypeStruct((tp, hv), F32),
                   jax.ShapeDtypeStruct((tp, 4 * hv), BF16), jax.ShapeDtypeStruct((tp, hv), BF16),
                   jax.ShapeDtypeStruct((tp, hv), BF16), jax.ShapeDtypeStruct((tp, d), BF16),
                   jax.ShapeDtypeStruct((tp, d), BF16), jax.ShapeDtypeStruct((tp, d), BF16),
                   jax.ShapeDtypeStruct((1, HEAD_DIM), F32), jax.ShapeDtypeStruct((1, hv), F32)],
        scratch_shapes=[pltpu.VMEM((tm, hv), F32), pltpu.VMEM((tm, hv), F32)],
        compiler_params=_params("arbitrary"))(oa, ob, proj, proj, proj, proj, gnorm, rnorm, wbg, wbr, wo, dh2)


def _final(h3, gain, target, name):
    tp, d = h3.shape
    tm = HEAD_ROWS

    def body(h_ref, g_ref, t_ref, loss_ref, dh_ref, dgain_ref):
        i = pl.program_id(0)

        @pl.when(i == 0)
        def _():
            loss_ref[...] = jnp.zeros_like(loss_ref)
            dgain_ref[...] = jnp.zeros_like(dgain_ref)

        xh, r = _rms_parts(h_ref[...])
        err = jnp.where(i == 0, 0.0, xh * g_ref[...] - t_ref[...])
        dx, dg = _rms_bwd(err * (1.0 / d), xh, r, g_ref[...])
        dh_ref[...] = dx
        dgain_ref[...] += dg
        loss_ref[...] += 0.5 * jnp.sum(jnp.mean(err * err, axis=-1, keepdims=True), axis=0, keepdims=True)

    row = pl.BlockSpec((tm, d), lambda i: (i, 0))
    vec = pl.BlockSpec((1, d), lambda i: (0, 0))
    return pl.pallas_call(
        body, name=name, grid=(tp // tm,),
        in_specs=[row, vec, pl.BlockSpec((tm, d), lambda i: (jnp.maximum(i - 1, 0), 0))],
        out_specs=[pl.BlockSpec((1, LANES), lambda i: (0, 0)), row, vec],
        out_shape=[jax.ShapeDtypeStruct((1, LANES), F32), jax.ShapeDtypeStruct((tp, d), F32),
                   jax.ShapeDtypeStruct((1, d), F32)],
        compiler_params=_params("arbitrary"))(h3, gain, target)


def _peer(k):
    x, y, c = lax.axis_index("x"), lax.axis_index("y"), lax.axis_index("c")
    return (1 - x if k & 4 else x, 1 - y if k & 2 else y, 1 - c if k & 1 else c)


def _my_index():
    return 4 * lax.axis_index("x") + 2 * lax.axis_index("y") + lax.axis_index("c")


def _exchange(bufs, scatter, name):
    n = len(bufs)

    def body(*refs):
        x_refs, out_refs = refs[:n], refs[n:2 * n]
        send_sems, recv_sems, local_sems = refs[2 * n:]
        me = _my_index()
        started = []
        for a in range(n):
            cp = pltpu.make_async_copy(x_refs[a].at[me] if scatter else x_refs[a], out_refs[a].at[me],
                                       local_sems.at[a])
            cp.start()
            started.append(cp)
        sends = []
        for k in range(1, N_DEV):
            x, y, c = _peer(k)
            for a in range(n):
                sem = (k - 1) * n + a
                cp = pltpu.make_async_remote_copy(
                    src_ref=x_refs[a].at[4 * x + 2 * y + c] if scatter else x_refs[a], dst_ref=out_refs[a].at[me],
                    send_sem=send_sems.at[sem], recv_sem=recv_sems.at[sem],
                    device_id=(x, y, c), device_id_type=pl.DeviceIdType.MESH)
                cp.start()
                sends.append(cp)
        for k in range(1, N_DEV):
            x, y, c = _peer(k)
            for a in range(n):
                sem = (k - 1) * n + a
                landed = out_refs[a].at[4 * x + 2 * y + c]
                pltpu.make_async_remote_copy(
                    src_ref=landed, dst_ref=landed, send_sem=send_sems.at[sem], recv_sem=recv_sems.at[sem],
                    device_id=(x, y, c), device_id_type=pl.DeviceIdType.MESH).wait_recv()
        for cp in sends:
            cp.wait_send()
        for cp in started:
            cp.wait()

    hbm = pl.BlockSpec(memory_space=pl.ANY)
    return pl.pallas_call(
        body, name=name, in_specs=[hbm] * n, out_specs=[hbm] * n,
        out_shape=[jax.ShapeDtypeStruct((N_DEV,) + b.shape[-2:], b.dtype) for b in bufs],
        scratch_shapes=[pltpu.SemaphoreType.DMA(((N_DEV - 1) * n,)), pltpu.SemaphoreType.DMA(((N_DEV - 1) * n,)),
                        pltpu.SemaphoreType.DMA((n,))],
        compiler_params=pltpu.CompilerParams(has_side_effects=True))(*bufs)


def _adamw(w, g, m, v, name):
    r, c = w.shape
    parts = g.ndim == 3
    tr = _tile(r, 256, 16 if parts else 8)
    c1 = 1.0 - ADAM_B1 ** ADAM_STEP
    c2 = 1.0 - ADAM_B2 ** ADAM_STEP

    def body(w_ref, g_ref, m_ref, v_ref, go_ref, d_ref, mo_ref, vo_ref):
        if parts:
            g = g_ref[0].astype(F32)
            for q in range(1, N_DEV):
                g = g + g_ref[q].astype(F32)
        else:
            g = g_ref[...]
        m = ADAM_B1 * m_ref[...] + (1.0 - ADAM_B1) * g
        v = ADAM_B2 * v_ref[...] + (1.0 - ADAM_B2) * (g * g)
        go_ref[...] = g
        d_ref[...] = -ADAM_LR * ((m / c1) / (jnp.sqrt(v / c2) + ADAM_EPS) + ADAM_WD * w_ref[...])
        mo_ref[...] = m
        vo_ref[...] = v

    blk = pl.BlockSpec((tr, c), lambda i: (i, 0))
    g_spec = pl.BlockSpec((N_DEV, tr, c), lambda i: (0, i, 0)) if parts else blk
    return pl.pallas_call(
        body, name=name, grid=(r // tr,), in_specs=[blk, g_spec, blk, blk], out_specs=[blk] * 4,
        out_shape=[jax.ShapeDtypeStruct((r, c), F32)] * 4,
        compiler_params=_params("parallel"))(w, g, m, v)


def _win_segments(hv, nh):
    o_z, o_b = 3 * hv, 4 * hv
    o_r = o_b + 2 * nh
    return [(0, 0, 3 * hv), (3 * hv, o_r, 3 * hv), (6 * hv, o_z, hv), (7 * hv, o_r + 3 * hv, 3 * hv),
            (10 * hv, o_b, 2 * nh)]


def _win_from_shards(shards, hv, nh):
    _, d, cs = shards.shape
    pieces = []
    for _, src, width in _win_segments(hv, nh):
        lo = src
        while lo < src + width:
            p = lo // cs
            hi = min(src + width, (p + 1) * cs)
            pieces.append(shards[p][:, lo - p * cs:hi - p * cs])
            lo = hi
    pieces.append(jnp.zeros((d, LANES - 2 * nh), shards.dtype))
    return jnp.concatenate(pieces, axis=1)


def _win_grad_to_shards(g, hv, nh, cs):
    segments = _win_segments(hv, nh)
    shards = []
    for p in range(N_DEV):
        pieces = []
        lo = p * cs
        while lo < (p + 1) * cs:
            here, src, width = next(s for s in segments if s[1] <= lo < s[1] + s[2])
            hi = min((p + 1) * cs, src + width)
            pieces.append(g[:, here + lo - src:here + hi - src])
            lo = hi
        shards.append(jnp.concatenate(pieces, axis=1))
    return jnp.stack(shards)


def _rope_tables(tp):
    pos = jnp.arange(tp, dtype=F32) - float(PAD_FRONT)
    inv = 1.0 / (ROPE_BASE ** jnp.linspace(0.0, 1.0, HEAD_DIM // 2, dtype=F32))
    ang = pos[:, None] * inv[None, :]
    cos = jnp.repeat(jnp.cos(ang), 2, axis=1)
    sin = jnp.repeat(jnp.sin(ang), 2, axis=1) * jnp.tile(jnp.array([-1.0, 1.0], F32), HEAD_DIM // 2)[None, :]
    return cos, sin


def _retention_tables(nh):
    log_gamma = jnp.log1p(-jnp.exp2(-5.0 - jnp.arange(nh, dtype=F32)))
    pos = jnp.arange(CHUNK, dtype=F32)
    causal = pos[:, None] >= pos[None, :]
    diff = pos[:, None] - pos[None, :]
    dec = jnp.where(causal, jnp.exp(jnp.where(causal, diff, 0.0) * log_gamma[:, None, None]), 0.0)
    ones = jnp.ones((1, 1, HEAD_DIM), F32)
    xi = jnp.exp((pos + 1.0)[None, :] * log_gamma[:, None])[:, :, None] * ones
    zeta = jnp.exp((CHUNK - 1.0 - pos)[None, :] * log_gamma[:, None])[:, :, None] * ones
    cd = jnp.exp(CHUNK * log_gamma)[:, None, None] * jnp.ones((1, 8, HEAD_DIM), F32)
    return dec, xi, zeta, cd


SHARDED = ("meta_tokens", "ffn1_w_in", "ffn1_w_out", "w_in", "gdn_conv_w", "w_branch_gdn", "w_branch_ret",
           "w_out", "ffn2_w_in", "ffn2_w_out")
COLUMN_SHARDED = ("meta_tokens", "ffn1_w_in", "w_in", "gdn_conv_w", "ffn2_w_in")
EXACT_F32 = ("meta_tokens", "gdn_conv_w")
REPLICATED = ("ffn1_norm", "mix_norm", "gdn_a_log", "gdn_dt_bias", "gdn_out_norm", "ret_out_norm", "ffn2_norm",
              "final_norm")
WEIGHTS = ("meta_tokens", "ffn1_norm", "ffn1_w_in", "ffn1_w_out", "mix_norm", "w_in", "gdn_conv_w", "gdn_a_log",
           "gdn_dt_bias", "gdn_out_norm", "ret_out_norm", "w_branch_gdn", "w_branch_ret", "w_out", "ffn2_norm",
           "ffn2_w_in", "ffn2_w_out", "final_norm")


def _as2d(a):
    if a.ndim == 3:
        return a[0]
    if a.ndim == 1:
        return a[None, :]
    return a


def _weights_from_shards(got, d):
    rows = lambda name: got[name].reshape(-1, d)
    cols = lambda name: got[name].transpose(1, 0, 2).reshape(got[name].shape[1], -1)
    hv = got["w_branch_gdn"].shape[0] * got["w_branch_gdn"].shape[1]
    w = {"ffn1_w_in": got["ffn1_w_in"], "ffn1_w_out": rows("ffn1_w_out"),
         "w_in": _win_from_shards(got["w_in"], hv, hv // HEAD_DIM), "gdn_conv_w": cols("gdn_conv_w"),
         "w_branch_gdn": rows("w_branch_gdn"), "w_branch_ret": rows("w_branch_ret"), "w_out": rows("w_out"),
         "ffn2_w_in": got["ffn2_w_in"], "ffn2_w_out": rows("ffn2_w_out")}
    return cols("meta_tokens"), w


def _grads_to_shards(grads, local_shapes):
    hv = grads["w_branch_gdn"].shape[0]
    out = {}
    for name in SHARDED:
        g = grads[name]
        r, c = local_shapes[name]
        if name in ("ffn1_w_in", "ffn2_w_in"):
            out[name] = g
        elif name == "w_in":
            out[name] = _win_grad_to_shards(g, hv, hv // HEAD_DIM, c)
        elif name in COLUMN_SHARDED:
            out[name] = g.reshape(r, N_DEV, c).transpose(1, 0, 2)
        else:
            out[name] = g.reshape(N_DEV, r, c)
    return out


def _local_step(x, target, meta, w, rep):
    seq, d = x.shape
    tp = HEAD_ROWS + seq
    hv = w["w_branch_gdn"].shape[0]
    nh = hv // HEAD_DIM
    assert hv == d and seq % CHUNK == 0 and tp % HEAD_ROWS == 0
    wp = w["w_in"]

    pad_lanes = lambda row: jnp.pad(row, ((0, 0), (nh, LANES - 2 * nh)))
    alog = pad_lanes(rep["gdn_a_log"])
    dtb = pad_lanes(rep["gdn_dt_bias"])
    cos, sin = _rope_tables(tp)
    dec, xi, zeta, cd = _retention_tables(nh)

    h0 = jnp.concatenate([jnp.zeros((PAD_FRONT, d), F32), meta, x], axis=0)
    f1i, f1o, f2i, f2o = w["ffn1_w_in"], w["ffn1_w_out"], w["ffn2_w_in"], w["ffn2_w_out"]

    h1 = _ffn_fwd(h0, rep["ffn1_norm"], f1i, f1o, "ffn1_fwd")
    proj, n2 = _proj_fwd(h1, rep["mix_norm"], wp, "proj_fwd")
    qkv = _conv_fwd(proj, w["gdn_conv_w"], hv, "conv_fwd")
    oa, s_gdn = _gdn_fwd(qkv, proj, alog, dtb, nh, "gdn_fwd")
    ob, s_ret = _ret_fwd(proj, cos, sin, dec, xi, zeta, cd, nh, "ret_fwd")
    h2 = _post_fwd(oa, ob, proj, rep["gdn_out_norm"], rep["ret_out_norm"], w["w_branch_gdn"], w["w_branch_ret"],
                   w["w_out"], h1, "post_fwd")
    h3 = _ffn_fwd(h2, rep["ffn2_norm"], f2i, f2o, "ffn2_fwd")
    loss_row, dh3, d_final = _final(h3, rep["final_norm"], target, "final")

    dh2, d_f2n, n3, hid2, dag2, dau2 = _ffn_bwd(h2, dh3, rep["ffn2_norm"], f2i, f2o, "ffn2_bwd")
    g_f2in = jnp.concatenate([_matmul_tn_blocks(n3, dag2, "ffn2_dwg"), _matmul_tn_blocks(n3, dau2, "ffn2_dwu")])
    g_f2out = _matmul_tn_blocks(hid2, dh3, "ffn2_dwo", 0.5)

    doa, dob, dgate, ya, yb, merged, dpa, dpb, d_gn, d_rn = _post_bwd(
        oa, ob, proj, rep["gdn_out_norm"], rep["ret_out_norm"], w["w_branch_gdn"], w["w_branch_ret"], w["w_out"],
        dh2, "post_bwd")
    g_wbg = _matmul_tn(ya, dpa, "dw_branch_gdn")
    g_wbr = _matmul_tn(yb, dpb, "dw_branch_ret")
    g_wo = _matmul_tn(merged, dh2, "dw_out")

    d_ret = _ret_bwd(proj, cos, sin, dec, xi, zeta, cd, s_ret, dob, nh, "ret_bwd")
    gdn_grads = _gdn_bwd(qkv, proj, alog, dtb, s_gdn, doa, nh, "gdn_bwd")
    dba, d_alog, d_dtb = gdn_grads[3:]
    dpre, g_conv = [], []
    for grp, tag in enumerate("qkv"):
        dc, dw = _conv_bwd_pre(proj, w["gdn_conv_w"], gdn_grads[grp], grp, hv, "conv_bwd_pre_" + tag)
        dpre.append(_conv_bwd_in(dc, w["gdn_conv_w"], grp, "conv_bwd_in_" + tag))
        g_conv.append(dw)
    g_conv = jnp.concatenate(g_conv, axis=1)

    wide = dpre + list(d_ret) + [dgate]
    dn2 = _matmul_nt_parts(wide, wp[:, :10 * hv], None, "dn2_wide")
    dn2 = _matmul_nt_parts([dba], wp[:, 10 * hv:], dn2, "dn2_beta_alpha")
    g_wp = [_matmul_tn(n2, dg, "dw_in_%d" % idx) for idx, dg in enumerate(wide + [dba])]
    dh1, d_mixn = _norm_bwd(h1, rep["mix_norm"], dn2, dh2, "mix_norm_bwd")

    dh0, d_f1n, n1, hid1, dag1, dau1 = _ffn_bwd(h0, dh1, rep["ffn1_norm"], f1i, f1o, "ffn1_bwd")
    g_f1in = jnp.concatenate([_matmul_tn_blocks(n1, dag1, "ffn1_dwg"), _matmul_tn_blocks(n1, dau1, "ffn1_dwu")])
    g_f1out = _matmul_tn_blocks(hid1, dh1, "ffn1_dwo", 0.5)

    g_win = jnp.concatenate(g_wp, axis=1)
    grads = {"meta_tokens": dh0[PAD_FRONT:HEAD_ROWS], "ffn1_w_in": g_f1in, "ffn1_w_out": g_f1out, "w_in": g_win,
             "gdn_conv_w": g_conv, "w_branch_gdn": g_wbg, "w_branch_ret": g_wbr, "w_out": g_wo,
             "ffn2_w_in": g_f2in, "ffn2_w_out": g_f2out}
    small = {"ffn1_norm": d_f1n, "mix_norm": d_mixn, "gdn_a_log": d_alog[:, nh:2 * nh],
             "gdn_dt_bias": d_dtb[:, nh:2 * nh], "gdn_out_norm": d_gn, "ret_out_norm": d_rn, "ffn2_norm": d_f2n,
             "final_norm": d_final}
    return loss_row[0, 0], dh0[HEAD_ROWS:], grads, small


def kernel(x, meta_tokens, ffn1_norm, ffn1_w_in, ffn1_w_out, mix_norm, w_in, gdn_conv_w, gdn_a_log, gdn_dt_bias, gdn_out_norm, ret_out_norm, w_branch_gdn, w_branch_ret, w_out, ffn2_norm, ffn2_w_in, ffn2_w_out, final_norm, loss_target, m_meta_tokens, m_ffn1_norm, m_ffn1_w_in, m_ffn1_w_out, m_mix_norm, m_w_in, m_gdn_conv_w, m_gdn_a_log, m_gdn_dt_bias, m_gdn_out_norm, m_ret_out_norm, m_w_branch_gdn, m_w_branch_ret, m_w_out, m_ffn2_norm, m_ffn2_w_in, m_ffn2_w_out, m_final_norm, v_meta_tokens, v_ffn1_norm, v_ffn1_w_in, v_ffn1_w_out, v_mix_norm, v_w_in, v_gdn_conv_w, v_gdn_a_log, v_gdn_dt_bias, v_gdn_out_norm, v_ret_out_norm, v_w_branch_gdn, v_w_branch_ret, v_w_out, v_ffn2_norm, v_ffn2_w_in, v_ffn2_w_out, v_final_norm):
    given = dict(locals())
    params = {n: _as2d(given[n]) for n in WEIGHTS}
    local = {n: params[n] for n in SHARDED}
    rep = {n: params[n] for n in REPLICATED}

    send = [local[n] if n in EXACT_F32 else local[n].astype(BF16) for n in SHARDED]
    got = dict(zip(SHARDED, _exchange(send, False, "gather_weights")))
    meta, w = _weights_from_shards(got, x.shape[-1])
    loss_sum, grad_x, grads, small = _local_step(x[0], loss_target[0], meta, w, rep)
    shards = _grads_to_shards(grads, {n: local[n].shape for n in SHARDED})
    parts = dict(zip(SHARDED, _exchange([shards[n].astype(BF16) for n in SHARDED], True, "scatter_grads")))
    parts.update(zip(REPLICATED, _exchange([small[n] for n in REPLICATED], False, "gather_small_grads")))
    loss = lax.psum(loss_sum, ("x", "y", "c"))

    outs = {}
    for n in WEIGHTS:
        res = _adamw(params[n], parts[n], _as2d(given["m_" + n]), _as2d(given["v_" + n]), "adamw_" + n)
        outs[n] = [r.reshape(given[n].shape) for r in res]
    return (loss, grad_x[None], *[outs[n][0] for n in WEIGHTS], *[outs[n][1] for n in WEIGHTS],
            *[outs[n][2] for n in WEIGHTS], *[outs[n][3] for n in WEIGHTS])
```

```python
import functools
import math

import numpy as np
import jax
import jax.numpy as jnp
from jax import lax
from jax.experimental import pallas as pl
from jax.experimental.pallas import tpu as pltpu

F32 = jnp.float32
BF16 = jnp.bfloat16

N_DEV = 8
N_META = 16
CHUNK = 64
HEAD_DIM = 128
CONV_K = 4
ROPE_BASE = 10000.0
EPS = 1e-6
PAD_FRONT = 240
HEAD_ROWS = PAD_FRONT + N_META
LANES = 128
VMEM_LIMIT_BYTES = 56 * 1024 * 1024

ADAM_LR = 0.001
ADAM_B1 = 0.9
ADAM_B2 = 0.999
ADAM_EPS = 1e-08
ADAM_WD = 0.01
ADAM_STEP = 10

NN = (((1,), (0,)), ((), ()))
NT = (((1,), (1,)), ((), ()))
TN = (((0,), (0,)), ((), ()))


def _tile(n, target, mult):
    best = 0
    for t in range(mult, min(n, target) + 1, mult):
        if n % t == 0:
            best = t
    return best if best else n


def _params(*semantics):
    return pltpu.CompilerParams(dimension_semantics=semantics, vmem_limit_bytes=VMEM_LIMIT_BYTES)


def _raw_dot(a, b, dims, hi):
    if hi:
        return lax.dot_general(a, b, dims, precision=lax.Precision.HIGHEST, preferred_element_type=F32)
    return lax.dot_general(a.astype(BF16), b.astype(BF16), dims, preferred_element_type=F32)


def _make_mm(hi):
    @jax.custom_vjp
    def nn(a, b):
        return _raw_dot(a, b, NN, hi)

    @jax.custom_vjp
    def nt(a, b):
        return _raw_dot(a, b, NT, hi)

    @jax.custom_vjp
    def tn(a, b):
        return _raw_dot(a, b, TN, hi)

    nn.defvjp(lambda a, b: (_raw_dot(a, b, NN, hi), (a, b)),
              lambda r, g: (_raw_dot(g, r[1], NT, hi), _raw_dot(r[0], g, TN, hi)))
    nt.defvjp(lambda a, b: (_raw_dot(a, b, NT, hi), (a, b)),
              lambda r, g: (_raw_dot(g, r[1], NN, hi), _raw_dot(g, r[0], TN, hi)))
    tn.defvjp(lambda a, b: (_raw_dot(a, b, TN, hi), (a, b)),
              lambda r, g: (_raw_dot(r[1], g, NT, hi), _raw_dot(r[0], g, NN, hi)))
    return nn, nt, tn


def _silu(x):
    return x * jax.nn.sigmoid(x)


def _rms_parts(x):
    r = lax.rsqrt(jnp.mean(x * x, axis=-1, keepdims=True) + EPS)
    return x * r, r


def _rms_bwd(dy, xh, r, gain):
    dxh = dy * gain
    dx = r * (dxh - xh * jnp.mean(dxh * xh, axis=-1, keepdims=True))
    return dx, jnp.sum(dy * xh, axis=0, keepdims=True)


def _ffn_specs(tm, d, tf, nj):
    return [pl.BlockSpec((tm, d), lambda i, j: (i, 0)), pl.BlockSpec((1, d), lambda i, j: (0, 0)),
            pl.BlockSpec((1, d, tf), lambda i, j: (j, 0, 0)), pl.BlockSpec((1, d, tf), lambda i, j: (nj + j, 0, 0)),
            pl.BlockSpec((tf, d), lambda i, j: (j, 0))]


def _first_step(ndim):
    return lambda: functools.reduce(lambda a, b: a & b, [pl.program_id(k) == 0 for k in range(ndim)])


def _last_step(grid):
    return lambda: functools.reduce(lambda a, b: a & b, [pl.program_id(k) == g - 1 for k, g in enumerate(grid)])


def _ffn_fwd(h, gain, w_in, wo, name, carry=None):
    tp, d = h.shape
    tf = w_in.shape[2]
    nj = w_in.shape[0] // 2
    tm = _tile(tp, 768, 8)
    row, vec, wg_spec, wu_spec, wo_spec = _ffn_specs(tm, d, tf, nj)

    def body(h_ref, g_ref, wg3_ref, wu3_ref, wo_ref, o_ref, n_sc, acc_sc):
        wg_ref, wu_ref = wg3_ref.at[0], wu3_ref.at[0]
        j = pl.program_id(1)

        @pl.when(j == 0)
        def _():
            xh, _ = _rms_parts(h_ref[...])
            n_sc[...] = (xh * g_ref[...]).astype(BF16)
            acc_sc[...] = jnp.zeros_like(acc_sc)

        n = n_sc[...]
        a_g = jnp.dot(n, wg_ref[...], preferred_element_type=F32)
        a_u = jnp.dot(n, wu_ref[...], preferred_element_type=F32)
        hid = (_silu(a_g) * a_u).astype(BF16)
        acc_sc[...] += jnp.dot(hid, wo_ref[...], preferred_element_type=F32)

        @pl.when(j == nj - 1)
        def _():
            o_ref[...] = h_ref[...] + 0.5 * acc_sc[...]

    grid = (tp // tm, nj)
    (out,), moved = _carried_call(
        body, carry, _first_step(2), _last_step(grid), name=name, grid=grid,
        in_specs=[row, vec, wg_spec, wu_spec, wo_spec], out_specs=[row],
        out_shape=[jax.ShapeDtypeStruct((tp, d), F32)],
        scratch_shapes=[pltpu.VMEM((tm, d), BF16), pltpu.VMEM((tm, d), F32)])(h, gain, w_in, w_in, wo)
    return out, moved


def _ffn_bwd(h, dho, gain, w_in, wo, name, carry=None):
    tp, d = h.shape
    tf = w_in.shape[2]
    nj = w_in.shape[0] // 2
    tm = _tile(tp, 384, 8)
    ni = tp // tm
    row, vec, wg_spec, wu_spec, wo_spec = _ffn_specs(tm, d, tf, nj)

    def body(h_ref, dho_ref, g_ref, wg3_ref, wu3_ref, wo_ref,
             dh_ref, dgain_ref, n_ref, hid3_ref, dag3_ref, dau3_ref, dn_sc, dhb_sc):
        wg_ref, wu_ref = wg3_ref.at[0], wu3_ref.at[0]
        hid_ref, dag_ref, dau_ref = hid3_ref.at[0], dag3_ref.at[0], dau3_ref.at[0]
        i, j = pl.program_id(0), pl.program_id(1)

        @pl.when(j == 0)
        def _():
            xh, _ = _rms_parts(h_ref[...])
            n_ref[...] = (xh * g_ref[...]).astype(BF16)
            dn_sc[...] = jnp.zeros_like(dn_sc)
            dhb_sc[...] = (0.5 * dho_ref[...]).astype(BF16)

        @pl.when((i == 0) & (j == 0))
        def _():
            dgain_ref[...] = jnp.zeros_like(dgain_ref)

        n = n_ref[...]
        a_g = jnp.dot(n, wg_ref[...], preferred_element_type=F32)
        a_u = jnp.dot(n, wu_ref[...], preferred_element_type=F32)
        sg = jax.nn.sigmoid(a_g)
        s = a_g * sg
        hid_ref[...] = (s * a_u).astype(BF16)
        d_hid = lax.dot_general(dhb_sc[...], wo_ref[...], NT, preferred_element_type=F32)
        d_au = (d_hid * s).astype(BF16)
        d_ag = (d_hid * a_u * (sg * (1.0 + a_g * (1.0 - sg)))).astype(BF16)
        dau_ref[...] = d_au
        dag_ref[...] = d_ag
        dn_sc[...] += (lax.dot_general(d_ag, wg_ref[...], NT, preferred_element_type=F32)
                       + lax.dot_general(d_au, wu_ref[...], NT, preferred_element_type=F32))

        @pl.when(j == nj - 1)
        def _():
            xh, r = _rms_parts(h_ref[...])
            dx, dg = _rms_bwd(dn_sc[...], xh, r, g_ref[...])
            dh_ref[...] = dho_ref[...] + dx
            dgain_ref[...] += dg

    act = pl.BlockSpec((1, tm, tf), lambda i, j: (j, i, 0))
    return _carried_call(
        body, carry, _first_step(2), _last_step((ni, nj)), name=name, grid=(ni, nj),
        in_specs=[row, row, vec, wg_spec, wu_spec, wo_spec],
        out_specs=[row, vec, row, act, act, act],
        out_shape=[jax.ShapeDtypeStruct((tp, d), F32), jax.ShapeDtypeStruct((1, d), F32),
                   jax.ShapeDtypeStruct((tp, d), BF16)] + [jax.ShapeDtypeStruct((nj, tp, tf), BF16)] * 3,
        scratch_shapes=[pltpu.VMEM((tm, d), F32), pltpu.VMEM((tm, d), BF16)])(h, dho, gain, w_in, w_in, wo)


def _matmul_tn(a, b, name, scale=1.0):
    t, m = a.shape
    n = b.shape[1]
    bm = _tile(m, 1024, LANES)
    bn = _tile(n, 1536, LANES)
    tk = _tile(t, 768, 16)
    nk = t // tk

    def body(a_ref, b_ref, o_ref):
        k = pl.program_id(2)

        @pl.when(k == 0)
        def _():
            o_ref[...] = jnp.zeros_like(o_ref)

        o_ref[...] += lax.dot_general(a_ref[...].astype(BF16), b_ref[...].astype(BF16), TN,
                                      preferred_element_type=F32)

        if scale != 1.0:
            @pl.when(k == nk - 1)
            def _():
                o_ref[...] = o_ref[...] * scale

    return pl.pallas_call(
        body, name=name, grid=(m // bm, n // bn, nk),
        in_specs=[pl.BlockSpec((tk, bm), lambda i, j, k: (k, i)), pl.BlockSpec((tk, bn), lambda i, j, k: (k, j))],
        out_specs=pl.BlockSpec((bm, bn), lambda i, j, k: (i, j)),
        out_shape=jax.ShapeDtypeStruct((m, n), F32),
        compiler_params=_params("parallel", "parallel", "arbitrary"))(a, b)


def _matmul_tn_blocks(a, b, name, scale=1.0):
    a_blocked = a.ndim == 3
    nb, t = (a.shape[0], a.shape[1]) if a_blocked else (b.shape[0], b.shape[1])
    m, n = a.shape[-1], b.shape[-1]
    tk = _tile(t, 768, 16)
    nk = t // tk
    if a_blocked:
        bo = _tile(n, 1024, LANES)
        a_spec = pl.BlockSpec((1, tk, m), lambda p, o, k: (p, k, 0))
        b_spec = pl.BlockSpec((tk, bo), lambda p, o, k: (k, o))
        o_spec = pl.BlockSpec((m, bo), lambda p, o, k: (p, o))
        out_shape = jax.ShapeDtypeStruct((nb * m, n), F32)
        grid = (nb, n // bo, nk)
    else:
        bo = _tile(m, 1024, LANES)
        a_spec = pl.BlockSpec((tk, bo), lambda p, o, k: (k, o))
        b_spec = pl.BlockSpec((1, tk, n), lambda p, o, k: (p, k, 0))
        o_spec = pl.BlockSpec((1, bo, n), lambda p, o, k: (p, o, 0))
        out_shape = jax.ShapeDtypeStruct((nb, m, n), F32)
        grid = (nb, m // bo, nk)

    def body(a_ref, b_ref, o_ref):
        k = pl.program_id(2)
        a_blk = a_ref[0] if a_blocked else a_ref[...]
        b_blk = b_ref[...] if a_blocked else b_ref[0]
        part = lax.dot_general(a_blk.astype(BF16), b_blk.astype(BF16), TN, preferred_element_type=F32)
        out = o_ref if a_blocked else o_ref.at[0]

        @pl.when(k == 0)
        def _():
            out[...] = part

        @pl.when(k > 0)
        def _():
            out[...] += part

        if scale != 1.0:
            @pl.when(k == nk - 1)
            def _():
                out[...] = out[...] * scale

    return pl.pallas_call(
        body, name=name, grid=grid, in_specs=[a_spec, b_spec], out_specs=o_spec, out_shape=out_shape,
        compiler_params=_params("parallel", "parallel", "arbitrary"))(a, b)


def _matmul_nt_parts(parts, w, acc, name, carry=None):
    t = parts[0].shape[0]
    d = w.shape[0]
    widths = [p.shape[1] for p in parts]
    tk = _tile(math.gcd(*widths), 1024, LANES)
    counts = [wd // tk for wd in widths]
    starts = [sum(counts[:g]) for g in range(len(parts))]
    nk = sum(counts)
    tm = _tile(t, 768, 8)
    n_parts = len(parts)

    def body(*refs):
        a_refs, w_ref, o_ref = refs[:n_parts], refs[n_parts], refs[-1]
        k = pl.program_id(1)

        @pl.when(k == 0)
        def _():
            o_ref[...] = jnp.zeros_like(o_ref) if acc is None else refs[n_parts + 1][...]

        for g in range(n_parts):
            @pl.when((k >= starts[g]) & (k < starts[g] + counts[g]))
            def _(g=g):
                o_ref[...] += lax.dot_general(a_refs[g][...].astype(BF16), w_ref[...], NT,
                                              preferred_element_type=F32)

    in_specs = [pl.BlockSpec((tm, tk), lambda i, k, lo=starts[g], nb=counts[g]: (i, jnp.clip(k - lo, 0, nb - 1)))
                for g in range(n_parts)]
    in_specs.append(pl.BlockSpec((d, tk), lambda i, k: (0, k)))
    args = list(parts) + [w]
    if acc is not None:
        in_specs.append(pl.BlockSpec((tm, d), lambda i, k: (i, 0)))
        args.append(acc)
    grid = (t // tm, nk)
    (out,), moved = _carried_call(
        body, carry, _first_step(2), _last_step(grid), name=name, grid=grid, in_specs=in_specs,
        out_specs=[pl.BlockSpec((tm, d), lambda i, k: (i, 0))],
        out_shape=[jax.ShapeDtypeStruct((t, d), F32)])(*args)
    return out, moved


def _proj_fwd(h, gain, wp, name, carry=None):
    tp, d = h.shape
    npad = wp.shape[1]
    tm = _tile(tp, 768, 8)
    tn = _tile(npad, 1152, LANES)

    def body(h_ref, g_ref, w_ref, o_ref, n_ref):
        @pl.when(pl.program_id(1) == 0)
        def _():
            xh, _ = _rms_parts(h_ref[...])
            n_ref[...] = (xh * g_ref[...]).astype(BF16)

        o_ref[...] = jnp.dot(n_ref[...], w_ref[...], preferred_element_type=F32)

    grid = (tp // tm, npad // tn)
    return _carried_call(
        body, carry, _first_step(2), _last_step(grid), name=name, grid=grid,
        in_specs=[pl.BlockSpec((tm, d), lambda i, j: (i, 0)), pl.BlockSpec((1, d), lambda i, j: (0, 0)),
                  pl.BlockSpec((d, tn), lambda i, j: (0, j))],
        out_specs=[pl.BlockSpec((tm, tn), lambda i, j: (i, j)), pl.BlockSpec((tm, d), lambda i, j: (i, 0))],
        out_shape=[jax.ShapeDtypeStruct((tp, npad), F32), jax.ShapeDtypeStruct((tp, d), BF16)])(h, gain, wp)


def _norm_bwd(h, gain, dn, dres, name):
    tp, d = h.shape
    tm = _tile(tp, 256, 8)

    def body(h_ref, g_ref, dn_ref, dres_ref, dh_ref, dgain_ref):
        @pl.when(pl.program_id(0) == 0)
        def _():
            dgain_ref[...] = jnp.zeros_like(dgain_ref)

        xh, r = _rms_parts(h_ref[...])
        dx, dg = _rms_bwd(dn_ref[...], xh, r, g_ref[...])
        dh_ref[...] = dres_ref[...] + dx
        dgain_ref[...] += dg

    row = pl.BlockSpec((tm, d), lambda i: (i, 0))
    vec = pl.BlockSpec((1, d), lambda i: (0, 0))
    return pl.pallas_call(
        body, name=name, grid=(tp // tm,), in_specs=[row, vec, row, row], out_specs=[row, vec],
        out_shape=[jax.ShapeDtypeStruct((tp, d), F32), jax.ShapeDtypeStruct((1, d), F32)],
        compiler_params=_params("arbitrary"))(h, gain, dn, dres)


def _head_post(a, grp):
    a = _silu(a)
    r = lax.rsqrt(jnp.sum(a * a, axis=-1, keepdims=True) + EPS)
    if isinstance(grp, int):
        return a if grp == 2 else a * r * (HEAD_DIM ** -0.5 if grp == 0 else 1.0)
    scale = jnp.where(grp == 0, HEAD_DIM ** -0.5, 1.0).astype(F32)
    return jnp.where(grp == 2, a, a * r * scale)


def _conv_taps(ext_sc, w_ref, tm):
    c = None
    for i in range(CONV_K):
        s = CONV_K - 1 - i
        term = w_ref[i:i + 1, :] * ext_sc[8 - s:8 - s + tm, :]
        c = term if c is None else c + term
    return c


def _conv_fwd(proj, conv_w, hv, name):
    tp = proj.shape[0]
    tm = _tile(tp, 256, 8)
    nh = hv // HEAD_DIM

    def body(x_ref, halo_ref, w_ref, o_ref, ext_sc):
        i, grp = pl.program_id(0), pl.program_id(1)
        ext_sc[0:8, :] = jnp.where(i == 0, 0.0, halo_ref[...])
        ext_sc[8:, :] = x_ref[...]
        c = _conv_taps(ext_sc, w_ref, tm)
        for h in range(nh):
            sl = slice(h * HEAD_DIM, (h + 1) * HEAD_DIM)
            o_ref[:, sl] = _head_post(c[:, sl], grp)

    return pl.pallas_call(
        body, name=name, grid=(tp // tm, 3),
        in_specs=[pl.BlockSpec((tm, hv), lambda i, g: (i, g)),
                  pl.BlockSpec((8, hv), lambda i, g: (jnp.maximum(i * (tm // 8) - 1, 0), g)),
                  pl.BlockSpec((CONV_K, hv), lambda i, g: (0, g))],
        out_specs=pl.BlockSpec((tm, hv), lambda i, g: (i, g)),
        out_shape=jax.ShapeDtypeStruct((tp, 3 * hv), F32),
        scratch_shapes=[pltpu.VMEM((tm + 8, hv), F32)],
        compiler_params=_params("parallel", "arbitrary"))(proj, proj, conv_w)


def _conv_bwd_pre(proj, conv_w, dy, grp, hv, name):
    tp = proj.shape[0]
    tm = _tile(tp, 256, 8)
    nh = hv // HEAD_DIM

    def body(x_ref, halo_ref, w_ref, dy_ref, dc_ref, dw_ref, ext_sc):
        i = pl.program_id(0)
        ext_sc[0:8, :] = jnp.where(i == 0, 0.0, halo_ref[...])
        ext_sc[8:, :] = x_ref[...]
        c = _conv_taps(ext_sc, w_ref, tm)
        for h in range(nh):
            sl = slice(h * HEAD_DIM, (h + 1) * HEAD_DIM)
            _, vjp = jax.vjp(lambda a: _head_post(a, grp), c[:, sl])
            dc_ref[:, sl] = vjp(dy_ref[:, sl])[0]

        @pl.when(i == 0)
        def _():
            dw_ref[...] = jnp.zeros_like(dw_ref)

        dc = dc_ref[...]
        for k in range(CONV_K):
            s = CONV_K - 1 - k
            dw_ref[k:k + 1, :] += jnp.sum(dc * ext_sc[8 - s:8 - s + tm, :], axis=0, keepdims=True)

    return pl.pallas_call(
        body, name=name, grid=(tp // tm,),
        in_specs=[pl.BlockSpec((tm, hv), lambda i: (i, grp)),
                  pl.BlockSpec((8, hv), lambda i: (jnp.maximum(i * (tm // 8) - 1, 0), grp)),
                  pl.BlockSpec((CONV_K, hv), lambda i: (0, grp)),
                  pl.BlockSpec((tm, hv), lambda i: (i, 0))],
        out_specs=[pl.BlockSpec((tm, hv), lambda i: (i, 0)), pl.BlockSpec((CONV_K, hv), lambda i: (0, 0))],
        out_shape=[jax.ShapeDtypeStruct((tp, hv), F32), jax.ShapeDtypeStruct((CONV_K, hv), F32)],
        scratch_shapes=[pltpu.VMEM((tm + 8, hv), F32)],
        compiler_params=_params("arbitrary"))(proj, proj, conv_w, dy)


def _conv_bwd_in(dc, conv_w, grp, name):
    tp, hv = dc.shape
    tm = _tile(tp, 256, 8)
    ni = tp // tm

    def body(dc_ref, halo_ref, w_ref, dx_ref, ext_sc):
        i = pl.program_id(0)
        ext_sc[0:tm, :] = dc_ref[...]
        ext_sc[tm:, :] = jnp.where(i == ni - 1, 0.0, halo_ref[...])
        dx = None
        for k in range(CONV_K):
            s = CONV_K - 1 - k
            term = w_ref[k:k + 1, :] * ext_sc[s:s + tm, :]
            dx = term if dx is None else dx + term
        dx_ref[...] = dx.astype(BF16)

    return pl.pallas_call(
        body, name=name, grid=(ni,),
        in_specs=[pl.BlockSpec((tm, hv), lambda i: (i, 0)),
                  pl.BlockSpec((8, hv), lambda i: (jnp.minimum((i + 1) * (tm // 8), tp // 8 - 1), 0)),
                  pl.BlockSpec((CONV_K, hv), lambda i: (0, grp))],
        out_specs=pl.BlockSpec((tm, hv), lambda i: (i, 0)),
        out_shape=jax.ShapeDtypeStruct((tp, hv), BF16),
        scratch_shapes=[pltpu.VMEM((tm + 8, hv), F32)],
        compiler_params=_params("parallel"))(dc, dc, conv_w)


def _gdn_gates(ba, alog, dtb):
    hi_nn, _, _ = _make_mm(True)
    c = ba.shape[0]
    tril = (lax.broadcasted_iota(jnp.int32, (c, c), 0) >= lax.broadcasted_iota(jnp.int32, (c, c), 1)).astype(F32)
    x = ba + dtb
    softplus = jnp.maximum(x, 0.0) + jnp.log1p(jnp.exp(-jnp.abs(x)))
    return hi_nn(tril, -jnp.exp(alog) * softplus), jax.nn.sigmoid(ba)


def _gdn_heads(states, qs, ks, vs, gc_all, beta_all):
    mm_nn, mm_nt, mm_tn = _make_mm(False)
    hi_nn, hi_nt, _ = _make_mm(True)
    nh = len(qs)
    heads = range(nh)
    c = qs[0].shape[0]
    lane = lax.broadcasted_iota(jnp.int32, (c, LANES), 1)
    last_row = (lax.broadcasted_iota(jnp.int32, (c, 1), 0) == c - 1).astype(F32)
    ri = lax.broadcasted_iota(jnp.int32, (c, c), 0)
    ci = lax.broadcasted_iota(jnp.int32, (c, c), 1)
    causal = ri >= ci
    strict = ri > ci
    eye = (ri == ci).astype(F32)
    sel_a = [(lane == nh + h).astype(F32) for h in heads]
    sel_b = [(lane == h).astype(F32) for h in heads]

    gcol = [jnp.sum(gc_all * sel_a[h], axis=1, keepdims=True) for h in heads]
    grow = [hi_nt(sel_a[h], gc_all) for h in heads]
    beta = [jnp.sum(beta_all * sel_b[h], axis=1, keepdims=True) for h in heads]
    decay = [jnp.where(causal, jnp.exp(jnp.where(causal, gcol[h] - grow[h], 0.0)), 0.0) for h in heads]
    kb = [ks[h] * beta[h] for h in heads]
    kk = [mm_nt(kb[h], ks[h]) for h in heads]
    qk = [mm_nt(qs[h], ks[h]) for h in heads]
    xp = [-jnp.where(strict, kk[h] * decay[h], 0.0) for h in heads]
    t_inv = [eye + xp[h] for h in heads]
    for _ in range(int(math.log2(c)) - 1):
        xp = [hi_nn(xp[h], xp[h]) for h in heads]
        t_inv = [t_inv[h] + hi_nn(t_inv[h], xp[h]) for h in heads]
    eg = [jnp.exp(gcol[h]) for h in heads]
    u = [hi_nn(t_inv[h], vs[h] * beta[h]) for h in heads]
    w = [hi_nn(t_inv[h], kb[h] * eg[h]) for h in heads]
    qk = [qk[h] * decay[h] for h in heads]
    glast = [jnp.sum(gcol[h] * last_row, axis=0, keepdims=True) for h in heads]
    ws = [mm_nn(w[h], states[h]) for h in heads]
    qs_state = [mm_nn(qs[h] * eg[h], states[h]) for h in heads]
    v_new = [u[h] - ws[h] for h in heads]
    intra = [mm_nn(qk[h], v_new[h]) for h in heads]
    kv = [mm_tn(ks[h] * jnp.exp(glast[h] - gcol[h]), v_new[h]) for h in heads]
    outs = [qs_state[h] + intra[h] for h in heads]
    new_states = [states[h] * jnp.exp(glast[h]) + kv[h] for h in heads]
    return outs, new_states


def _scan_specs(nh, nc, rev, first_col):
    cidx = (lambda c: nc - 1 - c) if rev else (lambda c: c)
    hv = nh * HEAD_DIM
    cols = [pl.BlockSpec((CHUNK, hv), lambda c, g=g: (cidx(c), first_col + g)) for g in range(3)]
    st = pl.BlockSpec((1, nh, HEAD_DIM, HEAD_DIM), lambda c: (cidx(c), 0, 0, 0))
    act = pl.BlockSpec((CHUNK, hv), lambda c: (cidx(c), 0))
    return cols, st, act


def _gdn_fwd(qkv, proj, alog, dtb, nh, name):
    tp = qkv.shape[0]
    nc = tp // CHUNK

    def body(q_ref, k_ref, v_ref, ba_ref, al_ref, dt_ref, o_ref, st_ref, s_sc):
        @pl.when(pl.program_id(0) == 0)
        def _():
            s_sc[...] = jnp.zeros_like(s_sc)

        gc_all, beta_all = _gdn_gates(ba_ref[...], al_ref[...], dt_ref[...])
        sls = [slice(h * HEAD_DIM, (h + 1) * HEAD_DIM) for h in range(nh)]
        states = [s_sc[h] for h in range(nh)]
        for h in range(nh):
            st_ref[0, h] = states[h]
        outs, new_states = _gdn_heads(states, [q_ref[:, sl] for sl in sls], [k_ref[:, sl] for sl in sls],
                                      [v_ref[:, sl] for sl in sls], gc_all, beta_all)
        for h in range(nh):
            o_ref[:, sls[h]] = outs[h]
            s_sc[h] = new_states[h]

    cols, st, act = _scan_specs(nh, nc, False, 0)
    ba = pl.BlockSpec((CHUNK, LANES), lambda c: (c, 10 * nh))
    vec = pl.BlockSpec((1, LANES), lambda c: (0, 0))
    return pl.pallas_call(
        body, name=name, grid=(nc,), in_specs=cols + [ba, vec, vec], out_specs=[act, st],
        out_shape=[jax.ShapeDtypeStruct((tp, nh * HEAD_DIM), F32),
                   jax.ShapeDtypeStruct((nc, nh, HEAD_DIM, HEAD_DIM), F32)],
        scratch_shapes=[pltpu.VMEM((nh, HEAD_DIM, HEAD_DIM), F32)],
        compiler_params=_params("arbitrary"))(qkv, qkv, qkv, proj, alog, dtb)


def _gdn_bwd(qkv, proj, alog, dtb, states, do, nh, name):
    tp = qkv.shape[0]
    nc = tp // CHUNK

    def body(q_ref, k_ref, v_ref, ba_ref, al_ref, dt_ref, st_ref, do_ref,
             dq_ref, dk_ref, dv_ref, dba_ref, dal_ref, ddt_ref, ds_sc):
        @pl.when(pl.program_id(0) == 0)
        def _():
            ds_sc[...] = jnp.zeros_like(ds_sc)
            dal_ref[...] = jnp.zeros_like(dal_ref)
            ddt_ref[...] = jnp.zeros_like(ddt_ref)

        (gc_all, beta_all), gates_vjp = jax.vjp(_gdn_gates, ba_ref[...], al_ref[...], dt_ref[...])
        sls = [slice(h * HEAD_DIM, (h + 1) * HEAD_DIM) for h in range(nh)]
        _, vjp = jax.vjp(_gdn_heads, [st_ref[0, h] for h in range(nh)], [q_ref[:, sl] for sl in sls],
                         [k_ref[:, sl] for sl in sls], [v_ref[:, sl] for sl in sls], gc_all, beta_all)
        ds, dq, dk, dv, dgc, dbeta = vjp(([do_ref[:, sl] for sl in sls], [ds_sc[h] for h in range(nh)]))
        for h in range(nh):
            ds_sc[h] = ds[h]
            dq_ref[:, sls[h]] = dq[h]
            dk_ref[:, sls[h]] = dk[h]
            dv_ref[:, sls[h]] = dv[h]
        dba, dal, ddt = gates_vjp((dgc, dbeta))
        dba_ref[...] = dba
        dal_ref[...] += dal
        ddt_ref[...] += ddt

    cols, st, act = _scan_specs(nh, nc, True, 0)
    ba = pl.BlockSpec((CHUNK, LANES), lambda c: (nc - 1 - c, 10 * nh))
    vec = pl.BlockSpec((1, LANES), lambda c: (0, 0))
    return pl.pallas_call(
        body, name=name, grid=(nc,), in_specs=cols + [ba, vec, vec, st, act],
        out_specs=[act, act, act, pl.BlockSpec((CHUNK, LANES), lambda c: (nc - 1 - c, 0)), vec, vec],
        out_shape=[jax.ShapeDtypeStruct((tp, nh * HEAD_DIM), F32)] * 3
                  + [jax.ShapeDtypeStruct((tp, LANES), F32), jax.ShapeDtypeStruct((1, LANES), F32),
                     jax.ShapeDtypeStruct((1, LANES), F32)],
        scratch_shapes=[pltpu.VMEM((nh, HEAD_DIM, HEAD_DIM), F32)],
        compiler_params=_params("arbitrary"))(qkv, qkv, qkv, proj, alog, dtb, states, do)


def _swap_pairs(t):
    lane = lax.broadcasted_iota(jnp.int32, t.shape, 1)
    n = t.shape[1]
    return jnp.where(lane % 2 == 0, pltpu.roll(t, n - 1, 1), pltpu.roll(t, 1, 1))


def _rot(t, cos, sin_signed):
    return t * cos + _swap_pairs(t) * sin_signed


def _rot_t(dt, cos, sin_signed):
    return dt * cos + _swap_pairs(dt * sin_signed)


def _ret_heads(states, qs, ks, vs, dec, xi, zeta, cd):
    mm_nn, mm_nt, mm_tn = _make_mm(False)
    heads = range(len(qs))
    scores = [mm_nt(qs[h], ks[h]) for h in heads]
    inter = [mm_nn(qs[h] * xi[h], states[h]) for h in heads]
    kv = [mm_tn(ks[h] * zeta[h], vs[h]) for h in heads]
    intra = [mm_nn(scores[h] * dec[h], vs[h]) for h in heads]
    return [intra[h] + inter[h] for h in heads], [states[h] * cd[h] + kv[h] for h in heads]


def _ret_table_specs(nh, nc, rev):
    cidx = (lambda c: nc - 1 - c) if rev else (lambda c: c)
    rope = pl.BlockSpec((CHUNK, HEAD_DIM), lambda c: (cidx(c), 0))
    dec = pl.BlockSpec((nh, CHUNK, CHUNK), lambda c: (0, 0, 0))
    tab = pl.BlockSpec((nh, CHUNK, HEAD_DIM), lambda c: (0, 0, 0))
    cd = pl.BlockSpec((nh, 8, HEAD_DIM), lambda c: (0, 0, 0))
    return [rope, rope, dec, tab, tab, cd]


def _ret_fwd(proj, cos, sin, dec, xi, zeta, cd, nh, name):
    tp = proj.shape[0]
    nc = tp // CHUNK
    kscale = HEAD_DIM ** -0.5

    def body(q_ref, k_ref, v_ref, cos_ref, sin_ref, dec_ref, xi_ref, zeta_ref, cd_ref, o_ref, st_ref, s_sc):
        @pl.when(pl.program_id(0) == 0)
        def _():
            s_sc[...] = jnp.zeros_like(s_sc)

        cos_t, sin_t = cos_ref[...], sin_ref[...]
        heads = range(nh)
        sls = [slice(h * HEAD_DIM, (h + 1) * HEAD_DIM) for h in heads]
        states = [s_sc[h] for h in heads]
        for h in heads:
            st_ref[0, h] = states[h]
        qs = [_rot(q_ref[:, sl], cos_t, sin_t) for sl in sls]
        ks = [_rot(k_ref[:, sl], cos_t, sin_t) * kscale for sl in sls]
        outs, new_states = _ret_heads(states, qs, ks, [v_ref[:, sl] for sl in sls], [dec_ref[h] for h in heads],
                                      [xi_ref[h] for h in heads], [zeta_ref[h] for h in heads],
                                      [cd_ref[h][0:1, :] for h in heads])
        for h in heads:
            o_ref[:, sls[h]] = outs[h]
            s_sc[h] = new_states[h]

    cols, st, act = _scan_specs(nh, nc, False, 3)
    return pl.pallas_call(
        body, name=name, grid=(nc,), in_specs=cols + _ret_table_specs(nh, nc, False), out_specs=[act, st],
        out_shape=[jax.ShapeDtypeStruct((tp, nh * HEAD_DIM), F32),
                   jax.ShapeDtypeStruct((nc, nh, HEAD_DIM, HEAD_DIM), F32)],
        scratch_shapes=[pltpu.VMEM((nh, HEAD_DIM, HEAD_DIM), F32)],
        compiler_params=_params("arbitrary"))(proj, proj, proj, cos, sin, dec, xi, zeta, cd)


def _ret_bwd(proj, cos, sin, dec, xi, zeta, cd, states, do, nh, name):
    tp = proj.shape[0]
    nc = tp // CHUNK
    kscale = HEAD_DIM ** -0.5

    def body(q_ref, k_ref, v_ref, cos_ref, sin_ref, dec_ref, xi_ref, zeta_ref, cd_ref, st_ref, do_ref,
             dq_ref, dk_ref, dv_ref, ds_sc):
        @pl.when(pl.program_id(0) == 0)
        def _():
            ds_sc[...] = jnp.zeros_like(ds_sc)

        cos_t, sin_t = cos_ref[...], sin_ref[...]
        heads = range(nh)
        sls = [slice(h * HEAD_DIM, (h + 1) * HEAD_DIM) for h in heads]
        qs = [_rot(q_ref[:, sl], cos_t, sin_t) for sl in sls]
        ks = [_rot(k_ref[:, sl], cos_t, sin_t) * kscale for sl in sls]
        fn = functools.partial(_ret_heads, dec=[dec_ref[h] for h in heads], xi=[xi_ref[h] for h in heads],
                               zeta=[zeta_ref[h] for h in heads], cd=[cd_ref[h][0:1, :] for h in heads])
        _, vjp = jax.vjp(fn, [st_ref[0, h] for h in heads], qs, ks, [v_ref[:, sl] for sl in sls])
        ds, dq, dk, dv = vjp(([do_ref[:, sl] for sl in sls], [ds_sc[h] for h in heads]))
        for h in heads:
            ds_sc[h] = ds[h]
            dq_ref[:, sls[h]] = _rot_t(dq[h], cos_t, sin_t).astype(BF16)
            dk_ref[:, sls[h]] = _rot_t(dk[h] * kscale, cos_t, sin_t).astype(BF16)
            dv_ref[:, sls[h]] = dv[h].astype(BF16)

    cols, st, act = _scan_specs(nh, nc, True, 3)
    return pl.pallas_call(
        body, name=name, grid=(nc,), in_specs=cols + _ret_table_specs(nh, nc, True) + [st, act],
        out_specs=[act, act, act],
        out_shape=[jax.ShapeDtypeStruct((tp, nh * HEAD_DIM), BF16)] * 3,
        scratch_shapes=[pltpu.VMEM((nh, HEAD_DIM, HEAD_DIM), F32)],
        compiler_params=_params("arbitrary"))(proj, proj, proj, cos, sin, dec, xi, zeta, cd, states, do)


def _gdn_out(o, z, gnorm):
    return o * lax.rsqrt(jnp.mean(o * o, axis=-1, keepdims=True) + EPS) * gnorm * _silu(z)


def _ret_out(o, rg, rnorm):
    mu = jnp.mean(o, axis=-1, keepdims=True)
    var = jnp.mean(jnp.square(o - mu), axis=-1, keepdims=True)
    return _silu(rg) * ((o - mu) * lax.rsqrt(var + EPS) * rnorm)


def _post_specs(tm, hv, d):
    row = lambda col: pl.BlockSpec((tm, hv), lambda i: (i, col))
    return dict(
        oa=row(0), ob=row(0), z=row(6), rg=row(7), ga=row(8), gb=row(9),
        gnorm=pl.BlockSpec((1, HEAD_DIM), lambda i: (0, 0)), rnorm=pl.BlockSpec((1, hv), lambda i: (0, 0)),
        w=pl.BlockSpec((hv, d), lambda i: (0, 0)), res=pl.BlockSpec((tm, d), lambda i: (i, 0)))


def _post_fwd(oa, ob, proj, gnorm, rnorm, wbg, wbr, wo, h1, name):
    tp, d = h1.shape
    hv = oa.shape[1]
    nh = hv // HEAD_DIM
    tm = _tile(tp, 256, 8)

    def body(oa_ref, ob_ref, z_ref, rg_ref, ga_ref, gb_ref, gn_ref, rn_ref, wbg_ref, wbr_ref, wo_ref, h_ref,
             o_ref, ya_sc, yb_sc):
        for h in range(nh):
            sl = slice(h * HEAD_DIM, (h + 1) * HEAD_DIM)
            ya_sc[:, sl] = _gdn_out(oa_ref[:, sl], z_ref[:, sl], gn_ref[...]).astype(BF16)
            yb_sc[:, sl] = _ret_out(ob_ref[:, sl], rg_ref[:, sl], rn_ref[:, sl]).astype(BF16)
        pa = jnp.dot(ya_sc[...], wbg_ref[...], preferred_element_type=F32)
        pb = jnp.dot(yb_sc[...], wbr_ref[...], preferred_element_type=F32)
        merged = jax.nn.sigmoid(ga_ref[...]) * pa + jax.nn.sigmoid(gb_ref[...]) * pb
        o_ref[...] = h_ref[...] + jnp.dot(merged.astype(BF16), wo_ref[...], preferred_element_type=F32)

    sp = _post_specs(tm, hv, d)
    return pl.pallas_call(
        body, name=name, grid=(tp // tm,),
        in_specs=[sp["oa"], sp["ob"], sp["z"], sp["rg"], sp["ga"], sp["gb"], sp["gnorm"], sp["rnorm"],
                  sp["w"], sp["w"], sp["w"], sp["res"]],
        out_specs=sp["res"], out_shape=jax.ShapeDtypeStruct((tp, d), F32),
        scratch_shapes=[pltpu.VMEM((tm, hv), BF16), pltpu.VMEM((tm, hv), BF16)],
        compiler_params=_params("parallel"))(oa, ob, proj, proj, proj, proj, gnorm, rnorm, wbg, wbr, wo, h1)


def _post_bwd(oa, ob, proj, gnorm, rnorm, wbg, wbr, wo, dh2, name):
    tp, d = dh2.shape
    hv = oa.shape[1]
    nh = hv // HEAD_DIM
    tm = _tile(tp, 256, 8)

    def body(oa_ref, ob_ref, z_ref, rg_ref, ga_ref, gb_ref, gn_ref, rn_ref, wbg_ref, wbr_ref, wo_ref, dh_ref,
             doa_ref, dob_ref, dg_ref, ya_ref, yb_ref, mg_ref, dpa_ref, dpb_ref, dgn_ref, drn_ref,
             dya_sc, dyb_sc):
        @pl.when(pl.program_id(0) == 0)
        def _():
            dgn_ref[...] = jnp.zeros_like(dgn_ref)
            drn_ref[...] = jnp.zeros_like(drn_ref)

        for h in range(nh):
            sl = slice(h * HEAD_DIM, (h + 1) * HEAD_DIM)
            ya_ref[:, sl] = _gdn_out(oa_ref[:, sl], z_ref[:, sl], gn_ref[...]).astype(BF16)
            yb_ref[:, sl] = _ret_out(ob_ref[:, sl], rg_ref[:, sl], rn_ref[:, sl]).astype(BF16)
        pa = jnp.dot(ya_ref[...], wbg_ref[...], preferred_element_type=F32)
        pb = jnp.dot(yb_ref[...], wbr_ref[...], preferred_element_type=F32)
        sa = jax.nn.sigmoid(ga_ref[...])
        sb = jax.nn.sigmoid(gb_ref[...])
        mg_ref[...] = (sa * pa + sb * pb).astype(BF16)
        dm = lax.dot_general(dh_ref[...].astype(BF16), wo_ref[...], NT, preferred_element_type=F32)
        dpa = (dm * sa).astype(BF16)
        dpb = (dm * sb).astype(BF16)
        dpa_ref[...] = dpa
        dpb_ref[...] = dpb
        dg_ref[:, 2 * hv:3 * hv] = (dm * pa * sa * (1.0 - sa)).astype(BF16)
        dg_ref[:, 3 * hv:4 * hv] = (dm * pb * sb * (1.0 - sb)).astype(BF16)
        dya_sc[...] = lax.dot_general(dpa, wbg_ref[...], NT, preferred_element_type=F32)
        dyb_sc[...] = lax.dot_general(dpb, wbr_ref[...], NT, preferred_element_type=F32)
        for h in range(nh):
            sl = slice(h * HEAD_DIM, (h + 1) * HEAD_DIM)
            _, vjp_a = jax.vjp(_gdn_out, oa_ref[:, sl], z_ref[:, sl], gn_ref[...])
            doa, dz, dgn = vjp_a(dya_sc[:, sl])
            doa_ref[:, sl] = doa
            dg_ref[:, sl] = dz.astype(BF16)
            dgn_ref[...] += dgn
            _, vjp_b = jax.vjp(_ret_out, ob_ref[:, sl], rg_ref[:, sl], rn_ref[:, sl])
            dob, drg, drn = vjp_b(dyb_sc[:, sl])
            dob_ref[:, sl] = dob
            dg_ref[:, hv + h * HEAD_DIM:hv + (h + 1) * HEAD_DIM] = drg.astype(BF16)
            drn_ref[:, sl] += drn

    sp = _post_specs(tm, hv, d)
    act = pl.BlockSpec((tm, hv), lambda i: (i, 0))
    return pl.pallas_call(
        body, name=name, grid=(tp // tm,),
        in_specs=[sp["oa"], sp["ob"], sp["z"], sp["rg"], sp["ga"], sp["gb"], sp["gnorm"], sp["rnorm"],
                  sp["w"], sp["w"], sp["w"], sp["res"]],
        out_specs=[act, act, pl.BlockSpec((tm, 4 * hv), lambda i: (i, 0)), act, act, sp["res"], sp["res"],
                   sp["res"], sp["gnorm"], sp["rnorm"]],
        out_shape=[jax.ShapeDtypeStruct((tp, hv), F32), jax.ShapeDtypeStruct((tp, hv), F32),
                   jax.ShapeDtypeStruct((tp, 4 * hv), BF16), jax.ShapeDtypeStruct((tp, hv), BF16),
                   jax.ShapeDtypeStruct((tp, hv), BF16), jax.ShapeDtypeStruct((tp, d), BF16),
                   jax.ShapeDtypeStruct((tp, d), BF16), jax.ShapeDtypeStruct((tp, d), BF16),
                   jax.ShapeDtypeStruct((1, HEAD_DIM), F32), jax.ShapeDtypeStruct((1, hv), F32)],
        scratch_shapes=[pltpu.VMEM((tm, hv), F32), pltpu.VMEM((tm, hv), F32)],
        compiler_params=_params("arbitrary"))(oa, ob, proj, proj, proj, proj, gnorm, rnorm, wbg, wbr, wo, dh2)


def _final(h3, gain, target, name):
    tp, d = h3.shape
    tm = HEAD_ROWS

    def body(h_ref, g_ref, t_ref, loss_ref, dh_ref, dgain_ref):
        i = pl.program_id(0)

        @pl.when(i == 0)
        def _():
            loss_ref[...] = jnp.zeros_like(loss_ref)
            dgain_ref[...] = jnp.zeros_like(dgain_ref)

        xh, r = _rms_parts(h_ref[...])
        err = jnp.where(i == 0, 0.0, xh * g_ref[...] - t_ref[...])
        dx, dg = _rms_bwd(err * (1.0 / d), xh, r, g_ref[...])
        dh_ref[...] = dx
        dgain_ref[...] += dg
        loss_ref[...] += 0.5 * jnp.sum(jnp.mean(err * err, axis=-1, keepdims=True), axis=0, keepdims=True)

    row = pl.BlockSpec((tm, d), lambda i: (i, 0))
    vec = pl.BlockSpec((1, d), lambda i: (0, 0))
    return pl.pallas_call(
        body, name=name, grid=(tp // tm,),
        in_specs=[row, vec, pl.BlockSpec((tm, d), lambda i: (jnp.maximum(i - 1, 0), 0))],
        out_specs=[pl.BlockSpec((1, LANES), lambda i: (0, 0)), row, vec],
        out_shape=[jax.ShapeDtypeStruct((1, LANES), F32), jax.ShapeDtypeStruct((tp, d), F32),
                   jax.ShapeDtypeStruct((1, d), F32)],
        compiler_params=_params("arbitrary"))(h3, gain, target)


def _peer(k):
    x, y, c = lax.axis_index("x"), lax.axis_index("y"), lax.axis_index("c")
    return (1 - x if k & 4 else x, 1 - y if k & 2 else y, 1 - c if k & 1 else c)


def _my_index():
    return 4 * lax.axis_index("x") + 2 * lax.axis_index("y") + lax.axis_index("c")


def _exchange(bufs, scatter, name):
    n = len(bufs)

    def body(*refs):
        _exchange_copies(refs[:n], refs[n:2 * n], refs[2 * n:], scatter, True, True)

    hbm, out_shape, sems = _exchange_refs(bufs)
    return pl.pallas_call(
        body, name=name, in_specs=hbm, out_specs=hbm, out_shape=out_shape, scratch_shapes=sems,
        compiler_params=pltpu.CompilerParams(has_side_effects=True))(*bufs)


def _exchange_refs(bufs):
    n = len(bufs)
    return ([pl.BlockSpec(memory_space=pl.ANY)] * n,
            [jax.ShapeDtypeStruct((N_DEV,) + b.shape[-2:], b.dtype) for b in bufs],
            [pltpu.SemaphoreType.DMA(((N_DEV - 1) * n,)), pltpu.SemaphoreType.DMA(((N_DEV - 1) * n,)),
             pltpu.SemaphoreType.DMA((n,))])


def _exchange_copies(x_refs, out_refs, sems, scatter, start, wait):
    n = len(x_refs)
    send_sems, recv_sems, local_sems = sems
    me = _my_index()
    copies = []
    for a in range(n):
        copies.append(pltpu.make_async_copy(x_refs[a].at[me] if scatter else x_refs[a], out_refs[a].at[me],
                                            local_sems.at[a]))
    sends = []
    arrivals = []
    for k in range(1, N_DEV):
        x, y, c = _peer(k)
        peer = 4 * x + 2 * y + c
        for a in range(n):
            sem = (k - 1) * n + a
            sends.append(pltpu.make_async_remote_copy(
                src_ref=x_refs[a].at[peer] if scatter else x_refs[a], dst_ref=out_refs[a].at[me],
                send_sem=send_sems.at[sem], recv_sem=recv_sems.at[sem],
                device_id=(x, y, c), device_id_type=pl.DeviceIdType.MESH))
            landed = out_refs[a].at[peer]
            arrivals.append(pltpu.make_async_remote_copy(
                src_ref=landed, dst_ref=landed, send_sem=send_sems.at[sem], recv_sem=recv_sems.at[sem],
                device_id=(x, y, c), device_id_type=pl.DeviceIdType.MESH))
    if start:
        for cp in copies + sends:
            cp.start()
    if wait:
        for cp in arrivals:
            cp.wait_recv()
        for cp in sends:
            cp.wait_send()
        for cp in copies:
            cp.wait()


def _carried_call(body, carry, first, last, *, name, grid, in_specs, out_specs, out_shape, scratch_shapes=()):
    in_specs, out_specs, out_shape = list(in_specs), list(out_specs), list(out_shape)
    semantics = ("arbitrary",) * len(grid)
    if carry is None:
        call = pl.pallas_call(body, name=name, grid=grid, in_specs=in_specs, out_specs=out_specs,
                              out_shape=out_shape, scratch_shapes=list(scratch_shapes),
                              compiler_params=_params(*semantics))
        return lambda *args: (call(*args), [])
    bufs, scatter = carry
    n, n_in, n_out, n_scratch = len(bufs), len(in_specs), len(out_specs), len(scratch_shapes)
    hbm, x_shapes, sems = _exchange_refs(bufs)

    def full_body(*refs):
        ins, x_refs = refs[:n_in], refs[n_in:n_in + n]
        outs, xo_refs = refs[n_in + n:n_in + n + n_out], refs[n_in + n + n_out:n_in + 2 * n + n_out]
        scratch = refs[n_in + 2 * n + n_out:n_in + 2 * n + n_out + n_scratch]
        x_sems = refs[n_in + 2 * n + n_out + n_scratch:]

        @pl.when(first())
        def _():
            _exchange_copies(x_refs, xo_refs, x_sems, scatter, True, False)

        body(*ins, *outs, *scratch)

        @pl.when(last())
        def _():
            _exchange_copies(x_refs, xo_refs, x_sems, scatter, False, True)

    call = pl.pallas_call(full_body, name=name, grid=grid, in_specs=in_specs + hbm, out_specs=out_specs + hbm,
                          out_shape=out_shape + x_shapes, scratch_shapes=list(scratch_shapes) + sems,
                          compiler_params=_params(*semantics))

    def run(*args):
        res = call(*args, *bufs)
        return res[:n_out], res[n_out:]
    return run


def _adamw(w, g, m, v, name):
    r, c = w.shape
    parts = g.ndim == 3
    tr = _tile(r, 256, 16 if parts else 8)
    c1 = 1.0 - ADAM_B1 ** ADAM_STEP
    c2 = 1.0 - ADAM_B2 ** ADAM_STEP

    def body(w_ref, g_ref, m_ref, v_ref, go_ref, d_ref, mo_ref, vo_ref):
        if parts:
            g = g_ref[0].astype(F32)
            for q in range(1, N_DEV):
                g = g + g_ref[q].astype(F32)
        else:
            g = g_ref[...]
        m = ADAM_B1 * m_ref[...] + (1.0 - ADAM_B1) * g
        v = ADAM_B2 * v_ref[...] + (1.0 - ADAM_B2) * (g * g)
        go_ref[...] = g
        d_ref[...] = -ADAM_LR * ((m / c1) / (jnp.sqrt(v / c2) + ADAM_EPS) + ADAM_WD * w_ref[...])
        mo_ref[...] = m
        vo_ref[...] = v

    blk = pl.BlockSpec((tr, c), lambda i: (i, 0))
    g_spec = pl.BlockSpec((N_DEV, tr, c), lambda i: (0, i, 0)) if parts else blk
    return pl.pallas_call(
        body, name=name, grid=(r // tr,), in_specs=[blk, g_spec, blk, blk], out_specs=[blk] * 4,
        out_shape=[jax.ShapeDtypeStruct((r, c), F32)] * 4,
        compiler_params=_params("parallel"))(w, g, m, v)


def _win_segments(hv, nh):
    o_z, o_b = 3 * hv, 4 * hv
    o_r = o_b + 2 * nh
    return [(0, 0, 3 * hv), (3 * hv, o_r, 3 * hv), (6 * hv, o_z, hv), (7 * hv, o_r + 3 * hv, 3 * hv),
            (10 * hv, o_b, 2 * nh)]


def _win_from_shards(shards, hv, nh):
    _, d, cs = shards.shape
    pieces = []
    for _, src, width in _win_segments(hv, nh):
        lo = src
        while lo < src + width:
            p = lo // cs
            hi = min(src + width, (p + 1) * cs)
            pieces.append(shards[p][:, lo - p * cs:hi - p * cs])
            lo = hi
    pieces.append(jnp.zeros((d, LANES - 2 * nh), shards.dtype))
    return jnp.concatenate(pieces, axis=1)


def _win_grad_to_shards(g, hv, nh, cs):
    segments = _win_segments(hv, nh)
    shards = []
    for p in range(N_DEV):
        pieces = []
        lo = p * cs
        while lo < (p + 1) * cs:
            here, src, width = next(s for s in segments if s[1] <= lo < s[1] + s[2])
            hi = min((p + 1) * cs, src + width)
            pieces.append(g[:, here + lo - src:here + hi - src])
            lo = hi
        shards.append(jnp.concatenate(pieces, axis=1))
    return jnp.stack(shards)


def _rope_tables(tp):
    pos = jnp.arange(tp, dtype=F32) - float(PAD_FRONT)
    inv = 1.0 / (ROPE_BASE ** jnp.linspace(0.0, 1.0, HEAD_DIM // 2, dtype=F32))
    ang = pos[:, None] * inv[None, :]
    cos = jnp.repeat(jnp.cos(ang), 2, axis=1)
    sin = jnp.repeat(jnp.sin(ang), 2, axis=1) * jnp.tile(jnp.array([-1.0, 1.0], F32), HEAD_DIM // 2)[None, :]
    return cos, sin


def _retention_tables(nh):
    log_gamma = jnp.log1p(-jnp.exp2(-5.0 - jnp.arange(nh, dtype=F32)))
    pos = jnp.arange(CHUNK, dtype=F32)
    causal = pos[:, None] >= pos[None, :]
    diff = pos[:, None] - pos[None, :]
    dec = jnp.where(causal, jnp.exp(jnp.where(causal, diff, 0.0) * log_gamma[:, None, None]), 0.0)
    ones = jnp.ones((1, 1, HEAD_DIM), F32)
    xi = jnp.exp((pos + 1.0)[None, :] * log_gamma[:, None])[:, :, None] * ones
    zeta = jnp.exp((CHUNK - 1.0 - pos)[None, :] * log_gamma[:, None])[:, :, None] * ones
    cd = jnp.exp(CHUNK * log_gamma)[:, None, None] * jnp.ones((1, 8, HEAD_DIM), F32)
    return dec, xi, zeta, cd


SHARDED = ("meta_tokens", "ffn1_w_in", "ffn1_w_out", "w_in", "gdn_conv_w", "w_branch_gdn", "w_branch_ret",
           "w_out", "ffn2_w_in", "ffn2_w_out")
COLUMN_SHARDED = ("meta_tokens", "ffn1_w_in", "w_in", "gdn_conv_w", "ffn2_w_in")
EXACT_F32 = ("meta_tokens", "gdn_conv_w")
REPLICATED = ("ffn1_norm", "mix_norm", "gdn_a_log", "gdn_dt_bias", "gdn_out_norm", "ret_out_norm", "ffn2_norm",
              "final_norm")
WEIGHTS = ("meta_tokens", "ffn1_norm", "ffn1_w_in", "ffn1_w_out", "mix_norm", "w_in", "gdn_conv_w", "gdn_a_log",
           "gdn_dt_bias", "gdn_out_norm", "ret_out_norm", "w_branch_gdn", "w_branch_ret", "w_out", "ffn2_norm",
           "ffn2_w_in", "ffn2_w_out", "final_norm")


def _as2d(a):
    if a.ndim == 3:
        return a[0]
    if a.ndim == 1:
        return a[None, :]
    return a


def _rows_of(shards):
    return shards.reshape(-1, shards.shape[2])


def _cols_of(shards):
    return shards.transpose(1, 0, 2).reshape(shards.shape[1], -1)


def _row_shards(a):
    return a.reshape(N_DEV, -1, a.shape[1])


def _col_shards(a):
    return a.reshape(a.shape[0], N_DEV, -1).transpose(1, 0, 2)


GATHER_FIRST = ("meta_tokens", "ffn1_w_in", "ffn1_w_out")
GATHER_BEHIND_FFN1 = ("w_in", "gdn_conv_w")
GATHER_BEHIND_PROJ = ("w_branch_gdn", "w_branch_ret", "w_out", "ffn2_w_in", "ffn2_w_out")
SCATTER_BEHIND_DN2 = ("ffn2_w_in", "ffn2_w_out", "w_branch_gdn", "w_branch_ret", "w_out")
SCATTER_BEHIND_FFN1 = ("w_in", "gdn_conv_w")
SCATTER_LAST = ("meta_tokens", "ffn1_w_in", "ffn1_w_out")


def _device_step(x, target, send, rep):
    seq, d = x.shape
    tp = HEAD_ROWS + seq
    hv = d
    nh = hv // HEAD_DIM
    assert seq % CHUNK == 0 and tp % HEAD_ROWS == 0
    bf16_shards = lambda grads, names: [grads[n].astype(BF16) for n in names]

    pad_lanes = lambda row: jnp.pad(row, ((0, 0), (nh, LANES - 2 * nh)))
    alog = pad_lanes(rep["gdn_a_log"])
    dtb = pad_lanes(rep["gdn_dt_bias"])
    cos, sin = _rope_tables(tp)
    dec, xi, zeta, cd = _retention_tables(nh)

    got = dict(zip(GATHER_FIRST, _exchange([send[n] for n in GATHER_FIRST], False, "gather_ffn1")))
    h0 = jnp.concatenate([jnp.zeros((PAD_FRONT, d), F32), _cols_of(got["meta_tokens"]), x], axis=0)
    f1i, f1o = got["ffn1_w_in"], _rows_of(got["ffn1_w_out"])
    h1, moved = _ffn_fwd(h0, rep["ffn1_norm"], f1i, f1o, "ffn1_fwd", ([send[n] for n in GATHER_BEHIND_FFN1], False))
    got.update(zip(GATHER_BEHIND_FFN1, moved))
    wp = _win_from_shards(got["w_in"], hv, nh)
    conv_w = _cols_of(got["gdn_conv_w"])
    (proj, n2), moved = _proj_fwd(h1, rep["mix_norm"], wp, "proj_fwd",
                                  ([send[n] for n in GATHER_BEHIND_PROJ], False))
    got.update(zip(GATHER_BEHIND_PROJ, moved))
    wbg, wbr, wo = _rows_of(got["w_branch_gdn"]), _rows_of(got["w_branch_ret"]), _rows_of(got["w_out"])
    f2i, f2o = got["ffn2_w_in"], _rows_of(got["ffn2_w_out"])
    qkv = _conv_fwd(proj, conv_w, hv, "conv_fwd")
    oa, s_gdn = _gdn_fwd(qkv, proj, alog, dtb, nh, "gdn_fwd")
    ob, s_ret = _ret_fwd(proj, cos, sin, dec, xi, zeta, cd, nh, "ret_fwd")
    h2 = _post_fwd(oa, ob, proj, rep["gdn_out_norm"], rep["ret_out_norm"], wbg, wbr, wo, h1, "post_fwd")
    h3, _ = _ffn_fwd(h2, rep["ffn2_norm"], f2i, f2o, "ffn2_fwd")
    loss_row, dh3, d_final = _final(h3, rep["final_norm"], target, "final")

    (dh2, d_f2n, n3, hid2, dag2, dau2), _ = _ffn_bwd(h2, dh3, rep["ffn2_norm"], f2i, f2o, "ffn2_bwd")
    grads = {"ffn2_w_in": jnp.concatenate([_matmul_tn_blocks(n3, dag2, "ffn2_dwg"),
                                           _matmul_tn_blocks(n3, dau2, "ffn2_dwu")]),
             "ffn2_w_out": _row_shards(_matmul_tn_blocks(hid2, dh3, "ffn2_dwo", 0.5))}

    doa, dob, dgate, ya, yb, merged, dpa, dpb, d_gn, d_rn = _post_bwd(
        oa, ob, proj, rep["gdn_out_norm"], rep["ret_out_norm"], wbg, wbr, wo, dh2, "post_bwd")
    grads["w_branch_gdn"] = _row_shards(_matmul_tn(ya, dpa, "dw_branch_gdn"))
    grads["w_branch_ret"] = _row_shards(_matmul_tn(yb, dpb, "dw_branch_ret"))
    grads["w_out"] = _row_shards(_matmul_tn(merged, dh2, "dw_out"))

    d_ret = _ret_bwd(proj, cos, sin, dec, xi, zeta, cd, s_ret, dob, nh, "ret_bwd")
    gdn_grads = _gdn_bwd(qkv, proj, alog, dtb, s_gdn, doa, nh, "gdn_bwd")
    dba, d_alog, d_dtb = gdn_grads[3:]
    dpre, g_conv = [], []
    for grp, tag in enumerate("qkv"):
        dc, dw = _conv_bwd_pre(proj, conv_w, gdn_grads[grp], grp, hv, "conv_bwd_pre_" + tag)
        dpre.append(_conv_bwd_in(dc, conv_w, grp, "conv_bwd_in_" + tag))
        g_conv.append(dw)
    grads["gdn_conv_w"] = _col_shards(jnp.concatenate(g_conv, axis=1))

    wide = dpre + list(d_ret) + [dgate]
    dn2, moved = _matmul_nt_parts(wide, wp[:, :10 * hv], None, "dn2_wide",
                                  (bf16_shards(grads, SCATTER_BEHIND_DN2), True))
    parts = dict(zip(SCATTER_BEHIND_DN2, moved))
    dn2, _ = _matmul_nt_parts([dba], wp[:, 10 * hv:], dn2, "dn2_beta_alpha")
    g_wp = [_matmul_tn(n2, dg, "dw_in_%d" % idx) for idx, dg in enumerate(wide + [dba])]
    grads["w_in"] = _win_grad_to_shards(jnp.concatenate(g_wp, axis=1), hv, nh, send["w_in"].shape[1])
    dh1, d_mixn = _norm_bwd(h1, rep["mix_norm"], dn2, dh2, "mix_norm_bwd")

    (dh0, d_f1n, n1, hid1, dag1, dau1), moved = _ffn_bwd(h0, dh1, rep["ffn1_norm"], f1i, f1o, "ffn1_bwd",
                                                         (bf16_shards(grads, SCATTER_BEHIND_FFN1), True))
    parts.update(zip(SCATTER_BEHIND_FFN1, moved))
    grads["ffn1_w_in"] = jnp.concatenate([_matmul_tn_blocks(n1, dag1, "ffn1_dwg"),
                                          _matmul_tn_blocks(n1, dau1, "ffn1_dwu")])
    grads["ffn1_w_out"] = _row_shards(_matmul_tn_blocks(hid1, dh1, "ffn1_dwo", 0.5))
    grads["meta_tokens"] = _col_shards(dh0[PAD_FRONT:HEAD_ROWS])
    parts.update(zip(SCATTER_LAST, _exchange(bf16_shards(grads, SCATTER_LAST), True, "scatter_ffn1")))

    small = {"ffn1_norm": d_f1n, "mix_norm": d_mixn, "gdn_a_log": d_alog[:, nh:2 * nh],
             "gdn_dt_bias": d_dtb[:, nh:2 * nh], "gdn_out_norm": d_gn, "ret_out_norm": d_rn, "ffn2_norm": d_f2n,
             "final_norm": d_final}
    return loss_row[0, 0], dh0[HEAD_ROWS:], parts, small


def kernel(x, meta_tokens, ffn1_norm, ffn1_w_in, ffn1_w_out, mix_norm, w_in, gdn_conv_w, gdn_a_log, gdn_dt_bias, gdn_out_norm, ret_out_norm, w_branch_gdn, w_branch_ret, w_out, ffn2_norm, ffn2_w_in, ffn2_w_out, final_norm, loss_target, m_meta_tokens, m_ffn1_norm, m_ffn1_w_in, m_ffn1_w_out, m_mix_norm, m_w_in, m_gdn_conv_w, m_gdn_a_log, m_gdn_dt_bias, m_gdn_out_norm, m_ret_out_norm, m_w_branch_gdn, m_w_branch_ret, m_w_out, m_ffn2_norm, m_ffn2_w_in, m_ffn2_w_out, m_final_norm, v_meta_tokens, v_ffn1_norm, v_ffn1_w_in, v_ffn1_w_out, v_mix_norm, v_w_in, v_gdn_conv_w, v_gdn_a_log, v_gdn_dt_bias, v_gdn_out_norm, v_ret_out_norm, v_w_branch_gdn, v_w_branch_ret, v_w_out, v_ffn2_norm, v_ffn2_w_in, v_ffn2_w_out, v_final_norm):
    given = dict(locals())
    params = {n: _as2d(given[n]) for n in WEIGHTS}
    local = {n: params[n] for n in SHARDED}
    rep = {n: params[n] for n in REPLICATED}

    send = {n: local[n] if n in EXACT_F32 else local[n].astype(BF16) for n in SHARDED}
    loss_sum, grad_x, parts, small = _device_step(x[0], loss_target[0], send, rep)
    parts.update(zip(REPLICATED, _exchange([small[n] for n in REPLICATED], False, "gather_small_grads")))
    loss = lax.psum(loss_sum, ("x", "y", "c"))

    outs = {}
    for n in WEIGHTS:
        res = _adamw(params[n], parts[n], _as2d(given["m_" + n]), _as2d(given["v_" + n]), "adamw_" + n)
        outs[n] = [r.reshape(given[n].shape) for r in res]
    return (loss, grad_x[None], *[outs[n][0] for n in WEIGHTS], *[outs[n][1] for n in WEIGHTS],
            *[outs[n][2] for n in WEIGHTS], *[outs[n][3] for n in WEIGHTS])
```

```python
import functools
import math

import numpy as np
import jax
import jax.numpy as jnp
from jax import lax
from jax.experimental import pallas as pl
from jax.experimental.pallas import tpu as pltpu

F32 = jnp.float32
BF16 = jnp.bfloat16

N_DEV = 8
N_META = 16
CHUNK = 64
HEAD_DIM = 128
CONV_K = 4
ROPE_BASE = 10000.0
EPS = 1e-6
PAD_FRONT = 240
HEAD_ROWS = PAD_FRONT + N_META
LANES = 128
VMEM_LIMIT_BYTES = 56 * 1024 * 1024

ADAM_LR = 0.001
ADAM_B1 = 0.9
ADAM_B2 = 0.999
ADAM_EPS = 1e-08
ADAM_WD = 0.01
ADAM_STEP = 10

NN = (((1,), (0,)), ((), ()))
NT = (((1,), (1,)), ((), ()))
TN = (((0,), (0,)), ((), ()))


def _tile(n, target, mult):
    best = 0
    for t in range(mult, min(n, target) + 1, mult):
        if n % t == 0:
            best = t
    return best if best else n


def _params(*semantics):
    return pltpu.CompilerParams(dimension_semantics=semantics, vmem_limit_bytes=VMEM_LIMIT_BYTES)


def _split(a, pieces):
    out = []
    for _ in range(pieces - 1):
        part = a.astype(BF16)
        out.append(part)
        a = a - part.astype(F32)
    return out + [a.astype(BF16)]


def _raw_dot(a, b, dims, hi):
    dot = lambda x, y: lax.dot_general(x, y, dims, preferred_element_type=F32)
    if hi:
        (a_hi, a_lo), (b_hi, b_lo) = _split(a, 2), _split(b, 2)
        return dot(a_hi, b_hi) + (dot(a_hi, b_lo) + dot(a_lo, b_hi))
    return dot(a.astype(BF16), b.astype(BF16))


def _mask_dot(mask, x, dims):
    mask = mask.astype(BF16)
    hi, mid, lo = [lax.dot_general(mask, p, dims, preferred_element_type=F32) for p in _split(x, 3)]
    return hi + (mid + lo)


@jax.custom_vjp
def _cumsum_rows(x):
    c = x.shape[0]
    tril = lax.broadcasted_iota(jnp.int32, (c, c), 0) >= lax.broadcasted_iota(jnp.int32, (c, c), 1)
    return _mask_dot(tril, x, NN)


def _cumsum_rows_bwd(_, g):
    c = g.shape[0]
    tril = lax.broadcasted_iota(jnp.int32, (c, c), 0) >= lax.broadcasted_iota(jnp.int32, (c, c), 1)
    return (_mask_dot(tril, g, TN),)


_cumsum_rows.defvjp(lambda x: (_cumsum_rows(x), None), _cumsum_rows_bwd)


def _unit_lower_inverses(xs):
    c = xs[0].shape[0]
    eye = (lax.broadcasted_iota(jnp.int32, (c, c), 0) == lax.broadcasted_iota(jnp.int32, (c, c), 1)).astype(F32)
    t_inv = [eye + x for x in xs]
    for _ in range(int(math.log2(c)) - 1):
        xs = [_raw_dot(x, x, NN, True) for x in xs]
        t_inv = [t + _raw_dot(t, x, NN, True) for t, x in zip(t_inv, xs)]
    return t_inv


@jax.custom_vjp
def _known_inverse(x_neg, t_inv):
    return t_inv


_known_inverse.defvjp(
    lambda x_neg, t_inv: (t_inv, t_inv),
    lambda t_inv, g: (_raw_dot(_raw_dot(t_inv, g, TN, True), t_inv, NT, True), jnp.zeros_like(t_inv)))


def _make_mm(hi):
    @jax.custom_vjp
    def nn(a, b):
        return _raw_dot(a, b, NN, hi)

    @jax.custom_vjp
    def nt(a, b):
        return _raw_dot(a, b, NT, hi)

    @jax.custom_vjp
    def tn(a, b):
        return _raw_dot(a, b, TN, hi)

    nn.defvjp(lambda a, b: (_raw_dot(a, b, NN, hi), (a, b)),
              lambda r, g: (_raw_dot(g, r[1], NT, hi), _raw_dot(r[0], g, TN, hi)))
    nt.defvjp(lambda a, b: (_raw_dot(a, b, NT, hi), (a, b)),
              lambda r, g: (_raw_dot(g, r[1], NN, hi), _raw_dot(g, r[0], TN, hi)))
    tn.defvjp(lambda a, b: (_raw_dot(a, b, TN, hi), (a, b)),
              lambda r, g: (_raw_dot(r[1], g, NT, hi), _raw_dot(r[0], g, NN, hi)))
    return nn, nt, tn


def _silu(x):
    return x * jax.nn.sigmoid(x)


def _rms_parts(x):
    r = lax.rsqrt(jnp.mean(x * x, axis=-1, keepdims=True) + EPS)
    return x * r, r


def _rms_bwd(dy, xh, r, gain):
    dxh = dy * gain
    dx = r * (dxh - xh * jnp.mean(dxh * xh, axis=-1, keepdims=True))
    return dx, jnp.sum(dy * xh, axis=0, keepdims=True)


def _ffn_specs(tm, d, tf, nj):
    return [pl.BlockSpec((tm, d), lambda i, j: (i, 0)), pl.BlockSpec((1, d), lambda i, j: (0, 0)),
            pl.BlockSpec((1, d, tf), lambda i, j: (j, 0, 0)), pl.BlockSpec((1, d, tf), lambda i, j: (nj + j, 0, 0)),
            pl.BlockSpec((tf, d), lambda i, j: (j, 0))]


def _first_step(ndim):
    return lambda: functools.reduce(lambda a, b: a & b, [pl.program_id(k) == 0 for k in range(ndim)])


def _last_step(grid):
    return lambda: functools.reduce(lambda a, b: a & b, [pl.program_id(k) == g - 1 for k, g in enumerate(grid)])


def _ffn_fwd(h, gain, w_in, wo, name, carry=None):
    tp, d = h.shape
    tf = w_in.shape[2]
    nj = w_in.shape[0] // 2
    tm = _tile(tp, 768, 8)
    row, vec, wg_spec, wu_spec, wo_spec = _ffn_specs(tm, d, tf, nj)

    def body(h_ref, g_ref, wg3_ref, wu3_ref, wo_ref, o_ref, n_sc, acc_sc):
        wg_ref, wu_ref = wg3_ref.at[0], wu3_ref.at[0]
        j = pl.program_id(1)

        @pl.when(j == 0)
        def _():
            xh, _ = _rms_parts(h_ref[...])
            n_sc[...] = (xh * g_ref[...]).astype(BF16)
            acc_sc[...] = jnp.zeros_like(acc_sc)

        n = n_sc[...]
        a_g = jnp.dot(n, wg_ref[...], preferred_element_type=F32)
        a_u = jnp.dot(n, wu_ref[...], preferred_element_type=F32)
        hid = (_silu(a_g) * a_u).astype(BF16)
        acc_sc[...] += jnp.dot(hid, wo_ref[...], preferred_element_type=F32)

        @pl.when(j == nj - 1)
        def _():
            o_ref[...] = h_ref[...] + 0.5 * acc_sc[...]

    grid = (tp // tm, nj)
    (out,), moved = _carried_call(
        body, carry, _first_step(2), _last_step(grid), name=name, grid=grid,
        in_specs=[row, vec, wg_spec, wu_spec, wo_spec], out_specs=[row],
        out_shape=[jax.ShapeDtypeStruct((tp, d), F32)],
        scratch_shapes=[pltpu.VMEM((tm, d), BF16), pltpu.VMEM((tm, d), F32)])(h, gain, w_in, w_in, wo)
    return out, moved


def _ffn_bwd(h, dho, gain, w_in, wo, name, carry=None):
    tp, d = h.shape
    tf = w_in.shape[2]
    nj = w_in.shape[0] // 2
    tm = _tile(tp, 384, 8)
    ni = tp // tm
    row, vec, wg_spec, wu_spec, wo_spec = _ffn_specs(tm, d, tf, nj)

    def body(h_ref, dho_ref, g_ref, wg3_ref, wu3_ref, wo_ref,
             dh_ref, dgain_ref, n_ref, hid3_ref, dag3_ref, dau3_ref, dn_sc, dhb_sc):
        wg_ref, wu_ref = wg3_ref.at[0], wu3_ref.at[0]
        hid_ref, dag_ref, dau_ref = hid3_ref.at[0], dag3_ref.at[0], dau3_ref.at[0]
        i, j = pl.program_id(0), pl.program_id(1)

        @pl.when(j == 0)
        def _():
            xh, _ = _rms_parts(h_ref[...])
            n_ref[...] = (xh * g_ref[...]).astype(BF16)
            dn_sc[...] = jnp.zeros_like(dn_sc)
            dhb_sc[...] = (0.5 * dho_ref[...]).astype(BF16)

        @pl.when((i == 0) & (j == 0))
        def _():
            dgain_ref[...] = jnp.zeros_like(dgain_ref)

        n = n_ref[...]
        a_g = jnp.dot(n, wg_ref[...], preferred_element_type=F32)
        a_u = jnp.dot(n, wu_ref[...], preferred_element_type=F32)
        sg = jax.nn.sigmoid(a_g)
        s = a_g * sg
        hid_ref[...] = (s * a_u).astype(BF16)
        d_hid = lax.dot_general(dhb_sc[...], wo_ref[...], NT, preferred_element_type=F32)
        d_au = (d_hid * s).astype(BF16)
        d_ag = (d_hid * a_u * (sg * (1.0 + a_g * (1.0 - sg)))).astype(BF16)
        dau_ref[...] = d_au
        dag_ref[...] = d_ag
        dn_sc[...] += (lax.dot_general(d_ag, wg_ref[...], NT, preferred_element_type=F32)
                       + lax.dot_general(d_au, wu_ref[...], NT, preferred_element_type=F32))

        @pl.when(j == nj - 1)
        def _():
            xh, r = _rms_parts(h_ref[...])
            dx, dg = _rms_bwd(dn_sc[...], xh, r, g_ref[...])
            dh_ref[...] = dho_ref[...] + dx
            dgain_ref[...] += dg

    act = pl.BlockSpec((1, tm, tf), lambda i, j: (j, i, 0))
    return _carried_call(
        body, carry, _first_step(2), _last_step((ni, nj)), name=name, grid=(ni, nj),
        in_specs=[row, row, vec, wg_spec, wu_spec, wo_spec],
        out_specs=[row, vec, row, act, act, act],
        out_shape=[jax.ShapeDtypeStruct((tp, d), F32), jax.ShapeDtypeStruct((1, d), F32),
                   jax.ShapeDtypeStruct((tp, d), BF16)] + [jax.ShapeDtypeStruct((nj, tp, tf), BF16)] * 3,
        scratch_shapes=[pltpu.VMEM((tm, d), F32), pltpu.VMEM((tm, d), BF16)])(h, dho, gain, w_in, w_in, wo)


def _matmul_tn(a, b, name, scale=1.0):
    t, m = a.shape
    n = b.shape[1]
    bm = _tile(m, 1024, LANES)
    bn = _tile(n, 1536, LANES)
    tk = _tile(t, 768, 16)
    nk = t // tk

    def body(a_ref, b_ref, o_ref):
        k = pl.program_id(2)

        @pl.when(k == 0)
        def _():
            o_ref[...] = jnp.zeros_like(o_ref)

        o_ref[...] += lax.dot_general(a_ref[...].astype(BF16), b_ref[...].astype(BF16), TN,
                                      preferred_element_type=F32)

        if scale != 1.0:
            @pl.when(k == nk - 1)
            def _():
                o_ref[...] = o_ref[...] * scale

    return pl.pallas_call(
        body, name=name, grid=(m // bm, n // bn, nk),
        in_specs=[pl.BlockSpec((tk, bm), lambda i, j, k: (k, i)), pl.BlockSpec((tk, bn), lambda i, j, k: (k, j))],
        out_specs=pl.BlockSpec((bm, bn), lambda i, j, k: (i, j)),
        out_shape=jax.ShapeDtypeStruct((m, n), F32),
        compiler_params=_params("parallel", "parallel", "arbitrary"))(a, b)


def _matmul_tn_blocks(a, b, name, scale=1.0):
    a_blocked = a.ndim == 3
    nb, t = (a.shape[0], a.shape[1]) if a_blocked else (b.shape[0], b.shape[1])
    m, n = a.shape[-1], b.shape[-1]
    tk = _tile(t, 768, 16)
    nk = t // tk
    if a_blocked:
        bo = _tile(n, 1024, LANES)
        a_spec = pl.BlockSpec((1, tk, m), lambda p, o, k: (p, k, 0))
        b_spec = pl.BlockSpec((tk, bo), lambda p, o, k: (k, o))
        o_spec = pl.BlockSpec((m, bo), lambda p, o, k: (p, o))
        out_shape = jax.ShapeDtypeStruct((nb * m, n), F32)
        grid = (nb, n // bo, nk)
    else:
        bo = _tile(m, 1024, LANES)
        a_spec = pl.BlockSpec((tk, bo), lambda p, o, k: (k, o))
        b_spec = pl.BlockSpec((1, tk, n), lambda p, o, k: (p, k, 0))
        o_spec = pl.BlockSpec((1, bo, n), lambda p, o, k: (p, o, 0))
        out_shape = jax.ShapeDtypeStruct((nb, m, n), F32)
        grid = (nb, m // bo, nk)

    def body(a_ref, b_ref, o_ref):
        k = pl.program_id(2)
        a_blk = a_ref[0] if a_blocked else a_ref[...]
        b_blk = b_ref[...] if a_blocked else b_ref[0]
        part = lax.dot_general(a_blk.astype(BF16), b_blk.astype(BF16), TN, preferred_element_type=F32)
        out = o_ref if a_blocked else o_ref.at[0]

        @pl.when(k == 0)
        def _():
            out[...] = part

        @pl.when(k > 0)
        def _():
            out[...] += part

        if scale != 1.0:
            @pl.when(k == nk - 1)
            def _():
                out[...] = out[...] * scale

    return pl.pallas_call(
        body, name=name, grid=grid, in_specs=[a_spec, b_spec], out_specs=o_spec, out_shape=out_shape,
        compiler_params=_params("parallel", "parallel", "arbitrary"))(a, b)


def _matmul_nt_parts(parts, w, acc, name, carry=None):
    t = parts[0].shape[0]
    d = w.shape[0]
    widths = [p.shape[1] for p in parts]
    tk = _tile(math.gcd(*widths), 1024, LANES)
    counts = [wd // tk for wd in widths]
    starts = [sum(counts[:g]) for g in range(len(parts))]
    nk = sum(counts)
    tm = _tile(t, 768, 8)
    n_parts = len(parts)

    def body(*refs):
        a_refs, w_ref, o_ref = refs[:n_parts], refs[n_parts], refs[-1]
        k = pl.program_id(1)

        @pl.when(k == 0)
        def _():
            o_ref[...] = jnp.zeros_like(o_ref) if acc is None else refs[n_parts + 1][...]

        for g in range(n_parts):
            @pl.when((k >= starts[g]) & (k < starts[g] + counts[g]))
            def _(g=g):
                o_ref[...] += lax.dot_general(a_refs[g][...].astype(BF16), w_ref[...], NT,
                                              preferred_element_type=F32)

    in_specs = [pl.BlockSpec((tm, tk), lambda i, k, lo=starts[g], nb=counts[g]: (i, jnp.clip(k - lo, 0, nb - 1)))
                for g in range(n_parts)]
    in_specs.append(pl.BlockSpec((d, tk), lambda i, k: (0, k)))
    args = list(parts) + [w]
    if acc is not None:
        in_specs.append(pl.BlockSpec((tm, d), lambda i, k: (i, 0)))
        args.append(acc)
    grid = (t // tm, nk)
    (out,), moved = _carried_call(
        body, carry, _first_step(2), _last_step(grid), name=name, grid=grid, in_specs=in_specs,
        out_specs=[pl.BlockSpec((tm, d), lambda i, k: (i, 0))],
        out_shape=[jax.ShapeDtypeStruct((t, d), F32)])(*args)
    return out, moved


def _proj_fwd(h, gain, wp, name, carry=None):
    tp, d = h.shape
    npad = wp.shape[1]
    tm = _tile(tp, 768, 8)
    tn = _tile(npad, 1152, LANES)

    def body(h_ref, g_ref, w_ref, o_ref, n_ref):
        @pl.when(pl.program_id(1) == 0)
        def _():
            xh, _ = _rms_parts(h_ref[...])
            n_ref[...] = (xh * g_ref[...]).astype(BF16)

        o_ref[...] = jnp.dot(n_ref[...], w_ref[...], preferred_element_type=F32)

    grid = (tp // tm, npad // tn)
    return _carried_call(
        body, carry, _first_step(2), _last_step(grid), name=name, grid=grid,
        in_specs=[pl.BlockSpec((tm, d), lambda i, j: (i, 0)), pl.BlockSpec((1, d), lambda i, j: (0, 0)),
                  pl.BlockSpec((d, tn), lambda i, j: (0, j))],
        out_specs=[pl.BlockSpec((tm, tn), lambda i, j: (i, j)), pl.BlockSpec((tm, d), lambda i, j: (i, 0))],
        out_shape=[jax.ShapeDtypeStruct((tp, npad), F32), jax.ShapeDtypeStruct((tp, d), BF16)])(h, gain, wp)


def _norm_bwd(h, gain, dn, dres, name):
    tp, d = h.shape
    tm = _tile(tp, 256, 8)

    def body(h_ref, g_ref, dn_ref, dres_ref, dh_ref, dgain_ref):
        @pl.when(pl.program_id(0) == 0)
        def _():
            dgain_ref[...] = jnp.zeros_like(dgain_ref)

        xh, r = _rms_parts(h_ref[...])
        dx, dg = _rms_bwd(dn_ref[...], xh, r, g_ref[...])
        dh_ref[...] = dres_ref[...] + dx
        dgain_ref[...] += dg

    row = pl.BlockSpec((tm, d), lambda i: (i, 0))
    vec = pl.BlockSpec((1, d), lambda i: (0, 0))
    return pl.pallas_call(
        body, name=name, grid=(tp // tm,), in_specs=[row, vec, row, row], out_specs=[row, vec],
        out_shape=[jax.ShapeDtypeStruct((tp, d), F32), jax.ShapeDtypeStruct((1, d), F32)],
        compiler_params=_params("arbitrary"))(h, gain, dn, dres)


def _head_post(a, grp):
    a = _silu(a)
    r = lax.rsqrt(jnp.sum(a * a, axis=-1, keepdims=True) + EPS)
    if isinstance(grp, int):
        return a if grp == 2 else a * r * (HEAD_DIM ** -0.5 if grp == 0 else 1.0)
    scale = jnp.where(grp == 0, HEAD_DIM ** -0.5, 1.0).astype(F32)
    return jnp.where(grp == 2, a, a * r * scale)


def _conv_taps(ext_sc, w_ref, tm):
    c = None
    for i in range(CONV_K):
        s = CONV_K - 1 - i
        term = w_ref[i:i + 1, :] * ext_sc[8 - s:8 - s + tm, :]
        c = term if c is None else c + term
    return c


def _conv_fwd(proj, conv_w, hv, name):
    tp = proj.shape[0]
    tm = _tile(tp, 256, 8)
    nh = hv // HEAD_DIM

    def body(x_ref, halo_ref, w_ref, o_ref, ext_sc):
        i, grp = pl.program_id(0), pl.program_id(1)
        ext_sc[0:8, :] = jnp.where(i == 0, 0.0, halo_ref[...])
        ext_sc[8:, :] = x_ref[...]
        c = _conv_taps(ext_sc, w_ref, tm)
        for h in range(nh):
            sl = slice(h * HEAD_DIM, (h + 1) * HEAD_DIM)
            o_ref[:, sl] = _head_post(c[:, sl], grp)

    return pl.pallas_call(
        body, name=name, grid=(tp // tm, 3),
        in_specs=[pl.BlockSpec((tm, hv), lambda i, g: (i, g)),
                  pl.BlockSpec((8, hv), lambda i, g: (jnp.maximum(i * (tm // 8) - 1, 0), g)),
                  pl.BlockSpec((CONV_K, hv), lambda i, g: (0, g))],
        out_specs=pl.BlockSpec((tm, hv), lambda i, g: (i, g)),
        out_shape=jax.ShapeDtypeStruct((tp, 3 * hv), F32),
        scratch_shapes=[pltpu.VMEM((tm + 8, hv), F32)],
        compiler_params=_params("parallel", "arbitrary"))(proj, proj, conv_w)


def _conv_bwd_pre(proj, conv_w, dy, grp, hv, name):
    tp = proj.shape[0]
    tm = _tile(tp, 256, 8)
    nh = hv // HEAD_DIM

    def body(x_ref, halo_ref, w_ref, dy_ref, dc_ref, dw_ref, ext_sc):
        i = pl.program_id(0)
        ext_sc[0:8, :] = jnp.where(i == 0, 0.0, halo_ref[...])
        ext_sc[8:, :] = x_ref[...]
        c = _conv_taps(ext_sc, w_ref, tm)
        for h in range(nh):
            sl = slice(h * HEAD_DIM, (h + 1) * HEAD_DIM)
            _, vjp = jax.vjp(lambda a: _head_post(a, grp), c[:, sl])
            dc_ref[:, sl] = vjp(dy_ref[:, sl])[0]

        @pl.when(i == 0)
        def _():
            dw_ref[...] = jnp.zeros_like(dw_ref)

        dc = dc_ref[...]
        for k in range(CONV_K):
            s = CONV_K - 1 - k
            dw_ref[k:k + 1, :] += jnp.sum(dc * ext_sc[8 - s:8 - s + tm, :], axis=0, keepdims=True)

    return pl.pallas_call(
        body, name=name, grid=(tp // tm,),
        in_specs=[pl.BlockSpec((tm, hv), lambda i: (i, grp)),
                  pl.BlockSpec((8, hv), lambda i: (jnp.maximum(i * (tm // 8) - 1, 0), grp)),
                  pl.BlockSpec((CONV_K, hv), lambda i: (0, grp)),
                  pl.BlockSpec((tm, hv), lambda i: (i, 0))],
        out_specs=[pl.BlockSpec((tm, hv), lambda i: (i, 0)), pl.BlockSpec((CONV_K, hv), lambda i: (0, 0))],
        out_shape=[jax.ShapeDtypeStruct((tp, hv), F32), jax.ShapeDtypeStruct((CONV_K, hv), F32)],
        scratch_shapes=[pltpu.VMEM((tm + 8, hv), F32)],
        compiler_params=_params("arbitrary"))(proj, proj, conv_w, dy)


def _conv_bwd_in(dc, conv_w, grp, name):
    tp, hv = dc.shape
    tm = _tile(tp, 256, 8)
    ni = tp // tm

    def body(dc_ref, halo_ref, w_ref, dx_ref, ext_sc):
        i = pl.program_id(0)
        ext_sc[0:tm, :] = dc_ref[...]
        ext_sc[tm:, :] = jnp.where(i == ni - 1, 0.0, halo_ref[...])
        dx = None
        for k in range(CONV_K):
            s = CONV_K - 1 - k
            term = w_ref[k:k + 1, :] * ext_sc[s:s + tm, :]
            dx = term if dx is None else dx + term
        dx_ref[...] = dx.astype(BF16)

    return pl.pallas_call(
        body, name=name, grid=(ni,),
        in_specs=[pl.BlockSpec((tm, hv), lambda i: (i, 0)),
                  pl.BlockSpec((8, hv), lambda i: (jnp.minimum((i + 1) * (tm // 8), tp // 8 - 1), 0)),
                  pl.BlockSpec((CONV_K, hv), lambda i: (0, grp))],
        out_specs=pl.BlockSpec((tm, hv), lambda i: (i, 0)),
        out_shape=jax.ShapeDtypeStruct((tp, hv), BF16),
        scratch_shapes=[pltpu.VMEM((tm + 8, hv), F32)],
        compiler_params=_params("parallel"))(dc, dc, conv_w)


def _gdn_gates(ba, alog, dtb):
    x = ba + dtb
    softplus = jnp.maximum(x, 0.0) + jnp.log1p(jnp.exp(-jnp.abs(x)))
    return _cumsum_rows(-jnp.exp(alog) * softplus), jax.nn.sigmoid(ba)


def _gdn_heads(states, qs, ks, vs, gc_all, beta_all, known_inverses=None):
    mm_nn, mm_nt, mm_tn = _make_mm(False)
    hi_nn, _, _ = _make_mm(True)
    nh = len(qs)
    heads = range(nh)
    c = qs[0].shape[0]
    lane = lax.broadcasted_iota(jnp.int32, (c, LANES), 1)
    last_row = (lax.broadcasted_iota(jnp.int32, (c, 1), 0) == c - 1).astype(F32)
    ri = lax.broadcasted_iota(jnp.int32, (c, c), 0)
    ci = lax.broadcasted_iota(jnp.int32, (c, c), 1)
    causal = ri >= ci
    strict = ri > ci
    eye = (ri == ci).astype(F32)
    sel_a = [(lane == nh + h).astype(F32) for h in heads]
    sel_b = [(lane == h).astype(F32) for h in heads]

    gcol = [jnp.sum(gc_all * sel_a[h], axis=1, keepdims=True) for h in heads]
    grow = [jnp.sum(eye * gcol[h], axis=0, keepdims=True) for h in heads]
    beta = [jnp.sum(beta_all * sel_b[h], axis=1, keepdims=True) for h in heads]
    decay = [jnp.where(causal, jnp.exp(jnp.where(causal, gcol[h] - grow[h], 0.0)), 0.0) for h in heads]
    kb = [ks[h] * beta[h] for h in heads]
    kk = [mm_nt(kb[h], ks[h]) for h in heads]
    qk = [mm_nt(qs[h], ks[h]) for h in heads]
    x_neg = [-jnp.where(strict, kk[h] * decay[h], 0.0) for h in heads]
    if known_inverses is None:
        t_inv = _unit_lower_inverses(x_neg)
    else:
        t_inv = [_known_inverse(x_neg[h], known_inverses[h]) for h in heads]
    eg = [jnp.exp(gcol[h]) for h in heads]
    u = [hi_nn(t_inv[h], vs[h] * beta[h]) for h in heads]
    w = [hi_nn(t_inv[h], kb[h] * eg[h]) for h in heads]
    qk = [qk[h] * decay[h] for h in heads]
    glast = [jnp.sum(gcol[h] * last_row, axis=0, keepdims=True) for h in heads]
    ws = [mm_nn(w[h], states[h]) for h in heads]
    qs_state = [mm_nn(qs[h] * eg[h], states[h]) for h in heads]
    v_new = [u[h] - ws[h] for h in heads]
    intra = [mm_nn(qk[h], v_new[h]) for h in heads]
    kv = [mm_tn(ks[h] * jnp.exp(glast[h] - gcol[h]), v_new[h]) for h in heads]
    outs = [qs_state[h] + intra[h] for h in heads]
    new_states = [states[h] * jnp.exp(glast[h]) + kv[h] for h in heads]
    return outs, new_states, t_inv


def _scan_specs(nh, nc, rev, first_col):
    cidx = (lambda c: nc - 1 - c) if rev else (lambda c: c)
    hv = nh * HEAD_DIM
    cols = [pl.BlockSpec((CHUNK, hv), lambda c, g=g: (cidx(c), first_col + g)) for g in range(3)]
    st = pl.BlockSpec((1, nh, HEAD_DIM, HEAD_DIM), lambda c: (cidx(c), 0, 0, 0))
    act = pl.BlockSpec((CHUNK, hv), lambda c: (cidx(c), 0))
    return cols, st, act


def _gdn_fwd(qkv, proj, alog, dtb, nh, name):
    tp = qkv.shape[0]
    nc = tp // CHUNK

    def body(q_ref, k_ref, v_ref, ba_ref, al_ref, dt_ref, o_ref, st_ref, inv_ref, s_sc):
        @pl.when(pl.program_id(0) == 0)
        def _():
            s_sc[...] = jnp.zeros_like(s_sc)

        gc_all, beta_all = _gdn_gates(ba_ref[...], al_ref[...], dt_ref[...])
        sls = [slice(h * HEAD_DIM, (h + 1) * HEAD_DIM) for h in range(nh)]
        states = [s_sc[h] for h in range(nh)]
        for h in range(nh):
            st_ref[0, h] = states[h]
        outs, new_states, t_inv = _gdn_heads(states, [q_ref[:, sl] for sl in sls], [k_ref[:, sl] for sl in sls],
                                             [v_ref[:, sl] for sl in sls], gc_all, beta_all)
        for h in range(nh):
            o_ref[:, sls[h]] = outs[h]
            s_sc[h] = new_states[h]
            inv_ref[0, h] = t_inv[h]

    cols, st, act = _scan_specs(nh, nc, False, 0)
    ba = pl.BlockSpec((CHUNK, LANES), lambda c: (c, 10 * nh))
    vec = pl.BlockSpec((1, LANES), lambda c: (0, 0))
    inv = pl.BlockSpec((1, nh, CHUNK, CHUNK), lambda c: (c, 0, 0, 0))
    return pl.pallas_call(
        body, name=name, grid=(nc,), in_specs=cols + [ba, vec, vec], out_specs=[act, st, inv],
        out_shape=[jax.ShapeDtypeStruct((tp, nh * HEAD_DIM), F32),
                   jax.ShapeDtypeStruct((nc, nh, HEAD_DIM, HEAD_DIM), F32),
                   jax.ShapeDtypeStruct((nc, nh, CHUNK, CHUNK), F32)],
        scratch_shapes=[pltpu.VMEM((nh, HEAD_DIM, HEAD_DIM), F32)],
        compiler_params=_params("arbitrary"))(qkv, qkv, qkv, proj, alog, dtb)


def _gdn_bwd(qkv, proj, alog, dtb, states, inverses, do, nh, name):
    tp = qkv.shape[0]
    nc = tp // CHUNK

    def body(q_ref, k_ref, v_ref, ba_ref, al_ref, dt_ref, st_ref, inv_ref, do_ref,
             dq_ref, dk_ref, dv_ref, dba_ref, dal_ref, ddt_ref, ds_sc):
        @pl.when(pl.program_id(0) == 0)
        def _():
            ds_sc[...] = jnp.zeros_like(ds_sc)
            dal_ref[...] = jnp.zeros_like(dal_ref)
            ddt_ref[...] = jnp.zeros_like(ddt_ref)

        (gc_all, beta_all), gates_vjp = jax.vjp(_gdn_gates, ba_ref[...], al_ref[...], dt_ref[...])
        sls = [slice(h * HEAD_DIM, (h + 1) * HEAD_DIM) for h in range(nh)]
        known = [inv_ref[0, h] for h in range(nh)]
        fn = lambda s, q, k, v, gc, beta: _gdn_heads(s, q, k, v, gc, beta, known)[:2]
        _, vjp = jax.vjp(fn, [st_ref[0, h] for h in range(nh)], [q_ref[:, sl] for sl in sls],
                         [k_ref[:, sl] for sl in sls], [v_ref[:, sl] for sl in sls], gc_all, beta_all)
        ds, dq, dk, dv, dgc, dbeta = vjp(([do_ref[:, sl] for sl in sls], [ds_sc[h] for h in range(nh)]))
        for h in range(nh):
            ds_sc[h] = ds[h]
            dq_ref[:, sls[h]] = dq[h]
            dk_ref[:, sls[h]] = dk[h]
            dv_ref[:, sls[h]] = dv[h]
        dba, dal, ddt = gates_vjp((dgc, dbeta))
        dba_ref[...] = dba
        dal_ref[...] += dal
        ddt_ref[...] += ddt

    cols, st, act = _scan_specs(nh, nc, True, 0)
    ba = pl.BlockSpec((CHUNK, LANES), lambda c: (nc - 1 - c, 10 * nh))
    vec = pl.BlockSpec((1, LANES), lambda c: (0, 0))
    inv = pl.BlockSpec((1, nh, CHUNK, CHUNK), lambda c: (nc - 1 - c, 0, 0, 0))
    return pl.pallas_call(
        body, name=name, grid=(nc,), in_specs=cols + [ba, vec, vec, st, inv, act],
        out_specs=[act, act, act, pl.BlockSpec((CHUNK, LANES), lambda c: (nc - 1 - c, 0)), vec, vec],
        out_shape=[jax.ShapeDtypeStruct((tp, nh * HEAD_DIM), F32)] * 3
                  + [jax.ShapeDtypeStruct((tp, LANES), F32), jax.ShapeDtypeStruct((1, LANES), F32),
                     jax.ShapeDtypeStruct((1, LANES), F32)],
        scratch_shapes=[pltpu.VMEM((nh, HEAD_DIM, HEAD_DIM), F32)],
        compiler_params=_params("arbitrary"))(qkv, qkv, qkv, proj, alog, dtb, states, inverses, do)


def _swap_pairs(t):
    lane = lax.broadcasted_iota(jnp.int32, t.shape, 1)
    n = t.shape[1]
    return jnp.where(lane % 2 == 0, pltpu.roll(t, n - 1, 1), pltpu.roll(t, 1, 1))


def _rot(t, cos, sin_signed):
    return t * cos + _swap_pairs(t) * sin_signed


def _rot_t(dt, cos, sin_signed):
    return dt * cos + _swap_pairs(dt * sin_signed)


def _ret_heads(states, qs, ks, vs, dec, xi, zeta, cd):
    mm_nn, mm_nt, mm_tn = _make_mm(False)
    heads = range(len(qs))
    scores = [mm_nt(qs[h], ks[h]) for h in heads]
    inter = [mm_nn(qs[h] * xi[h], states[h]) for h in heads]
    kv = [mm_tn(ks[h] * zeta[h], vs[h]) for h in heads]
    intra = [mm_nn(scores[h] * dec[h], vs[h]) for h in heads]
    return [intra[h] + inter[h] for h in heads], [states[h] * cd[h] + kv[h] for h in heads]


def _ret_table_specs(nh, nc, rev):
    cidx = (lambda c: nc - 1 - c) if rev else (lambda c: c)
    rope = pl.BlockSpec((CHUNK, HEAD_DIM), lambda c: (cidx(c), 0))
    dec = pl.BlockSpec((nh, CHUNK, CHUNK), lambda c: (0, 0, 0))
    tab = pl.BlockSpec((nh, CHUNK, HEAD_DIM), lambda c: (0, 0, 0))
    cd = pl.BlockSpec((nh, 8, HEAD_DIM), lambda c: (0, 0, 0))
    return [rope, rope, dec, tab, tab, cd]


def _ret_fwd(proj, cos, sin, dec, xi, zeta, cd, nh, name):
    tp = proj.shape[0]
    nc = tp // CHUNK
    kscale = HEAD_DIM ** -0.5

    def body(q_ref, k_ref, v_ref, cos_ref, sin_ref, dec_ref, xi_ref, zeta_ref, cd_ref, o_ref, st_ref, s_sc):
        @pl.when(pl.program_id(0) == 0)
        def _():
            s_sc[...] = jnp.zeros_like(s_sc)

        cos_t, sin_t = cos_ref[...], sin_ref[...]
        heads = range(nh)
        sls = [slice(h * HEAD_DIM, (h + 1) * HEAD_DIM) for h in heads]
        states = [s_sc[h] for h in heads]
        for h in heads:
            st_ref[0, h] = states[h]
        qs = [_rot(q_ref[:, sl], cos_t, sin_t) for sl in sls]
        ks = [_rot(k_ref[:, sl], cos_t, sin_t) * kscale for sl in sls]
        outs, new_states = _ret_heads(states, qs, ks, [v_ref[:, sl] for sl in sls], [dec_ref[h] for h in heads],
                                      [xi_ref[h] for h in heads], [zeta_ref[h] for h in heads],
                                      [cd_ref[h][0:1, :] for h in heads])
        for h in heads:
            o_ref[:, sls[h]] = outs[h]
            s_sc[h] = new_states[h]

    cols, st, act = _scan_specs(nh, nc, False, 3)
    return pl.pallas_call(
        body, name=name, grid=(nc,), in_specs=cols + _ret_table_specs(nh, nc, False), out_specs=[act, st],
        out_shape=[jax.ShapeDtypeStruct((tp, nh * HEAD_DIM), F32),
                   jax.ShapeDtypeStruct((nc, nh, HEAD_DIM, HEAD_DIM), F32)],
        scratch_shapes=[pltpu.VMEM((nh, HEAD_DIM, HEAD_DIM), F32)],
        compiler_params=_params("arbitrary"))(proj, proj, proj, cos, sin, dec, xi, zeta, cd)


def _ret_bwd(proj, cos, sin, dec, xi, zeta, cd, states, do, nh, name):
    tp = proj.shape[0]
    nc = tp // CHUNK
    kscale = HEAD_DIM ** -0.5

    def body(q_ref, k_ref, v_ref, cos_ref, sin_ref, dec_ref, xi_ref, zeta_ref, cd_ref, st_ref, do_ref,
             dq_ref, dk_ref, dv_ref, ds_sc):
        @pl.when(pl.program_id(0) == 0)
        def _():
            ds_sc[...] = jnp.zeros_like(ds_sc)

        cos_t, sin_t = cos_ref[...], sin_ref[...]
        heads = range(nh)
        sls = [slice(h * HEAD_DIM, (h + 1) * HEAD_DIM) for h in heads]
        qs = [_rot(q_ref[:, sl], cos_t, sin_t) for sl in sls]
        ks = [_rot(k_ref[:, sl], cos_t, sin_t) * kscale for sl in sls]
        fn = functools.partial(_ret_heads, dec=[dec_ref[h] for h in heads], xi=[xi_ref[h] for h in heads],
                               zeta=[zeta_ref[h] for h in heads], cd=[cd_ref[h][0:1, :] for h in heads])
        _, vjp = jax.vjp(fn, [st_ref[0, h] for h in heads], qs, ks, [v_ref[:, sl] for sl in sls])
        ds, dq, dk, dv = vjp(([do_ref[:, sl] for sl in sls], [ds_sc[h] for h in heads]))
        for h in heads:
            ds_sc[h] = ds[h]
            dq_ref[:, sls[h]] = _rot_t(dq[h], cos_t, sin_t).astype(BF16)
            dk_ref[:, sls[h]] = _rot_t(dk[h] * kscale, cos_t, sin_t).astype(BF16)
            dv_ref[:, sls[h]] = dv[h].astype(BF16)

    cols, st, act = _scan_specs(nh, nc, True, 3)
    return pl.pallas_call(
        body, name=name, grid=(nc,), in_specs=cols + _ret_table_specs(nh, nc, True) + [st, act],
        out_specs=[act, act, act],
        out_shape=[jax.ShapeDtypeStruct((tp, nh * HEAD_DIM), BF16)] * 3,
        scratch_shapes=[pltpu.VMEM((nh, HEAD_DIM, HEAD_DIM), F32)],
        compiler_params=_params("arbitrary"))(proj, proj, proj, cos, sin, dec, xi, zeta, cd, states, do)


def _gdn_out(o, z, gnorm):
    return o * lax.rsqrt(jnp.mean(o * o, axis=-1, keepdims=True) + EPS) * gnorm * _silu(z)


def _ret_out(o, rg, rnorm):
    mu = jnp.mean(o, axis=-1, keepdims=True)
    var = jnp.mean(jnp.square(o - mu), axis=-1, keepdims=True)
    return _silu(rg) * ((o - mu) * lax.rsqrt(var + EPS) * rnorm)


def _post_specs(tm, hv, d):
    row = lambda col: pl.BlockSpec((tm, hv), lambda i: (i, col))
    return dict(
        oa=row(0), ob=row(0), z=row(6), rg=row(7), ga=row(8), gb=row(9),
        gnorm=pl.BlockSpec((1, HEAD_DIM), lambda i: (0, 0)), rnorm=pl.BlockSpec((1, hv), lambda i: (0, 0)),
        w=pl.BlockSpec((hv, d), lambda i: (0, 0)), res=pl.BlockSpec((tm, d), lambda i: (i, 0)))


def _post_fwd(oa, ob, proj, gnorm, rnorm, wbg, wbr, wo, h1, name):
    tp, d = h1.shape
    hv = oa.shape[1]
    nh = hv // HEAD_DIM
    tm = _tile(tp, 256, 8)

    def body(oa_ref, ob_ref, z_ref, rg_ref, ga_ref, gb_ref, gn_ref, rn_ref, wbg_ref, wbr_ref, wo_ref, h_ref,
             o_ref, ya_sc, yb_sc):
        for h in range(nh):
            sl = slice(h * HEAD_DIM, (h + 1) * HEAD_DIM)
            ya_sc[:, sl] = _gdn_out(oa_ref[:, sl], z_ref[:, sl], gn_ref[...]).astype(BF16)
            yb_sc[:, sl] = _ret_out(ob_ref[:, sl], rg_ref[:, sl], rn_ref[:, sl]).astype(BF16)
        pa = jnp.dot(ya_sc[...], wbg_ref[...], preferred_element_type=F32)
        pb = jnp.dot(yb_sc[...], wbr_ref[...], preferred_element_type=F32)
        merged = jax.nn.sigmoid(ga_ref[...]) * pa + jax.nn.sigmoid(gb_ref[...]) * pb
        o_ref[...] = h_ref[...] + jnp.dot(merged.astype(BF16), wo_ref[...], preferred_element_type=F32)

    sp = _post_specs(tm, hv, d)
    return pl.pallas_call(
        body, name=name, grid=(tp // tm,),
        in_specs=[sp["oa"], sp["ob"], sp["z"], sp["rg"], sp["ga"], sp["gb"], sp["gnorm"], sp["rnorm"],
                  sp["w"], sp["w"], sp["w"], sp["res"]],
        out_specs=sp["res"], out_shape=jax.ShapeDtypeStruct((tp, d), F32),
        scratch_shapes=[pltpu.VMEM((tm, hv), BF16), pltpu.VMEM((tm, hv), BF16)],
        compiler_params=_params("parallel"))(oa, ob, proj, proj, proj, proj, gnorm, rnorm, wbg, wbr, wo, h1)


def _post_bwd(oa, ob, proj, gnorm, rnorm, wbg, wbr, wo, dh2, name):
    tp, d = dh2.shape
    hv = oa.shape[1]
    nh = hv // HEAD_DIM
    tm = _tile(tp, 256, 8)

    def body(oa_ref, ob_ref, z_ref, rg_ref, ga_ref, gb_ref, gn_ref, rn_ref, wbg_ref, wbr_ref, wo_ref, dh_ref,
             doa_ref, dob_ref, dg_ref, ya_ref, yb_ref, mg_ref, dpa_ref, dpb_ref, dgn_ref, drn_ref,
             dya_sc, dyb_sc):
        @pl.when(pl.program_id(0) == 0)
        def _():
            dgn_ref[...] = jnp.zeros_like(dgn_ref)
            drn_ref[...] = jnp.zeros_like(drn_ref)

        for h in range(nh):
            sl = slice(h * HEAD_DIM, (h + 1) * HEAD_DIM)
            ya_ref[:, sl] = _gdn_out(oa_ref[:, sl], z_ref[:, sl], gn_ref[...]).astype(BF16)
            yb_ref[:, sl] = _ret_out(ob_ref[:, sl], rg_ref[:, sl], rn_ref[:, sl]).astype(BF16)
        pa = jnp.dot(ya_ref[...], wbg_ref[...], preferred_element_type=F32)
        pb = jnp.dot(yb_ref[...], wbr_ref[...], preferred_element_type=F32)
        sa = jax.nn.sigmoid(ga_ref[...])
        sb = jax.nn.sigmoid(gb_ref[...])
        mg_ref[...] = (sa * pa + sb * pb).astype(BF16)
        dm = lax.dot_general(dh_ref[...].astype(BF16), wo_ref[...], NT, preferred_element_type=F32)
        dpa = (dm * sa).astype(BF16)
        dpb = (dm * sb).astype(BF16)
        dpa_ref[...] = dpa
        dpb_ref[...] = dpb
        dg_ref[:, 2 * hv:3 * hv] = (dm * pa * sa * (1.0 - sa)).astype(BF16)
        dg_ref[:, 3 * hv:4 * hv] = (dm * pb * sb * (1.0 - sb)).astype(BF16)
        dya_sc[...] = lax.dot_general(dpa, wbg_ref[...], NT, preferred_element_type=F32)
        dyb_sc[...] = lax.dot_general(dpb, wbr_ref[...], NT, preferred_element_type=F32)
        for h in range(nh):
            sl = slice(h * HEAD_DIM, (h + 1) * HEAD_DIM)
            _, vjp_a = jax.vjp(_gdn_out, oa_ref[:, sl], z_ref[:, sl], gn_ref[...])
            doa, dz, dgn = vjp_a(dya_sc[:, sl])
            doa_ref[:, sl] = doa
            dg_ref[:, sl] = dz.astype(BF16)
            dgn_ref[...] += dgn
            _, vjp_b = jax.vjp(_ret_out, ob_ref[:, sl], rg_ref[:, sl], rn_ref[:, sl])
            dob, drg, drn = vjp_b(dyb_sc[:, sl])
            dob_ref[:, sl] = dob
            dg_ref[:, hv + h * HEAD_DIM:hv + (h + 1) * HEAD_DIM] = drg.astype(BF16)
            drn_ref[:, sl] += drn

    sp = _post_specs(tm, hv, d)
    act = pl.BlockSpec((tm, hv), lambda i: (i, 0))
    return pl.pallas_call(
        body, name=name, grid=(tp // tm,),
        in_specs=[sp["oa"], sp["ob"], sp["z"], sp["rg"], sp["ga"], sp["gb"], sp["gnorm"], sp["rnorm"],
                  sp["w"], sp["w"], sp["w"], sp["res"]],
        out_specs=[act, act, pl.BlockSpec((tm, 4 * hv), lambda i: (i, 0)), act, act, sp["res"], sp["res"],
                   sp["res"], sp["gnorm"], sp["rnorm"]],
        out_shape=[jax.ShapeDtypeStruct((tp, hv), F32), jax.ShapeDtypeStruct((tp, hv), F32),
                   jax.ShapeDtypeStruct((tp, 4 * hv), BF16), jax.ShapeDtypeStruct((tp, hv), BF16),
                   jax.ShapeDtypeStruct((tp, hv), BF16), jax.ShapeDtypeStruct((tp, d), BF16),
                   jax.ShapeDtypeStruct((tp, d), BF16), jax.ShapeDtypeStruct((tp, d), BF16),
                   jax.ShapeDtypeStruct((1, HEAD_DIM), F32), jax.ShapeDtypeStruct((1, hv), F32)],
        scratch_shapes=[pltpu.VMEM((tm, hv), F32), pltpu.VMEM((tm, hv), F32)],
        compiler_params=_params("arbitrary"))(oa, ob, proj, proj, proj, proj, gnorm, rnorm, wbg, wbr, wo, dh2)


def _final(h3, gain, target, name):
    tp, d = h3.shape
    tm = HEAD_ROWS

    def body(h_ref, g_ref, t_ref, loss_ref, dh_ref, dgain_ref):
        i = pl.program_id(0)

        @pl.when(i == 0)
        def _():
            loss_ref[...] = jnp.zeros_like(loss_ref)
            dgain_ref[...] = jnp.zeros_like(dgain_ref)

        xh, r = _rms_parts(h_ref[...])
        err = jnp.where(i == 0, 0.0, xh * g_ref[...] - t_ref[...])
        dx, dg = _rms_bwd(err * (1.0 / d), xh, r, g_ref[...])
        dh_ref[...] = dx
        dgain_ref[...] += dg
        loss_ref[...] += 0.5 * jnp.sum(jnp.mean(err * err, axis=-1, keepdims=True), axis=0, keepdims=True)

    row = pl.BlockSpec((tm, d), lambda i: (i, 0))
    vec = pl.BlockSpec((1, d), lambda i: (0, 0))
    return pl.pallas_call(
        body, name=name, grid=(tp // tm,),
        in_specs=[row, vec, pl.BlockSpec((tm, d), lambda i: (jnp.maximum(i - 1, 0), 0))],
        out_specs=[pl.BlockSpec((1, LANES), lambda i: (0, 0)), row, vec],
        out_shape=[jax.ShapeDtypeStruct((1, LANES), F32), jax.ShapeDtypeStruct((tp, d), F32),
                   jax.ShapeDtypeStruct((1, d), F32)],
        compiler_params=_params("arbitrary"))(h3, gain, target)


def _peer(k):
    x, y, c = lax.axis_index("x"), lax.axis_index("y"), lax.axis_index("c")
    return (1 - x if k & 4 else x, 1 - y if k & 2 else y, 1 - c if k & 1 else c)


def _my_index():
    return 4 * lax.axis_index("x") + 2 * lax.axis_index("y") + lax.axis_index("c")


def _exchange(bufs, scatter, name):
    n = len(bufs)

    def body(*refs):
        _exchange_copies(refs[:n], refs[n:2 * n], refs[2 * n:], scatter, True, True)

    hbm, out_shape, sems = _exchange_refs(bufs)
    return pl.pallas_call(
        body, name=name, in_specs=hbm, out_specs=hbm, out_shape=out_shape, scratch_shapes=sems,
        compiler_params=pltpu.CompilerParams(has_side_effects=True))(*bufs)


def _exchange_refs(bufs):
    n = len(bufs)
    return ([pl.BlockSpec(memory_space=pl.ANY)] * n,
            [jax.ShapeDtypeStruct((N_DEV,) + b.shape[-2:], b.dtype) for b in bufs],
            [pltpu.SemaphoreType.DMA(((N_DEV - 1) * n,)), pltpu.SemaphoreType.DMA(((N_DEV - 1) * n,)),
             pltpu.SemaphoreType.DMA((n,))])


def _exchange_copies(x_refs, out_refs, sems, scatter, start, wait):
    n = len(x_refs)
    send_sems, recv_sems, local_sems = sems
    me = _my_index()
    copies = []
    for a in range(n):
        copies.append(pltpu.make_async_copy(x_refs[a].at[me] if scatter else x_refs[a], out_refs[a].at[me],
                                            local_sems.at[a]))
    sends = []
    arrivals = []
    for k in range(1, N_DEV):
        x, y, c = _peer(k)
        peer = 4 * x + 2 * y + c
        for a in range(n):
            sem = (k - 1) * n + a
            sends.append(pltpu.make_async_remote_copy(
                src_ref=x_refs[a].at[peer] if scatter else x_refs[a], dst_ref=out_refs[a].at[me],
                send_sem=send_sems.at[sem], recv_sem=recv_sems.at[sem],
                device_id=(x, y, c), device_id_type=pl.DeviceIdType.MESH))
            landed = out_refs[a].at[peer]
            arrivals.append(pltpu.make_async_remote_copy(
                src_ref=landed, dst_ref=landed, send_sem=send_sems.at[sem], recv_sem=recv_sems.at[sem],
                device_id=(x, y, c), device_id_type=pl.DeviceIdType.MESH))
    if start:
        for cp in copies + sends:
            cp.start()
    if wait:
        for cp in arrivals:
            cp.wait_recv()
        for cp in sends:
            cp.wait_send()
        for cp in copies:
            cp.wait()


def _carried_call(body, carry, first, last, *, name, grid, in_specs, out_specs, out_shape, scratch_shapes=()):
    in_specs, out_specs, out_shape = list(in_specs), list(out_specs), list(out_shape)
    semantics = ("arbitrary",) * len(grid)
    if carry is None:
        call = pl.pallas_call(body, name=name, grid=grid, in_specs=in_specs, out_specs=out_specs,
                              out_shape=out_shape, scratch_shapes=list(scratch_shapes),
                              compiler_params=_params(*semantics))
        return lambda *args: (call(*args), [])
    bufs, scatter = carry
    n, n_in, n_out, n_scratch = len(bufs), len(in_specs), len(out_specs), len(scratch_shapes)
    hbm, x_shapes, sems = _exchange_refs(bufs)

    def full_body(*refs):
        ins, x_refs = refs[:n_in], refs[n_in:n_in + n]
        outs, xo_refs = refs[n_in + n:n_in + n + n_out], refs[n_in + n + n_out:n_in + 2 * n + n_out]
        scratch = refs[n_in + 2 * n + n_out:n_in + 2 * n + n_out + n_scratch]
        x_sems = refs[n_in + 2 * n + n_out + n_scratch:]

        @pl.when(first())
        def _():
            _exchange_copies(x_refs, xo_refs, x_sems, scatter, True, False)

        body(*ins, *outs, *scratch)

        @pl.when(last())
        def _():
            _exchange_copies(x_refs, xo_refs, x_sems, scatter, False, True)

    call = pl.pallas_call(full_body, name=name, grid=grid, in_specs=in_specs + hbm, out_specs=out_specs + hbm,
                          out_shape=out_shape + x_shapes, scratch_shapes=list(scratch_shapes) + sems,
                          compiler_params=_params(*semantics))

    def run(*args):
        res = call(*args, *bufs)
        return res[:n_out], res[n_out:]
    return run


def _adamw(w, g, m, v, name):
    r, c = w.shape
    parts = g.ndim == 3
    tr = _tile(r, 256, 16 if parts else 8)
    c1 = 1.0 - ADAM_B1 ** ADAM_STEP
    c2 = 1.0 - ADAM_B2 ** ADAM_STEP

    def body(w_ref, g_ref, m_ref, v_ref, go_ref, d_ref, mo_ref, vo_ref):
        if parts:
            g = g_ref[0].astype(F32)
            for q in range(1, N_DEV):
                g = g + g_ref[q].astype(F32)
        else:
            g = g_ref[...]
        m = ADAM_B1 * m_ref[...] + (1.0 - ADAM_B1) * g
        v = ADAM_B2 * v_ref[...] + (1.0 - ADAM_B2) * (g * g)
        go_ref[...] = g
        d_ref[...] = -ADAM_LR * ((m / c1) / (jnp.sqrt(v / c2) + ADAM_EPS) + ADAM_WD * w_ref[...])
        mo_ref[...] = m
        vo_ref[...] = v

    blk = pl.BlockSpec((tr, c), lambda i: (i, 0))
    g_spec = pl.BlockSpec((N_DEV, tr, c), lambda i: (0, i, 0)) if parts else blk
    return pl.pallas_call(
        body, name=name, grid=(r // tr,), in_specs=[blk, g_spec, blk, blk], out_specs=[blk] * 4,
        out_shape=[jax.ShapeDtypeStruct((r, c), F32)] * 4,
        compiler_params=_params("parallel"))(w, g, m, v)


def _win_segments(hv, nh):
    o_z, o_b = 3 * hv, 4 * hv
    o_r = o_b + 2 * nh
    return [(0, 0, 3 * hv), (3 * hv, o_r, 3 * hv), (6 * hv, o_z, hv), (7 * hv, o_r + 3 * hv, 3 * hv),
            (10 * hv, o_b, 2 * nh)]


def _win_from_shards(shards, hv, nh):
    _, d, cs = shards.shape
    pieces = []
    for _, src, width in _win_segments(hv, nh):
        lo = src
        while lo < src + width:
            p = lo // cs
            hi = min(src + width, (p + 1) * cs)
            pieces.append(shards[p][:, lo - p * cs:hi - p * cs])
            lo = hi
    pieces.append(jnp.zeros((d, LANES - 2 * nh), shards.dtype))
    return jnp.concatenate(pieces, axis=1)


def _win_grad_to_shards(g, hv, nh, cs):
    segments = _win_segments(hv, nh)
    shards = []
    for p in range(N_DEV):
        pieces = []
        lo = p * cs
        while lo < (p + 1) * cs:
            here, src, width = next(s for s in segments if s[1] <= lo < s[1] + s[2])
            hi = min((p + 1) * cs, src + width)
            pieces.append(g[:, here + lo - src:here + hi - src])
            lo = hi
        shards.append(jnp.concatenate(pieces, axis=1))
    return jnp.stack(shards)


def _rope_tables(tp):
    pos = jnp.arange(tp, dtype=F32) - float(PAD_FRONT)
    inv = 1.0 / (ROPE_BASE ** jnp.linspace(0.0, 1.0, HEAD_DIM // 2, dtype=F32))
    ang = pos[:, None] * inv[None, :]
    cos = jnp.repeat(jnp.cos(ang), 2, axis=1)
    sin = jnp.repeat(jnp.sin(ang), 2, axis=1) * jnp.tile(jnp.array([-1.0, 1.0], F32), HEAD_DIM // 2)[None, :]
    return cos, sin


def _retention_tables(nh):
    log_gamma = jnp.log1p(-jnp.exp2(-5.0 - jnp.arange(nh, dtype=F32)))
    pos = jnp.arange(CHUNK, dtype=F32)
    causal = pos[:, None] >= pos[None, :]
    diff = pos[:, None] - pos[None, :]
    dec = jnp.where(causal, jnp.exp(jnp.where(causal, diff, 0.0) * log_gamma[:, None, None]), 0.0)
    ones = jnp.ones((1, 1, HEAD_DIM), F32)
    xi = jnp.exp((pos + 1.0)[None, :] * log_gamma[:, None])[:, :, None] * ones
    zeta = jnp.exp((CHUNK - 1.0 - pos)[None, :] * log_gamma[:, None])[:, :, None] * ones
    cd = jnp.exp(CHUNK * log_gamma)[:, None, None] * jnp.ones((1, 8, HEAD_DIM), F32)
    return dec, xi, zeta, cd


SHARDED = ("meta_tokens", "ffn1_w_in", "ffn1_w_out", "w_in", "gdn_conv_w", "w_branch_gdn", "w_branch_ret",
           "w_out", "ffn2_w_in", "ffn2_w_out")
COLUMN_SHARDED = ("meta_tokens", "ffn1_w_in", "w_in", "gdn_conv_w", "ffn2_w_in")
EXACT_F32 = ("meta_tokens", "gdn_conv_w")
REPLICATED = ("ffn1_norm", "mix_norm", "gdn_a_log", "gdn_dt_bias", "gdn_out_norm", "ret_out_norm", "ffn2_norm",
              "final_norm")
WEIGHTS = ("meta_tokens", "ffn1_norm", "ffn1_w_in", "ffn1_w_out", "mix_norm", "w_in", "gdn_conv_w", "gdn_a_log",
           "gdn_dt_bias", "gdn_out_norm", "ret_out_norm", "w_branch_gdn", "w_branch_ret", "w_out", "ffn2_norm",
           "ffn2_w_in", "ffn2_w_out", "final_norm")


def _as2d(a):
    if a.ndim == 3:
        return a[0]
    if a.ndim == 1:
        return a[None, :]
    return a


def _rows_of(shards):
    return shards.reshape(-1, shards.shape[2])


def _cols_of(shards):
    return shards.transpose(1, 0, 2).reshape(shards.shape[1], -1)


def _row_shards(a):
    return a.reshape(N_DEV, -1, a.shape[1])


def _col_shards(a):
    return a.reshape(a.shape[0], N_DEV, -1).transpose(1, 0, 2)


GATHER_FIRST = ("meta_tokens", "ffn1_w_in", "ffn1_w_out")
GATHER_BEHIND_FFN1 = ("w_in", "gdn_conv_w")
GATHER_BEHIND_PROJ = ("w_branch_gdn", "w_branch_ret", "w_out", "ffn2_w_in", "ffn2_w_out")
SCATTER_BEHIND_DN2 = ("ffn2_w_in", "ffn2_w_out", "w_branch_gdn", "w_branch_ret", "w_out")
SCATTER_BEHIND_FFN1 = ("w_in", "gdn_conv_w")
SCATTER_LAST = ("meta_tokens", "ffn1_w_in", "ffn1_w_out")


def _device_step(x, target, send, rep):
    seq, d = x.shape
    tp = HEAD_ROWS + seq
    hv = d
    nh = hv // HEAD_DIM
    assert seq % CHUNK == 0 and tp % HEAD_ROWS == 0
    bf16_shards = lambda grads, names: [grads[n].astype(BF16) for n in names]

    pad_lanes = lambda row: jnp.pad(row, ((0, 0), (nh, LANES - 2 * nh)))
    alog = pad_lanes(rep["gdn_a_log"])
    dtb = pad_lanes(rep["gdn_dt_bias"])
    cos, sin = _rope_tables(tp)
    dec, xi, zeta, cd = _retention_tables(nh)

    got = dict(zip(GATHER_FIRST, _exchange([send[n] for n in GATHER_FIRST], False, "gather_ffn1")))
    h0 = jnp.concatenate([jnp.zeros((PAD_FRONT, d), F32), _cols_of(got["meta_tokens"]), x], axis=0)
    f1i, f1o = got["ffn1_w_in"], _rows_of(got["ffn1_w_out"])
    h1, moved = _ffn_fwd(h0, rep["ffn1_norm"], f1i, f1o, "ffn1_fwd", ([send[n] for n in GATHER_BEHIND_FFN1], False))
    got.update(zip(GATHER_BEHIND_FFN1, moved))
    wp = _win_from_shards(got["w_in"], hv, nh)
    conv_w = _cols_of(got["gdn_conv_w"])
    (proj, n2), moved = _proj_fwd(h1, rep["mix_norm"], wp, "proj_fwd",
                                  ([send[n] for n in GATHER_BEHIND_PROJ], False))
    got.update(zip(GATHER_BEHIND_PROJ, moved))
    wbg, wbr, wo = _rows_of(got["w_branch_gdn"]), _rows_of(got["w_branch_ret"]), _rows_of(got["w_out"])
    f2i, f2o = got["ffn2_w_in"], _rows_of(got["ffn2_w_out"])
    qkv = _conv_fwd(proj, conv_w, hv, "conv_fwd")
    oa, s_gdn, t_gdn = _gdn_fwd(qkv, proj, alog, dtb, nh, "gdn_fwd")
    ob, s_ret = _ret_fwd(proj, cos, sin, dec, xi, zeta, cd, nh, "ret_fwd")
    h2 = _post_fwd(oa, ob, proj, rep["gdn_out_norm"], rep["ret_out_norm"], wbg, wbr, wo, h1, "post_fwd")
    h3, _ = _ffn_fwd(h2, rep["ffn2_norm"], f2i, f2o, "ffn2_fwd")
    loss_row, dh3, d_final = _final(h3, rep["final_norm"], target, "final")

    (dh2, d_f2n, n3, hid2, dag2, dau2), _ = _ffn_bwd(h2, dh3, rep["ffn2_norm"], f2i, f2o, "ffn2_bwd")
    grads = {"ffn2_w_in": jnp.concatenate([_matmul_tn_blocks(n3, dag2, "ffn2_dwg"),
                                           _matmul_tn_blocks(n3, dau2, "ffn2_dwu")]),
             "ffn2_w_out": _row_shards(_matmul_tn_blocks(hid2, dh3, "ffn2_dwo", 0.5))}

    doa, dob, dgate, ya, yb, merged, dpa, dpb, d_gn, d_rn = _post_bwd(
        oa, ob, proj, rep["gdn_out_norm"], rep["ret_out_norm"], wbg, wbr, wo, dh2, "post_bwd")
    grads["w_branch_gdn"] = _row_shards(_matmul_tn(ya, dpa, "dw_branch_gdn"))
    grads["w_branch_ret"] = _row_shards(_matmul_tn(yb, dpb, "dw_branch_ret"))
    grads["w_out"] = _row_shards(_matmul_tn(merged, dh2, "dw_out"))

    d_ret = _ret_bwd(proj, cos, sin, dec, xi, zeta, cd, s_ret, dob, nh, "ret_bwd")
    gdn_grads = _gdn_bwd(qkv, proj, alog, dtb, s_gdn, t_gdn, doa, nh, "gdn_bwd")
    dba, d_alog, d_dtb = gdn_grads[3:]
    dpre, g_conv = [], []
    for grp, tag in enumerate("qkv"):
        dc, dw = _conv_bwd_pre(proj, conv_w, gdn_grads[grp], grp, hv, "conv_bwd_pre_" + tag)
        dpre.append(_conv_bwd_in(dc, conv_w, grp, "conv_bwd_in_" + tag))
        g_conv.append(dw)
    grads["gdn_conv_w"] = _col_shards(jnp.concatenate(g_conv, axis=1))

    wide = dpre + list(d_ret) + [dgate]
    dn2, moved = _matmul_nt_parts(wide, wp[:, :10 * hv], None, "dn2_wide",
                                  (bf16_shards(grads, SCATTER_BEHIND_DN2), True))
    parts = dict(zip(SCATTER_BEHIND_DN2, moved))
    dn2, _ = _matmul_nt_parts([dba], wp[:, 10 * hv:], dn2, "dn2_beta_alpha")
    g_wp = [_matmul_tn(n2, dg, "dw_in_%d" % idx) for idx, dg in enumerate(wide + [dba])]
    grads["w_in"] = _win_grad_to_shards(jnp.concatenate(g_wp, axis=1), hv, nh, send["w_in"].shape[1])
    dh1, d_mixn = _norm_bwd(h1, rep["mix_norm"], dn2, dh2, "mix_norm_bwd")

    (dh0, d_f1n, n1, hid1, dag1, dau1), moved = _ffn_bwd(h0, dh1, rep["ffn1_norm"], f1i, f1o, "ffn1_bwd",
                                                         (bf16_shards(grads, SCATTER_BEHIND_FFN1), True))
    parts.update(zip(SCATTER_BEHIND_FFN1, moved))
    grads["ffn1_w_in"] = jnp.concatenate([_matmul_tn_blocks(n1, dag1, "ffn1_dwg"),
                                          _matmul_tn_blocks(n1, dau1, "ffn1_dwu")])
    grads["ffn1_w_out"] = _row_shards(_matmul_tn_blocks(hid1, dh1, "ffn1_dwo", 0.5))
    grads["meta_tokens"] = _col_shards(dh0[PAD_FRONT:HEAD_ROWS])
    parts.update(zip(SCATTER_LAST, _exchange(bf16_shards(grads, SCATTER_LAST), True, "scatter_ffn1")))

    small = {"ffn1_norm": d_f1n, "mix_norm": d_mixn, "gdn_a_log": d_alog[:, nh:2 * nh],
             "gdn_dt_bias": d_dtb[:, nh:2 * nh], "gdn_out_norm": d_gn, "ret_out_norm": d_rn, "ffn2_norm": d_f2n,
             "final_norm": d_final}
    return loss_row[0, 0], dh0[HEAD_ROWS:], parts, small


def kernel(x, meta_tokens, ffn1_norm, ffn1_w_in, ffn1_w_out, mix_norm, w_in, gdn_conv_w, gdn_a_log, gdn_dt_bias, gdn_out_norm, ret_out_norm, w_branch_gdn, w_branch_ret, w_out, ffn2_norm, ffn2_w_in, ffn2_w_out, final_norm, loss_target, m_meta_tokens, m_ffn1_norm, m_ffn1_w_in, m_ffn1_w_out, m_mix_norm, m_w_in, m_gdn_conv_w, m_gdn_a_log, m_gdn_dt_bias, m_gdn_out_norm, m_ret_out_norm, m_w_branch_gdn, m_w_branch_ret, m_w_out, m_ffn2_norm, m_ffn2_w_in, m_ffn2_w_out, m_final_norm, v_meta_tokens, v_ffn1_norm, v_ffn1_w_in, v_ffn1_w_out, v_mix_norm, v_w_in, v_gdn_conv_w, v_gdn_a_log, v_gdn_dt_bias, v_gdn_out_norm, v_ret_out_norm, v_w_branch_gdn, v_w_branch_ret, v_w_out, v_ffn2_norm, v_ffn2_w_in, v_ffn2_w_out, v_final_norm):
    given = dict(locals())
    params = {n: _as2d(given[n]) for n in WEIGHTS}
    local = {n: params[n] for n in SHARDED}
    rep = {n: params[n] for n in REPLICATED}

    send = {n: local[n] if n in EXACT_F32 else local[n].astype(BF16) for n in SHARDED}
    loss_sum, grad_x, parts, small = _device_step(x[0], loss_target[0], send, rep)
    parts.update(zip(REPLICATED, _exchange([small[n] for n in REPLICATED], False, "gather_small_grads")))
    loss = lax.psum(loss_sum, ("x", "y", "c"))

    outs = {}
    for n in WEIGHTS:
        res = _adamw(params[n], parts[n], _as2d(given["m_" + n]), _as2d(given["v_" + n]), "adamw_" + n)
        outs[n] = [r.reshape(given[n].shape) for r in res]
    return (loss, grad_x[None], *[outs[n][0] for n in WEIGHTS], *[outs[n][1] for n in WEIGHTS],
            *[outs[n][2] for n in WEIGHTS], *[outs[n][3] for n in WEIGHTS])
```

```python
import functools
import math

import numpy as np
import jax
import jax.numpy as jnp
from jax import lax
from jax.experimental import pallas as pl
from jax.experimental.pallas import tpu as pltpu

F32 = jnp.float32
BF16 = jnp.bfloat16

N_DEV = 8
N_META = 16
CHUNK = 64
HEAD_DIM = 128
CONV_K = 4
ROPE_BASE = 10000.0
EPS = 1e-6
PAD_FRONT = 240
HEAD_ROWS = PAD_FRONT + N_META
LANES = 128
VMEM_LIMIT_BYTES = 56 * 1024 * 1024

ADAM_LR = 0.001
ADAM_B1 = 0.9
ADAM_B2 = 0.999
ADAM_EPS = 1e-08
ADAM_WD = 0.01
ADAM_STEP = 10

NN = (((1,), (0,)), ((), ()))
NT = (((1,), (1,)), ((), ()))
TN = (((0,), (0,)), ((), ()))


def _tile(n, target, mult):
    best = 0
    for t in range(mult, min(n, target) + 1, mult):
        if n % t == 0:
            best = t
    return best if best else n


def _params(*semantics):
    return pltpu.CompilerParams(dimension_semantics=semantics, vmem_limit_bytes=VMEM_LIMIT_BYTES)


def _split(a, pieces):
    out = []
    for _ in range(pieces - 1):
        part = a.astype(BF16)
        out.append(part)
        a = a - part.astype(F32)
    return out + [a.astype(BF16)]


def _raw_dot(a, b, dims, hi):
    dot = lambda x, y: lax.dot_general(x, y, dims, preferred_element_type=F32)
    if hi:
        (a_hi, a_lo), (b_hi, b_lo) = _split(a, 2), _split(b, 2)
        return dot(a_hi, b_hi) + (dot(a_hi, b_lo) + dot(a_lo, b_hi))
    return dot(a.astype(BF16), b.astype(BF16))


def _mask_dot(mask, x, dims):
    mask = mask.astype(BF16)
    hi, mid, lo = [lax.dot_general(mask, p, dims, preferred_element_type=F32) for p in _split(x, 3)]
    return hi + (mid + lo)


@jax.custom_vjp
def _cumsum_rows(x):
    c = x.shape[0]
    tril = lax.broadcasted_iota(jnp.int32, (c, c), 0) >= lax.broadcasted_iota(jnp.int32, (c, c), 1)
    return _mask_dot(tril, x, NN)


def _cumsum_rows_bwd(_, g):
    c = g.shape[0]
    tril = lax.broadcasted_iota(jnp.int32, (c, c), 0) >= lax.broadcasted_iota(jnp.int32, (c, c), 1)
    return (_mask_dot(tril, g, TN),)


_cumsum_rows.defvjp(lambda x: (_cumsum_rows(x), None), _cumsum_rows_bwd)


def _unit_lower_inverses(xs):
    c = xs[0].shape[0]
    eye = (lax.broadcasted_iota(jnp.int32, (c, c), 0) == lax.broadcasted_iota(jnp.int32, (c, c), 1)).astype(F32)
    t_inv = [eye + x for x in xs]
    for _ in range(int(math.log2(c)) - 1):
        xs = [_raw_dot(x, x, NN, True) for x in xs]
        t_inv = [t + _raw_dot(t, x, NN, True) for t, x in zip(t_inv, xs)]
    return t_inv


@jax.custom_vjp
def _known_inverse(x_neg, t_inv):
    return t_inv


_known_inverse.defvjp(
    lambda x_neg, t_inv: (t_inv, t_inv),
    lambda t_inv, g: (_raw_dot(_raw_dot(t_inv, g, TN, True), t_inv, NT, True), jnp.zeros_like(t_inv)))


def _make_mm(hi):
    @jax.custom_vjp
    def nn(a, b):
        return _raw_dot(a, b, NN, hi)

    @jax.custom_vjp
    def nt(a, b):
        return _raw_dot(a, b, NT, hi)

    @jax.custom_vjp
    def tn(a, b):
        return _raw_dot(a, b, TN, hi)

    nn.defvjp(lambda a, b: (_raw_dot(a, b, NN, hi), (a, b)),
              lambda r, g: (_raw_dot(g, r[1], NT, hi), _raw_dot(r[0], g, TN, hi)))
    nt.defvjp(lambda a, b: (_raw_dot(a, b, NT, hi), (a, b)),
              lambda r, g: (_raw_dot(g, r[1], NN, hi), _raw_dot(g, r[0], TN, hi)))
    tn.defvjp(lambda a, b: (_raw_dot(a, b, TN, hi), (a, b)),
              lambda r, g: (_raw_dot(r[1], g, NT, hi), _raw_dot(r[0], g, NN, hi)))
    return nn, nt, tn


def _silu(x):
    return x * jax.nn.sigmoid(x)


def _rms_parts(x):
    r = lax.rsqrt(jnp.mean(x * x, axis=-1, keepdims=True) + EPS)
    return x * r, r


def _rms_bwd(dy, xh, r, gain):
    dxh = dy * gain
    dx = r * (dxh - xh * jnp.mean(dxh * xh, axis=-1, keepdims=True))
    return dx, jnp.sum(dy * xh, axis=0, keepdims=True)


def _ffn_specs(tm, d, tf, nj):
    return [pl.BlockSpec((tm, d), lambda i, j: (i, 0)), pl.BlockSpec((1, d), lambda i, j: (0, 0)),
            pl.BlockSpec((1, d, tf), lambda i, j: (j, 0, 0)), pl.BlockSpec((1, d, tf), lambda i, j: (nj + j, 0, 0)),
            pl.BlockSpec((tf, d), lambda i, j: (j, 0))]


def _first_step(ndim):
    return lambda: functools.reduce(lambda a, b: a & b, [pl.program_id(k) == 0 for k in range(ndim)])


def _last_step(grid):
    return lambda: functools.reduce(lambda a, b: a & b, [pl.program_id(k) == g - 1 for k, g in enumerate(grid)])


def _ffn_fwd(h, gain, w_in, wo, name, carry=None):
    tp, d = h.shape
    tf = w_in.shape[2]
    nj = w_in.shape[0] // 2
    tm = _tile(tp, 768, 8)
    row, vec, wg_spec, wu_spec, wo_spec = _ffn_specs(tm, d, tf, nj)

    def body(h_ref, g_ref, wg3_ref, wu3_ref, wo_ref, o_ref, ag3_ref, au3_ref, n_sc, acc_sc):
        wg_ref, wu_ref = wg3_ref.at[0], wu3_ref.at[0]
        j = pl.program_id(1)

        @pl.when(j == 0)
        def _():
            xh, _ = _rms_parts(h_ref[...])
            n_sc[...] = (xh * g_ref[...]).astype(BF16)
            acc_sc[...] = jnp.zeros_like(acc_sc)

        n = n_sc[...]
        a_g = jnp.dot(n, wg_ref[...], preferred_element_type=F32)
        a_u = jnp.dot(n, wu_ref[...], preferred_element_type=F32)
        ag3_ref[0] = a_g
        au3_ref[0] = a_u
        hid = (_silu(a_g) * a_u).astype(BF16)
        acc_sc[...] += jnp.dot(hid, wo_ref[...], preferred_element_type=F32)

        @pl.when(j == nj - 1)
        def _():
            o_ref[...] = h_ref[...] + 0.5 * acc_sc[...]

    grid = (tp // tm, nj)
    act = pl.BlockSpec((1, tm, tf), lambda i, j: (j, i, 0))
    return _carried_call(
        body, carry, _first_step(2), _last_step(grid), name=name, grid=grid,
        in_specs=[row, vec, wg_spec, wu_spec, wo_spec], out_specs=[row, act, act],
        out_shape=[jax.ShapeDtypeStruct((tp, d), F32)] + [jax.ShapeDtypeStruct((nj, tp, tf), F32)] * 2,
        scratch_shapes=[pltpu.VMEM((tm, d), BF16), pltpu.VMEM((tm, d), F32)])(h, gain, w_in, w_in, wo)


def _ffn_bwd(h, dho, gain, w_in, wo, ag3, au3, name, carry=None):
    tp, d = h.shape
    tf = w_in.shape[2]
    nj = w_in.shape[0] // 2
    tm = _tile(tp, 528, 16)
    ni = tp // tm
    row, vec, wg_spec, wu_spec, wo_spec = _ffn_specs(tm, d, tf, nj)

    def body(h_ref, dho_ref, g_ref, wg3_ref, wu3_ref, wo_ref, ag3_ref, au3_ref,
             dh_ref, dgain_ref, n_ref, hid3_ref, dag3_ref, dau3_ref, dn_sc, dhb_sc):
        wg_ref, wu_ref = wg3_ref.at[0], wu3_ref.at[0]
        hid_ref, dag_ref, dau_ref = hid3_ref.at[0], dag3_ref.at[0], dau3_ref.at[0]
        i, j = pl.program_id(0), pl.program_id(1)

        @pl.when(j == 0)
        def _():
            xh, _ = _rms_parts(h_ref[...])
            n_ref[...] = (xh * g_ref[...]).astype(BF16)
            dn_sc[...] = jnp.zeros_like(dn_sc)
            dhb_sc[...] = (0.5 * dho_ref[...]).astype(BF16)

        @pl.when((i == 0) & (j == 0))
        def _():
            dgain_ref[...] = jnp.zeros_like(dgain_ref)

        a_g = ag3_ref[0]
        a_u = au3_ref[0]
        sg = jax.nn.sigmoid(a_g)
        s = a_g * sg
        hid_ref[...] = (s * a_u).astype(BF16)
        d_hid = lax.dot_general(dhb_sc[...], wo_ref[...], NT, preferred_element_type=F32)
        d_au = (d_hid * s).astype(BF16)
        d_ag = (d_hid * a_u * (sg * (1.0 + a_g * (1.0 - sg)))).astype(BF16)
        dau_ref[...] = d_au
        dag_ref[...] = d_ag
        dn_sc[...] += (lax.dot_general(d_ag, wg_ref[...], NT, preferred_element_type=F32)
                       + lax.dot_general(d_au, wu_ref[...], NT, preferred_element_type=F32))

        @pl.when(j == nj - 1)
        def _():
            xh, r = _rms_parts(h_ref[...])
            dx, dg = _rms_bwd(dn_sc[...], xh, r, g_ref[...])
            dh_ref[...] = dho_ref[...] + dx
            dgain_ref[...] += dg

    act = pl.BlockSpec((1, tm, tf), lambda i, j: (j, i, 0))
    return _carried_call(
        body, carry, _first_step(2), _last_step((ni, nj)), name=name, grid=(ni, nj),
        in_specs=[row, row, vec, wg_spec, wu_spec, wo_spec, act, act],
        out_specs=[row, vec, row, act, act, act],
        out_shape=[jax.ShapeDtypeStruct((tp, d), F32), jax.ShapeDtypeStruct((1, d), F32),
                   jax.ShapeDtypeStruct((tp, d), BF16)] + [jax.ShapeDtypeStruct((nj, tp, tf), BF16)] * 3,
        scratch_shapes=[pltpu.VMEM((tm, d), F32), pltpu.VMEM((tm, d), BF16)])(
            h, dho, gain, w_in, w_in, wo, ag3, au3)


def _matmul_tn(a, b, name, scale=1.0):
    t, m = a.shape
    n = b.shape[1]
    bm = _tile(m, 1024, LANES)
    bn = _tile(n, 1536, LANES)
    tk = _tile(t, 768, 16)
    nk = t // tk

    def body(a_ref, b_ref, o_ref):
        k = pl.program_id(2)

        @pl.when(k == 0)
        def _():
            o_ref[...] = jnp.zeros_like(o_ref)

        o_ref[...] += lax.dot_general(a_ref[...].astype(BF16), b_ref[...].astype(BF16), TN,
                                      preferred_element_type=F32)

        if scale != 1.0:
            @pl.when(k == nk - 1)
            def _():
                o_ref[...] = o_ref[...] * scale

    return pl.pallas_call(
        body, name=name, grid=(m // bm, n // bn, nk),
        in_specs=[pl.BlockSpec((tk, bm), lambda i, j, k: (k, i)), pl.BlockSpec((tk, bn), lambda i, j, k: (k, j))],
        out_specs=pl.BlockSpec((bm, bn), lambda i, j, k: (i, j)),
        out_shape=jax.ShapeDtypeStruct((m, n), F32),
        compiler_params=_params("parallel", "parallel", "arbitrary"))(a, b)


def _matmul_tn_blocks(a, b, name, scale=1.0, carry=None):
    a_blocked = a.ndim == 3
    nb, t = (a.shape[0], a.shape[1]) if a_blocked else (b.shape[0], b.shape[1])
    m, n = a.shape[-1], b.shape[-1]
    tk = _tile(t, 768, 16)
    nk = t // tk
    if a_blocked:
        bo = _tile(n, 1024, LANES)
        a_spec = pl.BlockSpec((1, tk, m), lambda p, o, k: (p, k, 0))
        b_spec = pl.BlockSpec((tk, bo), lambda p, o, k: (k, o))
        o_spec = pl.BlockSpec((m, bo), lambda p, o, k: (p, o))
        out_shape = jax.ShapeDtypeStruct((nb * m, n), F32)
        grid = (nb, n // bo, nk)
    else:
        bo = _tile(m, 1024, LANES)
        a_spec = pl.BlockSpec((tk, bo), lambda p, o, k: (k, o))
        b_spec = pl.BlockSpec((1, tk, n), lambda p, o, k: (p, k, 0))
        o_spec = pl.BlockSpec((1, bo, n), lambda p, o, k: (p, o, 0))
        out_shape = jax.ShapeDtypeStruct((nb, m, n), F32)
        grid = (nb, m // bo, nk)

    def body(a_ref, b_ref, o_ref):
        k = pl.program_id(2)
        a_blk = a_ref[0] if a_blocked else a_ref[...]
        b_blk = b_ref[...] if a_blocked else b_ref[0]
        part = lax.dot_general(a_blk.astype(BF16), b_blk.astype(BF16), TN, preferred_element_type=F32)
        out = o_ref if a_blocked else o_ref.at[0]

        @pl.when(k == 0)
        def _():
            out[...] = part

        @pl.when(k > 0)
        def _():
            out[...] += part

        if scale != 1.0:
            @pl.when(k == nk - 1)
            def _():
                out[...] = out[...] * scale

    (out,), moved = _carried_call(body, carry, _first_step(3), _last_step(grid), name=name, grid=grid,
                                  in_specs=[a_spec, b_spec], out_specs=[o_spec], out_shape=[out_shape])(a, b)
    return out if carry is None else (out, moved)


def _matmul_nt_parts(parts, w, acc, name, carry=None):
    t = parts[0].shape[0]
    d = w.shape[0]
    widths = [p.shape[1] for p in parts]
    tk = _tile(math.gcd(*widths), 1024, LANES)
    counts = [wd // tk for wd in widths]
    starts = [sum(counts[:g]) for g in range(len(parts))]
    nk = sum(counts)
    tm = _tile(t, 768, 8)
    n_parts = len(parts)

    def body(*refs):
        a_refs, w_ref, o_ref = refs[:n_parts], refs[n_parts], refs[-1]
        k = pl.program_id(1)

        @pl.when(k == 0)
        def _():
            o_ref[...] = jnp.zeros_like(o_ref) if acc is None else refs[n_parts + 1][...]

        for g in range(n_parts):
            @pl.when((k >= starts[g]) & (k < starts[g] + counts[g]))
            def _(g=g):
                o_ref[...] += lax.dot_general(a_refs[g][...].astype(BF16), w_ref[...], NT,
                                              preferred_element_type=F32)

    in_specs = [pl.BlockSpec((tm, tk), lambda i, k, lo=starts[g], nb=counts[g]: (i, jnp.clip(k - lo, 0, nb - 1)))
                for g in range(n_parts)]
    in_specs.append(pl.BlockSpec((d, tk), lambda i, k: (0, k)))
    args = list(parts) + [w]
    if acc is not None:
        in_specs.append(pl.BlockSpec((tm, d), lambda i, k: (i, 0)))
        args.append(acc)
    grid = (t // tm, nk)
    (out,), moved = _carried_call(
        body, carry, _first_step(2), _last_step(grid), name=name, grid=grid, in_specs=in_specs,
        out_specs=[pl.BlockSpec((tm, d), lambda i, k: (i, 0))],
        out_shape=[jax.ShapeDtypeStruct((t, d), F32)])(*args)
    return out, moved


def _proj_fwd(h, gain, wp, name, carry=None):
    tp, d = h.shape
    npad = wp.shape[1]
    tm = _tile(tp, 768, 8)
    tn = _tile(npad, 1152, LANES)

    def body(h_ref, g_ref, w_ref, o_ref, n_ref):
        @pl.when(pl.program_id(1) == 0)
        def _():
            xh, _ = _rms_parts(h_ref[...])
            n_ref[...] = (xh * g_ref[...]).astype(BF16)

        o_ref[...] = jnp.dot(n_ref[...], w_ref[...], preferred_element_type=F32)

    grid = (tp // tm, npad // tn)
    return _carried_call(
        body, carry, _first_step(2), _last_step(grid), name=name, grid=grid,
        in_specs=[pl.BlockSpec((tm, d), lambda i, j: (i, 0)), pl.BlockSpec((1, d), lambda i, j: (0, 0)),
                  pl.BlockSpec((d, tn), lambda i, j: (0, j))],
        out_specs=[pl.BlockSpec((tm, tn), lambda i, j: (i, j)), pl.BlockSpec((tm, d), lambda i, j: (i, 0))],
        out_shape=[jax.ShapeDtypeStruct((tp, npad), F32), jax.ShapeDtypeStruct((tp, d), BF16)])(h, gain, wp)


def _norm_bwd(h, gain, dn, dres, name):
    tp, d = h.shape
    tm = _tile(tp, 256, 8)

    def body(h_ref, g_ref, dn_ref, dres_ref, dh_ref, dgain_ref):
        @pl.when(pl.program_id(0) == 0)
        def _():
            dgain_ref[...] = jnp.zeros_like(dgain_ref)

        xh, r = _rms_parts(h_ref[...])
        dx, dg = _rms_bwd(dn_ref[...], xh, r, g_ref[...])
        dh_ref[...] = dres_ref[...] + dx
        dgain_ref[...] += dg

    row = pl.BlockSpec((tm, d), lambda i: (i, 0))
    vec = pl.BlockSpec((1, d), lambda i: (0, 0))
    return pl.pallas_call(
        body, name=name, grid=(tp // tm,), in_specs=[row, vec, row, row], out_specs=[row, vec],
        out_shape=[jax.ShapeDtypeStruct((tp, d), F32), jax.ShapeDtypeStruct((1, d), F32)],
        compiler_params=_params("arbitrary"))(h, gain, dn, dres)


def _head_post(a, grp):
    a = _silu(a)
    r = lax.rsqrt(jnp.sum(a * a, axis=-1, keepdims=True) + EPS)
    if isinstance(grp, int):
        return a if grp == 2 else a * r * (HEAD_DIM ** -0.5 if grp == 0 else 1.0)
    scale = jnp.where(grp == 0, HEAD_DIM ** -0.5, 1.0).astype(F32)
    return jnp.where(grp == 2, a, a * r * scale)


def _conv_taps(ext_sc, w_ref, tm):
    c = None
    for i in range(CONV_K):
        s = CONV_K - 1 - i
        term = w_ref[i:i + 1, :] * ext_sc[8 - s:8 - s + tm, :]
        c = term if c is None else c + term
    return c


def _conv_fwd(proj, conv_w, hv, name):
    tp = proj.shape[0]
    tm = _tile(tp, 256, 8)
    nh = hv // HEAD_DIM

    def body(x_ref, halo_ref, w_ref, o_ref, ext_sc):
        i, grp = pl.program_id(0), pl.program_id(1)
        ext_sc[0:8, :] = jnp.where(i == 0, 0.0, halo_ref[...])
        ext_sc[8:, :] = x_ref[...]
        c = _conv_taps(ext_sc, w_ref, tm)
        for h in range(nh):
            sl = slice(h * HEAD_DIM, (h + 1) * HEAD_DIM)
            o_ref[:, sl] = _head_post(c[:, sl], grp)

    return pl.pallas_call(
        body, name=name, grid=(tp // tm, 3),
        in_specs=[pl.BlockSpec((tm, hv), lambda i, g: (i, g)),
                  pl.BlockSpec((8, hv), lambda i, g: (jnp.maximum(i * (tm // 8) - 1, 0), g)),
                  pl.BlockSpec((CONV_K, hv), lambda i, g: (0, g))],
        out_specs=pl.BlockSpec((tm, hv), lambda i, g: (i, g)),
        out_shape=jax.ShapeDtypeStruct((tp, 3 * hv), F32),
        scratch_shapes=[pltpu.VMEM((tm + 8, hv), F32)],
        compiler_params=_params("parallel", "arbitrary"))(proj, proj, conv_w)


def _conv_bwd(proj, conv_w, dy, grp, hv, name):
    tp = proj.shape[0]
    tm = _tile(tp, 256, 8)
    ni = tp // tm
    nh = hv // HEAD_DIM

    def body(x_ref, halo_ref, w_ref, dy_ref, dx_ref, dw_ref, ext_sc, dc_sc):
        step = pl.program_id(0)
        ext_sc[0:8, :] = jnp.where(step == ni - 1, 0.0, halo_ref[...])
        ext_sc[8:, :] = x_ref[...]
        c = _conv_taps(ext_sc, w_ref, tm)
        @pl.when(step == 0)
        def _():
            dc_sc[tm:, :] = jnp.zeros((8, hv), F32)

        @pl.when(step > 0)
        def _():
            dc_sc[tm:, :] = dc_sc[0:8, :]

        for h in range(nh):
            sl = slice(h * HEAD_DIM, (h + 1) * HEAD_DIM)
            _, vjp = jax.vjp(lambda a: _head_post(a, grp), c[:, sl])
            dc_sc[0:tm, sl] = vjp(dy_ref[:, sl])[0]

        @pl.when(step == 0)
        def _():
            dw_ref[...] = jnp.zeros_like(dw_ref)

        dc = dc_sc[0:tm, :]
        dx = None
        for k in range(CONV_K):
            s = CONV_K - 1 - k
            dw_ref[k:k + 1, :] += jnp.sum(dc * ext_sc[8 - s:8 - s + tm, :], axis=0, keepdims=True)
            term = w_ref[k:k + 1, :] * dc_sc[s:s + tm, :]
            dx = term if dx is None else dx + term
        dx_ref[...] = dx.astype(BF16)

    tile = lambda step: ni - 1 - step
    return pl.pallas_call(
        body, name=name, grid=(ni,),
        in_specs=[pl.BlockSpec((tm, hv), lambda s: (tile(s), grp)),
                  pl.BlockSpec((8, hv), lambda s: (jnp.maximum(tile(s) * (tm // 8) - 1, 0), grp)),
                  pl.BlockSpec((CONV_K, hv), lambda s: (0, grp)),
                  pl.BlockSpec((tm, hv), lambda s: (tile(s), 0))],
        out_specs=[pl.BlockSpec((tm, hv), lambda s: (tile(s), 0)), pl.BlockSpec((CONV_K, hv), lambda s: (0, 0))],
        out_shape=[jax.ShapeDtypeStruct((tp, hv), BF16), jax.ShapeDtypeStruct((CONV_K, hv), F32)],
        scratch_shapes=[pltpu.VMEM((tm + 8, hv), F32), pltpu.VMEM((tm + 8, hv), F32)],
        compiler_params=_params("arbitrary"))(proj, proj, conv_w, dy)


def _gdn_gates(ba, alog, dtb):
    x = ba + dtb
    softplus = jnp.maximum(x, 0.0) + jnp.log1p(jnp.exp(-jnp.abs(x)))
    return _cumsum_rows(-jnp.exp(alog) * softplus), jax.nn.sigmoid(ba)


def _gdn_heads(states, qs, ks, vs, gc_all, beta_all, known_inverses=None):
    mm_nn, mm_nt, mm_tn = _make_mm(False)
    hi_nn, _, _ = _make_mm(True)
    nh = len(qs)
    heads = range(nh)
    c = qs[0].shape[0]
    lane = lax.broadcasted_iota(jnp.int32, (c, LANES), 1)
    last_row = (lax.broadcasted_iota(jnp.int32, (c, 1), 0) == c - 1).astype(F32)
    ri = lax.broadcasted_iota(jnp.int32, (c, c), 0)
    ci = lax.broadcasted_iota(jnp.int32, (c, c), 1)
    causal = ri >= ci
    strict = ri > ci
    eye = (ri == ci).astype(F32)
    sel_a = [(lane == nh + h).astype(F32) for h in heads]
    sel_b = [(lane == h).astype(F32) for h in heads]

    gcol = [jnp.sum(gc_all * sel_a[h], axis=1, keepdims=True) for h in heads]
    grow = [jnp.sum(eye * gcol[h], axis=0, keepdims=True) for h in heads]
    beta = [jnp.sum(beta_all * sel_b[h], axis=1, keepdims=True) for h in heads]
    decay = [jnp.where(causal, jnp.exp(jnp.where(causal, gcol[h] - grow[h], 0.0)), 0.0) for h in heads]
    kb = [ks[h] * beta[h] for h in heads]
    kk = [mm_nt(kb[h], ks[h]) for h in heads]
    qk = [mm_nt(qs[h], ks[h]) for h in heads]
    x_neg = [-jnp.where(strict, kk[h] * decay[h], 0.0) for h in heads]
    if known_inverses is None:
        t_inv = _unit_lower_inverses(x_neg)
    else:
        t_inv = [_known_inverse(x_neg[h], known_inverses[h]) for h in heads]
    eg = [jnp.exp(gcol[h]) for h in heads]
    u = [hi_nn(t_inv[h], vs[h] * beta[h]) for h in heads]
    w = [hi_nn(t_inv[h], kb[h] * eg[h]) for h in heads]
    qk = [qk[h] * decay[h] for h in heads]
    glast = [jnp.sum(gcol[h] * last_row, axis=0, keepdims=True) for h in heads]
    ws = [mm_nn(w[h], states[h]) for h in heads]
    qs_state = [mm_nn(qs[h] * eg[h], states[h]) for h in heads]
    v_new = [u[h] - ws[h] for h in heads]
    intra = [mm_nn(qk[h], v_new[h]) for h in heads]
    kv = [mm_tn(ks[h] * jnp.exp(glast[h] - gcol[h]), v_new[h]) for h in heads]
    outs = [qs_state[h] + intra[h] for h in heads]
    new_states = [states[h] * jnp.exp(glast[h]) + kv[h] for h in heads]
    return outs, new_states, t_inv


def _scan_specs(nh, nc, rev, first_col):
    cidx = (lambda c: nc - 1 - c) if rev else (lambda c: c)
    hv = nh * HEAD_DIM
    cols = [pl.BlockSpec((CHUNK, hv), lambda c, g=g: (cidx(c), first_col + g)) for g in range(3)]
    st = pl.BlockSpec((1, nh, HEAD_DIM, HEAD_DIM), lambda c: (cidx(c), 0, 0, 0))
    act = pl.BlockSpec((CHUNK, hv), lambda c: (cidx(c), 0))
    return cols, st, act


def _gdn_fwd(qkv, proj, alog, dtb, nh, name):
    tp = qkv.shape[0]
    nc = tp // CHUNK

    def body(q_ref, k_ref, v_ref, ba_ref, al_ref, dt_ref, o_ref, st_ref, inv_ref, s_sc):
        @pl.when(pl.program_id(0) == 0)
        def _():
            s_sc[...] = jnp.zeros_like(s_sc)

        gc_all, beta_all = _gdn_gates(ba_ref[...], al_ref[...], dt_ref[...])
        sls = [slice(h * HEAD_DIM, (h + 1) * HEAD_DIM) for h in range(nh)]
        states = [s_sc[h] for h in range(nh)]
        for h in range(nh):
            st_ref[0, h] = states[h]
        outs, new_states, t_inv = _gdn_heads(states, [q_ref[:, sl] for sl in sls], [k_ref[:, sl] for sl in sls],
                                             [v_ref[:, sl] for sl in sls], gc_all, beta_all)
        for h in range(nh):
            o_ref[:, sls[h]] = outs[h]
            s_sc[h] = new_states[h]
            inv_ref[0, h] = t_inv[h]

    cols, st, act = _scan_specs(nh, nc, False, 0)
    ba = pl.BlockSpec((CHUNK, LANES), lambda c: (c, 10 * nh))
    vec = pl.BlockSpec((1, LANES), lambda c: (0, 0))
    inv = pl.BlockSpec((1, nh, CHUNK, CHUNK), lambda c: (c, 0, 0, 0))
    return pl.pallas_call(
        body, name=name, grid=(nc,), in_specs=cols + [ba, vec, vec], out_specs=[act, st, inv],
        out_shape=[jax.ShapeDtypeStruct((tp, nh * HEAD_DIM), F32),
                   jax.ShapeDtypeStruct((nc, nh, HEAD_DIM, HEAD_DIM), F32),
                   jax.ShapeDtypeStruct((nc, nh, CHUNK, CHUNK), F32)],
        scratch_shapes=[pltpu.VMEM((nh, HEAD_DIM, HEAD_DIM), F32)],
        compiler_params=_params("arbitrary"))(qkv, qkv, qkv, proj, alog, dtb)


def _gdn_bwd(qkv, proj, alog, dtb, states, inverses, do, nh, name):
    tp = qkv.shape[0]
    nc = tp // CHUNK

    def body(q_ref, k_ref, v_ref, ba_ref, al_ref, dt_ref, st_ref, inv_ref, do_ref,
             dq_ref, dk_ref, dv_ref, dba_ref, dal_ref, ddt_ref, ds_sc):
        @pl.when(pl.program_id(0) == 0)
        def _():
            ds_sc[...] = jnp.zeros_like(ds_sc)
            dal_ref[...] = jnp.zeros_like(dal_ref)
            ddt_ref[...] = jnp.zeros_like(ddt_ref)

        (gc_all, beta_all), gates_vjp = jax.vjp(_gdn_gates, ba_ref[...], al_ref[...], dt_ref[...])
        sls = [slice(h * HEAD_DIM, (h + 1) * HEAD_DIM) for h in range(nh)]
        known = [inv_ref[0, h] for h in range(nh)]
        fn = lambda s, q, k, v, gc, beta: _gdn_heads(s, q, k, v, gc, beta, known)[:2]
        _, vjp = jax.vjp(fn, [st_ref[0, h] for h in range(nh)], [q_ref[:, sl] for sl in sls],
                         [k_ref[:, sl] for sl in sls], [v_ref[:, sl] for sl in sls], gc_all, beta_all)
        ds, dq, dk, dv, dgc, dbeta = vjp(([do_ref[:, sl] for sl in sls], [ds_sc[h] for h in range(nh)]))
        for h in range(nh):
            ds_sc[h] = ds[h]
            dq_ref[:, sls[h]] = dq[h]
            dk_ref[:, sls[h]] = dk[h]
            dv_ref[:, sls[h]] = dv[h]
        dba, dal, ddt = gates_vjp((dgc, dbeta))
        dba_ref[...] = dba
        dal_ref[...] += dal
        ddt_ref[...] += ddt

    cols, st, act = _scan_specs(nh, nc, True, 0)
    ba = pl.BlockSpec((CHUNK, LANES), lambda c: (nc - 1 - c, 10 * nh))
    vec = pl.BlockSpec((1, LANES), lambda c: (0, 0))
    inv = pl.BlockSpec((1, nh, CHUNK, CHUNK), lambda c: (nc - 1 - c, 0, 0, 0))
    return pl.pallas_call(
        body, name=name, grid=(nc,), in_specs=cols + [ba, vec, vec, st, inv, act],
        out_specs=[act, act, act, pl.BlockSpec((CHUNK, LANES), lambda c: (nc - 1 - c, 0)), vec, vec],
        out_shape=[jax.ShapeDtypeStruct((tp, nh * HEAD_DIM), F32)] * 3
                  + [jax.ShapeDtypeStruct((tp, LANES), F32), jax.ShapeDtypeStruct((1, LANES), F32),
                     jax.ShapeDtypeStruct((1, LANES), F32)],
        scratch_shapes=[pltpu.VMEM((nh, HEAD_DIM, HEAD_DIM), F32)],
        compiler_params=_params("arbitrary"))(qkv, qkv, qkv, proj, alog, dtb, states, inverses, do)


def _swap_pairs(t):
    lane = lax.broadcasted_iota(jnp.int32, t.shape, 1)
    n = t.shape[1]
    return jnp.where(lane % 2 == 0, pltpu.roll(t, n - 1, 1), pltpu.roll(t, 1, 1))


def _rot(t, cos, sin_signed):
    return t * cos + _swap_pairs(t) * sin_signed


def _rot_t(dt, cos, sin_signed):
    return dt * cos + _swap_pairs(dt * sin_signed)


def _ret_heads(states, qs, ks, vs, dec, xi, zeta, cd):
    mm_nn, mm_nt, mm_tn = _make_mm(False)
    heads = range(len(qs))
    scores = [mm_nt(qs[h], ks[h]) for h in heads]
    inter = [mm_nn(qs[h] * xi[h], states[h]) for h in heads]
    kv = [mm_tn(ks[h] * zeta[h], vs[h]) for h in heads]
    intra = [mm_nn(scores[h] * dec[h], vs[h]) for h in heads]
    return [intra[h] + inter[h] for h in heads], [states[h] * cd[h] + kv[h] for h in heads]


def _ret_table_specs(nh, nc, rev):
    cidx = (lambda c: nc - 1 - c) if rev else (lambda c: c)
    rope = pl.BlockSpec((CHUNK, HEAD_DIM), lambda c: (cidx(c), 0))
    dec = pl.BlockSpec((nh, CHUNK, CHUNK), lambda c: (0, 0, 0))
    tab = pl.BlockSpec((nh, CHUNK, HEAD_DIM), lambda c: (0, 0, 0))
    cd = pl.BlockSpec((nh, 8, HEAD_DIM), lambda c: (0, 0, 0))
    return [rope, rope, dec, tab, tab, cd]


def _ret_fwd(proj, cos, sin, dec, xi, zeta, cd, nh, name):
    tp = proj.shape[0]
    nc = tp // CHUNK
    kscale = HEAD_DIM ** -0.5

    def body(q_ref, k_ref, v_ref, cos_ref, sin_ref, dec_ref, xi_ref, zeta_ref, cd_ref, o_ref, st_ref, s_sc):
        @pl.when(pl.program_id(0) == 0)
        def _():
            s_sc[...] = jnp.zeros_like(s_sc)

        cos_t, sin_t = cos_ref[...], sin_ref[...]
        heads = range(nh)
        sls = [slice(h * HEAD_DIM, (h + 1) * HEAD_DIM) for h in heads]
        states = [s_sc[h] for h in heads]
        for h in heads:
            st_ref[0, h] = states[h]
        qs = [_rot(q_ref[:, sl], cos_t, sin_t) for sl in sls]
        ks = [_rot(k_ref[:, sl], cos_t, sin_t) * kscale for sl in sls]
        outs, new_states = _ret_heads(states, qs, ks, [v_ref[:, sl] for sl in sls], [dec_ref[h] for h in heads],
                                      [xi_ref[h] for h in heads], [zeta_ref[h] for h in heads],
                                      [cd_ref[h][0:1, :] for h in heads])
        for h in heads:
            o_ref[:, sls[h]] = outs[h]
            s_sc[h] = new_states[h]

    cols, st, act = _scan_specs(nh, nc, False, 3)
    return pl.pallas_call(
        body, name=name, grid=(nc,), in_specs=cols + _ret_table_specs(nh, nc, False), out_specs=[act, st],
        out_shape=[jax.ShapeDtypeStruct((tp, nh * HEAD_DIM), F32),
                   jax.ShapeDtypeStruct((nc, nh, HEAD_DIM, HEAD_DIM), F32)],
        scratch_shapes=[pltpu.VMEM((nh, HEAD_DIM, HEAD_DIM), F32)],
        compiler_params=_params("arbitrary"))(proj, proj, proj, cos, sin, dec, xi, zeta, cd)


def _ret_bwd(proj, cos, sin, dec, xi, zeta, cd, states, do, nh, name):
    tp = proj.shape[0]
    nc = tp // CHUNK
    kscale = HEAD_DIM ** -0.5

    def body(q_ref, k_ref, v_ref, cos_ref, sin_ref, dec_ref, xi_ref, zeta_ref, cd_ref, st_ref, do_ref,
             dq_ref, dk_ref, dv_ref, ds_sc):
        @pl.when(pl.program_id(0) == 0)
        def _():
            ds_sc[...] = jnp.zeros_like(ds_sc)

        cos_t, sin_t = cos_ref[...], sin_ref[...]
        heads = range(nh)
        sls = [slice(h * HEAD_DIM, (h + 1) * HEAD_DIM) for h in heads]
        qs = [_rot(q_ref[:, sl], cos_t, sin_t) for sl in sls]
        ks = [_rot(k_ref[:, sl], cos_t, sin_t) * kscale for sl in sls]
        fn = functools.partial(_ret_heads, dec=[dec_ref[h] for h in heads], xi=[xi_ref[h] for h in heads],
                               zeta=[zeta_ref[h] for h in heads], cd=[cd_ref[h][0:1, :] for h in heads])
        _, vjp = jax.vjp(fn, [st_ref[0, h] for h in heads], qs, ks, [v_ref[:, sl] for sl in sls])
        ds, dq, dk, dv = vjp(([do_ref[:, sl] for sl in sls], [ds_sc[h] for h in heads]))
        for h in heads:
            ds_sc[h] = ds[h]
            dq_ref[:, sls[h]] = _rot_t(dq[h], cos_t, sin_t).astype(BF16)
            dk_ref[:, sls[h]] = _rot_t(dk[h] * kscale, cos_t, sin_t).astype(BF16)
            dv_ref[:, sls[h]] = dv[h].astype(BF16)

    cols, st, act = _scan_specs(nh, nc, True, 3)
    return pl.pallas_call(
        body, name=name, grid=(nc,), in_specs=cols + _ret_table_specs(nh, nc, True) + [st, act],
        out_specs=[act, act, act],
        out_shape=[jax.ShapeDtypeStruct((tp, nh * HEAD_DIM), BF16)] * 3,
        scratch_shapes=[pltpu.VMEM((nh, HEAD_DIM, HEAD_DIM), F32)],
        compiler_params=_params("arbitrary"))(proj, proj, proj, cos, sin, dec, xi, zeta, cd, states, do)


def _gdn_out(o, z, gnorm):
    return o * lax.rsqrt(jnp.mean(o * o, axis=-1, keepdims=True) + EPS) * gnorm * _silu(z)


def _ret_out(o, rg, rnorm):
    mu = jnp.mean(o, axis=-1, keepdims=True)
    var = jnp.mean(jnp.square(o - mu), axis=-1, keepdims=True)
    return _silu(rg) * ((o - mu) * lax.rsqrt(var + EPS) * rnorm)


def _post_specs(tm, hv, d):
    row = lambda col: pl.BlockSpec((tm, hv), lambda i: (i, col))
    return dict(
        oa=row(0), ob=row(0), z=row(6), rg=row(7), ga=row(8), gb=row(9),
        gnorm=pl.BlockSpec((1, HEAD_DIM), lambda i: (0, 0)), rnorm=pl.BlockSpec((1, hv), lambda i: (0, 0)),
        w=pl.BlockSpec((hv, d), lambda i: (0, 0)), res=pl.BlockSpec((tm, d), lambda i: (i, 0)))


def _post_fwd(oa, ob, proj, gnorm, rnorm, wbg, wbr, wo, h1, name):
    tp, d = h1.shape
    hv = oa.shape[1]
    nh = hv // HEAD_DIM
    tm = _tile(tp, 256, 8)

    def body(oa_ref, ob_ref, z_ref, rg_ref, ga_ref, gb_ref, gn_ref, rn_ref, wbg_ref, wbr_ref, wo_ref, h_ref,
             o_ref, ya_sc, yb_sc):
        for h in range(nh):
            sl = slice(h * HEAD_DIM, (h + 1) * HEAD_DIM)
            ya_sc[:, sl] = _gdn_out(oa_ref[:, sl], z_ref[:, sl], gn_ref[...]).astype(BF16)
            yb_sc[:, sl] = _ret_out(ob_ref[:, sl], rg_ref[:, sl], rn_ref[:, sl]).astype(BF16)
        pa = jnp.dot(ya_sc[...], wbg_ref[...], preferred_element_type=F32)
        pb = jnp.dot(yb_sc[...], wbr_ref[...], preferred_element_type=F32)
        merged = jax.nn.sigmoid(ga_ref[...]) * pa + jax.nn.sigmoid(gb_ref[...]) * pb
        o_ref[...] = h_ref[...] + jnp.dot(merged.astype(BF16), wo_ref[...], preferred_element_type=F32)

    sp = _post_specs(tm, hv, d)
    return pl.pallas_call(
        body, name=name, grid=(tp // tm,),
        in_specs=[sp["oa"], sp["ob"], sp["z"], sp["rg"], sp["ga"], sp["gb"], sp["gnorm"], sp["rnorm"],
                  sp["w"], sp["w"], sp["w"], sp["res"]],
        out_specs=sp["res"], out_shape=jax.ShapeDtypeStruct((tp, d), F32),
        scratch_shapes=[pltpu.VMEM((tm, hv), BF16), pltpu.VMEM((tm, hv), BF16)],
        compiler_params=_params("parallel"))(oa, ob, proj, proj, proj, proj, gnorm, rnorm, wbg, wbr, wo, h1)


def _post_bwd(oa, ob, proj, gnorm, rnorm, wbg, wbr, wo, dh2, name):
    tp, d = dh2.shape
    hv = oa.shape[1]
    nh = hv // HEAD_DIM
    tm = _tile(tp, 256, 8)

    def body(oa_ref, ob_ref, z_ref, rg_ref, ga_ref, gb_ref, gn_ref, rn_ref, wbg_ref, wbr_ref, wo_ref, dh_ref,
             doa_ref, dob_ref, dg_ref, ya_ref, yb_ref, mg_ref, dpa_ref, dpb_ref, dgn_ref, drn_ref,
             dya_sc, dyb_sc):
        @pl.when(pl.program_id(0) == 0)
        def _():
            dgn_ref[...] = jnp.zeros_like(dgn_ref)
            drn_ref[...] = jnp.zeros_like(drn_ref)

        for h in range(nh):
            sl = slice(h * HEAD_DIM, (h + 1) * HEAD_DIM)
            ya_ref[:, sl] = _gdn_out(oa_ref[:, sl], z_ref[:, sl], gn_ref[...]).astype(BF16)
            yb_ref[:, sl] = _ret_out(ob_ref[:, sl], rg_ref[:, sl], rn_ref[:, sl]).astype(BF16)
        pa = jnp.dot(ya_ref[...], wbg_ref[...], preferred_element_type=F32)
        pb = jnp.dot(yb_ref[...], wbr_ref[...], preferred_element_type=F32)
        sa = jax.nn.sigmoid(ga_ref[...])
        sb = jax.nn.sigmoid(gb_ref[...])
        mg_ref[...] = (sa * pa + sb * pb).astype(BF16)
        dm = lax.dot_general(dh_ref[...].astype(BF16), wo_ref[...], NT, preferred_element_type=F32)
        dpa = (dm * sa).astype(BF16)
        dpb = (dm * sb).astype(BF16)
        dpa_ref[...] = dpa
        dpb_ref[...] = dpb
        dg_ref[:, 2 * hv:3 * hv] = (dm * pa * sa * (1.0 - sa)).astype(BF16)
        dg_ref[:, 3 * hv:4 * hv] = (dm * pb * sb * (1.0 - sb)).astype(BF16)
        dya_sc[...] = lax.dot_general(dpa, wbg_ref[...], NT, preferred_element_type=F32)
        dyb_sc[...] = lax.dot_general(dpb, wbr_ref[...], NT, preferred_element_type=F32)
        for h in range(nh):
            sl = slice(h * HEAD_DIM, (h + 1) * HEAD_DIM)
            _, vjp_a = jax.vjp(_gdn_out, oa_ref[:, sl], z_ref[:, sl], gn_ref[...])
            doa, dz, dgn = vjp_a(dya_sc[:, sl])
            doa_ref[:, sl] = doa
            dg_ref[:, sl] = dz.astype(BF16)
            dgn_ref[...] += dgn
            _, vjp_b = jax.vjp(_ret_out, ob_ref[:, sl], rg_ref[:, sl], rn_ref[:, sl])
            dob, drg, drn = vjp_b(dyb_sc[:, sl])
            dob_ref[:, sl] = dob
            dg_ref[:, hv + h * HEAD_DIM:hv + (h + 1) * HEAD_DIM] = drg.astype(BF16)
            drn_ref[:, sl] += drn

    sp = _post_specs(tm, hv, d)
    act = pl.BlockSpec((tm, hv), lambda i: (i, 0))
    return pl.pallas_call(
        body, name=name, grid=(tp // tm,),
        in_specs=[sp["oa"], sp["ob"], sp["z"], sp["rg"], sp["ga"], sp["gb"], sp["gnorm"], sp["rnorm"],
                  sp["w"], sp["w"], sp["w"], sp["res"]],
        out_specs=[act, act, pl.BlockSpec((tm, 4 * hv), lambda i: (i, 0)), act, act, sp["res"], sp["res"],
                   sp["res"], sp["gnorm"], sp["rnorm"]],
        out_shape=[jax.ShapeDtypeStruct((tp, hv), F32), jax.ShapeDtypeStruct((tp, hv), F32),
                   jax.ShapeDtypeStruct((tp, 4 * hv), BF16), jax.ShapeDtypeStruct((tp, hv), BF16),
                   jax.ShapeDtypeStruct((tp, hv), BF16), jax.ShapeDtypeStruct((tp, d), BF16),
                   jax.ShapeDtypeStruct((tp, d), BF16), jax.ShapeDtypeStruct((tp, d), BF16),
                   jax.ShapeDtypeStruct((1, HEAD_DIM), F32), jax.ShapeDtypeStruct((1, hv), F32)],
        scratch_shapes=[pltpu.VMEM((tm, hv), F32), pltpu.VMEM((tm, hv), F32)],
        compiler_params=_params("arbitrary"))(oa, ob, proj, proj, proj, proj, gnorm, rnorm, wbg, wbr, wo, dh2)


def _final(h3, gain, target, name):
    tp, d = h3.shape
    tm = HEAD_ROWS

    def body(h_ref, g_ref, t_ref, loss_ref, dh_ref, dgain_ref):
        i = pl.program_id(0)

        @pl.when(i == 0)
        def _():
            loss_ref[...] = jnp.zeros_like(loss_ref)
            dgain_ref[...] = jnp.zeros_like(dgain_ref)

        xh, r = _rms_parts(h_ref[...])
        err = jnp.where(i == 0, 0.0, xh * g_ref[...] - t_ref[...])
        dx, dg = _rms_bwd(err * (1.0 / d), xh, r, g_ref[...])
        dh_ref[...] = dx
        dgain_ref[...] += dg
        loss_ref[...] += 0.5 * jnp.sum(jnp.mean(err * err, axis=-1, keepdims=True), axis=0, keepdims=True)

    row = pl.BlockSpec((tm, d), lambda i: (i, 0))
    vec = pl.BlockSpec((1, d), lambda i: (0, 0))
    return pl.pallas_call(
        body, name=name, grid=(tp // tm,),
        in_specs=[row, vec, pl.BlockSpec((tm, d), lambda i: (jnp.maximum(i - 1, 0), 0))],
        out_specs=[pl.BlockSpec((1, LANES), lambda i: (0, 0)), row, vec],
        out_shape=[jax.ShapeDtypeStruct((1, LANES), F32), jax.ShapeDtypeStruct((tp, d), F32),
                   jax.ShapeDtypeStruct((1, d), F32)],
        compiler_params=_params("arbitrary"))(h3, gain, target)


def _peer(k):
    x, y, c = lax.axis_index("x"), lax.axis_index("y"), lax.axis_index("c")
    return (1 - x if k & 4 else x, 1 - y if k & 2 else y, 1 - c if k & 1 else c)


def _my_index():
    return 4 * lax.axis_index("x") + 2 * lax.axis_index("y") + lax.axis_index("c")


def _exchange(bufs, scatter, name):
    n = len(bufs)

    def body(*refs):
        _exchange_copies(refs[:n], refs[n:2 * n], refs[2 * n:], scatter, True, True)

    hbm, out_shape, sems = _exchange_refs(bufs)
    return pl.pallas_call(
        body, name=name, in_specs=hbm, out_specs=hbm, out_shape=out_shape, scratch_shapes=sems,
        compiler_params=pltpu.CompilerParams(has_side_effects=True))(*bufs)


def _gather_via_sibling(bufs, name):
    n = len(bufs)

    def body(*refs):
        x_refs, out_refs = refs[:n], refs[n:2 * n]
        send_sems, recv_sems, local_sems = refs[2 * n:]
        x, y, c = lax.axis_index("x"), lax.axis_index("y"), lax.axis_index("c")
        me, sibling = (x, y, c), (x, y, 1 - c)
        chips = [(1 - x, y), (x, 1 - y), (1 - x, 1 - y)]
        rows = lambda a, dev: out_refs[a].at[4 * dev[0] + 2 * dev[1] + dev[2]]

        def copy(k, a, block, to, src=None):
            return pltpu.make_async_remote_copy(
                src_ref=rows(a, block) if src is None else src, dst_ref=rows(a, block),
                send_sem=send_sems.at[k * n + a], recv_sem=recv_sems.at[k * n + a],
                device_id=to, device_id_type=pl.DeviceIdType.MESH)

        mine = [pltpu.make_async_copy(x_refs[a], rows(a, me), local_sems.at[a]) for a in range(n)]
        first = [copy(0, a, me, sibling, src=x_refs[a]) for a in range(n)]
        first += [copy(1 + j, a, me, (*chip, c), src=x_refs[a]) for j, chip in enumerate(chips) for a in range(n)]
        for cp in mine + first:
            cp.start()
        passed = []
        for j, chip in enumerate(chips):
            for a in range(n):
                copy(1 + j, a, (*chip, c), me).wait_recv()
                passed.append(copy(4 + j, a, (*chip, c), sibling))
                passed[-1].start()
        for a in range(n):
            copy(0, a, sibling, me).wait_recv()
        for j, chip in enumerate(chips):
            for a in range(n):
                copy(4 + j, a, (*chip, 1 - c), me).wait_recv()
        for cp in first + passed:
            cp.wait_send()
        for cp in mine:
            cp.wait()

    hbm, out_shape, sems = _exchange_refs(bufs)
    return pl.pallas_call(
        body, name=name, in_specs=hbm, out_specs=hbm, out_shape=out_shape, scratch_shapes=sems,
        compiler_params=pltpu.CompilerParams(has_side_effects=True))(*bufs)


def _exchange_refs(bufs):
    n = len(bufs)
    return ([pl.BlockSpec(memory_space=pl.ANY)] * n,
            [jax.ShapeDtypeStruct((N_DEV,) + b.shape[-2:], b.dtype) for b in bufs],
            [pltpu.SemaphoreType.DMA(((N_DEV - 1) * n,)), pltpu.SemaphoreType.DMA(((N_DEV - 1) * n,)),
             pltpu.SemaphoreType.DMA((n,))])


def _exchange_copies(x_refs, out_refs, sems, scatter, start, wait):
    n = len(x_refs)
    send_sems, recv_sems, local_sems = sems
    me = _my_index()
    copies = []
    for a in range(n):
        copies.append(pltpu.make_async_copy(x_refs[a].at[me] if scatter else x_refs[a], out_refs[a].at[me],
                                            local_sems.at[a]))
    sends = []
    arrivals = []
    for k in range(1, N_DEV):
        x, y, c = _peer(k)
        peer = 4 * x + 2 * y + c
        for a in range(n):
            sem = (k - 1) * n + a
            sends.append(pltpu.make_async_remote_copy(
                src_ref=x_refs[a].at[peer] if scatter else x_refs[a], dst_ref=out_refs[a].at[me],
                send_sem=send_sems.at[sem], recv_sem=recv_sems.at[sem],
                device_id=(x, y, c), device_id_type=pl.DeviceIdType.MESH))
            landed = out_refs[a].at[peer]
            arrivals.append(pltpu.make_async_remote_copy(
                src_ref=landed, dst_ref=landed, send_sem=send_sems.at[sem], recv_sem=recv_sems.at[sem],
                device_id=(x, y, c), device_id_type=pl.DeviceIdType.MESH))
    if start:
        for cp in copies + sends:
            cp.start()
    if wait:
        for cp in arrivals:
            cp.wait_recv()
        for cp in sends:
            cp.wait_send()
        for cp in copies:
            cp.wait()


def _carried_call(body, carry, first, last, *, name, grid, in_specs, out_specs, out_shape, scratch_shapes=()):
    in_specs, out_specs, out_shape = list(in_specs), list(out_specs), list(out_shape)
    semantics = ("arbitrary",) * len(grid)
    if carry is None:
        call = pl.pallas_call(body, name=name, grid=grid, in_specs=in_specs, out_specs=out_specs,
                              out_shape=out_shape, scratch_shapes=list(scratch_shapes),
                              compiler_params=_params(*semantics))
        return lambda *args: (call(*args), [])
    bufs, scatter = carry
    n, n_in, n_out, n_scratch = len(bufs), len(in_specs), len(out_specs), len(scratch_shapes)
    hbm, x_shapes, sems = _exchange_refs(bufs)

    def full_body(*refs):
        ins, x_refs = refs[:n_in], refs[n_in:n_in + n]
        outs, xo_refs = refs[n_in + n:n_in + n + n_out], refs[n_in + n + n_out:n_in + 2 * n + n_out]
        scratch = refs[n_in + 2 * n + n_out:n_in + 2 * n + n_out + n_scratch]
        x_sems = refs[n_in + 2 * n + n_out + n_scratch:]

        @pl.when(first())
        def _():
            _exchange_copies(x_refs, xo_refs, x_sems, scatter, True, False)

        body(*ins, *outs, *scratch)

        @pl.when(last())
        def _():
            _exchange_copies(x_refs, xo_refs, x_sems, scatter, False, True)

    call = pl.pallas_call(full_body, name=name, grid=grid, in_specs=in_specs + hbm, out_specs=out_specs + hbm,
                          out_shape=out_shape + x_shapes, scratch_shapes=list(scratch_shapes) + sems,
                          compiler_params=_params(*semantics))

    def run(*args):
        res = call(*args, *bufs)
        return res[:n_out], res[n_out:]
    return run


def _adamw(w, g, m, v, name):
    r, c = w.shape
    parts = g.ndim == 3
    tr = _tile(r, 256, 16 if parts else 8)
    c1 = 1.0 - ADAM_B1 ** ADAM_STEP
    c2 = 1.0 - ADAM_B2 ** ADAM_STEP

    def body(w_ref, g_ref, m_ref, v_ref, go_ref, d_ref, mo_ref, vo_ref):
        if parts:
            g = g_ref[0].astype(F32)
            for q in range(1, N_DEV):
                g = g + g_ref[q].astype(F32)
        else:
            g = g_ref[...]
        m = ADAM_B1 * m_ref[...] + (1.0 - ADAM_B1) * g
        v = ADAM_B2 * v_ref[...] + (1.0 - ADAM_B2) * (g * g)
        go_ref[...] = g
        d_ref[...] = -ADAM_LR * ((m / c1) / (jnp.sqrt(v / c2) + ADAM_EPS) + ADAM_WD * w_ref[...])
        mo_ref[...] = m
        vo_ref[...] = v

    blk = pl.BlockSpec((tr, c), lambda i: (i, 0))
    g_spec = pl.BlockSpec((N_DEV, tr, c), lambda i: (0, i, 0)) if parts else blk
    return pl.pallas_call(
        body, name=name, grid=(r // tr,), in_specs=[blk, g_spec, blk, blk], out_specs=[blk] * 4,
        out_shape=[jax.ShapeDtypeStruct((r, c), F32)] * 4,
        compiler_params=_params("parallel"))(w, g, m, v)


def _win_segments(hv, nh):
    o_z, o_b = 3 * hv, 4 * hv
    o_r = o_b + 2 * nh
    return [(0, 0, 3 * hv), (3 * hv, o_r, 3 * hv), (6 * hv, o_z, hv), (7 * hv, o_r + 3 * hv, 3 * hv),
            (10 * hv, o_b, 2 * nh)]


def _win_from_shards(shards, hv, nh):
    _, d, cs = shards.shape
    pieces = []
    for _, src, width in _win_segments(hv, nh):
        lo = src
        while lo < src + width:
            p = lo // cs
            hi = min(src + width, (p + 1) * cs)
            pieces.append(shards[p][:, lo - p * cs:hi - p * cs])
            lo = hi
    pieces.append(jnp.zeros((d, LANES - 2 * nh), shards.dtype))
    return jnp.concatenate(pieces, axis=1)


def _win_grad_to_shards(g, hv, nh, cs):
    segments = _win_segments(hv, nh)
    shards = []
    for p in range(N_DEV):
        pieces = []
        lo = p * cs
        while lo < (p + 1) * cs:
            here, src, width = next(s for s in segments if s[1] <= lo < s[1] + s[2])
            hi = min((p + 1) * cs, src + width)
            pieces.append(g[:, here + lo - src:here + hi - src])
            lo = hi
        shards.append(jnp.concatenate(pieces, axis=1))
    return jnp.stack(shards)


def _rope_tables(tp):
    pos = jnp.arange(tp, dtype=F32) - float(PAD_FRONT)
    inv = 1.0 / (ROPE_BASE ** jnp.linspace(0.0, 1.0, HEAD_DIM // 2, dtype=F32))
    ang = pos[:, None] * inv[None, :]
    cos = jnp.repeat(jnp.cos(ang), 2, axis=1)
    sin = jnp.repeat(jnp.sin(ang), 2, axis=1) * jnp.tile(jnp.array([-1.0, 1.0], F32), HEAD_DIM // 2)[None, :]
    return cos, sin


def _retention_tables(nh):
    log_gamma = jnp.log1p(-jnp.exp2(-5.0 - jnp.arange(nh, dtype=F32)))
    pos = jnp.arange(CHUNK, dtype=F32)
    causal = pos[:, None] >= pos[None, :]
    diff = pos[:, None] - pos[None, :]
    dec = jnp.where(causal, jnp.exp(jnp.where(causal, diff, 0.0) * log_gamma[:, None, None]), 0.0)
    ones = jnp.ones((1, 1, HEAD_DIM), F32)
    xi = jnp.exp((pos + 1.0)[None, :] * log_gamma[:, None])[:, :, None] * ones
    zeta = jnp.exp((CHUNK - 1.0 - pos)[None, :] * log_gamma[:, None])[:, :, None] * ones
    cd = jnp.exp(CHUNK * log_gamma)[:, None, None] * jnp.ones((1, 8, HEAD_DIM), F32)
    return dec, xi, zeta, cd


SHARDED = ("meta_tokens", "ffn1_w_in", "ffn1_w_out", "w_in", "gdn_conv_w", "w_branch_gdn", "w_branch_ret",
           "w_out", "ffn2_w_in", "ffn2_w_out")
COLUMN_SHARDED = ("meta_tokens", "ffn1_w_in", "w_in", "gdn_conv_w", "ffn2_w_in")
EXACT_F32 = ("meta_tokens", "gdn_conv_w")
REPLICATED = ("ffn1_norm", "mix_norm", "gdn_a_log", "gdn_dt_bias", "gdn_out_norm", "ret_out_norm", "ffn2_norm",
              "final_norm")
WEIGHTS = ("meta_tokens", "ffn1_norm", "ffn1_w_in", "ffn1_w_out", "mix_norm", "w_in", "gdn_conv_w", "gdn_a_log",
           "gdn_dt_bias", "gdn_out_norm", "ret_out_norm", "w_branch_gdn", "w_branch_ret", "w_out", "ffn2_norm",
           "ffn2_w_in", "ffn2_w_out", "final_norm")


def _as2d(a):
    if a.ndim == 3:
        return a[0]
    if a.ndim == 1:
        return a[None, :]
    return a


def _rows_of(shards):
    return shards.reshape(-1, shards.shape[2])


def _cols_of(shards):
    return shards.transpose(1, 0, 2).reshape(shards.shape[1], -1)


def _row_shards(a):
    return a.reshape(N_DEV, -1, a.shape[1])


def _col_shards(a):
    return a.reshape(a.shape[0], N_DEV, -1).transpose(1, 0, 2)


GATHER_FIRST = ("meta_tokens", "ffn1_w_in", "ffn1_w_out")
GATHER_BEHIND_FFN1 = ("w_in", "gdn_conv_w")
GATHER_BEHIND_PROJ = ("w_branch_gdn", "w_branch_ret", "w_out", "ffn2_w_in", "ffn2_w_out")
SCATTER_BEHIND_DN2 = ("ffn2_w_in", "ffn2_w_out", "w_branch_gdn", "w_branch_ret", "w_out")
SCATTER_BEHIND_FFN1 = ("w_in", "gdn_conv_w")
SCATTER_BEHIND_DWG = ("meta_tokens", "ffn1_w_out")
SCATTER_LAST = ("ffn1_w_in",)


def _device_step(x, target, send, rep):
    seq, d = x.shape
    tp = HEAD_ROWS + seq
    hv = d
    nh = hv // HEAD_DIM
    assert seq % CHUNK == 0 and tp % HEAD_ROWS == 0
    bf16_shards = lambda grads, names: [grads[n].astype(BF16) for n in names]

    pad_lanes = lambda row: jnp.pad(row, ((0, 0), (nh, LANES - 2 * nh)))
    alog = pad_lanes(rep["gdn_a_log"])
    dtb = pad_lanes(rep["gdn_dt_bias"])
    cos, sin = _rope_tables(tp)
    dec, xi, zeta, cd = _retention_tables(nh)

    got = dict(zip(GATHER_FIRST, _gather_via_sibling([send[n] for n in GATHER_FIRST], "gather_ffn1")))
    h0 = jnp.concatenate([jnp.zeros((PAD_FRONT, d), F32), _cols_of(got["meta_tokens"]), x], axis=0)
    f1i, f1o = got["ffn1_w_in"], _rows_of(got["ffn1_w_out"])
    (h1, ag1, au1), moved = _ffn_fwd(h0, rep["ffn1_norm"], f1i, f1o, "ffn1_fwd",
                                     ([send[n] for n in GATHER_BEHIND_FFN1], False))
    got.update(zip(GATHER_BEHIND_FFN1, moved))
    wp = _win_from_shards(got["w_in"], hv, nh)
    conv_w = _cols_of(got["gdn_conv_w"])
    (proj, n2), moved = _proj_fwd(h1, rep["mix_norm"], wp, "proj_fwd",
                                  ([send[n] for n in GATHER_BEHIND_PROJ], False))
    got.update(zip(GATHER_BEHIND_PROJ, moved))
    wbg, wbr, wo = _rows_of(got["w_branch_gdn"]), _rows_of(got["w_branch_ret"]), _rows_of(got["w_out"])
    f2i, f2o = got["ffn2_w_in"], _rows_of(got["ffn2_w_out"])
    qkv = _conv_fwd(proj, conv_w, hv, "conv_fwd")
    oa, s_gdn, t_gdn = _gdn_fwd(qkv, proj, alog, dtb, nh, "gdn_fwd")
    ob, s_ret = _ret_fwd(proj, cos, sin, dec, xi, zeta, cd, nh, "ret_fwd")
    h2 = _post_fwd(oa, ob, proj, rep["gdn_out_norm"], rep["ret_out_norm"], wbg, wbr, wo, h1, "post_fwd")
    (h3, ag2, au2), _ = _ffn_fwd(h2, rep["ffn2_norm"], f2i, f2o, "ffn2_fwd")
    loss_row, dh3, d_final = _final(h3, rep["final_norm"], target, "final")

    (dh2, d_f2n, n3, hid2, dag2, dau2), _ = _ffn_bwd(h2, dh3, rep["ffn2_norm"], f2i, f2o, ag2, au2, "ffn2_bwd")
    grads = {"ffn2_w_in": jnp.concatenate([_matmul_tn_blocks(n3, dag2, "ffn2_dwg"),
                                           _matmul_tn_blocks(n3, dau2, "ffn2_dwu")]),
             "ffn2_w_out": _row_shards(_matmul_tn_blocks(hid2, dh3, "ffn2_dwo", 0.5))}

    doa, dob, dgate, ya, yb, merged, dpa, dpb, d_gn, d_rn = _post_bwd(
        oa, ob, proj, rep["gdn_out_norm"], rep["ret_out_norm"], wbg, wbr, wo, dh2, "post_bwd")
    grads["w_branch_gdn"] = _row_shards(_matmul_tn(ya, dpa, "dw_branch_gdn"))
    grads["w_branch_ret"] = _row_shards(_matmul_tn(yb, dpb, "dw_branch_ret"))
    grads["w_out"] = _row_shards(_matmul_tn(merged, dh2, "dw_out"))

    d_ret = _ret_bwd(proj, cos, sin, dec, xi, zeta, cd, s_ret, dob, nh, "ret_bwd")
    gdn_grads = _gdn_bwd(qkv, proj, alog, dtb, s_gdn, t_gdn, doa, nh, "gdn_bwd")
    dba, d_alog, d_dtb = gdn_grads[3:]
    dpre, g_conv = [], []
    for grp, tag in enumerate("qkv"):
        dx, dw = _conv_bwd(proj, conv_w, gdn_grads[grp], grp, hv, "conv_bwd_" + tag)
        dpre.append(dx)
        g_conv.append(dw)
    grads["gdn_conv_w"] = _col_shards(jnp.concatenate(g_conv, axis=1))

    wide = dpre + list(d_ret) + [dgate]
    dn2, moved = _matmul_nt_parts(wide, wp[:, :10 * hv], None, "dn2_wide",
                                  (bf16_shards(grads, SCATTER_BEHIND_DN2), True))
    parts = dict(zip(SCATTER_BEHIND_DN2, moved))
    dn2, _ = _matmul_nt_parts([dba], wp[:, 10 * hv:], dn2, "dn2_beta_alpha")
    g_wp = [_matmul_tn(n2, dg, "dw_in_%d" % idx) for idx, dg in enumerate(wide + [dba])]
    grads["w_in"] = _win_grad_to_shards(jnp.concatenate(g_wp, axis=1), hv, nh, send["w_in"].shape[1])
    dh1, d_mixn = _norm_bwd(h1, rep["mix_norm"], dn2, dh2, "mix_norm_bwd")

    (dh0, d_f1n, n1, hid1, dag1, dau1), moved = _ffn_bwd(h0, dh1, rep["ffn1_norm"], f1i, f1o, ag1, au1, "ffn1_bwd",
                                                         (bf16_shards(grads, SCATTER_BEHIND_FFN1), True))
    parts.update(zip(SCATTER_BEHIND_FFN1, moved))
    grads["ffn1_w_out"] = _row_shards(_matmul_tn_blocks(hid1, dh1, "ffn1_dwo", 0.5))
    grads["meta_tokens"] = _col_shards(dh0[PAD_FRONT:HEAD_ROWS])
    g_gate, moved = _matmul_tn_blocks(n1, dag1, "ffn1_dwg", carry=(bf16_shards(grads, SCATTER_BEHIND_DWG), True))
    parts.update(zip(SCATTER_BEHIND_DWG, moved))
    grads["ffn1_w_in"] = jnp.concatenate([g_gate, _matmul_tn_blocks(n1, dau1, "ffn1_dwu")])
    parts.update(zip(SCATTER_LAST, _exchange(bf16_shards(grads, SCATTER_LAST), True, "scatter_ffn1")))

    small = {"ffn1_norm": d_f1n, "mix_norm": d_mixn, "gdn_a_log": d_alog[:, nh:2 * nh],
             "gdn_dt_bias": d_dtb[:, nh:2 * nh], "gdn_out_norm": d_gn, "ret_out_norm": d_rn, "ffn2_norm": d_f2n,
             "final_norm": d_final}
    return loss_row[0, 0], dh0[HEAD_ROWS:], parts, small


def kernel(x, meta_tokens, ffn1_norm, ffn1_w_in, ffn1_w_out, mix_norm, w_in, gdn_conv_w, gdn_a_log, gdn_dt_bias, gdn_out_norm, ret_out_norm, w_branch_gdn, w_branch_ret, w_out, ffn2_norm, ffn2_w_in, ffn2_w_out, final_norm, loss_target, m_meta_tokens, m_ffn1_norm, m_ffn1_w_in, m_ffn1_w_out, m_mix_norm, m_w_in, m_gdn_conv_w, m_gdn_a_log, m_gdn_dt_bias, m_gdn_out_norm, m_ret_out_norm, m_w_branch_gdn, m_w_branch_ret, m_w_out, m_ffn2_norm, m_ffn2_w_in, m_ffn2_w_out, m_final_norm, v_meta_tokens, v_ffn1_norm, v_ffn1_w_in, v_ffn1_w_out, v_mix_norm, v_w_in, v_gdn_conv_w, v_gdn_a_log, v_gdn_dt_bias, v_gdn_out_norm, v_ret_out_norm, v_w_branch_gdn, v_w_branch_ret, v_w_out, v_ffn2_norm, v_ffn2_w_in, v_ffn2_w_out, v_final_norm):
    given = dict(locals())
    params = {n: _as2d(given[n]) for n in WEIGHTS}
    local = {n: params[n] for n in SHARDED}
    rep = {n: params[n] for n in REPLICATED}

    send = {n: local[n] if n in EXACT_F32 else local[n].astype(BF16) for n in SHARDED}
    loss_sum, grad_x, parts, small = _device_step(x[0], loss_target[0], send, rep)
    parts.update(zip(REPLICATED, _exchange([small[n] for n in REPLICATED], False, "gather_small_grads")))
    loss = lax.psum(loss_sum, ("x", "y", "c"))

    outs = {}
    for n in WEIGHTS:
        res = _adamw(params[n], parts[n], _as2d(given["m_" + n]), _as2d(given["v_" + n]), "adamw_" + n)
        outs[n] = [r.reshape(given[n].shape) for r in res]
    return (loss, grad_x[None], *[outs[n][0] for n in WEIGHTS], *[outs[n][1] for n in WEIGHTS],
            *[outs[n][2] for n in WEIGHTS], *[outs[n][3] for n in WEIGHTS])
```

```python
import functools
import math

import numpy as np
import jax
import jax.numpy as jnp
from jax import lax
from jax.experimental import pallas as pl
from jax.experimental.pallas import tpu as pltpu

F32 = jnp.float32
BF16 = jnp.bfloat16

N_DEV = 8
N_META = 16
CHUNK = 64
HEAD_DIM = 128
CONV_K = 4
ROPE_BASE = 10000.0
EPS = 1e-6
PAD_FRONT = 240
HEAD_ROWS = PAD_FRONT + N_META
LANES = 128
VMEM_LIMIT_BYTES = 56 * 1024 * 1024

ADAM_LR = 0.001
ADAM_B1 = 0.9
ADAM_B2 = 0.999
ADAM_EPS = 1e-08
ADAM_WD = 0.01
ADAM_STEP = 10

NN = (((1,), (0,)), ((), ()))
NT = (((1,), (1,)), ((), ()))
TN = (((0,), (0,)), ((), ()))


def _tile(n, target, mult):
    best = 0
    for t in range(mult, min(n, target) + 1, mult):
        if n % t == 0:
            best = t
    return best if best else n


def _params(*semantics):
    return pltpu.CompilerParams(dimension_semantics=semantics, vmem_limit_bytes=VMEM_LIMIT_BYTES)


def _split(a, pieces):
    out = []
    for _ in range(pieces - 1):
        part = a.astype(BF16)
        out.append(part)
        a = a - part.astype(F32)
    return out + [a.astype(BF16)]


def _raw_dot(a, b, dims, hi):
    dot = lambda x, y: lax.dot_general(x, y, dims, preferred_element_type=F32)
    if hi:
        (a_hi, a_lo), (b_hi, b_lo) = _split(a, 2), _split(b, 2)
        return dot(a_hi, b_hi) + (dot(a_hi, b_lo) + dot(a_lo, b_hi))
    return dot(a.astype(BF16), b.astype(BF16))


def _mask_dot(mask, x, dims):
    mask = mask.astype(BF16)
    hi, mid, lo = [lax.dot_general(mask, p, dims, preferred_element_type=F32) for p in _split(x, 3)]
    return hi + (mid + lo)


@jax.custom_vjp
def _cumsum_rows(x):
    c = x.shape[0]
    tril = lax.broadcasted_iota(jnp.int32, (c, c), 0) >= lax.broadcasted_iota(jnp.int32, (c, c), 1)
    return _mask_dot(tril, x, NN)


def _cumsum_rows_bwd(_, g):
    c = g.shape[0]
    tril = lax.broadcasted_iota(jnp.int32, (c, c), 0) >= lax.broadcasted_iota(jnp.int32, (c, c), 1)
    return (_mask_dot(tril, g, TN),)


_cumsum_rows.defvjp(lambda x: (_cumsum_rows(x), None), _cumsum_rows_bwd)


def _unit_lower_inverses(xs):
    c = xs[0].shape[0]
    eye = (lax.broadcasted_iota(jnp.int32, (c, c), 0) == lax.broadcasted_iota(jnp.int32, (c, c), 1)).astype(F32)
    t_inv = [eye + x for x in xs]
    for _ in range(int(math.log2(c)) - 1):
        xs = [_raw_dot(x, x, NN, True) for x in xs]
        t_inv = [t + _raw_dot(t, x, NN, True) for t, x in zip(t_inv, xs)]
    return t_inv


@jax.custom_vjp
def _known_inverse(x_neg, t_inv):
    return t_inv


_known_inverse.defvjp(
    lambda x_neg, t_inv: (t_inv, t_inv),
    lambda t_inv, g: (_raw_dot(_raw_dot(t_inv, g, TN, True), t_inv, NT, True), jnp.zeros_like(t_inv)))


def _make_mm(hi):
    @jax.custom_vjp
    def nn(a, b):
        return _raw_dot(a, b, NN, hi)

    @jax.custom_vjp
    def nt(a, b):
        return _raw_dot(a, b, NT, hi)

    @jax.custom_vjp
    def tn(a, b):
        return _raw_dot(a, b, TN, hi)

    nn.defvjp(lambda a, b: (_raw_dot(a, b, NN, hi), (a, b)),
              lambda r, g: (_raw_dot(g, r[1], NT, hi), _raw_dot(r[0], g, TN, hi)))
    nt.defvjp(lambda a, b: (_raw_dot(a, b, NT, hi), (a, b)),
              lambda r, g: (_raw_dot(g, r[1], NN, hi), _raw_dot(g, r[0], TN, hi)))
    tn.defvjp(lambda a, b: (_raw_dot(a, b, TN, hi), (a, b)),
              lambda r, g: (_raw_dot(r[1], g, NT, hi), _raw_dot(r[0], g, NN, hi)))
    return nn, nt, tn


def _silu(x):
    return x * jax.nn.sigmoid(x)


def _rms_parts(x):
    r = lax.rsqrt(jnp.mean(x * x, axis=-1, keepdims=True) + EPS)
    return x * r, r


def _rms_bwd(dy, xh, r, gain):
    dxh = dy * gain
    dx = r * (dxh - xh * jnp.mean(dxh * xh, axis=-1, keepdims=True))
    return dx, jnp.sum(dy * xh, axis=0, keepdims=True)


def _ffn_specs(tm, d, tf, nj):
    return [pl.BlockSpec((tm, d), lambda i, j: (i, 0)), pl.BlockSpec((1, d), lambda i, j: (0, 0)),
            pl.BlockSpec((1, d, tf), lambda i, j: (j, 0, 0)), pl.BlockSpec((1, d, tf), lambda i, j: (nj + j, 0, 0)),
            pl.BlockSpec((tf, d), lambda i, j: (j, 0))]


def _first_step(ndim):
    return lambda: functools.reduce(lambda a, b: a & b, [pl.program_id(k) == 0 for k in range(ndim)])


def _last_step(grid):
    return lambda: functools.reduce(lambda a, b: a & b, [pl.program_id(k) == g - 1 for k, g in enumerate(grid)])


def _ffn_fwd(h, gain, w_in, wo, name, carry=None):
    tp, d = h.shape
    tf = w_in.shape[2]
    nj = w_in.shape[0] // 2
    tm = _tile(tp, 768, 8)
    row, vec, wg_spec, wu_spec, wo_spec = _ffn_specs(tm, d, tf, nj)

    def body(h_ref, g_ref, wg3_ref, wu3_ref, wo_ref, o_ref, ag3_ref, au3_ref, n_sc, acc_sc):
        wg_ref, wu_ref = wg3_ref.at[0], wu3_ref.at[0]
        j = pl.program_id(1)

        @pl.when(j == 0)
        def _():
            xh, _ = _rms_parts(h_ref[...])
            n_sc[...] = (xh * g_ref[...]).astype(BF16)
            acc_sc[...] = jnp.zeros_like(acc_sc)

        n = n_sc[...]
        a_g = jnp.dot(n, wg_ref[...], preferred_element_type=F32)
        a_u = jnp.dot(n, wu_ref[...], preferred_element_type=F32)
        ag3_ref[0] = a_g
        au3_ref[0] = a_u
        hid = (_silu(a_g) * a_u).astype(BF16)
        acc_sc[...] += jnp.dot(hid, wo_ref[...], preferred_element_type=F32)

        @pl.when(j == nj - 1)
        def _():
            o_ref[...] = h_ref[...] + 0.5 * acc_sc[...]

    grid = (tp // tm, nj)
    act = pl.BlockSpec((1, tm, tf), lambda i, j: (j, i, 0))
    return _carried_call(
        body, carry, _first_step(2), _last_step(grid), name=name, grid=grid,
        in_specs=[row, vec, wg_spec, wu_spec, wo_spec], out_specs=[row, act, act],
        out_shape=[jax.ShapeDtypeStruct((tp, d), F32)] + [jax.ShapeDtypeStruct((nj, tp, tf), F32)] * 2,
        scratch_shapes=[pltpu.VMEM((tm, d), BF16), pltpu.VMEM((tm, d), F32)])(h, gain, w_in, w_in, wo)


def _ffn_bwd(h, dho, gain, w_in, wo, ag3, au3, name, carry=None):
    tp, d = h.shape
    tf = w_in.shape[2]
    nj = w_in.shape[0] // 2
    tm = _tile(tp, 528, 16)
    ni = tp // tm
    row, vec, wg_spec, wu_spec, wo_spec = _ffn_specs(tm, d, tf, nj)

    def body(h_ref, dho_ref, g_ref, wg3_ref, wu3_ref, wo_ref, ag3_ref, au3_ref,
             dh_ref, dgain_ref, n_ref, hid3_ref, dag3_ref, dau3_ref, dn_sc, dhb_sc):
        wg_ref, wu_ref = wg3_ref.at[0], wu3_ref.at[0]
        hid_ref, dag_ref, dau_ref = hid3_ref.at[0], dag3_ref.at[0], dau3_ref.at[0]
        i, j = pl.program_id(0), pl.program_id(1)

        @pl.when(j == 0)
        def _():
            xh, _ = _rms_parts(h_ref[...])
            n_ref[...] = (xh * g_ref[...]).astype(BF16)
            dn_sc[...] = jnp.zeros_like(dn_sc)
            dhb_sc[...] = (0.5 * dho_ref[...]).astype(BF16)

        @pl.when((i == 0) & (j == 0))
        def _():
            dgain_ref[...] = jnp.zeros_like(dgain_ref)

        a_g = ag3_ref[0]
        a_u = au3_ref[0]
        sg = jax.nn.sigmoid(a_g)
        s = a_g * sg
        hid_ref[...] = (s * a_u).astype(BF16)
        d_hid = lax.dot_general(dhb_sc[...], wo_ref[...], NT, preferred_element_type=F32)
        d_au = (d_hid * s).astype(BF16)
        d_ag = (d_hid * a_u * (sg * (1.0 + a_g * (1.0 - sg)))).astype(BF16)
        dau_ref[...] = d_au
        dag_ref[...] = d_ag
        dn_sc[...] += (lax.dot_general(d_ag, wg_ref[...], NT, preferred_element_type=F32)
                       + lax.dot_general(d_au, wu_ref[...], NT, preferred_element_type=F32))

        @pl.when(j == nj - 1)
        def _():
            xh, r = _rms_parts(h_ref[...])
            dx, dg = _rms_bwd(dn_sc[...], xh, r, g_ref[...])
            dh_ref[...] = dho_ref[...] + dx
            dgain_ref[...] += dg

    act = pl.BlockSpec((1, tm, tf), lambda i, j: (j, i, 0))
    return _carried_call(
        body, carry, _first_step(2), _last_step((ni, nj)), name=name, grid=(ni, nj),
        in_specs=[row, row, vec, wg_spec, wu_spec, wo_spec, act, act],
        out_specs=[row, vec, row, act, act, act],
        out_shape=[jax.ShapeDtypeStruct((tp, d), F32), jax.ShapeDtypeStruct((1, d), F32),
                   jax.ShapeDtypeStruct((tp, d), BF16)] + [jax.ShapeDtypeStruct((nj, tp, tf), BF16)] * 3,
        scratch_shapes=[pltpu.VMEM((tm, d), F32), pltpu.VMEM((tm, d), BF16)])(
            h, dho, gain, w_in, w_in, wo, ag3, au3)


def _matmul_tn(a, b, name, scale=1.0):
    t, m = a.shape
    n = b.shape[1]
    bm = _tile(m, 1024, LANES)
    bn = _tile(n, 1536, LANES)
    tk = _tile(t, 1408, 16)
    nk = t // tk

    def body(a_ref, b_ref, o_ref):
        k = pl.program_id(2)

        @pl.when(k == 0)
        def _():
            o_ref[...] = jnp.zeros_like(o_ref)

        o_ref[...] += lax.dot_general(a_ref[...].astype(BF16), b_ref[...].astype(BF16), TN,
                                      preferred_element_type=F32)

        if scale != 1.0:
            @pl.when(k == nk - 1)
            def _():
                o_ref[...] = o_ref[...] * scale

    return pl.pallas_call(
        body, name=name, grid=(m // bm, n // bn, nk),
        in_specs=[pl.BlockSpec((tk, bm), lambda i, j, k: (k, i)), pl.BlockSpec((tk, bn), lambda i, j, k: (k, j))],
        out_specs=pl.BlockSpec((bm, bn), lambda i, j, k: (i, j)),
        out_shape=jax.ShapeDtypeStruct((m, n), F32),
        compiler_params=_params("parallel", "parallel", "arbitrary"))(a, b)


def _matmul_tn_blocks(a, b, name, scale=1.0, carry=None):
    a_blocked = a.ndim == 3
    nb, t = (a.shape[0], a.shape[1]) if a_blocked else (b.shape[0], b.shape[1])
    m, n = a.shape[-1], b.shape[-1]
    tk = _tile(t, 1408, 16)
    nk = t // tk
    if a_blocked:
        bo = _tile(n, 1024, LANES)
        a_spec = pl.BlockSpec((1, tk, m), lambda p, o, k: (p, k, 0))
        b_spec = pl.BlockSpec((tk, bo), lambda p, o, k: (k, o))
        o_spec = pl.BlockSpec((m, bo), lambda p, o, k: (p, o))
        out_shape = jax.ShapeDtypeStruct((nb * m, n), F32)
        grid = (nb, n // bo, nk)
    else:
        bo = _tile(m, 1024, LANES)
        a_spec = pl.BlockSpec((tk, bo), lambda p, o, k: (k, o))
        b_spec = pl.BlockSpec((1, tk, n), lambda p, o, k: (p, k, 0))
        o_spec = pl.BlockSpec((1, bo, n), lambda p, o, k: (p, o, 0))
        out_shape = jax.ShapeDtypeStruct((nb, m, n), F32)
        grid = (nb, m // bo, nk)

    def body(a_ref, b_ref, o_ref):
        k = pl.program_id(2)
        a_blk = a_ref[0] if a_blocked else a_ref[...]
        b_blk = b_ref[...] if a_blocked else b_ref[0]
        part = lax.dot_general(a_blk.astype(BF16), b_blk.astype(BF16), TN, preferred_element_type=F32)
        out = o_ref if a_blocked else o_ref.at[0]

        @pl.when(k == 0)
        def _():
            out[...] = part

        @pl.when(k > 0)
        def _():
            out[...] += part

        if scale != 1.0:
            @pl.when(k == nk - 1)
            def _():
                out[...] = out[...] * scale

    (out,), moved = _carried_call(body, carry, _first_step(3), _last_step(grid), name=name, grid=grid,
                                  in_specs=[a_spec, b_spec], out_specs=[o_spec], out_shape=[out_shape])(a, b)
    return out if carry is None else (out, moved)


def _matmul_nt_parts(parts, w, acc, name, carry=None):
    t = parts[0].shape[0]
    d = w.shape[0]
    widths = [p.shape[1] for p in parts]
    tk = _tile(math.gcd(*widths), 1024, LANES)
    counts = [wd // tk for wd in widths]
    starts = [sum(counts[:g]) for g in range(len(parts))]
    nk = sum(counts)
    tm = _tile(t, 768, 8)
    n_parts = len(parts)

    def body(*refs):
        a_refs, w_ref, o_ref = refs[:n_parts], refs[n_parts], refs[-1]
        k = pl.program_id(1)

        @pl.when(k == 0)
        def _():
            o_ref[...] = jnp.zeros_like(o_ref) if acc is None else refs[n_parts + 1][...]

        for g in range(n_parts):
            @pl.when((k >= starts[g]) & (k < starts[g] + counts[g]))
            def _(g=g):
                o_ref[...] += lax.dot_general(a_refs[g][...].astype(BF16), w_ref[...], NT,
                                              preferred_element_type=F32)

    in_specs = [pl.BlockSpec((tm, tk), lambda i, k, lo=starts[g], nb=counts[g]: (i, jnp.clip(k - lo, 0, nb - 1)))
                for g in range(n_parts)]
    in_specs.append(pl.BlockSpec((d, tk), lambda i, k: (0, k)))
    args = list(parts) + [w]
    if acc is not None:
        in_specs.append(pl.BlockSpec((tm, d), lambda i, k: (i, 0)))
        args.append(acc)
    grid = (t // tm, nk)
    (out,), moved = _carried_call(
        body, carry, _first_step(2), _last_step(grid), name=name, grid=grid, in_specs=in_specs,
        out_specs=[pl.BlockSpec((tm, d), lambda i, k: (i, 0))],
        out_shape=[jax.ShapeDtypeStruct((t, d), F32)])(*args)
    return out, moved


def _proj_fwd(h, gain, wp, name, carry=None):
    tp, d = h.shape
    npad = wp.shape[1]
    tm = _tile(tp, 768, 8)
    tn = _tile(npad, 1152, LANES)

    def body(h_ref, g_ref, w_ref, o_ref, n_ref):
        @pl.when(pl.program_id(1) == 0)
        def _():
            xh, _ = _rms_parts(h_ref[...])
            n_ref[...] = (xh * g_ref[...]).astype(BF16)

        o_ref[...] = jnp.dot(n_ref[...], w_ref[...], preferred_element_type=F32)

    grid = (tp // tm, npad // tn)
    return _carried_call(
        body, carry, _first_step(2), _last_step(grid), name=name, grid=grid,
        in_specs=[pl.BlockSpec((tm, d), lambda i, j: (i, 0)), pl.BlockSpec((1, d), lambda i, j: (0, 0)),
                  pl.BlockSpec((d, tn), lambda i, j: (0, j))],
        out_specs=[pl.BlockSpec((tm, tn), lambda i, j: (i, j)), pl.BlockSpec((tm, d), lambda i, j: (i, 0))],
        out_shape=[jax.ShapeDtypeStruct((tp, npad), F32), jax.ShapeDtypeStruct((tp, d), BF16)])(h, gain, wp)


def _norm_bwd(h, gain, dn, dres, name):
    tp, d = h.shape
    tm = _tile(tp, 256, 8)

    def body(h_ref, g_ref, dn_ref, dres_ref, dh_ref, dgain_ref):
        @pl.when(pl.program_id(0) == 0)
        def _():
            dgain_ref[...] = jnp.zeros_like(dgain_ref)

        xh, r = _rms_parts(h_ref[...])
        dx, dg = _rms_bwd(dn_ref[...], xh, r, g_ref[...])
        dh_ref[...] = dres_ref[...] + dx
        dgain_ref[...] += dg

    row = pl.BlockSpec((tm, d), lambda i: (i, 0))
    vec = pl.BlockSpec((1, d), lambda i: (0, 0))
    return pl.pallas_call(
        body, name=name, grid=(tp // tm,), in_specs=[row, vec, row, row], out_specs=[row, vec],
        out_shape=[jax.ShapeDtypeStruct((tp, d), F32), jax.ShapeDtypeStruct((1, d), F32)],
        compiler_params=_params("arbitrary"))(h, gain, dn, dres)


def _head_post(a, grp):
    a = _silu(a)
    r = lax.rsqrt(jnp.sum(a * a, axis=-1, keepdims=True) + EPS)
    if isinstance(grp, int):
        return a if grp == 2 else a * r * (HEAD_DIM ** -0.5 if grp == 0 else 1.0)
    scale = jnp.where(grp == 0, HEAD_DIM ** -0.5, 1.0).astype(F32)
    return jnp.where(grp == 2, a, a * r * scale)


def _head_post_bwd(c, dy, grp):
    sg = jax.nn.sigmoid(c)
    a = c * sg
    dsilu = sg * (1.0 + c * (1.0 - sg))
    if grp == 2:
        return dy * dsilu
    r = lax.rsqrt(jnp.sum(a * a, axis=-1, keepdims=True) + EPS)
    scale = HEAD_DIM ** -0.5 if grp == 0 else 1.0
    da = (scale * r) * (dy - a * (r * r * jnp.sum(dy * a, axis=-1, keepdims=True)))
    return da * dsilu


def _conv_taps(ext_sc, w_ref, tm):
    c = None
    for i in range(CONV_K):
        s = CONV_K - 1 - i
        term = w_ref[i:i + 1, :] * ext_sc[8 - s:8 - s + tm, :]
        c = term if c is None else c + term
    return c


def _conv_fwd(proj, conv_w, hv, name):
    tp = proj.shape[0]
    tm = _tile(tp, 256, 8)
    nh = hv // HEAD_DIM

    def body(x_ref, halo_ref, w_ref, o_ref, ext_sc):
        i, grp = pl.program_id(0), pl.program_id(1)
        ext_sc[0:8, :] = jnp.where(i == 0, 0.0, halo_ref[...])
        ext_sc[8:, :] = x_ref[...]
        c = _conv_taps(ext_sc, w_ref, tm)
        for h in range(nh):
            sl = slice(h * HEAD_DIM, (h + 1) * HEAD_DIM)
            o_ref[:, sl] = _head_post(c[:, sl], grp)

    return pl.pallas_call(
        body, name=name, grid=(tp // tm, 3),
        in_specs=[pl.BlockSpec((tm, hv), lambda i, g: (i, g)),
                  pl.BlockSpec((8, hv), lambda i, g: (jnp.maximum(i * (tm // 8) - 1, 0), g)),
                  pl.BlockSpec((CONV_K, hv), lambda i, g: (0, g))],
        out_specs=pl.BlockSpec((tm, hv), lambda i, g: (i, g)),
        out_shape=jax.ShapeDtypeStruct((tp, 3 * hv), F32),
        scratch_shapes=[pltpu.VMEM((tm + 8, hv), F32)],
        compiler_params=_params("parallel", "arbitrary"))(proj, proj, conv_w)


def _conv_bwd(proj, conv_w, dy, grp, hv, name):
    tp = proj.shape[0]
    tm = _tile(tp, 256, 8)
    ni = tp // tm
    nh = hv // HEAD_DIM

    def body(x_ref, halo_ref, w_ref, dy_ref, dx_ref, dw_ref, ext_sc, dc_sc):
        step = pl.program_id(0)
        ext_sc[0:8, :] = jnp.where(step == ni - 1, 0.0, halo_ref[...])
        ext_sc[8:, :] = x_ref[...]
        c = _conv_taps(ext_sc, w_ref, tm)
        @pl.when(step == 0)
        def _():
            dc_sc[tm:, :] = jnp.zeros((8, hv), F32)

        @pl.when(step > 0)
        def _():
            dc_sc[tm:, :] = dc_sc[0:8, :]

        for h in range(nh):
            sl = slice(h * HEAD_DIM, (h + 1) * HEAD_DIM)
            dc_sc[0:tm, sl] = _head_post_bwd(c[:, sl], dy_ref[:, sl], grp)

        @pl.when(step == 0)
        def _():
            dw_ref[...] = jnp.zeros_like(dw_ref)

        dc = dc_sc[0:tm, :]
        dx = None
        for k in range(CONV_K):
            s = CONV_K - 1 - k
            dw_ref[k:k + 1, :] += jnp.sum(dc * ext_sc[8 - s:8 - s + tm, :], axis=0, keepdims=True)
            term = w_ref[k:k + 1, :] * dc_sc[s:s + tm, :]
            dx = term if dx is None else dx + term
        dx_ref[...] = dx.astype(BF16)

    tile = lambda step: ni - 1 - step
    return pl.pallas_call(
        body, name=name, grid=(ni,),
        in_specs=[pl.BlockSpec((tm, hv), lambda s: (tile(s), grp)),
                  pl.BlockSpec((8, hv), lambda s: (jnp.maximum(tile(s) * (tm // 8) - 1, 0), grp)),
                  pl.BlockSpec((CONV_K, hv), lambda s: (0, grp)),
                  pl.BlockSpec((tm, hv), lambda s: (tile(s), 0))],
        out_specs=[pl.BlockSpec((tm, hv), lambda s: (tile(s), 0)), pl.BlockSpec((CONV_K, hv), lambda s: (0, 0))],
        out_shape=[jax.ShapeDtypeStruct((tp, hv), BF16), jax.ShapeDtypeStruct((CONV_K, hv), F32)],
        scratch_shapes=[pltpu.VMEM((tm + 8, hv), F32), pltpu.VMEM((tm + 8, hv), F32)],
        compiler_params=_params("arbitrary"))(proj, proj, conv_w, dy)


def _gdn_gates(ba, alog, dtb):
    x = ba + dtb
    softplus = jnp.maximum(x, 0.0) + jnp.log1p(jnp.exp(-jnp.abs(x)))
    return _cumsum_rows(-jnp.exp(alog) * softplus), jax.nn.sigmoid(ba)


def _gdn_chunks(states, qs, ks, vs, gates, known_inverses=None):
    mm_nn, mm_nt, mm_tn = _make_mm(False)
    hi_nn, _, _ = _make_mm(True)
    nh = len(states)
    items = range(len(qs))
    head = [i % nh for i in items]
    c = qs[0].shape[0]
    lane = lax.broadcasted_iota(jnp.int32, (c, LANES), 1)
    last_row = (lax.broadcasted_iota(jnp.int32, (c, 1), 0) == c - 1).astype(F32)
    ri = lax.broadcasted_iota(jnp.int32, (c, c), 0)
    ci = lax.broadcasted_iota(jnp.int32, (c, c), 1)
    causal = ri >= ci
    strict = ri > ci
    eye = (ri == ci).astype(F32)
    sel_a = [(lane == nh + h).astype(F32) for h in range(nh)]
    sel_b = [(lane == h).astype(F32) for h in range(nh)]

    gcol = [jnp.sum(gates[i // nh][0] * sel_a[head[i]], axis=1, keepdims=True) for i in items]
    grow = [jnp.sum(eye * gcol[i], axis=0, keepdims=True) for i in items]
    beta = [jnp.sum(gates[i // nh][1] * sel_b[head[i]], axis=1, keepdims=True) for i in items]
    decay = [jnp.where(causal, jnp.exp(jnp.where(causal, gcol[i] - grow[i], 0.0)), 0.0) for i in items]
    kb = [ks[i] * beta[i] for i in items]
    kk = [mm_nt(kb[i], ks[i]) for i in items]
    qk = [mm_nt(qs[i], ks[i]) for i in items]
    x_neg = [-jnp.where(strict, kk[i] * decay[i], 0.0) for i in items]
    if known_inverses is None:
        t_inv = _unit_lower_inverses(x_neg)
    else:
        t_inv = [_known_inverse(x_neg[i], known_inverses[i]) for i in items]
    eg = [jnp.exp(gcol[i]) for i in items]
    u = [hi_nn(t_inv[i], vs[i] * beta[i]) for i in items]
    w = [hi_nn(t_inv[i], kb[i] * eg[i]) for i in items]
    qk = [qk[i] * decay[i] for i in items]
    glast = [jnp.sum(gcol[i] * last_row, axis=0, keepdims=True) for i in items]
    q_dec = [qs[i] * eg[i] for i in items]
    k_dec = [ks[i] * jnp.exp(glast[i] - gcol[i]) for i in items]
    s_dec = [jnp.exp(glast[i]) for i in items]

    outs = []
    for first in range(0, len(qs), nh):
        chunk = range(first, first + nh)
        ws = [mm_nn(w[i], states[i - first]) for i in chunk]
        from_state = [mm_nn(q_dec[i], states[i - first]) for i in chunk]
        v_new = [u[i] - ws[i - first] for i in chunk]
        intra = [mm_nn(qk[i], v_new[i - first]) for i in chunk]
        kv = [mm_tn(k_dec[i], v_new[i - first]) for i in chunk]
        outs += [from_state[i - first] + intra[i - first] for i in chunk]
        states = [states[i - first] * s_dec[i] + kv[i - first] for i in chunk]
    return outs, states, t_inv


SCAN_CHUNKS = 2


def _scan_specs(nh, steps, rev, first_col):
    sidx = (lambda s: steps - 1 - s) if rev else (lambda s: s)
    hv = nh * HEAD_DIM
    rows = SCAN_CHUNKS * CHUNK
    cols = [pl.BlockSpec((rows, hv), lambda s, g=g: (sidx(s), first_col + g)) for g in range(3)]
    st = pl.BlockSpec((1, nh, HEAD_DIM, HEAD_DIM), lambda s: (sidx(s), 0, 0, 0))
    act = pl.BlockSpec((rows, hv), lambda s: (sidx(s), 0))
    return cols, st, act


def _chunk_heads(ref, nh):
    return [ref[j * CHUNK:(j + 1) * CHUNK, h * HEAD_DIM:(h + 1) * HEAD_DIM] for j in range(SCAN_CHUNKS)
            for h in range(nh)]


def _store_chunk_heads(ref, values, nh, dtype=None):
    for i, val in enumerate(values):
        j, h = divmod(i, nh)
        ref[j * CHUNK:(j + 1) * CHUNK, h * HEAD_DIM:(h + 1) * HEAD_DIM] = val if dtype is None else val.astype(dtype)


def _gdn_fwd(qkv, proj, alog, dtb, nh, name):
    tp = qkv.shape[0]
    steps = tp // (SCAN_CHUNKS * CHUNK)
    rows = SCAN_CHUNKS * CHUNK

    def body(q_ref, k_ref, v_ref, ba_ref, al_ref, dt_ref, o_ref, st_ref, inv_ref, s_sc):
        @pl.when(pl.program_id(0) == 0)
        def _():
            s_sc[...] = jnp.zeros_like(s_sc)

        gates = [_gdn_gates(ba_ref[j * CHUNK:(j + 1) * CHUNK, :], al_ref[...], dt_ref[...])
                 for j in range(SCAN_CHUNKS)]
        states = [s_sc[h] for h in range(nh)]
        for h in range(nh):
            st_ref[0, h] = states[h]
        outs, new_states, t_inv = _gdn_chunks(states, _chunk_heads(q_ref, nh), _chunk_heads(k_ref, nh),
                                              _chunk_heads(v_ref, nh), gates)
        _store_chunk_heads(o_ref, outs, nh)
        for h in range(nh):
            s_sc[h] = new_states[h]
        for i, t in enumerate(t_inv):
            inv_ref[0, i] = t

    cols, st, act = _scan_specs(nh, steps, False, 0)
    ba = pl.BlockSpec((rows, LANES), lambda s: (s, 10 * nh * HEAD_DIM // LANES))
    vec = pl.BlockSpec((1, LANES), lambda s: (0, 0))
    inv = pl.BlockSpec((1, SCAN_CHUNKS * nh, CHUNK, CHUNK), lambda s: (s, 0, 0, 0))
    return pl.pallas_call(
        body, name=name, grid=(steps,), in_specs=cols + [ba, vec, vec], out_specs=[act, st, inv],
        out_shape=[jax.ShapeDtypeStruct((tp, nh * HEAD_DIM), F32),
                   jax.ShapeDtypeStruct((steps, nh, HEAD_DIM, HEAD_DIM), F32),
                   jax.ShapeDtypeStruct((steps, SCAN_CHUNKS * nh, CHUNK, CHUNK), F32)],
        scratch_shapes=[pltpu.VMEM((nh, HEAD_DIM, HEAD_DIM), F32)],
        compiler_params=_params("arbitrary"))(qkv, qkv, qkv, proj, alog, dtb)


def _gdn_bwd(qkv, proj, alog, dtb, states, inverses, do, nh, name):
    tp = qkv.shape[0]
    steps = tp // (SCAN_CHUNKS * CHUNK)
    rows = SCAN_CHUNKS * CHUNK

    def body(q_ref, k_ref, v_ref, ba_ref, al_ref, dt_ref, st_ref, inv_ref, do_ref,
             dq_ref, dk_ref, dv_ref, dba_ref, dal_ref, ddt_ref, ds_sc):
        @pl.when(pl.program_id(0) == 0)
        def _():
            ds_sc[...] = jnp.zeros_like(ds_sc)
            dal_ref[...] = jnp.zeros_like(dal_ref)
            ddt_ref[...] = jnp.zeros_like(ddt_ref)

        gates, gates_vjps = [], []
        for j in range(SCAN_CHUNKS):
            g, g_vjp = jax.vjp(_gdn_gates, ba_ref[j * CHUNK:(j + 1) * CHUNK, :], al_ref[...], dt_ref[...])
            gates.append(g)
            gates_vjps.append(g_vjp)
        known = [inv_ref[0, i] for i in range(SCAN_CHUNKS * nh)]
        fn = lambda s, q, k, v, g: _gdn_chunks(s, q, k, v, g, known)[:2]
        _, vjp = jax.vjp(fn, [st_ref[0, h] for h in range(nh)], _chunk_heads(q_ref, nh), _chunk_heads(k_ref, nh),
                         _chunk_heads(v_ref, nh), gates)
        ds, dq, dk, dv, dgates = vjp((_chunk_heads(do_ref, nh), [ds_sc[h] for h in range(nh)]))
        for h in range(nh):
            ds_sc[h] = ds[h]
        _store_chunk_heads(dq_ref, dq, nh)
        _store_chunk_heads(dk_ref, dk, nh)
        _store_chunk_heads(dv_ref, dv, nh)
        for j in range(SCAN_CHUNKS):
            dba, dal, ddt = gates_vjps[j](dgates[j])
            dba_ref[j * CHUNK:(j + 1) * CHUNK, :] = dba
            dal_ref[...] += dal
            ddt_ref[...] += ddt

    cols, st, act = _scan_specs(nh, steps, True, 0)
    ba = pl.BlockSpec((rows, LANES), lambda s: (steps - 1 - s, 10 * nh * HEAD_DIM // LANES))
    vec = pl.BlockSpec((1, LANES), lambda s: (0, 0))
    inv = pl.BlockSpec((1, SCAN_CHUNKS * nh, CHUNK, CHUNK), lambda s: (steps - 1 - s, 0, 0, 0))
    return pl.pallas_call(
        body, name=name, grid=(steps,), in_specs=cols + [ba, vec, vec, st, inv, act],
        out_specs=[act, act, act, pl.BlockSpec((rows, LANES), lambda s: (steps - 1 - s, 0)), vec, vec],
        out_shape=[jax.ShapeDtypeStruct((tp, nh * HEAD_DIM), F32)] * 3
                  + [jax.ShapeDtypeStruct((tp, LANES), F32), jax.ShapeDtypeStruct((1, LANES), F32),
                     jax.ShapeDtypeStruct((1, LANES), F32)],
        scratch_shapes=[pltpu.VMEM((nh, HEAD_DIM, HEAD_DIM), F32)],
        compiler_params=_params("arbitrary"))(qkv, qkv, qkv, proj, alog, dtb, states, inverses, do)


def _swap_pairs(t):
    lane = lax.broadcasted_iota(jnp.int32, t.shape, 1)
    n = t.shape[1]
    return jnp.where(lane % 2 == 0, pltpu.roll(t, n - 1, 1), pltpu.roll(t, 1, 1))


def _rot(t, cos, sin_signed):
    return t * cos + _swap_pairs(t) * sin_signed


def _rot_t(dt, cos, sin_signed):
    return dt * cos + _swap_pairs(dt * sin_signed)


def _ret_chunks(states, qs, ks, vs, dec, xi, zeta, cd):
    mm_nn, mm_nt, mm_tn = _make_mm(False)
    nh = len(states)
    items = range(len(qs))
    scores = [mm_nt(qs[i], ks[i]) for i in items]
    kv = [mm_tn(ks[i] * zeta[i % nh], vs[i]) for i in items]
    intra = [mm_nn(scores[i] * dec[i % nh], vs[i]) for i in items]
    q_dec = [qs[i] * xi[i % nh] for i in items]
    outs = []
    for first in range(0, len(qs), nh):
        outs += [intra[first + h] + mm_nn(q_dec[first + h], states[h]) for h in range(nh)]
        states = [states[h] * cd[h] + kv[first + h] for h in range(nh)]
    return outs, states


def _ret_table_specs(nh, steps, rev):
    sidx = (lambda s: steps - 1 - s) if rev else (lambda s: s)
    rope = pl.BlockSpec((SCAN_CHUNKS * CHUNK, HEAD_DIM), lambda s: (sidx(s), 0))
    dec = pl.BlockSpec((nh, CHUNK, CHUNK), lambda s: (0, 0, 0))
    tab = pl.BlockSpec((nh, CHUNK, HEAD_DIM), lambda s: (0, 0, 0))
    cd = pl.BlockSpec((nh, 8, HEAD_DIM), lambda s: (0, 0, 0))
    return [rope, rope, dec, tab, tab, cd]


def _rotated(ref, cos_ref, sin_ref, nh, scale=1.0):
    out = []
    for j in range(SCAN_CHUNKS):
        rows = slice(j * CHUNK, (j + 1) * CHUNK)
        cos_t, sin_t = cos_ref[rows, :], sin_ref[rows, :]
        for h in range(nh):
            t = _rot(ref[rows, h * HEAD_DIM:(h + 1) * HEAD_DIM], cos_t, sin_t)
            out.append(t if scale == 1.0 else t * scale)
    return out


def _ret_fwd(proj, cos, sin, dec, xi, zeta, cd, nh, name):
    tp = proj.shape[0]
    steps = tp // (SCAN_CHUNKS * CHUNK)
    kscale = HEAD_DIM ** -0.5

    def body(q_ref, k_ref, v_ref, cos_ref, sin_ref, dec_ref, xi_ref, zeta_ref, cd_ref, o_ref, st_ref, s_sc):
        @pl.when(pl.program_id(0) == 0)
        def _():
            s_sc[...] = jnp.zeros_like(s_sc)

        heads = range(nh)
        states = [s_sc[h] for h in heads]
        for h in heads:
            st_ref[0, h] = states[h]
        outs, new_states = _ret_chunks(
            states, _rotated(q_ref, cos_ref, sin_ref, nh), _rotated(k_ref, cos_ref, sin_ref, nh, kscale),
            _chunk_heads(v_ref, nh), [dec_ref[h] for h in heads], [xi_ref[h] for h in heads],
            [zeta_ref[h] for h in heads], [cd_ref[h][0:1, :] for h in heads])
        _store_chunk_heads(o_ref, outs, nh)
        for h in heads:
            s_sc[h] = new_states[h]

    cols, st, act = _scan_specs(nh, steps, False, 3)
    return pl.pallas_call(
        body, name=name, grid=(steps,), in_specs=cols + _ret_table_specs(nh, steps, False), out_specs=[act, st],
        out_shape=[jax.ShapeDtypeStruct((tp, nh * HEAD_DIM), F32),
                   jax.ShapeDtypeStruct((steps, nh, HEAD_DIM, HEAD_DIM), F32)],
        scratch_shapes=[pltpu.VMEM((nh, HEAD_DIM, HEAD_DIM), F32)],
        compiler_params=_params("arbitrary"))(proj, proj, proj, cos, sin, dec, xi, zeta, cd)


def _ret_bwd(proj, cos, sin, dec, xi, zeta, cd, states, do, nh, name):
    tp = proj.shape[0]
    steps = tp // (SCAN_CHUNKS * CHUNK)
    kscale = HEAD_DIM ** -0.5

    def body(q_ref, k_ref, v_ref, cos_ref, sin_ref, dec_ref, xi_ref, zeta_ref, cd_ref, st_ref, do_ref,
             dq_ref, dk_ref, dv_ref, ds_sc):
        @pl.when(pl.program_id(0) == 0)
        def _():
            ds_sc[...] = jnp.zeros_like(ds_sc)

        heads = range(nh)
        fn = functools.partial(_ret_chunks, dec=[dec_ref[h] for h in heads], xi=[xi_ref[h] for h in heads],
                               zeta=[zeta_ref[h] for h in heads], cd=[cd_ref[h][0:1, :] for h in heads])
        _, vjp = jax.vjp(fn, [st_ref[0, h] for h in heads], _rotated(q_ref, cos_ref, sin_ref, nh),
                         _rotated(k_ref, cos_ref, sin_ref, nh, kscale), _chunk_heads(v_ref, nh))
        ds, dq, dk, dv = vjp((_chunk_heads(do_ref, nh), [ds_sc[h] for h in heads]))
        for h in heads:
            ds_sc[h] = ds[h]
        for i in range(SCAN_CHUNKS * nh):
            rows = slice((i // nh) * CHUNK, (i // nh + 1) * CHUNK)
            cos_t, sin_t = cos_ref[rows, :], sin_ref[rows, :]
            dq[i] = _rot_t(dq[i], cos_t, sin_t)
            dk[i] = _rot_t(dk[i] * kscale, cos_t, sin_t)
        _store_chunk_heads(dq_ref, dq, nh, BF16)
        _store_chunk_heads(dk_ref, dk, nh, BF16)
        _store_chunk_heads(dv_ref, dv, nh, BF16)

    cols, st, act = _scan_specs(nh, steps, True, 3)
    return pl.pallas_call(
        body, name=name, grid=(steps,), in_specs=cols + _ret_table_specs(nh, steps, True) + [st, act],
        out_specs=[act, act, act],
        out_shape=[jax.ShapeDtypeStruct((tp, nh * HEAD_DIM), BF16)] * 3,
        scratch_shapes=[pltpu.VMEM((nh, HEAD_DIM, HEAD_DIM), F32)],
        compiler_params=_params("arbitrary"))(proj, proj, proj, cos, sin, dec, xi, zeta, cd, states, do)


def _gdn_out(o, z, gnorm):
    return o * lax.rsqrt(jnp.mean(o * o, axis=-1, keepdims=True) + EPS) * gnorm * _silu(z)


def _ret_out(o, rg, rnorm):
    mu = jnp.mean(o, axis=-1, keepdims=True)
    var = jnp.mean(jnp.square(o - mu), axis=-1, keepdims=True)
    return _silu(rg) * ((o - mu) * lax.rsqrt(var + EPS) * rnorm)


def _post_specs(tm, hv, d):
    row = lambda col: pl.BlockSpec((tm, hv), lambda i: (i, col))
    return dict(
        oa=row(0), ob=row(0), z=row(6), rg=row(7), ga=row(8), gb=row(9),
        gnorm=pl.BlockSpec((1, HEAD_DIM), lambda i: (0, 0)), rnorm=pl.BlockSpec((1, hv), lambda i: (0, 0)),
        w=pl.BlockSpec((hv, d), lambda i: (0, 0)), res=pl.BlockSpec((tm, d), lambda i: (i, 0)))


def _post_fwd(oa, ob, proj, gnorm, rnorm, wbg, wbr, wo, h1, name):
    tp, d = h1.shape
    hv = oa.shape[1]
    nh = hv // HEAD_DIM
    tm = _tile(tp, 256, 8)

    def body(oa_ref, ob_ref, z_ref, rg_ref, ga_ref, gb_ref, gn_ref, rn_ref, wbg_ref, wbr_ref, wo_ref, h_ref,
             o_ref, ya_sc, yb_sc):
        for h in range(nh):
            sl = slice(h * HEAD_DIM, (h + 1) * HEAD_DIM)
            ya_sc[:, sl] = _gdn_out(oa_ref[:, sl], z_ref[:, sl], gn_ref[...]).astype(BF16)
            yb_sc[:, sl] = _ret_out(ob_ref[:, sl], rg_ref[:, sl], rn_ref[:, sl]).astype(BF16)
        pa = jnp.dot(ya_sc[...], wbg_ref[...], preferred_element_type=F32)
        pb = jnp.dot(yb_sc[...], wbr_ref[...], preferred_element_type=F32)
        merged = jax.nn.sigmoid(ga_ref[...]) * pa + jax.nn.sigmoid(gb_ref[...]) * pb
        o_ref[...] = h_ref[...] + jnp.dot(merged.astype(BF16), wo_ref[...], preferred_element_type=F32)

    sp = _post_specs(tm, hv, d)
    return pl.pallas_call(
        body, name=name, grid=(tp // tm,),
        in_specs=[sp["oa"], sp["ob"], sp["z"], sp["rg"], sp["ga"], sp["gb"], sp["gnorm"], sp["rnorm"],
                  sp["w"], sp["w"], sp["w"], sp["res"]],
        out_specs=sp["res"], out_shape=jax.ShapeDtypeStruct((tp, d), F32),
        scratch_shapes=[pltpu.VMEM((tm, hv), BF16), pltpu.VMEM((tm, hv), BF16)],
        compiler_params=_params("parallel"))(oa, ob, proj, proj, proj, proj, gnorm, rnorm, wbg, wbr, wo, h1)


def _post_bwd(oa, ob, proj, gnorm, rnorm, wbg, wbr, wo, dh2, name):
    tp, d = dh2.shape
    hv = oa.shape[1]
    nh = hv // HEAD_DIM
    tm = _tile(tp, 256, 8)

    def body(oa_ref, ob_ref, z_ref, rg_ref, ga_ref, gb_ref, gn_ref, rn_ref, wbg_ref, wbr_ref, wo_ref, dh_ref,
             doa_ref, dob_ref, dg_ref, ya_ref, yb_ref, mg_ref, dpa_ref, dpb_ref, dgn_ref, drn_ref,
             dya_sc, dyb_sc):
        @pl.when(pl.program_id(0) == 0)
        def _():
            dgn_ref[...] = jnp.zeros_like(dgn_ref)
            drn_ref[...] = jnp.zeros_like(drn_ref)

        for h in range(nh):
            sl = slice(h * HEAD_DIM, (h + 1) * HEAD_DIM)
            ya_ref[:, sl] = _gdn_out(oa_ref[:, sl], z_ref[:, sl], gn_ref[...]).astype(BF16)
            yb_ref[:, sl] = _ret_out(ob_ref[:, sl], rg_ref[:, sl], rn_ref[:, sl]).astype(BF16)
        pa = jnp.dot(ya_ref[...], wbg_ref[...], preferred_element_type=F32)
        pb = jnp.dot(yb_ref[...], wbr_ref[...], preferred_element_type=F32)
        sa = jax.nn.sigmoid(ga_ref[...])
        sb = jax.nn.sigmoid(gb_ref[...])
        mg_ref[...] = (sa * pa + sb * pb).astype(BF16)
        dm = lax.dot_general(dh_ref[...].astype(BF16), wo_ref[...], NT, preferred_element_type=F32)
        dpa = (dm * sa).astype(BF16)
        dpb = (dm * sb).astype(BF16)
        dpa_ref[...] = dpa
        dpb_ref[...] = dpb
        dg_ref[:, 2 * hv:3 * hv] = (dm * pa * sa * (1.0 - sa)).astype(BF16)
        dg_ref[:, 3 * hv:4 * hv] = (dm * pb * sb * (1.0 - sb)).astype(BF16)
        dya_sc[...] = lax.dot_general(dpa, wbg_ref[...], NT, preferred_element_type=F32)
        dyb_sc[...] = lax.dot_general(dpb, wbr_ref[...], NT, preferred_element_type=F32)
        for h in range(nh):
            sl = slice(h * HEAD_DIM, (h + 1) * HEAD_DIM)
            _, vjp_a = jax.vjp(_gdn_out, oa_ref[:, sl], z_ref[:, sl], gn_ref[...])
            doa, dz, dgn = vjp_a(dya_sc[:, sl])
            doa_ref[:, sl] = doa
            dg_ref[:, sl] = dz.astype(BF16)
            dgn_ref[...] += dgn
            _, vjp_b = jax.vjp(_ret_out, ob_ref[:, sl], rg_ref[:, sl], rn_ref[:, sl])
            dob, drg, drn = vjp_b(dyb_sc[:, sl])
            dob_ref[:, sl] = dob
            dg_ref[:, hv + h * HEAD_DIM:hv + (h + 1) * HEAD_DIM] = drg.astype(BF16)
            drn_ref[:, sl] += drn

    sp = _post_specs(tm, hv, d)
    act = pl.BlockSpec((tm, hv), lambda i: (i, 0))
    return pl.pallas_call(
        body, name=name, grid=(tp // tm,),
        in_specs=[sp["oa"], sp["ob"], sp["z"], sp["rg"], sp["ga"], sp["gb"], sp["gnorm"], sp["rnorm"],
                  sp["w"], sp["w"], sp["w"], sp["res"]],
        out_specs=[act, act, pl.BlockSpec((tm, 4 * hv), lambda i: (i, 0)), act, act, sp["res"], sp["res"],
                   sp["res"], sp["gnorm"], sp["rnorm"]],
        out_shape=[jax.ShapeDtypeStruct((tp, hv), F32), jax.ShapeDtypeStruct((tp, hv), F32),
                   jax.ShapeDtypeStruct((tp, 4 * hv), BF16), jax.ShapeDtypeStruct((tp, hv), BF16),
                   jax.ShapeDtypeStruct((tp, hv), BF16), jax.ShapeDtypeStruct((tp, d), BF16),
                   jax.ShapeDtypeStruct((tp, d), BF16), jax.ShapeDtypeStruct((tp, d), BF16),
                   jax.ShapeDtypeStruct((1, HEAD_DIM), F32), jax.ShapeDtypeStruct((1, hv), F32)],
        scratch_shapes=[pltpu.VMEM((tm, hv), F32), pltpu.VMEM((tm, hv), F32)],
        compiler_params=_params("arbitrary"))(oa, ob, proj, proj, proj, proj, gnorm, rnorm, wbg, wbr, wo, dh2)


def _final(h3, gain, target, name):
    tp, d = h3.shape
    tm = HEAD_ROWS

    def body(h_ref, g_ref, t_ref, loss_ref, dh_ref, dgain_ref):
        i = pl.program_id(0)

        @pl.when(i == 0)
        def _():
            loss_ref[...] = jnp.zeros_like(loss_ref)
            dgain_ref[...] = jnp.zeros_like(dgain_ref)

        xh, r = _rms_parts(h_ref[...])
        err = jnp.where(i == 0, 0.0, xh * g_ref[...] - t_ref[...])
        dx, dg = _rms_bwd(err * (1.0 / d), xh, r, g_ref[...])
        dh_ref[...] = dx
        dgain_ref[...] += dg
        loss_ref[...] += 0.5 * jnp.sum(jnp.mean(err * err, axis=-1, keepdims=True), axis=0, keepdims=True)

    row = pl.BlockSpec((tm, d), lambda i: (i, 0))
    vec = pl.BlockSpec((1, d), lambda i: (0, 0))
    return pl.pallas_call(
        body, name=name, grid=(tp // tm,),
        in_specs=[row, vec, pl.BlockSpec((tm, d), lambda i: (jnp.maximum(i - 1, 0), 0))],
        out_specs=[pl.BlockSpec((1, LANES), lambda i: (0, 0)), row, vec],
        out_shape=[jax.ShapeDtypeStruct((1, LANES), F32), jax.ShapeDtypeStruct((tp, d), F32),
                   jax.ShapeDtypeStruct((1, d), F32)],
        compiler_params=_params("arbitrary"))(h3, gain, target)


def _peer(k):
    x, y, c = lax.axis_index("x"), lax.axis_index("y"), lax.axis_index("c")
    return (1 - x if k & 4 else x, 1 - y if k & 2 else y, 1 - c if k & 1 else c)


def _my_index():
    return 4 * lax.axis_index("x") + 2 * lax.axis_index("y") + lax.axis_index("c")


def _exchange(bufs, scatter, name):
    n = len(bufs)

    def body(*refs):
        _exchange_copies(refs[:n], refs[n:2 * n], refs[2 * n:], scatter, True, True)

    hbm, out_shape, sems = _exchange_refs(bufs)
    return pl.pallas_call(
        body, name=name, in_specs=hbm, out_specs=hbm, out_shape=out_shape, scratch_shapes=sems,
        compiler_params=pltpu.CompilerParams(has_side_effects=True))(*bufs)


def _gather_via_sibling(bufs, name):
    n = len(bufs)

    def body(*refs):
        x_refs, out_refs = refs[:n], refs[n:2 * n]
        send_sems, recv_sems, local_sems = refs[2 * n:]
        x, y, c = lax.axis_index("x"), lax.axis_index("y"), lax.axis_index("c")
        me, sibling = (x, y, c), (x, y, 1 - c)
        chips = [(1 - x, y), (x, 1 - y), (1 - x, 1 - y)]
        rows = lambda a, dev: out_refs[a].at[4 * dev[0] + 2 * dev[1] + dev[2]]

        def copy(k, a, block, to, src=None):
            return pltpu.make_async_remote_copy(
                src_ref=rows(a, block) if src is None else src, dst_ref=rows(a, block),
                send_sem=send_sems.at[k * n + a], recv_sem=recv_sems.at[k * n + a],
                device_id=to, device_id_type=pl.DeviceIdType.MESH)

        mine = [pltpu.make_async_copy(x_refs[a], rows(a, me), local_sems.at[a]) for a in range(n)]
        first = [copy(0, a, me, sibling, src=x_refs[a]) for a in range(n)]
        first += [copy(1 + j, a, me, (*chip, c), src=x_refs[a]) for j, chip in enumerate(chips) for a in range(n)]
        for cp in mine + first:
            cp.start()
        passed = []
        for j, chip in enumerate(chips):
            for a in range(n):
                copy(1 + j, a, (*chip, c), me).wait_recv()
                passed.append(copy(4 + j, a, (*chip, c), sibling))
                passed[-1].start()
        for a in range(n):
            copy(0, a, sibling, me).wait_recv()
        for j, chip in enumerate(chips):
            for a in range(n):
                copy(4 + j, a, (*chip, 1 - c), me).wait_recv()
        for cp in first + passed:
            cp.wait_send()
        for cp in mine:
            cp.wait()

    hbm, out_shape, sems = _exchange_refs(bufs)
    return pl.pallas_call(
        body, name=name, in_specs=hbm, out_specs=hbm, out_shape=out_shape, scratch_shapes=sems,
        compiler_params=pltpu.CompilerParams(has_side_effects=True))(*bufs)


def _exchange_refs(bufs):
    n = len(bufs)
    return ([pl.BlockSpec(memory_space=pl.ANY)] * n,
            [jax.ShapeDtypeStruct((N_DEV,) + b.shape[-2:], b.dtype) for b in bufs],
            [pltpu.SemaphoreType.DMA(((N_DEV - 1) * n,)), pltpu.SemaphoreType.DMA(((N_DEV - 1) * n,)),
             pltpu.SemaphoreType.DMA((n,))])


def _exchange_copies(x_refs, out_refs, sems, scatter, start, wait):
    n = len(x_refs)
    send_sems, recv_sems, local_sems = sems
    me = _my_index()
    copies = []
    for a in range(n):
        copies.append(pltpu.make_async_copy(x_refs[a].at[me] if scatter else x_refs[a], out_refs[a].at[me],
                                            local_sems.at[a]))
    sends = []
    arrivals = []
    for k in range(1, N_DEV):
        x, y, c = _peer(k)
        peer = 4 * x + 2 * y + c
        for a in range(n):
            sem = (k - 1) * n + a
            sends.append(pltpu.make_async_remote_copy(
                src_ref=x_refs[a].at[peer] if scatter else x_refs[a], dst_ref=out_refs[a].at[me],
                send_sem=send_sems.at[sem], recv_sem=recv_sems.at[sem],
                device_id=(x, y, c), device_id_type=pl.DeviceIdType.MESH))
            landed = out_refs[a].at[peer]
            arrivals.append(pltpu.make_async_remote_copy(
                src_ref=landed, dst_ref=landed, send_sem=send_sems.at[sem], recv_sem=recv_sems.at[sem],
                device_id=(x, y, c), device_id_type=pl.DeviceIdType.MESH))
    if start:
        for cp in copies + sends:
            cp.start()
    if wait:
        for cp in arrivals:
            cp.wait_recv()
        for cp in sends:
            cp.wait_send()
        for cp in copies:
            cp.wait()


def _carried_call(body, carry, first, last, *, name, grid, in_specs, out_specs, out_shape, scratch_shapes=()):
    in_specs, out_specs, out_shape = list(in_specs), list(out_specs), list(out_shape)
    semantics = ("arbitrary",) * len(grid)
    if carry is None:
        call = pl.pallas_call(body, name=name, grid=grid, in_specs=in_specs, out_specs=out_specs,
                              out_shape=out_shape, scratch_shapes=list(scratch_shapes),
                              compiler_params=_params(*semantics))
        return lambda *args: (call(*args), [])
    bufs, scatter = carry
    n, n_in, n_out, n_scratch = len(bufs), len(in_specs), len(out_specs), len(scratch_shapes)
    hbm, x_shapes, sems = _exchange_refs(bufs)

    def full_body(*refs):
        ins, x_refs = refs[:n_in], refs[n_in:n_in + n]
        outs, xo_refs = refs[n_in + n:n_in + n + n_out], refs[n_in + n + n_out:n_in + 2 * n + n_out]
        scratch = refs[n_in + 2 * n + n_out:n_in + 2 * n + n_out + n_scratch]
        x_sems = refs[n_in + 2 * n + n_out + n_scratch:]

        @pl.when(first())
        def _():
            _exchange_copies(x_refs, xo_refs, x_sems, scatter, True, False)

        body(*ins, *outs, *scratch)

        @pl.when(last())
        def _():
            _exchange_copies(x_refs, xo_refs, x_sems, scatter, False, True)

    call = pl.pallas_call(full_body, name=name, grid=grid, in_specs=in_specs + hbm, out_specs=out_specs + hbm,
                          out_shape=out_shape + x_shapes, scratch_shapes=list(scratch_shapes) + sems,
                          compiler_params=_params(*semantics))

    def run(*args):
        res = call(*args, *bufs)
        return res[:n_out], res[n_out:]
    return run


def _adamw(w, g, m, v, name):
    r, c = w.shape
    parts = g.ndim == 3
    tr = _tile(r, 256, 16 if parts else 8)
    c1 = 1.0 - ADAM_B1 ** ADAM_STEP
    c2 = 1.0 - ADAM_B2 ** ADAM_STEP

    def body(w_ref, g_ref, m_ref, v_ref, go_ref, d_ref, mo_ref, vo_ref):
        if parts:
            g = g_ref[0].astype(F32)
            for q in range(1, N_DEV):
                g = g + g_ref[q].astype(F32)
        else:
            g = g_ref[...]
        m = ADAM_B1 * m_ref[...] + (1.0 - ADAM_B1) * g
        v = ADAM_B2 * v_ref[...] + (1.0 - ADAM_B2) * (g * g)
        go_ref[...] = g
        d_ref[...] = -ADAM_LR * ((m / c1) / (jnp.sqrt(v / c2) + ADAM_EPS) + ADAM_WD * w_ref[...])
        mo_ref[...] = m
        vo_ref[...] = v

    blk = pl.BlockSpec((tr, c), lambda i: (i, 0))
    g_spec = pl.BlockSpec((N_DEV, tr, c), lambda i: (0, i, 0)) if parts else blk
    return pl.pallas_call(
        body, name=name, grid=(r // tr,), in_specs=[blk, g_spec, blk, blk], out_specs=[blk] * 4,
        out_shape=[jax.ShapeDtypeStruct((r, c), F32)] * 4,
        compiler_params=_params("parallel"))(w, g, m, v)


def _win_segments(hv, nh):
    o_z, o_b = 3 * hv, 4 * hv
    o_r = o_b + 2 * nh
    return [(0, 0, 3 * hv), (3 * hv, o_r, 3 * hv), (6 * hv, o_z, hv), (7 * hv, o_r + 3 * hv, 3 * hv),
            (10 * hv, o_b, 2 * nh)]


def _win_from_shards(shards, hv, nh):
    _, d, cs = shards.shape
    pieces = []
    for _, src, width in _win_segments(hv, nh):
        lo = src
        while lo < src + width:
            p = lo // cs
            hi = min(src + width, (p + 1) * cs)
            pieces.append(shards[p][:, lo - p * cs:hi - p * cs])
            lo = hi
    pieces.append(jnp.zeros((d, LANES - 2 * nh), shards.dtype))
    return jnp.concatenate(pieces, axis=1)


def _win_grad_to_shards(parts, hv, nh, cs):
    segments = _win_segments(hv, nh)
    starts = [sum(p.shape[1] for p in parts[:i]) for i in range(len(parts))]

    def columns(a, b):
        out = []
        for part, start in zip(parts, starts):
            lo, hi = max(a, start), min(b, start + part.shape[1])
            if lo < hi:
                out.append(part[:, lo - start:hi - start])
        return out

    shards = []
    for p in range(N_DEV):
        pieces = []
        lo = p * cs
        while lo < (p + 1) * cs:
            here, src, width = next(s for s in segments if s[1] <= lo < s[1] + s[2])
            hi = min((p + 1) * cs, src + width)
            pieces += columns(here + lo - src, here + hi - src)
            lo = hi
        shards.append(jnp.concatenate(pieces, axis=1))
    return jnp.stack(shards)


def _rope_tables(tp):
    pos = jnp.arange(tp, dtype=F32) - float(PAD_FRONT)
    inv = 1.0 / (ROPE_BASE ** jnp.linspace(0.0, 1.0, HEAD_DIM // 2, dtype=F32))
    ang = pos[:, None] * inv[None, :]
    cos = jnp.repeat(jnp.cos(ang), 2, axis=1)
    sin = jnp.repeat(jnp.sin(ang), 2, axis=1) * jnp.tile(jnp.array([-1.0, 1.0], F32), HEAD_DIM // 2)[None, :]
    return cos, sin


def _retention_tables(nh):
    log_gamma = jnp.log1p(-jnp.exp2(-5.0 - jnp.arange(nh, dtype=F32)))
    pos = jnp.arange(CHUNK, dtype=F32)
    causal = pos[:, None] >= pos[None, :]
    diff = pos[:, None] - pos[None, :]
    dec = jnp.where(causal, jnp.exp(jnp.where(causal, diff, 0.0) * log_gamma[:, None, None]), 0.0)
    ones = jnp.ones((1, 1, HEAD_DIM), F32)
    xi = jnp.exp((pos + 1.0)[None, :] * log_gamma[:, None])[:, :, None] * ones
    zeta = jnp.exp((CHUNK - 1.0 - pos)[None, :] * log_gamma[:, None])[:, :, None] * ones
    cd = jnp.exp(CHUNK * log_gamma)[:, None, None] * jnp.ones((1, 8, HEAD_DIM), F32)
    return dec, xi, zeta, cd


SHARDED = ("meta_tokens", "ffn1_w_in", "ffn1_w_out", "w_in", "gdn_conv_w", "w_branch_gdn", "w_branch_ret",
           "w_out", "ffn2_w_in", "ffn2_w_out")
COLUMN_SHARDED = ("meta_tokens", "ffn1_w_in", "w_in", "gdn_conv_w", "ffn2_w_in")
EXACT_F32 = ("meta_tokens", "gdn_conv_w")
REPLICATED = ("ffn1_norm", "mix_norm", "gdn_a_log", "gdn_dt_bias", "gdn_out_norm", "ret_out_norm", "ffn2_norm",
              "final_norm")
WEIGHTS = ("meta_tokens", "ffn1_norm", "ffn1_w_in", "ffn1_w_out", "mix_norm", "w_in", "gdn_conv_w", "gdn_a_log",
           "gdn_dt_bias", "gdn_out_norm", "ret_out_norm", "w_branch_gdn", "w_branch_ret", "w_out", "ffn2_norm",
           "ffn2_w_in", "ffn2_w_out", "final_norm")


def _as2d(a):
    if a.ndim == 3:
        return a[0]
    if a.ndim == 1:
        return a[None, :]
    return a


def _rows_of(shards):
    return shards.reshape(-1, shards.shape[2])


def _cols_of(shards):
    return shards.transpose(1, 0, 2).reshape(shards.shape[1], -1)


def _row_shards(a):
    return a.reshape(N_DEV, -1, a.shape[1])


def _col_shards(a):
    return a.reshape(a.shape[0], N_DEV, -1).transpose(1, 0, 2)


GATHER_FIRST = ("meta_tokens", "ffn1_w_in", "ffn1_w_out")
GATHER_BEHIND_FFN1 = ("w_in", "gdn_conv_w")
GATHER_BEHIND_PROJ = ("w_branch_gdn", "w_branch_ret", "w_out", "ffn2_w_in", "ffn2_w_out")
SCATTER_BEHIND_DN2 = ("ffn2_w_in", "ffn2_w_out", "w_branch_gdn", "w_branch_ret", "w_out")
SCATTER_BEHIND_FFN1 = ("w_in", "gdn_conv_w")
SCATTER_BEHIND_DWG = ("meta_tokens", "ffn1_w_out")
SCATTER_LAST = ("ffn1_w_in",)


def _device_step(x, target, send, rep):
    seq, d = x.shape
    tp = HEAD_ROWS + seq
    hv = d
    nh = hv // HEAD_DIM
    assert tp % (SCAN_CHUNKS * CHUNK) == 0 and tp % HEAD_ROWS == 0
    bf16_shards = lambda grads, names: [grads[n].astype(BF16) for n in names]

    pad_lanes = lambda row: jnp.pad(row, ((0, 0), (nh, LANES - 2 * nh)))
    alog = pad_lanes(rep["gdn_a_log"])
    dtb = pad_lanes(rep["gdn_dt_bias"])
    cos, sin = _rope_tables(tp)
    dec, xi, zeta, cd = _retention_tables(nh)

    got = dict(zip(GATHER_FIRST, _gather_via_sibling([send[n] for n in GATHER_FIRST], "gather_ffn1")))
    h0 = jnp.concatenate([jnp.zeros((PAD_FRONT, d), F32), _cols_of(got["meta_tokens"]), x], axis=0)
    f1i, f1o = got["ffn1_w_in"], _rows_of(got["ffn1_w_out"])
    (h1, ag1, au1), moved = _ffn_fwd(h0, rep["ffn1_norm"], f1i, f1o, "ffn1_fwd",
                                     ([send[n] for n in GATHER_BEHIND_FFN1], False))
    got.update(zip(GATHER_BEHIND_FFN1, moved))
    wp = _win_from_shards(got["w_in"], hv, nh)
    conv_w = _cols_of(got["gdn_conv_w"])
    (proj, n2), moved = _proj_fwd(h1, rep["mix_norm"], wp, "proj_fwd",
                                  ([send[n] for n in GATHER_BEHIND_PROJ], False))
    got.update(zip(GATHER_BEHIND_PROJ, moved))
    wbg, wbr, wo = _rows_of(got["w_branch_gdn"]), _rows_of(got["w_branch_ret"]), _rows_of(got["w_out"])
    f2i, f2o = got["ffn2_w_in"], _rows_of(got["ffn2_w_out"])
    qkv = _conv_fwd(proj, conv_w, hv, "conv_fwd")
    oa, s_gdn, t_gdn = _gdn_fwd(qkv, proj, alog, dtb, nh, "gdn_fwd")
    ob, s_ret = _ret_fwd(proj, cos, sin, dec, xi, zeta, cd, nh, "ret_fwd")
    h2 = _post_fwd(oa, ob, proj, rep["gdn_out_norm"], rep["ret_out_norm"], wbg, wbr, wo, h1, "post_fwd")
    (h3, ag2, au2), _ = _ffn_fwd(h2, rep["ffn2_norm"], f2i, f2o, "ffn2_fwd")
    loss_row, dh3, d_final = _final(h3, rep["final_norm"], target, "final")

    (dh2, d_f2n, n3, hid2, dag2, dau2), _ = _ffn_bwd(h2, dh3, rep["ffn2_norm"], f2i, f2o, ag2, au2, "ffn2_bwd")
    grads = {"ffn2_w_in": jnp.concatenate([_matmul_tn_blocks(n3, dag2, "ffn2_dwg"),
                                           _matmul_tn_blocks(n3, dau2, "ffn2_dwu")]),
             "ffn2_w_out": _row_shards(_matmul_tn_blocks(hid2, dh3, "ffn2_dwo", 0.5))}

    doa, dob, dgate, ya, yb, merged, dpa, dpb, d_gn, d_rn = _post_bwd(
        oa, ob, proj, rep["gdn_out_norm"], rep["ret_out_norm"], wbg, wbr, wo, dh2, "post_bwd")
    grads["w_branch_gdn"] = _row_shards(_matmul_tn(ya, dpa, "dw_branch_gdn"))
    grads["w_branch_ret"] = _row_shards(_matmul_tn(yb, dpb, "dw_branch_ret"))
    grads["w_out"] = _row_shards(_matmul_tn(merged, dh2, "dw_out"))

    d_ret = _ret_bwd(proj, cos, sin, dec, xi, zeta, cd, s_ret, dob, nh, "ret_bwd")
    gdn_grads = _gdn_bwd(qkv, proj, alog, dtb, s_gdn, t_gdn, doa, nh, "gdn_bwd")
    dba, d_alog, d_dtb = gdn_grads[3:]
    dpre, g_conv = [], []
    for grp, tag in enumerate("qkv"):
        dx, dw = _conv_bwd(proj, conv_w, gdn_grads[grp], grp, hv, "conv_bwd_" + tag)
        dpre.append(dx)
        g_conv.append(dw)
    grads["gdn_conv_w"] = _col_shards(jnp.concatenate(g_conv, axis=1))

    wide = dpre + list(d_ret) + [dgate]
    dn2, moved = _matmul_nt_parts(wide, wp[:, :10 * hv], None, "dn2_wide",
                                  (bf16_shards(grads, SCATTER_BEHIND_DN2), True))
    parts = dict(zip(SCATTER_BEHIND_DN2, moved))
    dn2, _ = _matmul_nt_parts([dba], wp[:, 10 * hv:], dn2, "dn2_beta_alpha")
    g_wp = [_matmul_tn(n2, dg, "dw_in_%d" % idx) for idx, dg in enumerate(wide + [dba])]
    grads["w_in"] = _win_grad_to_shards(g_wp, hv, nh, send["w_in"].shape[1])
    dh1, d_mixn = _norm_bwd(h1, rep["mix_norm"], dn2, dh2, "mix_norm_bwd")

    (dh0, d_f1n, n1, hid1, dag1, dau1), moved = _ffn_bwd(h0, dh1, rep["ffn1_norm"], f1i, f1o, ag1, au1, "ffn1_bwd",
                                                         (bf16_shards(grads, SCATTER_BEHIND_FFN1), True))
    parts.update(zip(SCATTER_BEHIND_FFN1, moved))
    grads["ffn1_w_out"] = _row_shards(_matmul_tn_blocks(hid1, dh1, "ffn1_dwo", 0.5))
    grads["meta_tokens"] = _col_shards(dh0[PAD_FRONT:HEAD_ROWS])
    g_gate, moved = _matmul_tn_blocks(n1, dag1, "ffn1_dwg", carry=(bf16_shards(grads, SCATTER_BEHIND_DWG), True))
    parts.update(zip(SCATTER_BEHIND_DWG, moved))
    grads["ffn1_w_in"] = jnp.concatenate([g_gate, _matmul_tn_blocks(n1, dau1, "ffn1_dwu")])
    parts.update(zip(SCATTER_LAST, _exchange(bf16_shards(grads, SCATTER_LAST), True, "scatter_ffn1")))

    small = {"ffn1_norm": d_f1n, "mix_norm": d_mixn, "gdn_a_log": d_alog[:, nh:2 * nh],
             "gdn_dt_bias": d_dtb[:, nh:2 * nh], "gdn_out_norm": d_gn, "ret_out_norm": d_rn, "ffn2_norm": d_f2n,
             "final_norm": d_final}
    return loss_row[0, 0], dh0[HEAD_ROWS:], parts, small


def kernel(x, meta_tokens, ffn1_norm, ffn1_w_in, ffn1_w_out, mix_norm, w_in, gdn_conv_w, gdn_a_log, gdn_dt_bias, gdn_out_norm, ret_out_norm, w_branch_gdn, w_branch_ret, w_out, ffn2_norm, ffn2_w_in, ffn2_w_out, final_norm, loss_target, m_meta_tokens, m_ffn1_norm, m_ffn1_w_in, m_ffn1_w_out, m_mix_norm, m_w_in, m_gdn_conv_w, m_gdn_a_log, m_gdn_dt_bias, m_gdn_out_norm, m_ret_out_norm, m_w_branch_gdn, m_w_branch_ret, m_w_out, m_ffn2_norm, m_ffn2_w_in, m_ffn2_w_out, m_final_norm, v_meta_tokens, v_ffn1_norm, v_ffn1_w_in, v_ffn1_w_out, v_mix_norm, v_w_in, v_gdn_conv_w, v_gdn_a_log, v_gdn_dt_bias, v_gdn_out_norm, v_ret_out_norm, v_w_branch_gdn, v_w_branch_ret, v_w_out, v_ffn2_norm, v_ffn2_w_in, v_ffn2_w_out, v_final_norm):
    given = dict(locals())
    params = {n: _as2d(given[n]) for n in WEIGHTS}
    local = {n: params[n] for n in SHARDED}
    rep = {n: params[n] for n in REPLICATED}

    send = {n: local[n] if n in EXACT_F32 else local[n].astype(BF16) for n in SHARDED}
    loss_sum, grad_x, parts, small = _device_step(x[0], loss_target[0], send, rep)
    parts.update(zip(REPLICATED, _exchange([small[n] for n in REPLICATED], False, "gather_small_grads")))
    loss = lax.psum(loss_sum, ("x", "y", "c"))

    outs = {}
    for n in WEIGHTS:
        res = _adamw(params[n], parts[n], _as2d(given["m_" + n]), _as2d(given["v_" + n]), "adamw_" + n)
        outs[n] = [r.reshape(given[n].shape) for r in res]
    return (loss, grad_x[None], *[outs[n][0] for n in WEIGHTS], *[outs[n][1] for n in WEIGHTS],
            *[outs[n][2] for n in WEIGHTS], *[outs[n][3] for n in WEIGHTS])
```

```python
import functools
import math

import numpy as np
import jax
import jax.numpy as jnp
from jax import lax
from jax.experimental import pallas as pl
from jax.experimental.pallas import tpu as pltpu

F32 = jnp.float32
BF16 = jnp.bfloat16

N_DEV = 8
N_META = 16
CHUNK = 64
HEAD_DIM = 128
CONV_K = 4
ROPE_BASE = 10000.0
EPS = 1e-6
PAD_FRONT = 240
HEAD_ROWS = PAD_FRONT + N_META
LANES = 128
VMEM_LIMIT_BYTES = 56 * 1024 * 1024

ADAM_LR = 0.001
ADAM_B1 = 0.9
ADAM_B2 = 0.999
ADAM_EPS = 1e-08
ADAM_WD = 0.01
ADAM_STEP = 10

NN = (((1,), (0,)), ((), ()))
NT = (((1,), (1,)), ((), ()))
TN = (((0,), (0,)), ((), ()))


def _tile(n, target, mult):
    best = 0
    for t in range(mult, min(n, target) + 1, mult):
        if n % t == 0:
            best = t
    return best if best else n


def _params(*semantics):
    return pltpu.CompilerParams(dimension_semantics=semantics, vmem_limit_bytes=VMEM_LIMIT_BYTES)


def _split(a, pieces):
    out = []
    for _ in range(pieces - 1):
        part = a.astype(BF16)
        out.append(part)
        a = a - part.astype(F32)
    return out + [a.astype(BF16)]


def _raw_dot(a, b, dims, hi):
    dot = lambda x, y: lax.dot_general(x, y, dims, preferred_element_type=F32)
    if hi:
        (a_hi, a_lo), (b_hi, b_lo) = _split(a, 2), _split(b, 2)
        return dot(a_hi, b_hi) + (dot(a_hi, b_lo) + dot(a_lo, b_hi))
    return dot(a.astype(BF16), b.astype(BF16))


def _mask_dot(mask, x, dims):
    mask = mask.astype(BF16)
    hi, mid, lo = [lax.dot_general(mask, p, dims, preferred_element_type=F32) for p in _split(x, 3)]
    return hi + (mid + lo)


@jax.custom_vjp
def _cumsum_rows(x):
    c = x.shape[0]
    tril = lax.broadcasted_iota(jnp.int32, (c, c), 0) >= lax.broadcasted_iota(jnp.int32, (c, c), 1)
    return _mask_dot(tril, x, NN)


def _cumsum_rows_bwd(_, g):
    c = g.shape[0]
    tril = lax.broadcasted_iota(jnp.int32, (c, c), 0) >= lax.broadcasted_iota(jnp.int32, (c, c), 1)
    return (_mask_dot(tril, g, TN),)


_cumsum_rows.defvjp(lambda x: (_cumsum_rows(x), None), _cumsum_rows_bwd)


def _unit_lower_inverses(xs):
    c = xs[0].shape[0]
    eye = (lax.broadcasted_iota(jnp.int32, (c, c), 0) == lax.broadcasted_iota(jnp.int32, (c, c), 1)).astype(F32)
    t_inv = [eye + x for x in xs]
    for _ in range(int(math.log2(c)) - 1):
        xs = [_raw_dot(x, x, NN, True) for x in xs]
        t_inv = [t + _raw_dot(t, x, NN, True) for t, x in zip(t_inv, xs)]
    return t_inv


@jax.custom_vjp
def _known_inverse(x_neg, t_inv):
    return t_inv


_known_inverse.defvjp(
    lambda x_neg, t_inv: (t_inv, t_inv),
    lambda t_inv, g: (_raw_dot(_raw_dot(t_inv, g, TN, False), t_inv, NT, False), jnp.zeros_like(t_inv)))


def _make_mm(hi):
    @jax.custom_vjp
    def nn(a, b):
        return _raw_dot(a, b, NN, hi)

    @jax.custom_vjp
    def nt(a, b):
        return _raw_dot(a, b, NT, hi)

    @jax.custom_vjp
    def tn(a, b):
        return _raw_dot(a, b, TN, hi)

    nn.defvjp(lambda a, b: (_raw_dot(a, b, NN, hi), (a, b)),
              lambda r, g: (_raw_dot(g, r[1], NT, False), _raw_dot(r[0], g, TN, False)))
    nt.defvjp(lambda a, b: (_raw_dot(a, b, NT, hi), (a, b)),
              lambda r, g: (_raw_dot(g, r[1], NN, False), _raw_dot(g, r[0], TN, False)))
    tn.defvjp(lambda a, b: (_raw_dot(a, b, TN, hi), (a, b)),
              lambda r, g: (_raw_dot(r[1], g, NT, False), _raw_dot(r[0], g, NN, False)))
    return nn, nt, tn


def _silu(x):
    return x * jax.nn.sigmoid(x)


def _rms_parts(x):
    r = lax.rsqrt(jnp.mean(x * x, axis=-1, keepdims=True) + EPS)
    return x * r, r


def _rms_bwd(dy, xh, r, gain):
    dxh = dy * gain
    dx = r * (dxh - xh * jnp.mean(dxh * xh, axis=-1, keepdims=True))
    return dx, jnp.sum(dy * xh, axis=0, keepdims=True)


def _ffn_specs(tm, d, tf, nj):
    return [pl.BlockSpec((tm, d), lambda i, j: (i, 0)), pl.BlockSpec((1, d), lambda i, j: (0, 0)),
            pl.BlockSpec((1, d, tf), lambda i, j: (j, 0, 0)), pl.BlockSpec((1, d, tf), lambda i, j: (nj + j, 0, 0)),
            pl.BlockSpec((tf, d), lambda i, j: (j, 0))]


def _first_step(ndim):
    return lambda: functools.reduce(lambda a, b: a & b, [pl.program_id(k) == 0 for k in range(ndim)])


def _last_step(grid):
    return lambda: functools.reduce(lambda a, b: a & b, [pl.program_id(k) == g - 1 for k, g in enumerate(grid)])


def _ffn_fwd(h, gain, w_in, wo, name, carry=None):
    tp, d = h.shape
    tf = w_in.shape[2]
    nj = w_in.shape[0] // 2
    tm = _tile(tp, 768, 8)
    row, vec, wg_spec, wu_spec, wo_spec = _ffn_specs(tm, d, tf, nj)

    def body(h_ref, g_ref, wg3_ref, wu3_ref, wo_ref, o_ref, ag3_ref, au3_ref, n_sc, acc_sc):
        wg_ref, wu_ref = wg3_ref.at[0], wu3_ref.at[0]
        j = pl.program_id(1)

        @pl.when(j == 0)
        def _():
            xh, _ = _rms_parts(h_ref[...])
            n_sc[...] = (xh * g_ref[...]).astype(BF16)
            acc_sc[...] = jnp.zeros_like(acc_sc)

        n = n_sc[...]
        a_g = jnp.dot(n, wg_ref[...], preferred_element_type=F32)
        a_u = jnp.dot(n, wu_ref[...], preferred_element_type=F32)
        ag3_ref[0] = a_g
        au3_ref[0] = a_u
        hid = (_silu(a_g) * a_u).astype(BF16)
        acc_sc[...] += jnp.dot(hid, wo_ref[...], preferred_element_type=F32)

        @pl.when(j == nj - 1)
        def _():
            o_ref[...] = h_ref[...] + 0.5 * acc_sc[...]

    grid = (tp // tm, nj)
    act = pl.BlockSpec((1, tm, tf), lambda i, j: (j, i, 0))
    return _carried_call(
        body, carry, _first_step(2), _last_step(grid), name=name, grid=grid,
        in_specs=[row, vec, wg_spec, wu_spec, wo_spec], out_specs=[row, act, act],
        out_shape=[jax.ShapeDtypeStruct((tp, d), F32)] + [jax.ShapeDtypeStruct((nj, tp, tf), F32)] * 2,
        scratch_shapes=[pltpu.VMEM((tm, d), BF16), pltpu.VMEM((tm, d), F32)])(h, gain, w_in, w_in, wo)


def _ffn_bwd(h, dho, gain, w_in, wo, ag3, au3, name, carry=None):
    tp, d = h.shape
    tf = w_in.shape[2]
    nj = w_in.shape[0] // 2
    tm = _tile(tp, 528, 16)
    ni = tp // tm
    row, vec, wg_spec, wu_spec, wo_spec = _ffn_specs(tm, d, tf, nj)

    def body(h_ref, dho_ref, g_ref, wg3_ref, wu3_ref, wo_ref, ag3_ref, au3_ref,
             dh_ref, dgain_ref, n_ref, hid3_ref, dag3_ref, dau3_ref, dn_sc, dhb_sc):
        wg_ref, wu_ref = wg3_ref.at[0], wu3_ref.at[0]
        hid_ref, dag_ref, dau_ref = hid3_ref.at[0], dag3_ref.at[0], dau3_ref.at[0]
        i, j = pl.program_id(0), pl.program_id(1)

        @pl.when(j == 0)
        def _():
            xh, _ = _rms_parts(h_ref[...])
            n_ref[...] = (xh * g_ref[...]).astype(BF16)
            dn_sc[...] = jnp.zeros_like(dn_sc)
            dhb_sc[...] = (0.5 * dho_ref[...]).astype(BF16)

        @pl.when((i == 0) & (j == 0))
        def _():
            dgain_ref[...] = jnp.zeros_like(dgain_ref)

        a_g = ag3_ref[0]
        a_u = au3_ref[0]
        sg = jax.nn.sigmoid(a_g)
        s = a_g * sg
        hid_ref[...] = (s * a_u).astype(BF16)
        d_hid = lax.dot_general(dhb_sc[...], wo_ref[...], NT, preferred_element_type=F32)
        d_au = (d_hid * s).astype(BF16)
        d_ag = (d_hid * a_u * (sg * (1.0 + a_g * (1.0 - sg)))).astype(BF16)
        dau_ref[...] = d_au
        dag_ref[...] = d_ag
        dn_sc[...] += (lax.dot_general(d_ag, wg_ref[...], NT, preferred_element_type=F32)
                       + lax.dot_general(d_au, wu_ref[...], NT, preferred_element_type=F32))

        @pl.when(j == nj - 1)
        def _():
            xh, r = _rms_parts(h_ref[...])
            dx, dg = _rms_bwd(dn_sc[...], xh, r, g_ref[...])
            dh_ref[...] = dho_ref[...] + dx
            dgain_ref[...] += dg

    act = pl.BlockSpec((1, tm, tf), lambda i, j: (j, i, 0))
    return _carried_call(
        body, carry, _first_step(2), _last_step((ni, nj)), name=name, grid=(ni, nj),
        in_specs=[row, row, vec, wg_spec, wu_spec, wo_spec, act, act],
        out_specs=[row, vec, row, act, act, act],
        out_shape=[jax.ShapeDtypeStruct((tp, d), F32), jax.ShapeDtypeStruct((1, d), F32),
                   jax.ShapeDtypeStruct((tp, d), BF16)] + [jax.ShapeDtypeStruct((nj, tp, tf), BF16)] * 3,
        scratch_shapes=[pltpu.VMEM((tm, d), F32), pltpu.VMEM((tm, d), BF16)])(
            h, dho, gain, w_in, w_in, wo, ag3, au3)


def _matmul_tn(a, b, name, scale=1.0):
    t, m = a.shape
    n = b.shape[1]
    bm = _tile(m, 1024, LANES)
    bn = _tile(n, 1536, LANES)
    tk = _tile(t, 1408, 16)
    nk = t // tk

    def body(a_ref, b_ref, o_ref):
        k = pl.program_id(2)

        @pl.when(k == 0)
        def _():
            o_ref[...] = jnp.zeros_like(o_ref)

        o_ref[...] += lax.dot_general(a_ref[...].astype(BF16), b_ref[...].astype(BF16), TN,
                                      preferred_element_type=F32)

        if scale != 1.0:
            @pl.when(k == nk - 1)
            def _():
                o_ref[...] = o_ref[...] * scale

    return pl.pallas_call(
        body, name=name, grid=(m // bm, n // bn, nk),
        in_specs=[pl.BlockSpec((tk, bm), lambda i, j, k: (k, i)), pl.BlockSpec((tk, bn), lambda i, j, k: (k, j))],
        out_specs=pl.BlockSpec((bm, bn), lambda i, j, k: (i, j)),
        out_shape=jax.ShapeDtypeStruct((m, n), F32),
        compiler_params=_params("parallel", "parallel", "arbitrary"))(a, b)


def _matmul_tn_blocks(a, b, name, scale=1.0, carry=None):
    a_blocked = a.ndim == 3
    nb, t = (a.shape[0], a.shape[1]) if a_blocked else (b.shape[0], b.shape[1])
    m, n = a.shape[-1], b.shape[-1]
    tk = _tile(t, 1408, 16)
    nk = t // tk
    if a_blocked:
        bo = _tile(n, 1024, LANES)
        a_spec = pl.BlockSpec((1, tk, m), lambda p, o, k: (p, k, 0))
        b_spec = pl.BlockSpec((tk, bo), lambda p, o, k: (k, o))
        o_spec = pl.BlockSpec((m, bo), lambda p, o, k: (p, o))
        out_shape = jax.ShapeDtypeStruct((nb * m, n), F32)
        grid = (nb, n // bo, nk)
    else:
        bo = _tile(m, 1024, LANES)
        a_spec = pl.BlockSpec((tk, bo), lambda p, o, k: (k, o))
        b_spec = pl.BlockSpec((1, tk, n), lambda p, o, k: (p, k, 0))
        o_spec = pl.BlockSpec((1, bo, n), lambda p, o, k: (p, o, 0))
        out_shape = jax.ShapeDtypeStruct((nb, m, n), F32)
        grid = (nb, m // bo, nk)

    def body(a_ref, b_ref, o_ref):
        k = pl.program_id(2)
        a_blk = a_ref[0] if a_blocked else a_ref[...]
        b_blk = b_ref[...] if a_blocked else b_ref[0]
        part = lax.dot_general(a_blk.astype(BF16), b_blk.astype(BF16), TN, preferred_element_type=F32)
        out = o_ref if a_blocked else o_ref.at[0]

        @pl.when(k == 0)
        def _():
            out[...] = part

        @pl.when(k > 0)
        def _():
            out[...] += part

        if scale != 1.0:
            @pl.when(k == nk - 1)
            def _():
                out[...] = out[...] * scale

    (out,), moved = _carried_call(body, carry, _first_step(3), _last_step(grid), name=name, grid=grid,
                                  in_specs=[a_spec, b_spec], out_specs=[o_spec], out_shape=[out_shape])(a, b)
    return out if carry is None else (out, moved)


def _matmul_nt_parts(parts, w, acc, name, carry=None):
    t = parts[0].shape[0]
    d = w.shape[0]
    widths = [p.shape[1] for p in parts]
    tk = _tile(math.gcd(*widths), 1024, LANES)
    counts = [wd // tk for wd in widths]
    starts = [sum(counts[:g]) for g in range(len(parts))]
    nk = sum(counts)
    tm = _tile(t, 768, 8)
    n_parts = len(parts)

    def body(*refs):
        a_refs, w_ref, o_ref = refs[:n_parts], refs[n_parts], refs[-1]
        k = pl.program_id(1)

        @pl.when(k == 0)
        def _():
            o_ref[...] = jnp.zeros_like(o_ref) if acc is None else refs[n_parts + 1][...]

        for g in range(n_parts):
            @pl.when((k >= starts[g]) & (k < starts[g] + counts[g]))
            def _(g=g):
                o_ref[...] += lax.dot_general(a_refs[g][...].astype(BF16), w_ref[...], NT,
                                              preferred_element_type=F32)

    in_specs = [pl.BlockSpec((tm, tk), lambda i, k, lo=starts[g], nb=counts[g]: (i, jnp.clip(k - lo, 0, nb - 1)))
                for g in range(n_parts)]
    in_specs.append(pl.BlockSpec((d, tk), lambda i, k: (0, k)))
    args = list(parts) + [w]
    if acc is not None:
        in_specs.append(pl.BlockSpec((tm, d), lambda i, k: (i, 0)))
        args.append(acc)
    grid = (t // tm, nk)
    (out,), moved = _carried_call(
        body, carry, _first_step(2), _last_step(grid), name=name, grid=grid, in_specs=in_specs,
        out_specs=[pl.BlockSpec((tm, d), lambda i, k: (i, 0))],
        out_shape=[jax.ShapeDtypeStruct((t, d), F32)])(*args)
    return out, moved


def _proj_fwd(h, gain, wp, name, carry=None):
    tp, d = h.shape
    npad = wp.shape[1]
    tm = _tile(tp, 768, 8)
    tn = _tile(npad, 3456, LANES)

    def body(h_ref, g_ref, w_ref, o_ref, n_ref):
        @pl.when(pl.program_id(1) == 0)
        def _():
            xh, _ = _rms_parts(h_ref[...])
            n_ref[...] = (xh * g_ref[...]).astype(BF16)

        o_ref[...] = jnp.dot(n_ref[...], w_ref[...], preferred_element_type=F32)

    grid = (tp // tm, npad // tn)
    return _carried_call(
        body, carry, _first_step(2), _last_step(grid), name=name, grid=grid,
        in_specs=[pl.BlockSpec((tm, d), lambda i, j: (i, 0)), pl.BlockSpec((1, d), lambda i, j: (0, 0)),
                  pl.BlockSpec((d, tn), lambda i, j: (0, j))],
        out_specs=[pl.BlockSpec((tm, tn), lambda i, j: (i, j)), pl.BlockSpec((tm, d), lambda i, j: (i, 0))],
        out_shape=[jax.ShapeDtypeStruct((tp, npad), F32), jax.ShapeDtypeStruct((tp, d), BF16)])(h, gain, wp)


def _norm_bwd(h, gain, dn, dres, name):
    tp, d = h.shape
    tm = _tile(tp, 256, 8)

    def body(h_ref, g_ref, dn_ref, dres_ref, dh_ref, dgain_ref):
        @pl.when(pl.program_id(0) == 0)
        def _():
            dgain_ref[...] = jnp.zeros_like(dgain_ref)

        xh, r = _rms_parts(h_ref[...])
        dx, dg = _rms_bwd(dn_ref[...], xh, r, g_ref[...])
        dh_ref[...] = dres_ref[...] + dx
        dgain_ref[...] += dg

    row = pl.BlockSpec((tm, d), lambda i: (i, 0))
    vec = pl.BlockSpec((1, d), lambda i: (0, 0))
    return pl.pallas_call(
        body, name=name, grid=(tp // tm,), in_specs=[row, vec, row, row], out_specs=[row, vec],
        out_shape=[jax.ShapeDtypeStruct((tp, d), F32), jax.ShapeDtypeStruct((1, d), F32)],
        compiler_params=_params("arbitrary"))(h, gain, dn, dres)


def _head_post(a, grp):
    a = _silu(a)
    r = lax.rsqrt(jnp.sum(a * a, axis=-1, keepdims=True) + EPS)
    if isinstance(grp, int):
        return a if grp == 2 else a * r * (HEAD_DIM ** -0.5 if grp == 0 else 1.0)
    scale = jnp.where(grp == 0, HEAD_DIM ** -0.5, 1.0).astype(F32)
    return jnp.where(grp == 2, a, a * r * scale)


def _head_post_bwd(c, dy, grp):
    sg = jax.nn.sigmoid(c)
    a = c * sg
    dsilu = sg * (1.0 + c * (1.0 - sg))
    if grp == 2:
        return dy * dsilu
    r = lax.rsqrt(jnp.sum(a * a, axis=-1, keepdims=True) + EPS)
    scale = HEAD_DIM ** -0.5 if grp == 0 else 1.0
    da = (scale * r) * (dy - a * (r * r * jnp.sum(dy * a, axis=-1, keepdims=True)))
    return da * dsilu


def _conv_taps(ext_sc, w_ref, tm):
    c = None
    for i in range(CONV_K):
        s = CONV_K - 1 - i
        term = w_ref[i:i + 1, :] * ext_sc[8 - s:8 - s + tm, :]
        c = term if c is None else c + term
    return c


def _conv_fwd(proj, conv_w, hv, name):
    tp = proj.shape[0]
    tm = _tile(tp, 256, 8)
    nh = hv // HEAD_DIM

    def body(x_ref, halo_ref, w_ref, o_ref, ext_sc):
        i, grp = pl.program_id(0), pl.program_id(1)
        ext_sc[0:8, :] = jnp.where(i == 0, 0.0, halo_ref[...])
        ext_sc[8:, :] = x_ref[...]
        c = _conv_taps(ext_sc, w_ref, tm)
        for h in range(nh):
            sl = slice(h * HEAD_DIM, (h + 1) * HEAD_DIM)
            o_ref[:, sl] = _head_post(c[:, sl], grp)

    return pl.pallas_call(
        body, name=name, grid=(tp // tm, 3),
        in_specs=[pl.BlockSpec((tm, hv), lambda i, g: (i, g)),
                  pl.BlockSpec((8, hv), lambda i, g: (jnp.maximum(i * (tm // 8) - 1, 0), g)),
                  pl.BlockSpec((CONV_K, hv), lambda i, g: (0, g))],
        out_specs=pl.BlockSpec((tm, hv), lambda i, g: (i, g)),
        out_shape=jax.ShapeDtypeStruct((tp, 3 * hv), F32),
        scratch_shapes=[pltpu.VMEM((tm + 8, hv), F32)],
        compiler_params=_params("parallel", "arbitrary"))(proj, proj, conv_w)


def _conv_bwd(proj, conv_w, dy, grp, hv, name):
    tp = proj.shape[0]
    tm = _tile(tp, 256, 8)
    ni = tp // tm
    nh = hv // HEAD_DIM

    def body(x_ref, halo_ref, w_ref, dy_ref, dx_ref, dw_ref, ext_sc, dc_sc):
        step = pl.program_id(0)
        ext_sc[0:8, :] = jnp.where(step == ni - 1, 0.0, halo_ref[...])
        ext_sc[8:, :] = x_ref[...]
        c = _conv_taps(ext_sc, w_ref, tm)
        @pl.when(step == 0)
        def _():
            dc_sc[tm:, :] = jnp.zeros((8, hv), F32)

        @pl.when(step > 0)
        def _():
            dc_sc[tm:, :] = dc_sc[0:8, :]

        for h in range(nh):
            sl = slice(h * HEAD_DIM, (h + 1) * HEAD_DIM)
            dc_sc[0:tm, sl] = _head_post_bwd(c[:, sl], dy_ref[:, sl], grp)

        @pl.when(step == 0)
        def _():
            dw_ref[...] = jnp.zeros_like(dw_ref)

        dc = dc_sc[0:tm, :]
        dx = None
        for k in range(CONV_K):
            s = CONV_K - 1 - k
            dw_ref[k:k + 1, :] += jnp.sum(dc * ext_sc[8 - s:8 - s + tm, :], axis=0, keepdims=True)
            term = w_ref[k:k + 1, :] * dc_sc[s:s + tm, :]
            dx = term if dx is None else dx + term
        dx_ref[...] = dx.astype(BF16)

    tile = lambda step: ni - 1 - step
    return pl.pallas_call(
        body, name=name, grid=(ni,),
        in_specs=[pl.BlockSpec((tm, hv), lambda s: (tile(s), grp)),
                  pl.BlockSpec((8, hv), lambda s: (jnp.maximum(tile(s) * (tm // 8) - 1, 0), grp)),
                  pl.BlockSpec((CONV_K, hv), lambda s: (0, grp)),
                  pl.BlockSpec((tm, hv), lambda s: (tile(s), 0))],
        out_specs=[pl.BlockSpec((tm, hv), lambda s: (tile(s), 0)), pl.BlockSpec((CONV_K, hv), lambda s: (0, 0))],
        out_shape=[jax.ShapeDtypeStruct((tp, hv), BF16), jax.ShapeDtypeStruct((CONV_K, hv), F32)],
        scratch_shapes=[pltpu.VMEM((tm + 8, hv), F32), pltpu.VMEM((tm + 8, hv), F32)],
        compiler_params=_params("arbitrary"))(proj, proj, conv_w, dy)


def _gdn_gates(ba, alog, dtb):
    x = ba + dtb
    softplus = jnp.maximum(x, 0.0) + jnp.log1p(jnp.exp(-jnp.abs(x)))
    return _cumsum_rows(-jnp.exp(alog) * softplus), jax.nn.sigmoid(ba)


def _gdn_chunks(states, qs, ks, vs, gates, known_inverses=None):
    mm_nn, mm_nt, mm_tn = _make_mm(False)
    hi_nn, _, _ = _make_mm(True)
    nh = len(states)
    items = range(len(qs))
    head = [i % nh for i in items]
    c = qs[0].shape[0]
    lane = lax.broadcasted_iota(jnp.int32, (c, LANES), 1)
    last_row = (lax.broadcasted_iota(jnp.int32, (c, 1), 0) == c - 1).astype(F32)
    ri = lax.broadcasted_iota(jnp.int32, (c, c), 0)
    ci = lax.broadcasted_iota(jnp.int32, (c, c), 1)
    causal = ri >= ci
    strict = ri > ci
    eye = (ri == ci).astype(F32)
    sel_a = [(lane == nh + h).astype(F32) for h in range(nh)]
    sel_b = [(lane == h).astype(F32) for h in range(nh)]

    gcol = [jnp.sum(gates[i // nh][0] * sel_a[head[i]], axis=1, keepdims=True) for i in items]
    grow = [jnp.sum(eye * gcol[i], axis=0, keepdims=True) for i in items]
    beta = [jnp.sum(gates[i // nh][1] * sel_b[head[i]], axis=1, keepdims=True) for i in items]
    decay = [jnp.where(causal, jnp.exp(jnp.where(causal, gcol[i] - grow[i], 0.0)), 0.0) for i in items]
    kb = [ks[i] * beta[i] for i in items]
    kk = [mm_nt(kb[i], ks[i]) for i in items]
    qk = [mm_nt(qs[i], ks[i]) for i in items]
    x_neg = [-jnp.where(strict, kk[i] * decay[i], 0.0) for i in items]
    if known_inverses is None:
        t_inv = _unit_lower_inverses(x_neg)
    else:
        t_inv = [_known_inverse(x_neg[i], known_inverses[i]) for i in items]
    eg = [jnp.exp(gcol[i]) for i in items]
    u = [hi_nn(t_inv[i], vs[i] * beta[i]) for i in items]
    w = [hi_nn(t_inv[i], kb[i] * eg[i]) for i in items]
    qk = [qk[i] * decay[i] for i in items]
    glast = [jnp.sum(gcol[i] * last_row, axis=0, keepdims=True) for i in items]
    q_dec = [qs[i] * eg[i] for i in items]
    k_dec = [ks[i] * jnp.exp(glast[i] - gcol[i]) for i in items]
    s_dec = [jnp.exp(glast[i]) for i in items]

    outs = []
    for first in range(0, len(qs), nh):
        chunk = range(first, first + nh)
        ws = [mm_nn(w[i], states[i - first]) for i in chunk]
        from_state = [mm_nn(q_dec[i], states[i - first]) for i in chunk]
        v_new = [u[i] - ws[i - first] for i in chunk]
        intra = [mm_nn(qk[i], v_new[i - first]) for i in chunk]
        kv = [mm_tn(k_dec[i], v_new[i - first]) for i in chunk]
        outs += [from_state[i - first] + intra[i - first] for i in chunk]
        states = [states[i - first] * s_dec[i] + kv[i - first] for i in chunk]
    return outs, states, t_inv


SCAN_CHUNKS = 2


def _scan_specs(nh, steps, rev, first_col):
    sidx = (lambda s: steps - 1 - s) if rev else (lambda s: s)
    hv = nh * HEAD_DIM
    rows = SCAN_CHUNKS * CHUNK
    cols = [pl.BlockSpec((rows, hv), lambda s, g=g: (sidx(s), first_col + g)) for g in range(3)]
    st = pl.BlockSpec((1, nh, HEAD_DIM, HEAD_DIM), lambda s: (sidx(s), 0, 0, 0))
    act = pl.BlockSpec((rows, hv), lambda s: (sidx(s), 0))
    return cols, st, act


def _chunk_heads(ref, nh):
    return [ref[j * CHUNK:(j + 1) * CHUNK, h * HEAD_DIM:(h + 1) * HEAD_DIM] for j in range(SCAN_CHUNKS)
            for h in range(nh)]


def _store_chunk_heads(ref, values, nh, dtype=None):
    for i, val in enumerate(values):
        j, h = divmod(i, nh)
        ref[j * CHUNK:(j + 1) * CHUNK, h * HEAD_DIM:(h + 1) * HEAD_DIM] = val if dtype is None else val.astype(dtype)


def _gdn_fwd(qkv, proj, alog, dtb, nh, name):
    tp = qkv.shape[0]
    steps = tp // (SCAN_CHUNKS * CHUNK)
    rows = SCAN_CHUNKS * CHUNK

    def body(q_ref, k_ref, v_ref, ba_ref, al_ref, dt_ref, o_ref, st_ref, inv_ref, s_sc):
        @pl.when(pl.program_id(0) == 0)
        def _():
            s_sc[...] = jnp.zeros_like(s_sc)

        gates = [_gdn_gates(ba_ref[j * CHUNK:(j + 1) * CHUNK, :], al_ref[...], dt_ref[...])
                 for j in range(SCAN_CHUNKS)]
        states = [s_sc[h] for h in range(nh)]
        for h in range(nh):
            st_ref[0, h] = states[h]
        outs, new_states, t_inv = _gdn_chunks(states, _chunk_heads(q_ref, nh), _chunk_heads(k_ref, nh),
                                              _chunk_heads(v_ref, nh), gates)
        _store_chunk_heads(o_ref, outs, nh)
        for h in range(nh):
            s_sc[h] = new_states[h]
        for i, t in enumerate(t_inv):
            inv_ref[0, i] = t

    cols, st, act = _scan_specs(nh, steps, False, 0)
    ba = pl.BlockSpec((rows, LANES), lambda s: (s, 10 * nh * HEAD_DIM // LANES))
    vec = pl.BlockSpec((1, LANES), lambda s: (0, 0))
    inv = pl.BlockSpec((1, SCAN_CHUNKS * nh, CHUNK, CHUNK), lambda s: (s, 0, 0, 0))
    return pl.pallas_call(
        body, name=name, grid=(steps,), in_specs=cols + [ba, vec, vec], out_specs=[act, st, inv],
        out_shape=[jax.ShapeDtypeStruct((tp, nh * HEAD_DIM), F32),
                   jax.ShapeDtypeStruct((steps, nh, HEAD_DIM, HEAD_DIM), F32),
                   jax.ShapeDtypeStruct((steps, SCAN_CHUNKS * nh, CHUNK, CHUNK), F32)],
        scratch_shapes=[pltpu.VMEM((nh, HEAD_DIM, HEAD_DIM), F32)],
        compiler_params=_params("arbitrary"))(qkv, qkv, qkv, proj, alog, dtb)


def _gdn_bwd(qkv, proj, alog, dtb, states, inverses, do, nh, name):
    tp = qkv.shape[0]
    steps = tp // (SCAN_CHUNKS * CHUNK)
    rows = SCAN_CHUNKS * CHUNK

    def body(q_ref, k_ref, v_ref, ba_ref, al_ref, dt_ref, st_ref, inv_ref, do_ref,
             dq_ref, dk_ref, dv_ref, dba_ref, dal_ref, ddt_ref, ds_sc):
        @pl.when(pl.program_id(0) == 0)
        def _():
            ds_sc[...] = jnp.zeros_like(ds_sc)
            dal_ref[...] = jnp.zeros_like(dal_ref)
            ddt_ref[...] = jnp.zeros_like(ddt_ref)

        gates, gates_vjps = [], []
        for j in range(SCAN_CHUNKS):
            g, g_vjp = jax.vjp(_gdn_gates, ba_ref[j * CHUNK:(j + 1) * CHUNK, :], al_ref[...], dt_ref[...])
            gates.append(g)
            gates_vjps.append(g_vjp)
        known = [inv_ref[0, i] for i in range(SCAN_CHUNKS * nh)]
        fn = lambda s, q, k, v, g: _gdn_chunks(s, q, k, v, g, known)[:2]
        _, vjp = jax.vjp(fn, [st_ref[0, h] for h in range(nh)], _chunk_heads(q_ref, nh), _chunk_heads(k_ref, nh),
                         _chunk_heads(v_ref, nh), gates)
        ds, dq, dk, dv, dgates = vjp((_chunk_heads(do_ref, nh), [ds_sc[h] for h in range(nh)]))
        for h in range(nh):
            ds_sc[h] = ds[h]
        _store_chunk_heads(dq_ref, dq, nh)
        _store_chunk_heads(dk_ref, dk, nh)
        _store_chunk_heads(dv_ref, dv, nh)
        for j in range(SCAN_CHUNKS):
            dba, dal, ddt = gates_vjps[j](dgates[j])
            dba_ref[j * CHUNK:(j + 1) * CHUNK, :] = dba
            dal_ref[...] += dal
            ddt_ref[...] += ddt

    cols, st, act = _scan_specs(nh, steps, True, 0)
    ba = pl.BlockSpec((rows, LANES), lambda s: (steps - 1 - s, 10 * nh * HEAD_DIM // LANES))
    vec = pl.BlockSpec((1, LANES), lambda s: (0, 0))
    inv = pl.BlockSpec((1, SCAN_CHUNKS * nh, CHUNK, CHUNK), lambda s: (steps - 1 - s, 0, 0, 0))
    return pl.pallas_call(
        body, name=name, grid=(steps,), in_specs=cols + [ba, vec, vec, st, inv, act],
        out_specs=[act, act, act, pl.BlockSpec((rows, LANES), lambda s: (steps - 1 - s, 0)), vec, vec],
        out_shape=[jax.ShapeDtypeStruct((tp, nh * HEAD_DIM), F32)] * 3
                  + [jax.ShapeDtypeStruct((tp, LANES), F32), jax.ShapeDtypeStruct((1, LANES), F32),
                     jax.ShapeDtypeStruct((1, LANES), F32)],
        scratch_shapes=[pltpu.VMEM((nh, HEAD_DIM, HEAD_DIM), F32)],
        compiler_params=_params("arbitrary"))(qkv, qkv, qkv, proj, alog, dtb, states, inverses, do)


def _swap_pairs(t):
    lane = lax.broadcasted_iota(jnp.int32, t.shape, 1)
    n = t.shape[1]
    return jnp.where(lane % 2 == 0, pltpu.roll(t, n - 1, 1), pltpu.roll(t, 1, 1))


def _rot(t, cos, sin_signed):
    return t * cos + _swap_pairs(t) * sin_signed


def _rot_t(dt, cos, sin_signed):
    return dt * cos + _swap_pairs(dt * sin_signed)


def _ret_chunks(states, qs, ks, vs, dec, xi, zeta, cd):
    mm_nn, mm_nt, mm_tn = _make_mm(False)
    nh = len(states)
    items = range(len(qs))
    scores = [mm_nt(qs[i], ks[i]) for i in items]
    kv = [mm_tn(ks[i] * zeta[i % nh], vs[i]) for i in items]
    intra = [mm_nn(scores[i] * dec[i % nh], vs[i]) for i in items]
    q_dec = [qs[i] * xi[i % nh] for i in items]
    outs = []
    for first in range(0, len(qs), nh):
        outs += [intra[first + h] + mm_nn(q_dec[first + h], states[h]) for h in range(nh)]
        states = [states[h] * cd[h] + kv[first + h] for h in range(nh)]
    return outs, states


def _ret_table_specs(nh, steps, rev):
    sidx = (lambda s: steps - 1 - s) if rev else (lambda s: s)
    rope = pl.BlockSpec((SCAN_CHUNKS * CHUNK, HEAD_DIM), lambda s: (sidx(s), 0))
    dec = pl.BlockSpec((nh, CHUNK, CHUNK), lambda s: (0, 0, 0))
    tab = pl.BlockSpec((nh, CHUNK, HEAD_DIM), lambda s: (0, 0, 0))
    cd = pl.BlockSpec((nh, 8, HEAD_DIM), lambda s: (0, 0, 0))
    return [rope, rope, dec, tab, tab, cd]


def _rotated(ref, cos_ref, sin_ref, nh, scale=1.0):
    out = []
    for j in range(SCAN_CHUNKS):
        rows = slice(j * CHUNK, (j + 1) * CHUNK)
        cos_t, sin_t = cos_ref[rows, :], sin_ref[rows, :]
        for h in range(nh):
            t = _rot(ref[rows, h * HEAD_DIM:(h + 1) * HEAD_DIM], cos_t, sin_t)
            out.append(t if scale == 1.0 else t * scale)
    return out


def _ret_fwd(proj, cos, sin, dec, xi, zeta, cd, nh, name):
    tp = proj.shape[0]
    steps = tp // (SCAN_CHUNKS * CHUNK)
    kscale = HEAD_DIM ** -0.5

    def body(q_ref, k_ref, v_ref, cos_ref, sin_ref, dec_ref, xi_ref, zeta_ref, cd_ref, o_ref, st_ref, s_sc):
        @pl.when(pl.program_id(0) == 0)
        def _():
            s_sc[...] = jnp.zeros_like(s_sc)

        heads = range(nh)
        states = [s_sc[h] for h in heads]
        for h in heads:
            st_ref[0, h] = states[h]
        outs, new_states = _ret_chunks(
            states, _rotated(q_ref, cos_ref, sin_ref, nh), _rotated(k_ref, cos_ref, sin_ref, nh, kscale),
            _chunk_heads(v_ref, nh), [dec_ref[h] for h in heads], [xi_ref[h] for h in heads],
            [zeta_ref[h] for h in heads], [cd_ref[h][0:1, :] for h in heads])
        _store_chunk_heads(o_ref, outs, nh)
        for h in heads:
            s_sc[h] = new_states[h]

    cols, st, act = _scan_specs(nh, steps, False, 3)
    return pl.pallas_call(
        body, name=name, grid=(steps,), in_specs=cols + _ret_table_specs(nh, steps, False), out_specs=[act, st],
        out_shape=[jax.ShapeDtypeStruct((tp, nh * HEAD_DIM), F32),
                   jax.ShapeDtypeStruct((steps, nh, HEAD_DIM, HEAD_DIM), F32)],
        scratch_shapes=[pltpu.VMEM((nh, HEAD_DIM, HEAD_DIM), F32)],
        compiler_params=_params("arbitrary"))(proj, proj, proj, cos, sin, dec, xi, zeta, cd)


def _ret_bwd(proj, cos, sin, dec, xi, zeta, cd, states, do, nh, name):
    tp = proj.shape[0]
    steps = tp // (SCAN_CHUNKS * CHUNK)
    kscale = HEAD_DIM ** -0.5

    def body(q_ref, k_ref, v_ref, cos_ref, sin_ref, dec_ref, xi_ref, zeta_ref, cd_ref, st_ref, do_ref,
             dq_ref, dk_ref, dv_ref, ds_sc):
        @pl.when(pl.program_id(0) == 0)
        def _():
            ds_sc[...] = jnp.zeros_like(ds_sc)

        heads = range(nh)
        fn = functools.partial(_ret_chunks, dec=[dec_ref[h] for h in heads], xi=[xi_ref[h] for h in heads],
                               zeta=[zeta_ref[h] for h in heads], cd=[cd_ref[h][0:1, :] for h in heads])
        _, vjp = jax.vjp(fn, [st_ref[0, h] for h in heads], _rotated(q_ref, cos_ref, sin_ref, nh),
                         _rotated(k_ref, cos_ref, sin_ref, nh, kscale), _chunk_heads(v_ref, nh))
        ds, dq, dk, dv = vjp((_chunk_heads(do_ref, nh), [ds_sc[h] for h in heads]))
        for h in heads:
            ds_sc[h] = ds[h]
        for i in range(SCAN_CHUNKS * nh):
            rows = slice((i // nh) * CHUNK, (i // nh + 1) * CHUNK)
            cos_t, sin_t = cos_ref[rows, :], sin_ref[rows, :]
            dq[i] = _rot_t(dq[i], cos_t, sin_t)
            dk[i] = _rot_t(dk[i] * kscale, cos_t, sin_t)
        _store_chunk_heads(dq_ref, dq, nh, BF16)
        _store_chunk_heads(dk_ref, dk, nh, BF16)
        _store_chunk_heads(dv_ref, dv, nh, BF16)

    cols, st, act = _scan_specs(nh, steps, True, 3)
    return pl.pallas_call(
        body, name=name, grid=(steps,), in_specs=cols + _ret_table_specs(nh, steps, True) + [st, act],
        out_specs=[act, act, act],
        out_shape=[jax.ShapeDtypeStruct((tp, nh * HEAD_DIM), BF16)] * 3,
        scratch_shapes=[pltpu.VMEM((nh, HEAD_DIM, HEAD_DIM), F32)],
        compiler_params=_params("arbitrary"))(proj, proj, proj, cos, sin, dec, xi, zeta, cd, states, do)


def _gdn_out(o, z, gnorm):
    return o * lax.rsqrt(jnp.mean(o * o, axis=-1, keepdims=True) + EPS) * gnorm * _silu(z)


def _ret_out(o, rg, rnorm):
    mu = jnp.mean(o, axis=-1, keepdims=True)
    var = jnp.mean(jnp.square(o - mu), axis=-1, keepdims=True)
    return _silu(rg) * ((o - mu) * lax.rsqrt(var + EPS) * rnorm)


def _dsilu(x, sg):
    return sg * (1.0 + x * (1.0 - sg))


def _gdn_out_bwd(o, z, gnorm, dy):
    r = lax.rsqrt(jnp.mean(o * o, axis=-1, keepdims=True) + EPS)
    xh = o * r
    sg = jax.nn.sigmoid(z)
    sz = z * sg
    t = dy * (gnorm * sz)
    do = r * (t - xh * jnp.mean(t * xh, axis=-1, keepdims=True))
    e = dy * xh
    return do, e * (gnorm * _dsilu(z, sg)), jnp.sum(e * sz, axis=0, keepdims=True)


def _ret_out_bwd(o, rg, rnorm, dy):
    oc = o - jnp.mean(o, axis=-1, keepdims=True)
    rs = lax.rsqrt(jnp.mean(oc * oc, axis=-1, keepdims=True) + EPS)
    xh = oc * rs
    sg = jax.nn.sigmoid(rg)
    srg = rg * sg
    t = dy * (rnorm * srg)
    do = rs * (t - jnp.mean(t, axis=-1, keepdims=True) - xh * jnp.mean(t * xh, axis=-1, keepdims=True))
    e = dy * xh
    return do, e * (rnorm * _dsilu(rg, sg)), jnp.sum(e * srg, axis=0, keepdims=True)


def _post_specs(tm, hv, d):
    row = lambda col: pl.BlockSpec((tm, hv), lambda i: (i, col))
    return dict(
        oa=row(0), ob=row(0), z=row(6), rg=row(7), ga=row(8), gb=row(9),
        gnorm=pl.BlockSpec((1, HEAD_DIM), lambda i: (0, 0)), rnorm=pl.BlockSpec((1, hv), lambda i: (0, 0)),
        w=pl.BlockSpec((hv, d), lambda i: (0, 0)), res=pl.BlockSpec((tm, d), lambda i: (i, 0)))


def _post_fwd(oa, ob, proj, gnorm, rnorm, wbg, wbr, wo, h1, name):
    tp, d = h1.shape
    hv = oa.shape[1]
    nh = hv // HEAD_DIM
    tm = _tile(tp, 256, 8)

    def body(oa_ref, ob_ref, z_ref, rg_ref, ga_ref, gb_ref, gn_ref, rn_ref, wbg_ref, wbr_ref, wo_ref, h_ref,
             o_ref, ya_sc, yb_sc):
        for h in range(nh):
            sl = slice(h * HEAD_DIM, (h + 1) * HEAD_DIM)
            ya_sc[:, sl] = _gdn_out(oa_ref[:, sl], z_ref[:, sl], gn_ref[...]).astype(BF16)
            yb_sc[:, sl] = _ret_out(ob_ref[:, sl], rg_ref[:, sl], rn_ref[:, sl]).astype(BF16)
        pa = jnp.dot(ya_sc[...], wbg_ref[...], preferred_element_type=F32)
        pb = jnp.dot(yb_sc[...], wbr_ref[...], preferred_element_type=F32)
        merged = jax.nn.sigmoid(ga_ref[...]) * pa + jax.nn.sigmoid(gb_ref[...]) * pb
        o_ref[...] = h_ref[...] + jnp.dot(merged.astype(BF16), wo_ref[...], preferred_element_type=F32)

    sp = _post_specs(tm, hv, d)
    return pl.pallas_call(
        body, name=name, grid=(tp // tm,),
        in_specs=[sp["oa"], sp["ob"], sp["z"], sp["rg"], sp["ga"], sp["gb"], sp["gnorm"], sp["rnorm"],
                  sp["w"], sp["w"], sp["w"], sp["res"]],
        out_specs=sp["res"], out_shape=jax.ShapeDtypeStruct((tp, d), F32),
        scratch_shapes=[pltpu.VMEM((tm, hv), BF16), pltpu.VMEM((tm, hv), BF16)],
        compiler_params=_params("parallel"))(oa, ob, proj, proj, proj, proj, gnorm, rnorm, wbg, wbr, wo, h1)


def _post_bwd(oa, ob, proj, gnorm, rnorm, wbg, wbr, wo, dh2, name):
    tp, d = dh2.shape
    hv = oa.shape[1]
    nh = hv // HEAD_DIM
    tm = _tile(tp, 256, 8)

    def body(oa_ref, ob_ref, z_ref, rg_ref, ga_ref, gb_ref, gn_ref, rn_ref, wbg_ref, wbr_ref, wo_ref, dh_ref,
             doa_ref, dob_ref, dg_ref, ya_ref, yb_ref, mg_ref, dpa_ref, dpb_ref, dgn_ref, drn_ref,
             dya_sc, dyb_sc):
        @pl.when(pl.program_id(0) == 0)
        def _():
            dgn_ref[...] = jnp.zeros_like(dgn_ref)
            drn_ref[...] = jnp.zeros_like(drn_ref)

        for h in range(nh):
            sl = slice(h * HEAD_DIM, (h + 1) * HEAD_DIM)
            ya_ref[:, sl] = _gdn_out(oa_ref[:, sl], z_ref[:, sl], gn_ref[...]).astype(BF16)
            yb_ref[:, sl] = _ret_out(ob_ref[:, sl], rg_ref[:, sl], rn_ref[:, sl]).astype(BF16)
        pa = jnp.dot(ya_ref[...], wbg_ref[...], preferred_element_type=F32)
        pb = jnp.dot(yb_ref[...], wbr_ref[...], preferred_element_type=F32)
        sa = jax.nn.sigmoid(ga_ref[...])
        sb = jax.nn.sigmoid(gb_ref[...])
        mg_ref[...] = (sa * pa + sb * pb).astype(BF16)
        dm = lax.dot_general(dh_ref[...].astype(BF16), wo_ref[...], NT, preferred_element_type=F32)
        dpa = (dm * sa).astype(BF16)
        dpb = (dm * sb).astype(BF16)
        dpa_ref[...] = dpa
        dpb_ref[...] = dpb
        dg_ref[:, 2 * hv:3 * hv] = (dm * pa * sa * (1.0 - sa)).astype(BF16)
        dg_ref[:, 3 * hv:4 * hv] = (dm * pb * sb * (1.0 - sb)).astype(BF16)
        dya_sc[...] = lax.dot_general(dpa, wbg_ref[...], NT, preferred_element_type=F32)
        dyb_sc[...] = lax.dot_general(dpb, wbr_ref[...], NT, preferred_element_type=F32)
        for h in range(nh):
            sl = slice(h * HEAD_DIM, (h + 1) * HEAD_DIM)
            doa, dz, dgn = _gdn_out_bwd(oa_ref[:, sl], z_ref[:, sl], gn_ref[...], dya_sc[:, sl])
            doa_ref[:, sl] = doa
            dg_ref[:, sl] = dz.astype(BF16)
            dgn_ref[...] += dgn
            dob, drg, drn = _ret_out_bwd(ob_ref[:, sl], rg_ref[:, sl], rn_ref[:, sl], dyb_sc[:, sl])
            dob_ref[:, sl] = dob
            dg_ref[:, hv + h * HEAD_DIM:hv + (h + 1) * HEAD_DIM] = drg.astype(BF16)
            drn_ref[:, sl] += drn

    sp = _post_specs(tm, hv, d)
    act = pl.BlockSpec((tm, hv), lambda i: (i, 0))
    return pl.pallas_call(
        body, name=name, grid=(tp // tm,),
        in_specs=[sp["oa"], sp["ob"], sp["z"], sp["rg"], sp["ga"], sp["gb"], sp["gnorm"], sp["rnorm"],
                  sp["w"], sp["w"], sp["w"], sp["res"]],
        out_specs=[act, act, pl.BlockSpec((tm, 4 * hv), lambda i: (i, 0)), act, act, sp["res"], sp["res"],
                   sp["res"], sp["gnorm"], sp["rnorm"]],
        out_shape=[jax.ShapeDtypeStruct((tp, hv), F32), jax.ShapeDtypeStruct((tp, hv), F32),
                   jax.ShapeDtypeStruct((tp, 4 * hv), BF16), jax.ShapeDtypeStruct((tp, hv), BF16),
                   jax.ShapeDtypeStruct((tp, hv), BF16), jax.ShapeDtypeStruct((tp, d), BF16),
                   jax.ShapeDtypeStruct((tp, d), BF16), jax.ShapeDtypeStruct((tp, d), BF16),
                   jax.ShapeDtypeStruct((1, HEAD_DIM), F32), jax.ShapeDtypeStruct((1, hv), F32)],
        scratch_shapes=[pltpu.VMEM((tm, hv), F32), pltpu.VMEM((tm, hv), F32)],
        compiler_params=_params("arbitrary"))(oa, ob, proj, proj, proj, proj, gnorm, rnorm, wbg, wbr, wo, dh2)


def _final(h3, gain, target, name):
    tp, d = h3.shape
    tm = HEAD_ROWS

    def body(h_ref, g_ref, t_ref, loss_ref, dh_ref, dgain_ref):
        i = pl.program_id(0)

        @pl.when(i == 0)
        def _():
            loss_ref[...] = jnp.zeros_like(loss_ref)
            dgain_ref[...] = jnp.zeros_like(dgain_ref)

        xh, r = _rms_parts(h_ref[...])
        err = jnp.where(i == 0, 0.0, xh * g_ref[...] - t_ref[...])
        dx, dg = _rms_bwd(err * (1.0 / d), xh, r, g_ref[...])
        dh_ref[...] = dx
        dgain_ref[...] += dg
        loss_ref[...] += 0.5 * jnp.sum(jnp.mean(err * err, axis=-1, keepdims=True), axis=0, keepdims=True)

    row = pl.BlockSpec((tm, d), lambda i: (i, 0))
    vec = pl.BlockSpec((1, d), lambda i: (0, 0))
    return pl.pallas_call(
        body, name=name, grid=(tp // tm,),
        in_specs=[row, vec, pl.BlockSpec((tm, d), lambda i: (jnp.maximum(i - 1, 0), 0))],
        out_specs=[pl.BlockSpec((1, LANES), lambda i: (0, 0)), row, vec],
        out_shape=[jax.ShapeDtypeStruct((1, LANES), F32), jax.ShapeDtypeStruct((tp, d), F32),
                   jax.ShapeDtypeStruct((1, d), F32)],
        compiler_params=_params("arbitrary"))(h3, gain, target)


def _peer(k):
    x, y, c = lax.axis_index("x"), lax.axis_index("y"), lax.axis_index("c")
    return (1 - x if k & 4 else x, 1 - y if k & 2 else y, 1 - c if k & 1 else c)


def _my_index():
    return 4 * lax.axis_index("x") + 2 * lax.axis_index("y") + lax.axis_index("c")


def _exchange(bufs, scatter, name):
    n = len(bufs)

    def body(*refs):
        _exchange_copies(refs[:n], refs[n:2 * n], refs[2 * n:], scatter, True, True)

    hbm, out_shape, sems = _exchange_refs(bufs)
    return pl.pallas_call(
        body, name=name, in_specs=hbm, out_specs=hbm, out_shape=out_shape, scratch_shapes=sems,
        compiler_params=pltpu.CompilerParams(has_side_effects=True))(*bufs)


def _gather_via_sibling(bufs, name):
    n = len(bufs)

    def body(*refs):
        x_refs, out_refs = refs[:n], refs[n:2 * n]
        send_sems, recv_sems, local_sems = refs[2 * n:]
        x, y, c = lax.axis_index("x"), lax.axis_index("y"), lax.axis_index("c")
        me, sibling = (x, y, c), (x, y, 1 - c)
        chips = [(1 - x, y), (x, 1 - y), (1 - x, 1 - y)]
        rows = lambda a, dev: out_refs[a].at[4 * dev[0] + 2 * dev[1] + dev[2]]

        def copy(k, a, block, to, src=None):
            return pltpu.make_async_remote_copy(
                src_ref=rows(a, block) if src is None else src, dst_ref=rows(a, block),
                send_sem=send_sems.at[k * n + a], recv_sem=recv_sems.at[k * n + a],
                device_id=to, device_id_type=pl.DeviceIdType.MESH)

        mine = [pltpu.make_async_copy(x_refs[a], rows(a, me), local_sems.at[a]) for a in range(n)]
        first = [copy(0, a, me, sibling, src=x_refs[a]) for a in range(n)]
        first += [copy(1 + j, a, me, (*chip, c), src=x_refs[a]) for j, chip in enumerate(chips) for a in range(n)]
        for cp in mine + first:
            cp.start()
        passed = []
        for j, chip in enumerate(chips):
            for a in range(n):
                copy(1 + j, a, (*chip, c), me).wait_recv()
                passed.append(copy(4 + j, a, (*chip, c), sibling))
                passed[-1].start()
        for a in range(n):
            copy(0, a, sibling, me).wait_recv()
        for j, chip in enumerate(chips):
            for a in range(n):
                copy(4 + j, a, (*chip, 1 - c), me).wait_recv()
        for cp in first + passed:
            cp.wait_send()
        for cp in mine:
            cp.wait()

    hbm, out_shape, sems = _exchange_refs(bufs)
    return pl.pallas_call(
        body, name=name, in_specs=hbm, out_specs=hbm, out_shape=out_shape, scratch_shapes=sems,
        compiler_params=pltpu.CompilerParams(has_side_effects=True))(*bufs)


def _exchange_refs(bufs):
    n = len(bufs)
    return ([pl.BlockSpec(memory_space=pl.ANY)] * n,
            [jax.ShapeDtypeStruct((N_DEV,) + b.shape[-2:], b.dtype) for b in bufs],
            [pltpu.SemaphoreType.DMA(((N_DEV - 1) * n,)), pltpu.SemaphoreType.DMA(((N_DEV - 1) * n,)),
             pltpu.SemaphoreType.DMA((n,))])


def _exchange_copies(x_refs, out_refs, sems, scatter, start, wait):
    n = len(x_refs)
    send_sems, recv_sems, local_sems = sems
    me = _my_index()
    copies = []
    for a in range(n):
        copies.append(pltpu.make_async_copy(x_refs[a].at[me] if scatter else x_refs[a], out_refs[a].at[me],
                                            local_sems.at[a]))
    sends = []
    arrivals = []
    for k in range(1, N_DEV):
        x, y, c = _peer(k)
        peer = 4 * x + 2 * y + c
        for a in range(n):
            sem = (k - 1) * n + a
            sends.append(pltpu.make_async_remote_copy(
                src_ref=x_refs[a].at[peer] if scatter else x_refs[a], dst_ref=out_refs[a].at[me],
                send_sem=send_sems.at[sem], recv_sem=recv_sems.at[sem],
                device_id=(x, y, c), device_id_type=pl.DeviceIdType.MESH))
            landed = out_refs[a].at[peer]
            arrivals.append(pltpu.make_async_remote_copy(
                src_ref=landed, dst_ref=landed, send_sem=send_sems.at[sem], recv_sem=recv_sems.at[sem],
                device_id=(x, y, c), device_id_type=pl.DeviceIdType.MESH))
    if start:
        for cp in copies + sends:
            cp.start()
    if wait:
        for cp in arrivals:
            cp.wait_recv()
        for cp in sends:
            cp.wait_send()
        for cp in copies:
            cp.wait()


def _carried_call(body, carry, first, last, *, name, grid, in_specs, out_specs, out_shape, scratch_shapes=()):
    in_specs, out_specs, out_shape = list(in_specs), list(out_specs), list(out_shape)
    semantics = ("arbitrary",) * len(grid)
    if carry is None:
        call = pl.pallas_call(body, name=name, grid=grid, in_specs=in_specs, out_specs=out_specs,
                              out_shape=out_shape, scratch_shapes=list(scratch_shapes),
                              compiler_params=_params(*semantics))
        return lambda *args: (call(*args), [])
    bufs, scatter = carry
    n, n_in, n_out, n_scratch = len(bufs), len(in_specs), len(out_specs), len(scratch_shapes)
    hbm, x_shapes, sems = _exchange_refs(bufs)

    def full_body(*refs):
        ins, x_refs = refs[:n_in], refs[n_in:n_in + n]
        outs, xo_refs = refs[n_in + n:n_in + n + n_out], refs[n_in + n + n_out:n_in + 2 * n + n_out]
        scratch = refs[n_in + 2 * n + n_out:n_in + 2 * n + n_out + n_scratch]
        x_sems = refs[n_in + 2 * n + n_out + n_scratch:]

        @pl.when(first())
        def _():
            _exchange_copies(x_refs, xo_refs, x_sems, scatter, True, False)

        body(*ins, *outs, *scratch)

        @pl.when(last())
        def _():
            _exchange_copies(x_refs, xo_refs, x_sems, scatter, False, True)

    call = pl.pallas_call(full_body, name=name, grid=grid, in_specs=in_specs + hbm, out_specs=out_specs + hbm,
                          out_shape=out_shape + x_shapes, scratch_shapes=list(scratch_shapes) + sems,
                          compiler_params=_params(*semantics))

    def run(*args):
        res = call(*args, *bufs)
        return res[:n_out], res[n_out:]
    return run


def _adamw(w, g, m, v, name):
    r, c = w.shape
    parts = g.ndim == 3
    tr = _tile(r, 256, 16 if parts else 8)
    c1 = 1.0 - ADAM_B1 ** ADAM_STEP
    c2 = 1.0 - ADAM_B2 ** ADAM_STEP

    def body(w_ref, g_ref, m_ref, v_ref, go_ref, d_ref, mo_ref, vo_ref):
        if parts:
            g = g_ref[0].astype(F32)
            for q in range(1, N_DEV):
                g = g + g_ref[q].astype(F32)
        else:
            g = g_ref[...]
        m = ADAM_B1 * m_ref[...] + (1.0 - ADAM_B1) * g
        v = ADAM_B2 * v_ref[...] + (1.0 - ADAM_B2) * (g * g)
        go_ref[...] = g
        d_ref[...] = -ADAM_LR * ((m / c1) / (jnp.sqrt(v / c2) + ADAM_EPS) + ADAM_WD * w_ref[...])
        mo_ref[...] = m
        vo_ref[...] = v

    blk = pl.BlockSpec((tr, c), lambda i: (i, 0))
    g_spec = pl.BlockSpec((N_DEV, tr, c), lambda i: (0, i, 0)) if parts else blk
    return pl.pallas_call(
        body, name=name, grid=(r // tr,), in_specs=[blk, g_spec, blk, blk], out_specs=[blk] * 4,
        out_shape=[jax.ShapeDtypeStruct((r, c), F32)] * 4,
        compiler_params=_params("parallel"))(w, g, m, v)


def _win_segments(hv, nh):
    o_z, o_b = 3 * hv, 4 * hv
    o_r = o_b + 2 * nh
    return [(0, 0, 3 * hv), (3 * hv, o_r, 3 * hv), (6 * hv, o_z, hv), (7 * hv, o_r + 3 * hv, 3 * hv),
            (10 * hv, o_b, 2 * nh)]


def _win_from_shards(shards, hv, nh):
    _, d, cs = shards.shape
    pieces = []
    for _, src, width in _win_segments(hv, nh):
        lo = src
        while lo < src + width:
            p = lo // cs
            hi = min(src + width, (p + 1) * cs)
            pieces.append(shards[p][:, lo - p * cs:hi - p * cs])
            lo = hi
    pieces.append(jnp.zeros((d, LANES - 2 * nh), shards.dtype))
    return jnp.concatenate(pieces, axis=1)


def _win_grad_to_shards(parts, hv, nh, cs):
    segments = _win_segments(hv, nh)
    starts = [sum(p.shape[1] for p in parts[:i]) for i in range(len(parts))]

    def columns(a, b):
        out = []
        for part, start in zip(parts, starts):
            lo, hi = max(a, start), min(b, start + part.shape[1])
            if lo < hi:
                out.append(part[:, lo - start:hi - start])
        return out

    shards = []
    for p in range(N_DEV):
        pieces = []
        lo = p * cs
        while lo < (p + 1) * cs:
            here, src, width = next(s for s in segments if s[1] <= lo < s[1] + s[2])
            hi = min((p + 1) * cs, src + width)
            pieces += columns(here + lo - src, here + hi - src)
            lo = hi
        shards.append(jnp.concatenate(pieces, axis=1))
    return jnp.stack(shards)


def _rope_tables(tp):
    pos = jnp.arange(tp, dtype=F32) - float(PAD_FRONT)
    inv = 1.0 / (ROPE_BASE ** jnp.linspace(0.0, 1.0, HEAD_DIM // 2, dtype=F32))
    ang = pos[:, None] * inv[None, :]
    cos = jnp.repeat(jnp.cos(ang), 2, axis=1)
    sin = jnp.repeat(jnp.sin(ang), 2, axis=1) * jnp.tile(jnp.array([-1.0, 1.0], F32), HEAD_DIM // 2)[None, :]
    return cos, sin


def _retention_tables(nh):
    log_gamma = jnp.log1p(-jnp.exp2(-5.0 - jnp.arange(nh, dtype=F32)))
    pos = jnp.arange(CHUNK, dtype=F32)
    causal = pos[:, None] >= pos[None, :]
    diff = pos[:, None] - pos[None, :]
    dec = jnp.where(causal, jnp.exp(jnp.where(causal, diff, 0.0) * log_gamma[:, None, None]), 0.0)
    ones = jnp.ones((1, 1, HEAD_DIM), F32)
    xi = jnp.exp((pos + 1.0)[None, :] * log_gamma[:, None])[:, :, None] * ones
    zeta = jnp.exp((CHUNK - 1.0 - pos)[None, :] * log_gamma[:, None])[:, :, None] * ones
    cd = jnp.exp(CHUNK * log_gamma)[:, None, None] * jnp.ones((1, 8, HEAD_DIM), F32)
    return dec, xi, zeta, cd


SHARDED = ("meta_tokens", "ffn1_w_in", "ffn1_w_out", "w_in", "gdn_conv_w", "w_branch_gdn", "w_branch_ret",
           "w_out", "ffn2_w_in", "ffn2_w_out")
COLUMN_SHARDED = ("meta_tokens", "ffn1_w_in", "w_in", "gdn_conv_w", "ffn2_w_in")
EXACT_F32 = ("meta_tokens", "gdn_conv_w")
REPLICATED = ("ffn1_norm", "mix_norm", "gdn_a_log", "gdn_dt_bias", "gdn_out_norm", "ret_out_norm", "ffn2_norm",
              "final_norm")
WEIGHTS = ("meta_tokens", "ffn1_norm", "ffn1_w_in", "ffn1_w_out", "mix_norm", "w_in", "gdn_conv_w", "gdn_a_log",
           "gdn_dt_bias", "gdn_out_norm", "ret_out_norm", "w_branch_gdn", "w_branch_ret", "w_out", "ffn2_norm",
           "ffn2_w_in", "ffn2_w_out", "final_norm")


def _as2d(a):
    if a.ndim == 3:
        return a[0]
    if a.ndim == 1:
        return a[None, :]
    return a


def _rows_of(shards):
    return shards.reshape(-1, shards.shape[2])


def _cols_of(shards):
    return shards.transpose(1, 0, 2).reshape(shards.shape[1], -1)


def _row_shards(a):
    return a.reshape(N_DEV, -1, a.shape[1])


def _col_shards(a):
    return a.reshape(a.shape[0], N_DEV, -1).transpose(1, 0, 2)


GATHER_FIRST = ("meta_tokens", "ffn1_w_in", "ffn1_w_out")
GATHER_BEHIND_FFN1 = ("w_in", "gdn_conv_w")
GATHER_BEHIND_PROJ = ("w_branch_gdn", "w_branch_ret", "w_out", "ffn2_w_in", "ffn2_w_out")
SCATTER_BEHIND_DN2 = ("ffn2_w_in", "ffn2_w_out", "w_branch_gdn", "w_branch_ret", "w_out")
SCATTER_BEHIND_FFN1 = ("w_in", "gdn_conv_w")
SCATTER_BEHIND_DWG = ("meta_tokens", "ffn1_w_out")
SCATTER_LAST = ("ffn1_w_in",)


def _device_step(x, target, send, rep):
    seq, d = x.shape
    tp = HEAD_ROWS + seq
    hv = d
    nh = hv // HEAD_DIM
    assert tp % (SCAN_CHUNKS * CHUNK) == 0 and tp % HEAD_ROWS == 0
    bf16_shards = lambda grads, names: [grads[n].astype(BF16) for n in names]

    pad_lanes = lambda row: jnp.pad(row, ((0, 0), (nh, LANES - 2 * nh)))
    alog = pad_lanes(rep["gdn_a_log"])
    dtb = pad_lanes(rep["gdn_dt_bias"])
    cos, sin = _rope_tables(tp)
    dec, xi, zeta, cd = _retention_tables(nh)

    got = dict(zip(GATHER_FIRST, _gather_via_sibling([send[n] for n in GATHER_FIRST], "gather_ffn1")))
    h0 = jnp.concatenate([jnp.zeros((PAD_FRONT, d), F32), _cols_of(got["meta_tokens"]), x], axis=0)
    f1i, f1o = got["ffn1_w_in"], _rows_of(got["ffn1_w_out"])
    (h1, ag1, au1), moved = _ffn_fwd(h0, rep["ffn1_norm"], f1i, f1o, "ffn1_fwd",
                                     ([send[n] for n in GATHER_BEHIND_FFN1], False))
    got.update(zip(GATHER_BEHIND_FFN1, moved))
    wp = _win_from_shards(got["w_in"], hv, nh)
    conv_w = _cols_of(got["gdn_conv_w"])
    (proj, n2), moved = _proj_fwd(h1, rep["mix_norm"], wp, "proj_fwd",
                                  ([send[n] for n in GATHER_BEHIND_PROJ], False))
    got.update(zip(GATHER_BEHIND_PROJ, moved))
    wbg, wbr, wo = _rows_of(got["w_branch_gdn"]), _rows_of(got["w_branch_ret"]), _rows_of(got["w_out"])
    f2i, f2o = got["ffn2_w_in"], _rows_of(got["ffn2_w_out"])
    qkv = _conv_fwd(proj, conv_w, hv, "conv_fwd")
    oa, s_gdn, t_gdn = _gdn_fwd(qkv, proj, alog, dtb, nh, "gdn_fwd")
    ob, s_ret = _ret_fwd(proj, cos, sin, dec, xi, zeta, cd, nh, "ret_fwd")
    h2 = _post_fwd(oa, ob, proj, rep["gdn_out_norm"], rep["ret_out_norm"], wbg, wbr, wo, h1, "post_fwd")
    (h3, ag2, au2), _ = _ffn_fwd(h2, rep["ffn2_norm"], f2i, f2o, "ffn2_fwd")
    loss_row, dh3, d_final = _final(h3, rep["final_norm"], target, "final")

    (dh2, d_f2n, n3, hid2, dag2, dau2), _ = _ffn_bwd(h2, dh3, rep["ffn2_norm"], f2i, f2o, ag2, au2, "ffn2_bwd")
    grads = {"ffn2_w_in": jnp.concatenate([_matmul_tn_blocks(n3, dag2, "ffn2_dwg"),
                                           _matmul_tn_blocks(n3, dau2, "ffn2_dwu")]),
             "ffn2_w_out": _row_shards(_matmul_tn_blocks(hid2, dh3, "ffn2_dwo", 0.5))}

    doa, dob, dgate, ya, yb, merged, dpa, dpb, d_gn, d_rn = _post_bwd(
        oa, ob, proj, rep["gdn_out_norm"], rep["ret_out_norm"], wbg, wbr, wo, dh2, "post_bwd")
    grads["w_branch_gdn"] = _row_shards(_matmul_tn(ya, dpa, "dw_branch_gdn"))
    grads["w_branch_ret"] = _row_shards(_matmul_tn(yb, dpb, "dw_branch_ret"))
    grads["w_out"] = _row_shards(_matmul_tn(merged, dh2, "dw_out"))

    d_ret = _ret_bwd(proj, cos, sin, dec, xi, zeta, cd, s_ret, dob, nh, "ret_bwd")
    gdn_grads = _gdn_bwd(qkv, proj, alog, dtb, s_gdn, t_gdn, doa, nh, "gdn_bwd")
    dba, d_alog, d_dtb = gdn_grads[3:]
    dpre, g_conv = [], []
    for grp, tag in enumerate("qkv"):
        dx, dw = _conv_bwd(proj, conv_w, gdn_grads[grp], grp, hv, "conv_bwd_" + tag)
        dpre.append(dx)
        g_conv.append(dw)
    grads["gdn_conv_w"] = _col_shards(jnp.concatenate(g_conv, axis=1))

    wide = dpre + list(d_ret) + [dgate]
    dn2, moved = _matmul_nt_parts(wide, wp[:, :10 * hv], None, "dn2_wide",
                                  (bf16_shards(grads, SCATTER_BEHIND_DN2), True))
    parts = dict(zip(SCATTER_BEHIND_DN2, moved))
    dn2, _ = _matmul_nt_parts([dba], wp[:, 10 * hv:], dn2, "dn2_beta_alpha")
    g_wp = [_matmul_tn(n2, dg, "dw_in_%d" % idx) for idx, dg in enumerate(wide + [dba])]
    grads["w_in"] = _win_grad_to_shards(g_wp, hv, nh, send["w_in"].shape[1])
    dh1, d_mixn = _norm_bwd(h1, rep["mix_norm"], dn2, dh2, "mix_norm_bwd")

    (dh0, d_f1n, n1, hid1, dag1, dau1), moved = _ffn_bwd(h0, dh1, rep["ffn1_norm"], f1i, f1o, ag1, au1, "ffn1_bwd",
                                                         (bf16_shards(grads, SCATTER_BEHIND_FFN1), True))
    parts.update(zip(SCATTER_BEHIND_FFN1, moved))
    grads["ffn1_w_out"] = _row_shards(_matmul_tn_blocks(hid1, dh1, "ffn1_dwo", 0.5))
    grads["meta_tokens"] = _col_shards(dh0[PAD_FRONT:HEAD_ROWS])
    g_gate, moved = _matmul_tn_blocks(n1, dag1, "ffn1_dwg", carry=(bf16_shards(grads, SCATTER_BEHIND_DWG), True))
    parts.update(zip(SCATTER_BEHIND_DWG, moved))
    grads["ffn1_w_in"] = jnp.concatenate([g_gate, _matmul_tn_blocks(n1, dau1, "ffn1_dwu")])
    parts.update(zip(SCATTER_LAST, _exchange(bf16_shards(grads, SCATTER_LAST), True, "scatter_ffn1")))

    small = {"ffn1_norm": d_f1n, "mix_norm": d_mixn, "gdn_a_log": d_alog[:, nh:2 * nh],
             "gdn_dt_bias": d_dtb[:, nh:2 * nh], "gdn_out_norm": d_gn, "ret_out_norm": d_rn, "ffn2_norm": d_f2n,
             "final_norm": d_final}
    return loss_row[0, 0], dh0[HEAD_ROWS:], parts, small


def kernel(x, meta_tokens, ffn1_norm, ffn1_w_in, ffn1_w_out, mix_norm, w_in, gdn_conv_w, gdn_a_log, gdn_dt_bias, gdn_out_norm, ret_out_norm, w_branch_gdn, w_branch_ret, w_out, ffn2_norm, ffn2_w_in, ffn2_w_out, final_norm, loss_target, m_meta_tokens, m_ffn1_norm, m_ffn1_w_in, m_ffn1_w_out, m_mix_norm, m_w_in, m_gdn_conv_w, m_gdn_a_log, m_gdn_dt_bias, m_gdn_out_norm, m_ret_out_norm, m_w_branch_gdn, m_w_branch_ret, m_w_out, m_ffn2_norm, m_ffn2_w_in, m_ffn2_w_out, m_final_norm, v_meta_tokens, v_ffn1_norm, v_ffn1_w_in, v_ffn1_w_out, v_mix_norm, v_w_in, v_gdn_conv_w, v_gdn_a_log, v_gdn_dt_bias, v_gdn_out_norm, v_ret_out_norm, v_w_branch_gdn, v_w_branch_ret, v_w_out, v_ffn2_norm, v_ffn2_w_in, v_ffn2_w_out, v_final_norm):
    given = dict(locals())
    params = {n: _as2d(given[n]) for n in WEIGHTS}
    local = {n: params[n] for n in SHARDED}
    rep = {n: params[n] for n in REPLICATED}

    send = {n: local[n] if n in EXACT_F32 else local[n].astype(BF16) for n in SHARDED}
    loss_sum, grad_x, parts, small = _device_step(x[0], loss_target[0], send, rep)
    parts.update(zip(REPLICATED, _exchange([small[n] for n in REPLICATED], False, "gather_small_grads")))
    loss = lax.psum(loss_sum, ("x", "y", "c"))

    outs = {}
    for n in WEIGHTS:
        res = _adamw(params[n], parts[n], _as2d(given["m_" + n]), _as2d(given["v_" + n]), "adamw_" + n)
        outs[n] = [r.reshape(given[n].shape) for r in res]
    return (loss, grad_x[None], *[outs[n][0] for n in WEIGHTS], *[outs[n][1] for n in WEIGHTS],
            *[outs[n][2] for n in WEIGHTS], *[outs[n][3] for n in WEIGHTS])
```

```python
import functools
import math

import numpy as np
import jax
import jax.numpy as jnp
from jax import lax
from jax.experimental import pallas as pl
from jax.experimental.pallas import tpu as pltpu

F32 = jnp.float32
BF16 = jnp.bfloat16

N_DEV = 8
N_META = 16
CHUNK = 64
HEAD_DIM = 128
CONV_K = 4
ROPE_BASE = 10000.0
EPS = 1e-6
PAD_FRONT = 240
HEAD_ROWS = PAD_FRONT + N_META
LANES = 128
VMEM_LIMIT_BYTES = 56 * 1024 * 1024

ADAM_LR = 0.001
ADAM_B1 = 0.9
ADAM_B2 = 0.999
ADAM_EPS = 1e-08
ADAM_WD = 0.01
ADAM_STEP = 10

NN = (((1,), (0,)), ((), ()))
NT = (((1,), (1,)), ((), ()))
TN = (((0,), (0,)), ((), ()))


def _tile(n, target, mult):
    best = 0
    for t in range(mult, min(n, target) + 1, mult):
        if n % t == 0:
            best = t
    return best if best else n


def _params(*semantics):
    return pltpu.CompilerParams(dimension_semantics=semantics, vmem_limit_bytes=VMEM_LIMIT_BYTES)


def _split(a, pieces):
    out = []
    for _ in range(pieces - 1):
        part = a.astype(BF16)
        out.append(part)
        a = a - part.astype(F32)
    return out + [a.astype(BF16)]


def _raw_dot(a, b, dims, hi):
    dot = lambda x, y: lax.dot_general(x, y, dims, preferred_element_type=F32)
    if hi:
        (a_hi, a_lo), (b_hi, b_lo) = _split(a, 2), _split(b, 2)
        return dot(a_hi, b_hi) + (dot(a_hi, b_lo) + dot(a_lo, b_hi))
    return dot(a.astype(BF16), b.astype(BF16))


def _mask_dot(mask, x, dims):
    mask = mask.astype(BF16)
    hi, mid, lo = [lax.dot_general(mask, p, dims, preferred_element_type=F32) for p in _split(x, 3)]
    return hi + (mid + lo)


@jax.custom_vjp
def _cumsum_rows(x):
    c = x.shape[0]
    tril = lax.broadcasted_iota(jnp.int32, (c, c), 0) >= lax.broadcasted_iota(jnp.int32, (c, c), 1)
    return _mask_dot(tril, x, NN)


def _cumsum_rows_bwd(_, g):
    c = g.shape[0]
    tril = lax.broadcasted_iota(jnp.int32, (c, c), 0) >= lax.broadcasted_iota(jnp.int32, (c, c), 1)
    return (_mask_dot(tril, g, TN),)


_cumsum_rows.defvjp(lambda x: (_cumsum_rows(x), None), _cumsum_rows_bwd)


def _unit_lower_inverses(xs):
    c = xs[0].shape[0]
    eye = (lax.broadcasted_iota(jnp.int32, (c, c), 0) == lax.broadcasted_iota(jnp.int32, (c, c), 1)).astype(F32)
    t_inv = [eye + x for x in xs]
    for _ in range(int(math.log2(c)) - 1):
        xs = [_raw_dot(x, x, NN, True) for x in xs]
        t_inv = [t + _raw_dot(t, x, NN, True) for t, x in zip(t_inv, xs)]
    return t_inv


@jax.custom_vjp
def _known_inverse(x_neg, t_inv):
    return t_inv


_known_inverse.defvjp(
    lambda x_neg, t_inv: (t_inv, t_inv),
    lambda t_inv, g: (_raw_dot(_raw_dot(t_inv, g, TN, False), t_inv, NT, False), jnp.zeros_like(t_inv)))


def _make_mm(hi):
    @jax.custom_vjp
    def nn(a, b):
        return _raw_dot(a, b, NN, hi)

    @jax.custom_vjp
    def nt(a, b):
        return _raw_dot(a, b, NT, hi)

    @jax.custom_vjp
    def tn(a, b):
        return _raw_dot(a, b, TN, hi)

    nn.defvjp(lambda a, b: (_raw_dot(a, b, NN, hi), (a, b)),
              lambda r, g: (_raw_dot(g, r[1], NT, False), _raw_dot(r[0], g, TN, False)))
    nt.defvjp(lambda a, b: (_raw_dot(a, b, NT, hi), (a, b)),
              lambda r, g: (_raw_dot(g, r[1], NN, False), _raw_dot(g, r[0], TN, False)))
    tn.defvjp(lambda a, b: (_raw_dot(a, b, TN, hi), (a, b)),
              lambda r, g: (_raw_dot(r[1], g, NT, False), _raw_dot(r[0], g, NN, False)))
    return nn, nt, tn


def _silu(x):
    return x * jax.nn.sigmoid(x)


def _rms_parts(x):
    r = lax.rsqrt(jnp.mean(x * x, axis=-1, keepdims=True) + EPS)
    return x * r, r


def _rms_bwd(dy, xh, r, gain):
    dxh = dy * gain
    dx = r * (dxh - xh * jnp.mean(dxh * xh, axis=-1, keepdims=True))
    return dx, jnp.sum(dy * xh, axis=0, keepdims=True)


def _ffn_specs(tm, d, tf, nj):
    return [pl.BlockSpec((tm, d), lambda i, j: (i, 0)), pl.BlockSpec((1, d), lambda i, j: (0, 0)),
            pl.BlockSpec((1, d, tf), lambda i, j: (j, 0, 0)), pl.BlockSpec((1, d, tf), lambda i, j: (nj + j, 0, 0)),
            pl.BlockSpec((tf, d), lambda i, j: (j, 0))]


def _first_step(ndim):
    return lambda: functools.reduce(lambda a, b: a & b, [pl.program_id(k) == 0 for k in range(ndim)])


def _last_step(grid):
    return lambda: functools.reduce(lambda a, b: a & b, [pl.program_id(k) == g - 1 for k, g in enumerate(grid)])


def _ffn_fwd(h, gain, w_in, wo, name, carry=None):
    tp, d = h.shape
    tf = w_in.shape[2]
    nj = w_in.shape[0] // 2
    tm = _tile(tp, 768, 8)
    row, vec, wg_spec, wu_spec, wo_spec = _ffn_specs(tm, d, tf, nj)

    def body(h_ref, g_ref, wg3_ref, wu3_ref, wo_ref, o_ref, hid3_ref, dup3_ref, dgate3_ref, n_sc, acc_sc):
        wg_ref, wu_ref = wg3_ref.at[0], wu3_ref.at[0]
        j = pl.program_id(1)

        @pl.when(j == 0)
        def _():
            xh, _ = _rms_parts(h_ref[...])
            n_sc[...] = (xh * g_ref[...]).astype(BF16)
            acc_sc[...] = jnp.zeros_like(acc_sc)

        n = n_sc[...]
        a_g = jnp.dot(n, wg_ref[...], preferred_element_type=F32)
        a_u = jnp.dot(n, wu_ref[...], preferred_element_type=F32)
        sg = jax.nn.sigmoid(a_g)
        s = a_g * sg
        hid = (s * a_u).astype(BF16)
        hid3_ref[0] = hid
        dup3_ref[0] = s.astype(BF16)
        dgate3_ref[0] = (a_u * _dsilu(a_g, sg)).astype(BF16)
        acc_sc[...] += jnp.dot(hid, wo_ref[...], preferred_element_type=F32)

        @pl.when(j == nj - 1)
        def _():
            o_ref[...] = h_ref[...] + 0.5 * acc_sc[...]

    grid = (tp // tm, nj)
    act = pl.BlockSpec((1, tm, tf), lambda i, j: (j, i, 0))
    return _carried_call(
        body, carry, _first_step(2), _last_step(grid), name=name, grid=grid,
        in_specs=[row, vec, wg_spec, wu_spec, wo_spec], out_specs=[row, act, act, act],
        out_shape=[jax.ShapeDtypeStruct((tp, d), F32)] + [jax.ShapeDtypeStruct((nj, tp, tf), BF16)] * 3,
        scratch_shapes=[pltpu.VMEM((tm, d), BF16), pltpu.VMEM((tm, d), F32)])(h, gain, w_in, w_in, wo)


def _ffn_bwd(h, dho, gain, w_in, wo, dup3, dgate3, name, carry=None):
    tp, d = h.shape
    tf = w_in.shape[2]
    nj = w_in.shape[0] // 2
    tm = _tile(tp, 704, 16)
    ni = tp // tm
    row, vec, wg_spec, wu_spec, wo_spec = _ffn_specs(tm, d, tf, nj)

    def body(h_ref, dho_ref, g_ref, wg3_ref, wu3_ref, wo_ref, dup3_ref, dgate3_ref,
             dh_ref, dgain_ref, n_ref, dag3_ref, dau3_ref, dn_sc, dhb_sc):
        wg_ref, wu_ref = wg3_ref.at[0], wu3_ref.at[0]
        dag_ref, dau_ref = dag3_ref.at[0], dau3_ref.at[0]
        i, j = pl.program_id(0), pl.program_id(1)

        @pl.when(j == 0)
        def _():
            xh, _ = _rms_parts(h_ref[...])
            n_ref[...] = (xh * g_ref[...]).astype(BF16)
            dn_sc[...] = jnp.zeros_like(dn_sc)
            dhb_sc[...] = (0.5 * dho_ref[...]).astype(BF16)

        @pl.when((i == 0) & (j == 0))
        def _():
            dgain_ref[...] = jnp.zeros_like(dgain_ref)

        d_hid = lax.dot_general(dhb_sc[...], wo_ref[...], NT, preferred_element_type=F32)
        d_au = (d_hid * dup3_ref[0].astype(F32)).astype(BF16)
        d_ag = (d_hid * dgate3_ref[0].astype(F32)).astype(BF16)
        dau_ref[...] = d_au
        dag_ref[...] = d_ag
        dn_sc[...] += (lax.dot_general(d_ag, wg_ref[...], NT, preferred_element_type=F32)
                       + lax.dot_general(d_au, wu_ref[...], NT, preferred_element_type=F32))

        @pl.when(j == nj - 1)
        def _():
            xh, r = _rms_parts(h_ref[...])
            dx, dg = _rms_bwd(dn_sc[...], xh, r, g_ref[...])
            dh_ref[...] = dho_ref[...] + dx
            dgain_ref[...] += dg

    act = pl.BlockSpec((1, tm, tf), lambda i, j: (j, i, 0))
    return _carried_call(
        body, carry, _first_step(2), _last_step((ni, nj)), name=name, grid=(ni, nj),
        in_specs=[row, row, vec, wg_spec, wu_spec, wo_spec, act, act],
        out_specs=[row, vec, row, act, act],
        out_shape=[jax.ShapeDtypeStruct((tp, d), F32), jax.ShapeDtypeStruct((1, d), F32),
                   jax.ShapeDtypeStruct((tp, d), BF16)] + [jax.ShapeDtypeStruct((nj, tp, tf), BF16)] * 2,
        scratch_shapes=[pltpu.VMEM((tm, d), F32), pltpu.VMEM((tm, d), BF16)])(
            h, dho, gain, w_in, w_in, wo, dup3, dgate3)


def _matmul_tn(a, b, name, scale=1.0):
    t, m = a.shape
    n = b.shape[1]
    bm = _tile(m, 1024, LANES)
    bn = _tile(n, 1536, LANES)
    tk = _tile(t, 1408, 16)
    nk = t // tk

    def body(a_ref, b_ref, o_ref):
        k = pl.program_id(2)

        @pl.when(k == 0)
        def _():
            o_ref[...] = jnp.zeros_like(o_ref)

        o_ref[...] += lax.dot_general(a_ref[...].astype(BF16), b_ref[...].astype(BF16), TN,
                                      preferred_element_type=F32)

        if scale != 1.0:
            @pl.when(k == nk - 1)
            def _():
                o_ref[...] = o_ref[...] * scale

    return pl.pallas_call(
        body, name=name, grid=(m // bm, n // bn, nk),
        in_specs=[pl.BlockSpec((tk, bm), lambda i, j, k: (k, i)), pl.BlockSpec((tk, bn), lambda i, j, k: (k, j))],
        out_specs=pl.BlockSpec((bm, bn), lambda i, j, k: (i, j)),
        out_shape=jax.ShapeDtypeStruct((m, n), F32),
        compiler_params=_params("parallel", "parallel", "arbitrary"))(a, b)


def _matmul_tn_blocks(a, b, name, scale=1.0, carry=None):
    a_blocked = a.ndim == 3
    nb, t = (a.shape[0], a.shape[1]) if a_blocked else (b.shape[0], b.shape[1])
    m, n = a.shape[-1], b.shape[-1]
    tk = _tile(t, 1408, 16)
    nk = t // tk
    if a_blocked:
        bo = _tile(n, 1024, LANES)
        a_spec = pl.BlockSpec((1, tk, m), lambda p, o, k: (p, k, 0))
        b_spec = pl.BlockSpec((tk, bo), lambda p, o, k: (k, o))
        o_spec = pl.BlockSpec((m, bo), lambda p, o, k: (p, o))
        out_shape = jax.ShapeDtypeStruct((nb * m, n), F32)
        grid = (nb, n // bo, nk)
    else:
        bo = _tile(m, 1024, LANES)
        a_spec = pl.BlockSpec((tk, bo), lambda p, o, k: (k, o))
        b_spec = pl.BlockSpec((1, tk, n), lambda p, o, k: (p, k, 0))
        o_spec = pl.BlockSpec((1, bo, n), lambda p, o, k: (p, o, 0))
        out_shape = jax.ShapeDtypeStruct((nb, m, n), F32)
        grid = (nb, m // bo, nk)

    def body(a_ref, b_ref, o_ref):
        k = pl.program_id(2)
        a_blk = a_ref[0] if a_blocked else a_ref[...]
        b_blk = b_ref[...] if a_blocked else b_ref[0]
        part = lax.dot_general(a_blk.astype(BF16), b_blk.astype(BF16), TN, preferred_element_type=F32)
        out = o_ref if a_blocked else o_ref.at[0]

        @pl.when(k == 0)
        def _():
            out[...] = part

        @pl.when(k > 0)
        def _():
            out[...] += part

        if scale != 1.0:
            @pl.when(k == nk - 1)
            def _():
                out[...] = out[...] * scale

    (out,), moved = _carried_call(body, carry, _first_step(3), _last_step(grid), name=name, grid=grid,
                                  in_specs=[a_spec, b_spec], out_specs=[o_spec], out_shape=[out_shape])(a, b)
    return out if carry is None else (out, moved)


def _matmul_nt_parts(parts, w, acc, name, carry=None):
    t = parts[0].shape[0]
    d = w.shape[0]
    widths = [p.shape[1] for p in parts]
    tk = _tile(math.gcd(*widths), 1024, LANES)
    counts = [wd // tk for wd in widths]
    starts = [sum(counts[:g]) for g in range(len(parts))]
    nk = sum(counts)
    tm = _tile(t, 768, 8)
    n_parts = len(parts)

    def body(*refs):
        a_refs, w_ref, o_ref = refs[:n_parts], refs[n_parts], refs[-1]
        k = pl.program_id(1)

        @pl.when(k == 0)
        def _():
            o_ref[...] = jnp.zeros_like(o_ref) if acc is None else refs[n_parts + 1][...]

        for g in range(n_parts):
            @pl.when((k >= starts[g]) & (k < starts[g] + counts[g]))
            def _(g=g):
                o_ref[...] += lax.dot_general(a_refs[g][...].astype(BF16), w_ref[...], NT,
                                              preferred_element_type=F32)

    in_specs = [pl.BlockSpec((tm, tk), lambda i, k, lo=starts[g], nb=counts[g]: (i, jnp.clip(k - lo, 0, nb - 1)))
                for g in range(n_parts)]
    in_specs.append(pl.BlockSpec((d, tk), lambda i, k: (0, k)))
    args = list(parts) + [w]
    if acc is not None:
        in_specs.append(pl.BlockSpec((tm, d), lambda i, k: (i, 0)))
        args.append(acc)
    grid = (t // tm, nk)
    (out,), moved = _carried_call(
        body, carry, _first_step(2), _last_step(grid), name=name, grid=grid, in_specs=in_specs,
        out_specs=[pl.BlockSpec((tm, d), lambda i, k: (i, 0))],
        out_shape=[jax.ShapeDtypeStruct((t, d), F32)])(*args)
    return out, moved


def _proj_fwd(h, gain, wp, name, carry=None):
    tp, d = h.shape
    npad = wp.shape[1]
    tm = _tile(tp, 768, 8)
    tn = _tile(npad, 3456, LANES)

    def body(h_ref, g_ref, w_ref, o_ref, n_ref):
        @pl.when(pl.program_id(1) == 0)
        def _():
            xh, _ = _rms_parts(h_ref[...])
            n_ref[...] = (xh * g_ref[...]).astype(BF16)

        o_ref[...] = jnp.dot(n_ref[...], w_ref[...], preferred_element_type=F32)

    grid = (tp // tm, npad // tn)
    return _carried_call(
        body, carry, _first_step(2), _last_step(grid), name=name, grid=grid,
        in_specs=[pl.BlockSpec((tm, d), lambda i, j: (i, 0)), pl.BlockSpec((1, d), lambda i, j: (0, 0)),
                  pl.BlockSpec((d, tn), lambda i, j: (0, j))],
        out_specs=[pl.BlockSpec((tm, tn), lambda i, j: (i, j)), pl.BlockSpec((tm, d), lambda i, j: (i, 0))],
        out_shape=[jax.ShapeDtypeStruct((tp, npad), F32), jax.ShapeDtypeStruct((tp, d), BF16)])(h, gain, wp)


def _norm_bwd(h, gain, dn, dres, name):
    tp, d = h.shape
    tm = _tile(tp, 256, 8)

    def body(h_ref, g_ref, dn_ref, dres_ref, dh_ref, dgain_ref):
        @pl.when(pl.program_id(0) == 0)
        def _():
            dgain_ref[...] = jnp.zeros_like(dgain_ref)

        xh, r = _rms_parts(h_ref[...])
        dx, dg = _rms_bwd(dn_ref[...], xh, r, g_ref[...])
        dh_ref[...] = dres_ref[...] + dx
        dgain_ref[...] += dg

    row = pl.BlockSpec((tm, d), lambda i: (i, 0))
    vec = pl.BlockSpec((1, d), lambda i: (0, 0))
    return pl.pallas_call(
        body, name=name, grid=(tp // tm,), in_specs=[row, vec, row, row], out_specs=[row, vec],
        out_shape=[jax.ShapeDtypeStruct((tp, d), F32), jax.ShapeDtypeStruct((1, d), F32)],
        compiler_params=_params("arbitrary"))(h, gain, dn, dres)


def _head_post(a, grp):
    a = _silu(a)
    r = lax.rsqrt(jnp.sum(a * a, axis=-1, keepdims=True) + EPS)
    if isinstance(grp, int):
        return a if grp == 2 else a * r * (HEAD_DIM ** -0.5 if grp == 0 else 1.0)
    scale = jnp.where(grp == 0, HEAD_DIM ** -0.5, 1.0).astype(F32)
    return jnp.where(grp == 2, a, a * r * scale)


def _head_post_bwd(c, dy, grp):
    sg = jax.nn.sigmoid(c)
    a = c * sg
    dsilu = sg * (1.0 + c * (1.0 - sg))
    if grp == 2:
        return dy * dsilu
    r = lax.rsqrt(jnp.sum(a * a, axis=-1, keepdims=True) + EPS)
    scale = HEAD_DIM ** -0.5 if grp == 0 else 1.0
    da = (scale * r) * (dy - a * (r * r * jnp.sum(dy * a, axis=-1, keepdims=True)))
    return da * dsilu


def _conv_taps(ext_sc, w_ref, tm):
    c = None
    for i in range(CONV_K):
        s = CONV_K - 1 - i
        term = w_ref[i:i + 1, :] * ext_sc[8 - s:8 - s + tm, :]
        c = term if c is None else c + term
    return c


def _conv_fwd(proj, conv_w, hv, name):
    tp = proj.shape[0]
    tm = _tile(tp, 256, 8)
    nh = hv // HEAD_DIM

    def body(x_ref, halo_ref, w_ref, o_ref, ext_sc):
        i, grp = pl.program_id(0), pl.program_id(1)
        ext_sc[0:8, :] = jnp.where(i == 0, 0.0, halo_ref[...])
        ext_sc[8:, :] = x_ref[...]
        c = _conv_taps(ext_sc, w_ref, tm)
        for h in range(nh):
            sl = slice(h * HEAD_DIM, (h + 1) * HEAD_DIM)
            o_ref[:, sl] = _head_post(c[:, sl], grp)

    return pl.pallas_call(
        body, name=name, grid=(tp // tm, 3),
        in_specs=[pl.BlockSpec((tm, hv), lambda i, g: (i, g)),
                  pl.BlockSpec((8, hv), lambda i, g: (jnp.maximum(i * (tm // 8) - 1, 0), g)),
                  pl.BlockSpec((CONV_K, hv), lambda i, g: (0, g))],
        out_specs=pl.BlockSpec((tm, hv), lambda i, g: (i, g)),
        out_shape=jax.ShapeDtypeStruct((tp, 3 * hv), F32),
        scratch_shapes=[pltpu.VMEM((tm + 8, hv), F32)],
        compiler_params=_params("parallel", "arbitrary"))(proj, proj, conv_w)


def _conv_bwd(proj, conv_w, dy, grp, hv, name):
    tp = proj.shape[0]
    tm = _tile(tp, 256, 8)
    ni = tp // tm
    nh = hv // HEAD_DIM

    def body(x_ref, halo_ref, w_ref, dy_ref, dx_ref, dw_ref, ext_sc, dc_sc):
        step = pl.program_id(0)
        ext_sc[0:8, :] = jnp.where(step == ni - 1, 0.0, halo_ref[...])
        ext_sc[8:, :] = x_ref[...]
        c = _conv_taps(ext_sc, w_ref, tm)
        @pl.when(step == 0)
        def _():
            dc_sc[tm:, :] = jnp.zeros((8, hv), F32)

        @pl.when(step > 0)
        def _():
            dc_sc[tm:, :] = dc_sc[0:8, :]

        for h in range(nh):
            sl = slice(h * HEAD_DIM, (h + 1) * HEAD_DIM)
            dc_sc[0:tm, sl] = _head_post_bwd(c[:, sl], dy_ref[:, sl], grp)

        @pl.when(step == 0)
        def _():
            dw_ref[...] = jnp.zeros_like(dw_ref)

        dc = dc_sc[0:tm, :]
        dx = None
        for k in range(CONV_K):
            s = CONV_K - 1 - k
            dw_ref[k:k + 1, :] += jnp.sum(dc * ext_sc[8 - s:8 - s + tm, :], axis=0, keepdims=True)
            term = w_ref[k:k + 1, :] * dc_sc[s:s + tm, :]
            dx = term if dx is None else dx + term
        dx_ref[...] = dx.astype(BF16)

    tile = lambda step: ni - 1 - step
    return pl.pallas_call(
        body, name=name, grid=(ni,),
        in_specs=[pl.BlockSpec((tm, hv), lambda s: (tile(s), grp)),
                  pl.BlockSpec((8, hv), lambda s: (jnp.maximum(tile(s) * (tm // 8) - 1, 0), grp)),
                  pl.BlockSpec((CONV_K, hv), lambda s: (0, grp)),
                  pl.BlockSpec((tm, hv), lambda s: (tile(s), 0))],
        out_specs=[pl.BlockSpec((tm, hv), lambda s: (tile(s), 0)), pl.BlockSpec((CONV_K, hv), lambda s: (0, 0))],
        out_shape=[jax.ShapeDtypeStruct((tp, hv), BF16), jax.ShapeDtypeStruct((CONV_K, hv), F32)],
        scratch_shapes=[pltpu.VMEM((tm + 8, hv), F32), pltpu.VMEM((tm + 8, hv), F32)],
        compiler_params=_params("arbitrary"))(proj, proj, conv_w, dy)


def _gdn_gates(ba, alog, dtb):
    x = ba + dtb
    softplus = jnp.maximum(x, 0.0) + jnp.log1p(jnp.exp(-jnp.abs(x)))
    return _cumsum_rows(-jnp.exp(alog) * softplus), jax.nn.sigmoid(ba)


def _gdn_chunks(states, qs, ks, vs, gates, known_inverses=None):
    mm_nn, mm_nt, mm_tn = _make_mm(False)
    hi_nn, _, _ = _make_mm(True)
    nh = len(states)
    items = range(len(qs))
    head = [i % nh for i in items]
    c = qs[0].shape[0]
    lane = lax.broadcasted_iota(jnp.int32, (c, LANES), 1)
    last_row = (lax.broadcasted_iota(jnp.int32, (c, 1), 0) == c - 1).astype(F32)
    ri = lax.broadcasted_iota(jnp.int32, (c, c), 0)
    ci = lax.broadcasted_iota(jnp.int32, (c, c), 1)
    causal = ri >= ci
    strict = ri > ci
    eye = (ri == ci).astype(F32)
    sel_a = [(lane == nh + h).astype(F32) for h in range(nh)]
    sel_b = [(lane == h).astype(F32) for h in range(nh)]

    gcol = [jnp.sum(gates[i // nh][0] * sel_a[head[i]], axis=1, keepdims=True) for i in items]
    grow = [jnp.sum(eye * gcol[i], axis=0, keepdims=True) for i in items]
    beta = [jnp.sum(gates[i // nh][1] * sel_b[head[i]], axis=1, keepdims=True) for i in items]
    decay = [jnp.where(causal, jnp.exp(jnp.where(causal, gcol[i] - grow[i], 0.0)), 0.0) for i in items]
    kb = [ks[i] * beta[i] for i in items]
    kk = [mm_nt(kb[i], ks[i]) for i in items]
    qk = [mm_nt(qs[i], ks[i]) for i in items]
    x_neg = [-jnp.where(strict, kk[i] * decay[i], 0.0) for i in items]
    if known_inverses is None:
        t_inv = _unit_lower_inverses(x_neg)
    else:
        t_inv = [_known_inverse(x_neg[i], known_inverses[i]) for i in items]
    eg = [jnp.exp(gcol[i]) for i in items]
    u = [hi_nn(t_inv[i], vs[i] * beta[i]) for i in items]
    w = [hi_nn(t_inv[i], kb[i] * eg[i]) for i in items]
    qk = [qk[i] * decay[i] for i in items]
    glast = [jnp.sum(gcol[i] * last_row, axis=0, keepdims=True) for i in items]
    q_dec = [qs[i] * eg[i] for i in items]
    k_dec = [ks[i] * jnp.exp(glast[i] - gcol[i]) for i in items]
    s_dec = [jnp.exp(glast[i]) for i in items]

    outs = []
    for first in range(0, len(qs), nh):
        chunk = range(first, first + nh)
        ws = [mm_nn(w[i], states[i - first]) for i in chunk]
        from_state = [mm_nn(q_dec[i], states[i - first]) for i in chunk]
        v_new = [u[i] - ws[i - first] for i in chunk]
        intra = [mm_nn(qk[i], v_new[i - first]) for i in chunk]
        kv = [mm_tn(k_dec[i], v_new[i - first]) for i in chunk]
        outs += [from_state[i - first] + intra[i - first] for i in chunk]
        states = [states[i - first] * s_dec[i] + kv[i - first] for i in chunk]
    return outs, states, t_inv


SCAN_CHUNKS = 2


def _scan_specs(nh, steps, rev, first_col):
    sidx = (lambda s: steps - 1 - s) if rev else (lambda s: s)
    hv = nh * HEAD_DIM
    rows = SCAN_CHUNKS * CHUNK
    cols = [pl.BlockSpec((rows, hv), lambda s, g=g: (sidx(s), first_col + g)) for g in range(3)]
    st = pl.BlockSpec((1, nh, HEAD_DIM, HEAD_DIM), lambda s: (sidx(s), 0, 0, 0))
    act = pl.BlockSpec((rows, hv), lambda s: (sidx(s), 0))
    return cols, st, act


def _chunk_heads(ref, nh):
    return [ref[j * CHUNK:(j + 1) * CHUNK, h * HEAD_DIM:(h + 1) * HEAD_DIM] for j in range(SCAN_CHUNKS)
            for h in range(nh)]


def _store_chunk_heads(ref, values, nh, dtype=None):
    for i, val in enumerate(values):
        j, h = divmod(i, nh)
        ref[j * CHUNK:(j + 1) * CHUNK, h * HEAD_DIM:(h + 1) * HEAD_DIM] = val if dtype is None else val.astype(dtype)


def _gdn_fwd(qkv, proj, alog, dtb, nh, name):
    tp = qkv.shape[0]
    steps = tp // (SCAN_CHUNKS * CHUNK)
    rows = SCAN_CHUNKS * CHUNK

    def body(q_ref, k_ref, v_ref, ba_ref, al_ref, dt_ref, o_ref, st_ref, inv_ref, s_sc):
        @pl.when(pl.program_id(0) == 0)
        def _():
            s_sc[...] = jnp.zeros_like(s_sc)

        gates = [_gdn_gates(ba_ref[j * CHUNK:(j + 1) * CHUNK, :], al_ref[...], dt_ref[...])
                 for j in range(SCAN_CHUNKS)]
        states = [s_sc[h] for h in range(nh)]
        for h in range(nh):
            st_ref[0, h] = states[h]
        outs, new_states, t_inv = _gdn_chunks(states, _chunk_heads(q_ref, nh), _chunk_heads(k_ref, nh),
                                              _chunk_heads(v_ref, nh), gates)
        _store_chunk_heads(o_ref, outs, nh)
        for h in range(nh):
            s_sc[h] = new_states[h]
        for i, t in enumerate(t_inv):
            inv_ref[0, i] = t

    cols, st, act = _scan_specs(nh, steps, False, 0)
    ba = pl.BlockSpec((rows, LANES), lambda s: (s, 10 * nh * HEAD_DIM // LANES))
    vec = pl.BlockSpec((1, LANES), lambda s: (0, 0))
    inv = pl.BlockSpec((1, SCAN_CHUNKS * nh, CHUNK, CHUNK), lambda s: (s, 0, 0, 0))
    return pl.pallas_call(
        body, name=name, grid=(steps,), in_specs=cols + [ba, vec, vec], out_specs=[act, st, inv],
        out_shape=[jax.ShapeDtypeStruct((tp, nh * HEAD_DIM), F32),
                   jax.ShapeDtypeStruct((steps, nh, HEAD_DIM, HEAD_DIM), F32),
                   jax.ShapeDtypeStruct((steps, SCAN_CHUNKS * nh, CHUNK, CHUNK), F32)],
        scratch_shapes=[pltpu.VMEM((nh, HEAD_DIM, HEAD_DIM), F32)],
        compiler_params=_params("arbitrary"))(qkv, qkv, qkv, proj, alog, dtb)


def _gdn_bwd(qkv, proj, alog, dtb, states, inverses, do, nh, name):
    tp = qkv.shape[0]
    steps = tp // (SCAN_CHUNKS * CHUNK)
    rows = SCAN_CHUNKS * CHUNK

    def body(q_ref, k_ref, v_ref, ba_ref, al_ref, dt_ref, st_ref, inv_ref, do_ref,
             dq_ref, dk_ref, dv_ref, dba_ref, dal_ref, ddt_ref, ds_sc):
        @pl.when(pl.program_id(0) == 0)
        def _():
            ds_sc[...] = jnp.zeros_like(ds_sc)
            dal_ref[...] = jnp.zeros_like(dal_ref)
            ddt_ref[...] = jnp.zeros_like(ddt_ref)

        gates, gates_vjps = [], []
        for j in range(SCAN_CHUNKS):
            g, g_vjp = jax.vjp(_gdn_gates, ba_ref[j * CHUNK:(j + 1) * CHUNK, :], al_ref[...], dt_ref[...])
            gates.append(g)
            gates_vjps.append(g_vjp)
        known = [inv_ref[0, i] for i in range(SCAN_CHUNKS * nh)]
        fn = lambda s, q, k, v, g: _gdn_chunks(s, q, k, v, g, known)[:2]
        _, vjp = jax.vjp(fn, [st_ref[0, h] for h in range(nh)], _chunk_heads(q_ref, nh), _chunk_heads(k_ref, nh),
                         _chunk_heads(v_ref, nh), gates)
        ds, dq, dk, dv, dgates = vjp((_chunk_heads(do_ref, nh), [ds_sc[h] for h in range(nh)]))
        for h in range(nh):
            ds_sc[h] = ds[h]
        _store_chunk_heads(dq_ref, dq, nh)
        _store_chunk_heads(dk_ref, dk, nh)
        _store_chunk_heads(dv_ref, dv, nh)
        for j in range(SCAN_CHUNKS):
            dba, dal, ddt = gates_vjps[j](dgates[j])
            dba_ref[j * CHUNK:(j + 1) * CHUNK, :] = dba
            dal_ref[...] += dal
            ddt_ref[...] += ddt

    cols, st, act = _scan_specs(nh, steps, True, 0)
    ba = pl.BlockSpec((rows, LANES), lambda s: (steps - 1 - s, 10 * nh * HEAD_DIM // LANES))
    vec = pl.BlockSpec((1, LANES), lambda s: (0, 0))
    inv = pl.BlockSpec((1, SCAN_CHUNKS * nh, CHUNK, CHUNK), lambda s: (steps - 1 - s, 0, 0, 0))
    return pl.pallas_call(
        body, name=name, grid=(steps,), in_specs=cols + [ba, vec, vec, st, inv, act],
        out_specs=[act, act, act, pl.BlockSpec((rows, LANES), lambda s: (steps - 1 - s, 0)), vec, vec],
        out_shape=[jax.ShapeDtypeStruct((tp, nh * HEAD_DIM), F32)] * 3
                  + [jax.ShapeDtypeStruct((tp, LANES), F32), jax.ShapeDtypeStruct((1, LANES), F32),
                     jax.ShapeDtypeStruct((1, LANES), F32)],
        scratch_shapes=[pltpu.VMEM((nh, HEAD_DIM, HEAD_DIM), F32)],
        compiler_params=_params("arbitrary"))(qkv, qkv, qkv, proj, alog, dtb, states, inverses, do)


def _swap_pairs(t):
    lane = lax.broadcasted_iota(jnp.int32, t.shape, 1)
    n = t.shape[1]
    return jnp.where(lane % 2 == 0, pltpu.roll(t, n - 1, 1), pltpu.roll(t, 1, 1))


def _rot(t, cos, sin_signed):
    return t * cos + _swap_pairs(t) * sin_signed


def _rot_t(dt, cos, sin_signed):
    return dt * cos + _swap_pairs(dt * sin_signed)


def _ret_chunks(states, qs, ks, vs, dec, xi, zeta, cd):
    mm_nn, mm_nt, mm_tn = _make_mm(False)
    nh = len(states)
    items = range(len(qs))
    scores = [mm_nt(qs[i], ks[i]) for i in items]
    kv = [mm_tn(ks[i] * zeta[i % nh], vs[i]) for i in items]
    intra = [mm_nn(scores[i] * dec[i % nh], vs[i]) for i in items]
    q_dec = [qs[i] * xi[i % nh] for i in items]
    outs = []
    for first in range(0, len(qs), nh):
        outs += [intra[first + h] + mm_nn(q_dec[first + h], states[h]) for h in range(nh)]
        states = [states[h] * cd[h] + kv[first + h] for h in range(nh)]
    return outs, states


def _ret_table_specs(nh, steps, rev):
    sidx = (lambda s: steps - 1 - s) if rev else (lambda s: s)
    rope = pl.BlockSpec((SCAN_CHUNKS * CHUNK, HEAD_DIM), lambda s: (sidx(s), 0))
    dec = pl.BlockSpec((nh, CHUNK, CHUNK), lambda s: (0, 0, 0))
    tab = pl.BlockSpec((nh, CHUNK, HEAD_DIM), lambda s: (0, 0, 0))
    cd = pl.BlockSpec((nh, 8, HEAD_DIM), lambda s: (0, 0, 0))
    return [rope, rope, dec, tab, tab, cd]


def _rotated(ref, cos_ref, sin_ref, nh, scale=1.0):
    out = []
    for j in range(SCAN_CHUNKS):
        rows = slice(j * CHUNK, (j + 1) * CHUNK)
        cos_t, sin_t = cos_ref[rows, :], sin_ref[rows, :]
        for h in range(nh):
            t = _rot(ref[rows, h * HEAD_DIM:(h + 1) * HEAD_DIM], cos_t, sin_t)
            out.append(t if scale == 1.0 else t * scale)
    return out


def _ret_fwd(proj, cos, sin, dec, xi, zeta, cd, nh, name):
    tp = proj.shape[0]
    steps = tp // (SCAN_CHUNKS * CHUNK)
    kscale = HEAD_DIM ** -0.5

    def body(q_ref, k_ref, v_ref, cos_ref, sin_ref, dec_ref, xi_ref, zeta_ref, cd_ref, o_ref, st_ref, s_sc):
        @pl.when(pl.program_id(0) == 0)
        def _():
            s_sc[...] = jnp.zeros_like(s_sc)

        heads = range(nh)
        states = [s_sc[h] for h in heads]
        for h in heads:
            st_ref[0, h] = states[h]
        outs, new_states = _ret_chunks(
            states, _rotated(q_ref, cos_ref, sin_ref, nh), _rotated(k_ref, cos_ref, sin_ref, nh, kscale),
            _chunk_heads(v_ref, nh), [dec_ref[h] for h in heads], [xi_ref[h] for h in heads],
            [zeta_ref[h] for h in heads], [cd_ref[h][0:1, :] for h in heads])
        _store_chunk_heads(o_ref, outs, nh)
        for h in heads:
            s_sc[h] = new_states[h]

    cols, st, act = _scan_specs(nh, steps, False, 3)
    return pl.pallas_call(
        body, name=name, grid=(steps,), in_specs=cols + _ret_table_specs(nh, steps, False), out_specs=[act, st],
        out_shape=[jax.ShapeDtypeStruct((tp, nh * HEAD_DIM), F32),
                   jax.ShapeDtypeStruct((steps, nh, HEAD_DIM, HEAD_DIM), F32)],
        scratch_shapes=[pltpu.VMEM((nh, HEAD_DIM, HEAD_DIM), F32)],
        compiler_params=_params("arbitrary"))(proj, proj, proj, cos, sin, dec, xi, zeta, cd)


def _ret_bwd(proj, cos, sin, dec, xi, zeta, cd, states, do, nh, name):
    tp = proj.shape[0]
    steps = tp // (SCAN_CHUNKS * CHUNK)
    kscale = HEAD_DIM ** -0.5

    def body(q_ref, k_ref, v_ref, cos_ref, sin_ref, dec_ref, xi_ref, zeta_ref, cd_ref, st_ref, do_ref,
             dq_ref, dk_ref, dv_ref, ds_sc):
        @pl.when(pl.program_id(0) == 0)
        def _():
            ds_sc[...] = jnp.zeros_like(ds_sc)

        heads = range(nh)
        fn = functools.partial(_ret_chunks, dec=[dec_ref[h] for h in heads], xi=[xi_ref[h] for h in heads],
                               zeta=[zeta_ref[h] for h in heads], cd=[cd_ref[h][0:1, :] for h in heads])
        _, vjp = jax.vjp(fn, [st_ref[0, h] for h in heads], _rotated(q_ref, cos_ref, sin_ref, nh),
                         _rotated(k_ref, cos_ref, sin_ref, nh, kscale), _chunk_heads(v_ref, nh))
        ds, dq, dk, dv = vjp((_chunk_heads(do_ref, nh), [ds_sc[h] for h in heads]))
        for h in heads:
            ds_sc[h] = ds[h]
        for i in range(SCAN_CHUNKS * nh):
            rows = slice((i // nh) * CHUNK, (i // nh + 1) * CHUNK)
            cos_t, sin_t = cos_ref[rows, :], sin_ref[rows, :]
            dq[i] = _rot_t(dq[i], cos_t, sin_t)
            dk[i] = _rot_t(dk[i] * kscale, cos_t, sin_t)
        _store_chunk_heads(dq_ref, dq, nh, BF16)
        _store_chunk_heads(dk_ref, dk, nh, BF16)
        _store_chunk_heads(dv_ref, dv, nh, BF16)

    cols, st, act = _scan_specs(nh, steps, True, 3)
    return pl.pallas_call(
        body, name=name, grid=(steps,), in_specs=cols + _ret_table_specs(nh, steps, True) + [st, act],
        out_specs=[act, act, act],
        out_shape=[jax.ShapeDtypeStruct((tp, nh * HEAD_DIM), BF16)] * 3,
        scratch_shapes=[pltpu.VMEM((nh, HEAD_DIM, HEAD_DIM), F32)],
        compiler_params=_params("arbitrary"))(proj, proj, proj, cos, sin, dec, xi, zeta, cd, states, do)


def _gdn_out(o, z, gnorm):
    return o * lax.rsqrt(jnp.mean(o * o, axis=-1, keepdims=True) + EPS) * gnorm * _silu(z)


def _ret_out(o, rg, rnorm):
    mu = jnp.mean(o, axis=-1, keepdims=True)
    var = jnp.mean(jnp.square(o - mu), axis=-1, keepdims=True)
    return _silu(rg) * ((o - mu) * lax.rsqrt(var + EPS) * rnorm)


def _dsilu(x, sg):
    return sg * (1.0 + x * (1.0 - sg))


def _gdn_out_bwd(o, z, gnorm, dy):
    r = lax.rsqrt(jnp.mean(o * o, axis=-1, keepdims=True) + EPS)
    xh = o * r
    sg = jax.nn.sigmoid(z)
    sz = z * sg
    t = dy * (gnorm * sz)
    do = r * (t - xh * jnp.mean(t * xh, axis=-1, keepdims=True))
    e = dy * xh
    return do, e * (gnorm * _dsilu(z, sg)), jnp.sum(e * sz, axis=0, keepdims=True)


def _ret_out_bwd(o, rg, rnorm, dy):
    oc = o - jnp.mean(o, axis=-1, keepdims=True)
    rs = lax.rsqrt(jnp.mean(oc * oc, axis=-1, keepdims=True) + EPS)
    xh = oc * rs
    sg = jax.nn.sigmoid(rg)
    srg = rg * sg
    t = dy * (rnorm * srg)
    do = rs * (t - jnp.mean(t, axis=-1, keepdims=True) - xh * jnp.mean(t * xh, axis=-1, keepdims=True))
    e = dy * xh
    return do, e * (rnorm * _dsilu(rg, sg)), jnp.sum(e * srg, axis=0, keepdims=True)


def _post_specs(tm, hv, d):
    row = lambda col: pl.BlockSpec((tm, hv), lambda i: (i, col))
    return dict(
        oa=row(0), ob=row(0), z=row(6), rg=row(7), ga=row(8), gb=row(9),
        gnorm=pl.BlockSpec((1, HEAD_DIM), lambda i: (0, 0)), rnorm=pl.BlockSpec((1, hv), lambda i: (0, 0)),
        w=pl.BlockSpec((hv, d), lambda i: (0, 0)), res=pl.BlockSpec((tm, d), lambda i: (i, 0)))


def _post_fwd(oa, ob, proj, gnorm, rnorm, wbg, wbr, wo, h1, name):
    tp, d = h1.shape
    hv = oa.shape[1]
    nh = hv // HEAD_DIM
    tm = _tile(tp, 256, 8)

    def body(oa_ref, ob_ref, z_ref, rg_ref, ga_ref, gb_ref, gn_ref, rn_ref, wbg_ref, wbr_ref, wo_ref, h_ref,
             o_ref, ya_sc, yb_sc):
        for h in range(nh):
            sl = slice(h * HEAD_DIM, (h + 1) * HEAD_DIM)
            ya_sc[:, sl] = _gdn_out(oa_ref[:, sl], z_ref[:, sl], gn_ref[...]).astype(BF16)
            yb_sc[:, sl] = _ret_out(ob_ref[:, sl], rg_ref[:, sl], rn_ref[:, sl]).astype(BF16)
        pa = jnp.dot(ya_sc[...], wbg_ref[...], preferred_element_type=F32)
        pb = jnp.dot(yb_sc[...], wbr_ref[...], preferred_element_type=F32)
        merged = jax.nn.sigmoid(ga_ref[...]) * pa + jax.nn.sigmoid(gb_ref[...]) * pb
        o_ref[...] = h_ref[...] + jnp.dot(merged.astype(BF16), wo_ref[...], preferred_element_type=F32)

    sp = _post_specs(tm, hv, d)
    return pl.pallas_call(
        body, name=name, grid=(tp // tm,),
        in_specs=[sp["oa"], sp["ob"], sp["z"], sp["rg"], sp["ga"], sp["gb"], sp["gnorm"], sp["rnorm"],
                  sp["w"], sp["w"], sp["w"], sp["res"]],
        out_specs=sp["res"], out_shape=jax.ShapeDtypeStruct((tp, d), F32),
        scratch_shapes=[pltpu.VMEM((tm, hv), BF16), pltpu.VMEM((tm, hv), BF16)],
        compiler_params=_params("parallel"))(oa, ob, proj, proj, proj, proj, gnorm, rnorm, wbg, wbr, wo, h1)


def _post_bwd(oa, ob, proj, gnorm, rnorm, wbg, wbr, wo, dh2, name):
    tp, d = dh2.shape
    hv = oa.shape[1]
    nh = hv // HEAD_DIM
    tm = _tile(tp, 256, 8)

    def body(oa_ref, ob_ref, z_ref, rg_ref, ga_ref, gb_ref, gn_ref, rn_ref, wbg_ref, wbr_ref, wo_ref, dh_ref,
             doa_ref, dob_ref, dg_ref, ya_ref, yb_ref, mg_ref, dpa_ref, dpb_ref, dgn_ref, drn_ref,
             dya_sc, dyb_sc):
        @pl.when(pl.program_id(0) == 0)
        def _():
            dgn_ref[...] = jnp.zeros_like(dgn_ref)
            drn_ref[...] = jnp.zeros_like(drn_ref)

        for h in range(nh):
            sl = slice(h * HEAD_DIM, (h + 1) * HEAD_DIM)
            ya_ref[:, sl] = _gdn_out(oa_ref[:, sl], z_ref[:, sl], gn_ref[...]).astype(BF16)
            yb_ref[:, sl] = _ret_out(ob_ref[:, sl], rg_ref[:, sl], rn_ref[:, sl]).astype(BF16)
        pa = jnp.dot(ya_ref[...], wbg_ref[...], preferred_element_type=F32)
        pb = jnp.dot(yb_ref[...], wbr_ref[...], preferred_element_type=F32)
        sa = jax.nn.sigmoid(ga_ref[...])
        sb = jax.nn.sigmoid(gb_ref[...])
        mg_ref[...] = (sa * pa + sb * pb).astype(BF16)
        dm = lax.dot_general(dh_ref[...].astype(BF16), wo_ref[...], NT, preferred_element_type=F32)
        dpa = (dm * sa).astype(BF16)
        dpb = (dm * sb).astype(BF16)
        dpa_ref[...] = dpa
        dpb_ref[...] = dpb
        dg_ref[:, 2 * hv:3 * hv] = (dm * pa * sa * (1.0 - sa)).astype(BF16)
        dg_ref[:, 3 * hv:4 * hv] = (dm * pb * sb * (1.0 - sb)).astype(BF16)
        dya_sc[...] = lax.dot_general(dpa, wbg_ref[...], NT, preferred_element_type=F32)
        dyb_sc[...] = lax.dot_general(dpb, wbr_ref[...], NT, preferred_element_type=F32)
        for h in range(nh):
            sl = slice(h * HEAD_DIM, (h + 1) * HEAD_DIM)
            doa, dz, dgn = _gdn_out_bwd(oa_ref[:, sl], z_ref[:, sl], gn_ref[...], dya_sc[:, sl])
            doa_ref[:, sl] = doa
            dg_ref[:, sl] = dz.astype(BF16)
            dgn_ref[...] += dgn
            dob, drg, drn = _ret_out_bwd(ob_ref[:, sl], rg_ref[:, sl], rn_ref[:, sl], dyb_sc[:, sl])
            dob_ref[:, sl] = dob
            dg_ref[:, hv + h * HEAD_DIM:hv + (h + 1) * HEAD_DIM] = drg.astype(BF16)
            drn_ref[:, sl] += drn

    sp = _post_specs(tm, hv, d)
    act = pl.BlockSpec((tm, hv), lambda i: (i, 0))
    return pl.pallas_call(
        body, name=name, grid=(tp // tm,),
        in_specs=[sp["oa"], sp["ob"], sp["z"], sp["rg"], sp["ga"], sp["gb"], sp["gnorm"], sp["rnorm"],
                  sp["w"], sp["w"], sp["w"], sp["res"]],
        out_specs=[act, act, pl.BlockSpec((tm, 4 * hv), lambda i: (i, 0)), act, act, sp["res"], sp["res"],
                   sp["res"], sp["gnorm"], sp["rnorm"]],
        out_shape=[jax.ShapeDtypeStruct((tp, hv), F32), jax.ShapeDtypeStruct((tp, hv), F32),
                   jax.ShapeDtypeStruct((tp, 4 * hv), BF16), jax.ShapeDtypeStruct((tp, hv), BF16),
                   jax.ShapeDtypeStruct((tp, hv), BF16), jax.ShapeDtypeStruct((tp, d), BF16),
                   jax.ShapeDtypeStruct((tp, d), BF16), jax.ShapeDtypeStruct((tp, d), BF16),
                   jax.ShapeDtypeStruct((1, HEAD_DIM), F32), jax.ShapeDtypeStruct((1, hv), F32)],
        scratch_shapes=[pltpu.VMEM((tm, hv), F32), pltpu.VMEM((tm, hv), F32)],
        compiler_params=_params("arbitrary"))(oa, ob, proj, proj, proj, proj, gnorm, rnorm, wbg, wbr, wo, dh2)


def _final(h3, gain, target, name):
    tp, d = h3.shape
    tm = HEAD_ROWS

    def body(h_ref, g_ref, t_ref, loss_ref, dh_ref, dgain_ref):
        i = pl.program_id(0)

        @pl.when(i == 0)
        def _():
            loss_ref[...] = jnp.zeros_like(loss_ref)
            dgain_ref[...] = jnp.zeros_like(dgain_ref)

        xh, r = _rms_parts(h_ref[...])
        err = jnp.where(i == 0, 0.0, xh * g_ref[...] - t_ref[...])
        dx, dg = _rms_bwd(err * (1.0 / d), xh, r, g_ref[...])
        dh_ref[...] = dx
        dgain_ref[...] += dg
        loss_ref[...] += 0.5 * jnp.sum(jnp.mean(err * err, axis=-1, keepdims=True), axis=0, keepdims=True)

    row = pl.BlockSpec((tm, d), lambda i: (i, 0))
    vec = pl.BlockSpec((1, d), lambda i: (0, 0))
    return pl.pallas_call(
        body, name=name, grid=(tp // tm,),
        in_specs=[row, vec, pl.BlockSpec((tm, d), lambda i: (jnp.maximum(i - 1, 0), 0))],
        out_specs=[pl.BlockSpec((1, LANES), lambda i: (0, 0)), row, vec],
        out_shape=[jax.ShapeDtypeStruct((1, LANES), F32), jax.ShapeDtypeStruct((tp, d), F32),
                   jax.ShapeDtypeStruct((1, d), F32)],
        compiler_params=_params("arbitrary"))(h3, gain, target)


def _peer(k):
    x, y, c = lax.axis_index("x"), lax.axis_index("y"), lax.axis_index("c")
    return (1 - x if k & 4 else x, 1 - y if k & 2 else y, 1 - c if k & 1 else c)


def _my_index():
    return 4 * lax.axis_index("x") + 2 * lax.axis_index("y") + lax.axis_index("c")


def _exchange(bufs, scatter, name):
    n = len(bufs)

    def body(*refs):
        _exchange_copies(refs[:n], refs[n:2 * n], refs[2 * n:], scatter, True, True)

    hbm, out_shape, sems = _exchange_refs(bufs)
    return pl.pallas_call(
        body, name=name, in_specs=hbm, out_specs=hbm, out_shape=out_shape, scratch_shapes=sems,
        compiler_params=pltpu.CompilerParams(has_side_effects=True))(*bufs)


def _gather_via_sibling(bufs, name):
    n = len(bufs)

    def body(*refs):
        x_refs, out_refs = refs[:n], refs[n:2 * n]
        send_sems, recv_sems, local_sems = refs[2 * n:]
        x, y, c = lax.axis_index("x"), lax.axis_index("y"), lax.axis_index("c")
        me, sibling = (x, y, c), (x, y, 1 - c)
        chips = [(1 - x, y), (x, 1 - y), (1 - x, 1 - y)]
        rows = lambda a, dev: out_refs[a].at[4 * dev[0] + 2 * dev[1] + dev[2]]

        def copy(k, a, block, to, src=None):
            return pltpu.make_async_remote_copy(
                src_ref=rows(a, block) if src is None else src, dst_ref=rows(a, block),
                send_sem=send_sems.at[k * n + a], recv_sem=recv_sems.at[k * n + a],
                device_id=to, device_id_type=pl.DeviceIdType.MESH)

        mine = [pltpu.make_async_copy(x_refs[a], rows(a, me), local_sems.at[a]) for a in range(n)]
        first = [copy(0, a, me, sibling, src=x_refs[a]) for a in range(n)]
        first += [copy(1 + j, a, me, (*chip, c), src=x_refs[a]) for j, chip in enumerate(chips) for a in range(n)]
        for cp in mine + first:
            cp.start()
        passed = []
        for j, chip in enumerate(chips):
            for a in range(n):
                copy(1 + j, a, (*chip, c), me).wait_recv()
                passed.append(copy(4 + j, a, (*chip, c), sibling))
                passed[-1].start()
        for a in range(n):
            copy(0, a, sibling, me).wait_recv()
        for j, chip in enumerate(chips):
            for a in range(n):
                copy(4 + j, a, (*chip, 1 - c), me).wait_recv()
        for cp in first + passed:
            cp.wait_send()
        for cp in mine:
            cp.wait()

    hbm, out_shape, sems = _exchange_refs(bufs)
    return pl.pallas_call(
        body, name=name, in_specs=hbm, out_specs=hbm, out_shape=out_shape, scratch_shapes=sems,
        compiler_params=pltpu.CompilerParams(has_side_effects=True))(*bufs)


def _exchange_refs(bufs):
    n = len(bufs)
    return ([pl.BlockSpec(memory_space=pl.ANY)] * n,
            [jax.ShapeDtypeStruct((N_DEV,) + b.shape[-2:], b.dtype) for b in bufs],
            [pltpu.SemaphoreType.DMA(((N_DEV - 1) * n,)), pltpu.SemaphoreType.DMA(((N_DEV - 1) * n,)),
             pltpu.SemaphoreType.DMA((n,))])


def _exchange_copies(x_refs, out_refs, sems, scatter, start, wait):
    n = len(x_refs)
    send_sems, recv_sems, local_sems = sems
    me = _my_index()
    copies = []
    for a in range(n):
        copies.append(pltpu.make_async_copy(x_refs[a].at[me] if scatter else x_refs[a], out_refs[a].at[me],
                                            local_sems.at[a]))
    sends = []
    arrivals = []
    for k in range(1, N_DEV):
        x, y, c = _peer(k)
        peer = 4 * x + 2 * y + c
        for a in range(n):
            sem = (k - 1) * n + a
            sends.append(pltpu.make_async_remote_copy(
                src_ref=x_refs[a].at[peer] if scatter else x_refs[a], dst_ref=out_refs[a].at[me],
                send_sem=send_sems.at[sem], recv_sem=recv_sems.at[sem],
                device_id=(x, y, c), device_id_type=pl.DeviceIdType.MESH))
            landed = out_refs[a].at[peer]
            arrivals.append(pltpu.make_async_remote_copy(
                src_ref=landed, dst_ref=landed, send_sem=send_sems.at[sem], recv_sem=recv_sems.at[sem],
                device_id=(x, y, c), device_id_type=pl.DeviceIdType.MESH))
    if start:
        for cp in copies + sends:
            cp.start()
    if wait:
        for cp in arrivals:
            cp.wait_recv()
        for cp in sends:
            cp.wait_send()
        for cp in copies:
            cp.wait()


def _carried_call(body, carry, first, last, *, name, grid, in_specs, out_specs, out_shape, scratch_shapes=()):
    in_specs, out_specs, out_shape = list(in_specs), list(out_specs), list(out_shape)
    semantics = ("arbitrary",) * len(grid)
    if carry is None:
        call = pl.pallas_call(body, name=name, grid=grid, in_specs=in_specs, out_specs=out_specs,
                              out_shape=out_shape, scratch_shapes=list(scratch_shapes),
                              compiler_params=_params(*semantics))
        return lambda *args: (call(*args), [])
    bufs, scatter = carry
    n, n_in, n_out, n_scratch = len(bufs), len(in_specs), len(out_specs), len(scratch_shapes)
    hbm, x_shapes, sems = _exchange_refs(bufs)

    def full_body(*refs):
        ins, x_refs = refs[:n_in], refs[n_in:n_in + n]
        outs, xo_refs = refs[n_in + n:n_in + n + n_out], refs[n_in + n + n_out:n_in + 2 * n + n_out]
        scratch = refs[n_in + 2 * n + n_out:n_in + 2 * n + n_out + n_scratch]
        x_sems = refs[n_in + 2 * n + n_out + n_scratch:]

        @pl.when(first())
        def _():
            _exchange_copies(x_refs, xo_refs, x_sems, scatter, True, False)

        body(*ins, *outs, *scratch)

        @pl.when(last())
        def _():
            _exchange_copies(x_refs, xo_refs, x_sems, scatter, False, True)

    call = pl.pallas_call(full_body, name=name, grid=grid, in_specs=in_specs + hbm, out_specs=out_specs + hbm,
                          out_shape=out_shape + x_shapes, scratch_shapes=list(scratch_shapes) + sems,
                          compiler_params=_params(*semantics))

    def run(*args):
        res = call(*args, *bufs)
        return res[:n_out], res[n_out:]
    return run


def _adamw(w, g, m, v, name):
    r, c = w.shape
    parts = g.ndim == 3
    tr = _tile(r, 256, 16 if parts else 8)
    c1 = 1.0 - ADAM_B1 ** ADAM_STEP
    c2 = 1.0 - ADAM_B2 ** ADAM_STEP

    def body(w_ref, g_ref, m_ref, v_ref, go_ref, d_ref, mo_ref, vo_ref):
        if parts:
            g = g_ref[0].astype(F32)
            for q in range(1, N_DEV):
                g = g + g_ref[q].astype(F32)
        else:
            g = g_ref[...]
        m = ADAM_B1 * m_ref[...] + (1.0 - ADAM_B1) * g
        v = ADAM_B2 * v_ref[...] + (1.0 - ADAM_B2) * (g * g)
        go_ref[...] = g
        d_ref[...] = -ADAM_LR * ((m / c1) / (jnp.sqrt(v / c2) + ADAM_EPS) + ADAM_WD * w_ref[...])
        mo_ref[...] = m
        vo_ref[...] = v

    blk = pl.BlockSpec((tr, c), lambda i: (i, 0))
    g_spec = pl.BlockSpec((N_DEV, tr, c), lambda i: (0, i, 0)) if parts else blk
    return pl.pallas_call(
        body, name=name, grid=(r // tr,), in_specs=[blk, g_spec, blk, blk], out_specs=[blk] * 4,
        out_shape=[jax.ShapeDtypeStruct((r, c), F32)] * 4,
        compiler_params=_params("parallel"))(w, g, m, v)


def _win_segments(hv, nh):
    o_z, o_b = 3 * hv, 4 * hv
    o_r = o_b + 2 * nh
    return [(0, 0, 3 * hv), (3 * hv, o_r, 3 * hv), (6 * hv, o_z, hv), (7 * hv, o_r + 3 * hv, 3 * hv),
            (10 * hv, o_b, 2 * nh)]


def _win_from_shards(shards, hv, nh):
    _, d, cs = shards.shape
    pieces = []
    for _, src, width in _win_segments(hv, nh):
        lo = src
        while lo < src + width:
            p = lo // cs
            hi = min(src + width, (p + 1) * cs)
            pieces.append(shards[p][:, lo - p * cs:hi - p * cs])
            lo = hi
    pieces.append(jnp.zeros((d, LANES - 2 * nh), shards.dtype))
    return jnp.concatenate(pieces, axis=1)


def _win_grad_to_shards(parts, hv, nh, cs):
    segments = _win_segments(hv, nh)
    starts = [sum(p.shape[1] for p in parts[:i]) for i in range(len(parts))]

    def columns(a, b):
        out = []
        for part, start in zip(parts, starts):
            lo, hi = max(a, start), min(b, start + part.shape[1])
            if lo < hi:
                out.append(part[:, lo - start:hi - start])
        return out

    shards = []
    for p in range(N_DEV):
        pieces = []
        lo = p * cs
        while lo < (p + 1) * cs:
            here, src, width = next(s for s in segments if s[1] <= lo < s[1] + s[2])
            hi = min((p + 1) * cs, src + width)
            pieces += columns(here + lo - src, here + hi - src)
            lo = hi
        shards.append(jnp.concatenate(pieces, axis=1))
    return jnp.stack(shards)


def _rope_tables(tp):
    pos = jnp.arange(tp, dtype=F32) - float(PAD_FRONT)
    inv = 1.0 / (ROPE_BASE ** jnp.linspace(0.0, 1.0, HEAD_DIM // 2, dtype=F32))
    ang = pos[:, None] * inv[None, :]
    cos = jnp.repeat(jnp.cos(ang), 2, axis=1)
    sin = jnp.repeat(jnp.sin(ang), 2, axis=1) * jnp.tile(jnp.array([-1.0, 1.0], F32), HEAD_DIM // 2)[None, :]
    return cos, sin


def _retention_tables(nh):
    log_gamma = jnp.log1p(-jnp.exp2(-5.0 - jnp.arange(nh, dtype=F32)))
    pos = jnp.arange(CHUNK, dtype=F32)
    causal = pos[:, None] >= pos[None, :]
    diff = pos[:, None] - pos[None, :]
    dec = jnp.where(causal, jnp.exp(jnp.where(causal, diff, 0.0) * log_gamma[:, None, None]), 0.0)
    ones = jnp.ones((1, 1, HEAD_DIM), F32)
    xi = jnp.exp((pos + 1.0)[None, :] * log_gamma[:, None])[:, :, None] * ones
    zeta = jnp.exp((CHUNK - 1.0 - pos)[None, :] * log_gamma[:, None])[:, :, None] * ones
    cd = jnp.exp(CHUNK * log_gamma)[:, None, None] * jnp.ones((1, 8, HEAD_DIM), F32)
    return dec, xi, zeta, cd


SHARDED = ("meta_tokens", "ffn1_w_in", "ffn1_w_out", "w_in", "gdn_conv_w", "w_branch_gdn", "w_branch_ret",
           "w_out", "ffn2_w_in", "ffn2_w_out")
COLUMN_SHARDED = ("meta_tokens", "ffn1_w_in", "w_in", "gdn_conv_w", "ffn2_w_in")
EXACT_F32 = ("meta_tokens", "gdn_conv_w")
REPLICATED = ("ffn1_norm", "mix_norm", "gdn_a_log", "gdn_dt_bias", "gdn_out_norm", "ret_out_norm", "ffn2_norm",
              "final_norm")
WEIGHTS = ("meta_tokens", "ffn1_norm", "ffn1_w_in", "ffn1_w_out", "mix_norm", "w_in", "gdn_conv_w", "gdn_a_log",
           "gdn_dt_bias", "gdn_out_norm", "ret_out_norm", "w_branch_gdn", "w_branch_ret", "w_out", "ffn2_norm",
           "ffn2_w_in", "ffn2_w_out", "final_norm")


def _as2d(a):
    if a.ndim == 3:
        return a[0]
    if a.ndim == 1:
        return a[None, :]
    return a


def _rows_of(shards):
    return shards.reshape(-1, shards.shape[2])


def _cols_of(shards):
    return shards.transpose(1, 0, 2).reshape(shards.shape[1], -1)


def _row_shards(a):
    return a.reshape(N_DEV, -1, a.shape[1])


def _col_shards(a):
    return a.reshape(a.shape[0], N_DEV, -1).transpose(1, 0, 2)


GATHER_FIRST = ("meta_tokens", "ffn1_w_in", "ffn1_w_out")
GATHER_BEHIND_FFN1 = ("w_in", "gdn_conv_w")
GATHER_BEHIND_PROJ = ("w_branch_gdn", "w_branch_ret", "w_out", "ffn2_w_in", "ffn2_w_out")
SCATTER_BEHIND_DN2 = ("ffn2_w_in", "ffn2_w_out", "w_branch_gdn", "w_branch_ret", "w_out")
SCATTER_BEHIND_FFN1 = ("w_in", "gdn_conv_w")
SCATTER_BEHIND_DWG = ("meta_tokens", "ffn1_w_out")
SCATTER_LAST = ("ffn1_w_in",)


def _device_step(x, target, send, rep):
    seq, d = x.shape
    tp = HEAD_ROWS + seq
    hv = d
    nh = hv // HEAD_DIM
    assert tp % (SCAN_CHUNKS * CHUNK) == 0 and tp % HEAD_ROWS == 0
    bf16_shards = lambda grads, names: [grads[n].astype(BF16) for n in names]

    pad_lanes = lambda row: jnp.pad(row, ((0, 0), (nh, LANES - 2 * nh)))
    alog = pad_lanes(rep["gdn_a_log"])
    dtb = pad_lanes(rep["gdn_dt_bias"])
    cos, sin = _rope_tables(tp)
    dec, xi, zeta, cd = _retention_tables(nh)

    got = dict(zip(GATHER_FIRST, _gather_via_sibling([send[n] for n in GATHER_FIRST], "gather_ffn1")))
    h0 = jnp.concatenate([jnp.zeros((PAD_FRONT, d), F32), _cols_of(got["meta_tokens"]), x], axis=0)
    f1i, f1o = got["ffn1_w_in"], _rows_of(got["ffn1_w_out"])
    (h1, hid1, dup1, dgate1), moved = _ffn_fwd(h0, rep["ffn1_norm"], f1i, f1o, "ffn1_fwd",
                                               ([send[n] for n in GATHER_BEHIND_FFN1], False))
    got.update(zip(GATHER_BEHIND_FFN1, moved))
    wp = _win_from_shards(got["w_in"], hv, nh)
    conv_w = _cols_of(got["gdn_conv_w"])
    (proj, n2), moved = _proj_fwd(h1, rep["mix_norm"], wp, "proj_fwd",
                                  ([send[n] for n in GATHER_BEHIND_PROJ], False))
    got.update(zip(GATHER_BEHIND_PROJ, moved))
    wbg, wbr, wo = _rows_of(got["w_branch_gdn"]), _rows_of(got["w_branch_ret"]), _rows_of(got["w_out"])
    f2i, f2o = got["ffn2_w_in"], _rows_of(got["ffn2_w_out"])
    qkv = _conv_fwd(proj, conv_w, hv, "conv_fwd")
    oa, s_gdn, t_gdn = _gdn_fwd(qkv, proj, alog, dtb, nh, "gdn_fwd")
    ob, s_ret = _ret_fwd(proj, cos, sin, dec, xi, zeta, cd, nh, "ret_fwd")
    h2 = _post_fwd(oa, ob, proj, rep["gdn_out_norm"], rep["ret_out_norm"], wbg, wbr, wo, h1, "post_fwd")
    (h3, hid2, dup2, dgate2), _ = _ffn_fwd(h2, rep["ffn2_norm"], f2i, f2o, "ffn2_fwd")
    loss_row, dh3, d_final = _final(h3, rep["final_norm"], target, "final")

    (dh2, d_f2n, n3, dag2, dau2), _ = _ffn_bwd(h2, dh3, rep["ffn2_norm"], f2i, f2o, dup2, dgate2, "ffn2_bwd")
    grads = {"ffn2_w_in": jnp.concatenate([_matmul_tn_blocks(n3, dag2, "ffn2_dwg"),
                                           _matmul_tn_blocks(n3, dau2, "ffn2_dwu")]),
             "ffn2_w_out": _row_shards(_matmul_tn_blocks(hid2, dh3, "ffn2_dwo", 0.5))}

    doa, dob, dgate, ya, yb, merged, dpa, dpb, d_gn, d_rn = _post_bwd(
        oa, ob, proj, rep["gdn_out_norm"], rep["ret_out_norm"], wbg, wbr, wo, dh2, "post_bwd")
    grads["w_branch_gdn"] = _row_shards(_matmul_tn(ya, dpa, "dw_branch_gdn"))
    grads["w_branch_ret"] = _row_shards(_matmul_tn(yb, dpb, "dw_branch_ret"))
    grads["w_out"] = _row_shards(_matmul_tn(merged, dh2, "dw_out"))

    d_ret = _ret_bwd(proj, cos, sin, dec, xi, zeta, cd, s_ret, dob, nh, "ret_bwd")
    gdn_grads = _gdn_bwd(qkv, proj, alog, dtb, s_gdn, t_gdn, doa, nh, "gdn_bwd")
    dba, d_alog, d_dtb = gdn_grads[3:]
    dpre, g_conv = [], []
    for grp, tag in enumerate("qkv"):
        dx, dw = _conv_bwd(proj, conv_w, gdn_grads[grp], grp, hv, "conv_bwd_" + tag)
        dpre.append(dx)
        g_conv.append(dw)
    grads["gdn_conv_w"] = _col_shards(jnp.concatenate(g_conv, axis=1))

    wide = dpre + list(d_ret) + [dgate]
    dn2, moved = _matmul_nt_parts(wide, wp[:, :10 * hv], None, "dn2_wide",
                                  (bf16_shards(grads, SCATTER_BEHIND_DN2), True))
    parts = dict(zip(SCATTER_BEHIND_DN2, moved))
    dn2, _ = _matmul_nt_parts([dba], wp[:, 10 * hv:], dn2, "dn2_beta_alpha")
    g_wp = [_matmul_tn(n2, dg, "dw_in_%d" % idx) for idx, dg in enumerate(wide + [dba])]
    grads["w_in"] = _win_grad_to_shards(g_wp, hv, nh, send["w_in"].shape[1])
    dh1, d_mixn = _norm_bwd(h1, rep["mix_norm"], dn2, dh2, "mix_norm_bwd")

    (dh0, d_f1n, n1, dag1, dau1), moved = _ffn_bwd(h0, dh1, rep["ffn1_norm"], f1i, f1o, dup1, dgate1, "ffn1_bwd",
                                                   (bf16_shards(grads, SCATTER_BEHIND_FFN1), True))
    parts.update(zip(SCATTER_BEHIND_FFN1, moved))
    grads["ffn1_w_out"] = _row_shards(_matmul_tn_blocks(hid1, dh1, "ffn1_dwo", 0.5))
    grads["meta_tokens"] = _col_shards(dh0[PAD_FRONT:HEAD_ROWS])
    g_gate, moved = _matmul_tn_blocks(n1, dag1, "ffn1_dwg", carry=(bf16_shards(grads, SCATTER_BEHIND_DWG), True))
    parts.update(zip(SCATTER_BEHIND_DWG, moved))
    grads["ffn1_w_in"] = jnp.concatenate([g_gate, _matmul_tn_blocks(n1, dau1, "ffn1_dwu")])
    parts.update(zip(SCATTER_LAST, _exchange(bf16_shards(grads, SCATTER_LAST), True, "scatter_ffn1")))

    small = {"ffn1_norm": d_f1n, "mix_norm": d_mixn, "gdn_a_log": d_alog[:, nh:2 * nh],
             "gdn_dt_bias": d_dtb[:, nh:2 * nh], "gdn_out_norm": d_gn, "ret_out_norm": d_rn, "ffn2_norm": d_f2n,
             "final_norm": d_final}
    return loss_row[0, 0], dh0[HEAD_ROWS:], parts, small


def kernel(x, meta_tokens, ffn1_norm, ffn1_w_in, ffn1_w_out, mix_norm, w_in, gdn_conv_w, gdn_a_log, gdn_dt_bias, gdn_out_norm, ret_out_norm, w_branch_gdn, w_branch_ret, w_out, ffn2_norm, ffn2_w_in, ffn2_w_out, final_norm, loss_target, m_meta_tokens, m_ffn1_norm, m_ffn1_w_in, m_ffn1_w_out, m_mix_norm, m_w_in, m_gdn_conv_w, m_gdn_a_log, m_gdn_dt_bias, m_gdn_out_norm, m_ret_out_norm, m_w_branch_gdn, m_w_branch_ret, m_w_out, m_ffn2_norm, m_ffn2_w_in, m_ffn2_w_out, m_final_norm, v_meta_tokens, v_ffn1_norm, v_ffn1_w_in, v_ffn1_w_out, v_mix_norm, v_w_in, v_gdn_conv_w, v_gdn_a_log, v_gdn_dt_bias, v_gdn_out_norm, v_ret_out_norm, v_w_branch_gdn, v_w_branch_ret, v_w_out, v_ffn2_norm, v_ffn2_w_in, v_ffn2_w_out, v_final_norm):
    given = dict(locals())
    params = {n: _as2d(given[n]) for n in WEIGHTS}
    local = {n: params[n] for n in SHARDED}
    rep = {n: params[n] for n in REPLICATED}

    send = {n: local[n] if n in EXACT_F32 else local[n].astype(BF16) for n in SHARDED}
    loss_sum, grad_x, parts, small = _device_step(x[0], loss_target[0], send, rep)
    parts.update(zip(REPLICATED, _exchange([small[n] for n in REPLICATED], False, "gather_small_grads")))
    loss = lax.psum(loss_sum, ("x", "y", "c"))

    outs = {}
    for n in WEIGHTS:
        res = _adamw(params[n], parts[n], _as2d(given["m_" + n]), _as2d(given["v_" + n]), "adamw_" + n)
        outs[n] = [r.reshape(given[n].shape) for r in res]
    return (loss, grad_x[None], *[outs[n][0] for n in WEIGHTS], *[outs[n][1] for n in WEIGHTS],
            *[outs[n][2] for n in WEIGHTS], *[outs[n][3] for n in WEIGHTS])
```

```python
import functools
import math

import numpy as np
import jax
import jax.numpy as jnp
from jax import lax
from jax.experimental import pallas as pl
from jax.experimental.pallas import tpu as pltpu

F32 = jnp.float32
BF16 = jnp.bfloat16

N_DEV = 8
N_META = 16
CHUNK = 64
HEAD_DIM = 128
CONV_K = 4
ROPE_BASE = 10000.0
EPS = 1e-6
PAD_FRONT = 240
HEAD_ROWS = PAD_FRONT + N_META
LANES = 128
VMEM_LIMIT_BYTES = 56 * 1024 * 1024

ADAM_LR = 0.001
ADAM_B1 = 0.9
ADAM_B2 = 0.999
ADAM_EPS = 1e-08
ADAM_WD = 0.01
ADAM_STEP = 10

NN = (((1,), (0,)), ((), ()))
NT = (((1,), (1,)), ((), ()))
TN = (((0,), (0,)), ((), ()))


def _tile(n, target, mult):
    best = 0
    for t in range(mult, min(n, target) + 1, mult):
        if n % t == 0:
            best = t
    return best if best else n


def _params(*semantics):
    return pltpu.CompilerParams(dimension_semantics=semantics, vmem_limit_bytes=VMEM_LIMIT_BYTES)


def _split(a, pieces):
    out = []
    for _ in range(pieces - 1):
        part = a.astype(BF16)
        out.append(part)
        a = a - part.astype(F32)
    return out + [a.astype(BF16)]


def _raw_dot(a, b, dims, hi):
    dot = lambda x, y: lax.dot_general(x, y, dims, preferred_element_type=F32)
    if hi:
        (a_hi, a_lo), (b_hi, b_lo) = _split(a, 2), _split(b, 2)
        return dot(a_hi, b_hi) + (dot(a_hi, b_lo) + dot(a_lo, b_hi))
    return dot(a.astype(BF16), b.astype(BF16))


def _mask_dot(mask, x, dims):
    mask = mask.astype(BF16)
    hi, mid, lo = [lax.dot_general(mask, p, dims, preferred_element_type=F32) for p in _split(x, 3)]
    return hi + (mid + lo)


@jax.custom_vjp
def _cumsum_rows(x):
    c = x.shape[0]
    tril = lax.broadcasted_iota(jnp.int32, (c, c), 0) >= lax.broadcasted_iota(jnp.int32, (c, c), 1)
    return _mask_dot(tril, x, NN)


def _cumsum_rows_bwd(_, g):
    c = g.shape[0]
    tril = lax.broadcasted_iota(jnp.int32, (c, c), 0) >= lax.broadcasted_iota(jnp.int32, (c, c), 1)
    return (_mask_dot(tril, g, TN),)


_cumsum_rows.defvjp(lambda x: (_cumsum_rows(x), None), _cumsum_rows_bwd)


def _unit_lower_inverses(xs):
    c = xs[0].shape[0]
    eye = (lax.broadcasted_iota(jnp.int32, (c, c), 0) == lax.broadcasted_iota(jnp.int32, (c, c), 1)).astype(F32)
    t_inv = [eye + x for x in xs]
    for _ in range(int(math.log2(c)) - 1):
        xs = [_raw_dot(x, x, NN, True) for x in xs]
        t_inv = [t + _raw_dot(t, x, NN, True) for t, x in zip(t_inv, xs)]
    return t_inv


@jax.custom_vjp
def _known_inverse(x_neg, t_inv):
    return t_inv


_known_inverse.defvjp(
    lambda x_neg, t_inv: (t_inv, t_inv),
    lambda t_inv, g: (_raw_dot(_raw_dot(t_inv, g, TN, False), t_inv, NT, False), jnp.zeros_like(t_inv)))


def _make_mm(hi):
    @jax.custom_vjp
    def nn(a, b):
        return _raw_dot(a, b, NN, hi)

    @jax.custom_vjp
    def nt(a, b):
        return _raw_dot(a, b, NT, hi)

    @jax.custom_vjp
    def tn(a, b):
        return _raw_dot(a, b, TN, hi)

    nn.defvjp(lambda a, b: (_raw_dot(a, b, NN, hi), (a, b)),
              lambda r, g: (_raw_dot(g, r[1], NT, False), _raw_dot(r[0], g, TN, False)))
    nt.defvjp(lambda a, b: (_raw_dot(a, b, NT, hi), (a, b)),
              lambda r, g: (_raw_dot(g, r[1], NN, False), _raw_dot(g, r[0], TN, False)))
    tn.defvjp(lambda a, b: (_raw_dot(a, b, TN, hi), (a, b)),
              lambda r, g: (_raw_dot(r[1], g, NT, False), _raw_dot(r[0], g, NN, False)))
    return nn, nt, tn


def _silu(x):
    return x * jax.nn.sigmoid(x)


def _rms_parts(x):
    r = lax.rsqrt(jnp.mean(x * x, axis=-1, keepdims=True) + EPS)
    return x * r, r


def _rms_bwd(dy, xh, r, gain):
    dxh = dy * gain
    dx = r * (dxh - xh * jnp.mean(dxh * xh, axis=-1, keepdims=True))
    return dx, jnp.sum(dy * xh, axis=0, keepdims=True)


def _ffn_specs(tm, d, tf, nj):
    return [pl.BlockSpec((tm, d), lambda i, j: (i, 0)), pl.BlockSpec((1, d), lambda i, j: (0, 0)),
            pl.BlockSpec((1, d, tf), lambda i, j: (j, 0, 0)), pl.BlockSpec((1, d, tf), lambda i, j: (nj + j, 0, 0)),
            pl.BlockSpec((tf, d), lambda i, j: (j, 0))]


def _first_step(ndim):
    return lambda: functools.reduce(lambda a, b: a & b, [pl.program_id(k) == 0 for k in range(ndim)])


def _last_step(grid):
    return lambda: functools.reduce(lambda a, b: a & b, [pl.program_id(k) == g - 1 for k, g in enumerate(grid)])


def _ffn_fwd(h, gain, w_in, wo, name, carry=None):
    tp, d = h.shape
    tf = w_in.shape[2]
    nj = w_in.shape[0] // 2
    tm = _tile(tp, 768, 8)
    row, vec, wg_spec, wu_spec, wo_spec = _ffn_specs(tm, d, tf, nj)

    def body(h_ref, g_ref, wg3_ref, wu3_ref, wo_ref, o_ref, hid3_ref, dup3_ref, dgate3_ref, n_sc, acc_sc):
        wg_ref, wu_ref = wg3_ref.at[0], wu3_ref.at[0]
        j = pl.program_id(1)

        @pl.when(j == 0)
        def _():
            xh, _ = _rms_parts(h_ref[...])
            n_sc[...] = (xh * g_ref[...]).astype(BF16)
            acc_sc[...] = jnp.zeros_like(acc_sc)

        n = n_sc[...]
        a_g = jnp.dot(n, wg_ref[...], preferred_element_type=F32)
        a_u = jnp.dot(n, wu_ref[...], preferred_element_type=F32)
        sg = jax.nn.sigmoid(a_g)
        s = a_g * sg
        hid = (s * a_u).astype(BF16)
        hid3_ref[0] = hid
        dup3_ref[0] = s.astype(BF16)
        dgate3_ref[0] = (a_u * _dsilu(a_g, sg)).astype(BF16)
        acc_sc[...] += jnp.dot(hid, wo_ref[...], preferred_element_type=F32)

        @pl.when(j == nj - 1)
        def _():
            o_ref[...] = h_ref[...] + 0.5 * acc_sc[...]

    grid = (tp // tm, nj)
    act = pl.BlockSpec((1, tm, tf), lambda i, j: (j, i, 0))
    return _carried_call(
        body, carry, _first_step(2), _last_step(grid), name=name, grid=grid,
        in_specs=[row, vec, wg_spec, wu_spec, wo_spec], out_specs=[row, act, act, act],
        out_shape=[jax.ShapeDtypeStruct((tp, d), F32)] + [jax.ShapeDtypeStruct((nj, tp, tf), BF16)] * 3,
        scratch_shapes=[pltpu.VMEM((tm, d), BF16), pltpu.VMEM((tm, d), F32)])(h, gain, w_in, w_in, wo)


def _ffn_bwd(h, dho, gain, w_in, wo, dup3, dgate3, name, carry=None):
    tp, d = h.shape
    tf = w_in.shape[2]
    nj = w_in.shape[0] // 2
    tm = _tile(tp, 704, 16)
    ni = tp // tm
    row, vec, wg_spec, wu_spec, wo_spec = _ffn_specs(tm, d, tf, nj)

    def body(h_ref, dho_ref, g_ref, wg3_ref, wu3_ref, wo_ref, dup3_ref, dgate3_ref,
             dh_ref, dgain_ref, n_ref, dag3_ref, dau3_ref, dn_sc, dhb_sc):
        wg_ref, wu_ref = wg3_ref.at[0], wu3_ref.at[0]
        dag_ref, dau_ref = dag3_ref.at[0], dau3_ref.at[0]
        i, j = pl.program_id(0), pl.program_id(1)

        @pl.when(j == 0)
        def _():
            xh, _ = _rms_parts(h_ref[...])
            n_ref[...] = (xh * g_ref[...]).astype(BF16)
            dn_sc[...] = jnp.zeros_like(dn_sc)
            dhb_sc[...] = (0.5 * dho_ref[...]).astype(BF16)

        @pl.when((i == 0) & (j == 0))
        def _():
            dgain_ref[...] = jnp.zeros_like(dgain_ref)

        d_hid = lax.dot_general(dhb_sc[...], wo_ref[...], NT, preferred_element_type=F32)
        d_au = (d_hid * dup3_ref[0].astype(F32)).astype(BF16)
        d_ag = (d_hid * dgate3_ref[0].astype(F32)).astype(BF16)
        dau_ref[...] = d_au
        dag_ref[...] = d_ag
        dn_sc[...] += (lax.dot_general(d_ag, wg_ref[...], NT, preferred_element_type=F32)
                       + lax.dot_general(d_au, wu_ref[...], NT, preferred_element_type=F32))

        @pl.when(j == nj - 1)
        def _():
            xh, r = _rms_parts(h_ref[...])
            dx, dg = _rms_bwd(dn_sc[...], xh, r, g_ref[...])
            dh_ref[...] = dho_ref[...] + dx
            dgain_ref[...] += dg

    act = pl.BlockSpec((1, tm, tf), lambda i, j: (j, i, 0))
    return _carried_call(
        body, carry, _first_step(2), _last_step((ni, nj)), name=name, grid=(ni, nj),
        in_specs=[row, row, vec, wg_spec, wu_spec, wo_spec, act, act],
        out_specs=[row, vec, row, act, act],
        out_shape=[jax.ShapeDtypeStruct((tp, d), F32), jax.ShapeDtypeStruct((1, d), F32),
                   jax.ShapeDtypeStruct((tp, d), BF16)] + [jax.ShapeDtypeStruct((nj, tp, tf), BF16)] * 2,
        scratch_shapes=[pltpu.VMEM((tm, d), F32), pltpu.VMEM((tm, d), BF16)])(
            h, dho, gain, w_in, w_in, wo, dup3, dgate3)


def _matmul_tn(a, b, name, scale=1.0):
    t, m = a.shape
    n = b.shape[1]
    bm = _tile(m, 1024, LANES)
    bn = _tile(n, 1536, LANES)
    tk = _tile(t, 1408, 16)
    nk = t // tk

    def body(a_ref, b_ref, o_ref):
        k = pl.program_id(2)

        @pl.when(k == 0)
        def _():
            o_ref[...] = jnp.zeros_like(o_ref)

        o_ref[...] += lax.dot_general(a_ref[...].astype(BF16), b_ref[...].astype(BF16), TN,
                                      preferred_element_type=F32)

        if scale != 1.0:
            @pl.when(k == nk - 1)
            def _():
                o_ref[...] = o_ref[...] * scale

    return pl.pallas_call(
        body, name=name, grid=(m // bm, n // bn, nk),
        in_specs=[pl.BlockSpec((tk, bm), lambda i, j, k: (k, i)), pl.BlockSpec((tk, bn), lambda i, j, k: (k, j))],
        out_specs=pl.BlockSpec((bm, bn), lambda i, j, k: (i, j)),
        out_shape=jax.ShapeDtypeStruct((m, n), F32),
        compiler_params=_params("parallel", "parallel", "arbitrary"))(a, b)


def _matmul_tn_blocks(a, b, name, scale=1.0, carry=None):
    a_blocked = a.ndim == 3
    nb, t = (a.shape[0], a.shape[1]) if a_blocked else (b.shape[0], b.shape[1])
    m, n = a.shape[-1], b.shape[-1]
    tk = _tile(t, 1408, 16)
    nk = t // tk
    if a_blocked:
        bo = _tile(n, 1024, LANES)
        a_spec = pl.BlockSpec((1, tk, m), lambda p, o, k: (p, k, 0))
        b_spec = pl.BlockSpec((tk, bo), lambda p, o, k: (k, o))
        o_spec = pl.BlockSpec((m, bo), lambda p, o, k: (p, o))
        out_shape = jax.ShapeDtypeStruct((nb * m, n), F32)
        grid = (nb, n // bo, nk)
    else:
        bo = _tile(m, 1024, LANES)
        a_spec = pl.BlockSpec((tk, bo), lambda p, o, k: (k, o))
        b_spec = pl.BlockSpec((1, tk, n), lambda p, o, k: (p, k, 0))
        o_spec = pl.BlockSpec((1, bo, n), lambda p, o, k: (p, o, 0))
        out_shape = jax.ShapeDtypeStruct((nb, m, n), F32)
        grid = (nb, m // bo, nk)

    def body(a_ref, b_ref, o_ref):
        k = pl.program_id(2)
        a_blk = a_ref[0] if a_blocked else a_ref[...]
        b_blk = b_ref[...] if a_blocked else b_ref[0]
        part = lax.dot_general(a_blk.astype(BF16), b_blk.astype(BF16), TN, preferred_element_type=F32)
        out = o_ref if a_blocked else o_ref.at[0]

        @pl.when(k == 0)
        def _():
            out[...] = part

        @pl.when(k > 0)
        def _():
            out[...] += part

        if scale != 1.0:
            @pl.when(k == nk - 1)
            def _():
                out[...] = out[...] * scale

    (out,), moved = _carried_call(body, carry, _first_step(3), _last_step(grid), name=name, grid=grid,
                                  in_specs=[a_spec, b_spec], out_specs=[o_spec], out_shape=[out_shape])(a, b)
    return out if carry is None else (out, moved)


def _matmul_nt_parts(parts, w, acc, name, carry=None):
    t = parts[0].shape[0]
    d = w.shape[0]
    widths = [p.shape[1] for p in parts]
    tk = _tile(math.gcd(*widths), 1024, LANES)
    counts = [wd // tk for wd in widths]
    starts = [sum(counts[:g]) for g in range(len(parts))]
    nk = sum(counts)
    tm = _tile(t, 768, 8)
    n_parts = len(parts)

    def body(*refs):
        a_refs, w_ref, o_ref = refs[:n_parts], refs[n_parts], refs[-1]
        k = pl.program_id(1)

        @pl.when(k == 0)
        def _():
            o_ref[...] = jnp.zeros_like(o_ref) if acc is None else refs[n_parts + 1][...]

        for g in range(n_parts):
            @pl.when((k >= starts[g]) & (k < starts[g] + counts[g]))
            def _(g=g):
                o_ref[...] += lax.dot_general(a_refs[g][...].astype(BF16), w_ref[...], NT,
                                              preferred_element_type=F32)

    in_specs = [pl.BlockSpec((tm, tk), lambda i, k, lo=starts[g], nb=counts[g]: (i, jnp.clip(k - lo, 0, nb - 1)))
                for g in range(n_parts)]
    in_specs.append(pl.BlockSpec((d, tk), lambda i, k: (0, k)))
    args = list(parts) + [w]
    if acc is not None:
        in_specs.append(pl.BlockSpec((tm, d), lambda i, k: (i, 0)))
        args.append(acc)
    grid = (t // tm, nk)
    (out,), moved = _carried_call(
        body, carry, _first_step(2), _last_step(grid), name=name, grid=grid, in_specs=in_specs,
        out_specs=[pl.BlockSpec((tm, d), lambda i, k: (i, 0))],
        out_shape=[jax.ShapeDtypeStruct((t, d), F32)])(*args)
    return out, moved


def _proj_fwd(h, gain, wp, name, carry=None):
    tp, d = h.shape
    npad = wp.shape[1]
    tm = _tile(tp, 768, 8)
    tn = _tile(npad, 3456, LANES)

    def body(h_ref, g_ref, w_ref, o_ref, n_ref):
        @pl.when(pl.program_id(1) == 0)
        def _():
            xh, _ = _rms_parts(h_ref[...])
            n_ref[...] = (xh * g_ref[...]).astype(BF16)

        o_ref[...] = jnp.dot(n_ref[...], w_ref[...], preferred_element_type=F32)

    grid = (tp // tm, npad // tn)
    return _carried_call(
        body, carry, _first_step(2), _last_step(grid), name=name, grid=grid,
        in_specs=[pl.BlockSpec((tm, d), lambda i, j: (i, 0)), pl.BlockSpec((1, d), lambda i, j: (0, 0)),
                  pl.BlockSpec((d, tn), lambda i, j: (0, j))],
        out_specs=[pl.BlockSpec((tm, tn), lambda i, j: (i, j)), pl.BlockSpec((tm, d), lambda i, j: (i, 0))],
        out_shape=[jax.ShapeDtypeStruct((tp, npad), F32), jax.ShapeDtypeStruct((tp, d), BF16)])(h, gain, wp)


def _norm_bwd(h, gain, dn, dres, name):
    tp, d = h.shape
    tm = _tile(tp, 256, 8)

    def body(h_ref, g_ref, dn_ref, dres_ref, dh_ref, dgain_ref):
        @pl.when(pl.program_id(0) == 0)
        def _():
            dgain_ref[...] = jnp.zeros_like(dgain_ref)

        xh, r = _rms_parts(h_ref[...])
        dx, dg = _rms_bwd(dn_ref[...], xh, r, g_ref[...])
        dh_ref[...] = dres_ref[...] + dx
        dgain_ref[...] += dg

    row = pl.BlockSpec((tm, d), lambda i: (i, 0))
    vec = pl.BlockSpec((1, d), lambda i: (0, 0))
    return pl.pallas_call(
        body, name=name, grid=(tp // tm,), in_specs=[row, vec, row, row], out_specs=[row, vec],
        out_shape=[jax.ShapeDtypeStruct((tp, d), F32), jax.ShapeDtypeStruct((1, d), F32)],
        compiler_params=_params("arbitrary"))(h, gain, dn, dres)


def _head_post(a, grp):
    a = _silu(a)
    r = lax.rsqrt(jnp.sum(a * a, axis=-1, keepdims=True) + EPS)
    if isinstance(grp, int):
        return a if grp == 2 else a * r * (HEAD_DIM ** -0.5 if grp == 0 else 1.0)
    scale = jnp.where(grp == 0, HEAD_DIM ** -0.5, 1.0).astype(F32)
    return jnp.where(grp == 2, a, a * r * scale)


def _head_post_bwd(c, dy, grp):
    sg = jax.nn.sigmoid(c)
    a = c * sg
    dsilu = sg * (1.0 + c * (1.0 - sg))
    if grp == 2:
        return dy * dsilu
    r = lax.rsqrt(jnp.sum(a * a, axis=-1, keepdims=True) + EPS)
    scale = HEAD_DIM ** -0.5 if grp == 0 else 1.0
    da = (scale * r) * (dy - a * (r * r * jnp.sum(dy * a, axis=-1, keepdims=True)))
    return da * dsilu


def _conv_taps(ext_sc, w_ref, tm):
    c = None
    for i in range(CONV_K):
        s = CONV_K - 1 - i
        term = w_ref[i:i + 1, :] * ext_sc[8 - s:8 - s + tm, :]
        c = term if c is None else c + term
    return c


def _conv_fwd(proj, conv_w, hv, name):
    tp = proj.shape[0]
    tm = _tile(tp, 768, 8)
    nh = hv // HEAD_DIM

    def body(x_ref, halo_ref, w_ref, o_ref, c_ref, ext_sc):
        i, grp = pl.program_id(0), pl.program_id(1)
        ext_sc[0:8, :] = jnp.where(i == 0, 0.0, halo_ref[...])
        ext_sc[8:, :] = x_ref[...]
        c_ref[...] = _conv_taps(ext_sc, w_ref, tm)
        for h in range(nh):
            sl = slice(h * HEAD_DIM, (h + 1) * HEAD_DIM)
            o_ref[:, sl] = _head_post(c_ref[:, sl], grp)

    blk = pl.BlockSpec((tm, hv), lambda i, g: (i, g))
    return pl.pallas_call(
        body, name=name, grid=(tp // tm, 3),
        in_specs=[blk, pl.BlockSpec((8, hv), lambda i, g: (jnp.maximum(i * (tm // 8) - 1, 0), g)),
                  pl.BlockSpec((CONV_K, hv), lambda i, g: (0, g))],
        out_specs=[blk, blk],
        out_shape=[jax.ShapeDtypeStruct((tp, 3 * hv), F32)] * 2,
        scratch_shapes=[pltpu.VMEM((tm + 8, hv), F32)],
        compiler_params=_params("parallel", "arbitrary"))(proj, proj, conv_w)


def _conv_bwd(proj, conv_out, conv_w, dy, grp, hv, name):
    tp = proj.shape[0]
    tm = _tile(tp, 768, 8)
    ni = tp // tm
    nh = hv // HEAD_DIM

    def body(x_ref, c_ref, w_ref, dy_ref, dx_ref, dw_ref, dc_sc):
        step = pl.program_id(0)

        @pl.when(step == 0)
        def _():
            dc_sc[tm:, :] = jnp.zeros((8, hv), F32)
            dw_ref[...] = jnp.zeros_like(dw_ref)

        @pl.when(step > 0)
        def _():
            dc_sc[tm:, :] = dc_sc[0:8, :]

        for h in range(nh):
            sl = slice(h * HEAD_DIM, (h + 1) * HEAD_DIM)
            dc_sc[0:tm, sl] = _head_post_bwd(c_ref[:, sl], dy_ref[:, sl], grp)

        x = x_ref[...]
        dx = None
        for k in range(CONV_K):
            s = CONV_K - 1 - k
            shifted = dc_sc[s:s + tm, :]
            dw_ref[k:k + 1, :] += jnp.sum(shifted * x, axis=0, keepdims=True)
            term = w_ref[k:k + 1, :] * shifted
            dx = term if dx is None else dx + term
        dx_ref[...] = dx.astype(BF16)

    tile = lambda step: ni - 1 - step
    grp_blk = pl.BlockSpec((tm, hv), lambda s: (tile(s), grp))
    own_blk = pl.BlockSpec((tm, hv), lambda s: (tile(s), 0))
    return pl.pallas_call(
        body, name=name, grid=(ni,),
        in_specs=[grp_blk, grp_blk, pl.BlockSpec((CONV_K, hv), lambda s: (0, grp)), own_blk],
        out_specs=[own_blk, pl.BlockSpec((CONV_K, hv), lambda s: (0, 0))],
        out_shape=[jax.ShapeDtypeStruct((tp, hv), BF16), jax.ShapeDtypeStruct((CONV_K, hv), F32)],
        scratch_shapes=[pltpu.VMEM((tm + 8, hv), F32)],
        compiler_params=_params("arbitrary"))(proj, conv_out, conv_w, dy)


def _gdn_gates(ba, alog, dtb):
    x = ba + dtb
    softplus = jnp.maximum(x, 0.0) + jnp.log1p(jnp.exp(-jnp.abs(x)))
    return _cumsum_rows(-jnp.exp(alog) * softplus), jax.nn.sigmoid(ba)


def _gdn_chunks(states, qs, ks, vs, gates, known_inverses=None):
    mm_nn, mm_nt, mm_tn = _make_mm(False)
    hi_nn, _, _ = _make_mm(True)
    nh = len(states)
    items = range(len(qs))
    head = [i % nh for i in items]
    c = qs[0].shape[0]
    lane = lax.broadcasted_iota(jnp.int32, (c, LANES), 1)
    last_row = (lax.broadcasted_iota(jnp.int32, (c, 1), 0) == c - 1).astype(F32)
    ri = lax.broadcasted_iota(jnp.int32, (c, c), 0)
    ci = lax.broadcasted_iota(jnp.int32, (c, c), 1)
    causal = ri >= ci
    strict = ri > ci
    eye = (ri == ci).astype(F32)
    sel_a = [(lane == nh + h).astype(F32) for h in range(nh)]
    sel_b = [(lane == h).astype(F32) for h in range(nh)]

    gcol = [jnp.sum(gates[i // nh][0] * sel_a[head[i]], axis=1, keepdims=True) for i in items]
    grow = [jnp.sum(eye * gcol[i], axis=0, keepdims=True) for i in items]
    beta = [jnp.sum(gates[i // nh][1] * sel_b[head[i]], axis=1, keepdims=True) for i in items]
    decay = [jnp.where(causal, jnp.exp(jnp.where(causal, gcol[i] - grow[i], 0.0)), 0.0) for i in items]
    kb = [ks[i] * beta[i] for i in items]
    kk = [mm_nt(kb[i], ks[i]) for i in items]
    qk = [mm_nt(qs[i], ks[i]) for i in items]
    x_neg = [-jnp.where(strict, kk[i] * decay[i], 0.0) for i in items]
    if known_inverses is None:
        t_inv = _unit_lower_inverses(x_neg)
    else:
        t_inv = [_known_inverse(x_neg[i], known_inverses[i]) for i in items]
    eg = [jnp.exp(gcol[i]) for i in items]
    u = [hi_nn(t_inv[i], vs[i] * beta[i]) for i in items]
    w = [hi_nn(t_inv[i], kb[i] * eg[i]) for i in items]
    qk = [qk[i] * decay[i] for i in items]
    glast = [jnp.sum(gcol[i] * last_row, axis=0, keepdims=True) for i in items]
    q_dec = [qs[i] * eg[i] for i in items]
    k_dec = [ks[i] * jnp.exp(glast[i] - gcol[i]) for i in items]
    s_dec = [jnp.exp(glast[i]) for i in items]

    outs = []
    for first in range(0, len(qs), nh):
        chunk = range(first, first + nh)
        ws = [mm_nn(w[i], states[i - first]) for i in chunk]
        from_state = [mm_nn(q_dec[i], states[i - first]) for i in chunk]
        v_new = [u[i] - ws[i - first] for i in chunk]
        intra = [mm_nn(qk[i], v_new[i - first]) for i in chunk]
        kv = [mm_tn(k_dec[i], v_new[i - first]) for i in chunk]
        outs += [from_state[i - first] + intra[i - first] for i in chunk]
        states = [states[i - first] * s_dec[i] + kv[i - first] for i in chunk]
    return outs, states, t_inv


SCAN_CHUNKS = 4


def _scan_specs(nh, steps, rev, first_col):
    sidx = (lambda s: steps - 1 - s) if rev else (lambda s: s)
    hv = nh * HEAD_DIM
    rows = SCAN_CHUNKS * CHUNK
    cols = [pl.BlockSpec((rows, hv), lambda s, g=g: (sidx(s), first_col + g)) for g in range(3)]
    st = pl.BlockSpec((1, nh, HEAD_DIM, HEAD_DIM), lambda s: (sidx(s), 0, 0, 0))
    act = pl.BlockSpec((rows, hv), lambda s: (sidx(s), 0))
    return cols, st, act


def _chunk_heads(ref, nh):
    return [ref[j * CHUNK:(j + 1) * CHUNK, h * HEAD_DIM:(h + 1) * HEAD_DIM] for j in range(SCAN_CHUNKS)
            for h in range(nh)]


def _store_chunk_heads(ref, values, nh, dtype=None):
    for i, val in enumerate(values):
        j, h = divmod(i, nh)
        ref[j * CHUNK:(j + 1) * CHUNK, h * HEAD_DIM:(h + 1) * HEAD_DIM] = val if dtype is None else val.astype(dtype)


def _gdn_fwd(qkv, proj, alog, dtb, nh, name):
    tp = qkv.shape[0]
    steps = tp // (SCAN_CHUNKS * CHUNK)
    rows = SCAN_CHUNKS * CHUNK

    def body(q_ref, k_ref, v_ref, ba_ref, al_ref, dt_ref, o_ref, st_ref, inv_ref, s_sc):
        @pl.when(pl.program_id(0) == 0)
        def _():
            s_sc[...] = jnp.zeros_like(s_sc)

        gates = [_gdn_gates(ba_ref[j * CHUNK:(j + 1) * CHUNK, :], al_ref[...], dt_ref[...])
                 for j in range(SCAN_CHUNKS)]
        states = [s_sc[h] for h in range(nh)]
        for h in range(nh):
            st_ref[0, h] = states[h]
        outs, new_states, t_inv = _gdn_chunks(states, _chunk_heads(q_ref, nh), _chunk_heads(k_ref, nh),
                                              _chunk_heads(v_ref, nh), gates)
        _store_chunk_heads(o_ref, outs, nh)
        for h in range(nh):
            s_sc[h] = new_states[h]
        for i, t in enumerate(t_inv):
            inv_ref[0, i] = t

    cols, st, act = _scan_specs(nh, steps, False, 0)
    ba = pl.BlockSpec((rows, LANES), lambda s: (s, 10 * nh * HEAD_DIM // LANES))
    vec = pl.BlockSpec((1, LANES), lambda s: (0, 0))
    inv = pl.BlockSpec((1, SCAN_CHUNKS * nh, CHUNK, CHUNK), lambda s: (s, 0, 0, 0))
    return pl.pallas_call(
        body, name=name, grid=(steps,), in_specs=cols + [ba, vec, vec], out_specs=[act, st, inv],
        out_shape=[jax.ShapeDtypeStruct((tp, nh * HEAD_DIM), F32),
                   jax.ShapeDtypeStruct((steps, nh, HEAD_DIM, HEAD_DIM), F32),
                   jax.ShapeDtypeStruct((steps, SCAN_CHUNKS * nh, CHUNK, CHUNK), F32)],
        scratch_shapes=[pltpu.VMEM((nh, HEAD_DIM, HEAD_DIM), F32)],
        compiler_params=_params("arbitrary"))(qkv, qkv, qkv, proj, alog, dtb)


def _gdn_bwd(qkv, proj, alog, dtb, states, inverses, do, nh, name):
    tp = qkv.shape[0]
    steps = tp // (SCAN_CHUNKS * CHUNK)
    rows = SCAN_CHUNKS * CHUNK

    def body(q_ref, k_ref, v_ref, ba_ref, al_ref, dt_ref, st_ref, inv_ref, do_ref,
             dq_ref, dk_ref, dv_ref, dba_ref, dal_ref, ddt_ref, ds_sc):
        @pl.when(pl.program_id(0) == 0)
        def _():
            ds_sc[...] = jnp.zeros_like(ds_sc)
            dal_ref[...] = jnp.zeros_like(dal_ref)
            ddt_ref[...] = jnp.zeros_like(ddt_ref)

        gates, gates_vjps = [], []
        for j in range(SCAN_CHUNKS):
            g, g_vjp = jax.vjp(_gdn_gates, ba_ref[j * CHUNK:(j + 1) * CHUNK, :], al_ref[...], dt_ref[...])
            gates.append(g)
            gates_vjps.append(g_vjp)
        known = [inv_ref[0, i] for i in range(SCAN_CHUNKS * nh)]
        fn = lambda s, q, k, v, g: _gdn_chunks(s, q, k, v, g, known)[:2]
        _, vjp = jax.vjp(fn, [st_ref[0, h] for h in range(nh)], _chunk_heads(q_ref, nh), _chunk_heads(k_ref, nh),
                         _chunk_heads(v_ref, nh), gates)
        ds, dq, dk, dv, dgates = vjp((_chunk_heads(do_ref, nh), [ds_sc[h] for h in range(nh)]))
        for h in range(nh):
            ds_sc[h] = ds[h]
        _store_chunk_heads(dq_ref, dq, nh)
        _store_chunk_heads(dk_ref, dk, nh)
        _store_chunk_heads(dv_ref, dv, nh)
        for j in range(SCAN_CHUNKS):
            dba, dal, ddt = gates_vjps[j](dgates[j])
            dba_ref[j * CHUNK:(j + 1) * CHUNK, :] = dba
            dal_ref[...] += dal
            ddt_ref[...] += ddt

    cols, st, act = _scan_specs(nh, steps, True, 0)
    ba = pl.BlockSpec((rows, LANES), lambda s: (steps - 1 - s, 10 * nh * HEAD_DIM // LANES))
    vec = pl.BlockSpec((1, LANES), lambda s: (0, 0))
    inv = pl.BlockSpec((1, SCAN_CHUNKS * nh, CHUNK, CHUNK), lambda s: (steps - 1 - s, 0, 0, 0))
    return pl.pallas_call(
        body, name=name, grid=(steps,), in_specs=cols + [ba, vec, vec, st, inv, act],
        out_specs=[act, act, act, pl.BlockSpec((rows, LANES), lambda s: (steps - 1 - s, 0)), vec, vec],
        out_shape=[jax.ShapeDtypeStruct((tp, nh * HEAD_DIM), F32)] * 3
                  + [jax.ShapeDtypeStruct((tp, LANES), F32), jax.ShapeDtypeStruct((1, LANES), F32),
                     jax.ShapeDtypeStruct((1, LANES), F32)],
        scratch_shapes=[pltpu.VMEM((nh, HEAD_DIM, HEAD_DIM), F32)],
        compiler_params=_params("arbitrary"))(qkv, qkv, qkv, proj, alog, dtb, states, inverses, do)


def _swap_pairs(t):
    lane = lax.broadcasted_iota(jnp.int32, t.shape, 1)
    n = t.shape[1]
    return jnp.where(lane % 2 == 0, pltpu.roll(t, n - 1, 1), pltpu.roll(t, 1, 1))


def _rot(t, cos, sin_signed):
    return t * cos + _swap_pairs(t) * sin_signed


def _rot_t(dt, cos, sin_signed):
    return dt * cos + _swap_pairs(dt * sin_signed)


def _ret_chunks(states, qs, ks, vs, dec, xi, zeta, cd):
    mm_nn, mm_nt, mm_tn = _make_mm(False)
    nh = len(states)
    items = range(len(qs))
    scores = [mm_nt(qs[i], ks[i]) for i in items]
    kv = [mm_tn(ks[i] * zeta[i % nh], vs[i]) for i in items]
    intra = [mm_nn(scores[i] * dec[i % nh], vs[i]) for i in items]
    q_dec = [qs[i] * xi[i % nh] for i in items]
    outs = []
    for first in range(0, len(qs), nh):
        outs += [intra[first + h] + mm_nn(q_dec[first + h], states[h]) for h in range(nh)]
        states = [states[h] * cd[h] + kv[first + h] for h in range(nh)]
    return outs, states


def _ret_table_specs(nh, steps, rev):
    sidx = (lambda s: steps - 1 - s) if rev else (lambda s: s)
    rope = pl.BlockSpec((SCAN_CHUNKS * CHUNK, HEAD_DIM), lambda s: (sidx(s), 0))
    dec = pl.BlockSpec((nh, CHUNK, CHUNK), lambda s: (0, 0, 0))
    tab = pl.BlockSpec((nh, CHUNK, HEAD_DIM), lambda s: (0, 0, 0))
    cd = pl.BlockSpec((nh, 8, HEAD_DIM), lambda s: (0, 0, 0))
    return [rope, rope, dec, tab, tab, cd]


def _rotated(ref, cos_ref, sin_ref, nh, scale=1.0):
    out = []
    for j in range(SCAN_CHUNKS):
        rows = slice(j * CHUNK, (j + 1) * CHUNK)
        cos_t, sin_t = cos_ref[rows, :], sin_ref[rows, :]
        for h in range(nh):
            t = _rot(ref[rows, h * HEAD_DIM:(h + 1) * HEAD_DIM], cos_t, sin_t)
            out.append(t if scale == 1.0 else t * scale)
    return out


def _ret_fwd(proj, cos, sin, dec, xi, zeta, cd, nh, name):
    tp = proj.shape[0]
    steps = tp // (SCAN_CHUNKS * CHUNK)
    kscale = HEAD_DIM ** -0.5

    def body(q_ref, k_ref, v_ref, cos_ref, sin_ref, dec_ref, xi_ref, zeta_ref, cd_ref, o_ref, st_ref, s_sc):
        @pl.when(pl.program_id(0) == 0)
        def _():
            s_sc[...] = jnp.zeros_like(s_sc)

        heads = range(nh)
        states = [s_sc[h] for h in heads]
        for h in heads:
            st_ref[0, h] = states[h]
        outs, new_states = _ret_chunks(
            states, _rotated(q_ref, cos_ref, sin_ref, nh), _rotated(k_ref, cos_ref, sin_ref, nh, kscale),
            _chunk_heads(v_ref, nh), [dec_ref[h] for h in heads], [xi_ref[h] for h in heads],
            [zeta_ref[h] for h in heads], [cd_ref[h][0:1, :] for h in heads])
        _store_chunk_heads(o_ref, outs, nh)
        for h in heads:
            s_sc[h] = new_states[h]

    cols, st, act = _scan_specs(nh, steps, False, 3)
    return pl.pallas_call(
        body, name=name, grid=(steps,), in_specs=cols + _ret_table_specs(nh, steps, False), out_specs=[act, st],
        out_shape=[jax.ShapeDtypeStruct((tp, nh * HEAD_DIM), F32),
                   jax.ShapeDtypeStruct((steps, nh, HEAD_DIM, HEAD_DIM), F32)],
        scratch_shapes=[pltpu.VMEM((nh, HEAD_DIM, HEAD_DIM), F32)],
        compiler_params=_params("arbitrary"))(proj, proj, proj, cos, sin, dec, xi, zeta, cd)


def _ret_bwd(proj, cos, sin, dec, xi, zeta, cd, states, do, nh, name):
    tp = proj.shape[0]
    steps = tp // (SCAN_CHUNKS * CHUNK)
    kscale = HEAD_DIM ** -0.5

    def body(q_ref, k_ref, v_ref, cos_ref, sin_ref, dec_ref, xi_ref, zeta_ref, cd_ref, st_ref, do_ref,
             dq_ref, dk_ref, dv_ref, ds_sc):
        @pl.when(pl.program_id(0) == 0)
        def _():
            ds_sc[...] = jnp.zeros_like(ds_sc)

        heads = range(nh)
        fn = functools.partial(_ret_chunks, dec=[dec_ref[h] for h in heads], xi=[xi_ref[h] for h in heads],
                               zeta=[zeta_ref[h] for h in heads], cd=[cd_ref[h][0:1, :] for h in heads])
        _, vjp = jax.vjp(fn, [st_ref[0, h] for h in heads], _rotated(q_ref, cos_ref, sin_ref, nh),
                         _rotated(k_ref, cos_ref, sin_ref, nh, kscale), _chunk_heads(v_ref, nh))
        ds, dq, dk, dv = vjp((_chunk_heads(do_ref, nh), [ds_sc[h] for h in heads]))
        for h in heads:
            ds_sc[h] = ds[h]
        for i in range(SCAN_CHUNKS * nh):
            rows = slice((i // nh) * CHUNK, (i // nh + 1) * CHUNK)
            cos_t, sin_t = cos_ref[rows, :], sin_ref[rows, :]
            dq[i] = _rot_t(dq[i], cos_t, sin_t)
            dk[i] = _rot_t(dk[i] * kscale, cos_t, sin_t)
        _store_chunk_heads(dq_ref, dq, nh, BF16)
        _store_chunk_heads(dk_ref, dk, nh, BF16)
        _store_chunk_heads(dv_ref, dv, nh, BF16)

    cols, st, act = _scan_specs(nh, steps, True, 3)
    return pl.pallas_call(
        body, name=name, grid=(steps,), in_specs=cols + _ret_table_specs(nh, steps, True) + [st, act],
        out_specs=[act, act, act],
        out_shape=[jax.ShapeDtypeStruct((tp, nh * HEAD_DIM), BF16)] * 3,
        scratch_shapes=[pltpu.VMEM((nh, HEAD_DIM, HEAD_DIM), F32)],
        compiler_params=_params("arbitrary"))(proj, proj, proj, cos, sin, dec, xi, zeta, cd, states, do)


def _gdn_out(o, z, gnorm):
    return o * lax.rsqrt(jnp.mean(o * o, axis=-1, keepdims=True) + EPS) * gnorm * _silu(z)


def _ret_out(o, rg, rnorm):
    mu = jnp.mean(o, axis=-1, keepdims=True)
    var = jnp.mean(jnp.square(o - mu), axis=-1, keepdims=True)
    return _silu(rg) * ((o - mu) * lax.rsqrt(var + EPS) * rnorm)


def _dsilu(x, sg):
    return sg * (1.0 + x * (1.0 - sg))


def _gdn_out_bwd(o, z, gnorm, dy):
    r = lax.rsqrt(jnp.mean(o * o, axis=-1, keepdims=True) + EPS)
    xh = o * r
    sg = jax.nn.sigmoid(z)
    sz = z * sg
    t = dy * (gnorm * sz)
    do = r * (t - xh * jnp.mean(t * xh, axis=-1, keepdims=True))
    e = dy * xh
    return do, e * (gnorm * _dsilu(z, sg)), jnp.sum(e * sz, axis=0, keepdims=True)


def _ret_out_bwd(o, rg, rnorm, dy):
    oc = o - jnp.mean(o, axis=-1, keepdims=True)
    rs = lax.rsqrt(jnp.mean(oc * oc, axis=-1, keepdims=True) + EPS)
    xh = oc * rs
    sg = jax.nn.sigmoid(rg)
    srg = rg * sg
    t = dy * (rnorm * srg)
    do = rs * (t - jnp.mean(t, axis=-1, keepdims=True) - xh * jnp.mean(t * xh, axis=-1, keepdims=True))
    e = dy * xh
    return do, e * (rnorm * _dsilu(rg, sg)), jnp.sum(e * srg, axis=0, keepdims=True)


def _post_specs(tm, hv, d):
    row = lambda col: pl.BlockSpec((tm, hv), lambda i: (i, col))
    return dict(
        oa=row(0), ob=row(0), z=row(6), rg=row(7), ga=row(8), gb=row(9),
        gnorm=pl.BlockSpec((1, HEAD_DIM), lambda i: (0, 0)), rnorm=pl.BlockSpec((1, hv), lambda i: (0, 0)),
        w=pl.BlockSpec((hv, d), lambda i: (0, 0)), res=pl.BlockSpec((tm, d), lambda i: (i, 0)))


def _post_fwd(oa, ob, proj, gnorm, rnorm, wbg, wbr, wo, h1, name):
    tp, d = h1.shape
    hv = oa.shape[1]
    nh = hv // HEAD_DIM
    tm = _tile(tp, 256, 8)

    def body(oa_ref, ob_ref, z_ref, rg_ref, ga_ref, gb_ref, gn_ref, rn_ref, wbg_ref, wbr_ref, wo_ref, h_ref,
             o_ref, ya_sc, yb_sc):
        for h in range(nh):
            sl = slice(h * HEAD_DIM, (h + 1) * HEAD_DIM)
            ya_sc[:, sl] = _gdn_out(oa_ref[:, sl], z_ref[:, sl], gn_ref[...]).astype(BF16)
            yb_sc[:, sl] = _ret_out(ob_ref[:, sl], rg_ref[:, sl], rn_ref[:, sl]).astype(BF16)
        pa = jnp.dot(ya_sc[...], wbg_ref[...], preferred_element_type=F32)
        pb = jnp.dot(yb_sc[...], wbr_ref[...], preferred_element_type=F32)
        merged = jax.nn.sigmoid(ga_ref[...]) * pa + jax.nn.sigmoid(gb_ref[...]) * pb
        o_ref[...] = h_ref[...] + jnp.dot(merged.astype(BF16), wo_ref[...], preferred_element_type=F32)

    sp = _post_specs(tm, hv, d)
    return pl.pallas_call(
        body, name=name, grid=(tp // tm,),
        in_specs=[sp["oa"], sp["ob"], sp["z"], sp["rg"], sp["ga"], sp["gb"], sp["gnorm"], sp["rnorm"],
                  sp["w"], sp["w"], sp["w"], sp["res"]],
        out_specs=sp["res"], out_shape=jax.ShapeDtypeStruct((tp, d), F32),
        scratch_shapes=[pltpu.VMEM((tm, hv), BF16), pltpu.VMEM((tm, hv), BF16)],
        compiler_params=_params("parallel"))(oa, ob, proj, proj, proj, proj, gnorm, rnorm, wbg, wbr, wo, h1)


def _post_bwd(oa, ob, proj, gnorm, rnorm, wbg, wbr, wo, dh2, name):
    tp, d = dh2.shape
    hv = oa.shape[1]
    nh = hv // HEAD_DIM
    tm = _tile(tp, 256, 8)

    def body(oa_ref, ob_ref, z_ref, rg_ref, ga_ref, gb_ref, gn_ref, rn_ref, wbg_ref, wbr_ref, wo_ref, dh_ref,
             doa_ref, dob_ref, dg_ref, ya_ref, yb_ref, mg_ref, dpa_ref, dpb_ref, dgn_ref, drn_ref,
             dya_sc, dyb_sc):
        @pl.when(pl.program_id(0) == 0)
        def _():
            dgn_ref[...] = jnp.zeros_like(dgn_ref)
            drn_ref[...] = jnp.zeros_like(drn_ref)

        for h in range(nh):
            sl = slice(h * HEAD_DIM, (h + 1) * HEAD_DIM)
            ya_ref[:, sl] = _gdn_out(oa_ref[:, sl], z_ref[:, sl], gn_ref[...]).astype(BF16)
            yb_ref[:, sl] = _ret_out(ob_ref[:, sl], rg_ref[:, sl], rn_ref[:, sl]).astype(BF16)
        pa = jnp.dot(ya_ref[...], wbg_ref[...], preferred_element_type=F32)
        pb = jnp.dot(yb_ref[...], wbr_ref[...], preferred_element_type=F32)
        sa = jax.nn.sigmoid(ga_ref[...])
        sb = jax.nn.sigmoid(gb_ref[...])
        mg_ref[...] = (sa * pa + sb * pb).astype(BF16)
        dm = lax.dot_general(dh_ref[...].astype(BF16), wo_ref[...], NT, preferred_element_type=F32)
        dpa = (dm * sa).astype(BF16)
        dpb = (dm * sb).astype(BF16)
        dpa_ref[...] = dpa
        dpb_ref[...] = dpb
        dg_ref[:, 2 * hv:3 * hv] = (dm * pa * sa * (1.0 - sa)).astype(BF16)
        dg_ref[:, 3 * hv:4 * hv] = (dm * pb * sb * (1.0 - sb)).astype(BF16)
        dya_sc[...] = lax.dot_general(dpa, wbg_ref[...], NT, preferred_element_type=F32)
        dyb_sc[...] = lax.dot_general(dpb, wbr_ref[...], NT, preferred_element_type=F32)
        for h in range(nh):
            sl = slice(h * HEAD_DIM, (h + 1) * HEAD_DIM)
            doa, dz, dgn = _gdn_out_bwd(oa_ref[:, sl], z_ref[:, sl], gn_ref[...], dya_sc[:, sl])
            doa_ref[:, sl] = doa
            dg_ref[:, sl] = dz.astype(BF16)
            dgn_ref[...] += dgn
            dob, drg, drn = _ret_out_bwd(ob_ref[:, sl], rg_ref[:, sl], rn_ref[:, sl], dyb_sc[:, sl])
            dob_ref[:, sl] = dob
            dg_ref[:, hv + h * HEAD_DIM:hv + (h + 1) * HEAD_DIM] = drg.astype(BF16)
            drn_ref[:, sl] += drn

    sp = _post_specs(tm, hv, d)
    act = pl.BlockSpec((tm, hv), lambda i: (i, 0))
    return pl.pallas_call(
        body, name=name, grid=(tp // tm,),
        in_specs=[sp["oa"], sp["ob"], sp["z"], sp["rg"], sp["ga"], sp["gb"], sp["gnorm"], sp["rnorm"],
                  sp["w"], sp["w"], sp["w"], sp["res"]],
        out_specs=[act, act, pl.BlockSpec((tm, 4 * hv), lambda i: (i, 0)), act, act, sp["res"], sp["res"],
                   sp["res"], sp["gnorm"], sp["rnorm"]],
        out_shape=[jax.ShapeDtypeStruct((tp, hv), F32), jax.ShapeDtypeStruct((tp, hv), F32),
                   jax.ShapeDtypeStruct((tp, 4 * hv), BF16), jax.ShapeDtypeStruct((tp, hv), BF16),
                   jax.ShapeDtypeStruct((tp, hv), BF16), jax.ShapeDtypeStruct((tp, d), BF16),
                   jax.ShapeDtypeStruct((tp, d), BF16), jax.ShapeDtypeStruct((tp, d), BF16),
                   jax.ShapeDtypeStruct((1, HEAD_DIM), F32), jax.ShapeDtypeStruct((1, hv), F32)],
        scratch_shapes=[pltpu.VMEM((tm, hv), F32), pltpu.VMEM((tm, hv), F32)],
        compiler_params=_params("arbitrary"))(oa, ob, proj, proj, proj, proj, gnorm, rnorm, wbg, wbr, wo, dh2)


def _final(h3, gain, target, name):
    tp, d = h3.shape
    tm = HEAD_ROWS

    def body(h_ref, g_ref, t_ref, loss_ref, dh_ref, dgain_ref):
        i = pl.program_id(0)

        @pl.when(i == 0)
        def _():
            loss_ref[...] = jnp.zeros_like(loss_ref)
            dgain_ref[...] = jnp.zeros_like(dgain_ref)

        xh, r = _rms_parts(h_ref[...])
        err = jnp.where(i == 0, 0.0, xh * g_ref[...] - t_ref[...])
        dx, dg = _rms_bwd(err * (1.0 / d), xh, r, g_ref[...])
        dh_ref[...] = dx
        dgain_ref[...] += dg
        loss_ref[...] += 0.5 * jnp.sum(jnp.mean(err * err, axis=-1, keepdims=True), axis=0, keepdims=True)

    row = pl.BlockSpec((tm, d), lambda i: (i, 0))
    vec = pl.BlockSpec((1, d), lambda i: (0, 0))
    return pl.pallas_call(
        body, name=name, grid=(tp // tm,),
        in_specs=[row, vec, pl.BlockSpec((tm, d), lambda i: (jnp.maximum(i - 1, 0), 0))],
        out_specs=[pl.BlockSpec((1, LANES), lambda i: (0, 0)), row, vec],
        out_shape=[jax.ShapeDtypeStruct((1, LANES), F32), jax.ShapeDtypeStruct((tp, d), F32),
                   jax.ShapeDtypeStruct((1, d), F32)],
        compiler_params=_params("arbitrary"))(h3, gain, target)


def _peer(k):
    x, y, c = lax.axis_index("x"), lax.axis_index("y"), lax.axis_index("c")
    return (1 - x if k & 4 else x, 1 - y if k & 2 else y, 1 - c if k & 1 else c)


def _my_index():
    return 4 * lax.axis_index("x") + 2 * lax.axis_index("y") + lax.axis_index("c")


def _exchange(bufs, scatter, name):
    n = len(bufs)

    def body(*refs):
        _exchange_copies(refs[:n], refs[n:2 * n], refs[2 * n:], scatter, True, True)

    hbm, out_shape, sems = _exchange_refs(bufs)
    return pl.pallas_call(
        body, name=name, in_specs=hbm, out_specs=hbm, out_shape=out_shape, scratch_shapes=sems,
        compiler_params=pltpu.CompilerParams(has_side_effects=True))(*bufs)


def _gather_via_sibling(bufs, name):
    n = len(bufs)

    def body(*refs):
        x_refs, out_refs = refs[:n], refs[n:2 * n]
        send_sems, recv_sems, local_sems = refs[2 * n:]
        x, y, c = lax.axis_index("x"), lax.axis_index("y"), lax.axis_index("c")
        me, sibling = (x, y, c), (x, y, 1 - c)
        chips = [(1 - x, y), (x, 1 - y), (1 - x, 1 - y)]
        rows = lambda a, dev: out_refs[a].at[4 * dev[0] + 2 * dev[1] + dev[2]]

        def copy(k, a, block, to, src=None):
            return pltpu.make_async_remote_copy(
                src_ref=rows(a, block) if src is None else src, dst_ref=rows(a, block),
                send_sem=send_sems.at[k * n + a], recv_sem=recv_sems.at[k * n + a],
                device_id=to, device_id_type=pl.DeviceIdType.MESH)

        mine = [pltpu.make_async_copy(x_refs[a], rows(a, me), local_sems.at[a]) for a in range(n)]
        first = [copy(0, a, me, sibling, src=x_refs[a]) for a in range(n)]
        first += [copy(1 + j, a, me, (*chip, c), src=x_refs[a]) for j, chip in enumerate(chips) for a in range(n)]
        for cp in mine + first:
            cp.start()
        passed = []
        for j, chip in enumerate(chips):
            for a in range(n):
                copy(1 + j, a, (*chip, c), me).wait_recv()
                passed.append(copy(4 + j, a, (*chip, c), sibling))
                passed[-1].start()
        for a in range(n):
            copy(0, a, sibling, me).wait_recv()
        for j, chip in enumerate(chips):
            for a in range(n):
                copy(4 + j, a, (*chip, 1 - c), me).wait_recv()
        for cp in first + passed:
            cp.wait_send()
        for cp in mine:
            cp.wait()

    hbm, out_shape, sems = _exchange_refs(bufs)
    return pl.pallas_call(
        body, name=name, in_specs=hbm, out_specs=hbm, out_shape=out_shape, scratch_shapes=sems,
        compiler_params=pltpu.CompilerParams(has_side_effects=True))(*bufs)


def _exchange_refs(bufs):
    n = len(bufs)
    return ([pl.BlockSpec(memory_space=pl.ANY)] * n,
            [jax.ShapeDtypeStruct((N_DEV,) + b.shape[-2:], b.dtype) for b in bufs],
            [pltpu.SemaphoreType.DMA(((N_DEV - 1) * n,)), pltpu.SemaphoreType.DMA(((N_DEV - 1) * n,)),
             pltpu.SemaphoreType.DMA((n,))])


def _exchange_copies(x_refs, out_refs, sems, scatter, start, wait):
    n = len(x_refs)
    send_sems, recv_sems, local_sems = sems
    me = _my_index()
    copies = []
    for a in range(n):
        copies.append(pltpu.make_async_copy(x_refs[a].at[me] if scatter else x_refs[a], out_refs[a].at[me],
                                            local_sems.at[a]))
    sends = []
    arrivals = []
    for k in range(1, N_DEV):
        x, y, c = _peer(k)
        peer = 4 * x + 2 * y + c
        for a in range(n):
            sem = (k - 1) * n + a
            sends.append(pltpu.make_async_remote_copy(
                src_ref=x_refs[a].at[peer] if scatter else x_refs[a], dst_ref=out_refs[a].at[me],
                send_sem=send_sems.at[sem], recv_sem=recv_sems.at[sem],
                device_id=(x, y, c), device_id_type=pl.DeviceIdType.MESH))
            if wait:
                landed = out_refs[a].at[peer]
                arrivals.append(pltpu.make_async_remote_copy(
                    src_ref=landed, dst_ref=landed, send_sem=send_sems.at[sem], recv_sem=recv_sems.at[sem],
                    device_id=(x, y, c), device_id_type=pl.DeviceIdType.MESH))
    if start:
        for cp in copies + sends:
            cp.start()
    if wait:
        for cp in arrivals:
            cp.wait_recv()
        for cp in sends:
            cp.wait_send()
        for cp in copies:
            cp.wait()


def _carried_call(body, carry, first, last, *, name, grid, in_specs, out_specs, out_shape, scratch_shapes=()):
    in_specs, out_specs, out_shape = list(in_specs), list(out_specs), list(out_shape)
    semantics = ("arbitrary",) * len(grid)
    if carry is None:
        call = pl.pallas_call(body, name=name, grid=grid, in_specs=in_specs, out_specs=out_specs,
                              out_shape=out_shape, scratch_shapes=list(scratch_shapes),
                              compiler_params=_params(*semantics))
        return lambda *args: (call(*args), [])
    bufs, scatter = carry
    n, n_in, n_out, n_scratch = len(bufs), len(in_specs), len(out_specs), len(scratch_shapes)
    hbm, x_shapes, sems = _exchange_refs(bufs)

    def full_body(*refs):
        ins, x_refs = refs[:n_in], refs[n_in:n_in + n]
        outs, xo_refs = refs[n_in + n:n_in + n + n_out], refs[n_in + n + n_out:n_in + 2 * n + n_out]
        scratch = refs[n_in + 2 * n + n_out:n_in + 2 * n + n_out + n_scratch]
        x_sems = refs[n_in + 2 * n + n_out + n_scratch:]

        @pl.when(first())
        def _():
            _exchange_copies(x_refs, xo_refs, x_sems, scatter, True, False)

        body(*ins, *outs, *scratch)

        @pl.when(last())
        def _():
            _exchange_copies(x_refs, xo_refs, x_sems, scatter, False, True)

    call = pl.pallas_call(full_body, name=name, grid=grid, in_specs=in_specs + hbm, out_specs=out_specs + hbm,
                          out_shape=out_shape + x_shapes, scratch_shapes=list(scratch_shapes) + sems,
                          compiler_params=_params(*semantics))

    def run(*args):
        res = call(*args, *bufs)
        return res[:n_out], res[n_out:]
    return run


def _adamw(w, g, m, v, name):
    r, c = w.shape
    parts = g.ndim == 3
    tr = _tile(r, 256, 16 if parts else 8)
    c1 = 1.0 - ADAM_B1 ** ADAM_STEP
    c2 = 1.0 - ADAM_B2 ** ADAM_STEP

    def body(w_ref, g_ref, m_ref, v_ref, go_ref, d_ref, mo_ref, vo_ref):
        if parts:
            g = g_ref[0].astype(F32)
            for q in range(1, N_DEV):
                g = g + g_ref[q].astype(F32)
        else:
            g = g_ref[...]
        m = ADAM_B1 * m_ref[...] + (1.0 - ADAM_B1) * g
        v = ADAM_B2 * v_ref[...] + (1.0 - ADAM_B2) * (g * g)
        go_ref[...] = g
        d_ref[...] = -ADAM_LR * ((m / c1) / (jnp.sqrt(v / c2) + ADAM_EPS) + ADAM_WD * w_ref[...])
        mo_ref[...] = m
        vo_ref[...] = v

    blk = pl.BlockSpec((tr, c), lambda i: (i, 0))
    g_spec = pl.BlockSpec((N_DEV, tr, c), lambda i: (0, i, 0)) if parts else blk
    return pl.pallas_call(
        body, name=name, grid=(r // tr,), in_specs=[blk, g_spec, blk, blk], out_specs=[blk] * 4,
        out_shape=[jax.ShapeDtypeStruct((r, c), F32)] * 4,
        compiler_params=_params("parallel"))(w, g, m, v)


def _win_segments(hv, nh):
    o_z, o_b = 3 * hv, 4 * hv
    o_r = o_b + 2 * nh
    return [(0, 0, 3 * hv), (3 * hv, o_r, 3 * hv), (6 * hv, o_z, hv), (7 * hv, o_r + 3 * hv, 3 * hv),
            (10 * hv, o_b, 2 * nh)]


def _win_from_shards(shards, hv, nh):
    _, d, cs = shards.shape
    pieces = []
    for _, src, width in _win_segments(hv, nh):
        lo = src
        while lo < src + width:
            p = lo // cs
            hi = min(src + width, (p + 1) * cs)
            pieces.append(shards[p][:, lo - p * cs:hi - p * cs])
            lo = hi
    pieces.append(jnp.zeros((d, LANES - 2 * nh), shards.dtype))
    return jnp.concatenate(pieces, axis=1)


def _win_grad_to_shards(parts, hv, nh, cs):
    segments = _win_segments(hv, nh)
    starts = [sum(p.shape[1] for p in parts[:i]) for i in range(len(parts))]

    def columns(a, b):
        out = []
        for part, start in zip(parts, starts):
            lo, hi = max(a, start), min(b, start + part.shape[1])
            if lo < hi:
                out.append(part[:, lo - start:hi - start])
        return out

    shards = []
    for p in range(N_DEV):
        pieces = []
        lo = p * cs
        while lo < (p + 1) * cs:
            here, src, width = next(s for s in segments if s[1] <= lo < s[1] + s[2])
            hi = min((p + 1) * cs, src + width)
            pieces += columns(here + lo - src, here + hi - src)
            lo = hi
        shards.append(jnp.concatenate(pieces, axis=1))
    return jnp.stack(shards)


def _rope_tables(tp):
    pos = jnp.arange(tp, dtype=F32) - float(PAD_FRONT)
    inv = 1.0 / (ROPE_BASE ** jnp.linspace(0.0, 1.0, HEAD_DIM // 2, dtype=F32))
    ang = pos[:, None] * inv[None, :]
    cos = jnp.repeat(jnp.cos(ang), 2, axis=1)
    sin = jnp.repeat(jnp.sin(ang), 2, axis=1) * jnp.tile(jnp.array([-1.0, 1.0], F32), HEAD_DIM // 2)[None, :]
    return cos, sin


def _retention_tables(nh):
    log_gamma = jnp.log1p(-jnp.exp2(-5.0 - jnp.arange(nh, dtype=F32)))
    pos = jnp.arange(CHUNK, dtype=F32)
    causal = pos[:, None] >= pos[None, :]
    diff = pos[:, None] - pos[None, :]
    dec = jnp.where(causal, jnp.exp(jnp.where(causal, diff, 0.0) * log_gamma[:, None, None]), 0.0)
    ones = jnp.ones((1, 1, HEAD_DIM), F32)
    xi = jnp.exp((pos + 1.0)[None, :] * log_gamma[:, None])[:, :, None] * ones
    zeta = jnp.exp((CHUNK - 1.0 - pos)[None, :] * log_gamma[:, None])[:, :, None] * ones
    cd = jnp.exp(CHUNK * log_gamma)[:, None, None] * jnp.ones((1, 8, HEAD_DIM), F32)
    return dec, xi, zeta, cd


SHARDED = ("meta_tokens", "ffn1_w_in", "ffn1_w_out", "w_in", "gdn_conv_w", "w_branch_gdn", "w_branch_ret",
           "w_out", "ffn2_w_in", "ffn2_w_out")
COLUMN_SHARDED = ("meta_tokens", "ffn1_w_in", "w_in", "gdn_conv_w", "ffn2_w_in")
EXACT_F32 = ("meta_tokens", "gdn_conv_w")
REPLICATED = ("ffn1_norm", "mix_norm", "gdn_a_log", "gdn_dt_bias", "gdn_out_norm", "ret_out_norm", "ffn2_norm",
              "final_norm")
WEIGHTS = ("meta_tokens", "ffn1_norm", "ffn1_w_in", "ffn1_w_out", "mix_norm", "w_in", "gdn_conv_w", "gdn_a_log",
           "gdn_dt_bias", "gdn_out_norm", "ret_out_norm", "w_branch_gdn", "w_branch_ret", "w_out", "ffn2_norm",
           "ffn2_w_in", "ffn2_w_out", "final_norm")


def _as2d(a):
    if a.ndim == 3:
        return a[0]
    if a.ndim == 1:
        return a[None, :]
    return a


def _rows_of(shards):
    return shards.reshape(-1, shards.shape[2])


def _cols_of(shards):
    return shards.transpose(1, 0, 2).reshape(shards.shape[1], -1)


def _row_shards(a):
    return a.reshape(N_DEV, -1, a.shape[1])


def _col_shards(a):
    return a.reshape(a.shape[0], N_DEV, -1).transpose(1, 0, 2)


GATHER_FIRST = ("meta_tokens", "ffn1_w_in", "ffn1_w_out")
GATHER_BEHIND_FFN1 = ("w_in", "gdn_conv_w")
GATHER_BEHIND_PROJ = ("w_branch_gdn", "w_branch_ret", "w_out", "ffn2_w_in", "ffn2_w_out")
SCATTER_BEHIND_DN2 = ("ffn2_w_in", "ffn2_w_out", "w_branch_gdn", "w_branch_ret", "w_out")
SCATTER_BEHIND_FFN1 = ("w_in", "gdn_conv_w")
SCATTER_BEHIND_DWG = ("meta_tokens", "ffn1_w_out")
SCATTER_LAST = ("ffn1_w_in",)


def _device_step(x, target, send, rep):
    seq, d = x.shape
    tp = HEAD_ROWS + seq
    hv = d
    nh = hv // HEAD_DIM
    assert tp % (SCAN_CHUNKS * CHUNK) == 0 and tp % HEAD_ROWS == 0
    bf16_shards = lambda grads, names: [grads[n].astype(BF16) for n in names]

    pad_lanes = lambda row: jnp.pad(row, ((0, 0), (nh, LANES - 2 * nh)))
    alog = pad_lanes(rep["gdn_a_log"])
    dtb = pad_lanes(rep["gdn_dt_bias"])
    cos, sin = _rope_tables(tp)
    dec, xi, zeta, cd = _retention_tables(nh)

    got = dict(zip(GATHER_FIRST, _gather_via_sibling([send[n] for n in GATHER_FIRST], "gather_ffn1")))
    h0 = jnp.concatenate([jnp.zeros((PAD_FRONT, d), F32), _cols_of(got["meta_tokens"]), x], axis=0)
    f1i, f1o = got["ffn1_w_in"], _rows_of(got["ffn1_w_out"])
    (h1, hid1, dup1, dgate1), moved = _ffn_fwd(h0, rep["ffn1_norm"], f1i, f1o, "ffn1_fwd",
                                               ([send[n] for n in GATHER_BEHIND_FFN1], False))
    got.update(zip(GATHER_BEHIND_FFN1, moved))
    wp = _win_from_shards(got["w_in"], hv, nh)
    conv_w = _cols_of(got["gdn_conv_w"])
    (proj, n2), moved = _proj_fwd(h1, rep["mix_norm"], wp, "proj_fwd",
                                  ([send[n] for n in GATHER_BEHIND_PROJ], False))
    got.update(zip(GATHER_BEHIND_PROJ, moved))
    wbg, wbr, wo = _rows_of(got["w_branch_gdn"]), _rows_of(got["w_branch_ret"]), _rows_of(got["w_out"])
    f2i, f2o = got["ffn2_w_in"], _rows_of(got["ffn2_w_out"])
    qkv, conv_out = _conv_fwd(proj, conv_w, hv, "conv_fwd")
    oa, s_gdn, t_gdn = _gdn_fwd(qkv, proj, alog, dtb, nh, "gdn_fwd")
    ob, s_ret = _ret_fwd(proj, cos, sin, dec, xi, zeta, cd, nh, "ret_fwd")
    h2 = _post_fwd(oa, ob, proj, rep["gdn_out_norm"], rep["ret_out_norm"], wbg, wbr, wo, h1, "post_fwd")
    (h3, hid2, dup2, dgate2), _ = _ffn_fwd(h2, rep["ffn2_norm"], f2i, f2o, "ffn2_fwd")
    loss_row, dh3, d_final = _final(h3, rep["final_norm"], target, "final")

    (dh2, d_f2n, n3, dag2, dau2), _ = _ffn_bwd(h2, dh3, rep["ffn2_norm"], f2i, f2o, dup2, dgate2, "ffn2_bwd")
    grads = {"ffn2_w_in": jnp.concatenate([_matmul_tn_blocks(n3, dag2, "ffn2_dwg"),
                                           _matmul_tn_blocks(n3, dau2, "ffn2_dwu")]),
             "ffn2_w_out": _row_shards(_matmul_tn_blocks(hid2, dh3, "ffn2_dwo", 0.5))}

    doa, dob, dgate, ya, yb, merged, dpa, dpb, d_gn, d_rn = _post_bwd(
        oa, ob, proj, rep["gdn_out_norm"], rep["ret_out_norm"], wbg, wbr, wo, dh2, "post_bwd")
    grads["w_branch_gdn"] = _row_shards(_matmul_tn(ya, dpa, "dw_branch_gdn"))
    grads["w_branch_ret"] = _row_shards(_matmul_tn(yb, dpb, "dw_branch_ret"))
    grads["w_out"] = _row_shards(_matmul_tn(merged, dh2, "dw_out"))

    d_ret = _ret_bwd(proj, cos, sin, dec, xi, zeta, cd, s_ret, dob, nh, "ret_bwd")
    gdn_grads = _gdn_bwd(qkv, proj, alog, dtb, s_gdn, t_gdn, doa, nh, "gdn_bwd")
    dba, d_alog, d_dtb = gdn_grads[3:]
    dpre, g_conv = [], []
    for grp, tag in enumerate("qkv"):
        dx, dw = _conv_bwd(proj, conv_out, conv_w, gdn_grads[grp], grp, hv, "conv_bwd_" + tag)
        dpre.append(dx)
        g_conv.append(dw)
    grads["gdn_conv_w"] = _col_shards(jnp.concatenate(g_conv, axis=1))

    wide = dpre + list(d_ret) + [dgate]
    dn2, moved = _matmul_nt_parts(wide, wp[:, :10 * hv], None, "dn2_wide",
                                  (bf16_shards(grads, SCATTER_BEHIND_DN2), True))
    parts = dict(zip(SCATTER_BEHIND_DN2, moved))
    dn2, _ = _matmul_nt_parts([dba], wp[:, 10 * hv:], dn2, "dn2_beta_alpha")
    g_wp = [_matmul_tn(n2, dg, "dw_in_%d" % idx) for idx, dg in enumerate(wide + [dba])]
    grads["w_in"] = _win_grad_to_shards(g_wp, hv, nh, send["w_in"].shape[1])
    dh1, d_mixn = _norm_bwd(h1, rep["mix_norm"], dn2, dh2, "mix_norm_bwd")

    (dh0, d_f1n, n1, dag1, dau1), moved = _ffn_bwd(h0, dh1, rep["ffn1_norm"], f1i, f1o, dup1, dgate1, "ffn1_bwd",
                                                   (bf16_shards(grads, SCATTER_BEHIND_FFN1), True))
    parts.update(zip(SCATTER_BEHIND_FFN1, moved))
    grads["ffn1_w_out"] = _row_shards(_matmul_tn_blocks(hid1, dh1, "ffn1_dwo", 0.5))
    grads["meta_tokens"] = _col_shards(dh0[PAD_FRONT:HEAD_ROWS])
    g_gate, moved = _matmul_tn_blocks(n1, dag1, "ffn1_dwg", carry=(bf16_shards(grads, SCATTER_BEHIND_DWG), True))
    parts.update(zip(SCATTER_BEHIND_DWG, moved))
    grads["ffn1_w_in"] = jnp.concatenate([g_gate, _matmul_tn_blocks(n1, dau1, "ffn1_dwu")])
    parts.update(zip(SCATTER_LAST, _exchange(bf16_shards(grads, SCATTER_LAST), True, "scatter_ffn1")))

    small = {"ffn1_norm": d_f1n, "mix_norm": d_mixn, "gdn_a_log": d_alog[:, nh:2 * nh],
             "gdn_dt_bias": d_dtb[:, nh:2 * nh], "gdn_out_norm": d_gn, "ret_out_norm": d_rn, "ffn2_norm": d_f2n,
             "final_norm": d_final}
    return loss_row[0, 0], dh0[HEAD_ROWS:], parts, small


def kernel(x, meta_tokens, ffn1_norm, ffn1_w_in, ffn1_w_out, mix_norm, w_in, gdn_conv_w, gdn_a_log, gdn_dt_bias, gdn_out_norm, ret_out_norm, w_branch_gdn, w_branch_ret, w_out, ffn2_norm, ffn2_w_in, ffn2_w_out, final_norm, loss_target, m_meta_tokens, m_ffn1_norm, m_ffn1_w_in, m_ffn1_w_out, m_mix_norm, m_w_in, m_gdn_conv_w, m_gdn_a_log, m_gdn_dt_bias, m_gdn_out_norm, m_ret_out_norm, m_w_branch_gdn, m_w_branch_ret, m_w_out, m_ffn2_norm, m_ffn2_w_in, m_ffn2_w_out, m_final_norm, v_meta_tokens, v_ffn1_norm, v_ffn1_w_in, v_ffn1_w_out, v_mix_norm, v_w_in, v_gdn_conv_w, v_gdn_a_log, v_gdn_dt_bias, v_gdn_out_norm, v_ret_out_norm, v_w_branch_gdn, v_w_branch_ret, v_w_out, v_ffn2_norm, v_ffn2_w_in, v_ffn2_w_out, v_final_norm):
    given = dict(locals())
    params = {n: _as2d(given[n]) for n in WEIGHTS}
    local = {n: params[n] for n in SHARDED}
    rep = {n: params[n] for n in REPLICATED}

    send = {n: local[n] if n in EXACT_F32 else local[n].astype(BF16) for n in SHARDED}
    loss_sum, grad_x, parts, small = _device_step(x[0], loss_target[0], send, rep)
    parts.update(zip(REPLICATED, _exchange([small[n] for n in REPLICATED], False, "gather_small_grads")))
    loss = lax.psum(loss_sum, ("x", "y", "c"))

    outs = {}
    for n in WEIGHTS:
        res = _adamw(params[n], parts[n], _as2d(given["m_" + n]), _as2d(given["v_" + n]), "adamw_" + n)
        outs[n] = [r.reshape(given[n].shape) for r in res]
    return (loss, grad_x[None], *[outs[n][0] for n in WEIGHTS], *[outs[n][1] for n in WEIGHTS],
            *[outs[n][2] for n in WEIGHTS], *[outs[n][3] for n in WEIGHTS])
```

```python
import functools
import math

import numpy as np
import jax
import jax.numpy as jnp
from jax import lax
from jax.experimental import pallas as pl
from jax.experimental.pallas import tpu as pltpu

F32 = jnp.float32
BF16 = jnp.bfloat16

N_DEV = 8
N_META = 16
CHUNK = 64
HEAD_DIM = 128
CONV_K = 4
ROPE_BASE = 10000.0
EPS = 1e-6
PAD_FRONT = 240
HEAD_ROWS = PAD_FRONT + N_META
LANES = 128
VMEM_LIMIT_BYTES = 56 * 1024 * 1024

ADAM_LR = 0.001
ADAM_B1 = 0.9
ADAM_B2 = 0.999
ADAM_EPS = 1e-08
ADAM_WD = 0.01
ADAM_STEP = 10

NN = (((1,), (0,)), ((), ()))
NT = (((1,), (1,)), ((), ()))
TN = (((0,), (0,)), ((), ()))


def _tile(n, target, mult):
    best = 0
    for t in range(mult, min(n, target) + 1, mult):
        if n % t == 0:
            best = t
    return best if best else n


def _params(*semantics):
    return pltpu.CompilerParams(dimension_semantics=semantics, vmem_limit_bytes=VMEM_LIMIT_BYTES)


def _split(a, pieces):
    out = []
    for _ in range(pieces - 1):
        part = a.astype(BF16)
        out.append(part)
        a = a - part.astype(F32)
    return out + [a.astype(BF16)]


def _raw_dot(a, b, dims, hi):
    dot = lambda x, y: lax.dot_general(x, y, dims, preferred_element_type=F32)
    if hi:
        (a_hi, a_lo), (b_hi, b_lo) = _split(a, 2), _split(b, 2)
        return dot(a_hi, b_hi) + (dot(a_hi, b_lo) + dot(a_lo, b_hi))
    return dot(a.astype(BF16), b.astype(BF16))


def _mask_dot(mask, x, dims):
    mask = mask.astype(BF16)
    hi, mid, lo = [lax.dot_general(mask, p, dims, preferred_element_type=F32) for p in _split(x, 3)]
    return hi + (mid + lo)


@jax.custom_vjp
def _cumsum_rows(x):
    c = x.shape[0]
    tril = lax.broadcasted_iota(jnp.int32, (c, c), 0) >= lax.broadcasted_iota(jnp.int32, (c, c), 1)
    return _mask_dot(tril, x, NN)


def _cumsum_rows_bwd(_, g):
    c = g.shape[0]
    tril = lax.broadcasted_iota(jnp.int32, (c, c), 0) >= lax.broadcasted_iota(jnp.int32, (c, c), 1)
    return (_mask_dot(tril, g, TN),)


_cumsum_rows.defvjp(lambda x: (_cumsum_rows(x), None), _cumsum_rows_bwd)


def _unit_lower_inverses(xs):
    c = xs[0].shape[0]
    eye = (lax.broadcasted_iota(jnp.int32, (c, c), 0) == lax.broadcasted_iota(jnp.int32, (c, c), 1)).astype(F32)
    t_inv = [eye + x for x in xs]
    for _ in range(int(math.log2(c)) - 1):
        xs = [_raw_dot(x, x, NN, True) for x in xs]
        t_inv = [t + _raw_dot(t, x, NN, True) for t, x in zip(t_inv, xs)]
    return t_inv


@jax.custom_vjp
def _known_inverse(x_neg, t_inv):
    return t_inv


_known_inverse.defvjp(
    lambda x_neg, t_inv: (t_inv, t_inv),
    lambda t_inv, g: (_raw_dot(_raw_dot(t_inv, g, TN, False), t_inv, NT, False), jnp.zeros_like(t_inv)))


def _make_mm(hi):
    @jax.custom_vjp
    def nn(a, b):
        return _raw_dot(a, b, NN, hi)

    @jax.custom_vjp
    def nt(a, b):
        return _raw_dot(a, b, NT, hi)

    @jax.custom_vjp
    def tn(a, b):
        return _raw_dot(a, b, TN, hi)

    nn.defvjp(lambda a, b: (_raw_dot(a, b, NN, hi), (a, b)),
              lambda r, g: (_raw_dot(g, r[1], NT, False), _raw_dot(r[0], g, TN, False)))
    nt.defvjp(lambda a, b: (_raw_dot(a, b, NT, hi), (a, b)),
              lambda r, g: (_raw_dot(g, r[1], NN, False), _raw_dot(g, r[0], TN, False)))
    tn.defvjp(lambda a, b: (_raw_dot(a, b, TN, hi), (a, b)),
              lambda r, g: (_raw_dot(r[1], g, NT, False), _raw_dot(r[0], g, NN, False)))
    return nn, nt, tn


def _silu(x):
    return x * jax.nn.sigmoid(x)


def _rms_parts(x):
    r = lax.rsqrt(jnp.mean(x * x, axis=-1, keepdims=True) + EPS)
    return x * r, r


def _rms_bwd(dy, xh, r, gain):
    dxh = dy * gain
    dx = r * (dxh - xh * jnp.mean(dxh * xh, axis=-1, keepdims=True))
    return dx, jnp.sum(dy * xh, axis=0, keepdims=True)


def _ffn_specs(tm, d, tf, nj):
    return [pl.BlockSpec((tm, d), lambda i, j: (i, 0)), pl.BlockSpec((1, d), lambda i, j: (0, 0)),
            pl.BlockSpec((1, d, tf), lambda i, j: (j, 0, 0)), pl.BlockSpec((1, d, tf), lambda i, j: (nj + j, 0, 0)),
            pl.BlockSpec((tf, d), lambda i, j: (j, 0))]


def _first_step(ndim):
    return lambda: functools.reduce(lambda a, b: a & b, [pl.program_id(k) == 0 for k in range(ndim)])


def _last_step(grid):
    return lambda: functools.reduce(lambda a, b: a & b, [pl.program_id(k) == g - 1 for k, g in enumerate(grid)])


def _ffn_fwd(h, gain, w_in, wo, name, carry=None):
    tp, d = h.shape
    tf = w_in.shape[2]
    nj = w_in.shape[0] // 2
    tm = _tile(tp, 768, 8)
    row, vec, wg_spec, wu_spec, wo_spec = _ffn_specs(tm, d, tf, nj)

    def body(h_ref, g_ref, wg3_ref, wu3_ref, wo_ref, o_ref, hid3_ref, dup3_ref, dgate3_ref, n_sc, acc_sc):
        wg_ref, wu_ref = wg3_ref.at[0], wu3_ref.at[0]
        j = pl.program_id(1)

        @pl.when(j == 0)
        def _():
            xh, _ = _rms_parts(h_ref[...])
            n_sc[...] = (xh * g_ref[...]).astype(BF16)
            acc_sc[...] = jnp.zeros_like(acc_sc)

        n = n_sc[...]
        a_g = jnp.dot(n, wg_ref[...], preferred_element_type=F32)
        a_u = jnp.dot(n, wu_ref[...], preferred_element_type=F32)
        sg = jax.nn.sigmoid(a_g)
        s = a_g * sg
        hid = (s * a_u).astype(BF16)
        hid3_ref[0] = hid
        dup3_ref[0] = s.astype(BF16)
        dgate3_ref[0] = (a_u * _dsilu(a_g, sg)).astype(BF16)
        acc_sc[...] += jnp.dot(hid, wo_ref[...], preferred_element_type=F32)

        @pl.when(j == nj - 1)
        def _():
            o_ref[...] = h_ref[...] + 0.5 * acc_sc[...]

    grid = (tp // tm, nj)
    act = pl.BlockSpec((1, tm, tf), lambda i, j: (j, i, 0))
    return _carried_call(
        body, carry, _first_step(2), _last_step(grid), name=name, grid=grid,
        in_specs=[row, vec, wg_spec, wu_spec, wo_spec], out_specs=[row, act, act, act],
        out_shape=[jax.ShapeDtypeStruct((tp, d), F32)] + [jax.ShapeDtypeStruct((nj, tp, tf), BF16)] * 3,
        scratch_shapes=[pltpu.VMEM((tm, d), BF16), pltpu.VMEM((tm, d), F32)])(h, gain, w_in, w_in, wo)


def _ffn_bwd(h, dho, gain, w_in, wo, dup3, dgate3, name, carry=None):
    tp, d = h.shape
    tf = w_in.shape[2]
    nj = w_in.shape[0] // 2
    tm = _tile(tp, 704, 16)
    ni = tp // tm
    row, vec, wg_spec, wu_spec, wo_spec = _ffn_specs(tm, d, tf, nj)

    def body(h_ref, dho_ref, g_ref, wg3_ref, wu3_ref, wo_ref, dup3_ref, dgate3_ref,
             dh_ref, dgain_ref, n_ref, dag3_ref, dau3_ref, dn_sc, dhb_sc):
        wg_ref, wu_ref = wg3_ref.at[0], wu3_ref.at[0]
        dag_ref, dau_ref = dag3_ref.at[0], dau3_ref.at[0]
        i, j = pl.program_id(0), pl.program_id(1)

        @pl.when(j == 0)
        def _():
            xh, _ = _rms_parts(h_ref[...])
            n_ref[...] = (xh * g_ref[...]).astype(BF16)
            dn_sc[...] = jnp.zeros_like(dn_sc)
            dhb_sc[...] = (0.5 * dho_ref[...]).astype(BF16)

        @pl.when((i == 0) & (j == 0))
        def _():
            dgain_ref[...] = jnp.zeros_like(dgain_ref)

        d_hid = lax.dot_general(dhb_sc[...], wo_ref[...], NT, preferred_element_type=F32)
        d_au = (d_hid * dup3_ref[0].astype(F32)).astype(BF16)
        d_ag = (d_hid * dgate3_ref[0].astype(F32)).astype(BF16)
        dau_ref[...] = d_au
        dag_ref[...] = d_ag
        dn_sc[...] += (lax.dot_general(d_ag, wg_ref[...], NT, preferred_element_type=F32)
                       + lax.dot_general(d_au, wu_ref[...], NT, preferred_element_type=F32))

        @pl.when(j == nj - 1)
        def _():
            xh, r = _rms_parts(h_ref[...])
            dx, dg = _rms_bwd(dn_sc[...], xh, r, g_ref[...])
            dh_ref[...] = dho_ref[...] + dx
            dgain_ref[...] += dg

    act = pl.BlockSpec((1, tm, tf), lambda i, j: (j, i, 0))
    return _carried_call(
        body, carry, _first_step(2), _last_step((ni, nj)), name=name, grid=(ni, nj),
        in_specs=[row, row, vec, wg_spec, wu_spec, wo_spec, act, act],
        out_specs=[row, vec, row, act, act],
        out_shape=[jax.ShapeDtypeStruct((tp, d), F32), jax.ShapeDtypeStruct((1, d), F32),
                   jax.ShapeDtypeStruct((tp, d), BF16)] + [jax.ShapeDtypeStruct((nj, tp, tf), BF16)] * 2,
        scratch_shapes=[pltpu.VMEM((tm, d), F32), pltpu.VMEM((tm, d), BF16)])(
            h, dho, gain, w_in, w_in, wo, dup3, dgate3)


def _matmul_tn(a, b, name, scale=1.0):
    t, m = a.shape
    n = b.shape[1]
    bm = _tile(m, 1024, LANES)
    bn = _tile(n, 1536, LANES)
    tk = _tile(t, 1408, 16)
    nk = t // tk

    def body(a_ref, b_ref, o_ref):
        k = pl.program_id(2)

        @pl.when(k == 0)
        def _():
            o_ref[...] = jnp.zeros_like(o_ref)

        o_ref[...] += lax.dot_general(a_ref[...].astype(BF16), b_ref[...].astype(BF16), TN,
                                      preferred_element_type=F32)

        if scale != 1.0:
            @pl.when(k == nk - 1)
            def _():
                o_ref[...] = o_ref[...] * scale

    return pl.pallas_call(
        body, name=name, grid=(m // bm, n // bn, nk),
        in_specs=[pl.BlockSpec((tk, bm), lambda i, j, k: (k, i)), pl.BlockSpec((tk, bn), lambda i, j, k: (k, j))],
        out_specs=pl.BlockSpec((bm, bn), lambda i, j, k: (i, j)),
        out_shape=jax.ShapeDtypeStruct((m, n), F32),
        compiler_params=_params("parallel", "parallel", "arbitrary"))(a, b)


def _matmul_tn_blocks(a, b, name, scale=1.0, carry=None):
    a_blocked = a.ndim == 3
    nb, t = (a.shape[0], a.shape[1]) if a_blocked else (b.shape[0], b.shape[1])
    m, n = a.shape[-1], b.shape[-1]
    tk = _tile(t, 1408, 16)
    nk = t // tk
    if a_blocked:
        bo = _tile(n, 1024, LANES)
        a_spec = pl.BlockSpec((1, tk, m), lambda p, o, k: (p, k, 0))
        b_spec = pl.BlockSpec((tk, bo), lambda p, o, k: (k, o))
        o_spec = pl.BlockSpec((m, bo), lambda p, o, k: (p, o))
        out_shape = jax.ShapeDtypeStruct((nb * m, n), F32)
        grid = (nb, n // bo, nk)
    else:
        bo = _tile(m, 1024, LANES)
        a_spec = pl.BlockSpec((tk, bo), lambda p, o, k: (k, o))
        b_spec = pl.BlockSpec((1, tk, n), lambda p, o, k: (p, k, 0))
        o_spec = pl.BlockSpec((1, bo, n), lambda p, o, k: (p, o, 0))
        out_shape = jax.ShapeDtypeStruct((nb, m, n), F32)
        grid = (nb, m // bo, nk)

    def body(a_ref, b_ref, o_ref):
        k = pl.program_id(2)
        a_blk = a_ref[0] if a_blocked else a_ref[...]
        b_blk = b_ref[...] if a_blocked else b_ref[0]
        part = lax.dot_general(a_blk.astype(BF16), b_blk.astype(BF16), TN, preferred_element_type=F32)
        out = o_ref if a_blocked else o_ref.at[0]

        @pl.when(k == 0)
        def _():
            out[...] = part

        @pl.when(k > 0)
        def _():
            out[...] += part

        if scale != 1.0:
            @pl.when(k == nk - 1)
            def _():
                out[...] = out[...] * scale

    (out,), moved = _carried_call(body, carry, _first_step(3), _last_step(grid), name=name, grid=grid,
                                  in_specs=[a_spec, b_spec], out_specs=[o_spec], out_shape=[out_shape])(a, b)
    return out if carry is None else (out, moved)


def _matmul_nt_parts(parts, w, acc, name, carry=None):
    t = parts[0].shape[0]
    d = w.shape[0]
    widths = [p.shape[1] for p in parts]
    tk = _tile(math.gcd(*widths), 1024, LANES)
    counts = [wd // tk for wd in widths]
    starts = [sum(counts[:g]) for g in range(len(parts))]
    nk = sum(counts)
    tm = _tile(t, 768, 8)
    n_parts = len(parts)

    def body(*refs):
        a_refs, w_ref, o_ref = refs[:n_parts], refs[n_parts], refs[-1]
        k = pl.program_id(1)

        @pl.when(k == 0)
        def _():
            o_ref[...] = jnp.zeros_like(o_ref) if acc is None else refs[n_parts + 1][...]

        for g in range(n_parts):
            @pl.when((k >= starts[g]) & (k < starts[g] + counts[g]))
            def _(g=g):
                o_ref[...] += lax.dot_general(a_refs[g][...].astype(BF16), w_ref[...], NT,
                                              preferred_element_type=F32)

    in_specs = [pl.BlockSpec((tm, tk), lambda i, k, lo=starts[g], nb=counts[g]: (i, jnp.clip(k - lo, 0, nb - 1)))
                for g in range(n_parts)]
    in_specs.append(pl.BlockSpec((d, tk), lambda i, k: (0, k)))
    args = list(parts) + [w]
    if acc is not None:
        in_specs.append(pl.BlockSpec((tm, d), lambda i, k: (i, 0)))
        args.append(acc)
    grid = (t // tm, nk)
    (out,), moved = _carried_call(
        body, carry, _first_step(2), _last_step(grid), name=name, grid=grid, in_specs=in_specs,
        out_specs=[pl.BlockSpec((tm, d), lambda i, k: (i, 0))],
        out_shape=[jax.ShapeDtypeStruct((t, d), F32)])(*args)
    return out, moved


def _proj_fwd(h, gain, wp, name, carry=None):
    tp, d = h.shape
    npad = wp.shape[1]
    tm = _tile(tp, 768, 8)
    tn = _tile(npad, 3456, LANES)

    def body(h_ref, g_ref, w_ref, o_ref, n_ref):
        @pl.when(pl.program_id(1) == 0)
        def _():
            xh, _ = _rms_parts(h_ref[...])
            n_ref[...] = (xh * g_ref[...]).astype(BF16)

        o_ref[...] = jnp.dot(n_ref[...], w_ref[...], preferred_element_type=F32)

    grid = (tp // tm, npad // tn)
    return _carried_call(
        body, carry, _first_step(2), _last_step(grid), name=name, grid=grid,
        in_specs=[pl.BlockSpec((tm, d), lambda i, j: (i, 0)), pl.BlockSpec((1, d), lambda i, j: (0, 0)),
                  pl.BlockSpec((d, tn), lambda i, j: (0, j))],
        out_specs=[pl.BlockSpec((tm, tn), lambda i, j: (i, j)), pl.BlockSpec((tm, d), lambda i, j: (i, 0))],
        out_shape=[jax.ShapeDtypeStruct((tp, npad), F32), jax.ShapeDtypeStruct((tp, d), BF16)])(h, gain, wp)


def _norm_bwd(h, gain, dn, dres, name):
    tp, d = h.shape
    tm = _tile(tp, 256, 8)

    def body(h_ref, g_ref, dn_ref, dres_ref, dh_ref, dgain_ref):
        @pl.when(pl.program_id(0) == 0)
        def _():
            dgain_ref[...] = jnp.zeros_like(dgain_ref)

        xh, r = _rms_parts(h_ref[...])
        dx, dg = _rms_bwd(dn_ref[...], xh, r, g_ref[...])
        dh_ref[...] = dres_ref[...] + dx
        dgain_ref[...] += dg

    row = pl.BlockSpec((tm, d), lambda i: (i, 0))
    vec = pl.BlockSpec((1, d), lambda i: (0, 0))
    return pl.pallas_call(
        body, name=name, grid=(tp // tm,), in_specs=[row, vec, row, row], out_specs=[row, vec],
        out_shape=[jax.ShapeDtypeStruct((tp, d), F32), jax.ShapeDtypeStruct((1, d), F32)],
        compiler_params=_params("arbitrary"))(h, gain, dn, dres)


def _head_post(a, grp):
    a = _silu(a)
    r = lax.rsqrt(jnp.sum(a * a, axis=-1, keepdims=True) + EPS)
    if isinstance(grp, int):
        return a if grp == 2 else a * r * (HEAD_DIM ** -0.5 if grp == 0 else 1.0)
    scale = jnp.where(grp == 0, HEAD_DIM ** -0.5, 1.0).astype(F32)
    return jnp.where(grp == 2, a, a * r * scale)


def _head_post_bwd(c, dy, grp):
    sg = jax.nn.sigmoid(c)
    a = c * sg
    dsilu = sg * (1.0 + c * (1.0 - sg))
    if grp == 2:
        return dy * dsilu
    r = lax.rsqrt(jnp.sum(a * a, axis=-1, keepdims=True) + EPS)
    scale = HEAD_DIM ** -0.5 if grp == 0 else 1.0
    da = (scale * r) * (dy - a * (r * r * jnp.sum(dy * a, axis=-1, keepdims=True)))
    return da * dsilu


def _conv_taps(ext_sc, w_ref, tm):
    ext = ext_sc[...]
    c = w_ref[CONV_K - 1:CONV_K, :] * ext[8:, :]
    for i in range(CONV_K - 1):
        s = CONV_K - 1 - i
        c = c + w_ref[i:i + 1, :] * pltpu.roll(ext, s, 0)[8:, :]
    return c


def _conv_fwd(proj, conv_w, hv, name):
    tp = proj.shape[0]
    tm = _tile(tp, 768, 8)
    nh = hv // HEAD_DIM

    def body(x_ref, halo_ref, w_ref, o_ref, c_ref, ext_sc):
        i, grp = pl.program_id(0), pl.program_id(1)
        ext_sc[0:8, :] = jnp.where(i == 0, 0.0, halo_ref[...])
        ext_sc[8:, :] = x_ref[...]
        c_ref[...] = _conv_taps(ext_sc, w_ref, tm)
        for h in range(nh):
            sl = slice(h * HEAD_DIM, (h + 1) * HEAD_DIM)
            o_ref[:, sl] = _head_post(c_ref[:, sl], grp)

    blk = pl.BlockSpec((tm, hv), lambda i, g: (i, g))
    return pl.pallas_call(
        body, name=name, grid=(tp // tm, 3),
        in_specs=[blk, pl.BlockSpec((8, hv), lambda i, g: (jnp.maximum(i * (tm // 8) - 1, 0), g)),
                  pl.BlockSpec((CONV_K, hv), lambda i, g: (0, g))],
        out_specs=[blk, blk],
        out_shape=[jax.ShapeDtypeStruct((tp, 3 * hv), F32)] * 2,
        scratch_shapes=[pltpu.VMEM((tm + 8, hv), F32)],
        compiler_params=_params("parallel", "arbitrary"))(proj, proj, conv_w)


def _conv_bwd(proj, conv_out, conv_w, dy, grp, hv, name):
    tp = proj.shape[0]
    tm = _tile(tp, 768, 8)
    ni = tp // tm
    nh = hv // HEAD_DIM

    def body(x_ref, c_ref, w_ref, dy_ref, dx_ref, dw_ref, dc_sc):
        step = pl.program_id(0)

        @pl.when(step == 0)
        def _():
            dc_sc[tm:, :] = jnp.zeros((8, hv), F32)
            dw_ref[...] = jnp.zeros_like(dw_ref)

        @pl.when(step > 0)
        def _():
            dc_sc[tm:, :] = dc_sc[0:8, :]

        for h in range(nh):
            sl = slice(h * HEAD_DIM, (h + 1) * HEAD_DIM)
            dc_sc[0:tm, sl] = _head_post_bwd(c_ref[:, sl], dy_ref[:, sl], grp)

        x = x_ref[...]
        dc_ext = dc_sc[...]
        dx = None
        for k in range(CONV_K):
            s = CONV_K - 1 - k
            shifted = dc_ext[0:tm, :] if s == 0 else pltpu.roll(dc_ext, tm + 8 - s, 0)[0:tm, :]
            dw_ref[k:k + 1, :] += jnp.sum(shifted * x, axis=0, keepdims=True)
            term = w_ref[k:k + 1, :] * shifted
            dx = term if dx is None else dx + term
        dx_ref[...] = dx.astype(BF16)

    tile = lambda step: ni - 1 - step
    grp_blk = pl.BlockSpec((tm, hv), lambda s: (tile(s), grp))
    own_blk = pl.BlockSpec((tm, hv), lambda s: (tile(s), 0))
    return pl.pallas_call(
        body, name=name, grid=(ni,),
        in_specs=[grp_blk, grp_blk, pl.BlockSpec((CONV_K, hv), lambda s: (0, grp)), own_blk],
        out_specs=[own_blk, pl.BlockSpec((CONV_K, hv), lambda s: (0, 0))],
        out_shape=[jax.ShapeDtypeStruct((tp, hv), BF16), jax.ShapeDtypeStruct((CONV_K, hv), F32)],
        scratch_shapes=[pltpu.VMEM((tm + 8, hv), F32)],
        compiler_params=_params("arbitrary"))(proj, conv_out, conv_w, dy)


def _gdn_gates(ba, alog, dtb):
    x = ba + dtb
    softplus = jnp.maximum(x, 0.0) + jnp.log1p(jnp.exp(-jnp.abs(x)))
    return _cumsum_rows(-jnp.exp(alog) * softplus), jax.nn.sigmoid(ba)


def _gdn_chunks(states, qs, ks, vs, gates, known_inverses=None):
    mm_nn, mm_nt, mm_tn = _make_mm(False)
    hi_nn, _, _ = _make_mm(True)
    nh = len(states)
    items = range(len(qs))
    head = [i % nh for i in items]
    c = qs[0].shape[0]
    lane = lax.broadcasted_iota(jnp.int32, (c, LANES), 1)
    last_row = (lax.broadcasted_iota(jnp.int32, (c, 1), 0) == c - 1).astype(F32)
    ri = lax.broadcasted_iota(jnp.int32, (c, c), 0)
    ci = lax.broadcasted_iota(jnp.int32, (c, c), 1)
    causal = ri >= ci
    strict = ri > ci
    eye = (ri == ci).astype(F32)
    sel_a = [(lane == nh + h).astype(F32) for h in range(nh)]
    sel_b = [(lane == h).astype(F32) for h in range(nh)]

    gcol = [jnp.sum(gates[i // nh][0] * sel_a[head[i]], axis=1, keepdims=True) for i in items]
    grow = [jnp.sum(eye * gcol[i], axis=0, keepdims=True) for i in items]
    beta = [jnp.sum(gates[i // nh][1] * sel_b[head[i]], axis=1, keepdims=True) for i in items]
    decay = [jnp.where(causal, jnp.exp(jnp.where(causal, gcol[i] - grow[i], 0.0)), 0.0) for i in items]
    kb = [ks[i] * beta[i] for i in items]
    kk = [mm_nt(kb[i], ks[i]) for i in items]
    qk = [mm_nt(qs[i], ks[i]) for i in items]
    x_neg = [-jnp.where(strict, kk[i] * decay[i], 0.0) for i in items]
    if known_inverses is None:
        t_inv = _unit_lower_inverses(x_neg)
    else:
        t_inv = [_known_inverse(x_neg[i], known_inverses[i]) for i in items]
    eg = [jnp.exp(gcol[i]) for i in items]
    u = [hi_nn(t_inv[i], vs[i] * beta[i]) for i in items]
    w = [hi_nn(t_inv[i], kb[i] * eg[i]) for i in items]
    qk = [qk[i] * decay[i] for i in items]
    glast = [jnp.sum(gcol[i] * last_row, axis=0, keepdims=True) for i in items]
    q_dec = [qs[i] * eg[i] for i in items]
    k_dec = [ks[i] * jnp.exp(glast[i] - gcol[i]) for i in items]
    s_dec = [jnp.exp(glast[i]) for i in items]

    outs = []
    for first in range(0, len(qs), nh):
        chunk = range(first, first + nh)
        ws = [mm_nn(w[i], states[i - first]) for i in chunk]
        from_state = [mm_nn(q_dec[i], states[i - first]) for i in chunk]
        v_new = [u[i] - ws[i - first] for i in chunk]
        intra = [mm_nn(qk[i], v_new[i - first]) for i in chunk]
        kv = [mm_tn(k_dec[i], v_new[i - first]) for i in chunk]
        outs += [from_state[i - first] + intra[i - first] for i in chunk]
        states = [states[i - first] * s_dec[i] + kv[i - first] for i in chunk]
    return outs, states, t_inv


SCAN_CHUNKS = 4


def _scan_specs(nh, steps, rev, first_col):
    sidx = (lambda s: steps - 1 - s) if rev else (lambda s: s)
    hv = nh * HEAD_DIM
    rows = SCAN_CHUNKS * CHUNK
    cols = [pl.BlockSpec((rows, hv), lambda s, g=g: (sidx(s), first_col + g)) for g in range(3)]
    st = pl.BlockSpec((1, nh, HEAD_DIM, HEAD_DIM), lambda s: (sidx(s), 0, 0, 0))
    act = pl.BlockSpec((rows, hv), lambda s: (sidx(s), 0))
    return cols, st, act


def _chunk_heads(ref, nh):
    return [ref[j * CHUNK:(j + 1) * CHUNK, h * HEAD_DIM:(h + 1) * HEAD_DIM] for j in range(SCAN_CHUNKS)
            for h in range(nh)]


def _store_chunk_heads(ref, values, nh, dtype=None):
    for i, val in enumerate(values):
        j, h = divmod(i, nh)
        ref[j * CHUNK:(j + 1) * CHUNK, h * HEAD_DIM:(h + 1) * HEAD_DIM] = val if dtype is None else val.astype(dtype)


def _gdn_fwd(qkv, proj, alog, dtb, nh, name):
    tp = qkv.shape[0]
    steps = tp // (SCAN_CHUNKS * CHUNK)
    rows = SCAN_CHUNKS * CHUNK

    def body(q_ref, k_ref, v_ref, ba_ref, al_ref, dt_ref, o_ref, st_ref, inv_ref, s_sc):
        @pl.when(pl.program_id(0) == 0)
        def _():
            s_sc[...] = jnp.zeros_like(s_sc)

        gates = [_gdn_gates(ba_ref[j * CHUNK:(j + 1) * CHUNK, :], al_ref[...], dt_ref[...])
                 for j in range(SCAN_CHUNKS)]
        states = [s_sc[h] for h in range(nh)]
        for h in range(nh):
            st_ref[0, h] = states[h]
        outs, new_states, t_inv = _gdn_chunks(states, _chunk_heads(q_ref, nh), _chunk_heads(k_ref, nh),
                                              _chunk_heads(v_ref, nh), gates)
        _store_chunk_heads(o_ref, outs, nh)
        for h in range(nh):
            s_sc[h] = new_states[h]
        for i, t in enumerate(t_inv):
            inv_ref[0, i] = t

    cols, st, act = _scan_specs(nh, steps, False, 0)
    ba = pl.BlockSpec((rows, LANES), lambda s: (s, 10 * nh * HEAD_DIM // LANES))
    vec = pl.BlockSpec((1, LANES), lambda s: (0, 0))
    inv = pl.BlockSpec((1, SCAN_CHUNKS * nh, CHUNK, CHUNK), lambda s: (s, 0, 0, 0))
    return pl.pallas_call(
        body, name=name, grid=(steps,), in_specs=cols + [ba, vec, vec], out_specs=[act, st, inv],
        out_shape=[jax.ShapeDtypeStruct((tp, nh * HEAD_DIM), F32),
                   jax.ShapeDtypeStruct((steps, nh, HEAD_DIM, HEAD_DIM), F32),
                   jax.ShapeDtypeStruct((steps, SCAN_CHUNKS * nh, CHUNK, CHUNK), F32)],
        scratch_shapes=[pltpu.VMEM((nh, HEAD_DIM, HEAD_DIM), F32)],
        compiler_params=_params("arbitrary"))(qkv, qkv, qkv, proj, alog, dtb)


def _gdn_bwd(qkv, proj, alog, dtb, states, inverses, do, nh, name):
    tp = qkv.shape[0]
    steps = tp // (SCAN_CHUNKS * CHUNK)
    rows = SCAN_CHUNKS * CHUNK

    def body(q_ref, k_ref, v_ref, ba_ref, al_ref, dt_ref, st_ref, inv_ref, do_ref,
             dq_ref, dk_ref, dv_ref, dba_ref, dal_ref, ddt_ref, ds_sc):
        @pl.when(pl.program_id(0) == 0)
        def _():
            ds_sc[...] = jnp.zeros_like(ds_sc)
            dal_ref[...] = jnp.zeros_like(dal_ref)
            ddt_ref[...] = jnp.zeros_like(ddt_ref)

        gates, gates_vjps = [], []
        for j in range(SCAN_CHUNKS):
            g, g_vjp = jax.vjp(_gdn_gates, ba_ref[j * CHUNK:(j + 1) * CHUNK, :], al_ref[...], dt_ref[...])
            gates.append(g)
            gates_vjps.append(g_vjp)
        known = [inv_ref[0, i] for i in range(SCAN_CHUNKS * nh)]
        fn = lambda s, q, k, v, g: _gdn_chunks(s, q, k, v, g, known)[:2]
        _, vjp = jax.vjp(fn, [st_ref[0, h] for h in range(nh)], _chunk_heads(q_ref, nh), _chunk_heads(k_ref, nh),
                         _chunk_heads(v_ref, nh), gates)
        ds, dq, dk, dv, dgates = vjp((_chunk_heads(do_ref, nh), [ds_sc[h] for h in range(nh)]))
        for h in range(nh):
            ds_sc[h] = ds[h]
        _store_chunk_heads(dq_ref, dq, nh)
        _store_chunk_heads(dk_ref, dk, nh)
        _store_chunk_heads(dv_ref, dv, nh)
        for j in range(SCAN_CHUNKS):
            dba, dal, ddt = gates_vjps[j](dgates[j])
            dba_ref[j * CHUNK:(j + 1) * CHUNK, :] = dba
            dal_ref[...] += dal
            ddt_ref[...] += ddt

    cols, st, act = _scan_specs(nh, steps, True, 0)
    ba = pl.BlockSpec((rows, LANES), lambda s: (steps - 1 - s, 10 * nh * HEAD_DIM // LANES))
    vec = pl.BlockSpec((1, LANES), lambda s: (0, 0))
    inv = pl.BlockSpec((1, SCAN_CHUNKS * nh, CHUNK, CHUNK), lambda s: (steps - 1 - s, 0, 0, 0))
    return pl.pallas_call(
        body, name=name, grid=(steps,), in_specs=cols + [ba, vec, vec, st, inv, act],
        out_specs=[act, act, act, pl.BlockSpec((rows, LANES), lambda s: (steps - 1 - s, 0)), vec, vec],
        out_shape=[jax.ShapeDtypeStruct((tp, nh * HEAD_DIM), F32)] * 3
                  + [jax.ShapeDtypeStruct((tp, LANES), F32), jax.ShapeDtypeStruct((1, LANES), F32),
                     jax.ShapeDtypeStruct((1, LANES), F32)],
        scratch_shapes=[pltpu.VMEM((nh, HEAD_DIM, HEAD_DIM), F32)],
        compiler_params=_params("arbitrary"))(qkv, qkv, qkv, proj, alog, dtb, states, inverses, do)


def _swap_pairs(t):
    lane = lax.broadcasted_iota(jnp.int32, t.shape, 1)
    n = t.shape[1]
    return jnp.where(lane % 2 == 0, pltpu.roll(t, n - 1, 1), pltpu.roll(t, 1, 1))


def _rot(t, cos, sin_signed):
    return t * cos + _swap_pairs(t) * sin_signed


def _rot_t(dt, cos, sin_signed):
    return dt * cos + _swap_pairs(dt * sin_signed)


def _ret_chunks(states, qs, ks, vs, dec, xi, zeta, cd):
    mm_nn, mm_nt, mm_tn = _make_mm(False)
    nh = len(states)
    items = range(len(qs))
    scores = [mm_nt(qs[i], ks[i]) for i in items]
    kv = [mm_tn(ks[i] * zeta[i % nh], vs[i]) for i in items]
    intra = [mm_nn(scores[i] * dec[i % nh], vs[i]) for i in items]
    q_dec = [qs[i] * xi[i % nh] for i in items]
    outs = []
    for first in range(0, len(qs), nh):
        outs += [intra[first + h] + mm_nn(q_dec[first + h], states[h]) for h in range(nh)]
        states = [states[h] * cd[h] + kv[first + h] for h in range(nh)]
    return outs, states


def _ret_table_specs(nh, steps, rev):
    sidx = (lambda s: steps - 1 - s) if rev else (lambda s: s)
    rope = pl.BlockSpec((SCAN_CHUNKS * CHUNK, HEAD_DIM), lambda s: (sidx(s), 0))
    dec = pl.BlockSpec((nh, CHUNK, CHUNK), lambda s: (0, 0, 0))
    tab = pl.BlockSpec((nh, CHUNK, HEAD_DIM), lambda s: (0, 0, 0))
    cd = pl.BlockSpec((nh, 8, HEAD_DIM), lambda s: (0, 0, 0))
    return [rope, rope, dec, tab, tab, cd]


def _rotated(ref, cos_ref, sin_ref, nh, scale=1.0):
    out = []
    for j in range(SCAN_CHUNKS):
        rows = slice(j * CHUNK, (j + 1) * CHUNK)
        cos_t, sin_t = cos_ref[rows, :], sin_ref[rows, :]
        for h in range(nh):
            t = _rot(ref[rows, h * HEAD_DIM:(h + 1) * HEAD_DIM], cos_t, sin_t)
            out.append(t if scale == 1.0 else t * scale)
    return out


def _ret_fwd(proj, cos, sin, dec, xi, zeta, cd, nh, name):
    tp = proj.shape[0]
    steps = tp // (SCAN_CHUNKS * CHUNK)
    kscale = HEAD_DIM ** -0.5

    def body(q_ref, k_ref, v_ref, cos_ref, sin_ref, dec_ref, xi_ref, zeta_ref, cd_ref, o_ref, st_ref, s_sc):
        @pl.when(pl.program_id(0) == 0)
        def _():
            s_sc[...] = jnp.zeros_like(s_sc)

        heads = range(nh)
        states = [s_sc[h] for h in heads]
        for h in heads:
            st_ref[0, h] = states[h]
        outs, new_states = _ret_chunks(
            states, _rotated(q_ref, cos_ref, sin_ref, nh), _rotated(k_ref, cos_ref, sin_ref, nh, kscale),
            _chunk_heads(v_ref, nh), [dec_ref[h] for h in heads], [xi_ref[h] for h in heads],
            [zeta_ref[h] for h in heads], [cd_ref[h][0:1, :] for h in heads])
        _store_chunk_heads(o_ref, outs, nh)
        for h in heads:
            s_sc[h] = new_states[h]

    cols, st, act = _scan_specs(nh, steps, False, 3)
    return pl.pallas_call(
        body, name=name, grid=(steps,), in_specs=cols + _ret_table_specs(nh, steps, False), out_specs=[act, st],
        out_shape=[jax.ShapeDtypeStruct((tp, nh * HEAD_DIM), F32),
                   jax.ShapeDtypeStruct((steps, nh, HEAD_DIM, HEAD_DIM), F32)],
        scratch_shapes=[pltpu.VMEM((nh, HEAD_DIM, HEAD_DIM), F32)],
        compiler_params=_params("arbitrary"))(proj, proj, proj, cos, sin, dec, xi, zeta, cd)


def _ret_bwd(proj, cos, sin, dec, xi, zeta, cd, states, do, nh, name):
    tp = proj.shape[0]
    steps = tp // (SCAN_CHUNKS * CHUNK)
    kscale = HEAD_DIM ** -0.5

    def body(q_ref, k_ref, v_ref, cos_ref, sin_ref, dec_ref, xi_ref, zeta_ref, cd_ref, st_ref, do_ref,
             dq_ref, dk_ref, dv_ref, ds_sc):
        @pl.when(pl.program_id(0) == 0)
        def _():
            ds_sc[...] = jnp.zeros_like(ds_sc)

        heads = range(nh)
        fn = functools.partial(_ret_chunks, dec=[dec_ref[h] for h in heads], xi=[xi_ref[h] for h in heads],
                               zeta=[zeta_ref[h] for h in heads], cd=[cd_ref[h][0:1, :] for h in heads])
        _, vjp = jax.vjp(fn, [st_ref[0, h] for h in heads], _rotated(q_ref, cos_ref, sin_ref, nh),
                         _rotated(k_ref, cos_ref, sin_ref, nh, kscale), _chunk_heads(v_ref, nh))
        ds, dq, dk, dv = vjp((_chunk_heads(do_ref, nh), [ds_sc[h] for h in heads]))
        for h in heads:
            ds_sc[h] = ds[h]
        for i in range(SCAN_CHUNKS * nh):
            rows = slice((i // nh) * CHUNK, (i // nh + 1) * CHUNK)
            cos_t, sin_t = cos_ref[rows, :], sin_ref[rows, :]
            dq[i] = _rot_t(dq[i], cos_t, sin_t)
            dk[i] = _rot_t(dk[i] * kscale, cos_t, sin_t)
        _store_chunk_heads(dq_ref, dq, nh, BF16)
        _store_chunk_heads(dk_ref, dk, nh, BF16)
        _store_chunk_heads(dv_ref, dv, nh, BF16)

    cols, st, act = _scan_specs(nh, steps, True, 3)
    return pl.pallas_call(
        body, name=name, grid=(steps,), in_specs=cols + _ret_table_specs(nh, steps, True) + [st, act],
        out_specs=[act, act, act],
        out_shape=[jax.ShapeDtypeStruct((tp, nh * HEAD_DIM), BF16)] * 3,
        scratch_shapes=[pltpu.VMEM((nh, HEAD_DIM, HEAD_DIM), F32)],
        compiler_params=_params("arbitrary"))(proj, proj, proj, cos, sin, dec, xi, zeta, cd, states, do)


def _gdn_out(o, z, gnorm):
    return o * lax.rsqrt(jnp.mean(o * o, axis=-1, keepdims=True) + EPS) * gnorm * _silu(z)


def _ret_out(o, rg, rnorm):
    mu = jnp.mean(o, axis=-1, keepdims=True)
    var = jnp.mean(jnp.square(o - mu), axis=-1, keepdims=True)
    return _silu(rg) * ((o - mu) * lax.rsqrt(var + EPS) * rnorm)


def _dsilu(x, sg):
    return sg * (1.0 + x * (1.0 - sg))


def _gdn_out_bwd(o, z, gnorm, dy):
    r = lax.rsqrt(jnp.mean(o * o, axis=-1, keepdims=True) + EPS)
    xh = o * r
    sg = jax.nn.sigmoid(z)
    sz = z * sg
    t = dy * (gnorm * sz)
    do = r * (t - xh * jnp.mean(t * xh, axis=-1, keepdims=True))
    e = dy * xh
    return do, e * (gnorm * _dsilu(z, sg)), jnp.sum(e * sz, axis=0, keepdims=True)


def _ret_out_bwd(o, rg, rnorm, dy):
    oc = o - jnp.mean(o, axis=-1, keepdims=True)
    rs = lax.rsqrt(jnp.mean(oc * oc, axis=-1, keepdims=True) + EPS)
    xh = oc * rs
    sg = jax.nn.sigmoid(rg)
    srg = rg * sg
    t = dy * (rnorm * srg)
    do = rs * (t - jnp.mean(t, axis=-1, keepdims=True) - xh * jnp.mean(t * xh, axis=-1, keepdims=True))
    e = dy * xh
    return do, e * (rnorm * _dsilu(rg, sg)), jnp.sum(e * srg, axis=0, keepdims=True)


def _post_specs(tm, hv, d):
    row = lambda col: pl.BlockSpec((tm, hv), lambda i: (i, col))
    return dict(
        oa=row(0), ob=row(0), z=row(6), rg=row(7), ga=row(8), gb=row(9),
        gnorm=pl.BlockSpec((1, HEAD_DIM), lambda i: (0, 0)), rnorm=pl.BlockSpec((1, hv), lambda i: (0, 0)),
        w=pl.BlockSpec((hv, d), lambda i: (0, 0)), res=pl.BlockSpec((tm, d), lambda i: (i, 0)))


def _post_fwd(oa, ob, proj, gnorm, rnorm, wbg, wbr, wo, h1, name):
    tp, d = h1.shape
    hv = oa.shape[1]
    nh = hv // HEAD_DIM
    tm = _tile(tp, 256, 8)

    def body(oa_ref, ob_ref, z_ref, rg_ref, ga_ref, gb_ref, gn_ref, rn_ref, wbg_ref, wbr_ref, wo_ref, h_ref,
             o_ref, ya_sc, yb_sc):
        for h in range(nh):
            sl = slice(h * HEAD_DIM, (h + 1) * HEAD_DIM)
            ya_sc[:, sl] = _gdn_out(oa_ref[:, sl], z_ref[:, sl], gn_ref[...]).astype(BF16)
            yb_sc[:, sl] = _ret_out(ob_ref[:, sl], rg_ref[:, sl], rn_ref[:, sl]).astype(BF16)
        pa = jnp.dot(ya_sc[...], wbg_ref[...], preferred_element_type=F32)
        pb = jnp.dot(yb_sc[...], wbr_ref[...], preferred_element_type=F32)
        merged = jax.nn.sigmoid(ga_ref[...]) * pa + jax.nn.sigmoid(gb_ref[...]) * pb
        o_ref[...] = h_ref[...] + jnp.dot(merged.astype(BF16), wo_ref[...], preferred_element_type=F32)

    sp = _post_specs(tm, hv, d)
    return pl.pallas_call(
        body, name=name, grid=(tp // tm,),
        in_specs=[sp["oa"], sp["ob"], sp["z"], sp["rg"], sp["ga"], sp["gb"], sp["gnorm"], sp["rnorm"],
                  sp["w"], sp["w"], sp["w"], sp["res"]],
        out_specs=sp["res"], out_shape=jax.ShapeDtypeStruct((tp, d), F32),
        scratch_shapes=[pltpu.VMEM((tm, hv), BF16), pltpu.VMEM((tm, hv), BF16)],
        compiler_params=_params("parallel"))(oa, ob, proj, proj, proj, proj, gnorm, rnorm, wbg, wbr, wo, h1)


def _post_bwd(oa, ob, proj, gnorm, rnorm, wbg, wbr, wo, dh2, name):
    tp, d = dh2.shape
    hv = oa.shape[1]
    nh = hv // HEAD_DIM
    tm = _tile(tp, 256, 8)

    def body(oa_ref, ob_ref, z_ref, rg_ref, ga_ref, gb_ref, gn_ref, rn_ref, wbg_ref, wbr_ref, wo_ref, dh_ref,
             doa_ref, dob_ref, dg_ref, ya_ref, yb_ref, mg_ref, dpa_ref, dpb_ref, dgn_ref, drn_ref,
             dya_sc, dyb_sc):
        @pl.when(pl.program_id(0) == 0)
        def _():
            dgn_ref[...] = jnp.zeros_like(dgn_ref)
            drn_ref[...] = jnp.zeros_like(drn_ref)

        for h in range(nh):
            sl = slice(h * HEAD_DIM, (h + 1) * HEAD_DIM)
            ya_ref[:, sl] = _gdn_out(oa_ref[:, sl], z_ref[:, sl], gn_ref[...]).astype(BF16)
            yb_ref[:, sl] = _ret_out(ob_ref[:, sl], rg_ref[:, sl], rn_ref[:, sl]).astype(BF16)
        pa = jnp.dot(ya_ref[...], wbg_ref[...], preferred_element_type=F32)
        pb = jnp.dot(yb_ref[...], wbr_ref[...], preferred_element_type=F32)
        sa = jax.nn.sigmoid(ga_ref[...])
        sb = jax.nn.sigmoid(gb_ref[...])
        mg_ref[...] = (sa * pa + sb * pb).astype(BF16)
        dm = lax.dot_general(dh_ref[...].astype(BF16), wo_ref[...], NT, preferred_element_type=F32)
        dpa = (dm * sa).astype(BF16)
        dpb = (dm * sb).astype(BF16)
        dpa_ref[...] = dpa
        dpb_ref[...] = dpb
        dg_ref[:, 2 * hv:3 * hv] = (dm * pa * sa * (1.0 - sa)).astype(BF16)
        dg_ref[:, 3 * hv:4 * hv] = (dm * pb * sb * (1.0 - sb)).astype(BF16)
        dya_sc[...] = lax.dot_general(dpa, wbg_ref[...], NT, preferred_element_type=F32)
        dyb_sc[...] = lax.dot_general(dpb, wbr_ref[...], NT, preferred_element_type=F32)
        for h in range(nh):
            sl = slice(h * HEAD_DIM, (h + 1) * HEAD_DIM)
            doa, dz, dgn = _gdn_out_bwd(oa_ref[:, sl], z_ref[:, sl], gn_ref[...], dya_sc[:, sl])
            doa_ref[:, sl] = doa
            dg_ref[:, sl] = dz.astype(BF16)
            dgn_ref[...] += dgn
            dob, drg, drn = _ret_out_bwd(ob_ref[:, sl], rg_ref[:, sl], rn_ref[:, sl], dyb_sc[:, sl])
            dob_ref[:, sl] = dob
            dg_ref[:, hv + h * HEAD_DIM:hv + (h + 1) * HEAD_DIM] = drg.astype(BF16)
            drn_ref[:, sl] += drn

    sp = _post_specs(tm, hv, d)
    act = pl.BlockSpec((tm, hv), lambda i: (i, 0))
    return pl.pallas_call(
        body, name=name, grid=(tp // tm,),
        in_specs=[sp["oa"], sp["ob"], sp["z"], sp["rg"], sp["ga"], sp["gb"], sp["gnorm"], sp["rnorm"],
                  sp["w"], sp["w"], sp["w"], sp["res"]],
        out_specs=[act, act, pl.BlockSpec((tm, 4 * hv), lambda i: (i, 0)), act, act, sp["res"], sp["res"],
                   sp["res"], sp["gnorm"], sp["rnorm"]],
        out_shape=[jax.ShapeDtypeStruct((tp, hv), F32), jax.ShapeDtypeStruct((tp, hv), F32),
                   jax.ShapeDtypeStruct((tp, 4 * hv), BF16), jax.ShapeDtypeStruct((tp, hv), BF16),
                   jax.ShapeDtypeStruct((tp, hv), BF16), jax.ShapeDtypeStruct((tp, d), BF16),
                   jax.ShapeDtypeStruct((tp, d), BF16), jax.ShapeDtypeStruct((tp, d), BF16),
                   jax.ShapeDtypeStruct((1, HEAD_DIM), F32), jax.ShapeDtypeStruct((1, hv), F32)],
        scratch_shapes=[pltpu.VMEM((tm, hv), F32), pltpu.VMEM((tm, hv), F32)],
        compiler_params=_params("arbitrary"))(oa, ob, proj, proj, proj, proj, gnorm, rnorm, wbg, wbr, wo, dh2)


def _final(h3, gain, target, name):
    tp, d = h3.shape
    tm = HEAD_ROWS

    def body(h_ref, g_ref, t_ref, loss_ref, dh_ref, dgain_ref):
        i = pl.program_id(0)

        @pl.when(i == 0)
        def _():
            loss_ref[...] = jnp.zeros_like(loss_ref)
            dgain_ref[...] = jnp.zeros_like(dgain_ref)

        xh, r = _rms_parts(h_ref[...])
        err = jnp.where(i == 0, 0.0, xh * g_ref[...] - t_ref[...])
        dx, dg = _rms_bwd(err * (1.0 / d), xh, r, g_ref[...])
        dh_ref[...] = dx
        dgain_ref[...] += dg
        loss_ref[...] += 0.5 * jnp.sum(jnp.mean(err * err, axis=-1, keepdims=True), axis=0, keepdims=True)

    row = pl.BlockSpec((tm, d), lambda i: (i, 0))
    vec = pl.BlockSpec((1, d), lambda i: (0, 0))
    return pl.pallas_call(
        body, name=name, grid=(tp // tm,),
        in_specs=[row, vec, pl.BlockSpec((tm, d), lambda i: (jnp.maximum(i - 1, 0), 0))],
        out_specs=[pl.BlockSpec((1, LANES), lambda i: (0, 0)), row, vec],
        out_shape=[jax.ShapeDtypeStruct((1, LANES), F32), jax.ShapeDtypeStruct((tp, d), F32),
                   jax.ShapeDtypeStruct((1, d), F32)],
        compiler_params=_params("arbitrary"))(h3, gain, target)


def _peer(k):
    x, y, c = lax.axis_index("x"), lax.axis_index("y"), lax.axis_index("c")
    return (1 - x if k & 4 else x, 1 - y if k & 2 else y, 1 - c if k & 1 else c)


def _my_index():
    return 4 * lax.axis_index("x") + 2 * lax.axis_index("y") + lax.axis_index("c")


def _exchange(bufs, scatter, name):
    n = len(bufs)

    def body(*refs):
        _exchange_copies(refs[:n], refs[n:2 * n], refs[2 * n:], scatter, True, True)

    hbm, out_shape, sems = _exchange_refs(bufs)
    return pl.pallas_call(
        body, name=name, in_specs=hbm, out_specs=hbm, out_shape=out_shape, scratch_shapes=sems,
        compiler_params=pltpu.CompilerParams(has_side_effects=True))(*bufs)


def _gather_via_sibling(bufs, name):
    n = len(bufs)

    def body(*refs):
        x_refs, out_refs = refs[:n], refs[n:2 * n]
        send_sems, recv_sems, local_sems = refs[2 * n:]
        x, y, c = lax.axis_index("x"), lax.axis_index("y"), lax.axis_index("c")
        me, sibling = (x, y, c), (x, y, 1 - c)
        chips = [(1 - x, y), (x, 1 - y), (1 - x, 1 - y)]
        rows = lambda a, dev: out_refs[a].at[4 * dev[0] + 2 * dev[1] + dev[2]]

        def copy(k, a, block, to, src=None):
            return pltpu.make_async_remote_copy(
                src_ref=rows(a, block) if src is None else src, dst_ref=rows(a, block),
                send_sem=send_sems.at[k * n + a], recv_sem=recv_sems.at[k * n + a],
                device_id=to, device_id_type=pl.DeviceIdType.MESH)

        mine = [pltpu.make_async_copy(x_refs[a], rows(a, me), local_sems.at[a]) for a in range(n)]
        first = [copy(0, a, me, sibling, src=x_refs[a]) for a in range(n)]
        first += [copy(1 + j, a, me, (*chip, c), src=x_refs[a]) for j, chip in enumerate(chips) for a in range(n)]
        for cp in mine + first:
            cp.start()
        passed = []
        for j, chip in enumerate(chips):
            for a in range(n):
                copy(1 + j, a, (*chip, c), me).wait_recv()
                passed.append(copy(4 + j, a, (*chip, c), sibling))
                passed[-1].start()
        for a in range(n):
            copy(0, a, sibling, me).wait_recv()
        for j, chip in enumerate(chips):
            for a in range(n):
                copy(4 + j, a, (*chip, 1 - c), me).wait_recv()
        for cp in first + passed:
            cp.wait_send()
        for cp in mine:
            cp.wait()

    hbm, out_shape, sems = _exchange_refs(bufs)
    return pl.pallas_call(
        body, name=name, in_specs=hbm, out_specs=hbm, out_shape=out_shape, scratch_shapes=sems,
        compiler_params=pltpu.CompilerParams(has_side_effects=True))(*bufs)


def _exchange_refs(bufs):
    n = len(bufs)
    return ([pl.BlockSpec(memory_space=pl.ANY)] * n,
            [jax.ShapeDtypeStruct((N_DEV,) + b.shape[-2:], b.dtype) for b in bufs],
            [pltpu.SemaphoreType.DMA(((N_DEV - 1) * n,)), pltpu.SemaphoreType.DMA(((N_DEV - 1) * n,)),
             pltpu.SemaphoreType.DMA((n,))])


def _exchange_copies(x_refs, out_refs, sems, scatter, start, wait):
    n = len(x_refs)
    send_sems, recv_sems, local_sems = sems
    me = _my_index()
    copies = []
    for a in range(n):
        copies.append(pltpu.make_async_copy(x_refs[a].at[me] if scatter else x_refs[a], out_refs[a].at[me],
                                            local_sems.at[a]))
    sends = []
    arrivals = []
    for k in range(1, N_DEV):
        x, y, c = _peer(k)
        peer = 4 * x + 2 * y + c
        for a in range(n):
            sem = (k - 1) * n + a
            sends.append(pltpu.make_async_remote_copy(
                src_ref=x_refs[a].at[peer] if scatter else x_refs[a], dst_ref=out_refs[a].at[me],
                send_sem=send_sems.at[sem], recv_sem=recv_sems.at[sem],
                device_id=(x, y, c), device_id_type=pl.DeviceIdType.MESH))
            if wait:
                landed = out_refs[a].at[peer]
                arrivals.append(pltpu.make_async_remote_copy(
                    src_ref=landed, dst_ref=landed, send_sem=send_sems.at[sem], recv_sem=recv_sems.at[sem],
                    device_id=(x, y, c), device_id_type=pl.DeviceIdType.MESH))
    if start:
        for cp in copies + sends:
            cp.start()
    if wait:
        for cp in arrivals:
            cp.wait_recv()
        for cp in sends:
            cp.wait_send()
        for cp in copies:
            cp.wait()


def _carried_call(body, carry, first, last, *, name, grid, in_specs, out_specs, out_shape, scratch_shapes=()):
    in_specs, out_specs, out_shape = list(in_specs), list(out_specs), list(out_shape)
    semantics = ("arbitrary",) * len(grid)
    if carry is None:
        call = pl.pallas_call(body, name=name, grid=grid, in_specs=in_specs, out_specs=out_specs,
                              out_shape=out_shape, scratch_shapes=list(scratch_shapes),
                              compiler_params=_params(*semantics))
        return lambda *args: (call(*args), [])
    bufs, scatter = carry
    n, n_in, n_out, n_scratch = len(bufs), len(in_specs), len(out_specs), len(scratch_shapes)
    hbm, x_shapes, sems = _exchange_refs(bufs)

    def full_body(*refs):
        ins, x_refs = refs[:n_in], refs[n_in:n_in + n]
        outs, xo_refs = refs[n_in + n:n_in + n + n_out], refs[n_in + n + n_out:n_in + 2 * n + n_out]
        scratch = refs[n_in + 2 * n + n_out:n_in + 2 * n + n_out + n_scratch]
        x_sems = refs[n_in + 2 * n + n_out + n_scratch:]

        @pl.when(first())
        def _():
            _exchange_copies(x_refs, xo_refs, x_sems, scatter, True, False)

        body(*ins, *outs, *scratch)

        @pl.when(last())
        def _():
            _exchange_copies(x_refs, xo_refs, x_sems, scatter, False, True)

    call = pl.pallas_call(full_body, name=name, grid=grid, in_specs=in_specs + hbm, out_specs=out_specs + hbm,
                          out_shape=out_shape + x_shapes, scratch_shapes=list(scratch_shapes) + sems,
                          compiler_params=_params(*semantics))

    def run(*args):
        res = call(*args, *bufs)
        return res[:n_out], res[n_out:]
    return run


def _adamw(w, g, m, v, name):
    r, c = w.shape
    parts = g.ndim == 3
    tr = _tile(r, 256, 16 if parts else 8)
    c1 = 1.0 - ADAM_B1 ** ADAM_STEP
    c2 = 1.0 - ADAM_B2 ** ADAM_STEP

    def body(w_ref, g_ref, m_ref, v_ref, go_ref, d_ref, mo_ref, vo_ref):
        if parts:
            g = g_ref[0].astype(F32)
            for q in range(1, N_DEV):
                g = g + g_ref[q].astype(F32)
        else:
            g = g_ref[...]
        m = ADAM_B1 * m_ref[...] + (1.0 - ADAM_B1) * g
        v = ADAM_B2 * v_ref[...] + (1.0 - ADAM_B2) * (g * g)
        go_ref[...] = g
        d_ref[...] = -ADAM_LR * ((m / c1) / (jnp.sqrt(v / c2) + ADAM_EPS) + ADAM_WD * w_ref[...])
        mo_ref[...] = m
        vo_ref[...] = v

    blk = pl.BlockSpec((tr, c), lambda i: (i, 0))
    g_spec = pl.BlockSpec((N_DEV, tr, c), lambda i: (0, i, 0)) if parts else blk
    return pl.pallas_call(
        body, name=name, grid=(r // tr,), in_specs=[blk, g_spec, blk, blk], out_specs=[blk] * 4,
        out_shape=[jax.ShapeDtypeStruct((r, c), F32)] * 4,
        compiler_params=_params("parallel"))(w, g, m, v)


def _win_segments(hv, nh):
    o_z, o_b = 3 * hv, 4 * hv
    o_r = o_b + 2 * nh
    return [(0, 0, 3 * hv), (3 * hv, o_r, 3 * hv), (6 * hv, o_z, hv), (7 * hv, o_r + 3 * hv, 3 * hv),
            (10 * hv, o_b, 2 * nh)]


def _win_from_shards(shards, hv, nh):
    _, d, cs = shards.shape
    pieces = []
    for _, src, width in _win_segments(hv, nh):
        lo = src
        while lo < src + width:
            p = lo // cs
            hi = min(src + width, (p + 1) * cs)
            pieces.append(shards[p][:, lo - p * cs:hi - p * cs])
            lo = hi
    pieces.append(jnp.zeros((d, LANES - 2 * nh), shards.dtype))
    return jnp.concatenate(pieces, axis=1)


def _win_grad_to_shards(parts, hv, nh, cs):
    segments = _win_segments(hv, nh)
    starts = [sum(p.shape[1] for p in parts[:i]) for i in range(len(parts))]

    def columns(a, b):
        out = []
        for part, start in zip(parts, starts):
            lo, hi = max(a, start), min(b, start + part.shape[1])
            if lo < hi:
                out.append(part[:, lo - start:hi - start])
        return out

    shards = []
    for p in range(N_DEV):
        pieces = []
        lo = p * cs
        while lo < (p + 1) * cs:
            here, src, width = next(s for s in segments if s[1] <= lo < s[1] + s[2])
            hi = min((p + 1) * cs, src + width)
            pieces += columns(here + lo - src, here + hi - src)
            lo = hi
        shards.append(jnp.concatenate(pieces, axis=1))
    return jnp.stack(shards)


def _rope_tables(tp):
    pos = jnp.arange(tp, dtype=F32) - float(PAD_FRONT)
    inv = 1.0 / (ROPE_BASE ** jnp.linspace(0.0, 1.0, HEAD_DIM // 2, dtype=F32))
    ang = pos[:, None] * inv[None, :]
    cos = jnp.repeat(jnp.cos(ang), 2, axis=1)
    sin = jnp.repeat(jnp.sin(ang), 2, axis=1) * jnp.tile(jnp.array([-1.0, 1.0], F32), HEAD_DIM // 2)[None, :]
    return cos, sin


def _retention_tables(nh):
    log_gamma = jnp.log1p(-jnp.exp2(-5.0 - jnp.arange(nh, dtype=F32)))
    pos = jnp.arange(CHUNK, dtype=F32)
    causal = pos[:, None] >= pos[None, :]
    diff = pos[:, None] - pos[None, :]
    dec = jnp.where(causal, jnp.exp(jnp.where(causal, diff, 0.0) * log_gamma[:, None, None]), 0.0)
    ones = jnp.ones((1, 1, HEAD_DIM), F32)
    xi = jnp.exp((pos + 1.0)[None, :] * log_gamma[:, None])[:, :, None] * ones
    zeta = jnp.exp((CHUNK - 1.0 - pos)[None, :] * log_gamma[:, None])[:, :, None] * ones
    cd = jnp.exp(CHUNK * log_gamma)[:, None, None] * jnp.ones((1, 8, HEAD_DIM), F32)
    return dec, xi, zeta, cd


SHARDED = ("meta_tokens", "ffn1_w_in", "ffn1_w_out", "w_in", "gdn_conv_w", "w_branch_gdn", "w_branch_ret",
           "w_out", "ffn2_w_in", "ffn2_w_out")
COLUMN_SHARDED = ("meta_tokens", "ffn1_w_in", "w_in", "gdn_conv_w", "ffn2_w_in")
EXACT_F32 = ("meta_tokens", "gdn_conv_w")
REPLICATED = ("ffn1_norm", "mix_norm", "gdn_a_log", "gdn_dt_bias", "gdn_out_norm", "ret_out_norm", "ffn2_norm",
              "final_norm")
WEIGHTS = ("meta_tokens", "ffn1_norm", "ffn1_w_in", "ffn1_w_out", "mix_norm", "w_in", "gdn_conv_w", "gdn_a_log",
           "gdn_dt_bias", "gdn_out_norm", "ret_out_norm", "w_branch_gdn", "w_branch_ret", "w_out", "ffn2_norm",
           "ffn2_w_in", "ffn2_w_out", "final_norm")


def _as2d(a):
    if a.ndim == 3:
        return a[0]
    if a.ndim == 1:
        return a[None, :]
    return a


def _rows_of(shards):
    return shards.reshape(-1, shards.shape[2])


def _cols_of(shards):
    return shards.transpose(1, 0, 2).reshape(shards.shape[1], -1)


def _row_shards(a):
    return a.reshape(N_DEV, -1, a.shape[1])


def _col_shards(a):
    return a.reshape(a.shape[0], N_DEV, -1).transpose(1, 0, 2)


GATHER_FIRST = ("meta_tokens", "ffn1_w_in", "ffn1_w_out")
GATHER_BEHIND_FFN1 = ("w_in", "gdn_conv_w")
GATHER_BEHIND_PROJ = ("w_branch_gdn", "w_branch_ret", "w_out", "ffn2_w_in", "ffn2_w_out")
SCATTER_BEHIND_DN2 = ("ffn2_w_in", "ffn2_w_out", "w_branch_gdn", "w_branch_ret", "w_out")
SCATTER_BEHIND_FFN1 = ("w_in", "gdn_conv_w")
SCATTER_BEHIND_DWG = ("meta_tokens", "ffn1_w_out")
SCATTER_LAST = ("ffn1_w_in",)


def _device_step(x, target, send, rep):
    seq, d = x.shape
    tp = HEAD_ROWS + seq
    hv = d
    nh = hv // HEAD_DIM
    assert tp % (SCAN_CHUNKS * CHUNK) == 0 and tp % HEAD_ROWS == 0
    bf16_shards = lambda grads, names: [grads[n].astype(BF16) for n in names]

    pad_lanes = lambda row: jnp.pad(row, ((0, 0), (nh, LANES - 2 * nh)))
    alog = pad_lanes(rep["gdn_a_log"])
    dtb = pad_lanes(rep["gdn_dt_bias"])
    cos, sin = _rope_tables(tp)
    dec, xi, zeta, cd = _retention_tables(nh)

    got = dict(zip(GATHER_FIRST, _gather_via_sibling([send[n] for n in GATHER_FIRST], "gather_ffn1")))
    h0 = jnp.concatenate([jnp.zeros((PAD_FRONT, d), F32), _cols_of(got["meta_tokens"]), x], axis=0)
    f1i, f1o = got["ffn1_w_in"], _rows_of(got["ffn1_w_out"])
    (h1, hid1, dup1, dgate1), moved = _ffn_fwd(h0, rep["ffn1_norm"], f1i, f1o, "ffn1_fwd",
                                               ([send[n] for n in GATHER_BEHIND_FFN1], False))
    got.update(zip(GATHER_BEHIND_FFN1, moved))
    wp = _win_from_shards(got["w_in"], hv, nh)
    conv_w = _cols_of(got["gdn_conv_w"])
    (proj, n2), moved = _proj_fwd(h1, rep["mix_norm"], wp, "proj_fwd",
                                  ([send[n] for n in GATHER_BEHIND_PROJ], False))
    got.update(zip(GATHER_BEHIND_PROJ, moved))
    wbg, wbr, wo = _rows_of(got["w_branch_gdn"]), _rows_of(got["w_branch_ret"]), _rows_of(got["w_out"])
    f2i, f2o = got["ffn2_w_in"], _rows_of(got["ffn2_w_out"])
    qkv, conv_out = _conv_fwd(proj, conv_w, hv, "conv_fwd")
    oa, s_gdn, t_gdn = _gdn_fwd(qkv, proj, alog, dtb, nh, "gdn_fwd")
    ob, s_ret = _ret_fwd(proj, cos, sin, dec, xi, zeta, cd, nh, "ret_fwd")
    h2 = _post_fwd(oa, ob, proj, rep["gdn_out_norm"], rep["ret_out_norm"], wbg, wbr, wo, h1, "post_fwd")
    (h3, hid2, dup2, dgate2), _ = _ffn_fwd(h2, rep["ffn2_norm"], f2i, f2o, "ffn2_fwd")
    loss_row, dh3, d_final = _final(h3, rep["final_norm"], target, "final")

    (dh2, d_f2n, n3, dag2, dau2), _ = _ffn_bwd(h2, dh3, rep["ffn2_norm"], f2i, f2o, dup2, dgate2, "ffn2_bwd")
    grads = {"ffn2_w_in": jnp.concatenate([_matmul_tn_blocks(n3, dag2, "ffn2_dwg"),
                                           _matmul_tn_blocks(n3, dau2, "ffn2_dwu")]),
             "ffn2_w_out": _row_shards(_matmul_tn_blocks(hid2, dh3, "ffn2_dwo", 0.5))}

    doa, dob, dgate, ya, yb, merged, dpa, dpb, d_gn, d_rn = _post_bwd(
        oa, ob, proj, rep["gdn_out_norm"], rep["ret_out_norm"], wbg, wbr, wo, dh2, "post_bwd")
    grads["w_branch_gdn"] = _row_shards(_matmul_tn(ya, dpa, "dw_branch_gdn"))
    grads["w_branch_ret"] = _row_shards(_matmul_tn(yb, dpb, "dw_branch_ret"))
    grads["w_out"] = _row_shards(_matmul_tn(merged, dh2, "dw_out"))

    d_ret = _ret_bwd(proj, cos, sin, dec, xi, zeta, cd, s_ret, dob, nh, "ret_bwd")
    gdn_grads = _gdn_bwd(qkv, proj, alog, dtb, s_gdn, t_gdn, doa, nh, "gdn_bwd")
    dba, d_alog, d_dtb = gdn_grads[3:]
    dpre, g_conv = [], []
    for grp, tag in enumerate("qkv"):
        dx, dw = _conv_bwd(proj, conv_out, conv_w, gdn_grads[grp], grp, hv, "conv_bwd_" + tag)
        dpre.append(dx)
        g_conv.append(dw)
    grads["gdn_conv_w"] = _col_shards(jnp.concatenate(g_conv, axis=1))

    wide = dpre + list(d_ret) + [dgate]
    dn2, moved = _matmul_nt_parts(wide, wp[:, :10 * hv], None, "dn2_wide",
                                  (bf16_shards(grads, SCATTER_BEHIND_DN2), True))
    parts = dict(zip(SCATTER_BEHIND_DN2, moved))
    dn2, _ = _matmul_nt_parts([dba], wp[:, 10 * hv:], dn2, "dn2_beta_alpha")
    g_wp = [_matmul_tn(n2, dg, "dw_in_%d" % idx) for idx, dg in enumerate(wide + [dba])]
    grads["w_in"] = _win_grad_to_shards(g_wp, hv, nh, send["w_in"].shape[1])
    dh1, d_mixn = _norm_bwd(h1, rep["mix_norm"], dn2, dh2, "mix_norm_bwd")

    (dh0, d_f1n, n1, dag1, dau1), moved = _ffn_bwd(h0, dh1, rep["ffn1_norm"], f1i, f1o, dup1, dgate1, "ffn1_bwd",
                                                   (bf16_shards(grads, SCATTER_BEHIND_FFN1), True))
    parts.update(zip(SCATTER_BEHIND_FFN1, moved))
    grads["ffn1_w_out"] = _row_shards(_matmul_tn_blocks(hid1, dh1, "ffn1_dwo", 0.5))
    grads["meta_tokens"] = _col_shards(dh0[PAD_FRONT:HEAD_ROWS])
    g_gate, moved = _matmul_tn_blocks(n1, dag1, "ffn1_dwg", carry=(bf16_shards(grads, SCATTER_BEHIND_DWG), True))
    parts.update(zip(SCATTER_BEHIND_DWG, moved))
    grads["ffn1_w_in"] = jnp.concatenate([g_gate, _matmul_tn_blocks(n1, dau1, "ffn1_dwu")])
    parts.update(zip(SCATTER_LAST, _exchange(bf16_shards(grads, SCATTER_LAST), True, "scatter_ffn1")))

    small = {"ffn1_norm": d_f1n, "mix_norm": d_mixn, "gdn_a_log": d_alog[:, nh:2 * nh],
             "gdn_dt_bias": d_dtb[:, nh:2 * nh], "gdn_out_norm": d_gn, "ret_out_norm": d_rn, "ffn2_norm": d_f2n,
             "final_norm": d_final}
    return loss_row[0, 0], dh0[HEAD_ROWS:], parts, small


def kernel(x, meta_tokens, ffn1_norm, ffn1_w_in, ffn1_w_out, mix_norm, w_in, gdn_conv_w, gdn_a_log, gdn_dt_bias, gdn_out_norm, ret_out_norm, w_branch_gdn, w_branch_ret, w_out, ffn2_norm, ffn2_w_in, ffn2_w_out, final_norm, loss_target, m_meta_tokens, m_ffn1_norm, m_ffn1_w_in, m_ffn1_w_out, m_mix_norm, m_w_in, m_gdn_conv_w, m_gdn_a_log, m_gdn_dt_bias, m_gdn_out_norm, m_ret_out_norm, m_w_branch_gdn, m_w_branch_ret, m_w_out, m_ffn2_norm, m_ffn2_w_in, m_ffn2_w_out, m_final_norm, v_meta_tokens, v_ffn1_norm, v_ffn1_w_in, v_ffn1_w_out, v_mix_norm, v_w_in, v_gdn_conv_w, v_gdn_a_log, v_gdn_dt_bias, v_gdn_out_norm, v_ret_out_norm, v_w_branch_gdn, v_w_branch_ret, v_w_out, v_ffn2_norm, v_ffn2_w_in, v_ffn2_w_out, v_final_norm):
    given = dict(locals())
    params = {n: _as2d(given[n]) for n in WEIGHTS}
    local = {n: params[n] for n in SHARDED}
    rep = {n: params[n] for n in REPLICATED}

    send = {n: local[n] if n in EXACT_F32 else local[n].astype(BF16) for n in SHARDED}
    loss_sum, grad_x, parts, small = _device_step(x[0], loss_target[0], send, rep)
    parts.update(zip(REPLICATED, _exchange([small[n] for n in REPLICATED], False, "gather_small_grads")))
    loss = lax.psum(loss_sum, ("x", "y", "c"))

    outs = {}
    for n in WEIGHTS:
        res = _adamw(params[n], parts[n], _as2d(given["m_" + n]), _as2d(given["v_" + n]), "adamw_" + n)
        outs[n] = [r.reshape(given[n].shape) for r in res]
    return (loss, grad_x[None], *[outs[n][0] for n in WEIGHTS], *[outs[n][1] for n in WEIGHTS],
            *[outs[n][2] for n in WEIGHTS], *[outs[n][3] for n in WEIGHTS])
```

```python
import functools
import math

import numpy as np
import jax
import jax.numpy as jnp
from jax import lax
from jax.experimental import pallas as pl
from jax.experimental.pallas import tpu as pltpu

F32 = jnp.float32
BF16 = jnp.bfloat16

N_DEV = 8
N_META = 16
CHUNK = 64
HEAD_DIM = 128
CONV_K = 4
ROPE_BASE = 10000.0
EPS = 1e-6
PAD_FRONT = 240
HEAD_ROWS = PAD_FRONT + N_META
LANES = 128
VMEM_LIMIT_BYTES = 56 * 1024 * 1024

ADAM_LR = 0.001
ADAM_B1 = 0.9
ADAM_B2 = 0.999
ADAM_EPS = 1e-08
ADAM_WD = 0.01
ADAM_STEP = 10

NN = (((1,), (0,)), ((), ()))
NT = (((1,), (1,)), ((), ()))
TN = (((0,), (0,)), ((), ()))


def _tile(n, target, mult):
    best = 0
    for t in range(mult, min(n, target) + 1, mult):
        if n % t == 0:
            best = t
    return best if best else n


def _params(*semantics):
    return pltpu.CompilerParams(dimension_semantics=semantics, vmem_limit_bytes=VMEM_LIMIT_BYTES)


def _split(a, pieces):
    out = []
    for _ in range(pieces - 1):
        part = a.astype(BF16)
        out.append(part)
        a = a - part.astype(F32)
    return out + [a.astype(BF16)]


def _raw_dot(a, b, dims, hi):
    dot = lambda x, y: lax.dot_general(x, y, dims, preferred_element_type=F32)
    if hi:
        (a_hi, a_lo), (b_hi, b_lo) = _split(a, 2), _split(b, 2)
        return dot(a_hi, b_hi) + (dot(a_hi, b_lo) + dot(a_lo, b_hi))
    return dot(a.astype(BF16), b.astype(BF16))


def _mask_dot(mask, x, dims):
    mask = mask.astype(BF16)
    hi, mid, lo = [lax.dot_general(mask, p, dims, preferred_element_type=F32) for p in _split(x, 3)]
    return hi + (mid + lo)


@jax.custom_vjp
def _cumsum_rows(x):
    c = x.shape[0]
    tril = lax.broadcasted_iota(jnp.int32, (c, c), 0) >= lax.broadcasted_iota(jnp.int32, (c, c), 1)
    return _mask_dot(tril, x, NN)


def _cumsum_rows_bwd(_, g):
    c = g.shape[0]
    tril = lax.broadcasted_iota(jnp.int32, (c, c), 0) >= lax.broadcasted_iota(jnp.int32, (c, c), 1)
    return (_mask_dot(tril, g, TN),)


_cumsum_rows.defvjp(lambda x: (_cumsum_rows(x), None), _cumsum_rows_bwd)


def _unit_lower_inverses(xs):
    c = xs[0].shape[0]
    eye = (lax.broadcasted_iota(jnp.int32, (c, c), 0) == lax.broadcasted_iota(jnp.int32, (c, c), 1)).astype(F32)
    t_inv = [eye + x for x in xs]
    for _ in range(int(math.log2(c)) - 1):
        xs = [_raw_dot(x, x, NN, True) for x in xs]
        t_inv = [t + _raw_dot(t, x, NN, True) for t, x in zip(t_inv, xs)]
    return t_inv


@jax.custom_vjp
def _known_inverse(x_neg, t_inv):
    return t_inv


_known_inverse.defvjp(
    lambda x_neg, t_inv: (t_inv, t_inv),
    lambda t_inv, g: (_raw_dot(_raw_dot(t_inv, g, TN, False), t_inv, NT, False), jnp.zeros_like(t_inv)))


def _make_mm(hi):
    @jax.custom_vjp
    def nn(a, b):
        return _raw_dot(a, b, NN, hi)

    @jax.custom_vjp
    def nt(a, b):
        return _raw_dot(a, b, NT, hi)

    @jax.custom_vjp
    def tn(a, b):
        return _raw_dot(a, b, TN, hi)

    nn.defvjp(lambda a, b: (_raw_dot(a, b, NN, hi), (a, b)),
              lambda r, g: (_raw_dot(g, r[1], NT, False), _raw_dot(r[0], g, TN, False)))
    nt.defvjp(lambda a, b: (_raw_dot(a, b, NT, hi), (a, b)),
              lambda r, g: (_raw_dot(g, r[1], NN, False), _raw_dot(g, r[0], TN, False)))
    tn.defvjp(lambda a, b: (_raw_dot(a, b, TN, hi), (a, b)),
              lambda r, g: (_raw_dot(r[1], g, NT, False), _raw_dot(r[0], g, NN, False)))
    return nn, nt, tn


def _silu(x):
    return x * jax.nn.sigmoid(x)


def _rms_parts(x):
    r = lax.rsqrt(jnp.mean(x * x, axis=-1, keepdims=True) + EPS)
    return x * r, r


def _rms_bwd(dy, xh, r, gain):
    dxh = dy * gain
    dx = r * (dxh - xh * jnp.mean(dxh * xh, axis=-1, keepdims=True))
    return dx, jnp.sum(dy * xh, axis=0, keepdims=True)


def _ffn_specs(tm, d, tf, nj):
    return [pl.BlockSpec((tm, d), lambda i, j: (i, 0)), pl.BlockSpec((1, d), lambda i, j: (0, 0)),
            pl.BlockSpec((1, d, tf), lambda i, j: (j, 0, 0)), pl.BlockSpec((1, d, tf), lambda i, j: (nj + j, 0, 0)),
            pl.BlockSpec((tf, d), lambda i, j: (j, 0))]


def _first_step(ndim):
    return lambda: functools.reduce(lambda a, b: a & b, [pl.program_id(k) == 0 for k in range(ndim)])


def _last_step(grid):
    return lambda: functools.reduce(lambda a, b: a & b, [pl.program_id(k) == g - 1 for k, g in enumerate(grid)])


def _ffn_fwd(h, gain, w_in, wo, name, carry=None):
    tp, d = h.shape
    tf = w_in.shape[2]
    nj = w_in.shape[0] // 2
    tm = _tile(tp, 768, 8)
    row, vec, wg_spec, wu_spec, wo_spec = _ffn_specs(tm, d, tf, nj)

    def body(h_ref, g_ref, wg3_ref, wu3_ref, wo_ref, o_ref, hid3_ref, dup3_ref, dgate3_ref, n_sc, acc_sc):
        wg_ref, wu_ref = wg3_ref.at[0], wu3_ref.at[0]
        j = pl.program_id(1)

        @pl.when(j == 0)
        def _():
            xh, _ = _rms_parts(h_ref[...])
            n_sc[...] = (xh * g_ref[...]).astype(BF16)
            acc_sc[...] = jnp.zeros_like(acc_sc)

        n = n_sc[...]
        a_g = jnp.dot(n, wg_ref[...], preferred_element_type=F32)
        a_u = jnp.dot(n, wu_ref[...], preferred_element_type=F32)
        sg = jax.nn.sigmoid(a_g)
        s = a_g * sg
        hid = (s * a_u).astype(BF16)
        hid3_ref[0] = hid
        dup3_ref[0] = s.astype(BF16)
        dgate3_ref[0] = (a_u * _dsilu(a_g, sg)).astype(BF16)
        acc_sc[...] += jnp.dot(hid, wo_ref[...], preferred_element_type=F32)

        @pl.when(j == nj - 1)
        def _():
            o_ref[...] = h_ref[...] + 0.5 * acc_sc[...]

    grid = (tp // tm, nj)
    act = pl.BlockSpec((1, tm, tf), lambda i, j: (j, i, 0))
    return _carried_call(
        body, carry, _first_step(2), _last_step(grid), name=name, grid=grid,
        in_specs=[row, vec, wg_spec, wu_spec, wo_spec], out_specs=[row, act, act, act],
        out_shape=[jax.ShapeDtypeStruct((tp, d), F32)] + [jax.ShapeDtypeStruct((nj, tp, tf), BF16)] * 3,
        scratch_shapes=[pltpu.VMEM((tm, d), BF16), pltpu.VMEM((tm, d), F32)])(h, gain, w_in, w_in, wo)


def _ffn_bwd(h, dho, gain, w_in, wo, dup3, dgate3, name, carry=None):
    tp, d = h.shape
    tf = w_in.shape[2]
    nj = w_in.shape[0] // 2
    tm = _tile(tp, 704, 16)
    ni = tp // tm
    row, vec, wg_spec, wu_spec, wo_spec = _ffn_specs(tm, d, tf, nj)

    def body(h_ref, dho_ref, g_ref, wg3_ref, wu3_ref, wo_ref, dup3_ref, dgate3_ref,
             dh_ref, dgain_ref, n_ref, dag3_ref, dau3_ref, dn_sc, dhb_sc):
        wg_ref, wu_ref = wg3_ref.at[0], wu3_ref.at[0]
        dag_ref, dau_ref = dag3_ref.at[0], dau3_ref.at[0]
        i, j = pl.program_id(0), pl.program_id(1)

        @pl.when(j == 0)
        def _():
            xh, _ = _rms_parts(h_ref[...])
            n_ref[...] = (xh * g_ref[...]).astype(BF16)
            dn_sc[...] = jnp.zeros_like(dn_sc)
            dhb_sc[...] = (0.5 * dho_ref[...]).astype(BF16)

        @pl.when((i == 0) & (j == 0))
        def _():
            dgain_ref[...] = jnp.zeros_like(dgain_ref)

        d_hid = lax.dot_general(dhb_sc[...], wo_ref[...], NT, preferred_element_type=F32)
        d_au = (d_hid * dup3_ref[0].astype(F32)).astype(BF16)
        d_ag = (d_hid * dgate3_ref[0].astype(F32)).astype(BF16)
        dau_ref[...] = d_au
        dag_ref[...] = d_ag
        dn_sc[...] += (lax.dot_general(d_ag, wg_ref[...], NT, preferred_element_type=F32)
                       + lax.dot_general(d_au, wu_ref[...], NT, preferred_element_type=F32))

        @pl.when(j == nj - 1)
        def _():
            xh, r = _rms_parts(h_ref[...])
            dx, dg = _rms_bwd(dn_sc[...], xh, r, g_ref[...])
            dh_ref[...] = dho_ref[...] + dx
            dgain_ref[...] += dg

    act = pl.BlockSpec((1, tm, tf), lambda i, j: (j, i, 0))
    return _carried_call(
        body, carry, _first_step(2), _last_step((ni, nj)), name=name, grid=(ni, nj),
        in_specs=[row, row, vec, wg_spec, wu_spec, wo_spec, act, act],
        out_specs=[row, vec, row, act, act],
        out_shape=[jax.ShapeDtypeStruct((tp, d), F32), jax.ShapeDtypeStruct((1, d), F32),
                   jax.ShapeDtypeStruct((tp, d), BF16)] + [jax.ShapeDtypeStruct((nj, tp, tf), BF16)] * 2,
        scratch_shapes=[pltpu.VMEM((tm, d), F32), pltpu.VMEM((tm, d), BF16)])(
            h, dho, gain, w_in, w_in, wo, dup3, dgate3)


def _matmul_tn(a, b, name, scale=1.0):
    t, m = a.shape
    n = b.shape[1]
    bm = _tile(m, 1024, LANES)
    bn = _tile(n, 1536, LANES)
    tk = _tile(t, 2816, 16)
    nk = t // tk

    def body(a_ref, b_ref, o_ref):
        k = pl.program_id(2)

        @pl.when(k == 0)
        def _():
            o_ref[...] = jnp.zeros_like(o_ref)

        o_ref[...] += lax.dot_general(a_ref[...].astype(BF16), b_ref[...].astype(BF16), TN,
                                      preferred_element_type=F32)

        if scale != 1.0:
            @pl.when(k == nk - 1)
            def _():
                o_ref[...] = o_ref[...] * scale

    return pl.pallas_call(
        body, name=name, grid=(m // bm, n // bn, nk),
        in_specs=[pl.BlockSpec((tk, bm), lambda i, j, k: (k, i)), pl.BlockSpec((tk, bn), lambda i, j, k: (k, j))],
        out_specs=pl.BlockSpec((bm, bn), lambda i, j, k: (i, j)),
        out_shape=jax.ShapeDtypeStruct((m, n), F32),
        compiler_params=_params("parallel", "parallel", "arbitrary"))(a, b)


def _matmul_tn_blocks(a, b, name, scale=1.0, carry=None):
    a_blocked = a.ndim == 3
    nb, t = (a.shape[0], a.shape[1]) if a_blocked else (b.shape[0], b.shape[1])
    m, n = a.shape[-1], b.shape[-1]
    tk = _tile(t, 2816, 16)
    nk = t // tk
    if a_blocked:
        bo = _tile(n, 1024, LANES)
        a_spec = pl.BlockSpec((1, tk, m), lambda p, o, k: (p, k, 0))
        b_spec = pl.BlockSpec((tk, bo), lambda p, o, k: (k, o))
        o_spec = pl.BlockSpec((m, bo), lambda p, o, k: (p, o))
        out_shape = jax.ShapeDtypeStruct((nb * m, n), F32)
        grid = (nb, n // bo, nk)
    else:
        bo = _tile(m, 1024, LANES)
        a_spec = pl.BlockSpec((tk, bo), lambda p, o, k: (k, o))
        b_spec = pl.BlockSpec((1, tk, n), lambda p, o, k: (p, k, 0))
        o_spec = pl.BlockSpec((1, bo, n), lambda p, o, k: (p, o, 0))
        out_shape = jax.ShapeDtypeStruct((nb, m, n), F32)
        grid = (nb, m // bo, nk)

    def body(a_ref, b_ref, o_ref):
        k = pl.program_id(2)
        a_blk = a_ref[0] if a_blocked else a_ref[...]
        b_blk = b_ref[...] if a_blocked else b_ref[0]
        part = lax.dot_general(a_blk.astype(BF16), b_blk.astype(BF16), TN, preferred_element_type=F32)
        out = o_ref if a_blocked else o_ref.at[0]

        @pl.when(k == 0)
        def _():
            out[...] = part

        @pl.when(k > 0)
        def _():
            out[...] += part

        if scale != 1.0:
            @pl.when(k == nk - 1)
            def _():
                out[...] = out[...] * scale

    (out,), moved = _carried_call(body, carry, _first_step(3), _last_step(grid), name=name, grid=grid,
                                  in_specs=[a_spec, b_spec], out_specs=[o_spec], out_shape=[out_shape])(a, b)
    return out if carry is None else (out, moved)


def _matmul_nt_parts(parts, w, name, carry=None):
    t = parts[0].shape[0]
    d = w.shape[0]
    widths = [p.shape[1] for p in parts]
    tk = _tile(math.gcd(*widths), 1024, LANES)
    counts = [wd // tk for wd in widths]
    starts = [sum(counts[:g]) for g in range(len(parts))]
    nk = sum(counts)
    tm = _tile(t, 768, 8)
    n_parts = len(parts)

    def body(*refs):
        a_refs, w_ref, o_ref = refs[:n_parts], refs[n_parts], refs[-1]
        k = pl.program_id(1)

        @pl.when(k == 0)
        def _():
            o_ref[...] = jnp.zeros_like(o_ref)

        for g in range(n_parts):
            @pl.when((k >= starts[g]) & (k < starts[g] + counts[g]))
            def _(g=g):
                o_ref[...] += lax.dot_general(a_refs[g][...].astype(BF16), w_ref[...], NT,
                                              preferred_element_type=F32)

    in_specs = [pl.BlockSpec((tm, tk), lambda i, k, lo=starts[g], nb=counts[g]: (i, jnp.clip(k - lo, 0, nb - 1)))
                for g in range(n_parts)]
    in_specs.append(pl.BlockSpec((d, tk), lambda i, k: (0, k)))
    args = list(parts) + [w]
    grid = (t // tm, nk)
    (out,), moved = _carried_call(
        body, carry, _first_step(2), _last_step(grid), name=name, grid=grid, in_specs=in_specs,
        out_specs=[pl.BlockSpec((tm, d), lambda i, k: (i, 0))],
        out_shape=[jax.ShapeDtypeStruct((t, d), F32)])(*args)
    return out, moved


def _proj_fwd(h, gain, wp, name, carry=None):
    tp, d = h.shape
    npad = wp.shape[1]
    tm = _tile(tp, 768, 8)
    tn = _tile(npad, 3456, LANES)

    def body(h_ref, g_ref, w_ref, o_ref, n_ref):
        @pl.when(pl.program_id(1) == 0)
        def _():
            xh, _ = _rms_parts(h_ref[...])
            n_ref[...] = (xh * g_ref[...]).astype(BF16)

        o_ref[...] = jnp.dot(n_ref[...], w_ref[...], preferred_element_type=F32)

    grid = (tp // tm, npad // tn)
    return _carried_call(
        body, carry, _first_step(2), _last_step(grid), name=name, grid=grid,
        in_specs=[pl.BlockSpec((tm, d), lambda i, j: (i, 0)), pl.BlockSpec((1, d), lambda i, j: (0, 0)),
                  pl.BlockSpec((d, tn), lambda i, j: (0, j))],
        out_specs=[pl.BlockSpec((tm, tn), lambda i, j: (i, j)), pl.BlockSpec((tm, d), lambda i, j: (i, 0))],
        out_shape=[jax.ShapeDtypeStruct((tp, npad), F32), jax.ShapeDtypeStruct((tp, d), BF16)])(h, gain, wp)


def _norm_bwd(h, gain, dn, last, w_last, dres, name):
    tp, d = h.shape
    kl = last.shape[1]
    tm = _tile(tp, 256, 8)

    def body(h_ref, g_ref, dn_ref, last_ref, w_ref, dres_ref, dh_ref, dgain_ref):
        @pl.when(pl.program_id(0) == 0)
        def _():
            dgain_ref[...] = jnp.zeros_like(dgain_ref)

        dn_all = dn_ref[...] + lax.dot_general(last_ref[...].astype(BF16), w_ref[...], NT,
                                               preferred_element_type=F32)
        xh, r = _rms_parts(h_ref[...])
        dx, dg = _rms_bwd(dn_all, xh, r, g_ref[...])
        dh_ref[...] = dres_ref[...] + dx
        dgain_ref[...] += dg

    row = pl.BlockSpec((tm, d), lambda i: (i, 0))
    vec = pl.BlockSpec((1, d), lambda i: (0, 0))
    return pl.pallas_call(
        body, name=name, grid=(tp // tm,),
        in_specs=[row, vec, row, pl.BlockSpec((tm, kl), lambda i: (i, 0)), pl.BlockSpec((d, kl), lambda i: (0, 0)), row],
        out_specs=[row, vec],
        out_shape=[jax.ShapeDtypeStruct((tp, d), F32), jax.ShapeDtypeStruct((1, d), F32)],
        compiler_params=_params("arbitrary"))(h, gain, dn, last, w_last, dres)


def _head_post(a, grp):
    a = _silu(a)
    r = lax.rsqrt(jnp.sum(a * a, axis=-1, keepdims=True) + EPS)
    if isinstance(grp, int):
        return a if grp == 2 else a * r * (HEAD_DIM ** -0.5 if grp == 0 else 1.0)
    scale = jnp.where(grp == 0, HEAD_DIM ** -0.5, 1.0).astype(F32)
    return jnp.where(grp == 2, a, a * r * scale)


def _head_post_bwd(c, dy, grp):
    sg = jax.nn.sigmoid(c)
    a = c * sg
    dsilu = sg * (1.0 + c * (1.0 - sg))
    if grp == 2:
        return dy * dsilu
    r = lax.rsqrt(jnp.sum(a * a, axis=-1, keepdims=True) + EPS)
    scale = HEAD_DIM ** -0.5 if grp == 0 else 1.0
    da = (scale * r) * (dy - a * (r * r * jnp.sum(dy * a, axis=-1, keepdims=True)))
    return da * dsilu


def _conv_taps(ext_sc, w_ref, tm):
    ext = ext_sc[...]
    c = w_ref[CONV_K - 1:CONV_K, :] * ext[8:, :]
    for i in range(CONV_K - 1):
        s = CONV_K - 1 - i
        c = c + w_ref[i:i + 1, :] * pltpu.roll(ext, s, 0)[8:, :]
    return c


def _conv_fwd(proj, conv_w, hv, name):
    tp = proj.shape[0]
    tm = _tile(tp, 768, 8)
    nh = hv // HEAD_DIM

    def body(x_ref, halo_ref, w_ref, o_ref, c_ref, ext_sc):
        i, grp = pl.program_id(0), pl.program_id(1)
        ext_sc[0:8, :] = jnp.where(i == 0, 0.0, halo_ref[...])
        ext_sc[8:, :] = x_ref[...]
        c_ref[...] = _conv_taps(ext_sc, w_ref, tm)
        for h in range(nh):
            sl = slice(h * HEAD_DIM, (h + 1) * HEAD_DIM)
            o_ref[:, sl] = _head_post(c_ref[:, sl], grp)

    blk = pl.BlockSpec((tm, hv), lambda i, g: (i, g))
    return pl.pallas_call(
        body, name=name, grid=(tp // tm, 3),
        in_specs=[blk, pl.BlockSpec((8, hv), lambda i, g: (jnp.maximum(i * (tm // 8) - 1, 0), g)),
                  pl.BlockSpec((CONV_K, hv), lambda i, g: (0, g))],
        out_specs=[blk, blk],
        out_shape=[jax.ShapeDtypeStruct((tp, 3 * hv), F32)] * 2,
        scratch_shapes=[pltpu.VMEM((tm + 8, hv), F32)],
        compiler_params=_params("parallel", "arbitrary"))(proj, proj, conv_w)


def _conv_bwd(proj, conv_out, conv_w, dy, grp, hv, name):
    tp = proj.shape[0]
    tm = _tile(tp, 768, 8)
    ni = tp // tm
    nh = hv // HEAD_DIM

    def body(x_ref, c_ref, w_ref, dy_ref, dx_ref, dw_ref, dc_sc):
        step = pl.program_id(0)

        @pl.when(step == 0)
        def _():
            dc_sc[tm:, :] = jnp.zeros((8, hv), F32)
            dw_ref[...] = jnp.zeros_like(dw_ref)

        @pl.when(step > 0)
        def _():
            dc_sc[tm:, :] = dc_sc[0:8, :]

        for h in range(nh):
            sl = slice(h * HEAD_DIM, (h + 1) * HEAD_DIM)
            dc_sc[0:tm, sl] = _head_post_bwd(c_ref[:, sl], dy_ref[:, sl], grp)

        x = x_ref[...]
        dc_ext = dc_sc[...]
        dx = None
        for k in range(CONV_K):
            s = CONV_K - 1 - k
            shifted = dc_ext[0:tm, :] if s == 0 else pltpu.roll(dc_ext, tm + 8 - s, 0)[0:tm, :]
            dw_ref[k:k + 1, :] += jnp.sum(shifted * x, axis=0, keepdims=True)
            term = w_ref[k:k + 1, :] * shifted
            dx = term if dx is None else dx + term
        dx_ref[...] = dx.astype(BF16)

    tile = lambda step: ni - 1 - step
    grp_blk = pl.BlockSpec((tm, hv), lambda s: (tile(s), grp))
    own_blk = pl.BlockSpec((tm, hv), lambda s: (tile(s), 0))
    return pl.pallas_call(
        body, name=name, grid=(ni,),
        in_specs=[grp_blk, grp_blk, pl.BlockSpec((CONV_K, hv), lambda s: (0, grp)), own_blk],
        out_specs=[own_blk, pl.BlockSpec((CONV_K, hv), lambda s: (0, 0))],
        out_shape=[jax.ShapeDtypeStruct((tp, hv), BF16), jax.ShapeDtypeStruct((CONV_K, hv), F32)],
        scratch_shapes=[pltpu.VMEM((tm + 8, hv), F32)],
        compiler_params=_params("arbitrary"))(proj, conv_out, conv_w, dy)


def _gdn_gates(ba, alog, dtb):
    x = ba + dtb
    softplus = jnp.maximum(x, 0.0) + jnp.log1p(jnp.exp(-jnp.abs(x)))
    return _cumsum_rows(-jnp.exp(alog) * softplus), jax.nn.sigmoid(ba)


def _gdn_chunks(states, qs, ks, vs, gates, known_inverses=None):
    mm_nn, mm_nt, mm_tn = _make_mm(False)
    hi_nn, _, _ = _make_mm(True)
    nh = len(states)
    items = range(len(qs))
    head = [i % nh for i in items]
    c = qs[0].shape[0]
    lane = lax.broadcasted_iota(jnp.int32, (c, LANES), 1)
    last_row = (lax.broadcasted_iota(jnp.int32, (c, 1), 0) == c - 1).astype(F32)
    ri = lax.broadcasted_iota(jnp.int32, (c, c), 0)
    ci = lax.broadcasted_iota(jnp.int32, (c, c), 1)
    causal = ri >= ci
    strict = ri > ci
    eye = (ri == ci).astype(F32)
    sel_a = [(lane == nh + h).astype(F32) for h in range(nh)]
    sel_b = [(lane == h).astype(F32) for h in range(nh)]

    gcol = [jnp.sum(gates[i // nh][0] * sel_a[head[i]], axis=1, keepdims=True) for i in items]
    grow = [jnp.sum(eye * gcol[i], axis=0, keepdims=True) for i in items]
    beta = [jnp.sum(gates[i // nh][1] * sel_b[head[i]], axis=1, keepdims=True) for i in items]
    decay = [jnp.where(causal, jnp.exp(jnp.where(causal, gcol[i] - grow[i], 0.0)), 0.0) for i in items]
    kb = [ks[i] * beta[i] for i in items]
    kk = [mm_nt(kb[i], ks[i]) for i in items]
    qk = [mm_nt(qs[i], ks[i]) for i in items]
    x_neg = [-jnp.where(strict, kk[i] * decay[i], 0.0) for i in items]
    if known_inverses is None:
        t_inv = _unit_lower_inverses(x_neg)
    else:
        t_inv = [_known_inverse(x_neg[i], known_inverses[i]) for i in items]
    eg = [jnp.exp(gcol[i]) for i in items]
    u = [hi_nn(t_inv[i], vs[i] * beta[i]) for i in items]
    w = [hi_nn(t_inv[i], kb[i] * eg[i]) for i in items]
    qk = [qk[i] * decay[i] for i in items]
    glast = [jnp.sum(gcol[i] * last_row, axis=0, keepdims=True) for i in items]
    q_dec = [qs[i] * eg[i] for i in items]
    k_dec = [ks[i] * jnp.exp(glast[i] - gcol[i]) for i in items]
    s_dec = [jnp.exp(glast[i]) for i in items]

    outs = []
    for first in range(0, len(qs), nh):
        chunk = range(first, first + nh)
        ws = [mm_nn(w[i], states[i - first]) for i in chunk]
        from_state = [mm_nn(q_dec[i], states[i - first]) for i in chunk]
        v_new = [u[i] - ws[i - first] for i in chunk]
        intra = [mm_nn(qk[i], v_new[i - first]) for i in chunk]
        kv = [mm_tn(k_dec[i], v_new[i - first]) for i in chunk]
        outs += [from_state[i - first] + intra[i - first] for i in chunk]
        states = [states[i - first] * s_dec[i] + kv[i - first] for i in chunk]
    return outs, states, t_inv


SCAN_CHUNKS = 4


def _scan_specs(nh, steps, rev, first_col):
    sidx = (lambda s: steps - 1 - s) if rev else (lambda s: s)
    hv = nh * HEAD_DIM
    rows = SCAN_CHUNKS * CHUNK
    cols = [pl.BlockSpec((rows, hv), lambda s, g=g: (sidx(s), first_col + g)) for g in range(3)]
    st = pl.BlockSpec((1, nh, HEAD_DIM, HEAD_DIM), lambda s: (sidx(s), 0, 0, 0))
    act = pl.BlockSpec((rows, hv), lambda s: (sidx(s), 0))
    return cols, st, act


def _chunk_heads(ref, nh):
    return [ref[j * CHUNK:(j + 1) * CHUNK, h * HEAD_DIM:(h + 1) * HEAD_DIM] for j in range(SCAN_CHUNKS)
            for h in range(nh)]


def _store_chunk_heads(ref, values, nh, dtype=None):
    for i, val in enumerate(values):
        j, h = divmod(i, nh)
        ref[j * CHUNK:(j + 1) * CHUNK, h * HEAD_DIM:(h + 1) * HEAD_DIM] = val if dtype is None else val.astype(dtype)


def _gdn_fwd(qkv, proj, alog, dtb, nh, name):
    tp = qkv.shape[0]
    steps = tp // (SCAN_CHUNKS * CHUNK)
    rows = SCAN_CHUNKS * CHUNK

    def body(q_ref, k_ref, v_ref, ba_ref, al_ref, dt_ref, o_ref, st_ref, inv_ref, s_sc):
        @pl.when(pl.program_id(0) == 0)
        def _():
            s_sc[...] = jnp.zeros_like(s_sc)

        gates = [_gdn_gates(ba_ref[j * CHUNK:(j + 1) * CHUNK, :], al_ref[...], dt_ref[...])
                 for j in range(SCAN_CHUNKS)]
        states = [s_sc[h] for h in range(nh)]
        for h in range(nh):
            st_ref[0, h] = states[h]
        outs, new_states, t_inv = _gdn_chunks(states, _chunk_heads(q_ref, nh), _chunk_heads(k_ref, nh),
                                              _chunk_heads(v_ref, nh), gates)
        _store_chunk_heads(o_ref, outs, nh)
        for h in range(nh):
            s_sc[h] = new_states[h]
        for i, t in enumerate(t_inv):
            inv_ref[0, i] = t

    cols, st, act = _scan_specs(nh, steps, False, 0)
    ba = pl.BlockSpec((rows, LANES), lambda s: (s, 10 * nh * HEAD_DIM // LANES))
    vec = pl.BlockSpec((1, LANES), lambda s: (0, 0))
    inv = pl.BlockSpec((1, SCAN_CHUNKS * nh, CHUNK, CHUNK), lambda s: (s, 0, 0, 0))
    return pl.pallas_call(
        body, name=name, grid=(steps,), in_specs=cols + [ba, vec, vec], out_specs=[act, st, inv],
        out_shape=[jax.ShapeDtypeStruct((tp, nh * HEAD_DIM), F32),
                   jax.ShapeDtypeStruct((steps, nh, HEAD_DIM, HEAD_DIM), F32),
                   jax.ShapeDtypeStruct((steps, SCAN_CHUNKS * nh, CHUNK, CHUNK), F32)],
        scratch_shapes=[pltpu.VMEM((nh, HEAD_DIM, HEAD_DIM), F32)],
        compiler_params=_params("arbitrary"))(qkv, qkv, qkv, proj, alog, dtb)


def _gdn_bwd(qkv, proj, alog, dtb, states, inverses, do, nh, name):
    tp = qkv.shape[0]
    steps = tp // (SCAN_CHUNKS * CHUNK)
    rows = SCAN_CHUNKS * CHUNK

    def body(q_ref, k_ref, v_ref, ba_ref, al_ref, dt_ref, st_ref, inv_ref, do_ref,
             dq_ref, dk_ref, dv_ref, dba_ref, dal_ref, ddt_ref, ds_sc):
        @pl.when(pl.program_id(0) == 0)
        def _():
            ds_sc[...] = jnp.zeros_like(ds_sc)
            dal_ref[...] = jnp.zeros_like(dal_ref)
            ddt_ref[...] = jnp.zeros_like(ddt_ref)

        gates, gates_vjps = [], []
        for j in range(SCAN_CHUNKS):
            g, g_vjp = jax.vjp(_gdn_gates, ba_ref[j * CHUNK:(j + 1) * CHUNK, :], al_ref[...], dt_ref[...])
            gates.append(g)
            gates_vjps.append(g_vjp)
        known = [inv_ref[0, i] for i in range(SCAN_CHUNKS * nh)]
        fn = lambda s, q, k, v, g: _gdn_chunks(s, q, k, v, g, known)[:2]
        _, vjp = jax.vjp(fn, [st_ref[0, h] for h in range(nh)], _chunk_heads(q_ref, nh), _chunk_heads(k_ref, nh),
                         _chunk_heads(v_ref, nh), gates)
        ds, dq, dk, dv, dgates = vjp((_chunk_heads(do_ref, nh), [ds_sc[h] for h in range(nh)]))
        for h in range(nh):
            ds_sc[h] = ds[h]
        _store_chunk_heads(dq_ref, dq, nh)
        _store_chunk_heads(dk_ref, dk, nh)
        _store_chunk_heads(dv_ref, dv, nh)
        for j in range(SCAN_CHUNKS):
            dba, dal, ddt = gates_vjps[j](dgates[j])
            dba_ref[j * CHUNK:(j + 1) * CHUNK, :] = dba
            dal_ref[...] += dal
            ddt_ref[...] += ddt

    cols, st, act = _scan_specs(nh, steps, True, 0)
    ba = pl.BlockSpec((rows, LANES), lambda s: (steps - 1 - s, 10 * nh * HEAD_DIM // LANES))
    vec = pl.BlockSpec((1, LANES), lambda s: (0, 0))
    inv = pl.BlockSpec((1, SCAN_CHUNKS * nh, CHUNK, CHUNK), lambda s: (steps - 1 - s, 0, 0, 0))
    return pl.pallas_call(
        body, name=name, grid=(steps,), in_specs=cols + [ba, vec, vec, st, inv, act],
        out_specs=[act, act, act, pl.BlockSpec((rows, LANES), lambda s: (steps - 1 - s, 0)), vec, vec],
        out_shape=[jax.ShapeDtypeStruct((tp, nh * HEAD_DIM), F32)] * 3
                  + [jax.ShapeDtypeStruct((tp, LANES), F32), jax.ShapeDtypeStruct((1, LANES), F32),
                     jax.ShapeDtypeStruct((1, LANES), F32)],
        scratch_shapes=[pltpu.VMEM((nh, HEAD_DIM, HEAD_DIM), F32)],
        compiler_params=_params("arbitrary"))(qkv, qkv, qkv, proj, alog, dtb, states, inverses, do)


def _swap_pairs(t):
    lane = lax.broadcasted_iota(jnp.int32, t.shape, 1)
    n = t.shape[1]
    return jnp.where(lane % 2 == 0, pltpu.roll(t, n - 1, 1), pltpu.roll(t, 1, 1))


def _rot(t, cos, sin_signed):
    return t * cos + _swap_pairs(t) * sin_signed


def _rot_t(dt, cos, sin_signed):
    return dt * cos + _swap_pairs(dt * sin_signed)


def _ret_chunks(states, qs, ks, vs, dec, xi, zeta, cd):
    mm_nn, mm_nt, mm_tn = _make_mm(False)
    nh = len(states)
    items = range(len(qs))
    scores = [mm_nt(qs[i], ks[i]) for i in items]
    kv = [mm_tn(ks[i] * zeta[i % nh], vs[i]) for i in items]
    intra = [mm_nn(scores[i] * dec[i % nh], vs[i]) for i in items]
    q_dec = [qs[i] * xi[i % nh] for i in items]
    outs = []
    for first in range(0, len(qs), nh):
        outs += [intra[first + h] + mm_nn(q_dec[first + h], states[h]) for h in range(nh)]
        states = [states[h] * cd[h] + kv[first + h] for h in range(nh)]
    return outs, states


def _ret_table_specs(nh, steps, rev):
    sidx = (lambda s: steps - 1 - s) if rev else (lambda s: s)
    rope = pl.BlockSpec((SCAN_CHUNKS * CHUNK, HEAD_DIM), lambda s: (sidx(s), 0))
    dec = pl.BlockSpec((nh, CHUNK, CHUNK), lambda s: (0, 0, 0))
    tab = pl.BlockSpec((nh, CHUNK, HEAD_DIM), lambda s: (0, 0, 0))
    cd = pl.BlockSpec((nh, 8, HEAD_DIM), lambda s: (0, 0, 0))
    return [rope, rope, dec, tab, tab, cd]


def _rotated(ref, cos_ref, sin_ref, nh, scale=1.0):
    out = []
    for j in range(SCAN_CHUNKS):
        rows = slice(j * CHUNK, (j + 1) * CHUNK)
        cos_t, sin_t = cos_ref[rows, :], sin_ref[rows, :]
        for h in range(nh):
            t = _rot(ref[rows, h * HEAD_DIM:(h + 1) * HEAD_DIM], cos_t, sin_t)
            out.append(t if scale == 1.0 else t * scale)
    return out


def _ret_fwd(proj, cos, sin, dec, xi, zeta, cd, nh, name):
    tp = proj.shape[0]
    steps = tp // (SCAN_CHUNKS * CHUNK)
    kscale = HEAD_DIM ** -0.5

    def body(q_ref, k_ref, v_ref, cos_ref, sin_ref, dec_ref, xi_ref, zeta_ref, cd_ref, o_ref, st_ref, s_sc):
        @pl.when(pl.program_id(0) == 0)
        def _():
            s_sc[...] = jnp.zeros_like(s_sc)

        heads = range(nh)
        states = [s_sc[h] for h in heads]
        for h in heads:
            st_ref[0, h] = states[h]
        outs, new_states = _ret_chunks(
            states, _rotated(q_ref, cos_ref, sin_ref, nh), _rotated(k_ref, cos_ref, sin_ref, nh, kscale),
            _chunk_heads(v_ref, nh), [dec_ref[h] for h in heads], [xi_ref[h] for h in heads],
            [zeta_ref[h] for h in heads], [cd_ref[h][0:1, :] for h in heads])
        _store_chunk_heads(o_ref, outs, nh)
        for h in heads:
            s_sc[h] = new_states[h]

    cols, st, act = _scan_specs(nh, steps, False, 3)
    return pl.pallas_call(
        body, name=name, grid=(steps,), in_specs=cols + _ret_table_specs(nh, steps, False), out_specs=[act, st],
        out_shape=[jax.ShapeDtypeStruct((tp, nh * HEAD_DIM), F32),
                   jax.ShapeDtypeStruct((steps, nh, HEAD_DIM, HEAD_DIM), F32)],
        scratch_shapes=[pltpu.VMEM((nh, HEAD_DIM, HEAD_DIM), F32)],
        compiler_params=_params("arbitrary"))(proj, proj, proj, cos, sin, dec, xi, zeta, cd)


def _ret_bwd(proj, cos, sin, dec, xi, zeta, cd, states, do, nh, name):
    tp = proj.shape[0]
    steps = tp // (SCAN_CHUNKS * CHUNK)
    kscale = HEAD_DIM ** -0.5

    def body(q_ref, k_ref, v_ref, cos_ref, sin_ref, dec_ref, xi_ref, zeta_ref, cd_ref, st_ref, do_ref,
             dq_ref, dk_ref, dv_ref, ds_sc):
        @pl.when(pl.program_id(0) == 0)
        def _():
            ds_sc[...] = jnp.zeros_like(ds_sc)

        heads = range(nh)
        fn = functools.partial(_ret_chunks, dec=[dec_ref[h] for h in heads], xi=[xi_ref[h] for h in heads],
                               zeta=[zeta_ref[h] for h in heads], cd=[cd_ref[h][0:1, :] for h in heads])
        _, vjp = jax.vjp(fn, [st_ref[0, h] for h in heads], _rotated(q_ref, cos_ref, sin_ref, nh),
                         _rotated(k_ref, cos_ref, sin_ref, nh, kscale), _chunk_heads(v_ref, nh))
        ds, dq, dk, dv = vjp((_chunk_heads(do_ref, nh), [ds_sc[h] for h in heads]))
        for h in heads:
            ds_sc[h] = ds[h]
        for i in range(SCAN_CHUNKS * nh):
            rows = slice((i // nh) * CHUNK, (i // nh + 1) * CHUNK)
            cos_t, sin_t = cos_ref[rows, :], sin_ref[rows, :]
            dq[i] = _rot_t(dq[i], cos_t, sin_t)
            dk[i] = _rot_t(dk[i] * kscale, cos_t, sin_t)
        _store_chunk_heads(dq_ref, dq, nh, BF16)
        _store_chunk_heads(dk_ref, dk, nh, BF16)
        _store_chunk_heads(dv_ref, dv, nh, BF16)

    cols, st, act = _scan_specs(nh, steps, True, 3)
    return pl.pallas_call(
        body, name=name, grid=(steps,), in_specs=cols + _ret_table_specs(nh, steps, True) + [st, act],
        out_specs=[act, act, act],
        out_shape=[jax.ShapeDtypeStruct((tp, nh * HEAD_DIM), BF16)] * 3,
        scratch_shapes=[pltpu.VMEM((nh, HEAD_DIM, HEAD_DIM), F32)],
        compiler_params=_params("arbitrary"))(proj, proj, proj, cos, sin, dec, xi, zeta, cd, states, do)


def _gdn_out(o, z, gnorm):
    return o * lax.rsqrt(jnp.mean(o * o, axis=-1, keepdims=True) + EPS) * gnorm * _silu(z)


def _ret_out(o, rg, rnorm):
    mu = jnp.mean(o, axis=-1, keepdims=True)
    var = jnp.mean(jnp.square(o - mu), axis=-1, keepdims=True)
    return _silu(rg) * ((o - mu) * lax.rsqrt(var + EPS) * rnorm)


def _dsilu(x, sg):
    return sg * (1.0 + x * (1.0 - sg))


def _gdn_out_bwd(o, z, gnorm, dy):
    r = lax.rsqrt(jnp.mean(o * o, axis=-1, keepdims=True) + EPS)
    xh = o * r
    sg = jax.nn.sigmoid(z)
    sz = z * sg
    t = dy * (gnorm * sz)
    do = r * (t - xh * jnp.mean(t * xh, axis=-1, keepdims=True))
    e = dy * xh
    return do, e * (gnorm * _dsilu(z, sg)), jnp.sum(e * sz, axis=0, keepdims=True)


def _ret_out_bwd(o, rg, rnorm, dy):
    oc = o - jnp.mean(o, axis=-1, keepdims=True)
    rs = lax.rsqrt(jnp.mean(oc * oc, axis=-1, keepdims=True) + EPS)
    xh = oc * rs
    sg = jax.nn.sigmoid(rg)
    srg = rg * sg
    t = dy * (rnorm * srg)
    do = rs * (t - jnp.mean(t, axis=-1, keepdims=True) - xh * jnp.mean(t * xh, axis=-1, keepdims=True))
    e = dy * xh
    return do, e * (rnorm * _dsilu(rg, sg)), jnp.sum(e * srg, axis=0, keepdims=True)


def _post_specs(tm, hv, d):
    row = lambda col: pl.BlockSpec((tm, hv), lambda i: (i, col))
    return dict(
        oa=row(0), ob=row(0), z=row(6), rg=row(7), ga=row(8), gb=row(9),
        gnorm=pl.BlockSpec((1, HEAD_DIM), lambda i: (0, 0)), rnorm=pl.BlockSpec((1, hv), lambda i: (0, 0)),
        w=pl.BlockSpec((hv, d), lambda i: (0, 0)), res=pl.BlockSpec((tm, d), lambda i: (i, 0)))


def _post_fwd(oa, ob, proj, gnorm, rnorm, wbg, wbr, wo, h1, name):
    tp, d = h1.shape
    hv = oa.shape[1]
    nh = hv // HEAD_DIM
    tm = _tile(tp, 256, 8)

    def body(oa_ref, ob_ref, z_ref, rg_ref, ga_ref, gb_ref, gn_ref, rn_ref, wbg_ref, wbr_ref, wo_ref, h_ref,
             o_ref, ya_sc, yb_sc):
        for h in range(nh):
            sl = slice(h * HEAD_DIM, (h + 1) * HEAD_DIM)
            ya_sc[:, sl] = _gdn_out(oa_ref[:, sl], z_ref[:, sl], gn_ref[...]).astype(BF16)
            yb_sc[:, sl] = _ret_out(ob_ref[:, sl], rg_ref[:, sl], rn_ref[:, sl]).astype(BF16)
        pa = jnp.dot(ya_sc[...], wbg_ref[...], preferred_element_type=F32)
        pb = jnp.dot(yb_sc[...], wbr_ref[...], preferred_element_type=F32)
        merged = jax.nn.sigmoid(ga_ref[...]) * pa + jax.nn.sigmoid(gb_ref[...]) * pb
        o_ref[...] = h_ref[...] + jnp.dot(merged.astype(BF16), wo_ref[...], preferred_element_type=F32)

    sp = _post_specs(tm, hv, d)
    return pl.pallas_call(
        body, name=name, grid=(tp // tm,),
        in_specs=[sp["oa"], sp["ob"], sp["z"], sp["rg"], sp["ga"], sp["gb"], sp["gnorm"], sp["rnorm"],
                  sp["w"], sp["w"], sp["w"], sp["res"]],
        out_specs=sp["res"], out_shape=jax.ShapeDtypeStruct((tp, d), F32),
        scratch_shapes=[pltpu.VMEM((tm, hv), BF16), pltpu.VMEM((tm, hv), BF16)],
        compiler_params=_params("parallel"))(oa, ob, proj, proj, proj, proj, gnorm, rnorm, wbg, wbr, wo, h1)


def _post_bwd(oa, ob, proj, gnorm, rnorm, wbg, wbr, wo, dh2, name):
    tp, d = dh2.shape
    hv = oa.shape[1]
    nh = hv // HEAD_DIM
    tm = _tile(tp, 256, 8)

    def body(oa_ref, ob_ref, z_ref, rg_ref, ga_ref, gb_ref, gn_ref, rn_ref, wbg_ref, wbr_ref, wo_ref, dh_ref,
             doa_ref, dob_ref, dg_ref, ya_ref, yb_ref, mg_ref, dpa_ref, dpb_ref, dgn_ref, drn_ref,
             dya_sc, dyb_sc):
        @pl.when(pl.program_id(0) == 0)
        def _():
            dgn_ref[...] = jnp.zeros_like(dgn_ref)
            drn_ref[...] = jnp.zeros_like(drn_ref)

        for h in range(nh):
            sl = slice(h * HEAD_DIM, (h + 1) * HEAD_DIM)
            ya_ref[:, sl] = _gdn_out(oa_ref[:, sl], z_ref[:, sl], gn_ref[...]).astype(BF16)
            yb_ref[:, sl] = _ret_out(ob_ref[:, sl], rg_ref[:, sl], rn_ref[:, sl]).astype(BF16)
        pa = jnp.dot(ya_ref[...], wbg_ref[...], preferred_element_type=F32)
        pb = jnp.dot(yb_ref[...], wbr_ref[...], preferred_element_type=F32)
        sa = jax.nn.sigmoid(ga_ref[...])
        sb = jax.nn.sigmoid(gb_ref[...])
        mg_ref[...] = (sa * pa + sb * pb).astype(BF16)
        dm = lax.dot_general(dh_ref[...].astype(BF16), wo_ref[...], NT, preferred_element_type=F32)
        dpa = (dm * sa).astype(BF16)
        dpb = (dm * sb).astype(BF16)
        dpa_ref[...] = dpa
        dpb_ref[...] = dpb
        dg_ref[:, 2 * hv:3 * hv] = (dm * pa * sa * (1.0 - sa)).astype(BF16)
        dg_ref[:, 3 * hv:4 * hv] = (dm * pb * sb * (1.0 - sb)).astype(BF16)
        dya_sc[...] = lax.dot_general(dpa, wbg_ref[...], NT, preferred_element_type=F32)
        dyb_sc[...] = lax.dot_general(dpb, wbr_ref[...], NT, preferred_element_type=F32)
        for h in range(nh):
            sl = slice(h * HEAD_DIM, (h + 1) * HEAD_DIM)
            doa, dz, dgn = _gdn_out_bwd(oa_ref[:, sl], z_ref[:, sl], gn_ref[...], dya_sc[:, sl])
            doa_ref[:, sl] = doa
            dg_ref[:, sl] = dz.astype(BF16)
            dgn_ref[...] += dgn
            dob, drg, drn = _ret_out_bwd(ob_ref[:, sl], rg_ref[:, sl], rn_ref[:, sl], dyb_sc[:, sl])
            dob_ref[:, sl] = dob
            dg_ref[:, hv + h * HEAD_DIM:hv + (h + 1) * HEAD_DIM] = drg.astype(BF16)
            drn_ref[:, sl] += drn

    sp = _post_specs(tm, hv, d)
    act = pl.BlockSpec((tm, hv), lambda i: (i, 0))
    return pl.pallas_call(
        body, name=name, grid=(tp // tm,),
        in_specs=[sp["oa"], sp["ob"], sp["z"], sp["rg"], sp["ga"], sp["gb"], sp["gnorm"], sp["rnorm"],
                  sp["w"], sp["w"], sp["w"], sp["res"]],
        out_specs=[act, act, pl.BlockSpec((tm, 4 * hv), lambda i: (i, 0)), act, act, sp["res"], sp["res"],
                   sp["res"], sp["gnorm"], sp["rnorm"]],
        out_shape=[jax.ShapeDtypeStruct((tp, hv), F32), jax.ShapeDtypeStruct((tp, hv), F32),
                   jax.ShapeDtypeStruct((tp, 4 * hv), BF16), jax.ShapeDtypeStruct((tp, hv), BF16),
                   jax.ShapeDtypeStruct((tp, hv), BF16), jax.ShapeDtypeStruct((tp, d), BF16),
                   jax.ShapeDtypeStruct((tp, d), BF16), jax.ShapeDtypeStruct((tp, d), BF16),
                   jax.ShapeDtypeStruct((1, HEAD_DIM), F32), jax.ShapeDtypeStruct((1, hv), F32)],
        scratch_shapes=[pltpu.VMEM((tm, hv), F32), pltpu.VMEM((tm, hv), F32)],
        compiler_params=_params("arbitrary"))(oa, ob, proj, proj, proj, proj, gnorm, rnorm, wbg, wbr, wo, dh2)


def _final(h3, gain, target, name):
    tp, d = h3.shape
    tm = HEAD_ROWS

    def body(h_ref, g_ref, t_ref, loss_ref, dh_ref, dgain_ref):
        i = pl.program_id(0)

        @pl.when(i == 0)
        def _():
            loss_ref[...] = jnp.zeros_like(loss_ref)
            dgain_ref[...] = jnp.zeros_like(dgain_ref)

        xh, r = _rms_parts(h_ref[...])
        err = jnp.where(i == 0, 0.0, xh * g_ref[...] - t_ref[...])
        dx, dg = _rms_bwd(err * (1.0 / d), xh, r, g_ref[...])
        dh_ref[...] = dx
        dgain_ref[...] += dg
        loss_ref[...] += 0.5 * jnp.sum(jnp.mean(err * err, axis=-1, keepdims=True), axis=0, keepdims=True)

    row = pl.BlockSpec((tm, d), lambda i: (i, 0))
    vec = pl.BlockSpec((1, d), lambda i: (0, 0))
    return pl.pallas_call(
        body, name=name, grid=(tp // tm,),
        in_specs=[row, vec, pl.BlockSpec((tm, d), lambda i: (jnp.maximum(i - 1, 0), 0))],
        out_specs=[pl.BlockSpec((1, LANES), lambda i: (0, 0)), row, vec],
        out_shape=[jax.ShapeDtypeStruct((1, LANES), F32), jax.ShapeDtypeStruct((tp, d), F32),
                   jax.ShapeDtypeStruct((1, d), F32)],
        compiler_params=_params("arbitrary"))(h3, gain, target)


def _peer(k):
    x, y, c = lax.axis_index("x"), lax.axis_index("y"), lax.axis_index("c")
    return (1 - x if k & 4 else x, 1 - y if k & 2 else y, 1 - c if k & 1 else c)


def _my_index():
    return 4 * lax.axis_index("x") + 2 * lax.axis_index("y") + lax.axis_index("c")


def _exchange(bufs, scatter, name):
    n = len(bufs)

    def body(*refs):
        _exchange_copies(refs[:n], refs[n:2 * n], refs[2 * n:], scatter, True, True)

    hbm, out_shape, sems = _exchange_refs(bufs)
    return pl.pallas_call(
        body, name=name, in_specs=hbm, out_specs=hbm, out_shape=out_shape, scratch_shapes=sems,
        compiler_params=pltpu.CompilerParams(has_side_effects=True))(*bufs)


def _gather_via_sibling(bufs, name):
    n = len(bufs)

    def body(*refs):
        x_refs, out_refs = refs[:n], refs[n:2 * n]
        send_sems, recv_sems, local_sems = refs[2 * n:]
        x, y, c = lax.axis_index("x"), lax.axis_index("y"), lax.axis_index("c")
        me, sibling = (x, y, c), (x, y, 1 - c)
        chips = [(1 - x, y), (x, 1 - y), (1 - x, 1 - y)]
        rows = lambda a, dev: out_refs[a].at[4 * dev[0] + 2 * dev[1] + dev[2]]

        def copy(k, a, block, to, src=None):
            return pltpu.make_async_remote_copy(
                src_ref=rows(a, block) if src is None else src, dst_ref=rows(a, block),
                send_sem=send_sems.at[k * n + a], recv_sem=recv_sems.at[k * n + a],
                device_id=to, device_id_type=pl.DeviceIdType.MESH)

        mine = [pltpu.make_async_copy(x_refs[a], rows(a, me), local_sems.at[a]) for a in range(n)]
        first = [copy(0, a, me, sibling, src=x_refs[a]) for a in range(n)]
        first += [copy(1 + j, a, me, (*chip, c), src=x_refs[a]) for j, chip in enumerate(chips) for a in range(n)]
        for cp in mine + first:
            cp.start()
        passed = []
        for j, chip in enumerate(chips):
            for a in range(n):
                copy(1 + j, a, (*chip, c), me).wait_recv()
                passed.append(copy(4 + j, a, (*chip, c), sibling))
                passed[-1].start()
        for a in range(n):
            copy(0, a, sibling, me).wait_recv()
        for j, chip in enumerate(chips):
            for a in range(n):
                copy(4 + j, a, (*chip, 1 - c), me).wait_recv()
        for cp in first + passed:
            cp.wait_send()
        for cp in mine:
            cp.wait()

    hbm, out_shape, sems = _exchange_refs(bufs)
    return pl.pallas_call(
        body, name=name, in_specs=hbm, out_specs=hbm, out_shape=out_shape, scratch_shapes=sems,
        compiler_params=pltpu.CompilerParams(has_side_effects=True))(*bufs)


def _exchange_refs(bufs):
    n = len(bufs)
    return ([pl.BlockSpec(memory_space=pl.ANY)] * n,
            [jax.ShapeDtypeStruct((N_DEV,) + b.shape[-2:], b.dtype) for b in bufs],
            [pltpu.SemaphoreType.DMA(((N_DEV - 1) * n,)), pltpu.SemaphoreType.DMA(((N_DEV - 1) * n,)),
             pltpu.SemaphoreType.DMA((n,))])


def _exchange_copies(x_refs, out_refs, sems, scatter, start, wait):
    n = len(x_refs)
    send_sems, recv_sems, local_sems = sems
    me = _my_index()
    copies = []
    for a in range(n):
        copies.append(pltpu.make_async_copy(x_refs[a].at[me] if scatter else x_refs[a], out_refs[a].at[me],
                                            local_sems.at[a]))
    sends = []
    arrivals = []
    for k in range(1, N_DEV):
        x, y, c = _peer(k)
        peer = 4 * x + 2 * y + c
        for a in range(n):
            sem = (k - 1) * n + a
            sends.append(pltpu.make_async_remote_copy(
                src_ref=x_refs[a].at[peer] if scatter else x_refs[a], dst_ref=out_refs[a].at[me],
                send_sem=send_sems.at[sem], recv_sem=recv_sems.at[sem],
                device_id=(x, y, c), device_id_type=pl.DeviceIdType.MESH))
            if wait:
                landed = out_refs[a].at[peer]
                arrivals.append(pltpu.make_async_remote_copy(
                    src_ref=landed, dst_ref=landed, send_sem=send_sems.at[sem], recv_sem=recv_sems.at[sem],
                    device_id=(x, y, c), device_id_type=pl.DeviceIdType.MESH))
    if start:
        for cp in copies + sends:
            cp.start()
    if wait:
        for cp in arrivals:
            cp.wait_recv()
        for cp in sends:
            cp.wait_send()
        for cp in copies:
            cp.wait()


def _carried_call(body, carry, first, last, *, name, grid, in_specs, out_specs, out_shape, scratch_shapes=()):
    in_specs, out_specs, out_shape = list(in_specs), list(out_specs), list(out_shape)
    semantics = ("arbitrary",) * len(grid)
    if carry is None:
        call = pl.pallas_call(body, name=name, grid=grid, in_specs=in_specs, out_specs=out_specs,
                              out_shape=out_shape, scratch_shapes=list(scratch_shapes),
                              compiler_params=_params(*semantics))
        return lambda *args: (call(*args), [])
    bufs, scatter = carry
    n, n_in, n_out, n_scratch = len(bufs), len(in_specs), len(out_specs), len(scratch_shapes)
    hbm, x_shapes, sems = _exchange_refs(bufs)

    def full_body(*refs):
        ins, x_refs = refs[:n_in], refs[n_in:n_in + n]
        outs, xo_refs = refs[n_in + n:n_in + n + n_out], refs[n_in + n + n_out:n_in + 2 * n + n_out]
        scratch = refs[n_in + 2 * n + n_out:n_in + 2 * n + n_out + n_scratch]
        x_sems = refs[n_in + 2 * n + n_out + n_scratch:]

        @pl.when(first())
        def _():
            _exchange_copies(x_refs, xo_refs, x_sems, scatter, True, False)

        body(*ins, *outs, *scratch)

        @pl.when(last())
        def _():
            _exchange_copies(x_refs, xo_refs, x_sems, scatter, False, True)

    call = pl.pallas_call(full_body, name=name, grid=grid, in_specs=in_specs + hbm, out_specs=out_specs + hbm,
                          out_shape=out_shape + x_shapes, scratch_shapes=list(scratch_shapes) + sems,
                          compiler_params=_params(*semantics))

    def run(*args):
        res = call(*args, *bufs)
        return res[:n_out], res[n_out:]
    return run


def _adamw(w, g, m, v, name):
    r, c = w.shape
    parts = g.ndim == 3
    tr = _tile(r, 256, 16 if parts else 8)
    c1 = 1.0 - ADAM_B1 ** ADAM_STEP
    c2 = 1.0 - ADAM_B2 ** ADAM_STEP

    def body(w_ref, g_ref, m_ref, v_ref, go_ref, d_ref, mo_ref, vo_ref):
        if parts:
            g = g_ref[0].astype(F32)
            for q in range(1, N_DEV):
                g = g + g_ref[q].astype(F32)
        else:
            g = g_ref[...]
        m = ADAM_B1 * m_ref[...] + (1.0 - ADAM_B1) * g
        v = ADAM_B2 * v_ref[...] + (1.0 - ADAM_B2) * (g * g)
        go_ref[...] = g
        d_ref[...] = -ADAM_LR * ((m / c1) / (jnp.sqrt(v / c2) + ADAM_EPS) + ADAM_WD * w_ref[...])
        mo_ref[...] = m
        vo_ref[...] = v

    blk = pl.BlockSpec((tr, c), lambda i: (i, 0))
    g_spec = pl.BlockSpec((N_DEV, tr, c), lambda i: (0, i, 0)) if parts else blk
    return pl.pallas_call(
        body, name=name, grid=(r // tr,), in_specs=[blk, g_spec, blk, blk], out_specs=[blk] * 4,
        out_shape=[jax.ShapeDtypeStruct((r, c), F32)] * 4,
        compiler_params=_params("parallel"))(w, g, m, v)


def _win_segments(hv, nh):
    o_z, o_b = 3 * hv, 4 * hv
    o_r = o_b + 2 * nh
    return [(0, 0, 3 * hv), (3 * hv, o_r, 3 * hv), (6 * hv, o_z, hv), (7 * hv, o_r + 3 * hv, 3 * hv),
            (10 * hv, o_b, 2 * nh)]


def _win_from_shards(shards, hv, nh):
    _, d, cs = shards.shape
    pieces = []
    for _, src, width in _win_segments(hv, nh):
        lo = src
        while lo < src + width:
            p = lo // cs
            hi = min(src + width, (p + 1) * cs)
            pieces.append(shards[p][:, lo - p * cs:hi - p * cs])
            lo = hi
    pieces.append(jnp.zeros((d, LANES - 2 * nh), shards.dtype))
    return jnp.concatenate(pieces, axis=1)


def _win_grad_to_shards(parts, hv, nh, cs):
    segments = _win_segments(hv, nh)
    starts = [sum(p.shape[1] for p in parts[:i]) for i in range(len(parts))]

    def columns(a, b):
        out = []
        for part, start in zip(parts, starts):
            lo, hi = max(a, start), min(b, start + part.shape[1])
            if lo < hi:
                out.append(part[:, lo - start:hi - start])
        return out

    shards = []
    for p in range(N_DEV):
        pieces = []
        lo = p * cs
        while lo < (p + 1) * cs:
            here, src, width = next(s for s in segments if s[1] <= lo < s[1] + s[2])
            hi = min((p + 1) * cs, src + width)
            pieces += columns(here + lo - src, here + hi - src)
            lo = hi
        shards.append(jnp.concatenate(pieces, axis=1))
    return jnp.stack(shards)


def _rope_tables(tp):
    pos = jnp.arange(tp, dtype=F32) - float(PAD_FRONT)
    inv = 1.0 / (ROPE_BASE ** jnp.linspace(0.0, 1.0, HEAD_DIM // 2, dtype=F32))
    ang = pos[:, None] * inv[None, :]
    cos = jnp.repeat(jnp.cos(ang), 2, axis=1)
    sin = jnp.repeat(jnp.sin(ang), 2, axis=1) * jnp.tile(jnp.array([-1.0, 1.0], F32), HEAD_DIM // 2)[None, :]
    return cos, sin


def _retention_tables(nh):
    log_gamma = jnp.log1p(-jnp.exp2(-5.0 - jnp.arange(nh, dtype=F32)))
    pos = jnp.arange(CHUNK, dtype=F32)
    causal = pos[:, None] >= pos[None, :]
    diff = pos[:, None] - pos[None, :]
    dec = jnp.where(causal, jnp.exp(jnp.where(causal, diff, 0.0) * log_gamma[:, None, None]), 0.0)
    ones = jnp.ones((1, 1, HEAD_DIM), F32)
    xi = jnp.exp((pos + 1.0)[None, :] * log_gamma[:, None])[:, :, None] * ones
    zeta = jnp.exp((CHUNK - 1.0 - pos)[None, :] * log_gamma[:, None])[:, :, None] * ones
    cd = jnp.exp(CHUNK * log_gamma)[:, None, None] * jnp.ones((1, 8, HEAD_DIM), F32)
    return dec, xi, zeta, cd


SHARDED = ("meta_tokens", "ffn1_w_in", "ffn1_w_out", "w_in", "gdn_conv_w", "w_branch_gdn", "w_branch_ret",
           "w_out", "ffn2_w_in", "ffn2_w_out")
COLUMN_SHARDED = ("meta_tokens", "ffn1_w_in", "w_in", "gdn_conv_w", "ffn2_w_in")
EXACT_F32 = ("meta_tokens", "gdn_conv_w")
REPLICATED = ("ffn1_norm", "mix_norm", "gdn_a_log", "gdn_dt_bias", "gdn_out_norm", "ret_out_norm", "ffn2_norm",
              "final_norm")
WEIGHTS = ("meta_tokens", "ffn1_norm", "ffn1_w_in", "ffn1_w_out", "mix_norm", "w_in", "gdn_conv_w", "gdn_a_log",
           "gdn_dt_bias", "gdn_out_norm", "ret_out_norm", "w_branch_gdn", "w_branch_ret", "w_out", "ffn2_norm",
           "ffn2_w_in", "ffn2_w_out", "final_norm")


def _as2d(a):
    if a.ndim == 3:
        return a[0]
    if a.ndim == 1:
        return a[None, :]
    return a


def _rows_of(shards):
    return shards.reshape(-1, shards.shape[2])


def _cols_of(shards):
    return shards.transpose(1, 0, 2).reshape(shards.shape[1], -1)


def _row_shards(a):
    return a.reshape(N_DEV, -1, a.shape[1])


def _col_shards(a):
    return a.reshape(a.shape[0], N_DEV, -1).transpose(1, 0, 2)


GATHER_FIRST = ("meta_tokens", "ffn1_w_in", "ffn1_w_out")
GATHER_BEHIND_FFN1 = ("w_in", "gdn_conv_w")
GATHER_BEHIND_PROJ = ("w_branch_gdn", "w_branch_ret", "w_out", "ffn2_w_in", "ffn2_w_out")
SCATTER_BEHIND_DN2 = ("ffn2_w_in", "ffn2_w_out", "w_branch_gdn", "w_branch_ret", "w_out")
SCATTER_BEHIND_FFN1 = ("w_in", "gdn_conv_w")
SCATTER_BEHIND_DWG = ("meta_tokens", "ffn1_w_out")
SCATTER_LAST = ("ffn1_w_in",)


def _device_step(x, target, send, rep):
    seq, d = x.shape
    tp = HEAD_ROWS + seq
    hv = d
    nh = hv // HEAD_DIM
    assert tp % (SCAN_CHUNKS * CHUNK) == 0 and tp % HEAD_ROWS == 0
    bf16_shards = lambda grads, names: [grads[n].astype(BF16) for n in names]

    pad_lanes = lambda row: jnp.pad(row, ((0, 0), (nh, LANES - 2 * nh)))
    alog = pad_lanes(rep["gdn_a_log"])
    dtb = pad_lanes(rep["gdn_dt_bias"])
    cos, sin = _rope_tables(tp)
    dec, xi, zeta, cd = _retention_tables(nh)

    got = dict(zip(GATHER_FIRST, _gather_via_sibling([send[n] for n in GATHER_FIRST], "gather_ffn1")))
    h0 = jnp.concatenate([jnp.zeros((PAD_FRONT, d), F32), _cols_of(got["meta_tokens"]), x], axis=0)
    f1i, f1o = got["ffn1_w_in"], _rows_of(got["ffn1_w_out"])
    (h1, hid1, dup1, dgate1), moved = _ffn_fwd(h0, rep["ffn1_norm"], f1i, f1o, "ffn1_fwd",
                                               ([send[n] for n in GATHER_BEHIND_FFN1], False))
    got.update(zip(GATHER_BEHIND_FFN1, moved))
    wp = _win_from_shards(got["w_in"], hv, nh)
    conv_w = _cols_of(got["gdn_conv_w"])
    (proj, n2), moved = _proj_fwd(h1, rep["mix_norm"], wp, "proj_fwd",
                                  ([send[n] for n in GATHER_BEHIND_PROJ], False))
    got.update(zip(GATHER_BEHIND_PROJ, moved))
    wbg, wbr, wo = _rows_of(got["w_branch_gdn"]), _rows_of(got["w_branch_ret"]), _rows_of(got["w_out"])
    f2i, f2o = got["ffn2_w_in"], _rows_of(got["ffn2_w_out"])
    qkv, conv_out = _conv_fwd(proj, conv_w, hv, "conv_fwd")
    oa, s_gdn, t_gdn = _gdn_fwd(qkv, proj, alog, dtb, nh, "gdn_fwd")
    ob, s_ret = _ret_fwd(proj, cos, sin, dec, xi, zeta, cd, nh, "ret_fwd")
    h2 = _post_fwd(oa, ob, proj, rep["gdn_out_norm"], rep["ret_out_norm"], wbg, wbr, wo, h1, "post_fwd")
    (h3, hid2, dup2, dgate2), _ = _ffn_fwd(h2, rep["ffn2_norm"], f2i, f2o, "ffn2_fwd")
    loss_row, dh3, d_final = _final(h3, rep["final_norm"], target, "final")

    (dh2, d_f2n, n3, dag2, dau2), _ = _ffn_bwd(h2, dh3, rep["ffn2_norm"], f2i, f2o, dup2, dgate2, "ffn2_bwd")
    grads = {"ffn2_w_in": jnp.concatenate([_matmul_tn_blocks(n3, dag2, "ffn2_dwg"),
                                           _matmul_tn_blocks(n3, dau2, "ffn2_dwu")]),
             "ffn2_w_out": _row_shards(_matmul_tn_blocks(hid2, dh3, "ffn2_dwo", 0.5))}

    doa, dob, dgate, ya, yb, merged, dpa, dpb, d_gn, d_rn = _post_bwd(
        oa, ob, proj, rep["gdn_out_norm"], rep["ret_out_norm"], wbg, wbr, wo, dh2, "post_bwd")
    grads["w_branch_gdn"] = _row_shards(_matmul_tn(ya, dpa, "dw_branch_gdn"))
    grads["w_branch_ret"] = _row_shards(_matmul_tn(yb, dpb, "dw_branch_ret"))
    grads["w_out"] = _row_shards(_matmul_tn(merged, dh2, "dw_out"))

    d_ret = _ret_bwd(proj, cos, sin, dec, xi, zeta, cd, s_ret, dob, nh, "ret_bwd")
    gdn_grads = _gdn_bwd(qkv, proj, alog, dtb, s_gdn, t_gdn, doa, nh, "gdn_bwd")
    dba, d_alog, d_dtb = gdn_grads[3:]
    dpre, g_conv = [], []
    for grp, tag in enumerate("qkv"):
        dx, dw = _conv_bwd(proj, conv_out, conv_w, gdn_grads[grp], grp, hv, "conv_bwd_" + tag)
        dpre.append(dx)
        g_conv.append(dw)
    grads["gdn_conv_w"] = _col_shards(jnp.concatenate(g_conv, axis=1))

    wide = dpre + list(d_ret) + [dgate]
    dn2, moved = _matmul_nt_parts(wide, wp[:, :10 * hv], "dn2_wide",
                                  (bf16_shards(grads, SCATTER_BEHIND_DN2), True))
    parts = dict(zip(SCATTER_BEHIND_DN2, moved))
    g_wp = [_matmul_tn(n2, dg, "dw_in_%d" % idx) for idx, dg in enumerate(wide + [dba])]
    grads["w_in"] = _win_grad_to_shards(g_wp, hv, nh, send["w_in"].shape[1])
    dh1, d_mixn = _norm_bwd(h1, rep["mix_norm"], dn2, dba, wp[:, 10 * hv:], dh2, "mix_norm_bwd")

    (dh0, d_f1n, n1, dag1, dau1), moved = _ffn_bwd(h0, dh1, rep["ffn1_norm"], f1i, f1o, dup1, dgate1, "ffn1_bwd",
                                                   (bf16_shards(grads, SCATTER_BEHIND_FFN1), True))
    parts.update(zip(SCATTER_BEHIND_FFN1, moved))
    grads["ffn1_w_out"] = _row_shards(_matmul_tn_blocks(hid1, dh1, "ffn1_dwo", 0.5))
    grads["meta_tokens"] = _col_shards(dh0[PAD_FRONT:HEAD_ROWS])
    g_gate, moved = _matmul_tn_blocks(n1, dag1, "ffn1_dwg", carry=(bf16_shards(grads, SCATTER_BEHIND_DWG), True))
    parts.update(zip(SCATTER_BEHIND_DWG, moved))
    grads["ffn1_w_in"] = jnp.concatenate([g_gate, _matmul_tn_blocks(n1, dau1, "ffn1_dwu")])
    parts.update(zip(SCATTER_LAST, _exchange(bf16_shards(grads, SCATTER_LAST), True, "scatter_ffn1")))

    small = {"ffn1_norm": d_f1n, "mix_norm": d_mixn, "gdn_a_log": d_alog[:, nh:2 * nh],
             "gdn_dt_bias": d_dtb[:, nh:2 * nh], "gdn_out_norm": d_gn, "ret_out_norm": d_rn, "ffn2_norm": d_f2n,
             "final_norm": d_final}
    return loss_row[0, 0], dh0[HEAD_ROWS:], parts, small


def kernel(x, meta_tokens, ffn1_norm, ffn1_w_in, ffn1_w_out, mix_norm, w_in, gdn_conv_w, gdn_a_log, gdn_dt_bias, gdn_out_norm, ret_out_norm, w_branch_gdn, w_branch_ret, w_out, ffn2_norm, ffn2_w_in, ffn2_w_out, final_norm, loss_target, m_meta_tokens, m_ffn1_norm, m_ffn1_w_in, m_ffn1_w_out, m_mix_norm, m_w_in, m_gdn_conv_w, m_gdn_a_log, m_gdn_dt_bias, m_gdn_out_norm, m_ret_out_norm, m_w_branch_gdn, m_w_branch_ret, m_w_out, m_ffn2_norm, m_ffn2_w_in, m_ffn2_w_out, m_final_norm, v_meta_tokens, v_ffn1_norm, v_ffn1_w_in, v_ffn1_w_out, v_mix_norm, v_w_in, v_gdn_conv_w, v_gdn_a_log, v_gdn_dt_bias, v_gdn_out_norm, v_ret_out_norm, v_w_branch_gdn, v_w_branch_ret, v_w_out, v_ffn2_norm, v_ffn2_w_in, v_ffn2_w_out, v_final_norm):
    given = dict(locals())
    params = {n: _as2d(given[n]) for n in WEIGHTS}
    local = {n: params[n] for n in SHARDED}
    rep = {n: params[n] for n in REPLICATED}

    send = {n: local[n] if n in EXACT_F32 else local[n].astype(BF16) for n in SHARDED}
    loss_sum, grad_x, parts, small = _device_step(x[0], loss_target[0], send, rep)
    parts.update(zip(REPLICATED, _exchange([small[n] for n in REPLICATED], False, "gather_small_grads")))
    loss = lax.psum(loss_sum, ("x", "y", "c"))

    outs = {}
    for n in WEIGHTS:
        res = _adamw(params[n], parts[n], _as2d(given["m_" + n]), _as2d(given["v_" + n]), "adamw_" + n)
        outs[n] = [r.reshape(given[n].shape) for r in res]
    return (loss, grad_x[None], *[outs[n][0] for n in WEIGHTS], *[outs[n][1] for n in WEIGHTS],
            *[outs[n][2] for n in WEIGHTS], *[outs[n][3] for n in WEIGHTS])
```

```python
import functools
import math

import jax
import jax.numpy as jnp
from jax import lax
from jax.experimental import pallas as pl
from jax.experimental.pallas import tpu as pltpu

F32 = jnp.float32
BF16 = jnp.bfloat16

N_DEV = 8
N_META = 16
CHUNK = 64
HEAD_DIM = 128
CONV_K = 4
ROPE_BASE = 10000.0
EPS = 1e-6
PAD_FRONT = 240
HEAD_ROWS = PAD_FRONT + N_META
LANES = 128
VMEM_LIMIT_BYTES = 56 * 1024 * 1024

ADAM_LR = 0.001
ADAM_B1 = 0.9
ADAM_B2 = 0.999
ADAM_EPS = 1e-08
ADAM_WD = 0.01
ADAM_STEP = 10

NN = (((1,), (0,)), ((), ()))
NT = (((1,), (1,)), ((), ()))
TN = (((0,), (0,)), ((), ()))


def _tile(n, target, mult):
    best = 0
    for t in range(mult, min(n, target) + 1, mult):
        if n % t == 0:
            best = t
    return best if best else n


def _params(*semantics):
    return pltpu.CompilerParams(dimension_semantics=semantics, vmem_limit_bytes=VMEM_LIMIT_BYTES)


def _split(a, pieces):
    out = []
    for _ in range(pieces - 1):
        part = a.astype(BF16)
        out.append(part)
        a = a - part.astype(F32)
    return out + [a.astype(BF16)]


def _raw_dot(a, b, dims, hi):
    dot = lambda x, y: lax.dot_general(x, y, dims, preferred_element_type=F32)
    if hi:
        (a_hi, a_lo), (b_hi, b_lo) = _split(a, 2), _split(b, 2)
        return dot(a_hi, b_hi) + (dot(a_hi, b_lo) + dot(a_lo, b_hi))
    return dot(a.astype(BF16), b.astype(BF16))


def _mask_dot(mask, x, dims):
    mask = mask.astype(BF16)
    hi, mid, lo = [lax.dot_general(mask, p, dims, preferred_element_type=F32) for p in _split(x, 3)]
    return hi + (mid + lo)


@jax.custom_vjp
def _cumsum_rows(x):
    c = x.shape[0]
    tril = lax.broadcasted_iota(jnp.int32, (c, c), 0) >= lax.broadcasted_iota(jnp.int32, (c, c), 1)
    return _mask_dot(tril, x, NN)


def _cumsum_rows_bwd(_, g):
    c = g.shape[0]
    tril = lax.broadcasted_iota(jnp.int32, (c, c), 0) >= lax.broadcasted_iota(jnp.int32, (c, c), 1)
    return (_mask_dot(tril, g, TN),)


_cumsum_rows.defvjp(lambda x: (_cumsum_rows(x), None), _cumsum_rows_bwd)


def _unit_lower_inverses(xs):
    c = xs[0].shape[0]
    eye = (lax.broadcasted_iota(jnp.int32, (c, c), 0) == lax.broadcasted_iota(jnp.int32, (c, c), 1)).astype(F32)
    t_inv = [eye + x for x in xs]
    for _ in range(int(math.log2(c)) - 1):
        xs = [_raw_dot(x, x, NN, True) for x in xs]
        t_inv = [t + _raw_dot(t, x, NN, True) for t, x in zip(t_inv, xs)]
    return t_inv


@jax.custom_vjp
def _known_inverse(x_neg, t_inv):
    return t_inv


_known_inverse.defvjp(
    lambda x_neg, t_inv: (t_inv, t_inv),
    lambda t_inv, g: (_raw_dot(_raw_dot(t_inv, g, TN, False), t_inv, NT, False), jnp.zeros_like(t_inv)))


def _make_mm(hi):
    @jax.custom_vjp
    def nn(a, b):
        return _raw_dot(a, b, NN, hi)

    @jax.custom_vjp
    def nt(a, b):
        return _raw_dot(a, b, NT, hi)

    @jax.custom_vjp
    def tn(a, b):
        return _raw_dot(a, b, TN, hi)

    nn.defvjp(lambda a, b: (_raw_dot(a, b, NN, hi), (a, b)),
              lambda r, g: (_raw_dot(g, r[1], NT, False), _raw_dot(r[0], g, TN, False)))
    nt.defvjp(lambda a, b: (_raw_dot(a, b, NT, hi), (a, b)),
              lambda r, g: (_raw_dot(g, r[1], NN, False), _raw_dot(g, r[0], TN, False)))
    tn.defvjp(lambda a, b: (_raw_dot(a, b, TN, hi), (a, b)),
              lambda r, g: (_raw_dot(r[1], g, NT, False), _raw_dot(r[0], g, NN, False)))
    return nn, nt, tn


def _silu(x):
    return x * jax.nn.sigmoid(x)


def _rms_parts(x):
    r = lax.rsqrt(jnp.mean(x * x, axis=-1, keepdims=True) + EPS)
    return x * r, r


def _rms_bwd(dy, xh, r, gain):
    dxh = dy * gain
    dx = r * (dxh - xh * jnp.mean(dxh * xh, axis=-1, keepdims=True))
    return dx, jnp.sum(dy * xh, axis=0, keepdims=True)


def _ffn_specs(tm, d, tf, nj):
    return [pl.BlockSpec((tm, d), lambda i, j: (i, 0)), pl.BlockSpec((1, d), lambda i, j: (0, 0)),
            pl.BlockSpec((1, d, tf), lambda i, j: (j, 0, 0)), pl.BlockSpec((1, d, tf), lambda i, j: (nj + j, 0, 0)),
            pl.BlockSpec((tf, d), lambda i, j: (j, 0))]


def _first_step(ndim):
    return lambda: functools.reduce(lambda a, b: a & b, [pl.program_id(k) == 0 for k in range(ndim)])


def _last_step(grid):
    return lambda: functools.reduce(lambda a, b: a & b, [pl.program_id(k) == g - 1 for k, g in enumerate(grid)])


def _ffn_fwd(h, gain, w_in, wo, name, carry=None):
    tp, d = h.shape
    tf = w_in.shape[2]
    nj = w_in.shape[0] // 2
    tm = _tile(tp, 768, 8)
    row, vec, wg_spec, wu_spec, wo_spec = _ffn_specs(tm, d, tf, nj)

    def body(h_ref, g_ref, wg3_ref, wu3_ref, wo_ref, o_ref, hid3_ref, dup3_ref, dgate3_ref, n_sc, acc_sc):
        wg_ref, wu_ref = wg3_ref.at[0], wu3_ref.at[0]
        j = pl.program_id(1)

        @pl.when(j == 0)
        def _():
            xh, _ = _rms_parts(h_ref[...])
            n_sc[...] = (xh * g_ref[...]).astype(BF16)
            acc_sc[...] = jnp.zeros_like(acc_sc)

        n = n_sc[...]
        a_g = jnp.dot(n, wg_ref[...], preferred_element_type=F32)
        a_u = jnp.dot(n, wu_ref[...], preferred_element_type=F32)
        sg = jax.nn.sigmoid(a_g)
        s = a_g * sg
        hid = (s * a_u).astype(BF16)
        hid3_ref[0] = hid
        dup3_ref[0] = s.astype(BF16)
        dgate3_ref[0] = (a_u * _dsilu(a_g, sg)).astype(BF16)
        acc_sc[...] += jnp.dot(hid, wo_ref[...], preferred_element_type=F32)

        @pl.when(j == nj - 1)
        def _():
            o_ref[...] = h_ref[...] + 0.5 * acc_sc[...]

    grid = (tp // tm, nj)
    act = pl.BlockSpec((1, tm, tf), lambda i, j: (j, i, 0))
    return _carried_call(
        body, carry, _first_step(2), _last_step(grid), name=name, grid=grid,
        in_specs=[row, vec, wg_spec, wu_spec, wo_spec], out_specs=[row, act, act, act],
        out_shape=[jax.ShapeDtypeStruct((tp, d), F32)] + [jax.ShapeDtypeStruct((nj, tp, tf), BF16)] * 3,
        scratch_shapes=[pltpu.VMEM((tm, d), BF16), pltpu.VMEM((tm, d), F32)])(h, gain, w_in, w_in, wo)


def _ffn_bwd(h, dho, gain, w_in, wo, dup3, dgate3, name, carry=None):
    tp, d = h.shape
    tf = w_in.shape[2]
    nj = w_in.shape[0] // 2
    tm = _tile(tp, 704, 16)
    ni = tp // tm
    row, vec, wg_spec, wu_spec, wo_spec = _ffn_specs(tm, d, tf, nj)

    def body(h_ref, dho_ref, g_ref, wg3_ref, wu3_ref, wo_ref, dup3_ref, dgate3_ref,
             dh_ref, dgain_ref, n_ref, dag3_ref, dau3_ref, dn_sc, dhb_sc):
        wg_ref, wu_ref = wg3_ref.at[0], wu3_ref.at[0]
        dag_ref, dau_ref = dag3_ref.at[0], dau3_ref.at[0]
        i, j = pl.program_id(0), pl.program_id(1)

        @pl.when(j == 0)
        def _():
            xh, _ = _rms_parts(h_ref[...])
            n_ref[...] = (xh * g_ref[...]).astype(BF16)
            dn_sc[...] = jnp.zeros_like(dn_sc)
            dhb_sc[...] = (0.5 * dho_ref[...]).astype(BF16)

        @pl.when((i == 0) & (j == 0))
        def _():
            dgain_ref[...] = jnp.zeros_like(dgain_ref)

        d_hid = lax.dot_general(dhb_sc[...], wo_ref[...], NT, preferred_element_type=F32)
        d_au = (d_hid * dup3_ref[0].astype(F32)).astype(BF16)
        d_ag = (d_hid * dgate3_ref[0].astype(F32)).astype(BF16)
        dau_ref[...] = d_au
        dag_ref[...] = d_ag
        dn_sc[...] += (lax.dot_general(d_ag, wg_ref[...], NT, preferred_element_type=F32)
                       + lax.dot_general(d_au, wu_ref[...], NT, preferred_element_type=F32))

        @pl.when(j == nj - 1)
        def _():
            xh, r = _rms_parts(h_ref[...])
            dx, dg = _rms_bwd(dn_sc[...], xh, r, g_ref[...])
            dh_ref[...] = dho_ref[...] + dx
            dgain_ref[...] += dg

    act = pl.BlockSpec((1, tm, tf), lambda i, j: (j, i, 0))
    return _carried_call(
        body, carry, _first_step(2), _last_step((ni, nj)), name=name, grid=(ni, nj),
        in_specs=[row, row, vec, wg_spec, wu_spec, wo_spec, act, act],
        out_specs=[row, vec, row, act, act],
        out_shape=[jax.ShapeDtypeStruct((tp, d), F32), jax.ShapeDtypeStruct((1, d), F32),
                   jax.ShapeDtypeStruct((tp, d), BF16)] + [jax.ShapeDtypeStruct((nj, tp, tf), BF16)] * 2,
        scratch_shapes=[pltpu.VMEM((tm, d), F32), pltpu.VMEM((tm, d), BF16)])(
            h, dho, gain, w_in, w_in, wo, dup3, dgate3)


def _matmul_tn(a, b, name, scale=1.0):
    t, m = a.shape
    n = b.shape[1]
    bm = _tile(m, 1024, LANES)
    bn = _tile(n, 1536, LANES)
    tk = _tile(t, 2816, 16)
    nk = t // tk

    def body(a_ref, b_ref, o_ref):
        k = pl.program_id(2)

        @pl.when(k == 0)
        def _():
            o_ref[...] = jnp.zeros_like(o_ref)

        o_ref[...] += lax.dot_general(a_ref[...].astype(BF16), b_ref[...].astype(BF16), TN,
                                      preferred_element_type=F32)

        if scale != 1.0:
            @pl.when(k == nk - 1)
            def _():
                o_ref[...] = o_ref[...] * scale

    return pl.pallas_call(
        body, name=name, grid=(m // bm, n // bn, nk),
        in_specs=[pl.BlockSpec((tk, bm), lambda i, j, k: (k, i)), pl.BlockSpec((tk, bn), lambda i, j, k: (k, j))],
        out_specs=pl.BlockSpec((bm, bn), lambda i, j, k: (i, j)),
        out_shape=jax.ShapeDtypeStruct((m, n), F32),
        compiler_params=_params("parallel", "parallel", "arbitrary"))(a, b)


def _matmul_tn_blocks(a, b, name, scale=1.0, carry=None):
    a_blocked = a.ndim == 3
    nb, t = (a.shape[0], a.shape[1]) if a_blocked else (b.shape[0], b.shape[1])
    m, n = a.shape[-1], b.shape[-1]
    tk = _tile(t, 2816, 16)
    nk = t // tk
    if a_blocked:
        bo = _tile(n, 1024, LANES)
        a_spec = pl.BlockSpec((1, tk, m), lambda p, o, k: (p, k, 0))
        b_spec = pl.BlockSpec((tk, bo), lambda p, o, k: (k, o))
        o_spec = pl.BlockSpec((m, bo), lambda p, o, k: (p, o))
        out_shape = jax.ShapeDtypeStruct((nb * m, n), F32)
        grid = (nb, n // bo, nk)
    else:
        bo = _tile(m, 1024, LANES)
        a_spec = pl.BlockSpec((tk, bo), lambda p, o, k: (k, o))
        b_spec = pl.BlockSpec((1, tk, n), lambda p, o, k: (p, k, 0))
        o_spec = pl.BlockSpec((1, bo, n), lambda p, o, k: (p, o, 0))
        out_shape = jax.ShapeDtypeStruct((nb, m, n), F32)
        grid = (nb, m // bo, nk)

    def body(a_ref, b_ref, o_ref):
        k = pl.program_id(2)
        a_blk = a_ref[0] if a_blocked else a_ref[...]
        b_blk = b_ref[...] if a_blocked else b_ref[0]
        part = lax.dot_general(a_blk.astype(BF16), b_blk.astype(BF16), TN, preferred_element_type=F32)
        out = o_ref if a_blocked else o_ref.at[0]

        @pl.when(k == 0)
        def _():
            out[...] = part

        @pl.when(k > 0)
        def _():
            out[...] += part

        if scale != 1.0:
            @pl.when(k == nk - 1)
            def _():
                out[...] = out[...] * scale

    (out,), moved = _carried_call(body, carry, _first_step(3), _last_step(grid), name=name, grid=grid,
                                  in_specs=[a_spec, b_spec], out_specs=[o_spec], out_shape=[out_shape])(a, b)
    return out if carry is None else (out, moved)


def _matmul_nt_parts(parts, w, name, carry=None):
    t = parts[0].shape[0]
    d = w.shape[0]
    widths = [p.shape[1] for p in parts]
    tk = _tile(math.gcd(*widths), 1024, LANES)
    counts = [wd // tk for wd in widths]
    starts = [sum(counts[:g]) for g in range(len(parts))]
    nk = sum(counts)
    tm = _tile(t, 1056, 16)
    n_parts = len(parts)

    def body(*refs):
        a_refs, w_ref, o_ref = refs[:n_parts], refs[n_parts], refs[-1]
        k = pl.program_id(1)

        @pl.when(k == 0)
        def _():
            o_ref[...] = jnp.zeros_like(o_ref)

        for g in range(n_parts):
            @pl.when((k >= starts[g]) & (k < starts[g] + counts[g]))
            def _(g=g):
                o_ref[...] += lax.dot_general(a_refs[g][...].astype(BF16), w_ref[...], NT,
                                              preferred_element_type=F32)

    in_specs = [pl.BlockSpec((tm, tk), lambda i, k, lo=starts[g], nb=counts[g]: (i, jnp.clip(k - lo, 0, nb - 1)))
                for g in range(n_parts)]
    in_specs.append(pl.BlockSpec((d, tk), lambda i, k: (0, k)))
    args = list(parts) + [w]
    grid = (t // tm, nk)
    (out,), moved = _carried_call(
        body, carry, _first_step(2), _last_step(grid), name=name, grid=grid, in_specs=in_specs,
        out_specs=[pl.BlockSpec((tm, d), lambda i, k: (i, 0))],
        out_shape=[jax.ShapeDtypeStruct((t, d), F32)])(*args)
    return out, moved


def _proj_fwd(h, gain, wp, name, carry=None):
    tp, d = h.shape
    npad = wp.shape[1]
    tm = _tile(tp, 768, 8)
    tn = _tile(npad, 3456, LANES)

    def body(h_ref, g_ref, w_ref, o_ref, n_ref):
        @pl.when(pl.program_id(1) == 0)
        def _():
            xh, _ = _rms_parts(h_ref[...])
            n_ref[...] = (xh * g_ref[...]).astype(BF16)

        o_ref[...] = jnp.dot(n_ref[...], w_ref[...], preferred_element_type=F32)

    grid = (tp // tm, npad // tn)
    return _carried_call(
        body, carry, _first_step(2), _last_step(grid), name=name, grid=grid,
        in_specs=[pl.BlockSpec((tm, d), lambda i, j: (i, 0)), pl.BlockSpec((1, d), lambda i, j: (0, 0)),
                  pl.BlockSpec((d, tn), lambda i, j: (0, j))],
        out_specs=[pl.BlockSpec((tm, tn), lambda i, j: (i, j)), pl.BlockSpec((tm, d), lambda i, j: (i, 0))],
        out_shape=[jax.ShapeDtypeStruct((tp, npad), F32), jax.ShapeDtypeStruct((tp, d), BF16)])(h, gain, wp)


def _norm_bwd(h, gain, dn, last, w_last, dres, name):
    tp, d = h.shape
    kl = last.shape[1]
    tm = _tile(tp, 256, 8)

    def body(h_ref, g_ref, dn_ref, last_ref, w_ref, dres_ref, dh_ref, dgain_ref):
        @pl.when(pl.program_id(0) == 0)
        def _():
            dgain_ref[...] = jnp.zeros_like(dgain_ref)

        dn_all = dn_ref[...] + lax.dot_general(last_ref[...].astype(BF16), w_ref[...], NT,
                                               preferred_element_type=F32)
        xh, r = _rms_parts(h_ref[...])
        dx, dg = _rms_bwd(dn_all, xh, r, g_ref[...])
        dh_ref[...] = dres_ref[...] + dx
        dgain_ref[...] += dg

    row = pl.BlockSpec((tm, d), lambda i: (i, 0))
    vec = pl.BlockSpec((1, d), lambda i: (0, 0))
    return pl.pallas_call(
        body, name=name, grid=(tp // tm,),
        in_specs=[row, vec, row, pl.BlockSpec((tm, kl), lambda i: (i, 0)), pl.BlockSpec((d, kl), lambda i: (0, 0)), row],
        out_specs=[row, vec],
        out_shape=[jax.ShapeDtypeStruct((tp, d), F32), jax.ShapeDtypeStruct((1, d), F32)],
        compiler_params=_params("arbitrary"))(h, gain, dn, last, w_last, dres)


def _head_post(a, grp):
    a = _silu(a)
    r = lax.rsqrt(jnp.sum(a * a, axis=-1, keepdims=True) + EPS)
    if isinstance(grp, int):
        return a if grp == 2 else a * r * (HEAD_DIM ** -0.5 if grp == 0 else 1.0)
    scale = jnp.where(grp == 0, HEAD_DIM ** -0.5, 1.0).astype(F32)
    return jnp.where(grp == 2, a, a * r * scale)


def _head_post_bwd(c, dy, grp):
    sg = jax.nn.sigmoid(c)
    a = c * sg
    dsilu = sg * (1.0 + c * (1.0 - sg))
    if grp == 2:
        return dy * dsilu
    r = lax.rsqrt(jnp.sum(a * a, axis=-1, keepdims=True) + EPS)
    scale = HEAD_DIM ** -0.5 if grp == 0 else 1.0
    da = (scale * r) * (dy - a * (r * r * jnp.sum(dy * a, axis=-1, keepdims=True)))
    return da * dsilu


def _conv_taps(ext_sc, w_ref, tm):
    ext = ext_sc[...]
    c = w_ref[CONV_K - 1:CONV_K, :] * ext[8:, :]
    for i in range(CONV_K - 1):
        s = CONV_K - 1 - i
        c = c + w_ref[i:i + 1, :] * pltpu.roll(ext, s, 0)[8:, :]
    return c


def _conv_fwd(proj, conv_w, hv, name):
    tp = proj.shape[0]
    tm = _tile(tp, 768, 8)
    nh = hv // HEAD_DIM

    def body(x_ref, halo_ref, w_ref, o_ref, c_ref, ext_sc):
        i, grp = pl.program_id(0), pl.program_id(1)
        ext_sc[0:8, :] = jnp.where(i == 0, 0.0, halo_ref[...])
        ext_sc[8:, :] = x_ref[...]
        c_ref[...] = _conv_taps(ext_sc, w_ref, tm)
        for h in range(nh):
            sl = slice(h * HEAD_DIM, (h + 1) * HEAD_DIM)
            o_ref[:, sl] = _head_post(c_ref[:, sl], grp)

    blk = pl.BlockSpec((tm, hv), lambda i, g: (i, g))
    return pl.pallas_call(
        body, name=name, grid=(tp // tm, 3),
        in_specs=[blk, pl.BlockSpec((8, hv), lambda i, g: (jnp.maximum(i * (tm // 8) - 1, 0), g)),
                  pl.BlockSpec((CONV_K, hv), lambda i, g: (0, g))],
        out_specs=[blk, blk],
        out_shape=[jax.ShapeDtypeStruct((tp, 3 * hv), F32)] * 2,
        scratch_shapes=[pltpu.VMEM((tm + 8, hv), F32)],
        compiler_params=_params("parallel", "arbitrary"))(proj, proj, conv_w)


def _conv_bwd(proj, conv_out, conv_w, dy, grp, hv, name):
    tp = proj.shape[0]
    tm = _tile(tp, 768, 8)
    ni = tp // tm
    nh = hv // HEAD_DIM

    def body(x_ref, c_ref, w_ref, dy_ref, dx_ref, dw_ref, dc_sc):
        step = pl.program_id(0)

        @pl.when(step == 0)
        def _():
            dc_sc[tm:, :] = jnp.zeros((8, hv), F32)
            dw_ref[...] = jnp.zeros_like(dw_ref)

        @pl.when(step > 0)
        def _():
            dc_sc[tm:, :] = dc_sc[0:8, :]

        for h in range(nh):
            sl = slice(h * HEAD_DIM, (h + 1) * HEAD_DIM)
            dc_sc[0:tm, sl] = _head_post_bwd(c_ref[:, sl], dy_ref[:, sl], grp)

        x = x_ref[...]
        dc_ext = dc_sc[...]
        dx = None
        for k in range(CONV_K):
            s = CONV_K - 1 - k
            shifted = dc_ext[0:tm, :] if s == 0 else pltpu.roll(dc_ext, tm + 8 - s, 0)[0:tm, :]
            dw_ref[k:k + 1, :] += jnp.sum(shifted * x, axis=0, keepdims=True)
            term = w_ref[k:k + 1, :] * shifted
            dx = term if dx is None else dx + term
        dx_ref[...] = dx.astype(BF16)

    tile = lambda step: ni - 1 - step
    grp_blk = pl.BlockSpec((tm, hv), lambda s: (tile(s), grp))
    own_blk = pl.BlockSpec((tm, hv), lambda s: (tile(s), 0))
    return pl.pallas_call(
        body, name=name, grid=(ni,),
        in_specs=[grp_blk, grp_blk, pl.BlockSpec((CONV_K, hv), lambda s: (0, grp)), own_blk],
        out_specs=[own_blk, pl.BlockSpec((CONV_K, hv), lambda s: (0, 0))],
        out_shape=[jax.ShapeDtypeStruct((tp, hv), BF16), jax.ShapeDtypeStruct((CONV_K, hv), F32)],
        scratch_shapes=[pltpu.VMEM((tm + 8, hv), F32)],
        compiler_params=_params("arbitrary"))(proj, conv_out, conv_w, dy)


def _gdn_gates(ba, alog, dtb):
    x = ba + dtb
    softplus = jnp.maximum(x, 0.0) + jnp.log1p(jnp.exp(-jnp.abs(x)))
    return _cumsum_rows(-jnp.exp(alog) * softplus), jax.nn.sigmoid(ba)


def _gdn_chunks(states, qs, ks, vs, gates, known_inverses=None):
    mm_nn, mm_nt, mm_tn = _make_mm(False)
    hi_nn, _, _ = _make_mm(True)
    nh = len(states)
    items = range(len(qs))
    head = [i % nh for i in items]
    c = qs[0].shape[0]
    lane = lax.broadcasted_iota(jnp.int32, (c, LANES), 1)
    last_row = (lax.broadcasted_iota(jnp.int32, (c, 1), 0) == c - 1).astype(F32)
    ri = lax.broadcasted_iota(jnp.int32, (c, c), 0)
    ci = lax.broadcasted_iota(jnp.int32, (c, c), 1)
    causal = ri >= ci
    strict = ri > ci
    eye = (ri == ci).astype(F32)
    sel_a = [(lane == nh + h).astype(F32) for h in range(nh)]
    sel_b = [(lane == h).astype(F32) for h in range(nh)]

    gcol = [jnp.sum(gates[i // nh][0] * sel_a[head[i]], axis=1, keepdims=True) for i in items]
    grow = [jnp.sum(eye * gcol[i], axis=0, keepdims=True) for i in items]
    beta = [jnp.sum(gates[i // nh][1] * sel_b[head[i]], axis=1, keepdims=True) for i in items]
    decay = [jnp.where(causal, jnp.exp(jnp.where(causal, gcol[i] - grow[i], 0.0)), 0.0) for i in items]
    kb = [ks[i] * beta[i] for i in items]
    kk = [mm_nt(kb[i], ks[i]) for i in items]
    qk = [mm_nt(qs[i], ks[i]) for i in items]
    x_neg = [-jnp.where(strict, kk[i] * decay[i], 0.0) for i in items]
    if known_inverses is None:
        t_inv = _unit_lower_inverses(x_neg)
    else:
        t_inv = [_known_inverse(x_neg[i], known_inverses[i]) for i in items]
    eg = [jnp.exp(gcol[i]) for i in items]
    u = [hi_nn(t_inv[i], vs[i] * beta[i]) for i in items]
    w = [hi_nn(t_inv[i], kb[i] * eg[i]) for i in items]
    qk = [qk[i] * decay[i] for i in items]
    glast = [jnp.sum(gcol[i] * last_row, axis=0, keepdims=True) for i in items]
    q_dec = [qs[i] * eg[i] for i in items]
    k_dec = [ks[i] * jnp.exp(glast[i] - gcol[i]) for i in items]
    s_dec = [jnp.exp(glast[i]) for i in items]

    outs = []
    for first in range(0, len(qs), nh):
        chunk = range(first, first + nh)
        ws = [mm_nn(w[i], states[i - first]) for i in chunk]
        from_state = [mm_nn(q_dec[i], states[i - first]) for i in chunk]
        v_new = [u[i] - ws[i - first] for i in chunk]
        intra = [mm_nn(qk[i], v_new[i - first]) for i in chunk]
        kv = [mm_tn(k_dec[i], v_new[i - first]) for i in chunk]
        outs += [from_state[i - first] + intra[i - first] for i in chunk]
        states = [states[i - first] * s_dec[i] + kv[i - first] for i in chunk]
    return outs, states, t_inv


SCAN_CHUNKS = 4


def _scan_specs(nh, steps, rev, first_col):
    sidx = (lambda s: steps - 1 - s) if rev else (lambda s: s)
    hv = nh * HEAD_DIM
    rows = SCAN_CHUNKS * CHUNK
    cols = [pl.BlockSpec((rows, hv), lambda s, g=g: (sidx(s), first_col + g)) for g in range(3)]
    st = pl.BlockSpec((1, nh, HEAD_DIM, HEAD_DIM), lambda s: (sidx(s), 0, 0, 0))
    act = pl.BlockSpec((rows, hv), lambda s: (sidx(s), 0))
    return cols, st, act


def _chunk_heads(ref, nh):
    return [ref[j * CHUNK:(j + 1) * CHUNK, h * HEAD_DIM:(h + 1) * HEAD_DIM] for j in range(SCAN_CHUNKS)
            for h in range(nh)]


def _store_chunk_heads(ref, values, nh, dtype=None):
    for i, val in enumerate(values):
        j, h = divmod(i, nh)
        ref[j * CHUNK:(j + 1) * CHUNK, h * HEAD_DIM:(h + 1) * HEAD_DIM] = val if dtype is None else val.astype(dtype)


def _gdn_fwd(qkv, proj, alog, dtb, nh, name):
    tp = qkv.shape[0]
    steps = tp // (SCAN_CHUNKS * CHUNK)
    rows = SCAN_CHUNKS * CHUNK

    def body(q_ref, k_ref, v_ref, ba_ref, al_ref, dt_ref, o_ref, st_ref, inv_ref, s_sc):
        @pl.when(pl.program_id(0) == 0)
        def _():
            s_sc[...] = jnp.zeros_like(s_sc)

        gates = [_gdn_gates(ba_ref[j * CHUNK:(j + 1) * CHUNK, :], al_ref[...], dt_ref[...])
                 for j in range(SCAN_CHUNKS)]
        states = [s_sc[h] for h in range(nh)]
        for h in range(nh):
            st_ref[0, h] = states[h]
        outs, new_states, t_inv = _gdn_chunks(states, _chunk_heads(q_ref, nh), _chunk_heads(k_ref, nh),
                                              _chunk_heads(v_ref, nh), gates)
        _store_chunk_heads(o_ref, outs, nh)
        for h in range(nh):
            s_sc[h] = new_states[h]
        for i, t in enumerate(t_inv):
            inv_ref[0, i] = t

    cols, st, act = _scan_specs(nh, steps, False, 0)
    ba = pl.BlockSpec((rows, LANES), lambda s: (s, 10 * nh * HEAD_DIM // LANES))
    vec = pl.BlockSpec((1, LANES), lambda s: (0, 0))
    inv = pl.BlockSpec((1, SCAN_CHUNKS * nh, CHUNK, CHUNK), lambda s: (s, 0, 0, 0))
    return pl.pallas_call(
        body, name=name, grid=(steps,), in_specs=cols + [ba, vec, vec], out_specs=[act, st, inv],
        out_shape=[jax.ShapeDtypeStruct((tp, nh * HEAD_DIM), F32),
                   jax.ShapeDtypeStruct((steps, nh, HEAD_DIM, HEAD_DIM), F32),
                   jax.ShapeDtypeStruct((steps, SCAN_CHUNKS * nh, CHUNK, CHUNK), F32)],
        scratch_shapes=[pltpu.VMEM((nh, HEAD_DIM, HEAD_DIM), F32)],
        compiler_params=_params("arbitrary"))(qkv, qkv, qkv, proj, alog, dtb)


def _gdn_bwd(qkv, proj, alog, dtb, states, inverses, do, nh, name):
    tp = qkv.shape[0]
    steps = tp // (SCAN_CHUNKS * CHUNK)
    rows = SCAN_CHUNKS * CHUNK

    def body(q_ref, k_ref, v_ref, ba_ref, al_ref, dt_ref, st_ref, inv_ref, do_ref,
             dq_ref, dk_ref, dv_ref, dba_ref, dal_ref, ddt_ref, ds_sc):
        @pl.when(pl.program_id(0) == 0)
        def _():
            ds_sc[...] = jnp.zeros_like(ds_sc)
            dal_ref[...] = jnp.zeros_like(dal_ref)
            ddt_ref[...] = jnp.zeros_like(ddt_ref)

        gates, gates_vjps = [], []
        for j in range(SCAN_CHUNKS):
            g, g_vjp = jax.vjp(_gdn_gates, ba_ref[j * CHUNK:(j + 1) * CHUNK, :], al_ref[...], dt_ref[...])
            gates.append(g)
            gates_vjps.append(g_vjp)
        known = [inv_ref[0, i] for i in range(SCAN_CHUNKS * nh)]
        fn = lambda s, q, k, v, g: _gdn_chunks(s, q, k, v, g, known)[:2]
        _, vjp = jax.vjp(fn, [st_ref[0, h] for h in range(nh)], _chunk_heads(q_ref, nh), _chunk_heads(k_ref, nh),
                         _chunk_heads(v_ref, nh), gates)
        ds, dq, dk, dv, dgates = vjp((_chunk_heads(do_ref, nh), [ds_sc[h] for h in range(nh)]))
        for h in range(nh):
            ds_sc[h] = ds[h]
        _store_chunk_heads(dq_ref, dq, nh)
        _store_chunk_heads(dk_ref, dk, nh)
        _store_chunk_heads(dv_ref, dv, nh)
        for j in range(SCAN_CHUNKS):
            dba, dal, ddt = gates_vjps[j](dgates[j])
            dba_ref[j * CHUNK:(j + 1) * CHUNK, :] = dba
            dal_ref[...] += dal
            ddt_ref[...] += ddt

    cols, st, act = _scan_specs(nh, steps, True, 0)
    ba = pl.BlockSpec((rows, LANES), lambda s: (steps - 1 - s, 10 * nh * HEAD_DIM // LANES))
    vec = pl.BlockSpec((1, LANES), lambda s: (0, 0))
    inv = pl.BlockSpec((1, SCAN_CHUNKS * nh, CHUNK, CHUNK), lambda s: (steps - 1 - s, 0, 0, 0))
    return pl.pallas_call(
        body, name=name, grid=(steps,), in_specs=cols + [ba, vec, vec, st, inv, act],
        out_specs=[act, act, act, pl.BlockSpec((rows, LANES), lambda s: (steps - 1 - s, 0)), vec, vec],
        out_shape=[jax.ShapeDtypeStruct((tp, nh * HEAD_DIM), F32)] * 3
                  + [jax.ShapeDtypeStruct((tp, LANES), F32), jax.ShapeDtypeStruct((1, LANES), F32),
                     jax.ShapeDtypeStruct((1, LANES), F32)],
        scratch_shapes=[pltpu.VMEM((nh, HEAD_DIM, HEAD_DIM), F32)],
        compiler_params=_params("arbitrary"))(qkv, qkv, qkv, proj, alog, dtb, states, inverses, do)


def _swap_pairs(t):
    lane = lax.broadcasted_iota(jnp.int32, t.shape, 1)
    n = t.shape[1]
    return jnp.where(lane % 2 == 0, pltpu.roll(t, n - 1, 1), pltpu.roll(t, 1, 1))


def _rot(t, cos, sin_signed):
    return t * cos + _swap_pairs(t) * sin_signed


def _rot_t(dt, cos, sin_signed):
    return dt * cos + _swap_pairs(dt * sin_signed)


def _ret_chunks(states, qs, ks, vs, dec, xi, zeta, cd):
    mm_nn, mm_nt, mm_tn = _make_mm(False)
    nh = len(states)
    items = range(len(qs))
    scores = [mm_nt(qs[i], ks[i]) for i in items]
    kv = [mm_tn(ks[i] * zeta[i % nh], vs[i]) for i in items]
    intra = [mm_nn(scores[i] * dec[i % nh], vs[i]) for i in items]
    q_dec = [qs[i] * xi[i % nh] for i in items]
    outs = []
    for first in range(0, len(qs), nh):
        outs += [intra[first + h] + mm_nn(q_dec[first + h], states[h]) for h in range(nh)]
        states = [states[h] * cd[h] + kv[first + h] for h in range(nh)]
    return outs, states


def _ret_table_specs(nh, steps, rev):
    sidx = (lambda s: steps - 1 - s) if rev else (lambda s: s)
    rope = pl.BlockSpec((SCAN_CHUNKS * CHUNK, HEAD_DIM), lambda s: (sidx(s), 0))
    dec = pl.BlockSpec((nh, CHUNK, CHUNK), lambda s: (0, 0, 0))
    tab = pl.BlockSpec((nh, CHUNK, HEAD_DIM), lambda s: (0, 0, 0))
    cd = pl.BlockSpec((nh, 8, HEAD_DIM), lambda s: (0, 0, 0))
    return [rope, rope, dec, tab, tab, cd]


def _rotated(ref, cos_ref, sin_ref, nh, scale=1.0):
    out = []
    for j in range(SCAN_CHUNKS):
        rows = slice(j * CHUNK, (j + 1) * CHUNK)
        cos_t, sin_t = cos_ref[rows, :], sin_ref[rows, :]
        for h in range(nh):
            t = _rot(ref[rows, h * HEAD_DIM:(h + 1) * HEAD_DIM], cos_t, sin_t)
            out.append(t if scale == 1.0 else t * scale)
    return out


def _ret_fwd(proj, cos, sin, dec, xi, zeta, cd, nh, name):
    tp = proj.shape[0]
    steps = tp // (SCAN_CHUNKS * CHUNK)
    kscale = HEAD_DIM ** -0.5

    def body(q_ref, k_ref, v_ref, cos_ref, sin_ref, dec_ref, xi_ref, zeta_ref, cd_ref, o_ref, st_ref, s_sc):
        @pl.when(pl.program_id(0) == 0)
        def _():
            s_sc[...] = jnp.zeros_like(s_sc)

        heads = range(nh)
        states = [s_sc[h] for h in heads]
        for h in heads:
            st_ref[0, h] = states[h]
        outs, new_states = _ret_chunks(
            states, _rotated(q_ref, cos_ref, sin_ref, nh), _rotated(k_ref, cos_ref, sin_ref, nh, kscale),
            _chunk_heads(v_ref, nh), [dec_ref[h] for h in heads], [xi_ref[h] for h in heads],
            [zeta_ref[h] for h in heads], [cd_ref[h][0:1, :] for h in heads])
        _store_chunk_heads(o_ref, outs, nh)
        for h in heads:
            s_sc[h] = new_states[h]

    cols, st, act = _scan_specs(nh, steps, False, 3)
    return pl.pallas_call(
        body, name=name, grid=(steps,), in_specs=cols + _ret_table_specs(nh, steps, False), out_specs=[act, st],
        out_shape=[jax.ShapeDtypeStruct((tp, nh * HEAD_DIM), F32),
                   jax.ShapeDtypeStruct((steps, nh, HEAD_DIM, HEAD_DIM), F32)],
        scratch_shapes=[pltpu.VMEM((nh, HEAD_DIM, HEAD_DIM), F32)],
        compiler_params=_params("arbitrary"))(proj, proj, proj, cos, sin, dec, xi, zeta, cd)


def _ret_bwd(proj, cos, sin, dec, xi, zeta, cd, states, do, nh, name):
    tp = proj.shape[0]
    steps = tp // (SCAN_CHUNKS * CHUNK)
    kscale = HEAD_DIM ** -0.5

    def body(q_ref, k_ref, v_ref, cos_ref, sin_ref, dec_ref, xi_ref, zeta_ref, cd_ref, st_ref, do_ref,
             dq_ref, dk_ref, dv_ref, ds_sc):
        @pl.when(pl.program_id(0) == 0)
        def _():
            ds_sc[...] = jnp.zeros_like(ds_sc)

        heads = range(nh)
        fn = functools.partial(_ret_chunks, dec=[dec_ref[h] for h in heads], xi=[xi_ref[h] for h in heads],
                               zeta=[zeta_ref[h] for h in heads], cd=[cd_ref[h][0:1, :] for h in heads])
        _, vjp = jax.vjp(fn, [st_ref[0, h] for h in heads], _rotated(q_ref, cos_ref, sin_ref, nh),
                         _rotated(k_ref, cos_ref, sin_ref, nh, kscale), _chunk_heads(v_ref, nh))
        ds, dq, dk, dv = vjp((_chunk_heads(do_ref, nh), [ds_sc[h] for h in heads]))
        for h in heads:
            ds_sc[h] = ds[h]
        for i in range(SCAN_CHUNKS * nh):
            rows = slice((i // nh) * CHUNK, (i // nh + 1) * CHUNK)
            cos_t, sin_t = cos_ref[rows, :], sin_ref[rows, :]
            dq[i] = _rot_t(dq[i], cos_t, sin_t)
            dk[i] = _rot_t(dk[i] * kscale, cos_t, sin_t)
        _store_chunk_heads(dq_ref, dq, nh, BF16)
        _store_chunk_heads(dk_ref, dk, nh, BF16)
        _store_chunk_heads(dv_ref, dv, nh, BF16)

    cols, st, act = _scan_specs(nh, steps, True, 3)
    return pl.pallas_call(
        body, name=name, grid=(steps,), in_specs=cols + _ret_table_specs(nh, steps, True) + [st, act],
        out_specs=[act, act, act],
        out_shape=[jax.ShapeDtypeStruct((tp, nh * HEAD_DIM), BF16)] * 3,
        scratch_shapes=[pltpu.VMEM((nh, HEAD_DIM, HEAD_DIM), F32)],
        compiler_params=_params("arbitrary"))(proj, proj, proj, cos, sin, dec, xi, zeta, cd, states, do)


def _gdn_out(o, z, gnorm):
    return o * lax.rsqrt(jnp.mean(o * o, axis=-1, keepdims=True) + EPS) * gnorm * _silu(z)


def _ret_out(o, rg, rnorm):
    mu = jnp.mean(o, axis=-1, keepdims=True)
    var = jnp.mean(jnp.square(o - mu), axis=-1, keepdims=True)
    return _silu(rg) * ((o - mu) * lax.rsqrt(var + EPS) * rnorm)


def _dsilu(x, sg):
    return sg * (1.0 + x * (1.0 - sg))


def _gdn_out_bwd(o, z, gnorm, dy):
    r = lax.rsqrt(jnp.mean(o * o, axis=-1, keepdims=True) + EPS)
    xh = o * r
    sg = jax.nn.sigmoid(z)
    sz = z * sg
    t = dy * (gnorm * sz)
    do = r * (t - xh * jnp.mean(t * xh, axis=-1, keepdims=True))
    e = dy * xh
    return do, e * (gnorm * _dsilu(z, sg)), jnp.sum(e * sz, axis=0, keepdims=True)


def _ret_out_bwd(o, rg, rnorm, dy):
    oc = o - jnp.mean(o, axis=-1, keepdims=True)
    rs = lax.rsqrt(jnp.mean(oc * oc, axis=-1, keepdims=True) + EPS)
    xh = oc * rs
    sg = jax.nn.sigmoid(rg)
    srg = rg * sg
    t = dy * (rnorm * srg)
    do = rs * (t - jnp.mean(t, axis=-1, keepdims=True) - xh * jnp.mean(t * xh, axis=-1, keepdims=True))
    e = dy * xh
    return do, e * (rnorm * _dsilu(rg, sg)), jnp.sum(e * srg, axis=0, keepdims=True)


def _post_specs(tm, hv, d):
    row = lambda col: pl.BlockSpec((tm, hv), lambda i: (i, col))
    return dict(
        oa=row(0), ob=row(0), z=row(6), rg=row(7), ga=row(8), gb=row(9),
        gnorm=pl.BlockSpec((1, HEAD_DIM), lambda i: (0, 0)), rnorm=pl.BlockSpec((1, hv), lambda i: (0, 0)),
        w=pl.BlockSpec((hv, d), lambda i: (0, 0)), res=pl.BlockSpec((tm, d), lambda i: (i, 0)))


def _post_fwd(oa, ob, proj, gnorm, rnorm, wbg, wbr, wo, h1, name):
    tp, d = h1.shape
    hv = oa.shape[1]
    nh = hv // HEAD_DIM
    tm = _tile(tp, 256, 8)

    def body(oa_ref, ob_ref, z_ref, rg_ref, ga_ref, gb_ref, gn_ref, rn_ref, wbg_ref, wbr_ref, wo_ref, h_ref,
             o_ref, ya_sc, yb_sc):
        for h in range(nh):
            sl = slice(h * HEAD_DIM, (h + 1) * HEAD_DIM)
            ya_sc[:, sl] = _gdn_out(oa_ref[:, sl], z_ref[:, sl], gn_ref[...]).astype(BF16)
            yb_sc[:, sl] = _ret_out(ob_ref[:, sl], rg_ref[:, sl], rn_ref[:, sl]).astype(BF16)
        pa = jnp.dot(ya_sc[...], wbg_ref[...], preferred_element_type=F32)
        pb = jnp.dot(yb_sc[...], wbr_ref[...], preferred_element_type=F32)
        merged = jax.nn.sigmoid(ga_ref[...]) * pa + jax.nn.sigmoid(gb_ref[...]) * pb
        o_ref[...] = h_ref[...] + jnp.dot(merged.astype(BF16), wo_ref[...], preferred_element_type=F32)

    sp = _post_specs(tm, hv, d)
    return pl.pallas_call(
        body, name=name, grid=(tp // tm,),
        in_specs=[sp["oa"], sp["ob"], sp["z"], sp["rg"], sp["ga"], sp["gb"], sp["gnorm"], sp["rnorm"],
                  sp["w"], sp["w"], sp["w"], sp["res"]],
        out_specs=sp["res"], out_shape=jax.ShapeDtypeStruct((tp, d), F32),
        scratch_shapes=[pltpu.VMEM((tm, hv), BF16), pltpu.VMEM((tm, hv), BF16)],
        compiler_params=_params("parallel"))(oa, ob, proj, proj, proj, proj, gnorm, rnorm, wbg, wbr, wo, h1)


def _post_bwd(oa, ob, proj, gnorm, rnorm, wbg, wbr, wo, dh2, name):
    tp, d = dh2.shape
    hv = oa.shape[1]
    nh = hv // HEAD_DIM
    tm = _tile(tp, 256, 8)

    def body(oa_ref, ob_ref, z_ref, rg_ref, ga_ref, gb_ref, gn_ref, rn_ref, wbg_ref, wbr_ref, wo_ref, dh_ref,
             doa_ref, dob_ref, dg_ref, ya_ref, yb_ref, mg_ref, dpa_ref, dpb_ref, dgn_ref, drn_ref,
             dya_sc, dyb_sc):
        @pl.when(pl.program_id(0) == 0)
        def _():
            dgn_ref[...] = jnp.zeros_like(dgn_ref)
            drn_ref[...] = jnp.zeros_like(drn_ref)

        for h in range(nh):
            sl = slice(h * HEAD_DIM, (h + 1) * HEAD_DIM)
            ya_ref[:, sl] = _gdn_out(oa_ref[:, sl], z_ref[:, sl], gn_ref[...]).astype(BF16)
            yb_ref[:, sl] = _ret_out(ob_ref[:, sl], rg_ref[:, sl], rn_ref[:, sl]).astype(BF16)
        pa = jnp.dot(ya_ref[...], wbg_ref[...], preferred_element_type=F32)
        pb = jnp.dot(yb_ref[...], wbr_ref[...], preferred_element_type=F32)
        sa = jax.nn.sigmoid(ga_ref[...])
        sb = jax.nn.sigmoid(gb_ref[...])
        mg_ref[...] = (sa * pa + sb * pb).astype(BF16)
        dm = lax.dot_general(dh_ref[...].astype(BF16), wo_ref[...], NT, preferred_element_type=F32)
        dpa = (dm * sa).astype(BF16)
        dpb = (dm * sb).astype(BF16)
        dpa_ref[...] = dpa
        dpb_ref[...] = dpb
        dg_ref[:, 2 * hv:3 * hv] = (dm * pa * sa * (1.0 - sa)).astype(BF16)
        dg_ref[:, 3 * hv:4 * hv] = (dm * pb * sb * (1.0 - sb)).astype(BF16)
        dya_sc[...] = lax.dot_general(dpa, wbg_ref[...], NT, preferred_element_type=F32)
        dyb_sc[...] = lax.dot_general(dpb, wbr_ref[...], NT, preferred_element_type=F32)
        for h in range(nh):
            sl = slice(h * HEAD_DIM, (h + 1) * HEAD_DIM)
            doa, dz, dgn = _gdn_out_bwd(oa_ref[:, sl], z_ref[:, sl], gn_ref[...], dya_sc[:, sl])
            doa_ref[:, sl] = doa
            dg_ref[:, sl] = dz.astype(BF16)
            dgn_ref[...] += dgn
            dob, drg, drn = _ret_out_bwd(ob_ref[:, sl], rg_ref[:, sl], rn_ref[:, sl], dyb_sc[:, sl])
            dob_ref[:, sl] = dob
            dg_ref[:, hv + h * HEAD_DIM:hv + (h + 1) * HEAD_DIM] = drg.astype(BF16)
            drn_ref[:, sl] += drn

    sp = _post_specs(tm, hv, d)
    act = pl.BlockSpec((tm, hv), lambda i: (i, 0))
    return pl.pallas_call(
        body, name=name, grid=(tp // tm,),
        in_specs=[sp["oa"], sp["ob"], sp["z"], sp["rg"], sp["ga"], sp["gb"], sp["gnorm"], sp["rnorm"],
                  sp["w"], sp["w"], sp["w"], sp["res"]],
        out_specs=[act, act, pl.BlockSpec((tm, 4 * hv), lambda i: (i, 0)), act, act, sp["res"], sp["res"],
                   sp["res"], sp["gnorm"], sp["rnorm"]],
        out_shape=[jax.ShapeDtypeStruct((tp, hv), F32), jax.ShapeDtypeStruct((tp, hv), F32),
                   jax.ShapeDtypeStruct((tp, 4 * hv), BF16), jax.ShapeDtypeStruct((tp, hv), BF16),
                   jax.ShapeDtypeStruct((tp, hv), BF16), jax.ShapeDtypeStruct((tp, d), BF16),
                   jax.ShapeDtypeStruct((tp, d), BF16), jax.ShapeDtypeStruct((tp, d), BF16),
                   jax.ShapeDtypeStruct((1, HEAD_DIM), F32), jax.ShapeDtypeStruct((1, hv), F32)],
        scratch_shapes=[pltpu.VMEM((tm, hv), F32), pltpu.VMEM((tm, hv), F32)],
        compiler_params=_params("arbitrary"))(oa, ob, proj, proj, proj, proj, gnorm, rnorm, wbg, wbr, wo, dh2)


def _final(h3, gain, target, name):
    tp, d = h3.shape
    tm = HEAD_ROWS

    def body(h_ref, g_ref, t_ref, loss_ref, dh_ref, dgain_ref):
        i = pl.program_id(0)

        @pl.when(i == 0)
        def _():
            loss_ref[...] = jnp.zeros_like(loss_ref)
            dgain_ref[...] = jnp.zeros_like(dgain_ref)

        xh, r = _rms_parts(h_ref[...])
        err = jnp.where(i == 0, 0.0, xh * g_ref[...] - t_ref[...])
        dx, dg = _rms_bwd(err * (1.0 / d), xh, r, g_ref[...])
        dh_ref[...] = dx
        dgain_ref[...] += dg
        loss_ref[...] += 0.5 * jnp.sum(jnp.mean(err * err, axis=-1, keepdims=True), axis=0, keepdims=True)

    row = pl.BlockSpec((tm, d), lambda i: (i, 0))
    vec = pl.BlockSpec((1, d), lambda i: (0, 0))
    return pl.pallas_call(
        body, name=name, grid=(tp // tm,),
        in_specs=[row, vec, pl.BlockSpec((tm, d), lambda i: (jnp.maximum(i - 1, 0), 0))],
        out_specs=[pl.BlockSpec((1, LANES), lambda i: (0, 0)), row, vec],
        out_shape=[jax.ShapeDtypeStruct((1, LANES), F32), jax.ShapeDtypeStruct((tp, d), F32),
                   jax.ShapeDtypeStruct((1, d), F32)],
        compiler_params=_params("arbitrary"))(h3, gain, target)


def _peer(k):
    x, y, c = lax.axis_index("x"), lax.axis_index("y"), lax.axis_index("c")
    return (1 - x if k & 4 else x, 1 - y if k & 2 else y, 1 - c if k & 1 else c)


def _my_index():
    return 4 * lax.axis_index("x") + 2 * lax.axis_index("y") + lax.axis_index("c")


def _exchange(bufs, scatter, name):
    n = len(bufs)

    def body(*refs):
        _exchange_copies(refs[:n], refs[n:2 * n], refs[2 * n:], scatter, True, True)

    hbm, out_shape, sems = _exchange_refs(bufs)
    return pl.pallas_call(
        body, name=name, in_specs=hbm, out_specs=hbm, out_shape=out_shape, scratch_shapes=sems,
        compiler_params=pltpu.CompilerParams(has_side_effects=True))(*bufs)


def _gather_via_sibling(bufs, name):
    n = len(bufs)

    def body(*refs):
        _sibling_gather_copies(refs[:n], refs[n:2 * n], refs[2 * n:], True, True)

    hbm, out_shape, sems = _exchange_refs(bufs)
    return pl.pallas_call(
        body, name=name, in_specs=hbm, out_specs=hbm, out_shape=out_shape, scratch_shapes=sems,
        compiler_params=pltpu.CompilerParams(has_side_effects=True))(*bufs)


def _sibling_gather_copies(x_refs, out_refs, sems, start, finish):
    n = len(x_refs)
    send_sems, recv_sems, local_sems = sems
    x, y, c = lax.axis_index("x"), lax.axis_index("y"), lax.axis_index("c")
    me, sibling = (x, y, c), (x, y, 1 - c)
    chips = [(1 - x, y), (x, 1 - y), (1 - x, 1 - y)]
    rows = lambda a, dev: out_refs[a].at[4 * dev[0] + 2 * dev[1] + dev[2]]

    def copy(k, a, block, to, src=None):
        return pltpu.make_async_remote_copy(
            src_ref=rows(a, block) if src is None else src, dst_ref=rows(a, block),
            send_sem=send_sems.at[k * n + a], recv_sem=recv_sems.at[k * n + a],
            device_id=to, device_id_type=pl.DeviceIdType.MESH)

    mine = [pltpu.make_async_copy(x_refs[a], rows(a, me), local_sems.at[a]) for a in range(n)]
    first = [copy(0, a, me, sibling, src=x_refs[a]) for a in range(n)]
    first += [copy(1 + j, a, me, (*chip, c), src=x_refs[a]) for j, chip in enumerate(chips) for a in range(n)]
    if start:
        for cp in mine + first:
            cp.start()
    if finish:
        passed = []
        for j, chip in enumerate(chips):
            for a in range(n):
                copy(1 + j, a, (*chip, c), me).wait_recv()
                passed.append(copy(4 + j, a, (*chip, c), sibling))
                passed[-1].start()
        for a in range(n):
            copy(0, a, sibling, me).wait_recv()
        for j, chip in enumerate(chips):
            for a in range(n):
                copy(4 + j, a, (*chip, 1 - c), me).wait_recv()
        for cp in first + passed:
            cp.wait_send()
        for cp in mine:
            cp.wait()


def _exchange_refs(bufs):
    n = len(bufs)
    return ([pl.BlockSpec(memory_space=pl.ANY)] * n,
            [jax.ShapeDtypeStruct((N_DEV,) + b.shape[-2:], b.dtype) for b in bufs],
            [pltpu.SemaphoreType.DMA(((N_DEV - 1) * n,)), pltpu.SemaphoreType.DMA(((N_DEV - 1) * n,)),
             pltpu.SemaphoreType.DMA((n,))])


def _exchange_copies(x_refs, out_refs, sems, scatter, start, wait):
    n = len(x_refs)
    send_sems, recv_sems, local_sems = sems
    me = _my_index()
    copies = []
    for a in range(n):
        copies.append(pltpu.make_async_copy(x_refs[a].at[me] if scatter else x_refs[a], out_refs[a].at[me],
                                            local_sems.at[a]))
    sends = []
    arrivals = []
    for k in range(1, N_DEV):
        x, y, c = _peer(k)
        peer = 4 * x + 2 * y + c
        for a in range(n):
            sem = (k - 1) * n + a
            sends.append(pltpu.make_async_remote_copy(
                src_ref=x_refs[a].at[peer] if scatter else x_refs[a], dst_ref=out_refs[a].at[me],
                send_sem=send_sems.at[sem], recv_sem=recv_sems.at[sem],
                device_id=(x, y, c), device_id_type=pl.DeviceIdType.MESH))
            if wait:
                landed = out_refs[a].at[peer]
                arrivals.append(pltpu.make_async_remote_copy(
                    src_ref=landed, dst_ref=landed, send_sem=send_sems.at[sem], recv_sem=recv_sems.at[sem],
                    device_id=(x, y, c), device_id_type=pl.DeviceIdType.MESH))
    if start:
        for cp in copies + sends:
            cp.start()
    if wait:
        for cp in arrivals:
            cp.wait_recv()
        for cp in sends:
            cp.wait_send()
        for cp in copies:
            cp.wait()


def _carried_call(body, carry, first, last, *, name, grid, in_specs, out_specs, out_shape, scratch_shapes=()):
    in_specs, out_specs, out_shape = list(in_specs), list(out_specs), list(out_shape)
    semantics = ("arbitrary",) * len(grid)
    if carry is None:
        call = pl.pallas_call(body, name=name, grid=grid, in_specs=in_specs, out_specs=out_specs,
                              out_shape=out_shape, scratch_shapes=list(scratch_shapes),
                              compiler_params=_params(*semantics))
        return lambda *args: (call(*args), [])
    bufs, kind = carry
    n, n_in, n_out, n_scratch = len(bufs), len(in_specs), len(out_specs), len(scratch_shapes)
    hbm, x_shapes, sems = _exchange_refs(bufs)

    def copies(x_refs, xo_refs, x_sems, start, finish):
        if kind == "sibling_gather":
            _sibling_gather_copies(x_refs, xo_refs, x_sems, start, finish)
        else:
            _exchange_copies(x_refs, xo_refs, x_sems, kind == "scatter", start, finish)

    def full_body(*refs):
        ins, x_refs = refs[:n_in], refs[n_in:n_in + n]
        outs, xo_refs = refs[n_in + n:n_in + n + n_out], refs[n_in + n + n_out:n_in + 2 * n + n_out]
        scratch = refs[n_in + 2 * n + n_out:n_in + 2 * n + n_out + n_scratch]
        x_sems = refs[n_in + 2 * n + n_out + n_scratch:]

        @pl.when(first())
        def _():
            copies(x_refs, xo_refs, x_sems, True, False)

        body(*ins, *outs, *scratch)

        @pl.when(last())
        def _():
            copies(x_refs, xo_refs, x_sems, False, True)

    call = pl.pallas_call(full_body, name=name, grid=grid, in_specs=in_specs + hbm, out_specs=out_specs + hbm,
                          out_shape=out_shape + x_shapes, scratch_shapes=list(scratch_shapes) + sems,
                          compiler_params=_params(*semantics))

    def run(*args):
        res = call(*args, *bufs)
        return res[:n_out], res[n_out:]
    return run


def _adamw(w, g, m, v, name):
    r, c = w.shape
    parts = g.ndim == 3
    tr = _tile(r, 256, 16 if parts else 8)
    c1 = 1.0 - ADAM_B1 ** ADAM_STEP
    c2 = 1.0 - ADAM_B2 ** ADAM_STEP

    def body(w_ref, g_ref, m_ref, v_ref, go_ref, d_ref, mo_ref, vo_ref):
        if parts:
            g = g_ref[0].astype(F32)
            for q in range(1, N_DEV):
                g = g + g_ref[q].astype(F32)
        else:
            g = g_ref[...]
        m = ADAM_B1 * m_ref[...] + (1.0 - ADAM_B1) * g
        v = ADAM_B2 * v_ref[...] + (1.0 - ADAM_B2) * (g * g)
        go_ref[...] = g
        d_ref[...] = -ADAM_LR * ((m / c1) / (jnp.sqrt(v / c2) + ADAM_EPS) + ADAM_WD * w_ref[...])
        mo_ref[...] = m
        vo_ref[...] = v

    blk = pl.BlockSpec((tr, c), lambda i: (i, 0))
    g_spec = pl.BlockSpec((N_DEV, tr, c), lambda i: (0, i, 0)) if parts else blk
    return pl.pallas_call(
        body, name=name, grid=(r // tr,), in_specs=[blk, g_spec, blk, blk], out_specs=[blk] * 4,
        out_shape=[jax.ShapeDtypeStruct((r, c), F32)] * 4,
        compiler_params=_params("parallel"))(w, g, m, v)


def _win_segments(hv, nh):
    o_z, o_b = 3 * hv, 4 * hv
    o_r = o_b + 2 * nh
    return [(0, 0, 3 * hv), (3 * hv, o_r, 3 * hv), (6 * hv, o_z, hv), (7 * hv, o_r + 3 * hv, 3 * hv),
            (10 * hv, o_b, 2 * nh)]


def _win_from_shards(shards, hv, nh):
    _, d, cs = shards.shape
    pieces = []
    for _, src, width in _win_segments(hv, nh):
        lo = src
        while lo < src + width:
            p = lo // cs
            hi = min(src + width, (p + 1) * cs)
            pieces.append(shards[p][:, lo - p * cs:hi - p * cs])
            lo = hi
    pieces.append(jnp.zeros((d, LANES - 2 * nh), shards.dtype))
    return jnp.concatenate(pieces, axis=1)


def _win_grad_to_shards(parts, hv, nh, cs):
    segments = _win_segments(hv, nh)
    starts = [sum(p.shape[1] for p in parts[:i]) for i in range(len(parts))]

    def columns(a, b):
        out = []
        for part, start in zip(parts, starts):
            lo, hi = max(a, start), min(b, start + part.shape[1])
            if lo < hi:
                out.append(part[:, lo - start:hi - start])
        return out

    shards = []
    for p in range(N_DEV):
        pieces = []
        lo = p * cs
        while lo < (p + 1) * cs:
            here, src, width = next(s for s in segments if s[1] <= lo < s[1] + s[2])
            hi = min((p + 1) * cs, src + width)
            pieces += columns(here + lo - src, here + hi - src)
            lo = hi
        shards.append(jnp.concatenate(pieces, axis=1))
    return jnp.stack(shards)


def _rope_tables(tp):
    pos = jnp.arange(tp, dtype=F32) - float(PAD_FRONT)
    inv = 1.0 / (ROPE_BASE ** jnp.linspace(0.0, 1.0, HEAD_DIM // 2, dtype=F32))
    ang = pos[:, None] * inv[None, :]
    cos = jnp.repeat(jnp.cos(ang), 2, axis=1)
    sin = jnp.repeat(jnp.sin(ang), 2, axis=1) * jnp.tile(jnp.array([-1.0, 1.0], F32), HEAD_DIM // 2)[None, :]
    return cos, sin


def _retention_tables(nh):
    log_gamma = jnp.log1p(-jnp.exp2(-5.0 - jnp.arange(nh, dtype=F32)))
    pos = jnp.arange(CHUNK, dtype=F32)
    causal = pos[:, None] >= pos[None, :]
    diff = pos[:, None] - pos[None, :]
    dec = jnp.where(causal, jnp.exp(jnp.where(causal, diff, 0.0) * log_gamma[:, None, None]), 0.0)
    ones = jnp.ones((1, 1, HEAD_DIM), F32)
    xi = jnp.exp((pos + 1.0)[None, :] * log_gamma[:, None])[:, :, None] * ones
    zeta = jnp.exp((CHUNK - 1.0 - pos)[None, :] * log_gamma[:, None])[:, :, None] * ones
    cd = jnp.exp(CHUNK * log_gamma)[:, None, None] * jnp.ones((1, 8, HEAD_DIM), F32)
    return dec, xi, zeta, cd


SHARDED = ("meta_tokens", "ffn1_w_in", "ffn1_w_out", "w_in", "gdn_conv_w", "w_branch_gdn", "w_branch_ret",
           "w_out", "ffn2_w_in", "ffn2_w_out")
EXACT_F32 = ("meta_tokens", "gdn_conv_w")
REPLICATED = ("ffn1_norm", "mix_norm", "gdn_a_log", "gdn_dt_bias", "gdn_out_norm", "ret_out_norm", "ffn2_norm",
              "final_norm")
WEIGHTS = ("meta_tokens", "ffn1_norm", "ffn1_w_in", "ffn1_w_out", "mix_norm", "w_in", "gdn_conv_w", "gdn_a_log",
           "gdn_dt_bias", "gdn_out_norm", "ret_out_norm", "w_branch_gdn", "w_branch_ret", "w_out", "ffn2_norm",
           "ffn2_w_in", "ffn2_w_out", "final_norm")


def _as2d(a):
    if a.ndim == 3:
        return a[0]
    if a.ndim == 1:
        return a[None, :]
    return a


def _rows_of(shards):
    return shards.reshape(-1, shards.shape[2])


def _cols_of(shards):
    return shards.transpose(1, 0, 2).reshape(shards.shape[1], -1)


def _row_shards(a):
    return a.reshape(N_DEV, -1, a.shape[1])


def _col_shards(a):
    return a.reshape(a.shape[0], N_DEV, -1).transpose(1, 0, 2)


GATHER_FIRST = ("meta_tokens", "ffn1_w_in", "ffn1_w_out")
GATHER_BEHIND_FFN1 = ("w_in", "gdn_conv_w")
GATHER_BEHIND_PROJ = ("w_branch_gdn", "w_branch_ret", "w_out", "ffn2_w_in", "ffn2_w_out")
SCATTER_BEHIND_DN2 = ("ffn2_w_in", "ffn2_w_out", "w_branch_gdn", "w_branch_ret", "w_out")
SCATTER_BEHIND_FFN1 = ("w_in", "gdn_conv_w")
SCATTER_BEHIND_DWG = ("meta_tokens", "ffn1_w_out")
SCATTER_LAST = ("ffn1_w_in",)


def _device_step(x, target, send, rep):
    seq, d = x.shape
    tp = HEAD_ROWS + seq
    hv = d
    nh = hv // HEAD_DIM
    assert tp % (SCAN_CHUNKS * CHUNK) == 0 and tp % HEAD_ROWS == 0
    bf16_shards = lambda grads, names: [grads[n].astype(BF16) for n in names]

    pad_lanes = lambda row: jnp.pad(row, ((0, 0), (nh, LANES - 2 * nh)))
    alog = pad_lanes(rep["gdn_a_log"])
    dtb = pad_lanes(rep["gdn_dt_bias"])
    cos, sin = _rope_tables(tp)
    dec, xi, zeta, cd = _retention_tables(nh)

    got = dict(zip(GATHER_FIRST, _gather_via_sibling([send[n] for n in GATHER_FIRST], "gather_ffn1")))
    h0 = jnp.concatenate([jnp.zeros((PAD_FRONT, d), F32), _cols_of(got["meta_tokens"]), x], axis=0)
    f1i, f1o = got["ffn1_w_in"], _rows_of(got["ffn1_w_out"])
    (h1, hid1, dup1, dgate1), moved = _ffn_fwd(h0, rep["ffn1_norm"], f1i, f1o, "ffn1_fwd",
                                               ([send[n] for n in GATHER_BEHIND_FFN1], "sibling_gather"))
    got.update(zip(GATHER_BEHIND_FFN1, moved))
    wp = _win_from_shards(got["w_in"], hv, nh)
    conv_w = _cols_of(got["gdn_conv_w"])
    (proj, n2), moved = _proj_fwd(h1, rep["mix_norm"], wp, "proj_fwd",
                                  ([send[n] for n in GATHER_BEHIND_PROJ], "sibling_gather"))
    got.update(zip(GATHER_BEHIND_PROJ, moved))
    wbg, wbr, wo = _rows_of(got["w_branch_gdn"]), _rows_of(got["w_branch_ret"]), _rows_of(got["w_out"])
    f2i, f2o = got["ffn2_w_in"], _rows_of(got["ffn2_w_out"])
    qkv, conv_out = _conv_fwd(proj, conv_w, hv, "conv_fwd")
    oa, s_gdn, t_gdn = _gdn_fwd(qkv, proj, alog, dtb, nh, "gdn_fwd")
    ob, s_ret = _ret_fwd(proj, cos, sin, dec, xi, zeta, cd, nh, "ret_fwd")
    h2 = _post_fwd(oa, ob, proj, rep["gdn_out_norm"], rep["ret_out_norm"], wbg, wbr, wo, h1, "post_fwd")
    (h3, hid2, dup2, dgate2), _ = _ffn_fwd(h2, rep["ffn2_norm"], f2i, f2o, "ffn2_fwd")
    loss_row, dh3, d_final = _final(h3, rep["final_norm"], target, "final")

    (dh2, d_f2n, n3, dag2, dau2), _ = _ffn_bwd(h2, dh3, rep["ffn2_norm"], f2i, f2o, dup2, dgate2, "ffn2_bwd")
    grads = {"ffn2_w_in": jnp.concatenate([_matmul_tn_blocks(n3, dag2, "ffn2_dwg"),
                                           _matmul_tn_blocks(n3, dau2, "ffn2_dwu")]),
             "ffn2_w_out": _row_shards(_matmul_tn_blocks(hid2, dh3, "ffn2_dwo", 0.5))}

    doa, dob, dgate, ya, yb, merged, dpa, dpb, d_gn, d_rn = _post_bwd(
        oa, ob, proj, rep["gdn_out_norm"], rep["ret_out_norm"], wbg, wbr, wo, dh2, "post_bwd")
    grads["w_branch_gdn"] = _row_shards(_matmul_tn(ya, dpa, "dw_branch_gdn"))
    grads["w_branch_ret"] = _row_shards(_matmul_tn(yb, dpb, "dw_branch_ret"))
    grads["w_out"] = _row_shards(_matmul_tn(merged, dh2, "dw_out"))

    d_ret = _ret_bwd(proj, cos, sin, dec, xi, zeta, cd, s_ret, dob, nh, "ret_bwd")
    gdn_grads = _gdn_bwd(qkv, proj, alog, dtb, s_gdn, t_gdn, doa, nh, "gdn_bwd")
    dba, d_alog, d_dtb = gdn_grads[3:]
    dpre, g_conv = [], []
    for grp, tag in enumerate("qkv"):
        dx, dw = _conv_bwd(proj, conv_out, conv_w, gdn_grads[grp], grp, hv, "conv_bwd_" + tag)
        dpre.append(dx)
        g_conv.append(dw)
    grads["gdn_conv_w"] = _col_shards(jnp.concatenate(g_conv, axis=1))

    wide = dpre + list(d_ret) + [dgate]
    dn2, moved = _matmul_nt_parts(wide, wp[:, :10 * hv], "dn2_wide",
                                  (bf16_shards(grads, SCATTER_BEHIND_DN2), "scatter"))
    parts = dict(zip(SCATTER_BEHIND_DN2, moved))
    g_wp = [_matmul_tn(n2, dg, "dw_in_%d" % idx) for idx, dg in enumerate(wide + [dba])]
    grads["w_in"] = _win_grad_to_shards(g_wp, hv, nh, send["w_in"].shape[1])
    dh1, d_mixn = _norm_bwd(h1, rep["mix_norm"], dn2, dba, wp[:, 10 * hv:], dh2, "mix_norm_bwd")

    (dh0, d_f1n, n1, dag1, dau1), moved = _ffn_bwd(h0, dh1, rep["ffn1_norm"], f1i, f1o, dup1, dgate1, "ffn1_bwd",
                                                   (bf16_shards(grads, SCATTER_BEHIND_FFN1), "scatter"))
    parts.update(zip(SCATTER_BEHIND_FFN1, moved))
    grads["ffn1_w_out"] = _row_shards(_matmul_tn_blocks(hid1, dh1, "ffn1_dwo", 0.5))
    grads["meta_tokens"] = _col_shards(dh0[PAD_FRONT:HEAD_ROWS])
    g_gate, moved = _matmul_tn_blocks(n1, dag1, "ffn1_dwg", carry=(bf16_shards(grads, SCATTER_BEHIND_DWG), "scatter"))
    parts.update(zip(SCATTER_BEHIND_DWG, moved))
    grads["ffn1_w_in"] = jnp.concatenate([g_gate, _matmul_tn_blocks(n1, dau1, "ffn1_dwu")])
    parts.update(zip(SCATTER_LAST, _exchange(bf16_shards(grads, SCATTER_LAST), True, "scatter_ffn1")))

    small = {"ffn1_norm": d_f1n, "mix_norm": d_mixn, "gdn_a_log": d_alog[:, nh:2 * nh],
             "gdn_dt_bias": d_dtb[:, nh:2 * nh], "gdn_out_norm": d_gn, "ret_out_norm": d_rn, "ffn2_norm": d_f2n,
             "final_norm": d_final}
    return loss_row[0, 0], dh0[HEAD_ROWS:], parts, small


def kernel(x, meta_tokens, ffn1_norm, ffn1_w_in, ffn1_w_out, mix_norm, w_in, gdn_conv_w, gdn_a_log, gdn_dt_bias, gdn_out_norm, ret_out_norm, w_branch_gdn, w_branch_ret, w_out, ffn2_norm, ffn2_w_in, ffn2_w_out, final_norm, loss_target, m_meta_tokens, m_ffn1_norm, m_ffn1_w_in, m_ffn1_w_out, m_mix_norm, m_w_in, m_gdn_conv_w, m_gdn_a_log, m_gdn_dt_bias, m_gdn_out_norm, m_ret_out_norm, m_w_branch_gdn, m_w_branch_ret, m_w_out, m_ffn2_norm, m_ffn2_w_in, m_ffn2_w_out, m_final_norm, v_meta_tokens, v_ffn1_norm, v_ffn1_w_in, v_ffn1_w_out, v_mix_norm, v_w_in, v_gdn_conv_w, v_gdn_a_log, v_gdn_dt_bias, v_gdn_out_norm, v_ret_out_norm, v_w_branch_gdn, v_w_branch_ret, v_w_out, v_ffn2_norm, v_ffn2_w_in, v_ffn2_w_out, v_final_norm):
    given = dict(locals())
    params = {n: _as2d(given[n]) for n in WEIGHTS}
    local = {n: params[n] for n in SHARDED}
    rep = {n: params[n] for n in REPLICATED}

    send = {n: local[n] if n in EXACT_F32 else local[n].astype(BF16) for n in SHARDED}
    loss_sum, grad_x, parts, small = _device_step(x[0], loss_target[0], send, rep)
    parts.update(zip(REPLICATED, _exchange([small[n] for n in REPLICATED], False, "gather_small_grads")))
    loss = lax.psum(loss_sum, ("x", "y", "c"))

    outs = {}
    for n in WEIGHTS:
        res = _adamw(params[n], parts[n], _as2d(given["m_" + n]), _as2d(given["v_" + n]), "adamw_" + n)
        outs[n] = [r.reshape(given[n].shape) for r in res]
    return (loss, grad_x[None], *[outs[n][0] for n in WEIGHTS], *[outs[n][1] for n in WEIGHTS],
            *[outs[n][2] for n in WEIGHTS], *[outs[n][3] for n in WEIGHTS])
```

```python
import functools
import math

import jax
import jax.numpy as jnp
from jax import lax
from jax.experimental import pallas as pl
from jax.experimental.pallas import tpu as pltpu

F32 = jnp.float32
BF16 = jnp.bfloat16

N_DEV = 8
N_META = 16
CHUNK = 64
HEAD_DIM = 128
CONV_K = 4
ROPE_BASE = 10000.0
EPS = 1e-6
PAD_FRONT = 240
HEAD_ROWS = PAD_FRONT + N_META
LANES = 128
VMEM_LIMIT_BYTES = 56 * 1024 * 1024

ADAM_LR = 0.001
ADAM_B1 = 0.9
ADAM_B2 = 0.999
ADAM_EPS = 1e-08
ADAM_WD = 0.01
ADAM_STEP = 10

NN = (((1,), (0,)), ((), ()))
NT = (((1,), (1,)), ((), ()))
TN = (((0,), (0,)), ((), ()))


def _tile(n, target, mult):
    best = 0
    for t in range(mult, min(n, target) + 1, mult):
        if n % t == 0:
            best = t
    return best if best else n


def _params(*semantics):
    return pltpu.CompilerParams(dimension_semantics=semantics, vmem_limit_bytes=VMEM_LIMIT_BYTES)


def _split(a, pieces):
    out = []
    for _ in range(pieces - 1):
        part = a.astype(BF16)
        out.append(part)
        a = a - part.astype(F32)
    return out + [a.astype(BF16)]


def _raw_dot(a, b, dims, hi):
    dot = lambda x, y: lax.dot_general(x, y, dims, preferred_element_type=F32)
    if hi:
        (a_hi, a_lo), (b_hi, b_lo) = _split(a, 2), _split(b, 2)
        return dot(a_hi, b_hi) + (dot(a_hi, b_lo) + dot(a_lo, b_hi))
    return dot(a.astype(BF16), b.astype(BF16))


def _mask_dot(mask, x, dims):
    mask = mask.astype(BF16)
    hi, mid, lo = [lax.dot_general(mask, p, dims, preferred_element_type=F32) for p in _split(x, 3)]
    return hi + (mid + lo)


@jax.custom_vjp
def _cumsum_rows(x):
    c = x.shape[0]
    tril = lax.broadcasted_iota(jnp.int32, (c, c), 0) >= lax.broadcasted_iota(jnp.int32, (c, c), 1)
    return _mask_dot(tril, x, NN)


def _cumsum_rows_bwd(_, g):
    c = g.shape[0]
    tril = lax.broadcasted_iota(jnp.int32, (c, c), 0) >= lax.broadcasted_iota(jnp.int32, (c, c), 1)
    return (_mask_dot(tril, g, TN),)


_cumsum_rows.defvjp(lambda x: (_cumsum_rows(x), None), _cumsum_rows_bwd)


def _unit_lower_inverses(xs):
    c = xs[0].shape[0]
    eye = (lax.broadcasted_iota(jnp.int32, (c, c), 0) == lax.broadcasted_iota(jnp.int32, (c, c), 1)).astype(F32)
    t_inv = [eye + x for x in xs]
    for _ in range(int(math.log2(c)) - 1):
        xs = [_raw_dot(x, x, NN, True) for x in xs]
        t_inv = [t + _raw_dot(t, x, NN, True) for t, x in zip(t_inv, xs)]
    return t_inv


@jax.custom_vjp
def _known_inverse(x_neg, t_inv):
    return t_inv


_known_inverse.defvjp(
    lambda x_neg, t_inv: (t_inv, t_inv),
    lambda t_inv, g: (_raw_dot(_raw_dot(t_inv, g, TN, False), t_inv, NT, False), jnp.zeros_like(t_inv)))


def _make_mm(hi):
    @jax.custom_vjp
    def nn(a, b):
        return _raw_dot(a, b, NN, hi)

    @jax.custom_vjp
    def nt(a, b):
        return _raw_dot(a, b, NT, hi)

    @jax.custom_vjp
    def tn(a, b):
        return _raw_dot(a, b, TN, hi)

    nn.defvjp(lambda a, b: (_raw_dot(a, b, NN, hi), (a, b)),
              lambda r, g: (_raw_dot(g, r[1], NT, False), _raw_dot(r[0], g, TN, False)))
    nt.defvjp(lambda a, b: (_raw_dot(a, b, NT, hi), (a, b)),
              lambda r, g: (_raw_dot(g, r[1], NN, False), _raw_dot(g, r[0], TN, False)))
    tn.defvjp(lambda a, b: (_raw_dot(a, b, TN, hi), (a, b)),
              lambda r, g: (_raw_dot(r[1], g, NT, False), _raw_dot(r[0], g, NN, False)))
    return nn, nt, tn


def _silu(x):
    return x * jax.nn.sigmoid(x)


def _rms_parts(x):
    r = lax.rsqrt(jnp.mean(x * x, axis=-1, keepdims=True) + EPS)
    return x * r, r


def _rms_bwd(dy, xh, r, gain):
    dxh = dy * gain
    dx = r * (dxh - xh * jnp.mean(dxh * xh, axis=-1, keepdims=True))
    return dx, jnp.sum(dy * xh, axis=0, keepdims=True)


def _ffn_specs(tm, d, tf, nj):
    return [pl.BlockSpec((tm, d), lambda i, j: (i, 0)), pl.BlockSpec((1, d), lambda i, j: (0, 0)),
            pl.BlockSpec((1, d, tf), lambda i, j: (j, 0, 0)), pl.BlockSpec((1, d, tf), lambda i, j: (nj + j, 0, 0)),
            pl.BlockSpec((tf, d), lambda i, j: (j, 0))]


def _first_step(ndim):
    return lambda: functools.reduce(lambda a, b: a & b, [pl.program_id(k) == 0 for k in range(ndim)])


def _last_step(grid):
    return lambda: functools.reduce(lambda a, b: a & b, [pl.program_id(k) == g - 1 for k, g in enumerate(grid)])


def _ffn_fwd(h, gain, w_in, wo, name, carry=None):
    tp, d = h.shape
    tf = w_in.shape[2]
    nj = w_in.shape[0] // 2
    tm = _tile(tp, 768, 8)
    row, vec, wg_spec, wu_spec, wo_spec = _ffn_specs(tm, d, tf, nj)

    def body(h_ref, g_ref, wg3_ref, wu3_ref, wo_ref, o_ref, hid3_ref, dup3_ref, dgate3_ref, n_sc, acc_sc):
        wg_ref, wu_ref = wg3_ref.at[0], wu3_ref.at[0]
        j = pl.program_id(1)

        @pl.when(j == 0)
        def _():
            xh, _ = _rms_parts(h_ref[...])
            n_sc[...] = (xh * g_ref[...]).astype(BF16)
            acc_sc[...] = jnp.zeros_like(acc_sc)

        n = n_sc[...]
        a_g = jnp.dot(n, wg_ref[...], preferred_element_type=F32)
        a_u = jnp.dot(n, wu_ref[...], preferred_element_type=F32)
        sg = jax.nn.sigmoid(a_g)
        s = a_g * sg
        hid = (s * a_u).astype(BF16)
        hid3_ref[0] = hid
        dup3_ref[0] = s.astype(BF16)
        dgate3_ref[0] = (a_u * _dsilu(a_g, sg)).astype(BF16)
        acc_sc[...] += jnp.dot(hid, wo_ref[...], preferred_element_type=F32)

        @pl.when(j == nj - 1)
        def _():
            o_ref[...] = h_ref[...] + 0.5 * acc_sc[...]

    grid = (tp // tm, nj)
    act = pl.BlockSpec((1, tm, tf), lambda i, j: (j, i, 0))
    return _carried_call(
        body, carry, _first_step(2), _last_step(grid), name=name, grid=grid,
        in_specs=[row, vec, wg_spec, wu_spec, wo_spec], out_specs=[row, act, act, act],
        out_shape=[jax.ShapeDtypeStruct((tp, d), F32)] + [jax.ShapeDtypeStruct((nj, tp, tf), BF16)] * 3,
        scratch_shapes=[pltpu.VMEM((tm, d), BF16), pltpu.VMEM((tm, d), F32)])(h, gain, w_in, w_in, wo)


def _ffn_bwd(h, dho, gain, w_in, wo, dup3, dgate3, name, carry=None):
    tp, d = h.shape
    tf = w_in.shape[2]
    nj = w_in.shape[0] // 2
    tm = _tile(tp, 704, 16)
    ni = tp // tm
    row, vec, wg_spec, wu_spec, wo_spec = _ffn_specs(tm, d, tf, nj)

    def body(h_ref, dho_ref, g_ref, wg3_ref, wu3_ref, wo_ref, dup3_ref, dgate3_ref,
             dh_ref, dgain_ref, n_ref, dag3_ref, dau3_ref, dn_sc, dhb_sc):
        wg_ref, wu_ref = wg3_ref.at[0], wu3_ref.at[0]
        dag_ref, dau_ref = dag3_ref.at[0], dau3_ref.at[0]
        i, j = pl.program_id(0), pl.program_id(1)

        @pl.when(j == 0)
        def _():
            xh, _ = _rms_parts(h_ref[...])
            n_ref[...] = (xh * g_ref[...]).astype(BF16)
            dn_sc[...] = jnp.zeros_like(dn_sc)
            dhb_sc[...] = (0.5 * dho_ref[...]).astype(BF16)

        @pl.when((i == 0) & (j == 0))
        def _():
            dgain_ref[...] = jnp.zeros_like(dgain_ref)

        d_hid = lax.dot_general(dhb_sc[...], wo_ref[...], NT, preferred_element_type=F32)
        d_au = (d_hid * dup3_ref[0].astype(F32)).astype(BF16)
        d_ag = (d_hid * dgate3_ref[0].astype(F32)).astype(BF16)
        dau_ref[...] = d_au
        dag_ref[...] = d_ag
        dn_sc[...] += (lax.dot_general(d_ag, wg_ref[...], NT, preferred_element_type=F32)
                       + lax.dot_general(d_au, wu_ref[...], NT, preferred_element_type=F32))

        @pl.when(j == nj - 1)
        def _():
            xh, r = _rms_parts(h_ref[...])
            dx, dg = _rms_bwd(dn_sc[...], xh, r, g_ref[...])
            dh_ref[...] = dho_ref[...] + dx
            dgain_ref[...] += dg

    act = pl.BlockSpec((1, tm, tf), lambda i, j: (j, i, 0))
    return _carried_call(
        body, carry, _first_step(2), _last_step((ni, nj)), name=name, grid=(ni, nj),
        in_specs=[row, row, vec, wg_spec, wu_spec, wo_spec, act, act],
        out_specs=[row, vec, row, act, act],
        out_shape=[jax.ShapeDtypeStruct((tp, d), F32), jax.ShapeDtypeStruct((1, d), F32),
                   jax.ShapeDtypeStruct((tp, d), BF16)] + [jax.ShapeDtypeStruct((nj, tp, tf), BF16)] * 2,
        scratch_shapes=[pltpu.VMEM((tm, d), F32), pltpu.VMEM((tm, d), BF16)])(
            h, dho, gain, w_in, w_in, wo, dup3, dgate3)


def _matmul_tn(a, b, name, scale=1.0):
    t, m = a.shape
    n = b.shape[1]
    bm = _tile(m, 1024, LANES)
    bn = _tile(n, 1536, LANES)
    tk = _tile(t, 2816, 16)
    nk = t // tk

    def body(a_ref, b_ref, o_ref):
        k = pl.program_id(2)

        @pl.when(k == 0)
        def _():
            o_ref[...] = jnp.zeros_like(o_ref)

        o_ref[...] += lax.dot_general(a_ref[...].astype(BF16), b_ref[...].astype(BF16), TN,
                                      preferred_element_type=F32)

        if scale != 1.0:
            @pl.when(k == nk - 1)
            def _():
                o_ref[...] = o_ref[...] * scale

    return pl.pallas_call(
        body, name=name, grid=(m // bm, n // bn, nk),
        in_specs=[pl.BlockSpec((tk, bm), lambda i, j, k: (k, i)), pl.BlockSpec((tk, bn), lambda i, j, k: (k, j))],
        out_specs=pl.BlockSpec((bm, bn), lambda i, j, k: (i, j)),
        out_shape=jax.ShapeDtypeStruct((m, n), F32),
        compiler_params=_params("parallel", "parallel", "arbitrary"))(a, b)


def _matmul_tn_blocks(a, b, name, scale=1.0, carry=None):
    a_blocked = a.ndim == 3
    nb, t = (a.shape[0], a.shape[1]) if a_blocked else (b.shape[0], b.shape[1])
    m, n = a.shape[-1], b.shape[-1]
    tk = _tile(t, 2816, 16)
    nk = t // tk
    if a_blocked:
        bo = _tile(n, 1024, LANES)
        a_spec = pl.BlockSpec((1, tk, m), lambda p, o, k: (p, k, 0))
        b_spec = pl.BlockSpec((tk, bo), lambda p, o, k: (k, o))
        o_spec = pl.BlockSpec((m, bo), lambda p, o, k: (p, o))
        out_shape = jax.ShapeDtypeStruct((nb * m, n), F32)
        grid = (nb, n // bo, nk)
    else:
        bo = _tile(m, 1024, LANES)
        a_spec = pl.BlockSpec((tk, bo), lambda p, o, k: (k, o))
        b_spec = pl.BlockSpec((1, tk, n), lambda p, o, k: (p, k, 0))
        o_spec = pl.BlockSpec((1, bo, n), lambda p, o, k: (p, o, 0))
        out_shape = jax.ShapeDtypeStruct((nb, m, n), F32)
        grid = (nb, m // bo, nk)

    def body(a_ref, b_ref, o_ref):
        k = pl.program_id(2)
        a_blk = a_ref[0] if a_blocked else a_ref[...]
        b_blk = b_ref[...] if a_blocked else b_ref[0]
        part = lax.dot_general(a_blk.astype(BF16), b_blk.astype(BF16), TN, preferred_element_type=F32)
        out = o_ref if a_blocked else o_ref.at[0]

        @pl.when(k == 0)
        def _():
            out[...] = part

        @pl.when(k > 0)
        def _():
            out[...] += part

        if scale != 1.0:
            @pl.when(k == nk - 1)
            def _():
                out[...] = out[...] * scale

    (out,), moved = _carried_call(body, carry, _first_step(3), _last_step(grid), name=name, grid=grid,
                                  in_specs=[a_spec, b_spec], out_specs=[o_spec], out_shape=[out_shape])(a, b)
    return out if carry is None else (out, moved)


def _matmul_nt_parts(parts, w, name, carry=None):
    t = parts[0].shape[0]
    d = w.shape[0]
    widths = [p.shape[1] for p in parts]
    tk = _tile(math.gcd(*widths), 1024, LANES)
    counts = [wd // tk for wd in widths]
    starts = [sum(counts[:g]) for g in range(len(parts))]
    nk = sum(counts)
    tm = _tile(t, 1056, 16)
    n_parts = len(parts)

    def body(*refs):
        a_refs, w_ref, o_ref = refs[:n_parts], refs[n_parts], refs[-1]
        k = pl.program_id(1)

        @pl.when(k == 0)
        def _():
            o_ref[...] = jnp.zeros_like(o_ref)

        for g in range(n_parts):
            @pl.when((k >= starts[g]) & (k < starts[g] + counts[g]))
            def _(g=g):
                o_ref[...] += lax.dot_general(a_refs[g][...].astype(BF16), w_ref[...], NT,
                                              preferred_element_type=F32)

    in_specs = [pl.BlockSpec((tm, tk), lambda i, k, lo=starts[g], nb=counts[g]: (i, jnp.clip(k - lo, 0, nb - 1)))
                for g in range(n_parts)]
    in_specs.append(pl.BlockSpec((d, tk), lambda i, k: (0, k)))
    args = list(parts) + [w]
    grid = (t // tm, nk)
    (out,), moved = _carried_call(
        body, carry, _first_step(2), _last_step(grid), name=name, grid=grid, in_specs=in_specs,
        out_specs=[pl.BlockSpec((tm, d), lambda i, k: (i, 0))],
        out_shape=[jax.ShapeDtypeStruct((t, d), F32)])(*args)
    return out, moved


def _proj_fwd(h, gain, wp, name, carry=None):
    tp, d = h.shape
    npad = wp.shape[1]
    tm = _tile(tp, 768, 8)
    tn = _tile(npad, 3456, LANES)

    def body(h_ref, g_ref, w_ref, o_ref, n_ref):
        @pl.when(pl.program_id(1) == 0)
        def _():
            xh, _ = _rms_parts(h_ref[...])
            n_ref[...] = (xh * g_ref[...]).astype(BF16)

        o_ref[...] = jnp.dot(n_ref[...], w_ref[...], preferred_element_type=F32)

    grid = (tp // tm, npad // tn)
    return _carried_call(
        body, carry, _first_step(2), _last_step(grid), name=name, grid=grid,
        in_specs=[pl.BlockSpec((tm, d), lambda i, j: (i, 0)), pl.BlockSpec((1, d), lambda i, j: (0, 0)),
                  pl.BlockSpec((d, tn), lambda i, j: (0, j))],
        out_specs=[pl.BlockSpec((tm, tn), lambda i, j: (i, j)), pl.BlockSpec((tm, d), lambda i, j: (i, 0))],
        out_shape=[jax.ShapeDtypeStruct((tp, npad), F32), jax.ShapeDtypeStruct((tp, d), BF16)])(h, gain, wp)


def _norm_bwd(h, gain, dn, last, w_last, dres, name):
    tp, d = h.shape
    kl = last.shape[1]
    tm = _tile(tp, 256, 8)

    def body(h_ref, g_ref, dn_ref, last_ref, w_ref, dres_ref, dh_ref, dgain_ref):
        @pl.when(pl.program_id(0) == 0)
        def _():
            dgain_ref[...] = jnp.zeros_like(dgain_ref)

        dn_all = dn_ref[...] + lax.dot_general(last_ref[...].astype(BF16), w_ref[...], NT,
                                               preferred_element_type=F32)
        xh, r = _rms_parts(h_ref[...])
        dx, dg = _rms_bwd(dn_all, xh, r, g_ref[...])
        dh_ref[...] = dres_ref[...] + dx
        dgain_ref[...] += dg

    row = pl.BlockSpec((tm, d), lambda i: (i, 0))
    vec = pl.BlockSpec((1, d), lambda i: (0, 0))
    return pl.pallas_call(
        body, name=name, grid=(tp // tm,),
        in_specs=[row, vec, row, pl.BlockSpec((tm, kl), lambda i: (i, 0)), pl.BlockSpec((d, kl), lambda i: (0, 0)), row],
        out_specs=[row, vec],
        out_shape=[jax.ShapeDtypeStruct((tp, d), F32), jax.ShapeDtypeStruct((1, d), F32)],
        compiler_params=_params("arbitrary"))(h, gain, dn, last, w_last, dres)


def _head_post(a, grp):
    a = _silu(a)
    r = lax.rsqrt(jnp.sum(a * a, axis=-1, keepdims=True) + EPS)
    if isinstance(grp, int):
        return a if grp == 2 else a * r * (HEAD_DIM ** -0.5 if grp == 0 else 1.0)
    scale = jnp.where(grp == 0, HEAD_DIM ** -0.5, 1.0).astype(F32)
    return jnp.where(grp == 2, a, a * r * scale)


def _head_post_bwd(c, dy, grp):
    sg = jax.nn.sigmoid(c)
    a = c * sg
    dsilu = sg * (1.0 + c * (1.0 - sg))
    if grp == 2:
        return dy * dsilu
    r = lax.rsqrt(jnp.sum(a * a, axis=-1, keepdims=True) + EPS)
    scale = HEAD_DIM ** -0.5 if grp == 0 else 1.0
    da = (scale * r) * (dy - a * (r * r * jnp.sum(dy * a, axis=-1, keepdims=True)))
    return da * dsilu


def _conv_taps(ext_sc, w_ref, tm):
    ext = ext_sc[...]
    c = w_ref[CONV_K - 1:CONV_K, :] * ext[8:, :]
    for i in range(CONV_K - 1):
        s = CONV_K - 1 - i
        c = c + w_ref[i:i + 1, :] * pltpu.roll(ext, s, 0)[8:, :]
    return c


def _conv_fwd(proj, conv_w, hv, name):
    tp = proj.shape[0]
    tm = _tile(tp, 768, 8)
    nh = hv // HEAD_DIM

    def body(x_ref, halo_ref, w_ref, o_ref, c_ref, ext_sc):
        i, grp = pl.program_id(0), pl.program_id(1)
        ext_sc[0:8, :] = jnp.where(i == 0, 0.0, halo_ref[...])
        ext_sc[8:, :] = x_ref[...]
        c_ref[...] = _conv_taps(ext_sc, w_ref, tm)
        for h in range(nh):
            sl = slice(h * HEAD_DIM, (h + 1) * HEAD_DIM)
            o_ref[:, sl] = _head_post(c_ref[:, sl], grp)

    blk = pl.BlockSpec((tm, hv), lambda i, g: (i, g))
    return pl.pallas_call(
        body, name=name, grid=(tp // tm, 3),
        in_specs=[blk, pl.BlockSpec((8, hv), lambda i, g: (jnp.maximum(i * (tm // 8) - 1, 0), g)),
                  pl.BlockSpec((CONV_K, hv), lambda i, g: (0, g))],
        out_specs=[blk, blk],
        out_shape=[jax.ShapeDtypeStruct((tp, 3 * hv), F32)] * 2,
        scratch_shapes=[pltpu.VMEM((tm + 8, hv), F32)],
        compiler_params=_params("parallel", "arbitrary"))(proj, proj, conv_w)


def _conv_bwd(proj, conv_out, conv_w, dy, grp, hv, name):
    tp = proj.shape[0]
    tm = _tile(tp, 768, 8)
    ni = tp // tm
    nh = hv // HEAD_DIM

    def body(x_ref, c_ref, w_ref, dy_ref, dx_ref, dw_ref, dc_sc):
        step = pl.program_id(0)

        @pl.when(step == 0)
        def _():
            dc_sc[tm:, :] = jnp.zeros((8, hv), F32)
            dw_ref[...] = jnp.zeros_like(dw_ref)

        @pl.when(step > 0)
        def _():
            dc_sc[tm:, :] = dc_sc[0:8, :]

        for h in range(nh):
            sl = slice(h * HEAD_DIM, (h + 1) * HEAD_DIM)
            dc_sc[0:tm, sl] = _head_post_bwd(c_ref[:, sl], dy_ref[:, sl], grp)

        x = x_ref[...]
        dc_ext = dc_sc[...]
        dx = None
        for k in range(CONV_K):
            s = CONV_K - 1 - k
            shifted = dc_ext[0:tm, :] if s == 0 else pltpu.roll(dc_ext, tm + 8 - s, 0)[0:tm, :]
            dw_ref[k:k + 1, :] += jnp.sum(shifted * x, axis=0, keepdims=True)
            term = w_ref[k:k + 1, :] * shifted
            dx = term if dx is None else dx + term
        dx_ref[...] = dx.astype(BF16)

    tile = lambda step: ni - 1 - step
    grp_blk = pl.BlockSpec((tm, hv), lambda s: (tile(s), grp))
    own_blk = pl.BlockSpec((tm, hv), lambda s: (tile(s), 0))
    return pl.pallas_call(
        body, name=name, grid=(ni,),
        in_specs=[grp_blk, grp_blk, pl.BlockSpec((CONV_K, hv), lambda s: (0, grp)), own_blk],
        out_specs=[own_blk, pl.BlockSpec((CONV_K, hv), lambda s: (0, 0))],
        out_shape=[jax.ShapeDtypeStruct((tp, hv), BF16), jax.ShapeDtypeStruct((CONV_K, hv), F32)],
        scratch_shapes=[pltpu.VMEM((tm + 8, hv), F32)],
        compiler_params=_params("arbitrary"))(proj, conv_out, conv_w, dy)


def _gdn_gates(ba, alog, dtb):
    x = ba + dtb
    softplus = jnp.maximum(x, 0.0) + jnp.log1p(jnp.exp(-jnp.abs(x)))
    return _cumsum_rows(-jnp.exp(alog) * softplus), jax.nn.sigmoid(ba)


def _gdn_chunks(states, qs, ks, vs, gates, known_inverses=None):
    mm_nn, mm_nt, mm_tn = _make_mm(False)
    hi_nn, _, _ = _make_mm(True)
    nh = len(states)
    items = range(len(qs))
    head = [i % nh for i in items]
    c = qs[0].shape[0]
    lane = lax.broadcasted_iota(jnp.int32, (c, LANES), 1)
    last_row = (lax.broadcasted_iota(jnp.int32, (c, 1), 0) == c - 1).astype(F32)
    ri = lax.broadcasted_iota(jnp.int32, (c, c), 0)
    ci = lax.broadcasted_iota(jnp.int32, (c, c), 1)
    causal = ri >= ci
    strict = ri > ci
    eye = (ri == ci).astype(F32)
    sel_a = [(lane == nh + h).astype(F32) for h in range(nh)]
    sel_b = [(lane == h).astype(F32) for h in range(nh)]

    gcol = [jnp.sum(gates[i // nh][0] * sel_a[head[i]], axis=1, keepdims=True) for i in items]
    grow = [jnp.sum(eye * gcol[i], axis=0, keepdims=True) for i in items]
    beta = [jnp.sum(gates[i // nh][1] * sel_b[head[i]], axis=1, keepdims=True) for i in items]
    decay = [jnp.where(causal, jnp.exp(jnp.where(causal, gcol[i] - grow[i], 0.0)), 0.0) for i in items]
    kb = [ks[i] * beta[i] for i in items]
    kk = [mm_nt(kb[i], ks[i]) for i in items]
    qk = [mm_nt(qs[i], ks[i]) for i in items]
    x_neg = [-jnp.where(strict, kk[i] * decay[i], 0.0) for i in items]
    if known_inverses is None:
        t_inv = _unit_lower_inverses(x_neg)
    else:
        t_inv = [_known_inverse(x_neg[i], known_inverses[i]) for i in items]
    eg = [jnp.exp(gcol[i]) for i in items]
    u = [hi_nn(t_inv[i], vs[i] * beta[i]) for i in items]
    w = [hi_nn(t_inv[i], kb[i] * eg[i]) for i in items]
    qk = [qk[i] * decay[i] for i in items]
    glast = [jnp.sum(gcol[i] * last_row, axis=0, keepdims=True) for i in items]
    q_dec = [qs[i] * eg[i] for i in items]
    k_dec = [ks[i] * jnp.exp(glast[i] - gcol[i]) for i in items]
    s_dec = [jnp.exp(glast[i]) for i in items]

    outs = []
    for first in range(0, len(qs), nh):
        chunk = range(first, first + nh)
        ws = [mm_nn(w[i], states[i - first]) for i in chunk]
        from_state = [mm_nn(q_dec[i], states[i - first]) for i in chunk]
        v_new = [u[i] - ws[i - first] for i in chunk]
        intra = [mm_nn(qk[i], v_new[i - first]) for i in chunk]
        kv = [mm_tn(k_dec[i], v_new[i - first]) for i in chunk]
        outs += [from_state[i - first] + intra[i - first] for i in chunk]
        states = [states[i - first] * s_dec[i] + kv[i - first] for i in chunk]
    return outs, states, t_inv


SCAN_CHUNKS = 4


def _scan_specs(nh, steps, rev, first_col):
    sidx = (lambda s: steps - 1 - s) if rev else (lambda s: s)
    hv = nh * HEAD_DIM
    rows = SCAN_CHUNKS * CHUNK
    cols = [pl.BlockSpec((rows, hv), lambda s, g=g: (sidx(s), first_col + g)) for g in range(3)]
    st = pl.BlockSpec((1, nh, HEAD_DIM, HEAD_DIM), lambda s: (sidx(s), 0, 0, 0))
    act = pl.BlockSpec((rows, hv), lambda s: (sidx(s), 0))
    return cols, st, act


def _chunk_heads(ref, nh):
    return [ref[j * CHUNK:(j + 1) * CHUNK, h * HEAD_DIM:(h + 1) * HEAD_DIM] for j in range(SCAN_CHUNKS)
            for h in range(nh)]


def _store_chunk_heads(ref, values, nh, dtype=None):
    for i, val in enumerate(values):
        j, h = divmod(i, nh)
        ref[j * CHUNK:(j + 1) * CHUNK, h * HEAD_DIM:(h + 1) * HEAD_DIM] = val if dtype is None else val.astype(dtype)


def _gdn_fwd(qkv, proj, alog, dtb, nh):
    tp = qkv.shape[0]
    steps = tp // (SCAN_CHUNKS * CHUNK)
    rows = SCAN_CHUNKS * CHUNK

    def body(q_ref, k_ref, v_ref, ba_ref, al_ref, dt_ref, o_ref, st_ref, inv_ref, s_sc):
        @pl.when(pl.program_id(0) == 0)
        def _():
            s_sc[...] = jnp.zeros_like(s_sc)

        gates = [_gdn_gates(ba_ref[j * CHUNK:(j + 1) * CHUNK, :], al_ref[...], dt_ref[...])
                 for j in range(SCAN_CHUNKS)]
        states = [s_sc[h] for h in range(nh)]
        for h in range(nh):
            st_ref[0, h] = states[h]
        outs, new_states, t_inv = _gdn_chunks(states, _chunk_heads(q_ref, nh), _chunk_heads(k_ref, nh),
                                              _chunk_heads(v_ref, nh), gates)
        _store_chunk_heads(o_ref, outs, nh)
        for h in range(nh):
            s_sc[h] = new_states[h]
        for i, t in enumerate(t_inv):
            inv_ref[0, i] = t

    cols, st, act = _scan_specs(nh, steps, False, 0)
    ba = pl.BlockSpec((rows, LANES), lambda s: (s, 10 * nh * HEAD_DIM // LANES))
    vec = pl.BlockSpec((1, LANES), lambda s: (0, 0))
    inv = pl.BlockSpec((1, SCAN_CHUNKS * nh, CHUNK, CHUNK), lambda s: (s, 0, 0, 0))
    return dict(
        body=body, grid=(steps,), in_specs=cols + [ba, vec, vec], out_specs=[act, st, inv],
        out_shape=[jax.ShapeDtypeStruct((tp, nh * HEAD_DIM), F32),
                   jax.ShapeDtypeStruct((steps, nh, HEAD_DIM, HEAD_DIM), F32),
                   jax.ShapeDtypeStruct((steps, SCAN_CHUNKS * nh, CHUNK, CHUNK), F32)],
        scratch_shapes=[pltpu.VMEM((nh, HEAD_DIM, HEAD_DIM), F32)], args=(qkv, qkv, qkv, proj, alog, dtb))


def _run_together(parts, name):
    grid = parts[0]["grid"]
    assert all(p["grid"] == grid for p in parts)
    counts = [(len(p["in_specs"]), len(p["out_specs"]), len(p["scratch_shapes"])) for p in parts]
    n_in, n_out = sum(c[0] for c in counts), sum(c[1] for c in counts)

    def body(*refs):
        ins, outs, scratch = refs[:n_in], refs[n_in:n_in + n_out], refs[n_in + n_out:]
        at = [0, 0, 0]
        for p, (ci, co, cs) in zip(parts, counts):
            p["body"](*ins[at[0]:at[0] + ci], *outs[at[1]:at[1] + co], *scratch[at[2]:at[2] + cs])
            at = [at[0] + ci, at[1] + co, at[2] + cs]

    flat = lambda key: [x for p in parts for x in p[key]]
    res = pl.pallas_call(
        body, name=name, grid=grid, in_specs=flat("in_specs"), out_specs=flat("out_specs"),
        out_shape=flat("out_shape"), scratch_shapes=flat("scratch_shapes"),
        compiler_params=_params(*(("arbitrary",) * len(grid))))(*flat("args"))
    out, at = [], 0
    for _, co, _ in counts:
        out.append(res[at:at + co])
        at += co
    return out


def _gdn_bwd(qkv, proj, alog, dtb, states, inverses, do, nh):
    tp = qkv.shape[0]
    steps = tp // (SCAN_CHUNKS * CHUNK)
    rows = SCAN_CHUNKS * CHUNK

    def body(q_ref, k_ref, v_ref, ba_ref, al_ref, dt_ref, st_ref, inv_ref, do_ref,
             dq_ref, dk_ref, dv_ref, dba_ref, dal_ref, ddt_ref, ds_sc):
        @pl.when(pl.program_id(0) == 0)
        def _():
            ds_sc[...] = jnp.zeros_like(ds_sc)
            dal_ref[...] = jnp.zeros_like(dal_ref)
            ddt_ref[...] = jnp.zeros_like(ddt_ref)

        gates, gates_vjps = [], []
        for j in range(SCAN_CHUNKS):
            g, g_vjp = jax.vjp(_gdn_gates, ba_ref[j * CHUNK:(j + 1) * CHUNK, :], al_ref[...], dt_ref[...])
            gates.append(g)
            gates_vjps.append(g_vjp)
        known = [inv_ref[0, i] for i in range(SCAN_CHUNKS * nh)]
        fn = lambda s, q, k, v, g: _gdn_chunks(s, q, k, v, g, known)[:2]
        _, vjp = jax.vjp(fn, [st_ref[0, h] for h in range(nh)], _chunk_heads(q_ref, nh), _chunk_heads(k_ref, nh),
                         _chunk_heads(v_ref, nh), gates)
        ds, dq, dk, dv, dgates = vjp((_chunk_heads(do_ref, nh), [ds_sc[h] for h in range(nh)]))
        for h in range(nh):
            ds_sc[h] = ds[h]
        _store_chunk_heads(dq_ref, dq, nh)
        _store_chunk_heads(dk_ref, dk, nh)
        _store_chunk_heads(dv_ref, dv, nh)
        for j in range(SCAN_CHUNKS):
            dba, dal, ddt = gates_vjps[j](dgates[j])
            dba_ref[j * CHUNK:(j + 1) * CHUNK, :] = dba
            dal_ref[...] += dal
            ddt_ref[...] += ddt

    cols, st, act = _scan_specs(nh, steps, True, 0)
    ba = pl.BlockSpec((rows, LANES), lambda s: (steps - 1 - s, 10 * nh * HEAD_DIM // LANES))
    vec = pl.BlockSpec((1, LANES), lambda s: (0, 0))
    inv = pl.BlockSpec((1, SCAN_CHUNKS * nh, CHUNK, CHUNK), lambda s: (steps - 1 - s, 0, 0, 0))
    return dict(
        body=body, grid=(steps,), in_specs=cols + [ba, vec, vec, st, inv, act],
        out_specs=[act, act, act, pl.BlockSpec((rows, LANES), lambda s: (steps - 1 - s, 0)), vec, vec],
        out_shape=[jax.ShapeDtypeStruct((tp, nh * HEAD_DIM), F32)] * 3
                  + [jax.ShapeDtypeStruct((tp, LANES), F32), jax.ShapeDtypeStruct((1, LANES), F32),
                     jax.ShapeDtypeStruct((1, LANES), F32)],
        scratch_shapes=[pltpu.VMEM((nh, HEAD_DIM, HEAD_DIM), F32)],
        args=(qkv, qkv, qkv, proj, alog, dtb, states, inverses, do))


def _swap_pairs(t):
    lane = lax.broadcasted_iota(jnp.int32, t.shape, 1)
    n = t.shape[1]
    return jnp.where(lane % 2 == 0, pltpu.roll(t, n - 1, 1), pltpu.roll(t, 1, 1))


def _rot(t, cos, sin_signed):
    return t * cos + _swap_pairs(t) * sin_signed


def _rot_t(dt, cos, sin_signed):
    return dt * cos + _swap_pairs(dt * sin_signed)


def _ret_chunks(states, qs, ks, vs, dec, xi, zeta, cd):
    mm_nn, mm_nt, mm_tn = _make_mm(False)
    nh = len(states)
    items = range(len(qs))
    scores = [mm_nt(qs[i], ks[i]) for i in items]
    kv = [mm_tn(ks[i] * zeta[i % nh], vs[i]) for i in items]
    intra = [mm_nn(scores[i] * dec[i % nh], vs[i]) for i in items]
    q_dec = [qs[i] * xi[i % nh] for i in items]
    outs = []
    for first in range(0, len(qs), nh):
        outs += [intra[first + h] + mm_nn(q_dec[first + h], states[h]) for h in range(nh)]
        states = [states[h] * cd[h] + kv[first + h] for h in range(nh)]
    return outs, states


def _ret_table_specs(nh, steps, rev):
    sidx = (lambda s: steps - 1 - s) if rev else (lambda s: s)
    rope = pl.BlockSpec((SCAN_CHUNKS * CHUNK, HEAD_DIM), lambda s: (sidx(s), 0))
    dec = pl.BlockSpec((nh, CHUNK, CHUNK), lambda s: (0, 0, 0))
    tab = pl.BlockSpec((nh, CHUNK, HEAD_DIM), lambda s: (0, 0, 0))
    cd = pl.BlockSpec((nh, 8, HEAD_DIM), lambda s: (0, 0, 0))
    return [rope, rope, dec, tab, tab, cd]


def _rotated(ref, cos_ref, sin_ref, nh, scale=1.0):
    out = []
    for j in range(SCAN_CHUNKS):
        rows = slice(j * CHUNK, (j + 1) * CHUNK)
        cos_t, sin_t = cos_ref[rows, :], sin_ref[rows, :]
        for h in range(nh):
            t = _rot(ref[rows, h * HEAD_DIM:(h + 1) * HEAD_DIM], cos_t, sin_t)
            out.append(t if scale == 1.0 else t * scale)
    return out


def _ret_fwd(proj, cos, sin, dec, xi, zeta, cd, nh):
    tp = proj.shape[0]
    steps = tp // (SCAN_CHUNKS * CHUNK)
    kscale = HEAD_DIM ** -0.5

    def body(q_ref, k_ref, v_ref, cos_ref, sin_ref, dec_ref, xi_ref, zeta_ref, cd_ref, o_ref, st_ref, s_sc):
        @pl.when(pl.program_id(0) == 0)
        def _():
            s_sc[...] = jnp.zeros_like(s_sc)

        heads = range(nh)
        states = [s_sc[h] for h in heads]
        for h in heads:
            st_ref[0, h] = states[h]
        outs, new_states = _ret_chunks(
            states, _rotated(q_ref, cos_ref, sin_ref, nh), _rotated(k_ref, cos_ref, sin_ref, nh, kscale),
            _chunk_heads(v_ref, nh), [dec_ref[h] for h in heads], [xi_ref[h] for h in heads],
            [zeta_ref[h] for h in heads], [cd_ref[h][0:1, :] for h in heads])
        _store_chunk_heads(o_ref, outs, nh)
        for h in heads:
            s_sc[h] = new_states[h]

    cols, st, act = _scan_specs(nh, steps, False, 3)
    return dict(
        body=body, grid=(steps,), in_specs=cols + _ret_table_specs(nh, steps, False), out_specs=[act, st],
        out_shape=[jax.ShapeDtypeStruct((tp, nh * HEAD_DIM), F32),
                   jax.ShapeDtypeStruct((steps, nh, HEAD_DIM, HEAD_DIM), F32)],
        scratch_shapes=[pltpu.VMEM((nh, HEAD_DIM, HEAD_DIM), F32)],
        args=(proj, proj, proj, cos, sin, dec, xi, zeta, cd))


def _ret_bwd(proj, cos, sin, dec, xi, zeta, cd, states, do, nh):
    tp = proj.shape[0]
    steps = tp // (SCAN_CHUNKS * CHUNK)
    kscale = HEAD_DIM ** -0.5

    def body(q_ref, k_ref, v_ref, cos_ref, sin_ref, dec_ref, xi_ref, zeta_ref, cd_ref, st_ref, do_ref,
             dq_ref, dk_ref, dv_ref, ds_sc):
        @pl.when(pl.program_id(0) == 0)
        def _():
            ds_sc[...] = jnp.zeros_like(ds_sc)

        heads = range(nh)
        fn = functools.partial(_ret_chunks, dec=[dec_ref[h] for h in heads], xi=[xi_ref[h] for h in heads],
                               zeta=[zeta_ref[h] for h in heads], cd=[cd_ref[h][0:1, :] for h in heads])
        _, vjp = jax.vjp(fn, [st_ref[0, h] for h in heads], _rotated(q_ref, cos_ref, sin_ref, nh),
                         _rotated(k_ref, cos_ref, sin_ref, nh, kscale), _chunk_heads(v_ref, nh))
        ds, dq, dk, dv = vjp((_chunk_heads(do_ref, nh), [ds_sc[h] for h in heads]))
        for h in heads:
            ds_sc[h] = ds[h]
        for i in range(SCAN_CHUNKS * nh):
            rows = slice((i // nh) * CHUNK, (i // nh + 1) * CHUNK)
            cos_t, sin_t = cos_ref[rows, :], sin_ref[rows, :]
            dq[i] = _rot_t(dq[i], cos_t, sin_t)
            dk[i] = _rot_t(dk[i] * kscale, cos_t, sin_t)
        _store_chunk_heads(dq_ref, dq, nh, BF16)
        _store_chunk_heads(dk_ref, dk, nh, BF16)
        _store_chunk_heads(dv_ref, dv, nh, BF16)

    cols, st, act = _scan_specs(nh, steps, True, 3)
    return dict(
        body=body, grid=(steps,), in_specs=cols + _ret_table_specs(nh, steps, True) + [st, act],
        out_specs=[act, act, act],
        out_shape=[jax.ShapeDtypeStruct((tp, nh * HEAD_DIM), BF16)] * 3,
        scratch_shapes=[pltpu.VMEM((nh, HEAD_DIM, HEAD_DIM), F32)],
        args=(proj, proj, proj, cos, sin, dec, xi, zeta, cd, states, do))


def _gdn_out(o, z, gnorm):
    return o * lax.rsqrt(jnp.mean(o * o, axis=-1, keepdims=True) + EPS) * gnorm * _silu(z)


def _ret_out(o, rg, rnorm):
    mu = jnp.mean(o, axis=-1, keepdims=True)
    var = jnp.mean(jnp.square(o - mu), axis=-1, keepdims=True)
    return _silu(rg) * ((o - mu) * lax.rsqrt(var + EPS) * rnorm)


def _dsilu(x, sg):
    return sg * (1.0 + x * (1.0 - sg))


def _gdn_out_bwd(o, z, gnorm, dy):
    r = lax.rsqrt(jnp.mean(o * o, axis=-1, keepdims=True) + EPS)
    xh = o * r
    sg = jax.nn.sigmoid(z)
    sz = z * sg
    t = dy * (gnorm * sz)
    do = r * (t - xh * jnp.mean(t * xh, axis=-1, keepdims=True))
    e = dy * xh
    return do, e * (gnorm * _dsilu(z, sg)), jnp.sum(e * sz, axis=0, keepdims=True)


def _ret_out_bwd(o, rg, rnorm, dy):
    oc = o - jnp.mean(o, axis=-1, keepdims=True)
    rs = lax.rsqrt(jnp.mean(oc * oc, axis=-1, keepdims=True) + EPS)
    xh = oc * rs
    sg = jax.nn.sigmoid(rg)
    srg = rg * sg
    t = dy * (rnorm * srg)
    do = rs * (t - jnp.mean(t, axis=-1, keepdims=True) - xh * jnp.mean(t * xh, axis=-1, keepdims=True))
    e = dy * xh
    return do, e * (rnorm * _dsilu(rg, sg)), jnp.sum(e * srg, axis=0, keepdims=True)


def _post_specs(tm, hv, d):
    row = lambda col: pl.BlockSpec((tm, hv), lambda i: (i, col))
    return dict(
        oa=row(0), ob=row(0), z=row(6), rg=row(7), ga=row(8), gb=row(9),
        gnorm=pl.BlockSpec((1, HEAD_DIM), lambda i: (0, 0)), rnorm=pl.BlockSpec((1, hv), lambda i: (0, 0)),
        w=pl.BlockSpec((hv, d), lambda i: (0, 0)), res=pl.BlockSpec((tm, d), lambda i: (i, 0)))


def _post_fwd(oa, ob, proj, gnorm, rnorm, wbg, wbr, wo, h1, name):
    tp, d = h1.shape
    hv = oa.shape[1]
    nh = hv // HEAD_DIM
    tm = _tile(tp, 256, 8)

    def body(oa_ref, ob_ref, z_ref, rg_ref, ga_ref, gb_ref, gn_ref, rn_ref, wbg_ref, wbr_ref, wo_ref, h_ref,
             o_ref, ya_sc, yb_sc):
        for h in range(nh):
            sl = slice(h * HEAD_DIM, (h + 1) * HEAD_DIM)
            ya_sc[:, sl] = _gdn_out(oa_ref[:, sl], z_ref[:, sl], gn_ref[...]).astype(BF16)
            yb_sc[:, sl] = _ret_out(ob_ref[:, sl], rg_ref[:, sl], rn_ref[:, sl]).astype(BF16)
        pa = jnp.dot(ya_sc[...], wbg_ref[...], preferred_element_type=F32)
        pb = jnp.dot(yb_sc[...], wbr_ref[...], preferred_element_type=F32)
        merged = jax.nn.sigmoid(ga_ref[...]) * pa + jax.nn.sigmoid(gb_ref[...]) * pb
        o_ref[...] = h_ref[...] + jnp.dot(merged.astype(BF16), wo_ref[...], preferred_element_type=F32)

    sp = _post_specs(tm, hv, d)
    return pl.pallas_call(
        body, name=name, grid=(tp // tm,),
        in_specs=[sp["oa"], sp["ob"], sp["z"], sp["rg"], sp["ga"], sp["gb"], sp["gnorm"], sp["rnorm"],
                  sp["w"], sp["w"], sp["w"], sp["res"]],
        out_specs=sp["res"], out_shape=jax.ShapeDtypeStruct((tp, d), F32),
        scratch_shapes=[pltpu.VMEM((tm, hv), BF16), pltpu.VMEM((tm, hv), BF16)],
        compiler_params=_params("parallel"))(oa, ob, proj, proj, proj, proj, gnorm, rnorm, wbg, wbr, wo, h1)


def _post_bwd(oa, ob, proj, gnorm, rnorm, wbg, wbr, wo, dh2, name):
    tp, d = dh2.shape
    hv = oa.shape[1]
    nh = hv // HEAD_DIM
    tm = _tile(tp, 256, 8)

    def body(oa_ref, ob_ref, z_ref, rg_ref, ga_ref, gb_ref, gn_ref, rn_ref, wbg_ref, wbr_ref, wo_ref, dh_ref,
             doa_ref, dob_ref, dg_ref, ya_ref, yb_ref, mg_ref, dpa_ref, dpb_ref, dgn_ref, drn_ref,
             dya_sc, dyb_sc):
        @pl.when(pl.program_id(0) == 0)
        def _():
            dgn_ref[...] = jnp.zeros_like(dgn_ref)
            drn_ref[...] = jnp.zeros_like(drn_ref)

        for h in range(nh):
            sl = slice(h * HEAD_DIM, (h + 1) * HEAD_DIM)
            ya_ref[:, sl] = _gdn_out(oa_ref[:, sl], z_ref[:, sl], gn_ref[...]).astype(BF16)
            yb_ref[:, sl] = _ret_out(ob_ref[:, sl], rg_ref[:, sl], rn_ref[:, sl]).astype(BF16)
        pa = jnp.dot(ya_ref[...], wbg_ref[...], preferred_element_type=F32)
        pb = jnp.dot(yb_ref[...], wbr_ref[...], preferred_element_type=F32)
        sa = jax.nn.sigmoid(ga_ref[...])
        sb = jax.nn.sigmoid(gb_ref[...])
        mg_ref[...] = (sa * pa + sb * pb).astype(BF16)
        dm = lax.dot_general(dh_ref[...].astype(BF16), wo_ref[...], NT, preferred_element_type=F32)
        dpa = (dm * sa).astype(BF16)
        dpb = (dm * sb).astype(BF16)
        dpa_ref[...] = dpa
        dpb_ref[...] = dpb
        dg_ref[:, 2 * hv:3 * hv] = (dm * pa * sa * (1.0 - sa)).astype(BF16)
        dg_ref[:, 3 * hv:4 * hv] = (dm * pb * sb * (1.0 - sb)).astype(BF16)
        dya_sc[...] = lax.dot_general(dpa, wbg_ref[...], NT, preferred_element_type=F32)
        dyb_sc[...] = lax.dot_general(dpb, wbr_ref[...], NT, preferred_element_type=F32)
        for h in range(nh):
            sl = slice(h * HEAD_DIM, (h + 1) * HEAD_DIM)
            doa, dz, dgn = _gdn_out_bwd(oa_ref[:, sl], z_ref[:, sl], gn_ref[...], dya_sc[:, sl])
            doa_ref[:, sl] = doa
            dg_ref[:, sl] = dz.astype(BF16)
            dgn_ref[...] += dgn
            dob, drg, drn = _ret_out_bwd(ob_ref[:, sl], rg_ref[:, sl], rn_ref[:, sl], dyb_sc[:, sl])
            dob_ref[:, sl] = dob
            dg_ref[:, hv + h * HEAD_DIM:hv + (h + 1) * HEAD_DIM] = drg.astype(BF16)
            drn_ref[:, sl] += drn

    sp = _post_specs(tm, hv, d)
    act = pl.BlockSpec((tm, hv), lambda i: (i, 0))
    return pl.pallas_call(
        body, name=name, grid=(tp // tm,),
        in_specs=[sp["oa"], sp["ob"], sp["z"], sp["rg"], sp["ga"], sp["gb"], sp["gnorm"], sp["rnorm"],
                  sp["w"], sp["w"], sp["w"], sp["res"]],
        out_specs=[act, act, pl.BlockSpec((tm, 4 * hv), lambda i: (i, 0)), act, act, sp["res"], sp["res"],
                   sp["res"], sp["gnorm"], sp["rnorm"]],
        out_shape=[jax.ShapeDtypeStruct((tp, hv), F32), jax.ShapeDtypeStruct((tp, hv), F32),
                   jax.ShapeDtypeStruct((tp, 4 * hv), BF16), jax.ShapeDtypeStruct((tp, hv), BF16),
                   jax.ShapeDtypeStruct((tp, hv), BF16), jax.ShapeDtypeStruct((tp, d), BF16),
                   jax.ShapeDtypeStruct((tp, d), BF16), jax.ShapeDtypeStruct((tp, d), BF16),
                   jax.ShapeDtypeStruct((1, HEAD_DIM), F32), jax.ShapeDtypeStruct((1, hv), F32)],
        scratch_shapes=[pltpu.VMEM((tm, hv), F32), pltpu.VMEM((tm, hv), F32)],
        compiler_params=_params("arbitrary"))(oa, ob, proj, proj, proj, proj, gnorm, rnorm, wbg, wbr, wo, dh2)


def _final(h3, gain, target, name):
    tp, d = h3.shape
    tm = HEAD_ROWS

    def body(h_ref, g_ref, t_ref, loss_ref, dh_ref, dgain_ref):
        i = pl.program_id(0)

        @pl.when(i == 0)
        def _():
            loss_ref[...] = jnp.zeros_like(loss_ref)
            dgain_ref[...] = jnp.zeros_like(dgain_ref)

        xh, r = _rms_parts(h_ref[...])
        err = jnp.where(i == 0, 0.0, xh * g_ref[...] - t_ref[...])
        dx, dg = _rms_bwd(err * (1.0 / d), xh, r, g_ref[...])
        dh_ref[...] = dx
        dgain_ref[...] += dg
        loss_ref[...] += 0.5 * jnp.sum(jnp.mean(err * err, axis=-1, keepdims=True), axis=0, keepdims=True)

    row = pl.BlockSpec((tm, d), lambda i: (i, 0))
    vec = pl.BlockSpec((1, d), lambda i: (0, 0))
    return pl.pallas_call(
        body, name=name, grid=(tp // tm,),
        in_specs=[row, vec, pl.BlockSpec((tm, d), lambda i: (jnp.maximum(i - 1, 0), 0))],
        out_specs=[pl.BlockSpec((1, LANES), lambda i: (0, 0)), row, vec],
        out_shape=[jax.ShapeDtypeStruct((1, LANES), F32), jax.ShapeDtypeStruct((tp, d), F32),
                   jax.ShapeDtypeStruct((1, d), F32)],
        compiler_params=_params("arbitrary"))(h3, gain, target)


def _peer(k):
    x, y, c = lax.axis_index("x"), lax.axis_index("y"), lax.axis_index("c")
    return (1 - x if k & 4 else x, 1 - y if k & 2 else y, 1 - c if k & 1 else c)


def _my_index():
    return 4 * lax.axis_index("x") + 2 * lax.axis_index("y") + lax.axis_index("c")


def _exchange(bufs, scatter, name):
    n = len(bufs)

    def body(*refs):
        _exchange_copies(refs[:n], refs[n:2 * n], refs[2 * n:], scatter, True, True)

    hbm, out_shape, sems = _exchange_refs(bufs)
    return pl.pallas_call(
        body, name=name, in_specs=hbm, out_specs=hbm, out_shape=out_shape, scratch_shapes=sems,
        compiler_params=pltpu.CompilerParams(has_side_effects=True))(*bufs)


def _gather_via_sibling(bufs, name):
    n = len(bufs)

    def body(*refs):
        _sibling_gather_copies(refs[:n], refs[n:2 * n], refs[2 * n:], True, True)

    hbm, out_shape, sems = _exchange_refs(bufs)
    return pl.pallas_call(
        body, name=name, in_specs=hbm, out_specs=hbm, out_shape=out_shape, scratch_shapes=sems,
        compiler_params=pltpu.CompilerParams(has_side_effects=True))(*bufs)


def _sibling_gather_copies(x_refs, out_refs, sems, start, finish):
    n = len(x_refs)
    send_sems, recv_sems, local_sems = sems
    x, y, c = lax.axis_index("x"), lax.axis_index("y"), lax.axis_index("c")
    me, sibling = (x, y, c), (x, y, 1 - c)
    chips = [(1 - x, y), (x, 1 - y), (1 - x, 1 - y)]
    rows = lambda a, dev: out_refs[a].at[4 * dev[0] + 2 * dev[1] + dev[2]]

    def copy(k, a, block, to, src=None):
        return pltpu.make_async_remote_copy(
            src_ref=rows(a, block) if src is None else src, dst_ref=rows(a, block),
            send_sem=send_sems.at[k * n + a], recv_sem=recv_sems.at[k * n + a],
            device_id=to, device_id_type=pl.DeviceIdType.MESH)

    mine = [pltpu.make_async_copy(x_refs[a], rows(a, me), local_sems.at[a]) for a in range(n)]
    first = [copy(0, a, me, sibling, src=x_refs[a]) for a in range(n)]
    first += [copy(1 + j, a, me, (*chip, c), src=x_refs[a]) for j, chip in enumerate(chips) for a in range(n)]
    if start:
        for cp in mine + first:
            cp.start()
    if finish:
        passed = []
        for j, chip in enumerate(chips):
            for a in range(n):
                copy(1 + j, a, (*chip, c), me).wait_recv()
                passed.append(copy(4 + j, a, (*chip, c), sibling))
                passed[-1].start()
        for a in range(n):
            copy(0, a, sibling, me).wait_recv()
        for j, chip in enumerate(chips):
            for a in range(n):
                copy(4 + j, a, (*chip, 1 - c), me).wait_recv()
        for cp in first + passed:
            cp.wait_send()
        for cp in mine:
            cp.wait()


def _exchange_refs(bufs):
    n = len(bufs)
    return ([pl.BlockSpec(memory_space=pl.ANY)] * n,
            [jax.ShapeDtypeStruct((N_DEV,) + b.shape[-2:], b.dtype) for b in bufs],
            [pltpu.SemaphoreType.DMA(((N_DEV - 1) * n,)), pltpu.SemaphoreType.DMA(((N_DEV - 1) * n,)),
             pltpu.SemaphoreType.DMA((n,))])


def _exchange_copies(x_refs, out_refs, sems, scatter, start, wait):
    n = len(x_refs)
    send_sems, recv_sems, local_sems = sems
    me = _my_index()
    copies = []
    for a in range(n):
        copies.append(pltpu.make_async_copy(x_refs[a].at[me] if scatter else x_refs[a], out_refs[a].at[me],
                                            local_sems.at[a]))
    sends = []
    arrivals = []
    for k in range(1, N_DEV):
        x, y, c = _peer(k)
        peer = 4 * x + 2 * y + c
        for a in range(n):
            sem = (k - 1) * n + a
            sends.append(pltpu.make_async_remote_copy(
                src_ref=x_refs[a].at[peer] if scatter else x_refs[a], dst_ref=out_refs[a].at[me],
                send_sem=send_sems.at[sem], recv_sem=recv_sems.at[sem],
                device_id=(x, y, c), device_id_type=pl.DeviceIdType.MESH))
            if wait:
                landed = out_refs[a].at[peer]
                arrivals.append(pltpu.make_async_remote_copy(
                    src_ref=landed, dst_ref=landed, send_sem=send_sems.at[sem], recv_sem=recv_sems.at[sem],
                    device_id=(x, y, c), device_id_type=pl.DeviceIdType.MESH))
    if start:
        for cp in copies + sends:
            cp.start()
    if wait:
        for cp in arrivals:
            cp.wait_recv()
        for cp in sends:
            cp.wait_send()
        for cp in copies:
            cp.wait()


def _carried_call(body, carry, first, last, *, name, grid, in_specs, out_specs, out_shape, scratch_shapes=()):
    in_specs, out_specs, out_shape = list(in_specs), list(out_specs), list(out_shape)
    semantics = ("arbitrary",) * len(grid)
    if carry is None:
        call = pl.pallas_call(body, name=name, grid=grid, in_specs=in_specs, out_specs=out_specs,
                              out_shape=out_shape, scratch_shapes=list(scratch_shapes),
                              compiler_params=_params(*semantics))
        return lambda *args: (call(*args), [])
    bufs, kind = carry
    n, n_in, n_out, n_scratch = len(bufs), len(in_specs), len(out_specs), len(scratch_shapes)
    hbm, x_shapes, sems = _exchange_refs(bufs)

    def copies(x_refs, xo_refs, x_sems, start, finish):
        if kind == "sibling_gather":
            _sibling_gather_copies(x_refs, xo_refs, x_sems, start, finish)
        else:
            _exchange_copies(x_refs, xo_refs, x_sems, kind == "scatter", start, finish)

    def full_body(*refs):
        ins, x_refs = refs[:n_in], refs[n_in:n_in + n]
        outs, xo_refs = refs[n_in + n:n_in + n + n_out], refs[n_in + n + n_out:n_in + 2 * n + n_out]
        scratch = refs[n_in + 2 * n + n_out:n_in + 2 * n + n_out + n_scratch]
        x_sems = refs[n_in + 2 * n + n_out + n_scratch:]

        @pl.when(first())
        def _():
            copies(x_refs, xo_refs, x_sems, True, False)

        body(*ins, *outs, *scratch)

        @pl.when(last())
        def _():
            copies(x_refs, xo_refs, x_sems, False, True)

    call = pl.pallas_call(full_body, name=name, grid=grid, in_specs=in_specs + hbm, out_specs=out_specs + hbm,
                          out_shape=out_shape + x_shapes, scratch_shapes=list(scratch_shapes) + sems,
                          compiler_params=_params(*semantics))

    def run(*args):
        res = call(*args, *bufs)
        return res[:n_out], res[n_out:]
    return run


def _adamw(w, g, m, v, name):
    r, c = w.shape
    parts = g.ndim == 3
    tr = _tile(r, 256, 16 if parts else 8)
    c1 = 1.0 - ADAM_B1 ** ADAM_STEP
    c2 = 1.0 - ADAM_B2 ** ADAM_STEP

    def body(w_ref, g_ref, m_ref, v_ref, go_ref, d_ref, mo_ref, vo_ref):
        if parts:
            g = g_ref[0].astype(F32)
            for q in range(1, N_DEV):
                g = g + g_ref[q].astype(F32)
        else:
            g = g_ref[...]
        m = ADAM_B1 * m_ref[...] + (1.0 - ADAM_B1) * g
        v = ADAM_B2 * v_ref[...] + (1.0 - ADAM_B2) * (g * g)
        go_ref[...] = g
        d_ref[...] = -ADAM_LR * ((m / c1) / (jnp.sqrt(v / c2) + ADAM_EPS) + ADAM_WD * w_ref[...])
        mo_ref[...] = m
        vo_ref[...] = v

    blk = pl.BlockSpec((tr, c), lambda i: (i, 0))
    g_spec = pl.BlockSpec((N_DEV, tr, c), lambda i: (0, i, 0)) if parts else blk
    return pl.pallas_call(
        body, name=name, grid=(r // tr,), in_specs=[blk, g_spec, blk, blk], out_specs=[blk] * 4,
        out_shape=[jax.ShapeDtypeStruct((r, c), F32)] * 4,
        compiler_params=_params("parallel"))(w, g, m, v)


def _win_segments(hv, nh):
    o_z, o_b = 3 * hv, 4 * hv
    o_r = o_b + 2 * nh
    return [(0, 0, 3 * hv), (3 * hv, o_r, 3 * hv), (6 * hv, o_z, hv), (7 * hv, o_r + 3 * hv, 3 * hv),
            (10 * hv, o_b, 2 * nh)]


def _win_from_shards(shards, hv, nh):
    _, d, cs = shards.shape
    pieces = []
    for _, src, width in _win_segments(hv, nh):
        lo = src
        while lo < src + width:
            p = lo // cs
            hi = min(src + width, (p + 1) * cs)
            pieces.append(shards[p][:, lo - p * cs:hi - p * cs])
            lo = hi
    pieces.append(jnp.zeros((d, LANES - 2 * nh), shards.dtype))
    return jnp.concatenate(pieces, axis=1)


def _win_grad_to_shards(parts, hv, nh, cs):
    segments = _win_segments(hv, nh)
    starts = [sum(p.shape[1] for p in parts[:i]) for i in range(len(parts))]

    def columns(a, b):
        out = []
        for part, start in zip(parts, starts):
            lo, hi = max(a, start), min(b, start + part.shape[1])
            if lo < hi:
                out.append(part[:, lo - start:hi - start])
        return out

    shards = []
    for p in range(N_DEV):
        pieces = []
        lo = p * cs
        while lo < (p + 1) * cs:
            here, src, width = next(s for s in segments if s[1] <= lo < s[1] + s[2])
            hi = min((p + 1) * cs, src + width)
            pieces += columns(here + lo - src, here + hi - src)
            lo = hi
        shards.append(jnp.concatenate(pieces, axis=1))
    return jnp.stack(shards)


def _rope_tables(tp):
    pos = jnp.arange(tp, dtype=F32) - float(PAD_FRONT)
    inv = 1.0 / (ROPE_BASE ** jnp.linspace(0.0, 1.0, HEAD_DIM // 2, dtype=F32))
    ang = pos[:, None] * inv[None, :]
    cos = jnp.repeat(jnp.cos(ang), 2, axis=1)
    sin = jnp.repeat(jnp.sin(ang), 2, axis=1) * jnp.tile(jnp.array([-1.0, 1.0], F32), HEAD_DIM // 2)[None, :]
    return cos, sin


def _retention_tables(nh):
    log_gamma = jnp.log1p(-jnp.exp2(-5.0 - jnp.arange(nh, dtype=F32)))
    pos = jnp.arange(CHUNK, dtype=F32)
    causal = pos[:, None] >= pos[None, :]
    diff = pos[:, None] - pos[None, :]
    dec = jnp.where(causal, jnp.exp(jnp.where(causal, diff, 0.0) * log_gamma[:, None, None]), 0.0)
    ones = jnp.ones((1, 1, HEAD_DIM), F32)
    xi = jnp.exp((pos + 1.0)[None, :] * log_gamma[:, None])[:, :, None] * ones
    zeta = jnp.exp((CHUNK - 1.0 - pos)[None, :] * log_gamma[:, None])[:, :, None] * ones
    cd = jnp.exp(CHUNK * log_gamma)[:, None, None] * jnp.ones((1, 8, HEAD_DIM), F32)
    return dec, xi, zeta, cd


SHARDED = ("meta_tokens", "ffn1_w_in", "ffn1_w_out", "w_in", "gdn_conv_w", "w_branch_gdn", "w_branch_ret",
           "w_out", "ffn2_w_in", "ffn2_w_out")
EXACT_F32 = ("meta_tokens", "gdn_conv_w")
REPLICATED = ("ffn1_norm", "mix_norm", "gdn_a_log", "gdn_dt_bias", "gdn_out_norm", "ret_out_norm", "ffn2_norm",
              "final_norm")
WEIGHTS = ("meta_tokens", "ffn1_norm", "ffn1_w_in", "ffn1_w_out", "mix_norm", "w_in", "gdn_conv_w", "gdn_a_log",
           "gdn_dt_bias", "gdn_out_norm", "ret_out_norm", "w_branch_gdn", "w_branch_ret", "w_out", "ffn2_norm",
           "ffn2_w_in", "ffn2_w_out", "final_norm")


def _as2d(a):
    if a.ndim == 3:
        return a[0]
    if a.ndim == 1:
        return a[None, :]
    return a


def _rows_of(shards):
    return shards.reshape(-1, shards.shape[2])


def _cols_of(shards):
    return shards.transpose(1, 0, 2).reshape(shards.shape[1], -1)


def _row_shards(a):
    return a.reshape(N_DEV, -1, a.shape[1])


def _col_shards(a):
    return a.reshape(a.shape[0], N_DEV, -1).transpose(1, 0, 2)


GATHER_FIRST = ("meta_tokens", "ffn1_w_in", "ffn1_w_out")
GATHER_BEHIND_FFN1 = ("w_in", "gdn_conv_w")
GATHER_BEHIND_PROJ = ("w_branch_gdn", "w_branch_ret", "w_out", "ffn2_w_in", "ffn2_w_out")
SCATTER_BEHIND_DN2 = ("ffn2_w_in", "ffn2_w_out", "w_branch_gdn", "w_branch_ret", "w_out")
SCATTER_BEHIND_FFN1 = ("w_in", "gdn_conv_w")
SCATTER_BEHIND_DWG = ("meta_tokens", "ffn1_w_out")
SCATTER_LAST = ("ffn1_w_in",)


def _device_step(x, target, send, rep):
    seq, d = x.shape
    tp = HEAD_ROWS + seq
    hv = d
    nh = hv // HEAD_DIM
    assert tp % (SCAN_CHUNKS * CHUNK) == 0 and tp % HEAD_ROWS == 0
    bf16_shards = lambda grads, names: [grads[n].astype(BF16) for n in names]

    pad_lanes = lambda row: jnp.pad(row, ((0, 0), (nh, LANES - 2 * nh)))
    alog = pad_lanes(rep["gdn_a_log"])
    dtb = pad_lanes(rep["gdn_dt_bias"])
    cos, sin = _rope_tables(tp)
    dec, xi, zeta, cd = _retention_tables(nh)

    got = dict(zip(GATHER_FIRST, _gather_via_sibling([send[n] for n in GATHER_FIRST], "gather_ffn1")))
    h0 = jnp.concatenate([jnp.zeros((PAD_FRONT, d), F32), _cols_of(got["meta_tokens"]), x], axis=0)
    f1i, f1o = got["ffn1_w_in"], _rows_of(got["ffn1_w_out"])
    (h1, hid1, dup1, dgate1), moved = _ffn_fwd(h0, rep["ffn1_norm"], f1i, f1o, "ffn1_fwd",
                                               ([send[n] for n in GATHER_BEHIND_FFN1], "sibling_gather"))
    got.update(zip(GATHER_BEHIND_FFN1, moved))
    wp = _win_from_shards(got["w_in"], hv, nh)
    conv_w = _cols_of(got["gdn_conv_w"])
    (proj, n2), moved = _proj_fwd(h1, rep["mix_norm"], wp, "proj_fwd",
                                  ([send[n] for n in GATHER_BEHIND_PROJ], "sibling_gather"))
    got.update(zip(GATHER_BEHIND_PROJ, moved))
    wbg, wbr, wo = _rows_of(got["w_branch_gdn"]), _rows_of(got["w_branch_ret"]), _rows_of(got["w_out"])
    f2i, f2o = got["ffn2_w_in"], _rows_of(got["ffn2_w_out"])
    qkv, conv_out = _conv_fwd(proj, conv_w, hv, "conv_fwd")
    (oa, s_gdn, t_gdn), (ob, s_ret) = _run_together(
        [_gdn_fwd(qkv, proj, alog, dtb, nh), _ret_fwd(proj, cos, sin, dec, xi, zeta, cd, nh)], "scans_fwd")
    h2 = _post_fwd(oa, ob, proj, rep["gdn_out_norm"], rep["ret_out_norm"], wbg, wbr, wo, h1, "post_fwd")
    (h3, hid2, dup2, dgate2), _ = _ffn_fwd(h2, rep["ffn2_norm"], f2i, f2o, "ffn2_fwd")
    loss_row, dh3, d_final = _final(h3, rep["final_norm"], target, "final")

    (dh2, d_f2n, n3, dag2, dau2), _ = _ffn_bwd(h2, dh3, rep["ffn2_norm"], f2i, f2o, dup2, dgate2, "ffn2_bwd")
    grads = {"ffn2_w_in": jnp.concatenate([_matmul_tn_blocks(n3, dag2, "ffn2_dwg"),
                                           _matmul_tn_blocks(n3, dau2, "ffn2_dwu")]),
             "ffn2_w_out": _row_shards(_matmul_tn_blocks(hid2, dh3, "ffn2_dwo", 0.5))}

    doa, dob, dgate, ya, yb, merged, dpa, dpb, d_gn, d_rn = _post_bwd(
        oa, ob, proj, rep["gdn_out_norm"], rep["ret_out_norm"], wbg, wbr, wo, dh2, "post_bwd")
    grads["w_branch_gdn"] = _row_shards(_matmul_tn(ya, dpa, "dw_branch_gdn"))
    grads["w_branch_ret"] = _row_shards(_matmul_tn(yb, dpb, "dw_branch_ret"))
    grads["w_out"] = _row_shards(_matmul_tn(merged, dh2, "dw_out"))

    gdn_grads, d_ret = _run_together(
        [_gdn_bwd(qkv, proj, alog, dtb, s_gdn, t_gdn, doa, nh),
         _ret_bwd(proj, cos, sin, dec, xi, zeta, cd, s_ret, dob, nh)], "scans_bwd")
    dba, d_alog, d_dtb = gdn_grads[3:]
    dpre, g_conv = [], []
    for grp, tag in enumerate("qkv"):
        dx, dw = _conv_bwd(proj, conv_out, conv_w, gdn_grads[grp], grp, hv, "conv_bwd_" + tag)
        dpre.append(dx)
        g_conv.append(dw)
    grads["gdn_conv_w"] = _col_shards(jnp.concatenate(g_conv, axis=1))

    wide = dpre + list(d_ret) + [dgate]
    dn2, moved = _matmul_nt_parts(wide, wp[:, :10 * hv], "dn2_wide",
                                  (bf16_shards(grads, SCATTER_BEHIND_DN2), "scatter"))
    parts = dict(zip(SCATTER_BEHIND_DN2, moved))
    g_wp = [_matmul_tn(n2, dg, "dw_in_%d" % idx) for idx, dg in enumerate(wide + [dba])]
    grads["w_in"] = _win_grad_to_shards(g_wp, hv, nh, send["w_in"].shape[1])
    dh1, d_mixn = _norm_bwd(h1, rep["mix_norm"], dn2, dba, wp[:, 10 * hv:], dh2, "mix_norm_bwd")

    (dh0, d_f1n, n1, dag1, dau1), moved = _ffn_bwd(h0, dh1, rep["ffn1_norm"], f1i, f1o, dup1, dgate1, "ffn1_bwd",
                                                   (bf16_shards(grads, SCATTER_BEHIND_FFN1), "scatter"))
    parts.update(zip(SCATTER_BEHIND_FFN1, moved))
    grads["ffn1_w_out"] = _row_shards(_matmul_tn_blocks(hid1, dh1, "ffn1_dwo", 0.5))
    grads["meta_tokens"] = _col_shards(dh0[PAD_FRONT:HEAD_ROWS])
    g_gate, moved = _matmul_tn_blocks(n1, dag1, "ffn1_dwg", carry=(bf16_shards(grads, SCATTER_BEHIND_DWG), "scatter"))
    parts.update(zip(SCATTER_BEHIND_DWG, moved))
    grads["ffn1_w_in"] = jnp.concatenate([g_gate, _matmul_tn_blocks(n1, dau1, "ffn1_dwu")])
    parts.update(zip(SCATTER_LAST, _exchange(bf16_shards(grads, SCATTER_LAST), True, "scatter_ffn1")))

    small = {"ffn1_norm": d_f1n, "mix_norm": d_mixn, "gdn_a_log": d_alog[:, nh:2 * nh],
             "gdn_dt_bias": d_dtb[:, nh:2 * nh], "gdn_out_norm": d_gn, "ret_out_norm": d_rn, "ffn2_norm": d_f2n,
             "final_norm": d_final}
    return loss_row[0, 0], dh0[HEAD_ROWS:], parts, small


def kernel(x, meta_tokens, ffn1_norm, ffn1_w_in, ffn1_w_out, mix_norm, w_in, gdn_conv_w, gdn_a_log, gdn_dt_bias, gdn_out_norm, ret_out_norm, w_branch_gdn, w_branch_ret, w_out, ffn2_norm, ffn2_w_in, ffn2_w_out, final_norm, loss_target, m_meta_tokens, m_ffn1_norm, m_ffn1_w_in, m_ffn1_w_out, m_mix_norm, m_w_in, m_gdn_conv_w, m_gdn_a_log, m_gdn_dt_bias, m_gdn_out_norm, m_ret_out_norm, m_w_branch_gdn, m_w_branch_ret, m_w_out, m_ffn2_norm, m_ffn2_w_in, m_ffn2_w_out, m_final_norm, v_meta_tokens, v_ffn1_norm, v_ffn1_w_in, v_ffn1_w_out, v_mix_norm, v_w_in, v_gdn_conv_w, v_gdn_a_log, v_gdn_dt_bias, v_gdn_out_norm, v_ret_out_norm, v_w_branch_gdn, v_w_branch_ret, v_w_out, v_ffn2_norm, v_ffn2_w_in, v_ffn2_w_out, v_final_norm):
    given = dict(locals())
    params = {n: _as2d(given[n]) for n in WEIGHTS}
    local = {n: params[n] for n in SHARDED}
    rep = {n: params[n] for n in REPLICATED}

    send = {n: local[n] if n in EXACT_F32 else local[n].astype(BF16) for n in SHARDED}
    loss_sum, grad_x, parts, small = _device_step(x[0], loss_target[0], send, rep)
    parts.update(zip(REPLICATED, _exchange([small[n] for n in REPLICATED], False, "gather_small_grads")))
    loss = lax.psum(loss_sum, ("x", "y", "c"))

    outs = {}
    for n in WEIGHTS:
        res = _adamw(params[n], parts[n], _as2d(given["m_" + n]), _as2d(given["v_" + n]), "adamw_" + n)
        outs[n] = [r.reshape(given[n].shape) for r in res]
    return (loss, grad_x[None], *[outs[n][0] for n in WEIGHTS], *[outs[n][1] for n in WEIGHTS],
            *[outs[n][2] for n in WEIGHTS], *[outs[n][3] for n in WEIGHTS])
```

```python
import functools
import math

import jax
import jax.numpy as jnp
from jax import lax
from jax.experimental import pallas as pl
from jax.experimental.pallas import tpu as pltpu

F32 = jnp.float32
BF16 = jnp.bfloat16

N_DEV = 8
N_META = 16
CHUNK = 64
HEAD_DIM = 128
CONV_K = 4
ROPE_BASE = 10000.0
EPS = 1e-6
PAD_FRONT = 240
HEAD_ROWS = PAD_FRONT + N_META
LANES = 128
VMEM_LIMIT_BYTES = 56 * 1024 * 1024

ADAM_LR = 0.001
ADAM_B1 = 0.9
ADAM_B2 = 0.999
ADAM_EPS = 1e-08
ADAM_WD = 0.01
ADAM_STEP = 10

NN = (((1,), (0,)), ((), ()))
NT = (((1,), (1,)), ((), ()))
TN = (((0,), (0,)), ((), ()))


def _tile(n, target, mult):
    best = 0
    for t in range(mult, min(n, target) + 1, mult):
        if n % t == 0:
            best = t
    return best if best else n


def _params(*semantics):
    return pltpu.CompilerParams(dimension_semantics=semantics, vmem_limit_bytes=VMEM_LIMIT_BYTES)


def _split(a, pieces):
    out = []
    for _ in range(pieces - 1):
        part = a.astype(BF16)
        out.append(part)
        a = a - part.astype(F32)
    return out + [a.astype(BF16)]


def _raw_dot(a, b, dims, hi):
    dot = lambda x, y: lax.dot_general(x, y, dims, preferred_element_type=F32)
    if hi:
        (a_hi, a_lo), (b_hi, b_lo) = _split(a, 2), _split(b, 2)
        return dot(a_hi, b_hi) + (dot(a_hi, b_lo) + dot(a_lo, b_hi))
    return dot(a.astype(BF16), b.astype(BF16))


def _mask_dot(mask, x, dims):
    mask = mask.astype(BF16)
    hi, mid, lo = [lax.dot_general(mask, p, dims, preferred_element_type=F32) for p in _split(x, 3)]
    return hi + (mid + lo)


@jax.custom_vjp
def _cumsum_rows(x):
    c = x.shape[0]
    tril = lax.broadcasted_iota(jnp.int32, (c, c), 0) >= lax.broadcasted_iota(jnp.int32, (c, c), 1)
    return _mask_dot(tril, x, NN)


def _cumsum_rows_bwd(_, g):
    c = g.shape[0]
    tril = lax.broadcasted_iota(jnp.int32, (c, c), 0) >= lax.broadcasted_iota(jnp.int32, (c, c), 1)
    return (_mask_dot(tril, g, TN),)


_cumsum_rows.defvjp(lambda x: (_cumsum_rows(x), None), _cumsum_rows_bwd)


def _unit_lower_inverses(xs):
    c = xs[0].shape[0]
    eye = (lax.broadcasted_iota(jnp.int32, (c, c), 0) == lax.broadcasted_iota(jnp.int32, (c, c), 1)).astype(F32)
    t_inv = [eye + x for x in xs]
    for _ in range(int(math.log2(c)) - 1):
        xs = [_raw_dot(x, x, NN, True) for x in xs]
        t_inv = [t + _raw_dot(t, x, NN, True) for t, x in zip(t_inv, xs)]
    return t_inv


@jax.custom_vjp
def _known_inverse(x_neg, t_inv):
    return t_inv


_known_inverse.defvjp(
    lambda x_neg, t_inv: (t_inv, t_inv),
    lambda t_inv, g: (_raw_dot(_raw_dot(t_inv, g, TN, False), t_inv, NT, False), jnp.zeros_like(t_inv)))


def _make_mm(hi):
    @jax.custom_vjp
    def nn(a, b):
        return _raw_dot(a, b, NN, hi)

    @jax.custom_vjp
    def nt(a, b):
        return _raw_dot(a, b, NT, hi)

    @jax.custom_vjp
    def tn(a, b):
        return _raw_dot(a, b, TN, hi)

    nn.defvjp(lambda a, b: (_raw_dot(a, b, NN, hi), (a, b)),
              lambda r, g: (_raw_dot(g, r[1], NT, False), _raw_dot(r[0], g, TN, False)))
    nt.defvjp(lambda a, b: (_raw_dot(a, b, NT, hi), (a, b)),
              lambda r, g: (_raw_dot(g, r[1], NN, False), _raw_dot(g, r[0], TN, False)))
    tn.defvjp(lambda a, b: (_raw_dot(a, b, TN, hi), (a, b)),
              lambda r, g: (_raw_dot(r[1], g, NT, False), _raw_dot(r[0], g, NN, False)))
    return nn, nt, tn


def _silu(x):
    return x * jax.nn.sigmoid(x)


def _rms_parts(x):
    r = lax.rsqrt(jnp.mean(x * x, axis=-1, keepdims=True) + EPS)
    return x * r, r


def _rms_bwd(dy, xh, r, gain):
    dxh = dy * gain
    dx = r * (dxh - xh * jnp.mean(dxh * xh, axis=-1, keepdims=True))
    return dx, jnp.sum(dy * xh, axis=0, keepdims=True)


def _ffn_specs(tm, d, tf, nj):
    return [pl.BlockSpec((tm, d), lambda i, j: (i, 0)), pl.BlockSpec((1, d), lambda i, j: (0, 0)),
            pl.BlockSpec((1, d, tf), lambda i, j: (j, 0, 0)), pl.BlockSpec((1, d, tf), lambda i, j: (nj + j, 0, 0)),
            pl.BlockSpec((tf, d), lambda i, j: (j, 0))]


def _first_step(ndim):
    return lambda: functools.reduce(lambda a, b: a & b, [pl.program_id(k) == 0 for k in range(ndim)])


def _last_step(grid):
    return lambda: functools.reduce(lambda a, b: a & b, [pl.program_id(k) == g - 1 for k, g in enumerate(grid)])


def _ffn_fwd(h, gain, w_in, wo, name, carry=None, head=None):
    tp, d = h.shape
    tf = w_in.shape[2]
    nj = w_in.shape[0] // 2
    tm = _tile(tp, 768, 8)
    row, vec, wg_spec, wu_spec, wo_spec = _ffn_specs(tm, d, tf, nj)
    n_sub = tm // HEAD_ROWS if head is not None else 0
    assert head is None or (tm % HEAD_ROWS == 0 and carry is None)

    def body(*refs):
        h_ref, g_ref, wg3_ref, wu3_ref, wo_ref = refs[:5]
        extra_in = refs[5:5 + (1 + n_sub if head is not None else 0)]
        o_ref, hid3_ref, dup3_ref, dgate3_ref = refs[5 + len(extra_in):9 + len(extra_in)]
        n_sc, acc_sc = refs[-2:]
        wg_ref, wu_ref = wg3_ref.at[0], wu3_ref.at[0]
        j = pl.program_id(1)

        @pl.when(j == 0)
        def _():
            xh, _ = _rms_parts(h_ref[...])
            n_sc[...] = (xh * g_ref[...]).astype(BF16)
            acc_sc[...] = jnp.zeros_like(acc_sc)

        n = n_sc[...]
        a_g = jnp.dot(n, wg_ref[...], preferred_element_type=F32)
        a_u = jnp.dot(n, wu_ref[...], preferred_element_type=F32)
        sg = jax.nn.sigmoid(a_g)
        s = a_g * sg
        hid = (s * a_u).astype(BF16)
        hid3_ref[0] = hid
        dup3_ref[0] = s.astype(BF16)
        dgate3_ref[0] = (a_u * _dsilu(a_g, sg)).astype(BF16)
        acc_sc[...] += jnp.dot(hid, wo_ref[...], preferred_element_type=F32)

        if head is None:
            @pl.when(j == nj - 1)
            def _():
                o_ref[...] = h_ref[...] + 0.5 * acc_sc[...]
            return

        loss_ref, dfg_ref = refs[9 + len(extra_in):11 + len(extra_in)]
        fg_ref, t_refs = extra_in[0], extra_in[1:]
        i = pl.program_id(0)

        @pl.when((i == 0) & (j == 0))
        def _():
            loss_ref[...] = jnp.zeros_like(loss_ref)
            dfg_ref[...] = jnp.zeros_like(dfg_ref)

        @pl.when(j == nj - 1)
        def _():
            for m in range(n_sub):
                rows = slice(m * HEAD_ROWS, (m + 1) * HEAD_ROWS)
                xh, r = _rms_parts(h_ref[rows, :] + 0.5 * acc_sc[rows, :])
                err = xh * fg_ref[...] - t_refs[m][...]
                if m == 0:
                    err = jnp.where(i == 0, 0.0, err)
                dx, dg = _rms_bwd(err * (1.0 / d), xh, r, fg_ref[...])
                o_ref[rows, :] = dx
                dfg_ref[...] += dg
                loss_ref[...] += 0.5 * jnp.sum(jnp.mean(err * err, axis=-1, keepdims=True), axis=0, keepdims=True)

    grid = (tp // tm, nj)
    act = pl.BlockSpec((1, tm, tf), lambda i, j: (j, i, 0))
    in_specs, args = [row, vec, wg_spec, wu_spec, wo_spec], [h, gain, w_in, w_in, wo]
    out_specs = [row, act, act, act]
    out_shape = [jax.ShapeDtypeStruct((tp, d), F32)] + [jax.ShapeDtypeStruct((nj, tp, tf), BF16)] * 3
    if head is not None:
        last = head[1].shape[0] // HEAD_ROWS - 1
        in_specs += [vec] + [pl.BlockSpec((HEAD_ROWS, d), lambda i, j, m=m: (jnp.clip(i * n_sub - 1 + m, 0, last), 0))
                             for m in range(n_sub)]
        args += [head[0]] + [head[1]] * n_sub
        out_specs += [pl.BlockSpec((1, LANES), lambda i, j: (0, 0)), vec]
        out_shape += [jax.ShapeDtypeStruct((1, LANES), F32), jax.ShapeDtypeStruct((1, d), F32)]
    return _carried_call(
        body, carry, _first_step(2), _last_step(grid), name=name, grid=grid,
        in_specs=in_specs, out_specs=out_specs, out_shape=out_shape,
        scratch_shapes=[pltpu.VMEM((tm, d), BF16), pltpu.VMEM((tm, d), F32)])(*args)


def _ffn_bwd(h, dho, gain, w_in, wo, dup3, dgate3, name, carry=None):
    tp, d = h.shape
    tf = w_in.shape[2]
    nj = w_in.shape[0] // 2
    tm = _tile(tp, 704, 16)
    ni = tp // tm
    row, vec, wg_spec, wu_spec, wo_spec = _ffn_specs(tm, d, tf, nj)

    def body(h_ref, dho_ref, g_ref, wg3_ref, wu3_ref, wo_ref, dup3_ref, dgate3_ref,
             dh_ref, dgain_ref, n_ref, dag3_ref, dau3_ref, dn_sc, dhb_sc):
        wg_ref, wu_ref = wg3_ref.at[0], wu3_ref.at[0]
        dag_ref, dau_ref = dag3_ref.at[0], dau3_ref.at[0]
        i, j = pl.program_id(0), pl.program_id(1)

        @pl.when(j == 0)
        def _():
            xh, _ = _rms_parts(h_ref[...])
            n_ref[...] = (xh * g_ref[...]).astype(BF16)
            dn_sc[...] = jnp.zeros_like(dn_sc)
            dhb_sc[...] = (0.5 * dho_ref[...]).astype(BF16)

        @pl.when((i == 0) & (j == 0))
        def _():
            dgain_ref[...] = jnp.zeros_like(dgain_ref)

        d_hid = lax.dot_general(dhb_sc[...], wo_ref[...], NT, preferred_element_type=F32)
        d_au = (d_hid * dup3_ref[0].astype(F32)).astype(BF16)
        d_ag = (d_hid * dgate3_ref[0].astype(F32)).astype(BF16)
        dau_ref[...] = d_au
        dag_ref[...] = d_ag
        dn_sc[...] += (lax.dot_general(d_ag, wg_ref[...], NT, preferred_element_type=F32)
                       + lax.dot_general(d_au, wu_ref[...], NT, preferred_element_type=F32))

        @pl.when(j == nj - 1)
        def _():
            xh, r = _rms_parts(h_ref[...])
            dx, dg = _rms_bwd(dn_sc[...], xh, r, g_ref[...])
            dh_ref[...] = dho_ref[...] + dx
            dgain_ref[...] += dg

    act = pl.BlockSpec((1, tm, tf), lambda i, j: (j, i, 0))
    return _carried_call(
        body, carry, _first_step(2), _last_step((ni, nj)), name=name, grid=(ni, nj),
        in_specs=[row, row, vec, wg_spec, wu_spec, wo_spec, act, act],
        out_specs=[row, vec, row, act, act],
        out_shape=[jax.ShapeDtypeStruct((tp, d), F32), jax.ShapeDtypeStruct((1, d), F32),
                   jax.ShapeDtypeStruct((tp, d), BF16)] + [jax.ShapeDtypeStruct((nj, tp, tf), BF16)] * 2,
        scratch_shapes=[pltpu.VMEM((tm, d), F32), pltpu.VMEM((tm, d), BF16)])(
            h, dho, gain, w_in, w_in, wo, dup3, dgate3)


def _matmul_tn(a, b, name, scale=1.0):
    t, m = a.shape
    n = b.shape[1]
    bm = _tile(m, 1024, LANES)
    bn = _tile(n, 1536, LANES)
    tk = _tile(t, 2816, 16)
    nk = t // tk

    def body(a_ref, b_ref, o_ref):
        k = pl.program_id(2)

        @pl.when(k == 0)
        def _():
            o_ref[...] = jnp.zeros_like(o_ref)

        o_ref[...] += lax.dot_general(a_ref[...].astype(BF16), b_ref[...].astype(BF16), TN,
                                      preferred_element_type=F32)

        if scale != 1.0:
            @pl.when(k == nk - 1)
            def _():
                o_ref[...] = o_ref[...] * scale

    return pl.pallas_call(
        body, name=name, grid=(m // bm, n // bn, nk),
        in_specs=[pl.BlockSpec((tk, bm), lambda i, j, k: (k, i)), pl.BlockSpec((tk, bn), lambda i, j, k: (k, j))],
        out_specs=pl.BlockSpec((bm, bn), lambda i, j, k: (i, j)),
        out_shape=jax.ShapeDtypeStruct((m, n), F32),
        compiler_params=_params("parallel", "parallel", "arbitrary"))(a, b)


def _matmul_tn_blocks(a, b, name, scale=1.0, carry=None):
    a_blocked = a.ndim == 3
    nb, t = (a.shape[0], a.shape[1]) if a_blocked else (b.shape[0], b.shape[1])
    m, n = a.shape[-1], b.shape[-1]
    tk = _tile(t, 2816, 16)
    nk = t // tk
    if a_blocked:
        bo = _tile(n, 1024, LANES)
        a_spec = pl.BlockSpec((1, tk, m), lambda p, o, k: (p, k, 0))
        b_spec = pl.BlockSpec((tk, bo), lambda p, o, k: (k, o))
        o_spec = pl.BlockSpec((m, bo), lambda p, o, k: (p, o))
        out_shape = jax.ShapeDtypeStruct((nb * m, n), F32)
        grid = (nb, n // bo, nk)
    else:
        bo = _tile(m, 1024, LANES)
        a_spec = pl.BlockSpec((tk, bo), lambda p, o, k: (k, o))
        b_spec = pl.BlockSpec((1, tk, n), lambda p, o, k: (p, k, 0))
        o_spec = pl.BlockSpec((1, bo, n), lambda p, o, k: (p, o, 0))
        out_shape = jax.ShapeDtypeStruct((nb, m, n), F32)
        grid = (nb, m // bo, nk)

    def body(a_ref, b_ref, o_ref):
        k = pl.program_id(2)
        a_blk = a_ref[0] if a_blocked else a_ref[...]
        b_blk = b_ref[...] if a_blocked else b_ref[0]
        part = lax.dot_general(a_blk.astype(BF16), b_blk.astype(BF16), TN, preferred_element_type=F32)
        out = o_ref if a_blocked else o_ref.at[0]

        @pl.when(k == 0)
        def _():
            out[...] = part

        @pl.when(k > 0)
        def _():
            out[...] += part

        if scale != 1.0:
            @pl.when(k == nk - 1)
            def _():
                out[...] = out[...] * scale

    (out,), moved = _carried_call(body, carry, _first_step(3), _last_step(grid), name=name, grid=grid,
                                  in_specs=[a_spec, b_spec], out_specs=[o_spec], out_shape=[out_shape])(a, b)
    return out if carry is None else (out, moved)


def _matmul_nt_parts(parts, w, name, carry=None):
    t = parts[0].shape[0]
    d = w.shape[0]
    widths = [p.shape[1] for p in parts]
    tk = _tile(math.gcd(*widths), 1024, LANES)
    counts = [wd // tk for wd in widths]
    starts = [sum(counts[:g]) for g in range(len(parts))]
    nk = sum(counts)
    tm = _tile(t, 1056, 16)
    n_parts = len(parts)

    def body(*refs):
        a_refs, w_ref, o_ref = refs[:n_parts], refs[n_parts], refs[-1]
        k = pl.program_id(1)

        @pl.when(k == 0)
        def _():
            o_ref[...] = jnp.zeros_like(o_ref)

        for g in range(n_parts):
            @pl.when((k >= starts[g]) & (k < starts[g] + counts[g]))
            def _(g=g):
                o_ref[...] += lax.dot_general(a_refs[g][...].astype(BF16), w_ref[...], NT,
                                              preferred_element_type=F32)

    in_specs = [pl.BlockSpec((tm, tk), lambda i, k, lo=starts[g], nb=counts[g]: (i, jnp.clip(k - lo, 0, nb - 1)))
                for g in range(n_parts)]
    in_specs.append(pl.BlockSpec((d, tk), lambda i, k: (0, k)))
    args = list(parts) + [w]
    grid = (t // tm, nk)
    (out,), moved = _carried_call(
        body, carry, _first_step(2), _last_step(grid), name=name, grid=grid, in_specs=in_specs,
        out_specs=[pl.BlockSpec((tm, d), lambda i, k: (i, 0))],
        out_shape=[jax.ShapeDtypeStruct((t, d), F32)])(*args)
    return out, moved


def _proj_fwd(h, gain, wp, name, carry=None):
    tp, d = h.shape
    npad = wp.shape[1]
    tm = _tile(tp, 768, 8)
    tn = _tile(npad, 3456, LANES)

    def body(h_ref, g_ref, w_ref, o_ref, n_ref):
        @pl.when(pl.program_id(1) == 0)
        def _():
            xh, _ = _rms_parts(h_ref[...])
            n_ref[...] = (xh * g_ref[...]).astype(BF16)

        o_ref[...] = jnp.dot(n_ref[...], w_ref[...], preferred_element_type=F32)

    grid = (tp // tm, npad // tn)
    return _carried_call(
        body, carry, _first_step(2), _last_step(grid), name=name, grid=grid,
        in_specs=[pl.BlockSpec((tm, d), lambda i, j: (i, 0)), pl.BlockSpec((1, d), lambda i, j: (0, 0)),
                  pl.BlockSpec((d, tn), lambda i, j: (0, j))],
        out_specs=[pl.BlockSpec((tm, tn), lambda i, j: (i, j)), pl.BlockSpec((tm, d), lambda i, j: (i, 0))],
        out_shape=[jax.ShapeDtypeStruct((tp, npad), F32), jax.ShapeDtypeStruct((tp, d), BF16)])(h, gain, wp)


def _norm_bwd(h, gain, dn, last, w_last, dres, name):
    tp, d = h.shape
    kl = last.shape[1]
    tm = _tile(tp, 256, 8)

    def body(h_ref, g_ref, dn_ref, last_ref, w_ref, dres_ref, dh_ref, dgain_ref):
        @pl.when(pl.program_id(0) == 0)
        def _():
            dgain_ref[...] = jnp.zeros_like(dgain_ref)

        dn_all = dn_ref[...] + lax.dot_general(last_ref[...].astype(BF16), w_ref[...], NT,
                                               preferred_element_type=F32)
        xh, r = _rms_parts(h_ref[...])
        dx, dg = _rms_bwd(dn_all, xh, r, g_ref[...])
        dh_ref[...] = dres_ref[...] + dx
        dgain_ref[...] += dg

    row = pl.BlockSpec((tm, d), lambda i: (i, 0))
    vec = pl.BlockSpec((1, d), lambda i: (0, 0))
    return pl.pallas_call(
        body, name=name, grid=(tp // tm,),
        in_specs=[row, vec, row, pl.BlockSpec((tm, kl), lambda i: (i, 0)), pl.BlockSpec((d, kl), lambda i: (0, 0)), row],
        out_specs=[row, vec],
        out_shape=[jax.ShapeDtypeStruct((tp, d), F32), jax.ShapeDtypeStruct((1, d), F32)],
        compiler_params=_params("arbitrary"))(h, gain, dn, last, w_last, dres)


def _head_post(a, grp):
    a = _silu(a)
    r = lax.rsqrt(jnp.sum(a * a, axis=-1, keepdims=True) + EPS)
    if isinstance(grp, int):
        return a if grp == 2 else a * r * (HEAD_DIM ** -0.5 if grp == 0 else 1.0)
    scale = jnp.where(grp == 0, HEAD_DIM ** -0.5, 1.0).astype(F32)
    return jnp.where(grp == 2, a, a * r * scale)


def _head_post_bwd(c, dy, grp):
    sg = jax.nn.sigmoid(c)
    a = c * sg
    dsilu = sg * (1.0 + c * (1.0 - sg))
    if grp == 2:
        return dy * dsilu
    r = lax.rsqrt(jnp.sum(a * a, axis=-1, keepdims=True) + EPS)
    scale = HEAD_DIM ** -0.5 if grp == 0 else 1.0
    da = (scale * r) * (dy - a * (r * r * jnp.sum(dy * a, axis=-1, keepdims=True)))
    return da * dsilu


def _conv_taps(ext_sc, w_ref, tm):
    ext = ext_sc[...]
    c = w_ref[CONV_K - 1:CONV_K, :] * ext[8:, :]
    for i in range(CONV_K - 1):
        s = CONV_K - 1 - i
        c = c + w_ref[i:i + 1, :] * pltpu.roll(ext, s, 0)[8:, :]
    return c


def _conv_fwd(proj, conv_w, hv, name):
    tp = proj.shape[0]
    tm = _tile(tp, 768, 8)
    nh = hv // HEAD_DIM

    def body(x_ref, halo_ref, w_ref, o_ref, c_ref, ext_sc):
        i, grp = pl.program_id(0), pl.program_id(1)
        ext_sc[0:8, :] = jnp.where(i == 0, 0.0, halo_ref[...])
        ext_sc[8:, :] = x_ref[...]
        c_ref[...] = _conv_taps(ext_sc, w_ref, tm)
        for h in range(nh):
            sl = slice(h * HEAD_DIM, (h + 1) * HEAD_DIM)
            o_ref[:, sl] = _head_post(c_ref[:, sl], grp)

    blk = pl.BlockSpec((tm, hv), lambda i, g: (i, g))
    return pl.pallas_call(
        body, name=name, grid=(tp // tm, 3),
        in_specs=[blk, pl.BlockSpec((8, hv), lambda i, g: (jnp.maximum(i * (tm // 8) - 1, 0), g)),
                  pl.BlockSpec((CONV_K, hv), lambda i, g: (0, g))],
        out_specs=[blk, blk],
        out_shape=[jax.ShapeDtypeStruct((tp, 3 * hv), F32)] * 2,
        scratch_shapes=[pltpu.VMEM((tm + 8, hv), F32)],
        compiler_params=_params("parallel", "arbitrary"))(proj, proj, conv_w)


def _conv_bwd(proj, conv_out, conv_w, dy, grp, hv, name):
    tp = proj.shape[0]
    tm = _tile(tp, 768, 8)
    ni = tp // tm
    nh = hv // HEAD_DIM

    def body(x_ref, c_ref, w_ref, dy_ref, dx_ref, dw_ref, dc_sc):
        step = pl.program_id(0)

        @pl.when(step == 0)
        def _():
            dc_sc[tm:, :] = jnp.zeros((8, hv), F32)
            dw_ref[...] = jnp.zeros_like(dw_ref)

        @pl.when(step > 0)
        def _():
            dc_sc[tm:, :] = dc_sc[0:8, :]

        for h in range(nh):
            sl = slice(h * HEAD_DIM, (h + 1) * HEAD_DIM)
            dc_sc[0:tm, sl] = _head_post_bwd(c_ref[:, sl], dy_ref[:, sl], grp)

        x = x_ref[...]
        dc_ext = dc_sc[...]
        dx = None
        for k in range(CONV_K):
            s = CONV_K - 1 - k
            shifted = dc_ext[0:tm, :] if s == 0 else pltpu.roll(dc_ext, tm + 8 - s, 0)[0:tm, :]
            dw_ref[k:k + 1, :] += jnp.sum(shifted * x, axis=0, keepdims=True)
            term = w_ref[k:k + 1, :] * shifted
            dx = term if dx is None else dx + term
        dx_ref[...] = dx.astype(BF16)

    tile = lambda step: ni - 1 - step
    grp_blk = pl.BlockSpec((tm, hv), lambda s: (tile(s), grp))
    own_blk = pl.BlockSpec((tm, hv), lambda s: (tile(s), 0))
    return pl.pallas_call(
        body, name=name, grid=(ni,),
        in_specs=[grp_blk, grp_blk, pl.BlockSpec((CONV_K, hv), lambda s: (0, grp)), own_blk],
        out_specs=[own_blk, pl.BlockSpec((CONV_K, hv), lambda s: (0, 0))],
        out_shape=[jax.ShapeDtypeStruct((tp, hv), BF16), jax.ShapeDtypeStruct((CONV_K, hv), F32)],
        scratch_shapes=[pltpu.VMEM((tm + 8, hv), F32)],
        compiler_params=_params("arbitrary"))(proj, conv_out, conv_w, dy)


def _gdn_gates(ba, alog, dtb):
    x = ba + dtb
    softplus = jnp.maximum(x, 0.0) + jnp.log1p(jnp.exp(-jnp.abs(x)))
    return _cumsum_rows(-jnp.exp(alog) * softplus), jax.nn.sigmoid(ba)


def _gdn_chunks(states, qs, ks, vs, gates, known_inverses=None):
    mm_nn, mm_nt, mm_tn = _make_mm(False)
    hi_nn, _, _ = _make_mm(True)
    nh = len(states)
    items = range(len(qs))
    head = [i % nh for i in items]
    c = qs[0].shape[0]
    lane = lax.broadcasted_iota(jnp.int32, (c, LANES), 1)
    last_row = (lax.broadcasted_iota(jnp.int32, (c, 1), 0) == c - 1).astype(F32)
    ri = lax.broadcasted_iota(jnp.int32, (c, c), 0)
    ci = lax.broadcasted_iota(jnp.int32, (c, c), 1)
    causal = ri >= ci
    strict = ri > ci
    eye = (ri == ci).astype(F32)
    sel_a = [(lane == nh + h).astype(F32) for h in range(nh)]
    sel_b = [(lane == h).astype(F32) for h in range(nh)]

    gcol = [jnp.sum(gates[i // nh][0] * sel_a[head[i]], axis=1, keepdims=True) for i in items]
    grow = [jnp.sum(eye * gcol[i], axis=0, keepdims=True) for i in items]
    beta = [jnp.sum(gates[i // nh][1] * sel_b[head[i]], axis=1, keepdims=True) for i in items]
    decay = [jnp.where(causal, jnp.exp(jnp.where(causal, gcol[i] - grow[i], 0.0)), 0.0) for i in items]
    kb = [ks[i] * beta[i] for i in items]
    kk = [mm_nt(kb[i], ks[i]) for i in items]
    qk = [mm_nt(qs[i], ks[i]) for i in items]
    x_neg = [-jnp.where(strict, kk[i] * decay[i], 0.0) for i in items]
    if known_inverses is None:
        t_inv = _unit_lower_inverses(x_neg)
    else:
        t_inv = [_known_inverse(x_neg[i], known_inverses[i]) for i in items]
    eg = [jnp.exp(gcol[i]) for i in items]
    u = [hi_nn(t_inv[i], vs[i] * beta[i]) for i in items]
    w = [hi_nn(t_inv[i], kb[i] * eg[i]) for i in items]
    qk = [qk[i] * decay[i] for i in items]
    glast = [jnp.sum(gcol[i] * last_row, axis=0, keepdims=True) for i in items]
    q_dec = [qs[i] * eg[i] for i in items]
    k_dec = [ks[i] * jnp.exp(glast[i] - gcol[i]) for i in items]
    s_dec = [jnp.exp(glast[i]) for i in items]

    outs = []
    for first in range(0, len(qs), nh):
        chunk = range(first, first + nh)
        ws = [mm_nn(w[i], states[i - first]) for i in chunk]
        from_state = [mm_nn(q_dec[i], states[i - first]) for i in chunk]
        v_new = [u[i] - ws[i - first] for i in chunk]
        intra = [mm_nn(qk[i], v_new[i - first]) for i in chunk]
        kv = [mm_tn(k_dec[i], v_new[i - first]) for i in chunk]
        outs += [from_state[i - first] + intra[i - first] for i in chunk]
        states = [states[i - first] * s_dec[i] + kv[i - first] for i in chunk]
    return outs, states, t_inv


SCAN_CHUNKS = 4


def _scan_specs(nh, steps, rev, first_col):
    sidx = (lambda s: steps - 1 - s) if rev else (lambda s: s)
    hv = nh * HEAD_DIM
    rows = SCAN_CHUNKS * CHUNK
    cols = [pl.BlockSpec((rows, hv), lambda s, g=g: (sidx(s), first_col + g)) for g in range(3)]
    st = pl.BlockSpec((1, nh, HEAD_DIM, HEAD_DIM), lambda s: (sidx(s), 0, 0, 0))
    act = pl.BlockSpec((rows, hv), lambda s: (sidx(s), 0))
    return cols, st, act


def _chunk_heads(ref, nh):
    return [ref[j * CHUNK:(j + 1) * CHUNK, h * HEAD_DIM:(h + 1) * HEAD_DIM] for j in range(SCAN_CHUNKS)
            for h in range(nh)]


def _store_chunk_heads(ref, values, nh, dtype=None):
    for i, val in enumerate(values):
        j, h = divmod(i, nh)
        ref[j * CHUNK:(j + 1) * CHUNK, h * HEAD_DIM:(h + 1) * HEAD_DIM] = val if dtype is None else val.astype(dtype)


def _gdn_fwd(qkv, proj, alog, dtb, nh):
    tp = qkv.shape[0]
    steps = tp // (SCAN_CHUNKS * CHUNK)
    rows = SCAN_CHUNKS * CHUNK

    def body(q_ref, k_ref, v_ref, ba_ref, al_ref, dt_ref, o_ref, st_ref, inv_ref, s_sc):
        @pl.when(pl.program_id(0) == 0)
        def _():
            s_sc[...] = jnp.zeros_like(s_sc)

        gates = [_gdn_gates(ba_ref[j * CHUNK:(j + 1) * CHUNK, :], al_ref[...], dt_ref[...])
                 for j in range(SCAN_CHUNKS)]
        states = [s_sc[h] for h in range(nh)]
        for h in range(nh):
            st_ref[0, h] = states[h]
        outs, new_states, t_inv = _gdn_chunks(states, _chunk_heads(q_ref, nh), _chunk_heads(k_ref, nh),
                                              _chunk_heads(v_ref, nh), gates)
        _store_chunk_heads(o_ref, outs, nh)
        for h in range(nh):
            s_sc[h] = new_states[h]
        for i, t in enumerate(t_inv):
            inv_ref[0, i] = t

    cols, st, act = _scan_specs(nh, steps, False, 0)
    ba = pl.BlockSpec((rows, LANES), lambda s: (s, 10 * nh * HEAD_DIM // LANES))
    vec = pl.BlockSpec((1, LANES), lambda s: (0, 0))
    inv = pl.BlockSpec((1, SCAN_CHUNKS * nh, CHUNK, CHUNK), lambda s: (s, 0, 0, 0))
    return dict(
        body=body, grid=(steps,), in_specs=cols + [ba, vec, vec], out_specs=[act, st, inv],
        out_shape=[jax.ShapeDtypeStruct((tp, nh * HEAD_DIM), F32),
                   jax.ShapeDtypeStruct((steps, nh, HEAD_DIM, HEAD_DIM), F32),
                   jax.ShapeDtypeStruct((steps, SCAN_CHUNKS * nh, CHUNK, CHUNK), F32)],
        scratch_shapes=[pltpu.VMEM((nh, HEAD_DIM, HEAD_DIM), F32)], args=(qkv, qkv, qkv, proj, alog, dtb))


def _run_together(parts, name):
    grid = parts[0]["grid"]
    assert all(p["grid"] == grid for p in parts)
    counts = [(len(p["in_specs"]), len(p["out_specs"]), len(p["scratch_shapes"])) for p in parts]
    n_in, n_out = sum(c[0] for c in counts), sum(c[1] for c in counts)

    def body(*refs):
        ins, outs, scratch = refs[:n_in], refs[n_in:n_in + n_out], refs[n_in + n_out:]
        at = [0, 0, 0]
        for p, (ci, co, cs) in zip(parts, counts):
            p["body"](*ins[at[0]:at[0] + ci], *outs[at[1]:at[1] + co], *scratch[at[2]:at[2] + cs])
            at = [at[0] + ci, at[1] + co, at[2] + cs]

    flat = lambda key: [x for p in parts for x in p[key]]
    res = pl.pallas_call(
        body, name=name, grid=grid, in_specs=flat("in_specs"), out_specs=flat("out_specs"),
        out_shape=flat("out_shape"), scratch_shapes=flat("scratch_shapes"),
        compiler_params=_params(*(("arbitrary",) * len(grid))))(*flat("args"))
    out, at = [], 0
    for _, co, _ in counts:
        out.append(res[at:at + co])
        at += co
    return out


def _gdn_bwd(qkv, proj, alog, dtb, states, inverses, do, nh):
    tp = qkv.shape[0]
    steps = tp // (SCAN_CHUNKS * CHUNK)
    rows = SCAN_CHUNKS * CHUNK

    def body(q_ref, k_ref, v_ref, ba_ref, al_ref, dt_ref, st_ref, inv_ref, do_ref,
             dq_ref, dk_ref, dv_ref, dba_ref, dal_ref, ddt_ref, ds_sc):
        @pl.when(pl.program_id(0) == 0)
        def _():
            ds_sc[...] = jnp.zeros_like(ds_sc)
            dal_ref[...] = jnp.zeros_like(dal_ref)
            ddt_ref[...] = jnp.zeros_like(ddt_ref)

        gates, gates_vjps = [], []
        for j in range(SCAN_CHUNKS):
            g, g_vjp = jax.vjp(_gdn_gates, ba_ref[j * CHUNK:(j + 1) * CHUNK, :], al_ref[...], dt_ref[...])
            gates.append(g)
            gates_vjps.append(g_vjp)
        known = [inv_ref[0, i] for i in range(SCAN_CHUNKS * nh)]
        fn = lambda s, q, k, v, g: _gdn_chunks(s, q, k, v, g, known)[:2]
        _, vjp = jax.vjp(fn, [st_ref[0, h] for h in range(nh)], _chunk_heads(q_ref, nh), _chunk_heads(k_ref, nh),
                         _chunk_heads(v_ref, nh), gates)
        ds, dq, dk, dv, dgates = vjp((_chunk_heads(do_ref, nh), [ds_sc[h] for h in range(nh)]))
        for h in range(nh):
            ds_sc[h] = ds[h]
        _store_chunk_heads(dq_ref, dq, nh)
        _store_chunk_heads(dk_ref, dk, nh)
        _store_chunk_heads(dv_ref, dv, nh)
        for j in range(SCAN_CHUNKS):
            dba, dal, ddt = gates_vjps[j](dgates[j])
            dba_ref[j * CHUNK:(j + 1) * CHUNK, :] = dba
            dal_ref[...] += dal
            ddt_ref[...] += ddt

    cols, st, act = _scan_specs(nh, steps, True, 0)
    ba = pl.BlockSpec((rows, LANES), lambda s: (steps - 1 - s, 10 * nh * HEAD_DIM // LANES))
    vec = pl.BlockSpec((1, LANES), lambda s: (0, 0))
    inv = pl.BlockSpec((1, SCAN_CHUNKS * nh, CHUNK, CHUNK), lambda s: (steps - 1 - s, 0, 0, 0))
    return dict(
        body=body, grid=(steps,), in_specs=cols + [ba, vec, vec, st, inv, act],
        out_specs=[act, act, act, pl.BlockSpec((rows, LANES), lambda s: (steps - 1 - s, 0)), vec, vec],
        out_shape=[jax.ShapeDtypeStruct((tp, nh * HEAD_DIM), F32)] * 3
                  + [jax.ShapeDtypeStruct((tp, LANES), F32), jax.ShapeDtypeStruct((1, LANES), F32),
                     jax.ShapeDtypeStruct((1, LANES), F32)],
        scratch_shapes=[pltpu.VMEM((nh, HEAD_DIM, HEAD_DIM), F32)],
        args=(qkv, qkv, qkv, proj, alog, dtb, states, inverses, do))


def _swap_pairs(t):
    lane = lax.broadcasted_iota(jnp.int32, t.shape, 1)
    n = t.shape[1]
    return jnp.where(lane % 2 == 0, pltpu.roll(t, n - 1, 1), pltpu.roll(t, 1, 1))


def _rot(t, cos, sin_signed):
    return t * cos + _swap_pairs(t) * sin_signed


def _rot_t(dt, cos, sin_signed):
    return dt * cos + _swap_pairs(dt * sin_signed)


def _ret_chunks(states, qs, ks, vs, dec, xi, zeta, cd):
    mm_nn, mm_nt, mm_tn = _make_mm(False)
    nh = len(states)
    items = range(len(qs))
    scores = [mm_nt(qs[i], ks[i]) for i in items]
    kv = [mm_tn(ks[i] * zeta[i % nh], vs[i]) for i in items]
    intra = [mm_nn(scores[i] * dec[i % nh], vs[i]) for i in items]
    q_dec = [qs[i] * xi[i % nh] for i in items]
    outs = []
    for first in range(0, len(qs), nh):
        outs += [intra[first + h] + mm_nn(q_dec[first + h], states[h]) for h in range(nh)]
        states = [states[h] * cd[h] + kv[first + h] for h in range(nh)]
    return outs, states


def _ret_table_specs(nh, steps, rev):
    sidx = (lambda s: steps - 1 - s) if rev else (lambda s: s)
    rope = pl.BlockSpec((SCAN_CHUNKS * CHUNK, HEAD_DIM), lambda s: (sidx(s), 0))
    dec = pl.BlockSpec((nh, CHUNK, CHUNK), lambda s: (0, 0, 0))
    tab = pl.BlockSpec((nh, CHUNK, HEAD_DIM), lambda s: (0, 0, 0))
    cd = pl.BlockSpec((nh, 8, HEAD_DIM), lambda s: (0, 0, 0))
    return [rope, rope, dec, tab, tab, cd]


def _rotated(ref, cos_ref, sin_ref, nh, scale=1.0):
    out = []
    for j in range(SCAN_CHUNKS):
        rows = slice(j * CHUNK, (j + 1) * CHUNK)
        cos_t, sin_t = cos_ref[rows, :], sin_ref[rows, :]
        for h in range(nh):
            t = _rot(ref[rows, h * HEAD_DIM:(h + 1) * HEAD_DIM], cos_t, sin_t)
            out.append(t if scale == 1.0 else t * scale)
    return out


def _ret_fwd(proj, cos, sin, dec, xi, zeta, cd, nh):
    tp = proj.shape[0]
    steps = tp // (SCAN_CHUNKS * CHUNK)
    kscale = HEAD_DIM ** -0.5

    def body(q_ref, k_ref, v_ref, cos_ref, sin_ref, dec_ref, xi_ref, zeta_ref, cd_ref, o_ref, st_ref, s_sc):
        @pl.when(pl.program_id(0) == 0)
        def _():
            s_sc[...] = jnp.zeros_like(s_sc)

        heads = range(nh)
        states = [s_sc[h] for h in heads]
        for h in heads:
            st_ref[0, h] = states[h]
        outs, new_states = _ret_chunks(
            states, _rotated(q_ref, cos_ref, sin_ref, nh), _rotated(k_ref, cos_ref, sin_ref, nh, kscale),
            _chunk_heads(v_ref, nh), [dec_ref[h] for h in heads], [xi_ref[h] for h in heads],
            [zeta_ref[h] for h in heads], [cd_ref[h][0:1, :] for h in heads])
        _store_chunk_heads(o_ref, outs, nh)
        for h in heads:
            s_sc[h] = new_states[h]

    cols, st, act = _scan_specs(nh, steps, False, 3)
    return dict(
        body=body, grid=(steps,), in_specs=cols + _ret_table_specs(nh, steps, False), out_specs=[act, st],
        out_shape=[jax.ShapeDtypeStruct((tp, nh * HEAD_DIM), F32),
                   jax.ShapeDtypeStruct((steps, nh, HEAD_DIM, HEAD_DIM), F32)],
        scratch_shapes=[pltpu.VMEM((nh, HEAD_DIM, HEAD_DIM), F32)],
        args=(proj, proj, proj, cos, sin, dec, xi, zeta, cd))


def _ret_bwd(proj, cos, sin, dec, xi, zeta, cd, states, do, nh):
    tp = proj.shape[0]
    steps = tp // (SCAN_CHUNKS * CHUNK)
    kscale = HEAD_DIM ** -0.5

    def body(q_ref, k_ref, v_ref, cos_ref, sin_ref, dec_ref, xi_ref, zeta_ref, cd_ref, st_ref, do_ref,
             dq_ref, dk_ref, dv_ref, ds_sc):
        @pl.when(pl.program_id(0) == 0)
        def _():
            ds_sc[...] = jnp.zeros_like(ds_sc)

        heads = range(nh)
        fn = functools.partial(_ret_chunks, dec=[dec_ref[h] for h in heads], xi=[xi_ref[h] for h in heads],
                               zeta=[zeta_ref[h] for h in heads], cd=[cd_ref[h][0:1, :] for h in heads])
        _, vjp = jax.vjp(fn, [st_ref[0, h] for h in heads], _rotated(q_ref, cos_ref, sin_ref, nh),
                         _rotated(k_ref, cos_ref, sin_ref, nh, kscale), _chunk_heads(v_ref, nh))
        ds, dq, dk, dv = vjp((_chunk_heads(do_ref, nh), [ds_sc[h] for h in heads]))
        for h in heads:
            ds_sc[h] = ds[h]
        for i in range(SCAN_CHUNKS * nh):
            rows = slice((i // nh) * CHUNK, (i // nh + 1) * CHUNK)
            cos_t, sin_t = cos_ref[rows, :], sin_ref[rows, :]
            dq[i] = _rot_t(dq[i], cos_t, sin_t)
            dk[i] = _rot_t(dk[i] * kscale, cos_t, sin_t)
        _store_chunk_heads(dq_ref, dq, nh, BF16)
        _store_chunk_heads(dk_ref, dk, nh, BF16)
        _store_chunk_heads(dv_ref, dv, nh, BF16)

    cols, st, act = _scan_specs(nh, steps, True, 3)
    return dict(
        body=body, grid=(steps,), in_specs=cols + _ret_table_specs(nh, steps, True) + [st, act],
        out_specs=[act, act, act],
        out_shape=[jax.ShapeDtypeStruct((tp, nh * HEAD_DIM), BF16)] * 3,
        scratch_shapes=[pltpu.VMEM((nh, HEAD_DIM, HEAD_DIM), F32)],
        args=(proj, proj, proj, cos, sin, dec, xi, zeta, cd, states, do))


def _gdn_out(o, z, gnorm):
    return o * lax.rsqrt(jnp.mean(o * o, axis=-1, keepdims=True) + EPS) * gnorm * _silu(z)


def _ret_out(o, rg, rnorm):
    mu = jnp.mean(o, axis=-1, keepdims=True)
    var = jnp.mean(jnp.square(o - mu), axis=-1, keepdims=True)
    return _silu(rg) * ((o - mu) * lax.rsqrt(var + EPS) * rnorm)


def _dsilu(x, sg):
    return sg * (1.0 + x * (1.0 - sg))


def _gdn_out_bwd(o, z, gnorm, dy):
    r = lax.rsqrt(jnp.mean(o * o, axis=-1, keepdims=True) + EPS)
    xh = o * r
    sg = jax.nn.sigmoid(z)
    sz = z * sg
    t = dy * (gnorm * sz)
    do = r * (t - xh * jnp.mean(t * xh, axis=-1, keepdims=True))
    e = dy * xh
    return do, e * (gnorm * _dsilu(z, sg)), jnp.sum(e * sz, axis=0, keepdims=True)


def _ret_out_bwd(o, rg, rnorm, dy):
    oc = o - jnp.mean(o, axis=-1, keepdims=True)
    rs = lax.rsqrt(jnp.mean(oc * oc, axis=-1, keepdims=True) + EPS)
    xh = oc * rs
    sg = jax.nn.sigmoid(rg)
    srg = rg * sg
    t = dy * (rnorm * srg)
    do = rs * (t - jnp.mean(t, axis=-1, keepdims=True) - xh * jnp.mean(t * xh, axis=-1, keepdims=True))
    e = dy * xh
    return do, e * (rnorm * _dsilu(rg, sg)), jnp.sum(e * srg, axis=0, keepdims=True)


def _post_specs(tm, hv, d):
    row = lambda col: pl.BlockSpec((tm, hv), lambda i: (i, col))
    return dict(
        oa=row(0), ob=row(0), z=row(6), rg=row(7), ga=row(8), gb=row(9),
        gnorm=pl.BlockSpec((1, HEAD_DIM), lambda i: (0, 0)), rnorm=pl.BlockSpec((1, hv), lambda i: (0, 0)),
        w=pl.BlockSpec((hv, d), lambda i: (0, 0)), res=pl.BlockSpec((tm, d), lambda i: (i, 0)))


def _post_fwd(oa, ob, proj, gnorm, rnorm, wbg, wbr, wo, h1, name):
    tp, d = h1.shape
    hv = oa.shape[1]
    nh = hv // HEAD_DIM
    tm = _tile(tp, 256, 8)

    def body(oa_ref, ob_ref, z_ref, rg_ref, ga_ref, gb_ref, gn_ref, rn_ref, wbg_ref, wbr_ref, wo_ref, h_ref,
             o_ref, ya_sc, yb_sc):
        for h in range(nh):
            sl = slice(h * HEAD_DIM, (h + 1) * HEAD_DIM)
            ya_sc[:, sl] = _gdn_out(oa_ref[:, sl], z_ref[:, sl], gn_ref[...]).astype(BF16)
            yb_sc[:, sl] = _ret_out(ob_ref[:, sl], rg_ref[:, sl], rn_ref[:, sl]).astype(BF16)
        pa = jnp.dot(ya_sc[...], wbg_ref[...], preferred_element_type=F32)
        pb = jnp.dot(yb_sc[...], wbr_ref[...], preferred_element_type=F32)
        merged = jax.nn.sigmoid(ga_ref[...]) * pa + jax.nn.sigmoid(gb_ref[...]) * pb
        o_ref[...] = h_ref[...] + jnp.dot(merged.astype(BF16), wo_ref[...], preferred_element_type=F32)

    sp = _post_specs(tm, hv, d)
    return pl.pallas_call(
        body, name=name, grid=(tp // tm,),
        in_specs=[sp["oa"], sp["ob"], sp["z"], sp["rg"], sp["ga"], sp["gb"], sp["gnorm"], sp["rnorm"],
                  sp["w"], sp["w"], sp["w"], sp["res"]],
        out_specs=sp["res"], out_shape=jax.ShapeDtypeStruct((tp, d), F32),
        scratch_shapes=[pltpu.VMEM((tm, hv), BF16), pltpu.VMEM((tm, hv), BF16)],
        compiler_params=_params("parallel"))(oa, ob, proj, proj, proj, proj, gnorm, rnorm, wbg, wbr, wo, h1)


def _post_bwd(oa, ob, proj, gnorm, rnorm, wbg, wbr, wo, dh2, name):
    tp, d = dh2.shape
    hv = oa.shape[1]
    nh = hv // HEAD_DIM
    tm = _tile(tp, 256, 8)

    def body(oa_ref, ob_ref, z_ref, rg_ref, ga_ref, gb_ref, gn_ref, rn_ref, wbg_ref, wbr_ref, wo_ref, dh_ref,
             doa_ref, dob_ref, dg_ref, ya_ref, yb_ref, mg_ref, dpa_ref, dpb_ref, dgn_ref, drn_ref,
             dya_sc, dyb_sc):
        @pl.when(pl.program_id(0) == 0)
        def _():
            dgn_ref[...] = jnp.zeros_like(dgn_ref)
            drn_ref[...] = jnp.zeros_like(drn_ref)

        for h in range(nh):
            sl = slice(h * HEAD_DIM, (h + 1) * HEAD_DIM)
            ya_ref[:, sl] = _gdn_out(oa_ref[:, sl], z_ref[:, sl], gn_ref[...]).astype(BF16)
            yb_ref[:, sl] = _ret_out(ob_ref[:, sl], rg_ref[:, sl], rn_ref[:, sl]).astype(BF16)
        pa = jnp.dot(ya_ref[...], wbg_ref[...], preferred_element_type=F32)
        pb = jnp.dot(yb_ref[...], wbr_ref[...], preferred_element_type=F32)
        sa = jax.nn.sigmoid(ga_ref[...])
        sb = jax.nn.sigmoid(gb_ref[...])
        mg_ref[...] = (sa * pa + sb * pb).astype(BF16)
        dm = lax.dot_general(dh_ref[...].astype(BF16), wo_ref[...], NT, preferred_element_type=F32)
        dpa = (dm * sa).astype(BF16)
        dpb = (dm * sb).astype(BF16)
        dpa_ref[...] = dpa
        dpb_ref[...] = dpb
        dg_ref[:, 2 * hv:3 * hv] = (dm * pa * sa * (1.0 - sa)).astype(BF16)
        dg_ref[:, 3 * hv:4 * hv] = (dm * pb * sb * (1.0 - sb)).astype(BF16)
        dya_sc[...] = lax.dot_general(dpa, wbg_ref[...], NT, preferred_element_type=F32)
        dyb_sc[...] = lax.dot_general(dpb, wbr_ref[...], NT, preferred_element_type=F32)
        for h in range(nh):
            sl = slice(h * HEAD_DIM, (h + 1) * HEAD_DIM)
            doa, dz, dgn = _gdn_out_bwd(oa_ref[:, sl], z_ref[:, sl], gn_ref[...], dya_sc[:, sl])
            doa_ref[:, sl] = doa
            dg_ref[:, sl] = dz.astype(BF16)
            dgn_ref[...] += dgn
            dob, drg, drn = _ret_out_bwd(ob_ref[:, sl], rg_ref[:, sl], rn_ref[:, sl], dyb_sc[:, sl])
            dob_ref[:, sl] = dob
            dg_ref[:, hv + h * HEAD_DIM:hv + (h + 1) * HEAD_DIM] = drg.astype(BF16)
            drn_ref[:, sl] += drn

    sp = _post_specs(tm, hv, d)
    act = pl.BlockSpec((tm, hv), lambda i: (i, 0))
    return pl.pallas_call(
        body, name=name, grid=(tp // tm,),
        in_specs=[sp["oa"], sp["ob"], sp["z"], sp["rg"], sp["ga"], sp["gb"], sp["gnorm"], sp["rnorm"],
                  sp["w"], sp["w"], sp["w"], sp["res"]],
        out_specs=[act, act, pl.BlockSpec((tm, 4 * hv), lambda i: (i, 0)), act, act, sp["res"], sp["res"],
                   sp["res"], sp["gnorm"], sp["rnorm"]],
        out_shape=[jax.ShapeDtypeStruct((tp, hv), F32), jax.ShapeDtypeStruct((tp, hv), F32),
                   jax.ShapeDtypeStruct((tp, 4 * hv), BF16), jax.ShapeDtypeStruct((tp, hv), BF16),
                   jax.ShapeDtypeStruct((tp, hv), BF16), jax.ShapeDtypeStruct((tp, d), BF16),
                   jax.ShapeDtypeStruct((tp, d), BF16), jax.ShapeDtypeStruct((tp, d), BF16),
                   jax.ShapeDtypeStruct((1, HEAD_DIM), F32), jax.ShapeDtypeStruct((1, hv), F32)],
        scratch_shapes=[pltpu.VMEM((tm, hv), F32), pltpu.VMEM((tm, hv), F32)],
        compiler_params=_params("arbitrary"))(oa, ob, proj, proj, proj, proj, gnorm, rnorm, wbg, wbr, wo, dh2)


def _peer(k):
    x, y, c = lax.axis_index("x"), lax.axis_index("y"), lax.axis_index("c")
    return (1 - x if k & 4 else x, 1 - y if k & 2 else y, 1 - c if k & 1 else c)


def _my_index():
    return 4 * lax.axis_index("x") + 2 * lax.axis_index("y") + lax.axis_index("c")


def _exchange(bufs, scatter, name):
    n = len(bufs)

    def body(*refs):
        _exchange_copies(refs[:n], refs[n:2 * n], refs[2 * n:], scatter, True, True)

    hbm, out_shape, sems = _exchange_refs(bufs)
    return pl.pallas_call(
        body, name=name, in_specs=hbm, out_specs=hbm, out_shape=out_shape, scratch_shapes=sems,
        compiler_params=pltpu.CompilerParams(has_side_effects=True))(*bufs)


def _gather_via_sibling(bufs, name):
    n = len(bufs)

    def body(*refs):
        _sibling_gather_copies(refs[:n], refs[n:2 * n], refs[2 * n:], True, True)

    hbm, out_shape, sems = _exchange_refs(bufs)
    return pl.pallas_call(
        body, name=name, in_specs=hbm, out_specs=hbm, out_shape=out_shape, scratch_shapes=sems,
        compiler_params=pltpu.CompilerParams(has_side_effects=True))(*bufs)


def _sibling_gather_copies(x_refs, out_refs, sems, start, finish):
    n = len(x_refs)
    send_sems, recv_sems, local_sems = sems
    x, y, c = lax.axis_index("x"), lax.axis_index("y"), lax.axis_index("c")
    me, sibling = (x, y, c), (x, y, 1 - c)
    chips = [(1 - x, y), (x, 1 - y), (1 - x, 1 - y)]
    rows = lambda a, dev: out_refs[a].at[4 * dev[0] + 2 * dev[1] + dev[2]]

    def copy(k, a, block, to, src=None):
        return pltpu.make_async_remote_copy(
            src_ref=rows(a, block) if src is None else src, dst_ref=rows(a, block),
            send_sem=send_sems.at[k * n + a], recv_sem=recv_sems.at[k * n + a],
            device_id=to, device_id_type=pl.DeviceIdType.MESH)

    mine = [pltpu.make_async_copy(x_refs[a], rows(a, me), local_sems.at[a]) for a in range(n)]
    first = [copy(0, a, me, sibling, src=x_refs[a]) for a in range(n)]
    first += [copy(1 + j, a, me, (*chip, c), src=x_refs[a]) for j, chip in enumerate(chips) for a in range(n)]
    if start:
        for cp in mine + first:
            cp.start()
    if finish:
        passed = []
        for j, chip in enumerate(chips):
            for a in range(n):
                copy(1 + j, a, (*chip, c), me).wait_recv()
                passed.append(copy(4 + j, a, (*chip, c), sibling))
                passed[-1].start()
        for a in range(n):
            copy(0, a, sibling, me).wait_recv()
        for j, chip in enumerate(chips):
            for a in range(n):
                copy(4 + j, a, (*chip, 1 - c), me).wait_recv()
        for cp in first + passed:
            cp.wait_send()
        for cp in mine:
            cp.wait()


def _exchange_refs(bufs):
    n = len(bufs)
    return ([pl.BlockSpec(memory_space=pl.ANY)] * n,
            [jax.ShapeDtypeStruct((N_DEV,) + b.shape[-2:], b.dtype) for b in bufs],
            [pltpu.SemaphoreType.DMA(((N_DEV - 1) * n,)), pltpu.SemaphoreType.DMA(((N_DEV - 1) * n,)),
             pltpu.SemaphoreType.DMA((n,))])


def _exchange_copies(x_refs, out_refs, sems, scatter, start, wait):
    n = len(x_refs)
    send_sems, recv_sems, local_sems = sems
    me = _my_index()
    copies = []
    for a in range(n):
        copies.append(pltpu.make_async_copy(x_refs[a].at[me] if scatter else x_refs[a], out_refs[a].at[me],
                                            local_sems.at[a]))
    sends = []
    arrivals = []
    for k in range(1, N_DEV):
        x, y, c = _peer(k)
        peer = 4 * x + 2 * y + c
        for a in range(n):
            sem = (k - 1) * n + a
            sends.append(pltpu.make_async_remote_copy(
                src_ref=x_refs[a].at[peer] if scatter else x_refs[a], dst_ref=out_refs[a].at[me],
                send_sem=send_sems.at[sem], recv_sem=recv_sems.at[sem],
                device_id=(x, y, c), device_id_type=pl.DeviceIdType.MESH))
            if wait:
                landed = out_refs[a].at[peer]
                arrivals.append(pltpu.make_async_remote_copy(
                    src_ref=landed, dst_ref=landed, send_sem=send_sems.at[sem], recv_sem=recv_sems.at[sem],
                    device_id=(x, y, c), device_id_type=pl.DeviceIdType.MESH))
    if start:
        for cp in copies + sends:
            cp.start()
    if wait:
        for cp in arrivals:
            cp.wait_recv()
        for cp in sends:
            cp.wait_send()
        for cp in copies:
            cp.wait()


def _carried_call(body, carry, first, last, *, name, grid, in_specs, out_specs, out_shape, scratch_shapes=()):
    in_specs, out_specs, out_shape = list(in_specs), list(out_specs), list(out_shape)
    semantics = ("arbitrary",) * len(grid)
    if carry is None:
        call = pl.pallas_call(body, name=name, grid=grid, in_specs=in_specs, out_specs=out_specs,
                              out_shape=out_shape, scratch_shapes=list(scratch_shapes),
                              compiler_params=_params(*semantics))
        return lambda *args: (call(*args), [])
    bufs, kind = carry
    n, n_in, n_out, n_scratch = len(bufs), len(in_specs), len(out_specs), len(scratch_shapes)
    hbm, x_shapes, sems = _exchange_refs(bufs)

    def copies(x_refs, xo_refs, x_sems, start, finish):
        if kind == "sibling_gather":
            _sibling_gather_copies(x_refs, xo_refs, x_sems, start, finish)
        else:
            _exchange_copies(x_refs, xo_refs, x_sems, kind == "scatter", start, finish)

    def full_body(*refs):
        ins, x_refs = refs[:n_in], refs[n_in:n_in + n]
        outs, xo_refs = refs[n_in + n:n_in + n + n_out], refs[n_in + n + n_out:n_in + 2 * n + n_out]
        scratch = refs[n_in + 2 * n + n_out:n_in + 2 * n + n_out + n_scratch]
        x_sems = refs[n_in + 2 * n + n_out + n_scratch:]

        @pl.when(first())
        def _():
            copies(x_refs, xo_refs, x_sems, True, False)

        body(*ins, *outs, *scratch)

        @pl.when(last())
        def _():
            copies(x_refs, xo_refs, x_sems, False, True)

    call = pl.pallas_call(full_body, name=name, grid=grid, in_specs=in_specs + hbm, out_specs=out_specs + hbm,
                          out_shape=out_shape + x_shapes, scratch_shapes=list(scratch_shapes) + sems,
                          compiler_params=_params(*semantics))

    def run(*args):
        res = call(*args, *bufs)
        return res[:n_out], res[n_out:]
    return run


def _adamw(w, g, m, v, name):
    r, c = w.shape
    parts = g.ndim == 3
    tr = _tile(r, 256, 16 if parts else 8)
    c1 = 1.0 - ADAM_B1 ** ADAM_STEP
    c2 = 1.0 - ADAM_B2 ** ADAM_STEP

    def body(w_ref, g_ref, m_ref, v_ref, go_ref, d_ref, mo_ref, vo_ref):
        if parts:
            g = g_ref[0].astype(F32)
            for q in range(1, N_DEV):
                g = g + g_ref[q].astype(F32)
        else:
            g = g_ref[...]
        m = ADAM_B1 * m_ref[...] + (1.0 - ADAM_B1) * g
        v = ADAM_B2 * v_ref[...] + (1.0 - ADAM_B2) * (g * g)
        go_ref[...] = g
        d_ref[...] = -ADAM_LR * ((m / c1) / (jnp.sqrt(v / c2) + ADAM_EPS) + ADAM_WD * w_ref[...])
        mo_ref[...] = m
        vo_ref[...] = v

    blk = pl.BlockSpec((tr, c), lambda i: (i, 0))
    g_spec = pl.BlockSpec((N_DEV, tr, c), lambda i: (0, i, 0)) if parts else blk
    return pl.pallas_call(
        body, name=name, grid=(r // tr,), in_specs=[blk, g_spec, blk, blk], out_specs=[blk] * 4,
        out_shape=[jax.ShapeDtypeStruct((r, c), F32)] * 4,
        compiler_params=_params("parallel"))(w, g, m, v)


def _win_segments(hv, nh):
    o_z, o_b = 3 * hv, 4 * hv
    o_r = o_b + 2 * nh
    return [(0, 0, 3 * hv), (3 * hv, o_r, 3 * hv), (6 * hv, o_z, hv), (7 * hv, o_r + 3 * hv, 3 * hv),
            (10 * hv, o_b, 2 * nh)]


def _win_from_shards(shards, hv, nh):
    _, d, cs = shards.shape
    pieces = []
    for _, src, width in _win_segments(hv, nh):
        lo = src
        while lo < src + width:
            p = lo // cs
            hi = min(src + width, (p + 1) * cs)
            pieces.append(shards[p][:, lo - p * cs:hi - p * cs])
            lo = hi
    pieces.append(jnp.zeros((d, LANES - 2 * nh), shards.dtype))
    return jnp.concatenate(pieces, axis=1)


def _win_grad_to_shards(parts, hv, nh, cs):
    segments = _win_segments(hv, nh)
    starts = [sum(p.shape[1] for p in parts[:i]) for i in range(len(parts))]

    def columns(a, b):
        out = []
        for part, start in zip(parts, starts):
            lo, hi = max(a, start), min(b, start + part.shape[1])
            if lo < hi:
                out.append(part[:, lo - start:hi - start])
        return out

    shards = []
    for p in range(N_DEV):
        pieces = []
        lo = p * cs
        while lo < (p + 1) * cs:
            here, src, width = next(s for s in segments if s[1] <= lo < s[1] + s[2])
            hi = min((p + 1) * cs, src + width)
            pieces += columns(here + lo - src, here + hi - src)
            lo = hi
        shards.append(jnp.concatenate(pieces, axis=1))
    return jnp.stack(shards)


def _rope_tables(tp):
    pos = jnp.arange(tp, dtype=F32) - float(PAD_FRONT)
    inv = 1.0 / (ROPE_BASE ** jnp.linspace(0.0, 1.0, HEAD_DIM // 2, dtype=F32))
    ang = pos[:, None] * inv[None, :]
    cos = jnp.repeat(jnp.cos(ang), 2, axis=1)
    sin = jnp.repeat(jnp.sin(ang), 2, axis=1) * jnp.tile(jnp.array([-1.0, 1.0], F32), HEAD_DIM // 2)[None, :]
    return cos, sin


def _retention_tables(nh):
    log_gamma = jnp.log1p(-jnp.exp2(-5.0 - jnp.arange(nh, dtype=F32)))
    pos = jnp.arange(CHUNK, dtype=F32)
    causal = pos[:, None] >= pos[None, :]
    diff = pos[:, None] - pos[None, :]
    dec = jnp.where(causal, jnp.exp(jnp.where(causal, diff, 0.0) * log_gamma[:, None, None]), 0.0)
    ones = jnp.ones((1, 1, HEAD_DIM), F32)
    xi = jnp.exp((pos + 1.0)[None, :] * log_gamma[:, None])[:, :, None] * ones
    zeta = jnp.exp((CHUNK - 1.0 - pos)[None, :] * log_gamma[:, None])[:, :, None] * ones
    cd = jnp.exp(CHUNK * log_gamma)[:, None, None] * jnp.ones((1, 8, HEAD_DIM), F32)
    return dec, xi, zeta, cd


SHARDED = ("meta_tokens", "ffn1_w_in", "ffn1_w_out", "w_in", "gdn_conv_w", "w_branch_gdn", "w_branch_ret",
           "w_out", "ffn2_w_in", "ffn2_w_out")
EXACT_F32 = ("meta_tokens", "gdn_conv_w")
REPLICATED = ("ffn1_norm", "mix_norm", "gdn_a_log", "gdn_dt_bias", "gdn_out_norm", "ret_out_norm", "ffn2_norm",
              "final_norm")
WEIGHTS = ("meta_tokens", "ffn1_norm", "ffn1_w_in", "ffn1_w_out", "mix_norm", "w_in", "gdn_conv_w", "gdn_a_log",
           "gdn_dt_bias", "gdn_out_norm", "ret_out_norm", "w_branch_gdn", "w_branch_ret", "w_out", "ffn2_norm",
           "ffn2_w_in", "ffn2_w_out", "final_norm")


def _as2d(a):
    if a.ndim == 3:
        return a[0]
    if a.ndim == 1:
        return a[None, :]
    return a


def _rows_of(shards):
    return shards.reshape(-1, shards.shape[2])


def _cols_of(shards):
    return shards.transpose(1, 0, 2).reshape(shards.shape[1], -1)


def _row_shards(a):
    return a.reshape(N_DEV, -1, a.shape[1])


def _col_shards(a):
    return a.reshape(a.shape[0], N_DEV, -1).transpose(1, 0, 2)


GATHER_FIRST = ("meta_tokens", "ffn1_w_in", "ffn1_w_out")
GATHER_BEHIND_FFN1 = ("w_in", "gdn_conv_w")
GATHER_BEHIND_PROJ = ("w_branch_gdn", "w_branch_ret", "w_out", "ffn2_w_in", "ffn2_w_out")
SCATTER_BEHIND_DN2 = ("ffn2_w_in", "ffn2_w_out", "w_branch_gdn", "w_branch_ret", "w_out")
SCATTER_BEHIND_FFN1 = ("w_in", "gdn_conv_w")
SCATTER_BEHIND_DWG = ("meta_tokens", "ffn1_w_out")
SCATTER_LAST = ("ffn1_w_in",)


def _device_step(x, target, send, rep):
    seq, d = x.shape
    tp = HEAD_ROWS + seq
    hv = d
    nh = hv // HEAD_DIM
    assert tp % (SCAN_CHUNKS * CHUNK) == 0 and tp % HEAD_ROWS == 0
    bf16_shards = lambda grads, names: [grads[n].astype(BF16) for n in names]

    pad_lanes = lambda row: jnp.pad(row, ((0, 0), (nh, LANES - 2 * nh)))
    alog = pad_lanes(rep["gdn_a_log"])
    dtb = pad_lanes(rep["gdn_dt_bias"])
    cos, sin = _rope_tables(tp)
    dec, xi, zeta, cd = _retention_tables(nh)

    got = dict(zip(GATHER_FIRST, _gather_via_sibling([send[n] for n in GATHER_FIRST], "gather_ffn1")))
    h0 = jnp.concatenate([jnp.zeros((PAD_FRONT, d), F32), _cols_of(got["meta_tokens"]), x], axis=0)
    f1i, f1o = got["ffn1_w_in"], _rows_of(got["ffn1_w_out"])
    (h1, hid1, dup1, dgate1), moved = _ffn_fwd(h0, rep["ffn1_norm"], f1i, f1o, "ffn1_fwd",
                                               ([send[n] for n in GATHER_BEHIND_FFN1], "sibling_gather"))
    got.update(zip(GATHER_BEHIND_FFN1, moved))
    wp = _win_from_shards(got["w_in"], hv, nh)
    conv_w = _cols_of(got["gdn_conv_w"])
    (proj, n2), moved = _proj_fwd(h1, rep["mix_norm"], wp, "proj_fwd",
                                  ([send[n] for n in GATHER_BEHIND_PROJ], "sibling_gather"))
    got.update(zip(GATHER_BEHIND_PROJ, moved))
    wbg, wbr, wo = _rows_of(got["w_branch_gdn"]), _rows_of(got["w_branch_ret"]), _rows_of(got["w_out"])
    f2i, f2o = got["ffn2_w_in"], _rows_of(got["ffn2_w_out"])
    qkv, conv_out = _conv_fwd(proj, conv_w, hv, "conv_fwd")
    (oa, s_gdn, t_gdn), (ob, s_ret) = _run_together(
        [_gdn_fwd(qkv, proj, alog, dtb, nh), _ret_fwd(proj, cos, sin, dec, xi, zeta, cd, nh)], "scans_fwd")
    h2 = _post_fwd(oa, ob, proj, rep["gdn_out_norm"], rep["ret_out_norm"], wbg, wbr, wo, h1, "post_fwd")
    (dh3, hid2, dup2, dgate2, loss_row, d_final), _ = _ffn_fwd(h2, rep["ffn2_norm"], f2i, f2o, "ffn2_fwd",
                                                                head=(rep["final_norm"], target))

    (dh2, d_f2n, n3, dag2, dau2), _ = _ffn_bwd(h2, dh3, rep["ffn2_norm"], f2i, f2o, dup2, dgate2, "ffn2_bwd")
    grads = {"ffn2_w_in": jnp.concatenate([_matmul_tn_blocks(n3, dag2, "ffn2_dwg"),
                                           _matmul_tn_blocks(n3, dau2, "ffn2_dwu")]),
             "ffn2_w_out": _row_shards(_matmul_tn_blocks(hid2, dh3, "ffn2_dwo", 0.5))}

    doa, dob, dgate, ya, yb, merged, dpa, dpb, d_gn, d_rn = _post_bwd(
        oa, ob, proj, rep["gdn_out_norm"], rep["ret_out_norm"], wbg, wbr, wo, dh2, "post_bwd")
    grads["w_branch_gdn"] = _row_shards(_matmul_tn(ya, dpa, "dw_branch_gdn"))
    grads["w_branch_ret"] = _row_shards(_matmul_tn(yb, dpb, "dw_branch_ret"))
    grads["w_out"] = _row_shards(_matmul_tn(merged, dh2, "dw_out"))

    gdn_grads, d_ret = _run_together(
        [_gdn_bwd(qkv, proj, alog, dtb, s_gdn, t_gdn, doa, nh),
         _ret_bwd(proj, cos, sin, dec, xi, zeta, cd, s_ret, dob, nh)], "scans_bwd")
    dba, d_alog, d_dtb = gdn_grads[3:]
    dpre, g_conv = [], []
    for grp, tag in enumerate("qkv"):
        dx, dw = _conv_bwd(proj, conv_out, conv_w, gdn_grads[grp], grp, hv, "conv_bwd_" + tag)
        dpre.append(dx)
        g_conv.append(dw)
    grads["gdn_conv_w"] = _col_shards(jnp.concatenate(g_conv, axis=1))

    wide = dpre + list(d_ret) + [dgate]
    dn2, moved = _matmul_nt_parts(wide, wp[:, :10 * hv], "dn2_wide",
                                  (bf16_shards(grads, SCATTER_BEHIND_DN2), "scatter"))
    parts = dict(zip(SCATTER_BEHIND_DN2, moved))
    g_wp = [_matmul_tn(n2, dg, "dw_in_%d" % idx) for idx, dg in enumerate(wide + [dba])]
    grads["w_in"] = _win_grad_to_shards(g_wp, hv, nh, send["w_in"].shape[1])
    dh1, d_mixn = _norm_bwd(h1, rep["mix_norm"], dn2, dba, wp[:, 10 * hv:], dh2, "mix_norm_bwd")

    (dh0, d_f1n, n1, dag1, dau1), moved = _ffn_bwd(h0, dh1, rep["ffn1_norm"], f1i, f1o, dup1, dgate1, "ffn1_bwd",
                                                   (bf16_shards(grads, SCATTER_BEHIND_FFN1), "scatter"))
    parts.update(zip(SCATTER_BEHIND_FFN1, moved))
    grads["ffn1_w_out"] = _row_shards(_matmul_tn_blocks(hid1, dh1, "ffn1_dwo", 0.5))
    grads["meta_tokens"] = _col_shards(dh0[PAD_FRONT:HEAD_ROWS])
    g_gate, moved = _matmul_tn_blocks(n1, dag1, "ffn1_dwg", carry=(bf16_shards(grads, SCATTER_BEHIND_DWG), "scatter"))
    parts.update(zip(SCATTER_BEHIND_DWG, moved))
    grads["ffn1_w_in"] = jnp.concatenate([g_gate, _matmul_tn_blocks(n1, dau1, "ffn1_dwu")])
    parts.update(zip(SCATTER_LAST, _exchange(bf16_shards(grads, SCATTER_LAST), True, "scatter_ffn1")))

    small = {"ffn1_norm": d_f1n, "mix_norm": d_mixn, "gdn_a_log": d_alog[:, nh:2 * nh],
             "gdn_dt_bias": d_dtb[:, nh:2 * nh], "gdn_out_norm": d_gn, "ret_out_norm": d_rn, "ffn2_norm": d_f2n,
             "final_norm": d_final}
    return loss_row[0, 0], dh0[HEAD_ROWS:], parts, small


def kernel(x, meta_tokens, ffn1_norm, ffn1_w_in, ffn1_w_out, mix_norm, w_in, gdn_conv_w, gdn_a_log, gdn_dt_bias, gdn_out_norm, ret_out_norm, w_branch_gdn, w_branch_ret, w_out, ffn2_norm, ffn2_w_in, ffn2_w_out, final_norm, loss_target, m_meta_tokens, m_ffn1_norm, m_ffn1_w_in, m_ffn1_w_out, m_mix_norm, m_w_in, m_gdn_conv_w, m_gdn_a_log, m_gdn_dt_bias, m_gdn_out_norm, m_ret_out_norm, m_w_branch_gdn, m_w_branch_ret, m_w_out, m_ffn2_norm, m_ffn2_w_in, m_ffn2_w_out, m_final_norm, v_meta_tokens, v_ffn1_norm, v_ffn1_w_in, v_ffn1_w_out, v_mix_norm, v_w_in, v_gdn_conv_w, v_gdn_a_log, v_gdn_dt_bias, v_gdn_out_norm, v_ret_out_norm, v_w_branch_gdn, v_w_branch_ret, v_w_out, v_ffn2_norm, v_ffn2_w_in, v_ffn2_w_out, v_final_norm):
    given = dict(locals())
    params = {n: _as2d(given[n]) for n in WEIGHTS}
    local = {n: params[n] for n in SHARDED}
    rep = {n: params[n] for n in REPLICATED}

    send = {n: local[n] if n in EXACT_F32 else local[n].astype(BF16) for n in SHARDED}
    loss_sum, grad_x, parts, small = _device_step(x[0], loss_target[0], send, rep)
    parts.update(zip(REPLICATED, _exchange([small[n] for n in REPLICATED], False, "gather_small_grads")))
    loss = lax.psum(loss_sum, ("x", "y", "c"))

    outs = {}
    for n in WEIGHTS:
        res = _adamw(params[n], parts[n], _as2d(given["m_" + n]), _as2d(given["v_" + n]), "adamw_" + n)
        outs[n] = [r.reshape(given[n].shape) for r in res]
    return (loss, grad_x[None], *[outs[n][0] for n in WEIGHTS], *[outs[n][1] for n in WEIGHTS],
            *[outs[n][2] for n in WEIGHTS], *[outs[n][3] for n in WEIGHTS])
```
